```python
import jax, jax.numpy as jnp
from jax import lax
import numpy as np

D_MODEL = 1024
BATCH = 4
SEQ = 4096
DEPTH = 2
DEC_BATCH = 32
DEC_SEQ = 8
PAST_LEN = 8192
PAGE_SIZE = 128

D_FF = 2816
PLE_DIM = 256
RMS_EPS = 1e-6
RWKV_HEAD_DIM = 64
RWKV_HEADS = D_MODEL // RWKV_HEAD_DIM
DECAY_LORA = 64
ICLR_LORA = 64
GATE_LORA = 160
LNX_EPS = 64e-5
ATTN_HEAD_DIM = 64
ATTN_HEADS = D_MODEL // ATTN_HEAD_DIM
KV_HEADS = 4
Q_PER_KV = ATTN_HEADS // KV_HEADS
DILATION_GROUPS = ((128, 1), (512, 4), (2048, 16))
N_GROUPS = len(DILATION_GROUPS)
MAX_WINDOW = max(w for w, _ in DILATION_GROUPS)
Q_BLOCK = 128
REL_BUCKETS = 32
REL_MAX_DIST = 2048
NEG_INF = -1e30

kernel_name = 'yoco_rwkv7_dilated_window_step'


def rms_norm(x, g):
    xf = x.astype(jnp.float32)
    y = xf * lax.rsqrt(jnp.mean(xf * xf, axis=-1, keepdims=True) + RMS_EPS)
    return (y * g.astype(jnp.float32)).astype(x.dtype)


def swiglu(x, wi, wo):
    gate, up = jnp.split(x @ wi, 2, axis=-1)
    return (jax.nn.silu(gate) * up) @ wo


def t5_buckets(dist):
    d = np.asarray(dist, dtype=np.int64)
    max_exact = REL_BUCKETS // 2
    large = max_exact + (np.log(np.maximum(d, 1) / max_exact) / np.log(REL_MAX_DIST / max_exact)
                         * (REL_BUCKETS - max_exact)).astype(np.int32)
    large = np.minimum(large, REL_BUCKETS - 1)
    return np.where(d < max_exact, d, large).astype(np.int32)


def rwkv7_time_mix(xn, shift_prev, s0, mix, w_rkv, w_o, w0, w1, w2, a0, a1, a2, g1, g2,
                   k_k, k_a, r_k, lnx_w, lnx_b):
    f32 = jnp.float32
    B, T, D = xn.shape
    H, N = RWKV_HEADS, RWKV_HEAD_DIM
    x_prev = jnp.concatenate([shift_prev[:, None].astype(xn.dtype), xn[:, :-1]], axis=1)
    xx = x_prev - xn
    xr, xw, xk, xv, xa, xg = (xn + xx * mix[j] for j in range(6))
    r = (xr @ w_rkv[0]).astype(f32).reshape(B, T, H, N)
    k = (xk @ w_rkv[1]).astype(f32)
    v = (xv @ w_rkv[2]).astype(f32).reshape(B, T, H, N)
    w_log = -jax.nn.softplus(-(w0 + jnp.tanh(xw @ w1) @ w2).astype(f32)) - 0.5
    decay = jnp.exp(-jnp.exp(w_log)).reshape(B, T, H, N)
    a = jax.nn.sigmoid((a0 + (xa @ a1) @ a2).astype(f32))
    g = jax.nn.sigmoid(xg @ g1) @ g2
    kk = (k * k_k.astype(f32)).reshape(B, T, H, N)
    kk = kk / jnp.maximum(jnp.sqrt(jnp.sum(kk * kk, -1, keepdims=True)), 1e-12)
    k = (k * (1.0 + (a - 1.0) * k_a.astype(f32))).reshape(B, T, H, N)
    b = kk * a.reshape(B, T, H, N)

    def step(S, inp):
        r_t, w_t, k_t, v_t, kk_t, b_t = inp
        sa = jnp.einsum('bhij,bhj->bhi', S, -kk_t)
        S = S * w_t[:, :, None, :] + sa[..., None] * b_t[:, :, None, :] + v_t[..., None] * k_t[:, :, None, :]
        return S, jnp.einsum('bhij,bhj->bhi', S, r_t)

    xs = tuple(jnp.moveaxis(t, 1, 0) for t in (r, decay, k, v, kk, b))
    s_T, ys = lax.scan(step, s0.astype(f32), xs)
    y = jnp.moveaxis(ys, 0, 1)
    mu = jnp.mean(y, -1, keepdims=True)
    var = jnp.mean(jnp.square(y - mu), -1, keepdims=True)
    y = ((y - mu) * lax.rsqrt(var + LNX_EPS)).reshape(B, T, D) * lnx_w.astype(f32) + lnx_b.astype(f32)
    y = y + (jnp.sum(r * k * r_k.astype(f32), -1, keepdims=True) * v).reshape(B, T, D)
    return (y.astype(xn.dtype) * g) @ w_o, s_T, xn[:, -1]


def dilated_window_attention(q, k_ext, v_ext, valid_ext, rel_bias):
    f32 = jnp.float32
    B, T = q.shape[:2]
    qb = Q_BLOCK if T % Q_BLOCK == 0 else T
    n_blk = T // qb
    idxs, biases = [], []
    for g, (win, dil) in enumerate(DILATION_GROUPS):
        offs = dil * np.arange(win // dil + 1)
        idxs.append(MAX_WINDOW + np.arange(qb)[:, None] - offs[None, :])
        bias = rel_bias[t5_buckets(offs)][:, g * ATTN_HEADS:(g + 1) * ATTN_HEADS].astype(f32)
        biases.append(bias.T.reshape(KV_HEADS, Q_PER_KV, -1))

    def block(bi):
        start = bi * qb
        q_blk = lax.dynamic_slice_in_dim(q, start, qb, axis=1)
        k_blk = lax.dynamic_slice_in_dim(k_ext, start, MAX_WINDOW + qb, axis=1)
        v_blk = lax.dynamic_slice_in_dim(v_ext, start, MAX_WINDOW + qb, axis=1)
        m_blk = lax.dynamic_slice_in_dim(valid_ext, start, MAX_WINDOW + qb, axis=0)
        outs, lses = [], []
        for g in range(N_GROUPS):
            idx = idxs[g]
            kg, vg = k_blk[:, idx], v_blk[:, idx]
            logits = jnp.einsum('bqhrd,bqjhd->bqhrj', q_blk[:, :, g], kg, preferred_element_type=f32) + biases[g]
            logits = jnp.where(m_blk[idx][None, :, None, None, :], logits, NEG_INF)
            mx = jnp.max(logits, -1, keepdims=True)
            pr = jnp.exp(logits - mx)
            den = jnp.sum(pr, -1, keepdims=True)
            outs.append(jnp.einsum('bqhrj,bqjhd->bqhrd', pr, vg, preferred_element_type=f32) / den)
            lses.append(mx + jnp.log(den))
        wts = jax.nn.softmax(jnp.stack(lses, 0), axis=0)
        return sum(wts[g] * outs[g] for g in range(N_GROUPS))

    out = lax.map(block, jnp.arange(n_blk))
    return jnp.moveaxis(out, 0, 1).reshape(B, T, ATTN_HEADS * ATTN_HEAD_DIM)


def trunk(x, p, wkv0, shift0, kv_past, w):
    B, T, _ = x.shape
    n_a = DEPTH // 2
    h = x
    wkv_out, shift_out = [], []
    kv_rows = kv_ext = valid_ext = None
    for i in range(DEPTH):
        if i == n_a:
            kv_rows = (rms_norm(h, w['kv_norm']) @ w['w_kv']).reshape(B, T, 2, KV_HEADS, ATTN_HEAD_DIM)
            pad = MAX_WINDOW - kv_past.shape[1]
            kv_ext = jnp.concatenate([jnp.zeros((B, pad, 2, KV_HEADS, ATTN_HEAD_DIM), h.dtype),
                                      kv_past.astype(h.dtype), kv_rows], axis=1)
            valid_ext = jnp.asarray(np.arange(MAX_WINDOW + T) >= pad)
        h = h + 0.5 * swiglu(rms_norm(h, w['norm_w'][i, 0]), w['ffn1_wi'][i], w['ffn1_wo'][i])
        hn = rms_norm(h, w['norm_w'][i, 1])
        if i < n_a:
            mix, s_new, sh_new = rwkv7_time_mix(
                hn, shift0[i], wkv0[i], w['rwkv_mix'][i], w['rwkv_wrkv'][i], w['rwkv_wo'][i],
                w['rwkv_w0'][i], w['rwkv_w1'][i], w['rwkv_w2'][i], w['rwkv_a0'][i], w['rwkv_a1'][i],
                w['rwkv_a2'][i], w['rwkv_g1'][i], w['rwkv_g2'][i], w['rwkv_kk'][i], w['rwkv_ka'][i],
                w['rwkv_rk'][i], w['rwkv_lnx_w'][i], w['rwkv_lnx_b'][i])
            wkv_out.append(s_new.astype(x.dtype))
            shift_out.append(sh_new)
        else:
            j = i - n_a
            q = (hn @ w['attn_wq'][j]).reshape(B, T, N_GROUPS, KV_HEADS, Q_PER_KV, ATTN_HEAD_DIM) * (ATTN_HEAD_DIM ** -0.5)
            att = dilated_window_attention(q, kv_ext[:, :, 0], kv_ext[:, :, 1], valid_ext, w['rel_bias'])
            mix = att.astype(h.dtype) @ w['attn_wo'][j]
        h = h + mix
        h = h + 0.5 * swiglu(rms_norm(h, w['norm_w'][i, 2]), w['ffn2_wi'][i], w['ffn2_wo'][i])
        h = h + jax.nn.sigmoid(rms_norm(h, w['norm_w'][i, 3]) @ w['pe_gate'][i]) * (p[i] @ w['pe_proj'][i])
    return rms_norm(h, w['final_norm']), jnp.stack(wkv_out), jnp.stack(shift_out), kv_rows


def setup_inputs(seed: int = 0) -> dict:
    key = jax.random.key(seed)
    ks = iter(jax.random.split(key, 40))
    def nrm(shape, scale=1.0):
        return scale * jax.random.normal(next(ks), shape, jnp.float32)
    def unif(shape, lo, hi):
        return jax.random.uniform(next(ks), shape, jnp.float32, minval=lo, maxval=hi)
    n_a = DEPTH // 2
    n_b = DEPTH - n_a
    H, N = RWKV_HEADS, RWKV_HEAD_DIM
    kv_len = min(MAX_WINDOW, PAST_LEN)
    D = D_MODEL
    return {
        'x_prompt': nrm((BATCH, SEQ, D)),
        'x_sample': nrm((DEC_BATCH, DEC_SEQ, D)),
        'state_wkv': nrm((n_a, DEC_BATCH, H, N, N), 0.3),
        'state_shift': nrm((n_a, DEC_BATCH, D)),
        'cache_kv': nrm((DEC_BATCH, kv_len, 2, KV_HEADS, ATTN_HEAD_DIM)),
        'p_prompt': nrm((DEPTH, BATCH, SEQ, PLE_DIM)),
        'p_sample': nrm((DEPTH, DEC_BATCH, DEC_SEQ, PLE_DIM)),
        'norm_w': 1.0 + nrm((DEPTH, 4, D), 0.1),
        'ffn1_wi': nrm((DEPTH, D, 2 * D_FF), D ** -0.5),
        'ffn1_wo': nrm((DEPTH, D_FF, D), D_FF ** -0.5),
        'ffn2_wi': nrm((DEPTH, D, 2 * D_FF), D ** -0.5),
        'ffn2_wo': nrm((DEPTH, D_FF, D), D_FF ** -0.5),
        'pe_proj': nrm((DEPTH, PLE_DIM, D), PLE_DIM ** -0.5),
        'pe_gate': nrm((DEPTH, D, D), D ** -0.5),
        'rwkv_mix': unif((n_a, 6, D), 0.0, 1.0),
        'rwkv_wrkv': nrm((n_a, 3, D, D), D ** -0.5),
        'rwkv_wo': nrm((n_a, D, D), D ** -0.5),
        'rwkv_w0': unif((n_a, D), -4.0, 0.0),
        'rwkv_w1': nrm((n_a, D, DECAY_LORA), D ** -0.5),
        'rwkv_w2': nrm((n_a, DECAY_LORA, D), 0.1 * DECAY_LORA ** -0.5),
        'rwkv_a0': nrm((n_a, D), 0.1),
        'rwkv_a1': nrm((n_a, D, ICLR_LORA), D ** -0.5),
        'rwkv_a2': nrm((n_a, ICLR_LORA, D), 0.3 * ICLR_LORA ** -0.5),
        'rwkv_g1': nrm((n_a, D, GATE_LORA), D ** -0.5),
        'rwkv_g2': nrm((n_a, GATE_LORA, D), GATE_LORA ** -0.5),
        'rwkv_kk': 0.85 + nrm((n_a, D), 0.05),
        'rwkv_ka': 1.0 + nrm((n_a, D), 0.05),
        'rwkv_rk': nrm((n_a, H, N), 0.1),
        'rwkv_lnx_w': 1.0 + nrm((n_a, D), 0.1),
        'rwkv_lnx_b': nrm((n_a, D), 0.01),
        'attn_wq': nrm((n_b, D, N_GROUPS * ATTN_HEADS * ATTN_HEAD_DIM), D ** -0.5),
        'attn_wo': nrm((n_b, ATTN_HEADS * ATTN_HEAD_DIM, D), (ATTN_HEADS * ATTN_HEAD_DIM) ** -0.5),
        'kv_norm': 1.0 + nrm((D,), 0.1),
        'w_kv': nrm((D, 2 * KV_HEADS * ATTN_HEAD_DIM), D ** -0.5),
        'rel_bias': nrm((REL_BUCKETS, N_GROUPS * ATTN_HEADS), 0.5),
        'final_norm': 1.0 + nrm((D,), 0.1),
    }


def reference(x_prompt, x_sample, state_wkv, state_shift, cache_kv, p_prompt, p_sample,
              norm_w, ffn1_wi, ffn1_wo, ffn2_wi, ffn2_wo, pe_proj, pe_gate,
              rwkv_mix, rwkv_wrkv, rwkv_wo, rwkv_w0, rwkv_w1, rwkv_w2, rwkv_a0, rwkv_a1, rwkv_a2,
              rwkv_g1, rwkv_g2, rwkv_kk, rwkv_ka, rwkv_rk, rwkv_lnx_w, rwkv_lnx_b,
              attn_wq, attn_wo, kv_norm, w_kv, rel_bias, final_norm):
    w = dict(norm_w=norm_w, ffn1_wi=ffn1_wi, ffn1_wo=ffn1_wo, ffn2_wi=ffn2_wi, ffn2_wo=ffn2_wo,
             pe_proj=pe_proj, pe_gate=pe_gate, rwkv_mix=rwkv_mix, rwkv_wrkv=rwkv_wrkv, rwkv_wo=rwkv_wo,
             rwkv_w0=rwkv_w0, rwkv_w1=rwkv_w1, rwkv_w2=rwkv_w2, rwkv_a0=rwkv_a0, rwkv_a1=rwkv_a1,
             rwkv_a2=rwkv_a2, rwkv_g1=rwkv_g1, rwkv_g2=rwkv_g2, rwkv_kk=rwkv_kk, rwkv_ka=rwkv_ka,
             rwkv_rk=rwkv_rk, rwkv_lnx_w=rwkv_lnx_w, rwkv_lnx_b=rwkv_lnx_b, attn_wq=attn_wq,
             attn_wo=attn_wo, kv_norm=kv_norm, w_kv=w_kv, rel_bias=rel_bias, final_norm=final_norm)
    n_a = DEPTH // 2
    B, T, _ = x_prompt.shape
    wkv0 = jnp.zeros((n_a, B, RWKV_HEADS, RWKV_HEAD_DIM, RWKV_HEAD_DIM), jnp.float32)
    shift0 = jnp.zeros((n_a, B, D_MODEL), x_prompt.dtype)
    kv0 = jnp.zeros((B, 0, 2, KV_HEADS, ATTN_HEAD_DIM), x_prompt.dtype)
    y_prompt, wkv_prompt, shift_prompt, kv_rows_p = trunk(x_prompt, p_prompt, wkv0, shift0, kv0, w)
    kv_prompt = kv_rows_p[:, T - min(MAX_WINDOW, T):]
    y_sample, wkv_sample, shift_sample, kv_sample = trunk(x_sample, p_sample, state_wkv, state_shift, cache_kv, w)
    return (y_prompt, y_sample, wkv_prompt, shift_prompt, kv_prompt, wkv_sample, shift_sample, kv_sample)
```

```python
import functools

import numpy as np
import jax
import jax.numpy as jnp
from jax import lax
from jax.experimental import pallas as pl
from jax.experimental.pallas import tpu as pltpu

F32 = jnp.float32
BF16 = jnp.bfloat16

D_MODEL = 1024
D_FF = 2816
PLE_DIM = 256
RMS_EPS = 1e-6
HEAD_DIM = 64
N_HEADS = D_MODEL // HEAD_DIM
LNX_EPS = 64e-5
KV_HEADS = 4
Q_PER_KV = N_HEADS // KV_HEADS
DILATION_GROUPS = ((128, 1), (512, 4), (2048, 16))
N_GROUPS = len(DILATION_GROUPS)
MAX_WINDOW = 2048
REL_BUCKETS = 32
REL_MAX_DIST = 2048
NEG_INF = -1e30

LANES = 128
FFN_CHUNK = 256
N_FFN_CHUNKS = D_FF // FFN_CHUNK
TOKEN_TILE = 256
WKV_CHUNK = 64
Q_BLOCK = 128
VMEM_LIMIT = 56 * 1024 * 1024
EXP_MINUS_HALF = 0.6065306597126334


def _params(*sem):
    return pltpu.CompilerParams(dimension_semantics=sem, vmem_limit_bytes=VMEM_LIMIT)


def _const_spec(shape):
    return pl.BlockSpec(shape, lambda *_: (0,) * len(shape))


def _tile(n, pref=TOKEN_TILE):
    t = min(n, pref)
    while n % t:
        t -= 8
    return t


def _rms(x, g):
    return x * lax.rsqrt(jnp.mean(x * x, axis=-1, keepdims=True) + RMS_EPS) * g


def _bdot(a, b):
    return jnp.dot(a.astype(BF16), b, preferred_element_type=F32)


def _dot_split(x, w):
    hi = x.astype(BF16)
    lo = (x - hi.astype(F32)).astype(BF16)
    return (jnp.dot(hi, w, preferred_element_type=F32) + jnp.dot(lo, w, preferred_element_type=F32))


def _head_sum(x, e_ref, et_ref):
    return _dot_split(_dot_split(x, e_ref[...]), et_ref[...])


def _ffn_kernel(x_ref, g_ref, wg_ref, wu_ref, wo_ref, o_ref):
    x = x_ref[...]
    xn = _rms(x, g_ref[...]).astype(BF16)
    acc = jnp.zeros_like(x)
    for j in range(N_FFN_CHUNKS):
        gate = jnp.dot(xn, wg_ref[j], preferred_element_type=F32)
        up = jnp.dot(xn, wu_ref[j], preferred_element_type=F32)
        act = (gate * jax.nn.sigmoid(gate) * up).astype(BF16)
        acc = acc + jnp.dot(act, wo_ref[j], preferred_element_type=F32)
    o_ref[...] = x + 0.5 * acc


def _ffn(h, g, wg, wu, wo):
    n = h.shape[0]
    tm = _tile(n)
    return pl.pallas_call(
        _ffn_kernel,
        grid=(n // tm,),
        in_specs=[pl.BlockSpec((tm, D_MODEL), lambda i: (i, 0)), _const_spec((1, D_MODEL)),
                  _const_spec(wg.shape), _const_spec(wu.shape), _const_spec(wo.shape)],
        out_specs=pl.BlockSpec((tm, D_MODEL), lambda i: (i, 0)),
        out_shape=jax.ShapeDtypeStruct((n, D_MODEL), F32),
        compiler_params=_params("arbitrary"),
        name="ffn",
    )(h, g, wg, wu, wo)


def _norm_mm_kernel(x_ref, g_ref, w_ref, o_ref):
    o_ref[...] = _bdot(_rms(x_ref[...], g_ref[...]), w_ref[...]).astype(o_ref.dtype)


def _norm_mm(h, g, w, out_dtype, col_tile=None, name="norm_mm"):
    n = h.shape[0]
    tm = _tile(n)
    nout = w.shape[1]
    tn = nout if col_tile is None else col_tile
    return pl.pallas_call(
        _norm_mm_kernel,
        grid=(n // tm, nout // tn),
        in_specs=[pl.BlockSpec((tm, D_MODEL), lambda i, j: (i, 0)), _const_spec((1, D_MODEL)),
                  pl.BlockSpec((D_MODEL, tn), lambda i, j: (0, j))],
        out_specs=pl.BlockSpec((tm, tn), lambda i, j: (i, j)),
        out_shape=jax.ShapeDtypeStruct((n, nout), out_dtype),
        compiler_params=_params("arbitrary", "arbitrary"),
        name=name,
    )(h, g, w)


def _pe_kernel(x_ref, p_ref, g_ref, wgate_ref, wproj_ref, gf_ref, o_ref, *, final):
    x = x_ref[...]
    gate = jax.nn.sigmoid(_bdot(_rms(x, g_ref[...]), wgate_ref[...]))
    y = x + gate * _bdot(p_ref[...], wproj_ref[...])
    o_ref[...] = _rms(y, gf_ref[...]) if final else y


def _pe(h, p, g, wgate, wproj, gfinal, final):
    n = h.shape[0]
    tm = _tile(n)
    return pl.pallas_call(
        functools.partial(_pe_kernel, final=final),
        grid=(n // tm,),
        in_specs=[pl.BlockSpec((tm, D_MODEL), lambda i: (i, 0)), pl.BlockSpec((tm, PLE_DIM), lambda i: (i, 0)),
                  _const_spec((1, D_MODEL)), _const_spec(wgate.shape), _const_spec(wproj.shape),
                  _const_spec((1, D_MODEL))],
        out_specs=pl.BlockSpec((tm, D_MODEL), lambda i: (i, 0)),
        out_shape=jax.ShapeDtypeStruct((n, D_MODEL), F32),
        compiler_params=_params("arbitrary"),
        name="pe_final" if final else "pe",
    )(h, p, g, wgate, wproj, gfinal)


def _rwkv_proj_kernel(h_ref, aux_ref, sh_ref, nw_ref, mix_ref, wrkv_ref, w0_ref, w1_ref, w2_ref,
                      a0_ref, a1_ref, a2_ref, g1_ref, g2_ref, kk_ref, ka_ref, e_ref, et_ref,
                      r_o, lw_o, k_o, v_o, kk_o, b_o, g_o, hn_o, *, within_seq, seq_len, tiles_per_seq):
    i = pl.program_id(0)
    nw = nw_ref[...]
    hn = _rms(h_ref[...], nw)
    tm = hn.shape[0]
    row = lax.broadcasted_iota(jnp.int32, hn.shape, 0)
    rolled = pltpu.roll(hn, 1, 0)
    if within_seq:
        prev_last = _rms(aux_ref[...], nw)[7:8]
        tile_in_seq = lax.rem(jnp.full((1, D_MODEL), i, jnp.int32), tiles_per_seq)
        first = jnp.where(tile_in_seq == 0, sh_ref[...], prev_last)
        x_prev = jnp.where(row == 0, first, rolled)
        hn_o[...] = hn[tm - 1:tm]
    else:
        x_prev = jnp.where(lax.rem(row, seq_len) == 0, aux_ref[...], rolled)
        hn_o[...] = hn
    xx = x_prev - hn
    mix = mix_ref[...]
    xr, xw, xk, xv, xa, xg = (hn + xx * mix[j:j + 1] for j in range(6))
    r = _bdot(xr, wrkv_ref[0])
    k = _bdot(xk, wrkv_ref[1])
    v = _bdot(xv, wrkv_ref[2])
    wl = w0_ref[...] + _bdot(jnp.tanh(_bdot(xw, w1_ref[...])), w2_ref[...])
    lw = -EXP_MINUS_HALF * jax.nn.sigmoid(wl)
    a = jax.nn.sigmoid(a0_ref[...] + _bdot(_bdot(xa, a1_ref[...]), a2_ref[...]))
    g = _bdot(jax.nn.sigmoid(_bdot(xg, g1_ref[...])), g2_ref[...])
    kkv = k * kk_ref[...]
    kk = kkv / jnp.maximum(jnp.sqrt(_head_sum(kkv * kkv, e_ref, et_ref)), 1e-12)
    r_o[...] = r
    lw_o[...] = lw
    k_o[...] = k * (1.0 + (a - 1.0) * ka_ref[...])
    v_o[...] = v
    kk_o[...] = kk
    b_o[...] = kk * a
    g_o[...] = g


def _rwkv_proj(h, shift, seq_len, lw):
    n = h.shape[0]
    nb = n // seq_len
    tm = _tile(n)
    within_seq = seq_len % tm == 0
    row_spec = pl.BlockSpec((tm, D_MODEL), lambda i: (i, 0))
    if within_seq:
        tiles_per_seq = seq_len // tm
        aux = h
        aux_spec = pl.BlockSpec((8, D_MODEL), lambda i: (jnp.maximum(i * (tm // 8) - 1, 0), 0))
        sh = shift.reshape(nb, 1, D_MODEL)
        sh_spec = pl.BlockSpec((None, 1, D_MODEL), lambda i: (i // tiles_per_seq, 0, 0))
        hn_shape = jax.ShapeDtypeStruct((nb, 1, D_MODEL), F32)
        hn_spec = pl.BlockSpec((None, 1, D_MODEL), lambda i: (i // tiles_per_seq, 0, 0))
    else:
        assert tm % seq_len == 0
        tiles_per_seq = 1
        aux = jnp.repeat(shift, seq_len, axis=0)
        aux_spec = row_spec
        sh = shift.reshape(nb, 1, D_MODEL)
        sh_spec = pl.BlockSpec((None, 1, D_MODEL), lambda i: (0, 0, 0))
        hn_shape = jax.ShapeDtypeStruct((n, D_MODEL), F32)
        hn_spec = row_spec
    consts = [lw["nw"], lw["mix"], lw["wrkv"], lw["w0"], lw["w1"], lw["w2"], lw["a0"], lw["a1"], lw["a2"],
              lw["g1"], lw["g2"], lw["kk"], lw["ka"], lw["e"], lw["et"]]
    big = jax.ShapeDtypeStruct((n, D_MODEL), F32)
    outs = pl.pallas_call(
        functools.partial(_rwkv_proj_kernel, within_seq=within_seq, seq_len=seq_len, tiles_per_seq=tiles_per_seq),
        grid=(n // tm,),
        in_specs=[row_spec, aux_spec, sh_spec] + [_const_spec(c.shape) for c in consts],
        out_specs=[row_spec] * 7 + [hn_spec],
        out_shape=[big] * 7 + [hn_shape],
        compiler_params=_params("arbitrary"),
        name="rwkv_proj",
    )(h, aux, sh, *consts)
    r, lwd, k, v, kk, b, g, hn = outs
    shift_out = hn.reshape(nb, D_MODEL) if within_seq else hn.reshape(nb, seq_len, D_MODEL)[:, -1]
    return r, lwd, k, v, kk, b, g, shift_out


def _mm(a, b, dims=(((1,), (0,)), ((), ()))):
    return lax.dot_general(a, b, dims, precision=lax.Precision.HIGHEST, preferred_element_type=F32)


def _mm_nt(a, b):
    return _mm(a, b, (((1,), (1,)), ((), ())))


def _wkv_kernel(r_ref, lw_ref, k_ref, v_ref, kk_ref, b_ref, s0_ref, y_ref, st_ref, state, *, n_sub):
    c = pl.program_id(1)
    C = WKV_CHUNK
    n_pairs = N_HEADS // 2

    @pl.when(c == 0)
    def _():
        state[...] = s0_ref[...]

    ri = lax.broadcasted_iota(jnp.int32, (C, C), 0)
    ci = lax.broadcasted_iota(jnp.int32, (C, C), 1)
    tri = (ri >= ci).astype(F32)
    r2 = lax.broadcasted_iota(jnp.int32, (2 * C, 2 * C), 0)
    c2 = lax.broadcasted_iota(jnp.int32, (2 * C, 2 * C), 1)
    strict = r2 > c2
    incl = r2 >= c2
    lane = lax.broadcasted_iota(jnp.int32, (C, LANES), 1)
    head0 = lane < HEAD_DIM

    def hat(x):
        return jnp.concatenate([jnp.where(head0, x, 0.0), jnp.where(head0, 0.0, x)], axis=0)

    def sub_chunk(s, carry):
        rows = pl.ds(pl.multiple_of(s * C, C), C)
        for p in range(n_pairs):
            lanes = slice(p * LANES, (p + 1) * LANES)
            lw = lw_ref[rows, lanes]
            r = r_ref[rows, lanes]
            k = k_ref[rows, lanes]
            v = v_ref[rows, lanes]
            kk = kk_ref[rows, lanes]
            bb = b_ref[rows, lanes]
            cw = _mm(tri, lw)
            cw_end = cw[C - 1:C, :]
            e_neg = jnp.exp(-cw)
            e_end = jnp.exp(cw_end - cw)
            a_h = hat(kk * jnp.exp(cw - lw))
            r_h = hat(r * jnp.exp(cw))
            b_h = hat(bb * e_neg)
            k_h = hat(k * e_neg)
            v_h = hat(v)
            bd_h = hat(bb * e_end)
            kd_h = hat(k * e_end)
            st = state[p]
            low = jnp.where(strict, _mm_nt(a_h, b_h), 0.0)
            ak = jnp.where(strict, _mm_nt(a_h, k_h), 0.0)
            rb = jnp.where(incl, _mm_nt(r_h, b_h), 0.0)
            rk = jnp.where(incl, _mm_nt(r_h, k_h), 0.0)
            x = -(_mm_nt(a_h, st) + _mm(ak, v_h))
            x = x - _mm(low, x)
            lp = low
            for _ in range(5):
                lp = _mm(lp, lp)
                x = x + _mm(lp, x)
            y_h = _mm_nt(r_h, st) + _mm(rb, x) + _mm(rk, v_h)
            y_ref[rows, lanes] = y_h[:C] + y_h[C:]
            state[p] = st * jnp.exp(cw_end) + _mm(x.T, bd_h) + _mm(v_h.T, kd_h)
        return carry

    lax.fori_loop(0, n_sub, sub_chunk, 0)

    @pl.when(c == pl.num_programs(1) - 1)
    def _():
        st_ref[...] = state[...]


def _wkv(r, lwd, k, v, kk, b, s0_bd, seq_len):
    n = r.shape[0]
    nb = n // seq_len
    ct = _tile(seq_len, 256)
    assert ct % WKV_CHUNK == 0
    steps = seq_len // ct
    row_spec = pl.BlockSpec((ct, D_MODEL), lambda bi, ci: (bi * steps + ci, 0))
    st_spec = pl.BlockSpec((None, N_HEADS // 2, LANES, LANES), lambda bi, ci: (bi, 0, 0, 0))
    return pl.pallas_call(
        functools.partial(_wkv_kernel, n_sub=ct // WKV_CHUNK),
        grid=(nb, steps),
        in_specs=[row_spec] * 6 + [st_spec],
        out_specs=[row_spec, st_spec],
        out_shape=[jax.ShapeDtypeStruct((n, D_MODEL), F32), jax.ShapeDtypeStruct(s0_bd.shape, F32)],
        scratch_shapes=[pltpu.VMEM((N_HEADS // 2, LANES, LANES), F32)],
        compiler_params=_params("arbitrary", "arbitrary"),
        name="wkv",
    )(r, lwd, k, v, kk, b, s0_bd)


def _rwkv_out_kernel(y_ref, r_ref, k_ref, v_ref, g_ref, h_ref, lnw_ref, lnb_ref, rk_ref, wo_ref, e_ref, et_ref, o_ref):
    y = y_ref[...]
    inv_n = 1.0 / HEAD_DIM
    mu = _head_sum(y, e_ref, et_ref) * inv_n
    yc = y - mu
    var = _head_sum(yc * yc, e_ref, et_ref) * inv_n
    yn = yc * lax.rsqrt(var + LNX_EPS) * lnw_ref[...] + lnb_ref[...]
    bonus = _head_sum(r_ref[...] * k_ref[...] * rk_ref[...], e_ref, et_ref) * v_ref[...]
    o_ref[...] = h_ref[...] + _bdot((yn + bonus) * g_ref[...], wo_ref[...])


def _rwkv_out(y, r, k, v, g, h, lw):
    n = h.shape[0]
    tm = _tile(n)
    row_spec = pl.BlockSpec((tm, D_MODEL), lambda i: (i, 0))
    consts = [lw["lnw"], lw["lnb"], lw["rk"], lw["wo"], lw["e"], lw["et"]]
    return pl.pallas_call(
        _rwkv_out_kernel,
        grid=(n // tm,),
        in_specs=[row_spec] * 6 + [_const_spec(c.shape) for c in consts],
        out_specs=row_spec,
        out_shape=jax.ShapeDtypeStruct((n, D_MODEL), F32),
        compiler_params=_params("arbitrary"),
        name="rwkv_out",
    )(y, r, k, v, g, h, *consts)


def _t5_buckets(dist):
    d = np.asarray(dist, dtype=np.int64)
    max_exact = REL_BUCKETS // 2
    large = max_exact + (np.log(np.maximum(d, 1) / max_exact) / np.log(REL_MAX_DIST / max_exact)
                         * (REL_BUCKETS - max_exact)).astype(np.int32)
    large = np.minimum(large, REL_BUCKETS - 1)
    return np.where(d < max_exact, d, large).astype(np.int32)


def _band_bias(rel_bias, group):
    win, dil = DILATION_GROUPS[group]
    assert win // dil == Q_BLOCK
    q = np.arange(Q_BLOCK)[:, None]
    kk = np.arange(2 * Q_BLOCK)[None, :]
    m = q + Q_BLOCK - kk
    valid = (m >= 0) & (m <= Q_BLOCK)
    buckets = _t5_buckets(dil * np.clip(m, 0, Q_BLOCK))
    tbl = rel_bias[:, group * N_HEADS:(group + 1) * N_HEADS].astype(F32)
    tile = jnp.moveaxis(tbl[buckets], -1, 0)
    general = jnp.where(valid[None], tile, NEG_INF)
    first = jnp.where((valid & (kk >= Q_BLOCK))[None], tile, NEG_INF)
    return jnp.stack([first, general])


def _attn_kernel(q_ref, klo_p, klo_c, khi_p, khi_c, vlo_p, vlo_c, vhi_p, vhi_c, bias_ref, o_ref, lse_ref):
    lane = lax.broadcasted_iota(jnp.int32, (Q_BLOCK, LANES), 1)
    low_half = lane < HEAD_DIM
    lse_tile = jnp.zeros((Q_BLOCK, LANES), F32)
    nt = (((1,), (1,)), ((), ()))

    def softmax_part(s, h):
        s = s + bias_ref[h]
        m = jnp.max(s, axis=-1, keepdims=True)
        p = jnp.exp(s - m)
        l = jnp.sum(p, axis=-1, keepdims=True)
        return p.astype(BF16), l, m + jnp.log(l)

    for c in range(KV_HEADS):
        cl = slice(c * LANES, (c + 1) * LANES)
        klo = jnp.concatenate([klo_p[:, cl], klo_c[:, cl]], axis=0)
        khi = jnp.concatenate([khi_p[:, cl], khi_c[:, cl]], axis=0)
        vcat = jnp.concatenate([vlo_p[:, cl], vlo_c[:, cl], vhi_p[:, cl], vhi_c[:, cl]], axis=0)
        ga, gb = 2 * c, 2 * c + 1
        lhs = jnp.concatenate([q_ref[:, ga * LANES:(ga + 1) * LANES], q_ref[:, gb * LANES:(gb + 1) * LANES]], axis=0)
        s_lo = lax.dot_general(lhs, klo, nt, preferred_element_type=F32)
        s_hi = lax.dot_general(lhs, khi, nt, preferred_element_type=F32)
        parts = [softmax_part(s_lo[:Q_BLOCK], 4 * c), softmax_part(s_hi[:Q_BLOCK], 4 * c + 1),
                 softmax_part(s_lo[Q_BLOCK:], 4 * c + 2), softmax_part(s_hi[Q_BLOCK:], 4 * c + 3)]
        for j, g in enumerate((ga, gb)):
            (p0, l0, e0), (p1, l1, e1) = parts[2 * j], parts[2 * j + 1]
            o = jnp.dot(jnp.concatenate([p0, p1], axis=1), vcat, preferred_element_type=F32)
            o_ref[:, g * LANES:(g + 1) * LANES] = o / jnp.where(low_half, l0, l1)
            lse_tile = jnp.where(lane == 4 * c + 2 * j, e0, lse_tile)
            lse_tile = jnp.where(lane == 4 * c + 2 * j + 1, e1, lse_tile)
    lse_ref[...] = lse_tile


def _attn_group(q, kvp, bias, nb, seq_len, group):
    _, dil = DILATION_GROUPS[group]
    tsub = seq_len // dil
    nblk = tsub // Q_BLOCK
    q3 = q.reshape(nb, tsub, dil * N_GROUPS * D_MODEL)
    kv3 = kvp.reshape(nb, tsub, dil * 4 * KV_HEADS * LANES)
    kvw = KV_HEADS * LANES

    def kv_spec(part, prev):
        if prev:
            return pl.BlockSpec((None, Q_BLOCK, kvw), lambda i, b, r: (b, jnp.maximum(i - 1, 0), 4 * r + part))
        return pl.BlockSpec((None, Q_BLOCK, kvw), lambda i, b, r: (b, i, 4 * r + part))

    o, lse = pl.pallas_call(
        _attn_kernel,
        grid=(nblk, nb, dil),
        in_specs=[pl.BlockSpec((None, Q_BLOCK, D_MODEL), lambda i, b, r: (b, i, N_GROUPS * r + group))]
                 + [kv_spec(part, prev) for part in range(4) for prev in (True, False)]
                 + [pl.BlockSpec((None, N_HEADS, Q_BLOCK, 2 * Q_BLOCK), lambda i, b, r: (jnp.minimum(i, 1), 0, 0, 0))],
        out_specs=[pl.BlockSpec((None, Q_BLOCK, D_MODEL), lambda i, b, r: (b, i, r)),
                   pl.BlockSpec((None, Q_BLOCK, LANES), lambda i, b, r: (b, i, r))],
        out_shape=[jax.ShapeDtypeStruct((nb, tsub, dil * D_MODEL), F32),
                   jax.ShapeDtypeStruct((nb, tsub, dil * LANES), F32)],
        compiler_params=_params("arbitrary", "arbitrary", "arbitrary"),
        name=f"attn_g{group}",
    )(q3, kv3, kv3, kv3, kv3, kv3, kv3, kv3, kv3, bias)
    return o.reshape(nb * seq_len, D_MODEL), lse.reshape(nb * seq_len, LANES)


def _attn_out_kernel(o0_ref, o1_ref, o2_ref, l0_ref, l1_ref, l2_ref, h_ref, wo_ref, et_ref, out_ref):
    l0, l1, l2 = l0_ref[...], l1_ref[...], l2_ref[...]
    m = jnp.maximum(jnp.maximum(l0, l1), l2)
    w0, w1, w2 = jnp.exp(l0 - m), jnp.exp(l1 - m), jnp.exp(l2 - m)
    inv = 1.0 / (w0 + w1 + w2)
    et = et_ref[...]
    att = (_dot_split(w0 * inv, et) * o0_ref[...] + _dot_split(w1 * inv, et) * o1_ref[...]
           + _dot_split(w2 * inv, et) * o2_ref[...])
    out_ref[...] = h_ref[...] + _bdot(att, wo_ref[...])


def _attn_out(outs, lses, h, wo, et):
    n = h.shape[0]
    tm = _tile(n)
    row_spec = pl.BlockSpec((tm, D_MODEL), lambda i: (i, 0))
    lse_spec = pl.BlockSpec((tm, LANES), lambda i: (i, 0))
    return pl.pallas_call(
        _attn_out_kernel,
        grid=(n // tm,),
        in_specs=[row_spec] * 3 + [lse_spec] * 3 + [row_spec, _const_spec(wo.shape), _const_spec(et.shape)],
        out_specs=row_spec,
        out_shape=jax.ShapeDtypeStruct((n, D_MODEL), F32),
        compiler_params=_params("arbitrary"),
        name="attn_out",
    )(*outs, *lses, h, wo, et)


def _decode_bias(rel_bias, seq_len, cache_len):
    t = np.arange(seq_len)
    rows_t = np.tile(t, N_GROUPS * N_HEADS)
    rows_h = np.tile(np.repeat(np.arange(N_HEADS), seq_len), N_GROUPS)
    rows_g = np.repeat(np.arange(N_GROUPS), N_HEADS * seq_len)
    win = np.array([w for w, _ in DILATION_GROUPS])[rows_g][:, None]
    dil = np.array([d for _, d in DILATION_GROUPS])[rows_g][:, None]
    pos = np.concatenate([np.arange(cache_len), cache_len + np.arange(LANES)])[None, :]
    dist = cache_len + rows_t[:, None] - pos
    valid = (dist >= 0) & (dist % dil == 0) & (dist <= win) & (pos < cache_len + seq_len)
    buckets = _t5_buckets(np.clip(dist, 0, MAX_WINDOW))
    cols = (rows_g * N_HEADS + rows_h)[:, None]
    bias = jnp.where(valid, rel_bias.astype(F32)[buckets, cols], NEG_INF)
    return bias[:, :cache_len], bias[:, cache_len:]


def _attn_decode_kernel(q_ref, cache_ref, kvn_ref, h_ref, bc_ref, bn_ref, wo_ref, out_ref, *, seq_len):
    kvd = KV_HEADS * HEAD_DIM
    nslot = N_GROUPS * N_HEADS
    rows_g = N_HEADS * seq_len
    nt = (((1,), (1,)), ((), ()))
    cache = cache_ref[...]
    kc = cache[:, :kvd].astype(BF16)
    vc = cache[:, kvd:].astype(BF16)
    kvn = kvn_ref[...]
    pad = jnp.zeros((LANES - seq_len, kvd), F32)
    kn = jnp.concatenate([kvn[:, :kvd], pad], axis=0).astype(BF16)
    vn = jnp.concatenate([kvn[:, kvd:], pad], axis=0).astype(BF16)
    lhs = jnp.concatenate([q_ref[:, s * kvd:(s + 1) * kvd] for s in range(nslot)], axis=0).astype(BF16)
    sc = lax.dot_general(lhs, kc, nt, preferred_element_type=F32) + bc_ref[...]
    sn = lax.dot_general(lhs, kn, nt, preferred_element_type=F32) + bn_ref[...]
    m_g = jnp.maximum(jnp.max(sc, axis=-1, keepdims=True), jnp.max(sn, axis=-1, keepdims=True))
    m = jnp.maximum(jnp.maximum(m_g[:rows_g], m_g[rows_g:2 * rows_g]), m_g[2 * rows_g:])
    m3 = jnp.concatenate([m, m, m], axis=0)
    pc = jnp.exp(sc - m3)
    pn = jnp.exp(sn - m3)
    l_g = jnp.sum(pc, axis=-1, keepdims=True) + jnp.sum(pn, axis=-1, keepdims=True)
    num_g = (jnp.dot(pc.astype(BF16), vc, preferred_element_type=F32)
             + jnp.dot(pn.astype(BF16), vn, preferred_element_type=F32))
    l = l_g[:rows_g] + l_g[rows_g:2 * rows_g] + l_g[2 * rows_g:]
    num = num_g[:rows_g] + num_g[rows_g:2 * rows_g] + num_g[2 * rows_g:]
    row = lax.broadcasted_iota(jnp.int32, (rows_g, kvd), 0)
    lane = lax.broadcasted_iota(jnp.int32, (rows_g, kvd), 1)
    own = (row // (Q_PER_KV * seq_len)) == (lane // HEAD_DIM)
    att = jnp.where(own, num / l, 0.0)
    out = h_ref[...]
    for r in range(Q_PER_KV):
        a_r = att[r * seq_len:(r + 1) * seq_len]
        for c in range(1, KV_HEADS):
            a_r = a_r + att[(c * Q_PER_KV + r) * seq_len:(c * Q_PER_KV + r + 1) * seq_len]
        out = out + _bdot(a_r, wo_ref[r])
    out_ref[...] = out


def _attn_decode(q, cache, kv_new, h, bias_c, bias_n, wo_r, nb, seq_len):
    cache_len = cache.shape[1]
    qw = q.shape[1]
    kvw = 2 * KV_HEADS * HEAD_DIM
    return pl.pallas_call(
        functools.partial(_attn_decode_kernel, seq_len=seq_len),
        grid=(nb,),
        in_specs=[pl.BlockSpec((seq_len, qw), lambda b: (b, 0)),
                  pl.BlockSpec((None, cache_len, kvw), lambda b: (b, 0, 0)),
                  pl.BlockSpec((seq_len, kvw), lambda b: (b, 0)),
                  pl.BlockSpec((seq_len, D_MODEL), lambda b: (b, 0)),
                  _const_spec(bias_c.shape), _const_spec(bias_n.shape), _const_spec(wo_r.shape)],
        out_specs=pl.BlockSpec((seq_len, D_MODEL), lambda b: (b, 0)),
        out_shape=jax.ShapeDtypeStruct((nb * seq_len, D_MODEL), F32),
        compiler_params=_params("arbitrary"),
        name="attn_decode",
    )(q, cache, kv_new, h, bias_c, bias_n, wo_r)


def _prep_weights(norm_w, ffn1_wi, ffn1_wo, ffn2_wi, ffn2_wo, pe_proj, pe_gate,
                  rwkv_mix, rwkv_wrkv, rwkv_wo, rwkv_w0, rwkv_w1, rwkv_w2, rwkv_a0, rwkv_a1, rwkv_a2,
                  rwkv_g1, rwkv_g2, rwkv_kk, rwkv_ka, rwkv_rk, rwkv_lnx_w, rwkv_lnx_b,
                  attn_wq, attn_wo, kv_norm, w_kv, rel_bias, final_norm):
    def row(v):
        return v.reshape(1, -1).astype(F32)

    def pad_cols(w, n):
        return jnp.pad(w, ((0, 0), (0, n - w.shape[1]))).astype(BF16)

    def pad_rows(w, n):
        return jnp.pad(w, ((0, n - w.shape[0]), (0, 0))).astype(BF16)

    def ffn_w(wi, wo):
        wg = wi[:, :D_FF].reshape(D_MODEL, N_FFN_CHUNKS, FFN_CHUNK).transpose(1, 0, 2).astype(BF16)
        wu = wi[:, D_FF:].reshape(D_MODEL, N_FFN_CHUNKS, FFN_CHUNK).transpose(1, 0, 2).astype(BF16)
        return wg, wu, wo.reshape(N_FFN_CHUNKS, FFN_CHUNK, D_MODEL).astype(BF16)

    head_of_lane = np.arange(D_MODEL) // HEAD_DIM
    e = jnp.asarray(head_of_lane[:, None] == np.arange(LANES)[None, :], BF16)
    et = jnp.asarray(np.arange(LANES)[:, None] == head_of_lane[None, :], BF16)

    depth = norm_w.shape[0]
    layers = []
    for i in range(depth):
        layers.append(dict(
            nw=[row(norm_w[i, j]) for j in range(4)],
            ffn1=ffn_w(ffn1_wi[i], ffn1_wo[i]), ffn2=ffn_w(ffn2_wi[i], ffn2_wo[i]),
            pe_gate=pe_gate[i].astype(BF16), pe_proj=pe_proj[i].astype(BF16)))
    n_a = depth // 2
    rw = []
    for i in range(n_a):
        rw.append(dict(
            nw=row(norm_w[i, 1]), mix=rwkv_mix[i].astype(F32), wrkv=rwkv_wrkv[i].astype(BF16),
            w0=row(rwkv_w0[i]), w1=pad_cols(rwkv_w1[i], LANES), w2=pad_rows(rwkv_w2[i], LANES),
            a0=row(rwkv_a0[i]), a1=pad_cols(rwkv_a1[i], LANES), a2=pad_rows(rwkv_a2[i], LANES),
            g1=pad_cols(rwkv_g1[i], 2 * LANES), g2=pad_rows(rwkv_g2[i], 2 * LANES),
            kk=row(rwkv_kk[i]), ka=row(rwkv_ka[i]), rk=row(rwkv_rk[i]),
            lnw=row(rwkv_lnx_w[i]), lnb=row(rwkv_lnx_b[i]), wo=rwkv_wo[i].astype(BF16), e=e, et=et))
    scale = HEAD_DIM ** -0.5
    at = []
    for j in range(depth - n_a):
        wq = attn_wq[j] * scale
        wq6 = wq.reshape(D_MODEL, N_GROUPS, KV_HEADS, Q_PER_KV, HEAD_DIM)
        wq_slots = jnp.einsum("dgcre,cx->dgcrxe", wq6, jnp.eye(KV_HEADS, dtype=wq.dtype))
        wq_slots = wq_slots.reshape(D_MODEL, N_GROUPS * N_HEADS * KV_HEADS * HEAD_DIM).astype(BF16)
        wo_r = attn_wo[j].reshape(KV_HEADS, Q_PER_KV, HEAD_DIM, D_MODEL).transpose(1, 0, 2, 3)
        wo_r = wo_r.reshape(Q_PER_KV, KV_HEADS * HEAD_DIM, D_MODEL).astype(BF16)
        at.append(dict(wq=wq.astype(BF16), wq_slots=wq_slots, wo=attn_wo[j].astype(BF16), wo_r=wo_r))
    wkv4 = w_kv.reshape(D_MODEL, 2, KV_HEADS, HEAD_DIM)
    zeros = jnp.zeros_like(wkv4)
    lo = jnp.concatenate([wkv4, zeros], axis=-1)
    hi = jnp.concatenate([zeros, wkv4], axis=-1)
    w_kv_pad = jnp.stack([lo[:, 0], hi[:, 0], lo[:, 1], hi[:, 1]], axis=1).reshape(D_MODEL, 4 * KV_HEADS * LANES)
    return dict(layers=layers, rwkv=rw, attn=at, kv_norm=row(kv_norm), w_kv=w_kv.astype(BF16),
                w_kv_pad=w_kv_pad.astype(BF16), final_norm=row(final_norm), rel_bias=rel_bias, et=et)


def _state_to_blockdiag(s):
    nb = s.shape[0]
    s5 = s.astype(F32).reshape(nb, N_HEADS // 2, 2, HEAD_DIM, HEAD_DIM)
    bd = jnp.einsum("bphij,hg->bphigj", s5, jnp.eye(2, dtype=F32))
    return bd.reshape(nb, N_HEADS // 2, LANES, LANES)


def _blockdiag_to_state(bd):
    nb = bd.shape[0]
    b6 = bd.reshape(nb, N_HEADS // 2, 2, HEAD_DIM, 2, HEAD_DIM)
    s = jnp.stack([b6[:, :, 0, :, 0, :], b6[:, :, 1, :, 1, :]], axis=2)
    return s.reshape(nb, N_HEADS, HEAD_DIM, HEAD_DIM)


def _pad_time(a, nb, seq_len, padded):
    a3 = a.reshape(nb, seq_len, D_MODEL)
    return jnp.pad(a3, ((0, 0), (0, padded - seq_len), (0, 0))).reshape(nb * padded, D_MODEL)


def _trunk(x, p, wkv0, shift0, cache, w):
    nb, seq_len, _ = x.shape
    n = nb * seq_len
    depth = len(w["layers"])
    n_a = depth // 2
    h = x.reshape(n, D_MODEL).astype(F32)
    wkv_out, shift_out = [], []
    kv_rows = kv_pad = None
    for i in range(depth):
        lw = w["layers"][i]
        if i == n_a:
            kv_rows = _norm_mm(h, w["kv_norm"], w["w_kv"], F32, name="kv_proj")
            if cache is None:
                kv_pad = _norm_mm(h, w["kv_norm"], w["w_kv_pad"], BF16, name="kv_proj_pad")
        h = _ffn(h, lw["nw"][0], *lw["ffn1"])
        if i < n_a:
            rwl = w["rwkv"][i]
            r, lwd, k, v, kk, b, g, sh = _rwkv_proj(h, shift0[i].astype(F32), seq_len, rwl)
            s0 = _state_to_blockdiag(wkv0[i])
            if seq_len % WKV_CHUNK:
                tp = -(-seq_len // WKV_CHUNK) * WKV_CHUNK
                padded = [_pad_time(a, nb, seq_len, tp) for a in (r, lwd, k, v, kk, b)]
                y, st = _wkv(*padded, s0, tp)
                y = y.reshape(nb, tp, D_MODEL)[:, :seq_len].reshape(n, D_MODEL)
            else:
                y, st = _wkv(r, lwd, k, v, kk, b, s0, seq_len)
            h = _rwkv_out(y, r, k, v, g, h, rwl)
            wkv_out.append(_blockdiag_to_state(st).astype(x.dtype))
            shift_out.append(sh.astype(x.dtype))
        else:
            al = w["attn"][i - n_a]
            if cache is None:
                q = _norm_mm(h, lw["nw"][1], al["wq"], BF16, name="q_proj")
                outs, lses = [], []
                for gi in range(N_GROUPS):
                    o, lse = _attn_group(q, kv_pad, _band_bias(w["rel_bias"], gi), nb, seq_len, gi)
                    outs.append(o)
                    lses.append(lse)
                h = _attn_out(outs, lses, h, al["wo"], w["et"])
            else:
                q = _norm_mm(h, lw["nw"][1], al["wq_slots"], F32, col_tile=2048, name="q_proj_slots")
                bias_c, bias_n = _decode_bias(w["rel_bias"], seq_len, cache.shape[1])
                h = _attn_decode(q, cache, kv_rows, h, bias_c, bias_n, al["wo_r"], nb, seq_len)
        h = _ffn(h, lw["nw"][2], *lw["ffn2"])
        h = _pe(h, p[i].reshape(n, PLE_DIM).astype(F32), lw["nw"][3], lw["pe_gate"], lw["pe_proj"],
                w["final_norm"], final=(i == depth - 1))
    y = h.reshape(nb, seq_len, D_MODEL).astype(x.dtype)
    kv_rows = kv_rows.reshape(nb, seq_len, 2, KV_HEADS, HEAD_DIM).astype(x.dtype)
    return y, jnp.stack(wkv_out), jnp.stack(shift_out), kv_rows


def kernel(x_prompt, x_sample, state_wkv, state_shift, cache_kv, p_prompt, p_sample, norm_w, ffn1_wi, ffn1_wo, ffn2_wi, ffn2_wo, pe_proj, pe_gate, rwkv_mix, rwkv_wrkv, rwkv_wo, rwkv_w0, rwkv_w1, rwkv_w2, rwkv_a0, rwkv_a1, rwkv_a2, rwkv_g1, rwkv_g2, rwkv_kk, rwkv_ka, rwkv_rk, rwkv_lnx_w, rwkv_lnx_b, attn_wq, attn_wo, kv_norm, w_kv, rel_bias, final_norm):
    w = _prep_weights(norm_w, ffn1_wi, ffn1_wo, ffn2_wi, ffn2_wo, pe_proj, pe_gate,
                      rwkv_mix, rwkv_wrkv, rwkv_wo, rwkv_w0, rwkv_w1, rwkv_w2, rwkv_a0, rwkv_a1, rwkv_a2,
                      rwkv_g1, rwkv_g2, rwkv_kk, rwkv_ka, rwkv_rk, rwkv_lnx_w, rwkv_lnx_b,
                      attn_wq, attn_wo, kv_norm, w_kv, rel_bias, final_norm)
    n_a = norm_w.shape[0] // 2
    nb, seq_len, _ = x_prompt.shape
    wkv0 = jnp.zeros((n_a, nb, N_HEADS, HEAD_DIM, HEAD_DIM), F32)
    shift0 = jnp.zeros((n_a, nb, D_MODEL), x_prompt.dtype)
    y_p, wkv_p, shift_p, kv_p = _trunk(x_prompt, p_prompt, wkv0, shift0, None, w)
    kv_prompt = kv_p[:, seq_len - min(MAX_WINDOW, seq_len):]
    cache = cache_kv.reshape(cache_kv.shape[0], cache_kv.shape[1], 2 * KV_HEADS * HEAD_DIM).astype(F32)
    y_s, wkv_s, shift_s, kv_s = _trunk(x_sample, p_sample, state_wkv, state_shift, cache, w)
    return (y_p, y_s, wkv_p, shift_p, kv_prompt, wkv_s, shift_s, kv_s)
```

```python
import functools

import numpy as np
import jax
import jax.numpy as jnp
from jax import lax
from jax.experimental import pallas as pl
from jax.experimental.pallas import tpu as pltpu

F32 = jnp.float32
BF16 = jnp.bfloat16

D_MODEL = 1024
D_FF = 2816
PLE_DIM = 256
RMS_EPS = 1e-6
HEAD_DIM = 64
N_HEADS = D_MODEL // HEAD_DIM
LNX_EPS = 64e-5
KV_HEADS = 4
Q_PER_KV = N_HEADS // KV_HEADS
DILATION_GROUPS = ((128, 1), (512, 4), (2048, 16))
N_GROUPS = len(DILATION_GROUPS)
MAX_WINDOW = 2048
REL_BUCKETS = 32
REL_MAX_DIST = 2048
NEG_INF = -1e30

LANES = 128
FFN_CHUNK = 256
N_FFN_CHUNKS = D_FF // FFN_CHUNK
TOKEN_TILE = 256
WKV_CHUNK = 64
Q_BLOCK = 128
VMEM_LIMIT = 56 * 1024 * 1024
EXP_MINUS_HALF = 0.6065306597126334


def _params(*sem):
    return pltpu.CompilerParams(dimension_semantics=sem, vmem_limit_bytes=VMEM_LIMIT)


def _const_spec(shape):
    return pl.BlockSpec(shape, lambda *_: (0,) * len(shape))


def _tile(n, pref=TOKEN_TILE):
    t = min(n, pref)
    while n % t:
        t -= 8
    return t


def _rms(x, g):
    return x * lax.rsqrt(jnp.mean(x * x, axis=-1, keepdims=True) + RMS_EPS) * g


def _bdot(a, b):
    return jnp.dot(a.astype(BF16), b, preferred_element_type=F32)


def _dot_split(x, w):
    hi = x.astype(BF16)
    lo = (x - hi.astype(F32)).astype(BF16)
    return (jnp.dot(hi, w, preferred_element_type=F32) + jnp.dot(lo, w, preferred_element_type=F32))


def _head_sum(x, e_ref, et_ref):
    return _dot_split(_dot_split(x, e_ref[...]), et_ref[...])


def _ffn_kernel(x_ref, g_ref, wg_ref, wu_ref, wo_ref, o_ref):
    x = x_ref[...]
    xn = _rms(x, g_ref[...]).astype(BF16)
    acc = jnp.zeros_like(x)
    for j in range(N_FFN_CHUNKS):
        gate = jnp.dot(xn, wg_ref[j], preferred_element_type=F32)
        up = jnp.dot(xn, wu_ref[j], preferred_element_type=F32)
        act = (gate * jax.nn.sigmoid(gate) * up).astype(BF16)
        acc = acc + jnp.dot(act, wo_ref[j], preferred_element_type=F32)
    o_ref[...] = x + 0.5 * acc


def _ffn(h, g, wg, wu, wo):
    n = h.shape[0]
    tm = _tile(n)
    return pl.pallas_call(
        _ffn_kernel,
        grid=(n // tm,),
        in_specs=[pl.BlockSpec((tm, D_MODEL), lambda i: (i, 0)), _const_spec((1, D_MODEL)),
                  _const_spec(wg.shape), _const_spec(wu.shape), _const_spec(wo.shape)],
        out_specs=pl.BlockSpec((tm, D_MODEL), lambda i: (i, 0)),
        out_shape=jax.ShapeDtypeStruct((n, D_MODEL), F32),
        compiler_params=_params("arbitrary"),
        name="ffn",
    )(h, g, wg, wu, wo)


def _norm_mm_kernel(x_ref, g_ref, w_ref, o_ref):
    o_ref[...] = _bdot(_rms(x_ref[...], g_ref[...]), w_ref[...]).astype(o_ref.dtype)


def _norm_mm(h, g, w, out_dtype, col_tile=None, name="norm_mm"):
    n = h.shape[0]
    tm = _tile(n)
    nout = w.shape[1]
    tn = nout if col_tile is None else col_tile
    return pl.pallas_call(
        _norm_mm_kernel,
        grid=(n // tm, nout // tn),
        in_specs=[pl.BlockSpec((tm, D_MODEL), lambda i, j: (i, 0)), _const_spec((1, D_MODEL)),
                  pl.BlockSpec((D_MODEL, tn), lambda i, j: (0, j))],
        out_specs=pl.BlockSpec((tm, tn), lambda i, j: (i, j)),
        out_shape=jax.ShapeDtypeStruct((n, nout), out_dtype),
        compiler_params=_params("arbitrary", "arbitrary"),
        name=name,
    )(h, g, w)


def _pe_kernel(x_ref, p_ref, g_ref, wgate_ref, wproj_ref, gf_ref, o_ref, *, final):
    x = x_ref[...]
    gate = jax.nn.sigmoid(_bdot(_rms(x, g_ref[...]), wgate_ref[...]))
    y = x + gate * _bdot(p_ref[...], wproj_ref[...])
    o_ref[...] = _rms(y, gf_ref[...]) if final else y


def _pe(h, p, g, wgate, wproj, gfinal, final):
    n = h.shape[0]
    tm = _tile(n)
    return pl.pallas_call(
        functools.partial(_pe_kernel, final=final),
        grid=(n // tm,),
        in_specs=[pl.BlockSpec((tm, D_MODEL), lambda i: (i, 0)), pl.BlockSpec((tm, PLE_DIM), lambda i: (i, 0)),
                  _const_spec((1, D_MODEL)), _const_spec(wgate.shape), _const_spec(wproj.shape),
                  _const_spec((1, D_MODEL))],
        out_specs=pl.BlockSpec((tm, D_MODEL), lambda i: (i, 0)),
        out_shape=jax.ShapeDtypeStruct((n, D_MODEL), F32),
        compiler_params=_params("arbitrary"),
        name="pe_final" if final else "pe",
    )(h, p, g, wgate, wproj, gfinal)


def _rwkv_proj_kernel(h_ref, aux_ref, sh_ref, nw_ref, mix_ref, wrkv_ref, w0_ref, w1_ref, w2_ref,
                      a0_ref, a1_ref, a2_ref, g1_ref, g2_ref, kk_ref, ka_ref, e_ref, et_ref,
                      r_o, lw_o, k_o, v_o, kk_o, b_o, g_o, hn_o, *, within_seq, seq_len, tiles_per_seq):
    i = pl.program_id(0)
    nw = nw_ref[...]
    hn = _rms(h_ref[...], nw)
    tm = hn.shape[0]
    row = lax.broadcasted_iota(jnp.int32, hn.shape, 0)
    rolled = pltpu.roll(hn, 1, 0)
    if within_seq:
        prev_last = _rms(aux_ref[...], nw)[7:8]
        tile_in_seq = lax.rem(jnp.full((1, D_MODEL), i, jnp.int32), tiles_per_seq)
        first = jnp.where(tile_in_seq == 0, sh_ref[...], prev_last)
        x_prev = jnp.where(row == 0, first, rolled)
        hn_o[...] = hn[tm - 1:tm]
    else:
        x_prev = jnp.where(lax.rem(row, seq_len) == 0, aux_ref[...], rolled)
        hn_o[...] = hn
    xx = x_prev - hn
    mix = mix_ref[...]
    xr, xw, xk, xv, xa, xg = (hn + xx * mix[j:j + 1] for j in range(6))
    r = _bdot(xr, wrkv_ref[0])
    k = _bdot(xk, wrkv_ref[1])
    v = _bdot(xv, wrkv_ref[2])
    wl = w0_ref[...] + _bdot(jnp.tanh(_bdot(xw, w1_ref[...])), w2_ref[...])
    lw = -EXP_MINUS_HALF * jax.nn.sigmoid(wl)
    a = jax.nn.sigmoid(a0_ref[...] + _bdot(_bdot(xa, a1_ref[...]), a2_ref[...]))
    g = _bdot(jax.nn.sigmoid(_bdot(xg, g1_ref[...])), g2_ref[...])
    kkv = k * kk_ref[...]
    kk = kkv / jnp.maximum(jnp.sqrt(_head_sum(kkv * kkv, e_ref, et_ref)), 1e-12)
    r_o[...] = r
    lw_o[...] = lw
    k_o[...] = k * (1.0 + (a - 1.0) * ka_ref[...])
    v_o[...] = v
    kk_o[...] = kk
    b_o[...] = kk * a
    g_o[...] = g


def _rwkv_proj(h, shift, seq_len, lw):
    n = h.shape[0]
    nb = n // seq_len
    tm = _tile(n)
    within_seq = seq_len % tm == 0
    row_spec = pl.BlockSpec((tm, D_MODEL), lambda i: (i, 0))
    if within_seq:
        tiles_per_seq = seq_len // tm
        aux = h
        aux_spec = pl.BlockSpec((8, D_MODEL), lambda i: (jnp.maximum(i * (tm // 8) - 1, 0), 0))
        sh = shift.reshape(nb, 1, D_MODEL)
        sh_spec = pl.BlockSpec((None, 1, D_MODEL), lambda i: (i // tiles_per_seq, 0, 0))
        hn_shape = jax.ShapeDtypeStruct((nb, 1, D_MODEL), F32)
        hn_spec = pl.BlockSpec((None, 1, D_MODEL), lambda i: (i // tiles_per_seq, 0, 0))
    else:
        assert tm % seq_len == 0
        tiles_per_seq = 1
        aux = jnp.repeat(shift, seq_len, axis=0)
        aux_spec = row_spec
        sh = shift.reshape(nb, 1, D_MODEL)
        sh_spec = pl.BlockSpec((None, 1, D_MODEL), lambda i: (0, 0, 0))
        hn_shape = jax.ShapeDtypeStruct((n, D_MODEL), F32)
        hn_spec = row_spec
    consts = [lw["nw"], lw["mix"], lw["wrkv"], lw["w0"], lw["w1"], lw["w2"], lw["a0"], lw["a1"], lw["a2"],
              lw["g1"], lw["g2"], lw["kk"], lw["ka"], lw["e"], lw["et"]]
    big = jax.ShapeDtypeStruct((n, D_MODEL), F32)
    outs = pl.pallas_call(
        functools.partial(_rwkv_proj_kernel, within_seq=within_seq, seq_len=seq_len, tiles_per_seq=tiles_per_seq),
        grid=(n // tm,),
        in_specs=[row_spec, aux_spec, sh_spec] + [_const_spec(c.shape) for c in consts],
        out_specs=[row_spec] * 7 + [hn_spec],
        out_shape=[big] * 7 + [hn_shape],
        compiler_params=_params("arbitrary"),
        name="rwkv_proj",
    )(h, aux, sh, *consts)
    r, lwd, k, v, kk, b, g, hn = outs
    shift_out = hn.reshape(nb, D_MODEL) if within_seq else hn.reshape(nb, seq_len, D_MODEL)[:, -1]
    return r, lwd, k, v, kk, b, g, shift_out


_NT = (((1,), (1,)), ((), ()))


def _bmm(a, b):
    return jnp.dot(a.astype(BF16), b.astype(BF16), preferred_element_type=F32)


def _bmm_nt(a, b):
    return lax.dot_general(a.astype(BF16), b.astype(BF16), _NT, preferred_element_type=F32)


def _wkv_kernel(r_ref, lw_ref, k_ref, v_ref, kk_ref, b_ref, s0_ref, y_ref, st_ref, state, *, n_sub):
    c = pl.program_id(1)
    C = WKV_CHUNK
    C2 = 2 * C
    n_pairs = N_HEADS // 2

    @pl.when(c == 0)
    def _():
        state[...] = s0_ref[...]

    ri = lax.broadcasted_iota(jnp.int32, (C, C), 0)
    ci = lax.broadcasted_iota(jnp.int32, (C, C), 1)
    tri = (ri >= ci).astype(BF16)
    r2 = lax.broadcasted_iota(jnp.int32, (C2, C2), 0)
    c2 = lax.broadcasted_iota(jnp.int32, (C2, C2), 1)
    strict = r2 > c2
    incl = r2 >= c2
    lane = lax.broadcasted_iota(jnp.int32, (C, LANES), 1)
    head0 = lane < HEAD_DIM

    def hat(x):
        return jnp.concatenate([jnp.where(head0, x, 0.0), jnp.where(head0, 0.0, x)], axis=0)

    def sub_chunk(s, carry):
        rows = pl.ds(pl.multiple_of(s * C, C), C)
        lw = lw_ref[rows, :]
        p1 = lw.astype(BF16)
        rem = lw - p1.astype(F32)
        p2 = rem.astype(BF16)
        p3 = (rem - p2.astype(F32)).astype(BF16)
        cw = (jnp.dot(tri, p1, preferred_element_type=F32) + jnp.dot(tri, p2, preferred_element_type=F32)
              + jnp.dot(tri, p3, preferred_element_type=F32))
        cw_end = cw[C - 1:C, :]
        e_neg = jnp.exp(-cw)
        e_end = jnp.exp(cw_end - cw)
        kk = kk_ref[rows, :]
        bb = b_ref[rows, :]
        k = k_ref[rows, :]
        a_all = kk * jnp.exp(cw - lw)
        r_all = r_ref[rows, :] * jnp.exp(cw)
        b_all = bb * e_neg
        k_all = k * e_neg
        bd_all = bb * e_end
        kd_all = k * e_end
        v_all = v_ref[rows, :]
        decay = jnp.exp(cw_end)
        for p in range(n_pairs):
            lanes = slice(p * LANES, (p + 1) * LANES)
            ar_h = jnp.concatenate([hat(a_all[:, lanes]), hat(r_all[:, lanes])], axis=0)
            bk_h = jnp.concatenate([hat(b_all[:, lanes]), hat(k_all[:, lanes])], axis=0)
            v_h = hat(v_all[:, lanes])
            st = state[p]
            g = _bmm_nt(ar_h, bk_h)
            low = jnp.where(strict, g[:C2, :C2], 0.0)
            ak = jnp.where(strict, g[:C2, C2:], 0.0)
            rbk = jnp.concatenate([jnp.where(incl, g[C2:, :C2], 0.0), jnp.where(incl, g[C2:, C2:], 0.0)], axis=1)
            ss = _bmm_nt(ar_h, st)
            x = -(ss[:C2] + _bmm(ak, v_h))
            t = _bmm(low, jnp.concatenate([low, x], axis=1))
            lp = t[:, :C2]
            x = x - t[:, C2:]
            for _ in range(4):
                t = _bmm(lp, jnp.concatenate([lp, x], axis=1))
                lp = t[:, :C2]
                x = x + t[:, C2:]
            x = x + _bmm(lp, x)
            xv = jnp.concatenate([x, v_h], axis=0)
            y_h = ss[C2:] + _bmm(rbk, xv)
            y_ref[rows, lanes] = y_h[:C] + y_h[C:]
            bkd_h = jnp.concatenate([hat(bd_all[:, lanes]), hat(kd_all[:, lanes])], axis=0)
            state[p] = st * decay[:, lanes] + _bmm(xv.T, bkd_h)
        return carry

    lax.fori_loop(0, n_sub, sub_chunk, 0)

    @pl.when(c == pl.num_programs(1) - 1)
    def _():
        st_ref[...] = state[...]


def _wkv(r, lwd, k, v, kk, b, s0_bd, seq_len):
    n = r.shape[0]
    nb = n // seq_len
    ct = _tile(seq_len, 256)
    assert ct % WKV_CHUNK == 0
    steps = seq_len // ct
    row_spec = pl.BlockSpec((ct, D_MODEL), lambda bi, ci: (bi * steps + ci, 0))
    st_spec = pl.BlockSpec((None, N_HEADS // 2, LANES, LANES), lambda bi, ci: (bi, 0, 0, 0))
    return pl.pallas_call(
        functools.partial(_wkv_kernel, n_sub=ct // WKV_CHUNK),
        grid=(nb, steps),
        in_specs=[row_spec] * 6 + [st_spec],
        out_specs=[row_spec, st_spec],
        out_shape=[jax.ShapeDtypeStruct((n, D_MODEL), F32), jax.ShapeDtypeStruct(s0_bd.shape, F32)],
        scratch_shapes=[pltpu.VMEM((N_HEADS // 2, LANES, LANES), F32)],
        compiler_params=_params("arbitrary", "arbitrary"),
        name="wkv",
    )(r, lwd, k, v, kk, b, s0_bd)


def _rwkv_out_kernel(y_ref, r_ref, k_ref, v_ref, g_ref, h_ref, lnw_ref, lnb_ref, rk_ref, wo_ref, e_ref, et_ref, o_ref):
    y = y_ref[...]
    inv_n = 1.0 / HEAD_DIM
    mu = _head_sum(y, e_ref, et_ref) * inv_n
    yc = y - mu
    var = _head_sum(yc * yc, e_ref, et_ref) * inv_n
    yn = yc * lax.rsqrt(var + LNX_EPS) * lnw_ref[...] + lnb_ref[...]
    bonus = _head_sum(r_ref[...] * k_ref[...] * rk_ref[...], e_ref, et_ref) * v_ref[...]
    o_ref[...] = h_ref[...] + _bdot((yn + bonus) * g_ref[...], wo_ref[...])


def _rwkv_out(y, r, k, v, g, h, lw):
    n = h.shape[0]
    tm = _tile(n)
    row_spec = pl.BlockSpec((tm, D_MODEL), lambda i: (i, 0))
    consts = [lw["lnw"], lw["lnb"], lw["rk"], lw["wo"], lw["e"], lw["et"]]
    return pl.pallas_call(
        _rwkv_out_kernel,
        grid=(n // tm,),
        in_specs=[row_spec] * 6 + [_const_spec(c.shape) for c in consts],
        out_specs=row_spec,
        out_shape=jax.ShapeDtypeStruct((n, D_MODEL), F32),
        compiler_params=_params("arbitrary"),
        name="rwkv_out",
    )(y, r, k, v, g, h, *consts)


def _t5_buckets(dist):
    d = np.asarray(dist, dtype=np.int64)
    max_exact = REL_BUCKETS // 2
    large = max_exact + (np.log(np.maximum(d, 1) / max_exact) / np.log(REL_MAX_DIST / max_exact)
                         * (REL_BUCKETS - max_exact)).astype(np.int32)
    large = np.minimum(large, REL_BUCKETS - 1)
    return np.where(d < max_exact, d, large).astype(np.int32)


def _band_bias(rel_bias, group):
    win, dil = DILATION_GROUPS[group]
    assert win // dil == Q_BLOCK
    m = np.arange(-(Q_BLOCK - 1), 2 * Q_BLOCK)
    valid = (m >= 0) & (m <= Q_BLOCK)
    buckets = _t5_buckets(dil * np.clip(m, 0, Q_BLOCK))
    tbl = jnp.take(rel_bias[:, group * N_HEADS:(group + 1) * N_HEADS].astype(F32), buckets, axis=0).T
    rev = jnp.where(valid[None], tbl, NEG_INF)[:, ::-1]
    general = jnp.stack([rev[:, Q_BLOCK - 1 - q:3 * Q_BLOCK - 1 - q] for q in range(Q_BLOCK)], axis=1)
    first = jnp.where((np.arange(2 * Q_BLOCK) >= Q_BLOCK)[None, None, :], general, NEG_INF)
    return jnp.stack([first, general])


def _attn_kernel(q_ref, klo_p, klo_c, khi_p, khi_c, vlo_p, vlo_c, vhi_p, vhi_c, bias_ref, o_ref, lse_ref):
    lane = lax.broadcasted_iota(jnp.int32, (Q_BLOCK, LANES), 1)
    low_half = lane < HEAD_DIM
    lse_tile = jnp.zeros((Q_BLOCK, LANES), F32)
    nt = (((1,), (1,)), ((), ()))

    def softmax_part(s, h):
        s = s + bias_ref[h]
        m = jnp.max(s, axis=-1, keepdims=True)
        p = jnp.exp(s - m)
        l = jnp.sum(p, axis=-1, keepdims=True)
        return p.astype(BF16), l, m + jnp.log(l)

    for c in range(KV_HEADS):
        cl = slice(c * LANES, (c + 1) * LANES)
        klo = jnp.concatenate([klo_p[:, cl], klo_c[:, cl]], axis=0)
        khi = jnp.concatenate([khi_p[:, cl], khi_c[:, cl]], axis=0)
        vcat = jnp.concatenate([vlo_p[:, cl], vlo_c[:, cl], vhi_p[:, cl], vhi_c[:, cl]], axis=0)
        ga, gb = 2 * c, 2 * c + 1
        lhs = jnp.concatenate([q_ref[:, ga * LANES:(ga + 1) * LANES], q_ref[:, gb * LANES:(gb + 1) * LANES]], axis=0)
        s_lo = lax.dot_general(lhs, klo, nt, preferred_element_type=F32)
        s_hi = lax.dot_general(lhs, khi, nt, preferred_element_type=F32)
        parts = [softmax_part(s_lo[:Q_BLOCK], 4 * c), softmax_part(s_hi[:Q_BLOCK], 4 * c + 1),
                 softmax_part(s_lo[Q_BLOCK:], 4 * c + 2), softmax_part(s_hi[Q_BLOCK:], 4 * c + 3)]
        for j, g in enumerate((ga, gb)):
            (p0, l0, e0), (p1, l1, e1) = parts[2 * j], parts[2 * j + 1]
            o = jnp.dot(jnp.concatenate([p0, p1], axis=1), vcat, preferred_element_type=F32)
            o_ref[:, g * LANES:(g + 1) * LANES] = o / jnp.where(low_half, l0, l1)
            lse_tile = jnp.where(lane == 4 * c + 2 * j, e0, lse_tile)
            lse_tile = jnp.where(lane == 4 * c + 2 * j + 1, e1, lse_tile)
    lse_ref[...] = lse_tile


def _attn_group(q, kvp, bias, nb, seq_len, group):
    _, dil = DILATION_GROUPS[group]
    tsub = seq_len // dil
    nblk = tsub // Q_BLOCK
    q3 = q.reshape(nb, tsub, dil * N_GROUPS * D_MODEL)
    kv3 = kvp.reshape(nb, tsub, dil * 4 * KV_HEADS * LANES)
    kvw = KV_HEADS * LANES

    def kv_spec(part, prev):
        if prev:
            return pl.BlockSpec((None, Q_BLOCK, kvw), lambda i, b, r: (b, jnp.maximum(i - 1, 0), 4 * r + part))
        return pl.BlockSpec((None, Q_BLOCK, kvw), lambda i, b, r: (b, i, 4 * r + part))

    o, lse = pl.pallas_call(
        _attn_kernel,
        grid=(nblk, nb, dil),
        in_specs=[pl.BlockSpec((None, Q_BLOCK, D_MODEL), lambda i, b, r: (b, i, N_GROUPS * r + group))]
                 + [kv_spec(part, prev) for part in range(4) for prev in (True, False)]
                 + [pl.BlockSpec((None, N_HEADS, Q_BLOCK, 2 * Q_BLOCK), lambda i, b, r: (jnp.minimum(i, 1), 0, 0, 0))],
        out_specs=[pl.BlockSpec((None, Q_BLOCK, D_MODEL), lambda i, b, r: (b, i, r)),
                   pl.BlockSpec((None, Q_BLOCK, LANES), lambda i, b, r: (b, i, r))],
        out_shape=[jax.ShapeDtypeStruct((nb, tsub, dil * D_MODEL), F32),
                   jax.ShapeDtypeStruct((nb, tsub, dil * LANES), F32)],
        compiler_params=_params("arbitrary", "arbitrary", "arbitrary"),
        name=f"attn_g{group}",
    )(q3, kv3, kv3, kv3, kv3, kv3, kv3, kv3, kv3, bias)
    return o.reshape(nb * seq_len, D_MODEL), lse.reshape(nb * seq_len, LANES)


def _attn_out_kernel(o0_ref, o1_ref, o2_ref, l0_ref, l1_ref, l2_ref, h_ref, wo_ref, et_ref, out_ref):
    l0, l1, l2 = l0_ref[...], l1_ref[...], l2_ref[...]
    m = jnp.maximum(jnp.maximum(l0, l1), l2)
    w0, w1, w2 = jnp.exp(l0 - m), jnp.exp(l1 - m), jnp.exp(l2 - m)
    inv = 1.0 / (w0 + w1 + w2)
    et = et_ref[...]
    att = (_dot_split(w0 * inv, et) * o0_ref[...] + _dot_split(w1 * inv, et) * o1_ref[...]
           + _dot_split(w2 * inv, et) * o2_ref[...])
    out_ref[...] = h_ref[...] + _bdot(att, wo_ref[...])


def _attn_out(outs, lses, h, wo, et):
    n = h.shape[0]
    tm = _tile(n)
    row_spec = pl.BlockSpec((tm, D_MODEL), lambda i: (i, 0))
    lse_spec = pl.BlockSpec((tm, LANES), lambda i: (i, 0))
    return pl.pallas_call(
        _attn_out_kernel,
        grid=(n // tm,),
        in_specs=[row_spec] * 3 + [lse_spec] * 3 + [row_spec, _const_spec(wo.shape), _const_spec(et.shape)],
        out_specs=row_spec,
        out_shape=jax.ShapeDtypeStruct((n, D_MODEL), F32),
        compiler_params=_params("arbitrary"),
        name="attn_out",
    )(*outs, *lses, h, wo, et)


def _decode_bias(rel_bias, seq_len, cache_len):
    ncol = cache_len + LANES
    dist = np.arange(-(LANES - 1), cache_len + seq_len)
    buckets = _t5_buckets(np.clip(dist, 0, MAX_WINDOW))
    tabs = []
    for g, (win, dil) in enumerate(DILATION_GROUPS):
        valid = (dist >= 0) & (dist % dil == 0) & (dist <= win)
        tbl = jnp.take(rel_bias[:, g * N_HEADS:(g + 1) * N_HEADS].astype(F32), buckets, axis=0).T
        tabs.append(jnp.where(valid[None], tbl, NEG_INF))
    rev = jnp.stack(tabs)[..., ::-1]
    rows = jnp.stack([rev[..., seq_len - 1 - t:seq_len - 1 - t + ncol] for t in range(seq_len)], axis=2)
    bias = rows.reshape(N_GROUPS * N_HEADS * seq_len, ncol)
    return bias[:, :cache_len], bias[:, cache_len:]


def _attn_decode_kernel(q_ref, cache_ref, kvn_ref, h_ref, bc_ref, bn_ref, wo_ref, out_ref, *, seq_len):
    kvd = KV_HEADS * HEAD_DIM
    nslot = N_GROUPS * N_HEADS
    rows_g = N_HEADS * seq_len
    nt = (((1,), (1,)), ((), ()))
    cache = cache_ref[...]
    kc = cache[:, :kvd].astype(BF16)
    vc = cache[:, kvd:].astype(BF16)
    kvn = kvn_ref[...]
    pad = jnp.zeros((LANES - seq_len, kvd), F32)
    kn = jnp.concatenate([kvn[:, :kvd], pad], axis=0).astype(BF16)
    vn = jnp.concatenate([kvn[:, kvd:], pad], axis=0).astype(BF16)
    lhs = jnp.concatenate([q_ref[:, s * kvd:(s + 1) * kvd] for s in range(nslot)], axis=0).astype(BF16)
    sc = lax.dot_general(lhs, kc, nt, preferred_element_type=F32) + bc_ref[...]
    sn = lax.dot_general(lhs, kn, nt, preferred_element_type=F32) + bn_ref[...]
    m_g = jnp.maximum(jnp.max(sc, axis=-1, keepdims=True), jnp.max(sn, axis=-1, keepdims=True))
    m = jnp.maximum(jnp.maximum(m_g[:rows_g], m_g[rows_g:2 * rows_g]), m_g[2 * rows_g:])
    m3 = jnp.concatenate([m, m, m], axis=0)
    pc = jnp.exp(sc - m3)
    pn = jnp.exp(sn - m3)
    l_g = jnp.sum(pc, axis=-1, keepdims=True) + jnp.sum(pn, axis=-1, keepdims=True)
    num_g = (jnp.dot(pc.astype(BF16), vc, preferred_element_type=F32)
             + jnp.dot(pn.astype(BF16), vn, preferred_element_type=F32))
    l = l_g[:rows_g] + l_g[rows_g:2 * rows_g] + l_g[2 * rows_g:]
    num = num_g[:rows_g] + num_g[rows_g:2 * rows_g] + num_g[2 * rows_g:]
    row = lax.broadcasted_iota(jnp.int32, (rows_g, kvd), 0)
    lane = lax.broadcasted_iota(jnp.int32, (rows_g, kvd), 1)
    own = (row // (Q_PER_KV * seq_len)) == (lane // HEAD_DIM)
    att = jnp.where(own, num / l, 0.0)
    out = h_ref[...]
    for r in range(Q_PER_KV):
        a_r = att[r * seq_len:(r + 1) * seq_len]
        for c in range(1, KV_HEADS):
            a_r = a_r + att[(c * Q_PER_KV + r) * seq_len:(c * Q_PER_KV + r + 1) * seq_len]
        out = out + _bdot(a_r, wo_ref[r])
    out_ref[...] = out


def _attn_decode(q, cache, kv_new, h, bias_c, bias_n, wo_r, nb, seq_len):
    cache_len = cache.shape[1]
    qw = q.shape[1]
    kvw = 2 * KV_HEADS * HEAD_DIM
    return pl.pallas_call(
        functools.partial(_attn_decode_kernel, seq_len=seq_len),
        grid=(nb,),
        in_specs=[pl.BlockSpec((seq_len, qw), lambda b: (b, 0)),
                  pl.BlockSpec((None, cache_len, kvw), lambda b: (b, 0, 0)),
                  pl.BlockSpec((seq_len, kvw), lambda b: (b, 0)),
                  pl.BlockSpec((seq_len, D_MODEL), lambda b: (b, 0)),
                  _const_spec(bias_c.shape), _const_spec(bias_n.shape), _const_spec(wo_r.shape)],
        out_specs=pl.BlockSpec((seq_len, D_MODEL), lambda b: (b, 0)),
        out_shape=jax.ShapeDtypeStruct((nb * seq_len, D_MODEL), F32),
        compiler_params=_params("arbitrary"),
        name="attn_decode",
    )(q, cache, kv_new, h, bias_c, bias_n, wo_r)


def _prep_weights(norm_w, ffn1_wi, ffn1_wo, ffn2_wi, ffn2_wo, pe_proj, pe_gate,
                  rwkv_mix, rwkv_wrkv, rwkv_wo, rwkv_w0, rwkv_w1, rwkv_w2, rwkv_a0, rwkv_a1, rwkv_a2,
                  rwkv_g1, rwkv_g2, rwkv_kk, rwkv_ka, rwkv_rk, rwkv_lnx_w, rwkv_lnx_b,
                  attn_wq, attn_wo, kv_norm, w_kv, rel_bias, final_norm):
    def row(v):
        return v.reshape(1, -1).astype(F32)

    def pad_cols(w, n):
        return jnp.pad(w, ((0, 0), (0, n - w.shape[1]))).astype(BF16)

    def pad_rows(w, n):
        return jnp.pad(w, ((0, n - w.shape[0]), (0, 0))).astype(BF16)

    def ffn_w(wi, wo):
        wg = wi[:, :D_FF].reshape(D_MODEL, N_FFN_CHUNKS, FFN_CHUNK).transpose(1, 0, 2).astype(BF16)
        wu = wi[:, D_FF:].reshape(D_MODEL, N_FFN_CHUNKS, FFN_CHUNK).transpose(1, 0, 2).astype(BF16)
        return wg, wu, wo.reshape(N_FFN_CHUNKS, FFN_CHUNK, D_MODEL).astype(BF16)

    head_of_lane = np.arange(D_MODEL) // HEAD_DIM
    e = jnp.asarray(head_of_lane[:, None] == np.arange(LANES)[None, :], BF16)
    et = jnp.asarray(np.arange(LANES)[:, None] == head_of_lane[None, :], BF16)

    depth = norm_w.shape[0]
    layers = []
    for i in range(depth):
        layers.append(dict(
            nw=[row(norm_w[i, j]) for j in range(4)],
            ffn1=ffn_w(ffn1_wi[i], ffn1_wo[i]), ffn2=ffn_w(ffn2_wi[i], ffn2_wo[i]),
            pe_gate=pe_gate[i].astype(BF16), pe_proj=pe_proj[i].astype(BF16)))
    n_a = depth // 2
    rw = []
    for i in range(n_a):
        rw.append(dict(
            nw=row(norm_w[i, 1]), mix=rwkv_mix[i].astype(F32), wrkv=rwkv_wrkv[i].astype(BF16),
            w0=row(rwkv_w0[i]), w1=pad_cols(rwkv_w1[i], LANES), w2=pad_rows(rwkv_w2[i], LANES),
            a0=row(rwkv_a0[i]), a1=pad_cols(rwkv_a1[i], LANES), a2=pad_rows(rwkv_a2[i], LANES),
            g1=pad_cols(rwkv_g1[i], 2 * LANES), g2=pad_rows(rwkv_g2[i], 2 * LANES),
            kk=row(rwkv_kk[i]), ka=row(rwkv_ka[i]), rk=row(rwkv_rk[i]),
            lnw=row(rwkv_lnx_w[i]), lnb=row(rwkv_lnx_b[i]), wo=rwkv_wo[i].astype(BF16), e=e, et=et))
    scale = HEAD_DIM ** -0.5
    at = []
    for j in range(depth - n_a):
        wq = attn_wq[j] * scale
        wq6 = wq.reshape(D_MODEL, N_GROUPS, KV_HEADS, Q_PER_KV, HEAD_DIM)
        wq_slots = jnp.einsum("dgcre,cx->dgcrxe", wq6, jnp.eye(KV_HEADS, dtype=wq.dtype))
        wq_slots = wq_slots.reshape(D_MODEL, N_GROUPS * N_HEADS * KV_HEADS * HEAD_DIM).astype(BF16)
        wo_r = attn_wo[j].reshape(KV_HEADS, Q_PER_KV, HEAD_DIM, D_MODEL).transpose(1, 0, 2, 3)
        wo_r = wo_r.reshape(Q_PER_KV, KV_HEADS * HEAD_DIM, D_MODEL).astype(BF16)
        at.append(dict(wq=wq.astype(BF16), wq_slots=wq_slots, wo=attn_wo[j].astype(BF16), wo_r=wo_r))
    wkv4 = w_kv.reshape(D_MODEL, 2, KV_HEADS, HEAD_DIM)
    zeros = jnp.zeros_like(wkv4)
    lo = jnp.concatenate([wkv4, zeros], axis=-1)
    hi = jnp.concatenate([zeros, wkv4], axis=-1)
    w_kv_pad = jnp.stack([lo[:, 0], hi[:, 0], lo[:, 1], hi[:, 1]], axis=1).reshape(D_MODEL, 4 * KV_HEADS * LANES)
    return dict(layers=layers, rwkv=rw, attn=at, kv_norm=row(kv_norm), w_kv=w_kv.astype(BF16),
                w_kv_pad=w_kv_pad.astype(BF16), final_norm=row(final_norm), rel_bias=rel_bias, et=et)


def _state_to_blockdiag(s):
    nb = s.shape[0]
    s5 = s.astype(F32).reshape(nb, N_HEADS // 2, 2, HEAD_DIM, HEAD_DIM)
    bd = jnp.einsum("bphij,hg->bphigj", s5, jnp.eye(2, dtype=F32))
    return bd.reshape(nb, N_HEADS // 2, LANES, LANES)


def _blockdiag_to_state(bd):
    nb = bd.shape[0]
    b6 = bd.reshape(nb, N_HEADS // 2, 2, HEAD_DIM, 2, HEAD_DIM)
    s = jnp.stack([b6[:, :, 0, :, 0, :], b6[:, :, 1, :, 1, :]], axis=2)
    return s.reshape(nb, N_HEADS, HEAD_DIM, HEAD_DIM)


def _pad_time(a, nb, seq_len, padded):
    a3 = a.reshape(nb, seq_len, D_MODEL)
    return jnp.pad(a3, ((0, 0), (0, padded - seq_len), (0, 0))).reshape(nb * padded, D_MODEL)


def _trunk(x, p, wkv0, shift0, cache, w):
    nb, seq_len, _ = x.shape
    n = nb * seq_len
    depth = len(w["layers"])
    n_a = depth // 2
    h = x.reshape(n, D_MODEL).astype(F32)
    wkv_out, shift_out = [], []
    kv_rows = kv_pad = None
    for i in range(depth):
        lw = w["layers"][i]
        if i == n_a:
            kv_rows = _norm_mm(h, w["kv_norm"], w["w_kv"], F32, name="kv_proj")
            if cache is None:
                kv_pad = _norm_mm(h, w["kv_norm"], w["w_kv_pad"], BF16, name="kv_proj_pad")
        h = _ffn(h, lw["nw"][0], *lw["ffn1"])
        if i < n_a:
            rwl = w["rwkv"][i]
            r, lwd, k, v, kk, b, g, sh = _rwkv_proj(h, shift0[i].astype(F32), seq_len, rwl)
            s0 = _state_to_blockdiag(wkv0[i])
            if seq_len % WKV_CHUNK:
                tp = -(-seq_len // WKV_CHUNK) * WKV_CHUNK
                padded = [_pad_time(a, nb, seq_len, tp) for a in (r, lwd, k, v, kk, b)]
                y, st = _wkv(*padded, s0, tp)
                y = y.reshape(nb, tp, D_MODEL)[:, :seq_len].reshape(n, D_MODEL)
            else:
                y, st = _wkv(r, lwd, k, v, kk, b, s0, seq_len)
            h = _rwkv_out(y, r, k, v, g, h, rwl)
            wkv_out.append(_blockdiag_to_state(st).astype(x.dtype))
            shift_out.append(sh.astype(x.dtype))
        else:
            al = w["attn"][i - n_a]
            if cache is None:
                q = _norm_mm(h, lw["nw"][1], al["wq"], BF16, name="q_proj")
                outs, lses = [], []
                for gi in range(N_GROUPS):
                    o, lse = _attn_group(q, kv_pad, _band_bias(w["rel_bias"], gi), nb, seq_len, gi)
                    outs.append(o)
                    lses.append(lse)
                h = _attn_out(outs, lses, h, al["wo"], w["et"])
            else:
                q = _norm_mm(h, lw["nw"][1], al["wq_slots"], F32, col_tile=2048, name="q_proj_slots")
                bias_c, bias_n = _decode_bias(w["rel_bias"], seq_len, cache.shape[1])
                h = _attn_decode(q, cache, kv_rows, h, bias_c, bias_n, al["wo_r"], nb, seq_len)
        h = _ffn(h, lw["nw"][2], *lw["ffn2"])
        h = _pe(h, p[i].reshape(n, PLE_DIM).astype(F32), lw["nw"][3], lw["pe_gate"], lw["pe_proj"],
                w["final_norm"], final=(i == depth - 1))
    y = h.reshape(nb, seq_len, D_MODEL).astype(x.dtype)
    kv_rows = kv_rows.reshape(nb, seq_len, 2, KV_HEADS, HEAD_DIM).astype(x.dtype)
    return y, jnp.stack(wkv_out), jnp.stack(shift_out), kv_rows


def kernel(x_prompt, x_sample, state_wkv, state_shift, cache_kv, p_prompt, p_sample, norm_w, ffn1_wi, ffn1_wo, ffn2_wi, ffn2_wo, pe_proj, pe_gate, rwkv_mix, rwkv_wrkv, rwkv_wo, rwkv_w0, rwkv_w1, rwkv_w2, rwkv_a0, rwkv_a1, rwkv_a2, rwkv_g1, rwkv_g2, rwkv_kk, rwkv_ka, rwkv_rk, rwkv_lnx_w, rwkv_lnx_b, attn_wq, attn_wo, kv_norm, w_kv, rel_bias, final_norm):
    w = _prep_weights(norm_w, ffn1_wi, ffn1_wo, ffn2_wi, ffn2_wo, pe_proj, pe_gate,
                      rwkv_mix, rwkv_wrkv, rwkv_wo, rwkv_w0, rwkv_w1, rwkv_w2, rwkv_a0, rwkv_a1, rwkv_a2,
                      rwkv_g1, rwkv_g2, rwkv_kk, rwkv_ka, rwkv_rk, rwkv_lnx_w, rwkv_lnx_b,
                      attn_wq, attn_wo, kv_norm, w_kv, rel_bias, final_norm)
    n_a = norm_w.shape[0] // 2
    nb, seq_len, _ = x_prompt.shape
    wkv0 = jnp.zeros((n_a, nb, N_HEADS, HEAD_DIM, HEAD_DIM), F32)
    shift0 = jnp.zeros((n_a, nb, D_MODEL), x_prompt.dtype)
    y_p, wkv_p, shift_p, kv_p = _trunk(x_prompt, p_prompt, wkv0, shift0, None, w)
    kv_prompt = kv_p[:, seq_len - min(MAX_WINDOW, seq_len):]
    cache = cache_kv.reshape(cache_kv.shape[0], cache_kv.shape[1], 2 * KV_HEADS * HEAD_DIM).astype(F32)
    y_s, wkv_s, shift_s, kv_s = _trunk(x_sample, p_sample, state_wkv, state_shift, cache, w)
    return (y_p, y_s, wkv_p, shift_p, kv_prompt, wkv_s, shift_s, kv_s)
```

```python
import functools

import numpy as np
import jax
import jax.numpy as jnp
from jax import lax
from jax.experimental import pallas as pl
from jax.experimental.pallas import tpu as pltpu

F32 = jnp.float32
BF16 = jnp.bfloat16

D_MODEL = 1024
D_FF = 2816
PLE_DIM = 256
RMS_EPS = 1e-6
HEAD_DIM = 64
N_HEADS = D_MODEL // HEAD_DIM
LNX_EPS = 64e-5
KV_HEADS = 4
Q_PER_KV = N_HEADS // KV_HEADS
DILATION_GROUPS = ((128, 1), (512, 4), (2048, 16))
N_GROUPS = len(DILATION_GROUPS)
MAX_WINDOW = 2048
REL_BUCKETS = 32
REL_MAX_DIST = 2048
NEG_INF = -1e30

LANES = 128
FFN_CHUNK = 256
N_FFN_CHUNKS = D_FF // FFN_CHUNK
TOKEN_TILE = 256
WKV_CHUNK = 64
Q_BLOCK = 128
VMEM_LIMIT = 56 * 1024 * 1024
EXP_MINUS_HALF = 0.6065306597126334


def _params(*sem):
    return pltpu.CompilerParams(dimension_semantics=sem, vmem_limit_bytes=VMEM_LIMIT)


def _const_spec(shape):
    return pl.BlockSpec(shape, lambda *_: (0,) * len(shape))


def _tile(n, pref=TOKEN_TILE):
    t = min(n, pref)
    while n % t:
        t -= 8
    return t


def _rms(x, g):
    return x * lax.rsqrt(jnp.mean(x * x, axis=-1, keepdims=True) + RMS_EPS) * g


def _bdot(a, b):
    return jnp.dot(a.astype(BF16), b, preferred_element_type=F32)


def _dot_split(x, w):
    hi = x.astype(BF16)
    lo = (x - hi.astype(F32)).astype(BF16)
    return (jnp.dot(hi, w, preferred_element_type=F32) + jnp.dot(lo, w, preferred_element_type=F32))


def _head_sum(x, e_ref, et_ref):
    return _dot_split(_dot_split(x, e_ref[...]), et_ref[...])


def _ffn_kernel(x_ref, g_ref, wg_ref, wu_ref, wo_ref, o_ref):
    x = x_ref[...]
    xn = _rms(x, g_ref[...]).astype(BF16)
    acc = jnp.zeros_like(x)
    for j in range(N_FFN_CHUNKS):
        gate = jnp.dot(xn, wg_ref[j], preferred_element_type=F32)
        up = jnp.dot(xn, wu_ref[j], preferred_element_type=F32)
        act = (gate * jax.nn.sigmoid(gate) * up).astype(BF16)
        acc = acc + jnp.dot(act, wo_ref[j], preferred_element_type=F32)
    o_ref[...] = x + 0.5 * acc


def _ffn(h, g, wg, wu, wo):
    n = h.shape[0]
    tm = _tile(n)
    return pl.pallas_call(
        _ffn_kernel,
        grid=(n // tm,),
        in_specs=[pl.BlockSpec((tm, D_MODEL), lambda i: (i, 0)), _const_spec((1, D_MODEL)),
                  _const_spec(wg.shape), _const_spec(wu.shape), _const_spec(wo.shape)],
        out_specs=pl.BlockSpec((tm, D_MODEL), lambda i: (i, 0)),
        out_shape=jax.ShapeDtypeStruct((n, D_MODEL), F32),
        compiler_params=_params("arbitrary"),
        name="ffn",
    )(h, g, wg, wu, wo)


def _norm_mm_kernel(x_ref, g_ref, w_ref, o_ref):
    o_ref[...] = _bdot(_rms(x_ref[...], g_ref[...]), w_ref[...]).astype(o_ref.dtype)


def _norm_mm(h, g, w, out_dtype, col_tile=None, name="norm_mm"):
    n = h.shape[0]
    tm = _tile(n)
    nout = w.shape[1]
    tn = nout if col_tile is None else col_tile
    return pl.pallas_call(
        _norm_mm_kernel,
        grid=(n // tm, nout // tn),
        in_specs=[pl.BlockSpec((tm, D_MODEL), lambda i, j: (i, 0)), _const_spec((1, D_MODEL)),
                  pl.BlockSpec((D_MODEL, tn), lambda i, j: (0, j))],
        out_specs=pl.BlockSpec((tm, tn), lambda i, j: (i, j)),
        out_shape=jax.ShapeDtypeStruct((n, nout), out_dtype),
        compiler_params=_params("arbitrary", "arbitrary"),
        name=name,
    )(h, g, w)


def _pe_kernel(x_ref, p_ref, g_ref, wgate_ref, wproj_ref, gf_ref, o_ref, *, final):
    x = x_ref[...]
    gate = jax.nn.sigmoid(_bdot(_rms(x, g_ref[...]), wgate_ref[...]))
    y = x + gate * _bdot(p_ref[...], wproj_ref[...])
    o_ref[...] = _rms(y, gf_ref[...]) if final else y


def _pe(h, p, g, wgate, wproj, gfinal, final):
    n = h.shape[0]
    tm = _tile(n)
    return pl.pallas_call(
        functools.partial(_pe_kernel, final=final),
        grid=(n // tm,),
        in_specs=[pl.BlockSpec((tm, D_MODEL), lambda i: (i, 0)), pl.BlockSpec((tm, PLE_DIM), lambda i: (i, 0)),
                  _const_spec((1, D_MODEL)), _const_spec(wgate.shape), _const_spec(wproj.shape),
                  _const_spec((1, D_MODEL))],
        out_specs=pl.BlockSpec((tm, D_MODEL), lambda i: (i, 0)),
        out_shape=jax.ShapeDtypeStruct((n, D_MODEL), F32),
        compiler_params=_params("arbitrary"),
        name="pe_final" if final else "pe",
    )(h, p, g, wgate, wproj, gfinal)


def _rwkv_proj_kernel(h_ref, aux_ref, sh_ref, nw_ref, mix_ref, wrkv_ref, w0_ref, w1_ref, w2_ref,
                      a0_ref, a1_ref, a2_ref, g1_ref, g2_ref, kk_ref, ka_ref, e_ref, et_ref,
                      r_o, lw_o, k_o, v_o, kk_o, b_o, g_o, hn_o, *, within_seq, seq_len, tiles_per_seq):
    i = pl.program_id(0)
    nw = nw_ref[...]
    hn = _rms(h_ref[...], nw)
    tm = hn.shape[0]
    row = lax.broadcasted_iota(jnp.int32, hn.shape, 0)
    rolled = pltpu.roll(hn, 1, 0)
    if within_seq:
        prev_last = _rms(aux_ref[...], nw)[7:8]
        tile_in_seq = lax.rem(jnp.full((1, D_MODEL), i, jnp.int32), tiles_per_seq)
        first = jnp.where(tile_in_seq == 0, sh_ref[...], prev_last)
        x_prev = jnp.where(row == 0, first, rolled)
        hn_o[...] = hn[tm - 1:tm]
    else:
        x_prev = jnp.where(lax.rem(row, seq_len) == 0, aux_ref[...], rolled)
        hn_o[...] = hn
    xx = x_prev - hn
    mix = mix_ref[...]
    xr, xw, xk, xv, xa, xg = (hn + xx * mix[j:j + 1] for j in range(6))
    r = _bdot(xr, wrkv_ref[0])
    k = _bdot(xk, wrkv_ref[1])
    v = _bdot(xv, wrkv_ref[2])
    wl = w0_ref[...] + _bdot(jnp.tanh(_bdot(xw, w1_ref[...])), w2_ref[...])
    lw = -EXP_MINUS_HALF * jax.nn.sigmoid(wl)
    a = jax.nn.sigmoid(a0_ref[...] + _bdot(_bdot(xa, a1_ref[...]), a2_ref[...]))
    g = _bdot(jax.nn.sigmoid(_bdot(xg, g1_ref[...])), g2_ref[...])
    kkv = k * kk_ref[...]
    kk = kkv / jnp.maximum(jnp.sqrt(_head_sum(kkv * kkv, e_ref, et_ref)), 1e-12)
    r_o[...] = r
    lw_o[...] = lw
    k_o[...] = k * (1.0 + (a - 1.0) * ka_ref[...])
    v_o[...] = v
    kk_o[...] = kk
    b_o[...] = kk * a
    g_o[...] = g


def _rwkv_proj(h, shift, seq_len, lw):
    n = h.shape[0]
    nb = n // seq_len
    tm = _tile(n)
    within_seq = seq_len % tm == 0
    row_spec = pl.BlockSpec((tm, D_MODEL), lambda i: (i, 0))
    if within_seq:
        tiles_per_seq = seq_len // tm
        aux = h
        aux_spec = pl.BlockSpec((8, D_MODEL), lambda i: (jnp.maximum(i * (tm // 8) - 1, 0), 0))
        sh = shift.reshape(nb, 1, D_MODEL)
        sh_spec = pl.BlockSpec((None, 1, D_MODEL), lambda i: (i // tiles_per_seq, 0, 0))
        hn_shape = jax.ShapeDtypeStruct((nb, 1, D_MODEL), F32)
        hn_spec = pl.BlockSpec((None, 1, D_MODEL), lambda i: (i // tiles_per_seq, 0, 0))
    else:
        assert tm % seq_len == 0
        tiles_per_seq = 1
        aux = jnp.repeat(shift, seq_len, axis=0)
        aux_spec = row_spec
        sh = shift.reshape(nb, 1, D_MODEL)
        sh_spec = pl.BlockSpec((None, 1, D_MODEL), lambda i: (0, 0, 0))
        hn_shape = jax.ShapeDtypeStruct((n, D_MODEL), F32)
        hn_spec = row_spec
    consts = [lw["nw"], lw["mix"], lw["wrkv"], lw["w0"], lw["w1"], lw["w2"], lw["a0"], lw["a1"], lw["a2"],
              lw["g1"], lw["g2"], lw["kk"], lw["ka"], lw["e"], lw["et"]]
    big = jax.ShapeDtypeStruct((n, D_MODEL), F32)
    outs = pl.pallas_call(
        functools.partial(_rwkv_proj_kernel, within_seq=within_seq, seq_len=seq_len, tiles_per_seq=tiles_per_seq),
        grid=(n // tm,),
        in_specs=[row_spec, aux_spec, sh_spec] + [_const_spec(c.shape) for c in consts],
        out_specs=[row_spec] * 7 + [hn_spec],
        out_shape=[big] * 7 + [hn_shape],
        compiler_params=_params("arbitrary"),
        name="rwkv_proj",
    )(h, aux, sh, *consts)
    r, lwd, k, v, kk, b, g, hn = outs
    shift_out = hn.reshape(nb, D_MODEL) if within_seq else hn.reshape(nb, seq_len, D_MODEL)[:, -1]
    return r, lwd, k, v, kk, b, g, shift_out


def _wkv_kernel(r_ref, lw_ref, k_ref, v_ref, kk_ref, b_ref, s0_ref, y_ref, st_ref, state, *, n_sub):
    c = pl.program_id(1)
    C = WKV_CHUNK
    C2 = 2 * C
    n_pairs = N_HEADS // 2

    @pl.when(c == 0)
    def _():
        state[...] = s0_ref[...]

    ri = lax.broadcasted_iota(jnp.int32, (C, C), 0)
    ci = lax.broadcasted_iota(jnp.int32, (C, C), 1)
    tri = (ri >= ci).astype(BF16)
    r2 = lax.broadcasted_iota(jnp.int32, (C2, C2), 0)
    c2 = lax.broadcasted_iota(jnp.int32, (C2, C2), 1)
    strict = r2 > c2
    incl = r2 >= c2
    lane = lax.broadcasted_iota(jnp.int32, (C, LANES), 1)
    head0 = lane < HEAD_DIM

    def hat(x):
        x3 = jnp.stack([x[:, p * LANES:(p + 1) * LANES] for p in range(n_pairs)])
        return jnp.concatenate([jnp.where(head0, x3, 0.0), jnp.where(head0, 0.0, x3)], axis=1)

    def mm(a, b):
        return jnp.einsum("pmk,pkn->pmn", a.astype(BF16), b.astype(BF16), preferred_element_type=F32)

    def mm_nt(a, b):
        return jnp.einsum("pmk,pnk->pmn", a.astype(BF16), b.astype(BF16), preferred_element_type=F32)

    def sub_chunk(s, carry):
        rows = pl.ds(pl.multiple_of(s * C, C), C)
        lw = lw_ref[rows, :]
        p1 = lw.astype(BF16)
        rem = lw - p1.astype(F32)
        p2 = rem.astype(BF16)
        p3 = (rem - p2.astype(F32)).astype(BF16)
        cw = (jnp.dot(tri, p1, preferred_element_type=F32) + jnp.dot(tri, p2, preferred_element_type=F32)
              + jnp.dot(tri, p3, preferred_element_type=F32))
        cw_end = cw[C - 1:C, :]
        e_neg = jnp.exp(-cw)
        e_end = jnp.exp(cw_end - cw)
        kk = kk_ref[rows, :]
        bb = b_ref[rows, :]
        k = k_ref[rows, :]
        a_all = kk * jnp.exp(cw - lw)
        r_all = r_ref[rows, :] * jnp.exp(cw)
        b_all = bb * e_neg
        k_all = k * e_neg
        bd_all = bb * e_end
        kd_all = k * e_end
        v_all = v_ref[rows, :]
        decay = jnp.exp(cw_end)
        ar_h = jnp.concatenate([hat(a_all), hat(r_all)], axis=1)
        bk_h = jnp.concatenate([hat(b_all), hat(k_all)], axis=1)
        v_h = hat(v_all)
        st = state[...]
        g = mm_nt(ar_h, bk_h)
        low = jnp.where(strict, g[:, :C2, :C2], 0.0)
        ak = jnp.where(strict, g[:, :C2, C2:], 0.0)
        rbk = jnp.concatenate([jnp.where(incl, g[:, C2:, :C2], 0.0), jnp.where(incl, g[:, C2:, C2:], 0.0)], axis=2)
        ss = mm_nt(ar_h, st)
        x = -(ss[:, :C2] + mm(ak, v_h))
        t = mm(low, jnp.concatenate([low, x], axis=2))
        lp = t[:, :, :C2]
        x = x - t[:, :, C2:]
        for _ in range(4):
            t = mm(lp, jnp.concatenate([lp, x], axis=2))
            lp = t[:, :, :C2]
            x = x + t[:, :, C2:]
        x = x + mm(lp, x)
        xv = jnp.concatenate([x, v_h], axis=1)
        y_h = ss[:, C2:] + mm(rbk, xv)
        y = y_h[:, :C] + y_h[:, C:]
        for p in range(n_pairs):
            y_ref[rows, p * LANES:(p + 1) * LANES] = y[p]
        bkd_h = jnp.concatenate([hat(bd_all), hat(kd_all)], axis=1)
        xv_t = jnp.stack([xv[p].T for p in range(n_pairs)])
        dec3 = jnp.stack([decay[:, p * LANES:(p + 1) * LANES] for p in range(n_pairs)])
        state[...] = st * dec3 + mm(xv_t, bkd_h)
        return carry

    lax.fori_loop(0, n_sub, sub_chunk, 0)

    @pl.when(c == pl.num_programs(1) - 1)
    def _():
        st_ref[...] = state[...]


def _wkv(r, lwd, k, v, kk, b, s0_bd, seq_len):
    n = r.shape[0]
    nb = n // seq_len
    ct = _tile(seq_len, 256)
    assert ct % WKV_CHUNK == 0
    steps = seq_len // ct
    row_spec = pl.BlockSpec((ct, D_MODEL), lambda bi, ci: (bi * steps + ci, 0))
    st_spec = pl.BlockSpec((None, N_HEADS // 2, LANES, LANES), lambda bi, ci: (bi, 0, 0, 0))
    return pl.pallas_call(
        functools.partial(_wkv_kernel, n_sub=ct // WKV_CHUNK),
        grid=(nb, steps),
        in_specs=[row_spec] * 6 + [st_spec],
        out_specs=[row_spec, st_spec],
        out_shape=[jax.ShapeDtypeStruct((n, D_MODEL), F32), jax.ShapeDtypeStruct(s0_bd.shape, F32)],
        scratch_shapes=[pltpu.VMEM((N_HEADS // 2, LANES, LANES), F32)],
        compiler_params=_params("arbitrary", "arbitrary"),
        name="wkv",
    )(r, lwd, k, v, kk, b, s0_bd)


def _rwkv_out_kernel(y_ref, r_ref, k_ref, v_ref, g_ref, h_ref, lnw_ref, lnb_ref, rk_ref, wo_ref, e_ref, et_ref, o_ref):
    y = y_ref[...]
    inv_n = 1.0 / HEAD_DIM
    mu = _head_sum(y, e_ref, et_ref) * inv_n
    yc = y - mu
    var = _head_sum(yc * yc, e_ref, et_ref) * inv_n
    yn = yc * lax.rsqrt(var + LNX_EPS) * lnw_ref[...] + lnb_ref[...]
    bonus = _head_sum(r_ref[...] * k_ref[...] * rk_ref[...], e_ref, et_ref) * v_ref[...]
    o_ref[...] = h_ref[...] + _bdot((yn + bonus) * g_ref[...], wo_ref[...])


def _rwkv_out(y, r, k, v, g, h, lw):
    n = h.shape[0]
    tm = _tile(n)
    row_spec = pl.BlockSpec((tm, D_MODEL), lambda i: (i, 0))
    consts = [lw["lnw"], lw["lnb"], lw["rk"], lw["wo"], lw["e"], lw["et"]]
    return pl.pallas_call(
        _rwkv_out_kernel,
        grid=(n // tm,),
        in_specs=[row_spec] * 6 + [_const_spec(c.shape) for c in consts],
        out_specs=row_spec,
        out_shape=jax.ShapeDtypeStruct((n, D_MODEL), F32),
        compiler_params=_params("arbitrary"),
        name="rwkv_out",
    )(y, r, k, v, g, h, *consts)


def _t5_buckets(dist):
    d = np.asarray(dist, dtype=np.int64)
    max_exact = REL_BUCKETS // 2
    large = max_exact + (np.log(np.maximum(d, 1) / max_exact) / np.log(REL_MAX_DIST / max_exact)
                         * (REL_BUCKETS - max_exact)).astype(np.int32)
    large = np.minimum(large, REL_BUCKETS - 1)
    return np.where(d < max_exact, d, large).astype(np.int32)


def _toeplitz(tab, n_rows, n_cols):
    width = n_rows + n_cols - 1
    lead = tab.shape[:-1]
    padded = jnp.pad(tab[..., :width], [(0, 0)] * len(lead) + [(0, 1)])
    flat = jnp.broadcast_to(padded[..., None, :], lead + (n_rows + 1, width + 1)).reshape(lead + (-1,))
    skew = flat[..., :n_rows * (width + 2)].reshape(lead + (n_rows, width + 2))
    return skew[..., :n_cols]


def _band_bias(rel_bias, group):
    win, dil = DILATION_GROUPS[group]
    assert win // dil == Q_BLOCK
    m = np.arange(-(Q_BLOCK - 1), 2 * Q_BLOCK)
    valid = (m >= 0) & (m <= Q_BLOCK)
    buckets = _t5_buckets(dil * np.clip(m, 0, Q_BLOCK))
    tbl = jnp.take(rel_bias[:, group * N_HEADS:(group + 1) * N_HEADS].astype(F32), buckets, axis=0).T
    rev = jnp.where(valid[None], tbl, NEG_INF)[:, ::-1]
    general = _toeplitz(rev, Q_BLOCK, 2 * Q_BLOCK)[:, ::-1, :]
    first = jnp.where((np.arange(2 * Q_BLOCK) >= Q_BLOCK)[None, None, :], general, NEG_INF)
    return jnp.stack([first, general])


def _attn_kernel(q_ref, klo_p, klo_c, khi_p, khi_c, vlo_p, vlo_c, vhi_p, vhi_c, bias_ref, o_ref, lse_ref):
    lane = lax.broadcasted_iota(jnp.int32, (Q_BLOCK, LANES), 1)
    low_half = lane < HEAD_DIM
    lse_tile = jnp.zeros((Q_BLOCK, LANES), F32)
    nt = (((1,), (1,)), ((), ()))

    def softmax_part(s, h):
        s = s + bias_ref[h]
        m = jnp.max(s, axis=-1, keepdims=True)
        p = jnp.exp(s - m)
        l = jnp.sum(p, axis=-1, keepdims=True)
        return p.astype(BF16), l, m + jnp.log(l)

    for c in range(KV_HEADS):
        cl = slice(c * LANES, (c + 1) * LANES)
        klo = jnp.concatenate([klo_p[:, cl], klo_c[:, cl]], axis=0)
        khi = jnp.concatenate([khi_p[:, cl], khi_c[:, cl]], axis=0)
        vcat = jnp.concatenate([vlo_p[:, cl], vlo_c[:, cl], vhi_p[:, cl], vhi_c[:, cl]], axis=0)
        ga, gb = 2 * c, 2 * c + 1
        lhs = jnp.concatenate([q_ref[:, ga * LANES:(ga + 1) * LANES], q_ref[:, gb * LANES:(gb + 1) * LANES]], axis=0)
        s_lo = lax.dot_general(lhs, klo, nt, preferred_element_type=F32)
        s_hi = lax.dot_general(lhs, khi, nt, preferred_element_type=F32)
        parts = [softmax_part(s_lo[:Q_BLOCK], 4 * c), softmax_part(s_hi[:Q_BLOCK], 4 * c + 1),
                 softmax_part(s_lo[Q_BLOCK:], 4 * c + 2), softmax_part(s_hi[Q_BLOCK:], 4 * c + 3)]
        for j, g in enumerate((ga, gb)):
            (p0, l0, e0), (p1, l1, e1) = parts[2 * j], parts[2 * j + 1]
            o = jnp.dot(jnp.concatenate([p0, p1], axis=1), vcat, preferred_element_type=F32)
            o_ref[:, g * LANES:(g + 1) * LANES] = o / jnp.where(low_half, l0, l1)
            lse_tile = jnp.where(lane == 4 * c + 2 * j, e0, lse_tile)
            lse_tile = jnp.where(lane == 4 * c + 2 * j + 1, e1, lse_tile)
    lse_ref[...] = lse_tile


def _attn_group(q, kvp, bias, nb, seq_len, group):
    _, dil = DILATION_GROUPS[group]
    tsub = seq_len // dil
    nblk = tsub // Q_BLOCK
    q3 = q.reshape(nb, tsub, dil * N_GROUPS * D_MODEL)
    kv3 = kvp.reshape(nb, tsub, dil * 4 * KV_HEADS * LANES)
    kvw = KV_HEADS * LANES

    def kv_spec(part, prev):
        if prev:
            return pl.BlockSpec((None, Q_BLOCK, kvw), lambda i, b, r: (b, jnp.maximum(i - 1, 0), 4 * r + part))
        return pl.BlockSpec((None, Q_BLOCK, kvw), lambda i, b, r: (b, i, 4 * r + part))

    o, lse = pl.pallas_call(
        _attn_kernel,
        grid=(nblk, nb, dil),
        in_specs=[pl.BlockSpec((None, Q_BLOCK, D_MODEL), lambda i, b, r: (b, i, N_GROUPS * r + group))]
                 + [kv_spec(part, prev) for part in range(4) for prev in (True, False)]
                 + [pl.BlockSpec((None, N_HEADS, Q_BLOCK, 2 * Q_BLOCK), lambda i, b, r: (jnp.minimum(i, 1), 0, 0, 0))],
        out_specs=[pl.BlockSpec((None, Q_BLOCK, D_MODEL), lambda i, b, r: (b, i, r)),
                   pl.BlockSpec((None, Q_BLOCK, LANES), lambda i, b, r: (b, i, r))],
        out_shape=[jax.ShapeDtypeStruct((nb, tsub, dil * D_MODEL), F32),
                   jax.ShapeDtypeStruct((nb, tsub, dil * LANES), F32)],
        compiler_params=_params("arbitrary", "arbitrary", "arbitrary"),
        name=f"attn_g{group}",
    )(q3, kv3, kv3, kv3, kv3, kv3, kv3, kv3, kv3, bias)
    return o.reshape(nb * seq_len, D_MODEL), lse.reshape(nb * seq_len, LANES)


def _attn_out_kernel(o0_ref, o1_ref, o2_ref, l0_ref, l1_ref, l2_ref, h_ref, wo_ref, et_ref, out_ref):
    l0, l1, l2 = l0_ref[...], l1_ref[...], l2_ref[...]
    m = jnp.maximum(jnp.maximum(l0, l1), l2)
    w0, w1, w2 = jnp.exp(l0 - m), jnp.exp(l1 - m), jnp.exp(l2 - m)
    inv = 1.0 / (w0 + w1 + w2)
    et = et_ref[...]
    att = (_dot_split(w0 * inv, et) * o0_ref[...] + _dot_split(w1 * inv, et) * o1_ref[...]
           + _dot_split(w2 * inv, et) * o2_ref[...])
    out_ref[...] = h_ref[...] + _bdot(att, wo_ref[...])


def _attn_out(outs, lses, h, wo, et):
    n = h.shape[0]
    tm = _tile(n)
    row_spec = pl.BlockSpec((tm, D_MODEL), lambda i: (i, 0))
    lse_spec = pl.BlockSpec((tm, LANES), lambda i: (i, 0))
    return pl.pallas_call(
        _attn_out_kernel,
        grid=(n // tm,),
        in_specs=[row_spec] * 3 + [lse_spec] * 3 + [row_spec, _const_spec(wo.shape), _const_spec(et.shape)],
        out_specs=row_spec,
        out_shape=jax.ShapeDtypeStruct((n, D_MODEL), F32),
        compiler_params=_params("arbitrary"),
        name="attn_out",
    )(*outs, *lses, h, wo, et)


def _decode_bias(rel_bias, seq_len, cache_len):
    ncol = cache_len + LANES
    dist = np.arange(-(LANES - 1), cache_len + seq_len)
    buckets = _t5_buckets(np.clip(dist, 0, MAX_WINDOW))
    tabs = []
    for g, (win, dil) in enumerate(DILATION_GROUPS):
        valid = (dist >= 0) & (dist % dil == 0) & (dist <= win)
        tbl = jnp.take(rel_bias[:, g * N_HEADS:(g + 1) * N_HEADS].astype(F32), buckets, axis=0).T
        tabs.append(jnp.where(valid[None], tbl, NEG_INF))
    rev = jnp.stack(tabs)[..., ::-1]
    rows = _toeplitz(rev, seq_len, ncol)[..., ::-1, :]
    bias = rows.reshape(N_GROUPS * N_HEADS * seq_len, ncol)
    return bias[:, :cache_len], bias[:, cache_len:]


def _attn_decode_kernel(q_ref, cache_ref, kvn_ref, h_ref, bc_ref, bn_ref, wo_ref, out_ref, *, seq_len):
    kvd = KV_HEADS * HEAD_DIM
    nslot = N_GROUPS * N_HEADS
    rows_g = N_HEADS * seq_len
    nt = (((1,), (1,)), ((), ()))
    cache = cache_ref[...]
    kc = cache[:, :kvd].astype(BF16)
    vc = cache[:, kvd:].astype(BF16)
    kvn = kvn_ref[...]
    pad = jnp.zeros((LANES - seq_len, kvd), F32)
    kn = jnp.concatenate([kvn[:, :kvd], pad], axis=0).astype(BF16)
    vn = jnp.concatenate([kvn[:, kvd:], pad], axis=0).astype(BF16)
    lhs = jnp.concatenate([q_ref[:, s * kvd:(s + 1) * kvd] for s in range(nslot)], axis=0).astype(BF16)
    sc = lax.dot_general(lhs, kc, nt, preferred_element_type=F32) + bc_ref[...]
    sn = lax.dot_general(lhs, kn, nt, preferred_element_type=F32) + bn_ref[...]
    m_g = jnp.maximum(jnp.max(sc, axis=-1, keepdims=True), jnp.max(sn, axis=-1, keepdims=True))
    m = jnp.maximum(jnp.maximum(m_g[:rows_g], m_g[rows_g:2 * rows_g]), m_g[2 * rows_g:])
    m3 = jnp.concatenate([m, m, m], axis=0)
    pc = jnp.exp(sc - m3)
    pn = jnp.exp(sn - m3)
    l_g = jnp.sum(pc, axis=-1, keepdims=True) + jnp.sum(pn, axis=-1, keepdims=True)
    num_g = (jnp.dot(pc.astype(BF16), vc, preferred_element_type=F32)
             + jnp.dot(pn.astype(BF16), vn, preferred_element_type=F32))
    l = l_g[:rows_g] + l_g[rows_g:2 * rows_g] + l_g[2 * rows_g:]
    num = num_g[:rows_g] + num_g[rows_g:2 * rows_g] + num_g[2 * rows_g:]
    row = lax.broadcasted_iota(jnp.int32, (rows_g, kvd), 0)
    lane = lax.broadcasted_iota(jnp.int32, (rows_g, kvd), 1)
    own = (row // (Q_PER_KV * seq_len)) == (lane // HEAD_DIM)
    att = jnp.where(own, num / l, 0.0)
    out = h_ref[...]
    for r in range(Q_PER_KV):
        a_r = att[r * seq_len:(r + 1) * seq_len]
        for c in range(1, KV_HEADS):
            a_r = a_r + att[(c * Q_PER_KV + r) * seq_len:(c * Q_PER_KV + r + 1) * seq_len]
        out = out + _bdot(a_r, wo_ref[r])
    out_ref[...] = out


def _attn_decode(q, cache, kv_new, h, bias_c, bias_n, wo_r, nb, seq_len):
    cache_len = cache.shape[1]
    qw = q.shape[1]
    kvw = 2 * KV_HEADS * HEAD_DIM
    return pl.pallas_call(
        functools.partial(_attn_decode_kernel, seq_len=seq_len),
        grid=(nb,),
        in_specs=[pl.BlockSpec((seq_len, qw), lambda b: (b, 0)),
                  pl.BlockSpec((None, cache_len, kvw), lambda b: (b, 0, 0)),
                  pl.BlockSpec((seq_len, kvw), lambda b: (b, 0)),
                  pl.BlockSpec((seq_len, D_MODEL), lambda b: (b, 0)),
                  _const_spec(bias_c.shape), _const_spec(bias_n.shape), _const_spec(wo_r.shape)],
        out_specs=pl.BlockSpec((seq_len, D_MODEL), lambda b: (b, 0)),
        out_shape=jax.ShapeDtypeStruct((nb * seq_len, D_MODEL), F32),
        compiler_params=_params("arbitrary"),
        name="attn_decode",
    )(q, cache, kv_new, h, bias_c, bias_n, wo_r)


def _prep_weights(norm_w, ffn1_wi, ffn1_wo, ffn2_wi, ffn2_wo, pe_proj, pe_gate,
                  rwkv_mix, rwkv_wrkv, rwkv_wo, rwkv_w0, rwkv_w1, rwkv_w2, rwkv_a0, rwkv_a1, rwkv_a2,
                  rwkv_g1, rwkv_g2, rwkv_kk, rwkv_ka, rwkv_rk, rwkv_lnx_w, rwkv_lnx_b,
                  attn_wq, attn_wo, kv_norm, w_kv, rel_bias, final_norm):
    def row(v):
        return v.reshape(1, -1).astype(F32)

    def pad_cols(w, n):
        return jnp.pad(w, ((0, 0), (0, n - w.shape[1]))).astype(BF16)

    def pad_rows(w, n):
        return jnp.pad(w, ((0, n - w.shape[0]), (0, 0))).astype(BF16)

    def ffn_w(wi, wo):
        wg = wi[:, :D_FF].reshape(D_MODEL, N_FFN_CHUNKS, FFN_CHUNK).transpose(1, 0, 2).astype(BF16)
        wu = wi[:, D_FF:].reshape(D_MODEL, N_FFN_CHUNKS, FFN_CHUNK).transpose(1, 0, 2).astype(BF16)
        return wg, wu, wo.reshape(N_FFN_CHUNKS, FFN_CHUNK, D_MODEL).astype(BF16)

    head_of_lane = np.arange(D_MODEL) // HEAD_DIM
    e = jnp.asarray(head_of_lane[:, None] == np.arange(LANES)[None, :], BF16)
    et = jnp.asarray(np.arange(LANES)[:, None] == head_of_lane[None, :], BF16)

    depth = norm_w.shape[0]
    layers = []
    for i in range(depth):
        layers.append(dict(
            nw=[row(norm_w[i, j]) for j in range(4)],
            ffn1=ffn_w(ffn1_wi[i], ffn1_wo[i]), ffn2=ffn_w(ffn2_wi[i], ffn2_wo[i]),
            pe_gate=pe_gate[i].astype(BF16), pe_proj=pe_proj[i].astype(BF16)))
    n_a = depth // 2
    rw = []
    for i in range(n_a):
        rw.append(dict(
            nw=row(norm_w[i, 1]), mix=rwkv_mix[i].astype(F32), wrkv=rwkv_wrkv[i].astype(BF16),
            w0=row(rwkv_w0[i]), w1=pad_cols(rwkv_w1[i], LANES), w2=pad_rows(rwkv_w2[i], LANES),
            a0=row(rwkv_a0[i]), a1=pad_cols(rwkv_a1[i], LANES), a2=pad_rows(rwkv_a2[i], LANES),
            g1=pad_cols(rwkv_g1[i], 2 * LANES), g2=pad_rows(rwkv_g2[i], 2 * LANES),
            kk=row(rwkv_kk[i]), ka=row(rwkv_ka[i]), rk=row(rwkv_rk[i]),
            lnw=row(rwkv_lnx_w[i]), lnb=row(rwkv_lnx_b[i]), wo=rwkv_wo[i].astype(BF16), e=e, et=et))
    scale = HEAD_DIM ** -0.5
    at = []
    for j in range(depth - n_a):
        wq = attn_wq[j] * scale
        wq6 = wq.reshape(D_MODEL, N_GROUPS, KV_HEADS, Q_PER_KV, HEAD_DIM)
        wq_slots = jnp.einsum("dgcre,cx->dgcrxe", wq6, jnp.eye(KV_HEADS, dtype=wq.dtype))
        wq_slots = wq_slots.reshape(D_MODEL, N_GROUPS * N_HEADS * KV_HEADS * HEAD_DIM).astype(BF16)
        wo_r = attn_wo[j].reshape(KV_HEADS, Q_PER_KV, HEAD_DIM, D_MODEL).transpose(1, 0, 2, 3)
        wo_r = wo_r.reshape(Q_PER_KV, KV_HEADS * HEAD_DIM, D_MODEL).astype(BF16)
        at.append(dict(wq=wq.astype(BF16), wq_slots=wq_slots, wo=attn_wo[j].astype(BF16), wo_r=wo_r))
    wkv4 = w_kv.reshape(D_MODEL, 2, KV_HEADS, HEAD_DIM)
    zeros = jnp.zeros_like(wkv4)
    lo = jnp.concatenate([wkv4, zeros], axis=-1)
    hi = jnp.concatenate([zeros, wkv4], axis=-1)
    w_kv_pad = jnp.stack([lo[:, 0], hi[:, 0], lo[:, 1], hi[:, 1]], axis=1).reshape(D_MODEL, 4 * KV_HEADS * LANES)
    return dict(layers=layers, rwkv=rw, attn=at, kv_norm=row(kv_norm), w_kv=w_kv.astype(BF16),
                w_kv_pad=w_kv_pad.astype(BF16), final_norm=row(final_norm), rel_bias=rel_bias, et=et)


def _state_to_blockdiag(s):
    nb = s.shape[0]
    s5 = s.astype(F32).reshape(nb, N_HEADS // 2, 2, HEAD_DIM, HEAD_DIM)
    bd = jnp.einsum("bphij,hg->bphigj", s5, jnp.eye(2, dtype=F32))
    return bd.reshape(nb, N_HEADS // 2, LANES, LANES)


def _blockdiag_to_state(bd):
    nb = bd.shape[0]
    b6 = bd.reshape(nb, N_HEADS // 2, 2, HEAD_DIM, 2, HEAD_DIM)
    s = jnp.stack([b6[:, :, 0, :, 0, :], b6[:, :, 1, :, 1, :]], axis=2)
    return s.reshape(nb, N_HEADS, HEAD_DIM, HEAD_DIM)


def _pad_time(a, nb, seq_len, padded):
    a3 = a.reshape(nb, seq_len, D_MODEL)
    return jnp.pad(a3, ((0, 0), (0, padded - seq_len), (0, 0))).reshape(nb * padded, D_MODEL)


def _trunk(x, p, wkv0, shift0, cache, w):
    nb, seq_len, _ = x.shape
    n = nb * seq_len
    depth = len(w["layers"])
    n_a = depth // 2
    h = x.reshape(n, D_MODEL).astype(F32)
    wkv_out, shift_out = [], []
    kv_rows = kv_pad = None
    for i in range(depth):
        lw = w["layers"][i]
        if i == n_a:
            kv_rows = _norm_mm(h, w["kv_norm"], w["w_kv"], F32, name="kv_proj")
            if cache is None:
                kv_pad = _norm_mm(h, w["kv_norm"], w["w_kv_pad"], BF16, name="kv_proj_pad")
        h = _ffn(h, lw["nw"][0], *lw["ffn1"])
        if i < n_a:
            rwl = w["rwkv"][i]
            r, lwd, k, v, kk, b, g, sh = _rwkv_proj(h, shift0[i].astype(F32), seq_len, rwl)
            s0 = _state_to_blockdiag(wkv0[i])
            if seq_len % WKV_CHUNK:
                tp = -(-seq_len // WKV_CHUNK) * WKV_CHUNK
                padded = [_pad_time(a, nb, seq_len, tp) for a in (r, lwd, k, v, kk, b)]
                y, st = _wkv(*padded, s0, tp)
                y = y.reshape(nb, tp, D_MODEL)[:, :seq_len].reshape(n, D_MODEL)
            else:
                y, st = _wkv(r, lwd, k, v, kk, b, s0, seq_len)
            h = _rwkv_out(y, r, k, v, g, h, rwl)
            wkv_out.append(_blockdiag_to_state(st).astype(x.dtype))
            shift_out.append(sh.astype(x.dtype))
        else:
            al = w["attn"][i - n_a]
            if cache is None:
                q = _norm_mm(h, lw["nw"][1], al["wq"], BF16, name="q_proj")
                outs, lses = [], []
                for gi in range(N_GROUPS):
                    o, lse = _attn_group(q, kv_pad, _band_bias(w["rel_bias"], gi), nb, seq_len, gi)
                    outs.append(o)
                    lses.append(lse)
                h = _attn_out(outs, lses, h, al["wo"], w["et"])
            else:
                q = _norm_mm(h, lw["nw"][1], al["wq_slots"], F32, col_tile=2048, name="q_proj_slots")
                bias_c, bias_n = _decode_bias(w["rel_bias"], seq_len, cache.shape[1])
                h = _attn_decode(q, cache, kv_rows, h, bias_c, bias_n, al["wo_r"], nb, seq_len)
        h = _ffn(h, lw["nw"][2], *lw["ffn2"])
        h = _pe(h, p[i].reshape(n, PLE_DIM).astype(F32), lw["nw"][3], lw["pe_gate"], lw["pe_proj"],
                w["final_norm"], final=(i == depth - 1))
    y = h.reshape(nb, seq_len, D_MODEL).astype(x.dtype)
    kv_rows = kv_rows.reshape(nb, seq_len, 2, KV_HEADS, HEAD_DIM).astype(x.dtype)
    return y, jnp.stack(wkv_out), jnp.stack(shift_out), kv_rows


def kernel(x_prompt, x_sample, state_wkv, state_shift, cache_kv, p_prompt, p_sample, norm_w, ffn1_wi, ffn1_wo, ffn2_wi, ffn2_wo, pe_proj, pe_gate, rwkv_mix, rwkv_wrkv, rwkv_wo, rwkv_w0, rwkv_w1, rwkv_w2, rwkv_a0, rwkv_a1, rwkv_a2, rwkv_g1, rwkv_g2, rwkv_kk, rwkv_ka, rwkv_rk, rwkv_lnx_w, rwkv_lnx_b, attn_wq, attn_wo, kv_norm, w_kv, rel_bias, final_norm):
    w = _prep_weights(norm_w, ffn1_wi, ffn1_wo, ffn2_wi, ffn2_wo, pe_proj, pe_gate,
                      rwkv_mix, rwkv_wrkv, rwkv_wo, rwkv_w0, rwkv_w1, rwkv_w2, rwkv_a0, rwkv_a1, rwkv_a2,
                      rwkv_g1, rwkv_g2, rwkv_kk, rwkv_ka, rwkv_rk, rwkv_lnx_w, rwkv_lnx_b,
                      attn_wq, attn_wo, kv_norm, w_kv, rel_bias, final_norm)
    n_a = norm_w.shape[0] // 2
    nb, seq_len, _ = x_prompt.shape
    wkv0 = jnp.zeros((n_a, nb, N_HEADS, HEAD_DIM, HEAD_DIM), F32)
    shift0 = jnp.zeros((n_a, nb, D_MODEL), x_prompt.dtype)
    y_p, wkv_p, shift_p, kv_p = _trunk(x_prompt, p_prompt, wkv0, shift0, None, w)
    kv_prompt = kv_p[:, seq_len - min(MAX_WINDOW, seq_len):]
    cache = cache_kv.reshape(cache_kv.shape[0], cache_kv.shape[1], 2 * KV_HEADS * HEAD_DIM).astype(F32)
    y_s, wkv_s, shift_s, kv_s = _trunk(x_sample, p_sample, state_wkv, state_shift, cache, w)
    return (y_p, y_s, wkv_p, shift_p, kv_prompt, wkv_s, shift_s, kv_s)
```

```python
import functools

import numpy as np
import jax
import jax.numpy as jnp
from jax import lax
from jax.experimental import pallas as pl
from jax.experimental.pallas import tpu as pltpu

F32 = jnp.float32
BF16 = jnp.bfloat16

D_MODEL = 1024
D_FF = 2816
PLE_DIM = 256
RMS_EPS = 1e-6
HEAD_DIM = 64
N_HEADS = D_MODEL // HEAD_DIM
LNX_EPS = 64e-5
KV_HEADS = 4
Q_PER_KV = N_HEADS // KV_HEADS
DILATION_GROUPS = ((128, 1), (512, 4), (2048, 16))
N_GROUPS = len(DILATION_GROUPS)
MAX_WINDOW = 2048
REL_BUCKETS = 32
REL_MAX_DIST = 2048
NEG_INF = -1e30

LANES = 128
FFN_CHUNK = 256
N_FFN_CHUNKS = D_FF // FFN_CHUNK
TOKEN_TILE = 256
WKV_CHUNK = 64
Q_BLOCK = 128
ATTN_SPAN = 2048
VMEM_LIMIT = 56 * 1024 * 1024
EXP_MINUS_HALF = 0.6065306597126334


def _params(*sem):
    return pltpu.CompilerParams(dimension_semantics=sem, vmem_limit_bytes=VMEM_LIMIT)


def _const_spec(shape):
    return pl.BlockSpec(shape, lambda *_: (0,) * len(shape))


def _tile(n, pref=TOKEN_TILE):
    t = min(n, pref)
    while n % t:
        t -= 8
    return t


def _rms(x, g):
    return x * lax.rsqrt(jnp.mean(x * x, axis=-1, keepdims=True) + RMS_EPS) * g


def _bdot(a, b):
    return jnp.dot(a.astype(BF16), b, preferred_element_type=F32)


def _dot_split(x, w):
    hi = x.astype(BF16)
    lo = (x - hi.astype(F32)).astype(BF16)
    return (jnp.dot(hi, w, preferred_element_type=F32) + jnp.dot(lo, w, preferred_element_type=F32))


def _head_sum(x, e_ref, et_ref):
    return _dot_split(_dot_split(x, e_ref[...]), et_ref[...])


def _ffn_kernel(x_ref, g_ref, wg_ref, wu_ref, wo_ref, o_ref):
    x = x_ref[...]
    xn = _rms(x, g_ref[...]).astype(BF16)
    acc = jnp.zeros_like(x)
    for j in range(N_FFN_CHUNKS):
        gate = jnp.dot(xn, wg_ref[j], preferred_element_type=F32)
        up = jnp.dot(xn, wu_ref[j], preferred_element_type=F32)
        act = (gate * jax.nn.sigmoid(gate) * up).astype(BF16)
        acc = acc + jnp.dot(act, wo_ref[j], preferred_element_type=F32)
    o_ref[...] = x + 0.5 * acc


def _ffn(h, g, wg, wu, wo):
    n = h.shape[0]
    tm = _tile(n)
    return pl.pallas_call(
        _ffn_kernel,
        grid=(n // tm,),
        in_specs=[pl.BlockSpec((tm, D_MODEL), lambda i: (i, 0)), _const_spec((1, D_MODEL)),
                  _const_spec(wg.shape), _const_spec(wu.shape), _const_spec(wo.shape)],
        out_specs=pl.BlockSpec((tm, D_MODEL), lambda i: (i, 0)),
        out_shape=jax.ShapeDtypeStruct((n, D_MODEL), F32),
        compiler_params=_params("arbitrary"),
        name="ffn",
    )(h, g, wg, wu, wo)


def _norm_mm_kernel(x_ref, g_ref, w_ref, o_ref):
    o_ref[...] = _bdot(_rms(x_ref[...], g_ref[...]), w_ref[...]).astype(o_ref.dtype)


def _norm_mm(h, g, w, out_dtype, col_tile=None, name="norm_mm"):
    n = h.shape[0]
    tm = _tile(n)
    nout = w.shape[1]
    tn = nout if col_tile is None else col_tile
    return pl.pallas_call(
        _norm_mm_kernel,
        grid=(n // tm, nout // tn),
        in_specs=[pl.BlockSpec((tm, D_MODEL), lambda i, j: (i, 0)), _const_spec((1, D_MODEL)),
                  pl.BlockSpec((D_MODEL, tn), lambda i, j: (0, j))],
        out_specs=pl.BlockSpec((tm, tn), lambda i, j: (i, j)),
        out_shape=jax.ShapeDtypeStruct((n, nout), out_dtype),
        compiler_params=_params("arbitrary", "arbitrary"),
        name=name,
    )(h, g, w)


def _pe_kernel(x_ref, p_ref, g_ref, wgate_ref, wproj_ref, gf_ref, o_ref, *, final):
    x = x_ref[...]
    gate = jax.nn.sigmoid(_bdot(_rms(x, g_ref[...]), wgate_ref[...]))
    y = x + gate * _bdot(p_ref[...], wproj_ref[...])
    o_ref[...] = _rms(y, gf_ref[...]) if final else y


def _pe(h, p, g, wgate, wproj, gfinal, final):
    n = h.shape[0]
    tm = _tile(n)
    return pl.pallas_call(
        functools.partial(_pe_kernel, final=final),
        grid=(n // tm,),
        in_specs=[pl.BlockSpec((tm, D_MODEL), lambda i: (i, 0)), pl.BlockSpec((tm, PLE_DIM), lambda i: (i, 0)),
                  _const_spec((1, D_MODEL)), _const_spec(wgate.shape), _const_spec(wproj.shape),
                  _const_spec((1, D_MODEL))],
        out_specs=pl.BlockSpec((tm, D_MODEL), lambda i: (i, 0)),
        out_shape=jax.ShapeDtypeStruct((n, D_MODEL), F32),
        compiler_params=_params("arbitrary"),
        name="pe_final" if final else "pe",
    )(h, p, g, wgate, wproj, gfinal)


def _rwkv_proj_kernel(h_ref, aux_ref, sh_ref, nw_ref, mix_ref, wrkv_ref, w0_ref, w1_ref, w2_ref,
                      a0_ref, a1_ref, a2_ref, g1_ref, g2_ref, kk_ref, ka_ref, e_ref, et_ref,
                      r_o, lw_o, k_o, v_o, kk_o, b_o, g_o, hn_o, *, within_seq, seq_len, tiles_per_seq):
    i = pl.program_id(0)
    nw = nw_ref[...]
    hn = _rms(h_ref[...], nw)
    tm = hn.shape[0]
    row = lax.broadcasted_iota(jnp.int32, hn.shape, 0)
    rolled = pltpu.roll(hn, 1, 0)
    if within_seq:
        prev_last = _rms(aux_ref[...], nw)[7:8]
        tile_in_seq = lax.rem(jnp.full((1, D_MODEL), i, jnp.int32), tiles_per_seq)
        first = jnp.where(tile_in_seq == 0, sh_ref[...], prev_last)
        x_prev = jnp.where(row == 0, first, rolled)
        hn_o[...] = hn[tm - 1:tm]
    else:
        x_prev = jnp.where(lax.rem(row, seq_len) == 0, aux_ref[...], rolled)
        hn_o[...] = hn
    xx = x_prev - hn
    mix = mix_ref[...]
    xr, xw, xk, xv, xa, xg = (hn + xx * mix[j:j + 1] for j in range(6))
    r = _bdot(xr, wrkv_ref[0])
    k = _bdot(xk, wrkv_ref[1])
    v = _bdot(xv, wrkv_ref[2])
    wl = w0_ref[...] + _bdot(jnp.tanh(_bdot(xw, w1_ref[...])), w2_ref[...])
    lw = -EXP_MINUS_HALF * jax.nn.sigmoid(wl)
    a = jax.nn.sigmoid(a0_ref[...] + _bdot(_bdot(xa, a1_ref[...]), a2_ref[...]))
    g = _bdot(jax.nn.sigmoid(_bdot(xg, g1_ref[...])), g2_ref[...])
    kkv = k * kk_ref[...]
    kk = kkv / jnp.maximum(jnp.sqrt(_head_sum(kkv * kkv, e_ref, et_ref)), 1e-12)
    r_o[...] = r
    lw_o[...] = lw
    k_o[...] = k * (1.0 + (a - 1.0) * ka_ref[...])
    v_o[...] = v
    kk_o[...] = kk
    b_o[...] = kk * a
    g_o[...] = g


def _rwkv_proj(h, shift, seq_len, lw):
    n = h.shape[0]
    nb = n // seq_len
    tm = _tile(n)
    within_seq = seq_len % tm == 0
    row_spec = pl.BlockSpec((tm, D_MODEL), lambda i: (i, 0))
    if within_seq:
        tiles_per_seq = seq_len // tm
        aux = h
        aux_spec = pl.BlockSpec((8, D_MODEL), lambda i: (jnp.maximum(i * (tm // 8) - 1, 0), 0))
        sh = shift.reshape(nb, 1, D_MODEL)
        sh_spec = pl.BlockSpec((None, 1, D_MODEL), lambda i: (i // tiles_per_seq, 0, 0))
        hn_shape = jax.ShapeDtypeStruct((nb, 1, D_MODEL), F32)
        hn_spec = pl.BlockSpec((None, 1, D_MODEL), lambda i: (i // tiles_per_seq, 0, 0))
    else:
        assert tm % seq_len == 0
        tiles_per_seq = 1
        aux = jnp.repeat(shift, seq_len, axis=0)
        aux_spec = row_spec
        sh = shift.reshape(nb, 1, D_MODEL)
        sh_spec = pl.BlockSpec((None, 1, D_MODEL), lambda i: (0, 0, 0))
        hn_shape = jax.ShapeDtypeStruct((n, D_MODEL), F32)
        hn_spec = row_spec
    consts = [lw["nw"], lw["mix"], lw["wrkv"], lw["w0"], lw["w1"], lw["w2"], lw["a0"], lw["a1"], lw["a2"],
              lw["g1"], lw["g2"], lw["kk"], lw["ka"], lw["e"], lw["et"]]
    big = jax.ShapeDtypeStruct((n, D_MODEL), F32)
    outs = pl.pallas_call(
        functools.partial(_rwkv_proj_kernel, within_seq=within_seq, seq_len=seq_len, tiles_per_seq=tiles_per_seq),
        grid=(n // tm,),
        in_specs=[row_spec, aux_spec, sh_spec] + [_const_spec(c.shape) for c in consts],
        out_specs=[row_spec] * 7 + [hn_spec],
        out_shape=[big] * 7 + [hn_shape],
        compiler_params=_params("arbitrary"),
        name="rwkv_proj",
    )(h, aux, sh, *consts)
    r, lwd, k, v, kk, b, g, hn = outs
    shift_out = hn.reshape(nb, D_MODEL) if within_seq else hn.reshape(nb, seq_len, D_MODEL)[:, -1]
    return r, lwd, k, v, kk, b, g, shift_out


def _wkv_kernel(r_ref, lw_ref, k_ref, v_ref, kk_ref, b_ref, s0_ref, y_ref, st_ref, state, *, n_sub):
    c = pl.program_id(1)
    C = WKV_CHUNK
    C2 = 2 * C
    n_pairs = N_HEADS // 2

    @pl.when(c == 0)
    def _():
        state[...] = s0_ref[...]

    ri = lax.broadcasted_iota(jnp.int32, (C, C), 0)
    ci = lax.broadcasted_iota(jnp.int32, (C, C), 1)
    tri = (ri >= ci).astype(BF16)
    r2 = lax.broadcasted_iota(jnp.int32, (C2, C2), 0)
    c2 = lax.broadcasted_iota(jnp.int32, (C2, C2), 1)
    strict = r2 > c2
    incl = r2 >= c2
    lane = lax.broadcasted_iota(jnp.int32, (C, LANES), 1)
    head0 = lane < HEAD_DIM

    def hat(x):
        x3 = jnp.stack([x[:, p * LANES:(p + 1) * LANES] for p in range(n_pairs)])
        return jnp.concatenate([jnp.where(head0, x3, 0.0), jnp.where(head0, 0.0, x3)], axis=1)

    def mm(a, b):
        return jnp.einsum("pmk,pkn->pmn", a.astype(BF16), b.astype(BF16), preferred_element_type=F32)

    def mm_nt(a, b):
        return jnp.einsum("pmk,pnk->pmn", a.astype(BF16), b.astype(BF16), preferred_element_type=F32)

    def sub_chunk(s, carry):
        rows = pl.ds(pl.multiple_of(s * C, C), C)
        lw = lw_ref[rows, :]
        p1 = lw.astype(BF16)
        rem = lw - p1.astype(F32)
        p2 = rem.astype(BF16)
        p3 = (rem - p2.astype(F32)).astype(BF16)
        cw = (jnp.dot(tri, p1, preferred_element_type=F32) + jnp.dot(tri, p2, preferred_element_type=F32)
              + jnp.dot(tri, p3, preferred_element_type=F32))
        cw_end = cw[C - 1:C, :]
        e_neg = jnp.exp(-cw)
        e_end = jnp.exp(cw_end - cw)
        kk = kk_ref[rows, :]
        bb = b_ref[rows, :]
        k = k_ref[rows, :]
        a_all = kk * jnp.exp(cw - lw)
        r_all = r_ref[rows, :] * jnp.exp(cw)
        b_all = bb * e_neg
        k_all = k * e_neg
        bd_all = bb * e_end
        kd_all = k * e_end
        v_all = v_ref[rows, :]
        decay = jnp.exp(cw_end)
        ar_h = jnp.concatenate([hat(a_all), hat(r_all)], axis=1)
        bk_h = jnp.concatenate([hat(b_all), hat(k_all)], axis=1)
        v_h = hat(v_all)
        st = state[...]
        g = mm_nt(ar_h, bk_h)
        low = jnp.where(strict, g[:, :C2, :C2], 0.0)
        ak = jnp.where(strict, g[:, :C2, C2:], 0.0)
        rbk = jnp.concatenate([jnp.where(incl, g[:, C2:, :C2], 0.0), jnp.where(incl, g[:, C2:, C2:], 0.0)], axis=2)
        ss = mm_nt(ar_h, st)
        x = -(ss[:, :C2] + mm(ak, v_h))
        t = mm(low, jnp.concatenate([low, x], axis=2))
        lp = t[:, :, :C2]
        x = x - t[:, :, C2:]
        for _ in range(4):
            t = mm(lp, jnp.concatenate([lp, x], axis=2))
            lp = t[:, :, :C2]
            x = x + t[:, :, C2:]
        x = x + mm(lp, x)
        xv = jnp.concatenate([x, v_h], axis=1)
        y_h = ss[:, C2:] + mm(rbk, xv)
        y = y_h[:, :C] + y_h[:, C:]
        for p in range(n_pairs):
            y_ref[rows, p * LANES:(p + 1) * LANES] = y[p]
        bkd_h = jnp.concatenate([hat(bd_all), hat(kd_all)], axis=1)
        xv_t = jnp.stack([xv[p].T for p in range(n_pairs)])
        dec3 = jnp.stack([decay[:, p * LANES:(p + 1) * LANES] for p in range(n_pairs)])
        state[...] = st * dec3 + mm(xv_t, bkd_h)
        return carry

    lax.fori_loop(0, n_sub, sub_chunk, 0)

    @pl.when(c == pl.num_programs(1) - 1)
    def _():
        st_ref[...] = state[...]


def _wkv(r, lwd, k, v, kk, b, s0_bd, seq_len):
    n = r.shape[0]
    nb = n // seq_len
    ct = _tile(seq_len, 256)
    assert ct % WKV_CHUNK == 0
    steps = seq_len // ct
    row_spec = pl.BlockSpec((ct, D_MODEL), lambda bi, ci: (bi * steps + ci, 0))
    st_spec = pl.BlockSpec((None, N_HEADS // 2, LANES, LANES), lambda bi, ci: (bi, 0, 0, 0))
    return pl.pallas_call(
        functools.partial(_wkv_kernel, n_sub=ct // WKV_CHUNK),
        grid=(nb, steps),
        in_specs=[row_spec] * 6 + [st_spec],
        out_specs=[row_spec, st_spec],
        out_shape=[jax.ShapeDtypeStruct((n, D_MODEL), F32), jax.ShapeDtypeStruct(s0_bd.shape, F32)],
        scratch_shapes=[pltpu.VMEM((N_HEADS // 2, LANES, LANES), F32)],
        compiler_params=_params("arbitrary", "arbitrary"),
        name="wkv",
    )(r, lwd, k, v, kk, b, s0_bd)


def _rwkv_out_kernel(y_ref, r_ref, k_ref, v_ref, g_ref, h_ref, lnw_ref, lnb_ref, rk_ref, wo_ref, e_ref, et_ref, o_ref):
    y = y_ref[...]
    inv_n = 1.0 / HEAD_DIM
    mu = _head_sum(y, e_ref, et_ref) * inv_n
    yc = y - mu
    var = _head_sum(yc * yc, e_ref, et_ref) * inv_n
    yn = yc * lax.rsqrt(var + LNX_EPS) * lnw_ref[...] + lnb_ref[...]
    bonus = _head_sum(r_ref[...] * k_ref[...] * rk_ref[...], e_ref, et_ref) * v_ref[...]
    o_ref[...] = h_ref[...] + _bdot((yn + bonus) * g_ref[...], wo_ref[...])


def _rwkv_out(y, r, k, v, g, h, lw):
    n = h.shape[0]
    tm = _tile(n)
    row_spec = pl.BlockSpec((tm, D_MODEL), lambda i: (i, 0))
    consts = [lw["lnw"], lw["lnb"], lw["rk"], lw["wo"], lw["e"], lw["et"]]
    return pl.pallas_call(
        _rwkv_out_kernel,
        grid=(n // tm,),
        in_specs=[row_spec] * 6 + [_const_spec(c.shape) for c in consts],
        out_specs=row_spec,
        out_shape=jax.ShapeDtypeStruct((n, D_MODEL), F32),
        compiler_params=_params("arbitrary"),
        name="rwkv_out",
    )(y, r, k, v, g, h, *consts)


def _t5_buckets(dist):
    d = np.asarray(dist, dtype=np.int64)
    max_exact = REL_BUCKETS // 2
    large = max_exact + (np.log(np.maximum(d, 1) / max_exact) / np.log(REL_MAX_DIST / max_exact)
                         * (REL_BUCKETS - max_exact)).astype(np.int32)
    large = np.minimum(large, REL_BUCKETS - 1)
    return np.where(d < max_exact, d, large).astype(np.int32)


def _toeplitz(tab, n_rows, n_cols):
    width = n_rows + n_cols - 1
    lead = tab.shape[:-1]
    padded = jnp.pad(tab[..., :width], [(0, 0)] * len(lead) + [(0, 1)])
    flat = jnp.broadcast_to(padded[..., None, :], lead + (n_rows + 1, width + 1)).reshape(lead + (-1,))
    skew = flat[..., :n_rows * (width + 2)].reshape(lead + (n_rows, width + 2))
    return skew[..., :n_cols]


def _band_bias(rel_bias, group):
    win, dil = DILATION_GROUPS[group]
    assert win // dil == Q_BLOCK
    m = np.arange(-(Q_BLOCK - 1), 2 * Q_BLOCK)
    valid = (m >= 0) & (m <= Q_BLOCK)
    buckets = _t5_buckets(dil * np.clip(m, 0, Q_BLOCK))
    tbl = jnp.take(rel_bias[:, group * N_HEADS:(group + 1) * N_HEADS].astype(F32), buckets, axis=0).T
    rev = jnp.where(valid[None], tbl, NEG_INF)[:, ::-1]
    general = _toeplitz(rev, Q_BLOCK, 2 * Q_BLOCK)[:, ::-1, :]
    first = jnp.where((np.arange(2 * Q_BLOCK) >= Q_BLOCK)[None, None, :], general, NEG_INF)
    return jnp.stack([first, general])


_HEAD_OF_SEG = np.array([4 * (2 * (s // 8) + s % 2) + (s // 2) % 4 for s in range(N_HEADS)])


def _proj_rm_kernel(x_ref, g_ref, w_ref, *refs, natural, lane_ranges, tm):
    scr = refs[-1]
    outs = refs[:-1]
    y = _bdot(_rms(x_ref[...], g_ref[...]), w_ref[...])
    if natural:
        outs[0][...] = y
        outs = outs[1:]
    for c in range(scr.shape[0]):
        scr[c] = y[:, c * LANES:(c + 1) * LANES]
    for gi, (_, dil) in enumerate(DILATION_GROUPS):
        lo, hi = lane_ranges[gi]
        if dil == 1:
            outs[gi][0] = y[:, lo:hi].astype(BF16)
            continue
        for rho in range(dil):
            rows = [scr[c, pl.ds(rho, tm // dil, stride=dil), :] for c in range(lo // LANES, hi // LANES)]
            outs[gi][rho] = jnp.concatenate(rows, axis=1).astype(BF16)


def _proj_rm(h, g, w, nb, seq_len, natural, lane_ranges, name):
    n = h.shape[0]
    tm = _tile(n)
    nout = w.shape[1]
    assert seq_len % tm == 0 and all(tm % (16 * dil) == 0 for _, dil in DILATION_GROUPS)
    tps = seq_len // tm
    out_shape, out_specs = [], []
    if natural:
        out_shape.append(jax.ShapeDtypeStruct((n, nout), F32))
        out_specs.append(pl.BlockSpec((tm, nout), lambda i: (i, 0)))
    for (_, dil), (lo, hi) in zip(DILATION_GROUPS, lane_ranges):
        out_shape.append(jax.ShapeDtypeStruct((nb, dil, seq_len // dil, hi - lo), BF16))
        out_specs.append(pl.BlockSpec((None, dil, tm // dil, hi - lo), lambda i: (i // tps, 0, i % tps, 0)))
    return pl.pallas_call(
        functools.partial(_proj_rm_kernel, natural=natural, lane_ranges=lane_ranges, tm=tm),
        grid=(n // tm,),
        in_specs=[pl.BlockSpec((tm, D_MODEL), lambda i: (i, 0)), _const_spec((1, D_MODEL)), _const_spec(w.shape)],
        out_specs=out_specs,
        out_shape=out_shape,
        scratch_shapes=[pltpu.VMEM((nout // LANES, tm, LANES), F32)],
        compiler_params=_params("arbitrary"),
        name=name,
    )(h, g, w)


def _attn_kernel(q_ref, kv_ref, halo_ref, bias_ref, o_ref, lse_ref, kvbuf, *, nq):
    i = pl.program_id(0)
    dil = q_ref.shape[0]
    kvd = KV_HEADS * HEAD_DIM
    nt = (((1,), (1,)), ((), ()))
    kvbuf[:, :Q_BLOCK, :] = halo_ref[...]
    kvbuf[:, Q_BLOCK:, :] = kv_ref[...]
    lane = lax.broadcasted_iota(jnp.int32, (Q_BLOCK, LANES), 1)
    low_half = lane < HEAD_DIM
    lane_row = lax.broadcasted_iota(jnp.int32, (1, LANES), 1)
    keep_lo = (lane_row < HEAD_DIM).astype(BF16)
    keep_hi = (lane_row >= HEAD_DIM).astype(BF16)

    def block(u, carry):
        rho = lax.div(u, nq)
        j = u - rho * nq
        r0 = pl.multiple_of(j * Q_BLOCK, Q_BLOCK)
        qb = q_ref[rho, pl.ds(r0, Q_BLOCK), :]
        kvb = kvbuf[rho, pl.ds(r0, 2 * Q_BLOCK), :]
        bsel = jnp.where(jnp.logical_and(i == 0, j == 0), 0, 1)
        lse_tile = jnp.zeros((Q_BLOCK, LANES), F32)
        for G in range(KV_HEADS // 2):
            kg = kvb[:, G * LANES:(G + 1) * LANES]
            vg = kvb[:, kvd + G * LANES:kvd + (G + 1) * LANES]
            vcat = jnp.concatenate([vg * keep_lo, vg * keep_hi], axis=0)
            pieces = []
            for r in range(Q_PER_KV):
                qg = qb[:, (G * Q_PER_KV + r) * LANES:(G * Q_PER_KV + r + 1) * LANES]
                pieces += [qg * keep_lo, qg * keep_hi]
            s_all = lax.dot_general(jnp.concatenate(pieces, axis=0), kg, nt, preferred_element_type=F32)
            for r in range(Q_PER_KV):
                grp = G * Q_PER_KV + r
                parts = []
                for half in range(2):
                    s = s_all[(2 * r + half) * Q_BLOCK:(2 * r + half + 1) * Q_BLOCK]
                    s = s + bias_ref[bsel, int(_HEAD_OF_SEG[2 * grp + half])]
                    m = jnp.max(s, axis=-1, keepdims=True)
                    p = jnp.exp(s - m)
                    l = jnp.sum(p, axis=-1, keepdims=True)
                    parts.append((p.astype(BF16), l, m + jnp.log(l)))
                (p0, l0, e0), (p1, l1, e1) = parts
                o = jnp.dot(jnp.concatenate([p0, p1], axis=1), vcat, preferred_element_type=F32)
                o_ref[rho, pl.ds(r0, Q_BLOCK), grp * LANES:(grp + 1) * LANES] = o / jnp.where(low_half, l0, l1)
                lse_tile = jnp.where(lane == 2 * grp, e0, lse_tile)
                lse_tile = jnp.where(lane == 2 * grp + 1, e1, lse_tile)
        lse_ref[rho, pl.ds(r0, Q_BLOCK), :] = lse_tile
        return carry

    lax.fori_loop(0, dil * nq, block, 0)


def _attn_group(q_rm, kv_rm, bias):
    nb, dil, tsub, _ = q_rm.shape
    kvw = kv_rm.shape[-1]
    rows = ATTN_SPAN // dil
    nq = rows // Q_BLOCK
    assert tsub % rows == 0 and nq >= 1
    span_spec = lambda width: pl.BlockSpec((None, dil, rows, width), lambda i, b: (b, 0, i, 0))
    return pl.pallas_call(
        functools.partial(_attn_kernel, nq=nq),
        grid=(tsub // rows, nb),
        in_specs=[span_spec(D_MODEL), span_spec(kvw),
                  pl.BlockSpec((None, dil, Q_BLOCK, kvw), lambda i, b: (b, 0, jnp.maximum(i * nq - 1, 0), 0)),
                  _const_spec(bias.shape)],
        out_specs=[span_spec(D_MODEL), span_spec(LANES)],
        out_shape=[jax.ShapeDtypeStruct((nb, dil, tsub, D_MODEL), F32),
                   jax.ShapeDtypeStruct((nb, dil, tsub, LANES), F32)],
        scratch_shapes=[pltpu.VMEM((dil, Q_BLOCK + rows, kvw), BF16)],
        compiler_params=_params("arbitrary", "arbitrary"),
        name=f"attn_d{dil}",
    )(q_rm, kv_rm, kv_rm, bias)


def _attn_out_kernel(o0_ref, o1_ref, o2_ref, l0_ref, l1_ref, l2_ref, h_ref, wo_ref, et_ref, out_ref, *scr, tm):
    outs, lses = [], []
    for gi, (o_ref, l_ref) in enumerate(((o0_ref, l0_ref), (o1_ref, l1_ref), (o2_ref, l2_ref))):
        dil = DILATION_GROUPS[gi][1]
        if dil == 1:
            outs.append(o_ref[0])
            lses.append(l_ref[0])
            continue
        so, sl = scr[2 * gi], scr[2 * gi + 1]
        n_tiles = so.shape[0]
        for rho in range(dil):
            rows = pl.ds(rho, tm // dil, stride=dil)
            for c in range(n_tiles):
                so[c, rows, :] = o_ref[rho, :, c * LANES:(c + 1) * LANES]
            sl[rows, :] = l_ref[rho]
        outs.append(jnp.concatenate([so[c] for c in range(n_tiles)], axis=1))
        lses.append(sl[...])
    l0, l1, l2 = lses
    m = jnp.maximum(jnp.maximum(l0, l1), l2)
    w0, w1, w2 = jnp.exp(l0 - m), jnp.exp(l1 - m), jnp.exp(l2 - m)
    inv = 1.0 / (w0 + w1 + w2)
    et = et_ref[...]
    att = (_dot_split(w0 * inv, et) * outs[0] + _dot_split(w1 * inv, et) * outs[1]
           + _dot_split(w2 * inv, et) * outs[2])
    out_ref[...] = h_ref[...] + _bdot(att, wo_ref[...])


def _attn_out(outs, lses, h, wo, et, seq_len):
    n = h.shape[0]
    tm = _tile(n)
    tps = seq_len // tm
    row_spec = pl.BlockSpec((tm, D_MODEL), lambda i: (i, 0))

    def rm_spec(dil, width):
        return pl.BlockSpec((None, dil, tm // dil, width), lambda i: (i // tps, 0, i % tps, 0))

    dils = [dil for _, dil in DILATION_GROUPS]
    scratch = []
    for _ in dils:
        scratch += [pltpu.VMEM((D_MODEL // LANES, tm, LANES), F32), pltpu.VMEM((tm, LANES), F32)]
    return pl.pallas_call(
        functools.partial(_attn_out_kernel, tm=tm),
        grid=(n // tm,),
        in_specs=[rm_spec(d, D_MODEL) for d in dils] + [rm_spec(d, LANES) for d in dils]
                 + [row_spec, _const_spec(wo.shape), _const_spec(et.shape)],
        out_specs=row_spec,
        out_shape=jax.ShapeDtypeStruct((n, D_MODEL), F32),
        scratch_shapes=scratch,
        compiler_params=_params("arbitrary"),
        name="attn_out",
    )(*outs, *lses, h, wo, et)


def _decode_bias(rel_bias, seq_len, cache_len):
    ncol = cache_len + LANES
    dist = np.arange(-(LANES - 1), cache_len + seq_len)
    buckets = _t5_buckets(np.clip(dist, 0, MAX_WINDOW))
    tabs = []
    for g, (win, dil) in enumerate(DILATION_GROUPS):
        valid = (dist >= 0) & (dist % dil == 0) & (dist <= win)
        tbl = jnp.take(rel_bias[:, g * N_HEADS:(g + 1) * N_HEADS].astype(F32), buckets, axis=0).T
        tabs.append(jnp.where(valid[None], tbl, NEG_INF))
    rev = jnp.stack(tabs)[..., ::-1]
    rows = _toeplitz(rev, seq_len, ncol)[..., ::-1, :]
    bias = rows.reshape(N_GROUPS * N_HEADS * seq_len, ncol)
    return bias[:, :cache_len], bias[:, cache_len:]


def _attn_decode_kernel(q_ref, cache_ref, kvn_ref, h_ref, bc_ref, bn_ref, wo_ref, out_ref, *, seq_len):
    kvd = KV_HEADS * HEAD_DIM
    nslot = N_GROUPS * N_HEADS
    rows_g = N_HEADS * seq_len
    nt = (((1,), (1,)), ((), ()))
    cache = cache_ref[...]
    kc = cache[:, :kvd].astype(BF16)
    vc = cache[:, kvd:].astype(BF16)
    kvn = kvn_ref[...]
    pad = jnp.zeros((LANES - seq_len, kvd), F32)
    kn = jnp.concatenate([kvn[:, :kvd], pad], axis=0).astype(BF16)
    vn = jnp.concatenate([kvn[:, kvd:], pad], axis=0).astype(BF16)
    lhs = jnp.concatenate([q_ref[:, s * kvd:(s + 1) * kvd] for s in range(nslot)], axis=0).astype(BF16)
    sc = lax.dot_general(lhs, kc, nt, preferred_element_type=F32) + bc_ref[...]
    sn = lax.dot_general(lhs, kn, nt, preferred_element_type=F32) + bn_ref[...]
    m_g = jnp.maximum(jnp.max(sc, axis=-1, keepdims=True), jnp.max(sn, axis=-1, keepdims=True))
    m = jnp.maximum(jnp.maximum(m_g[:rows_g], m_g[rows_g:2 * rows_g]), m_g[2 * rows_g:])
    m3 = jnp.concatenate([m, m, m], axis=0)
    pc = jnp.exp(sc - m3)
    pn = jnp.exp(sn - m3)
    l_g = jnp.sum(pc, axis=-1, keepdims=True) + jnp.sum(pn, axis=-1, keepdims=True)
    num_g = (jnp.dot(pc.astype(BF16), vc, preferred_element_type=F32)
             + jnp.dot(pn.astype(BF16), vn, preferred_element_type=F32))
    l = l_g[:rows_g] + l_g[rows_g:2 * rows_g] + l_g[2 * rows_g:]
    num = num_g[:rows_g] + num_g[rows_g:2 * rows_g] + num_g[2 * rows_g:]
    row = lax.broadcasted_iota(jnp.int32, (rows_g, kvd), 0)
    lane = lax.broadcasted_iota(jnp.int32, (rows_g, kvd), 1)
    own = (row // (Q_PER_KV * seq_len)) == (lane // HEAD_DIM)
    att = jnp.where(own, num / l, 0.0)
    out = h_ref[...]
    for r in range(Q_PER_KV):
        a_r = att[r * seq_len:(r + 1) * seq_len]
        for c in range(1, KV_HEADS):
            a_r = a_r + att[(c * Q_PER_KV + r) * seq_len:(c * Q_PER_KV + r + 1) * seq_len]
        out = out + _bdot(a_r, wo_ref[r])
    out_ref[...] = out


def _attn_decode(q, cache, kv_new, h, bias_c, bias_n, wo_r, nb, seq_len):
    cache_len = cache.shape[1]
    qw = q.shape[1]
    kvw = 2 * KV_HEADS * HEAD_DIM
    return pl.pallas_call(
        functools.partial(_attn_decode_kernel, seq_len=seq_len),
        grid=(nb,),
        in_specs=[pl.BlockSpec((seq_len, qw), lambda b: (b, 0)),
                  pl.BlockSpec((None, cache_len, kvw), lambda b: (b, 0, 0)),
                  pl.BlockSpec((seq_len, kvw), lambda b: (b, 0)),
                  pl.BlockSpec((seq_len, D_MODEL), lambda b: (b, 0)),
                  _const_spec(bias_c.shape), _const_spec(bias_n.shape), _const_spec(wo_r.shape)],
        out_specs=pl.BlockSpec((seq_len, D_MODEL), lambda b: (b, 0)),
        out_shape=jax.ShapeDtypeStruct((nb * seq_len, D_MODEL), F32),
        compiler_params=_params("arbitrary"),
        name="attn_decode",
    )(q, cache, kv_new, h, bias_c, bias_n, wo_r)


def _prep_weights(norm_w, ffn1_wi, ffn1_wo, ffn2_wi, ffn2_wo, pe_proj, pe_gate,
                  rwkv_mix, rwkv_wrkv, rwkv_wo, rwkv_w0, rwkv_w1, rwkv_w2, rwkv_a0, rwkv_a1, rwkv_a2,
                  rwkv_g1, rwkv_g2, rwkv_kk, rwkv_ka, rwkv_rk, rwkv_lnx_w, rwkv_lnx_b,
                  attn_wq, attn_wo, kv_norm, w_kv, rel_bias, final_norm):
    def row(v):
        return v.reshape(1, -1).astype(F32)

    def pad_cols(w, n):
        return jnp.pad(w, ((0, 0), (0, n - w.shape[1]))).astype(BF16)

    def pad_rows(w, n):
        return jnp.pad(w, ((0, n - w.shape[0]), (0, 0))).astype(BF16)

    def ffn_w(wi, wo):
        wg = wi[:, :D_FF].reshape(D_MODEL, N_FFN_CHUNKS, FFN_CHUNK).transpose(1, 0, 2).astype(BF16)
        wu = wi[:, D_FF:].reshape(D_MODEL, N_FFN_CHUNKS, FFN_CHUNK).transpose(1, 0, 2).astype(BF16)
        return wg, wu, wo.reshape(N_FFN_CHUNKS, FFN_CHUNK, D_MODEL).astype(BF16)

    head_of_lane = np.arange(D_MODEL) // HEAD_DIM
    e = jnp.asarray(head_of_lane[:, None] == np.arange(LANES)[None, :], BF16)
    et = jnp.asarray(np.arange(LANES)[:, None] == head_of_lane[None, :], BF16)

    depth = norm_w.shape[0]
    layers = []
    for i in range(depth):
        layers.append(dict(
            nw=[row(norm_w[i, j]) for j in range(4)],
            ffn1=ffn_w(ffn1_wi[i], ffn1_wo[i]), ffn2=ffn_w(ffn2_wi[i], ffn2_wo[i]),
            pe_gate=pe_gate[i].astype(BF16), pe_proj=pe_proj[i].astype(BF16)))
    n_a = depth // 2
    rw = []
    for i in range(n_a):
        rw.append(dict(
            nw=row(norm_w[i, 1]), mix=rwkv_mix[i].astype(F32), wrkv=rwkv_wrkv[i].astype(BF16),
            w0=row(rwkv_w0[i]), w1=pad_cols(rwkv_w1[i], LANES), w2=pad_rows(rwkv_w2[i], LANES),
            a0=row(rwkv_a0[i]), a1=pad_cols(rwkv_a1[i], LANES), a2=pad_rows(rwkv_a2[i], LANES),
            g1=pad_cols(rwkv_g1[i], 2 * LANES), g2=pad_rows(rwkv_g2[i], 2 * LANES),
            kk=row(rwkv_kk[i]), ka=row(rwkv_ka[i]), rk=row(rwkv_rk[i]),
            lnw=row(rwkv_lnx_w[i]), lnb=row(rwkv_lnx_b[i]), wo=rwkv_wo[i].astype(BF16), e=e, et=et))
    scale = HEAD_DIM ** -0.5
    at = []
    for j in range(depth - n_a):
        wq = attn_wq[j] * scale
        wq6 = wq.reshape(D_MODEL, N_GROUPS, KV_HEADS, Q_PER_KV, HEAD_DIM)
        wq_slots = jnp.einsum("dgcre,cx->dgcrxe", wq6, jnp.eye(KV_HEADS, dtype=wq.dtype))
        wq_slots = wq_slots.reshape(D_MODEL, N_GROUPS * N_HEADS * KV_HEADS * HEAD_DIM).astype(BF16)
        wo_r = attn_wo[j].reshape(KV_HEADS, Q_PER_KV, HEAD_DIM, D_MODEL).transpose(1, 0, 2, 3)
        wo_r = wo_r.reshape(Q_PER_KV, KV_HEADS * HEAD_DIM, D_MODEL).astype(BF16)
        wq_seg = wq.reshape(D_MODEL, N_GROUPS, N_HEADS, HEAD_DIM)[:, :, _HEAD_OF_SEG].reshape(D_MODEL, -1)
        wo_seg = attn_wo[j].reshape(N_HEADS, HEAD_DIM, D_MODEL)[_HEAD_OF_SEG].reshape(D_MODEL, D_MODEL)
        at.append(dict(wq_seg=wq_seg.astype(BF16), wq_slots=wq_slots, wo_seg=wo_seg.astype(BF16), wo_r=wo_r))
    return dict(layers=layers, rwkv=rw, attn=at, kv_norm=row(kv_norm), w_kv=w_kv.astype(BF16),
                final_norm=row(final_norm), rel_bias=rel_bias, et=et)


def _state_to_blockdiag(s):
    nb = s.shape[0]
    s5 = s.astype(F32).reshape(nb, N_HEADS // 2, 2, HEAD_DIM, HEAD_DIM)
    bd = jnp.einsum("bphij,hg->bphigj", s5, jnp.eye(2, dtype=F32))
    return bd.reshape(nb, N_HEADS // 2, LANES, LANES)


def _blockdiag_to_state(bd):
    nb = bd.shape[0]
    b6 = bd.reshape(nb, N_HEADS // 2, 2, HEAD_DIM, 2, HEAD_DIM)
    s = jnp.stack([b6[:, :, 0, :, 0, :], b6[:, :, 1, :, 1, :]], axis=2)
    return s.reshape(nb, N_HEADS, HEAD_DIM, HEAD_DIM)


def _pad_time(a, nb, seq_len, padded):
    a3 = a.reshape(nb, seq_len, D_MODEL)
    return jnp.pad(a3, ((0, 0), (0, padded - seq_len), (0, 0))).reshape(nb * padded, D_MODEL)


def _trunk(x, p, wkv0, shift0, cache, w):
    nb, seq_len, _ = x.shape
    n = nb * seq_len
    depth = len(w["layers"])
    n_a = depth // 2
    h = x.reshape(n, D_MODEL).astype(F32)
    wkv_out, shift_out = [], []
    kv_rows = kv_rm = None
    kvw = 2 * KV_HEADS * HEAD_DIM
    for i in range(depth):
        lw = w["layers"][i]
        if i == n_a:
            if cache is None:
                kv_rows, *kv_rm = _proj_rm(h, w["kv_norm"], w["w_kv"], nb, seq_len, True,
                                           [(0, kvw)] * N_GROUPS, "kv_proj")
            else:
                kv_rows = _norm_mm(h, w["kv_norm"], w["w_kv"], F32, name="kv_proj")
        h = _ffn(h, lw["nw"][0], *lw["ffn1"])
        if i < n_a:
            rwl = w["rwkv"][i]
            r, lwd, k, v, kk, b, g, sh = _rwkv_proj(h, shift0[i].astype(F32), seq_len, rwl)
            s0 = _state_to_blockdiag(wkv0[i])
            if seq_len % WKV_CHUNK:
                tp = -(-seq_len // WKV_CHUNK) * WKV_CHUNK
                padded = [_pad_time(a, nb, seq_len, tp) for a in (r, lwd, k, v, kk, b)]
                y, st = _wkv(*padded, s0, tp)
                y = y.reshape(nb, tp, D_MODEL)[:, :seq_len].reshape(n, D_MODEL)
            else:
                y, st = _wkv(r, lwd, k, v, kk, b, s0, seq_len)
            h = _rwkv_out(y, r, k, v, g, h, rwl)
            wkv_out.append(_blockdiag_to_state(st).astype(x.dtype))
            shift_out.append(sh.astype(x.dtype))
        else:
            al = w["attn"][i - n_a]
            if cache is None:
                q_rm = _proj_rm(h, lw["nw"][1], al["wq_seg"], nb, seq_len, False,
                                [(gi * D_MODEL, (gi + 1) * D_MODEL) for gi in range(N_GROUPS)], "q_proj")
                outs, lses = [], []
                for gi in range(N_GROUPS):
                    o, lse = _attn_group(q_rm[gi], kv_rm[gi], _band_bias(w["rel_bias"], gi))
                    outs.append(o)
                    lses.append(lse)
                h = _attn_out(outs, lses, h, al["wo_seg"], w["et"], seq_len)
            else:
                q = _norm_mm(h, lw["nw"][1], al["wq_slots"], F32, col_tile=2048, name="q_proj_slots")
                bias_c, bias_n = _decode_bias(w["rel_bias"], seq_len, cache.shape[1])
                h = _attn_decode(q, cache, kv_rows, h, bias_c, bias_n, al["wo_r"], nb, seq_len)
        h = _ffn(h, lw["nw"][2], *lw["ffn2"])
        h = _pe(h, p[i].reshape(n, PLE_DIM).astype(F32), lw["nw"][3], lw["pe_gate"], lw["pe_proj"],
                w["final_norm"], final=(i == depth - 1))
    y = h.reshape(nb, seq_len, D_MODEL).astype(x.dtype)
    kv_rows = kv_rows.reshape(nb, seq_len, 2, KV_HEADS, HEAD_DIM).astype(x.dtype)
    return y, jnp.stack(wkv_out), jnp.stack(shift_out), kv_rows


def kernel(x_prompt, x_sample, state_wkv, state_shift, cache_kv, p_prompt, p_sample, norm_w, ffn1_wi, ffn1_wo, ffn2_wi, ffn2_wo, pe_proj, pe_gate, rwkv_mix, rwkv_wrkv, rwkv_wo, rwkv_w0, rwkv_w1, rwkv_w2, rwkv_a0, rwkv_a1, rwkv_a2, rwkv_g1, rwkv_g2, rwkv_kk, rwkv_ka, rwkv_rk, rwkv_lnx_w, rwkv_lnx_b, attn_wq, attn_wo, kv_norm, w_kv, rel_bias, final_norm):
    w = _prep_weights(norm_w, ffn1_wi, ffn1_wo, ffn2_wi, ffn2_wo, pe_proj, pe_gate,
                      rwkv_mix, rwkv_wrkv, rwkv_wo, rwkv_w0, rwkv_w1, rwkv_w2, rwkv_a0, rwkv_a1, rwkv_a2,
                      rwkv_g1, rwkv_g2, rwkv_kk, rwkv_ka, rwkv_rk, rwkv_lnx_w, rwkv_lnx_b,
                      attn_wq, attn_wo, kv_norm, w_kv, rel_bias, final_norm)
    n_a = norm_w.shape[0] // 2
    nb, seq_len, _ = x_prompt.shape
    wkv0 = jnp.zeros((n_a, nb, N_HEADS, HEAD_DIM, HEAD_DIM), F32)
    shift0 = jnp.zeros((n_a, nb, D_MODEL), x_prompt.dtype)
    y_p, wkv_p, shift_p, kv_p = _trunk(x_prompt, p_prompt, wkv0, shift0, None, w)
    kv_prompt = kv_p[:, seq_len - min(MAX_WINDOW, seq_len):]
    cache = cache_kv.reshape(cache_kv.shape[0], cache_kv.shape[1], 2 * KV_HEADS * HEAD_DIM).astype(F32)
    y_s, wkv_s, shift_s, kv_s = _trunk(x_sample, p_sample, state_wkv, state_shift, cache, w)
    return (y_p, y_s, wkv_p, shift_p, kv_prompt, wkv_s, shift_s, kv_s)
```

```python
import functools

import numpy as np
import jax
import jax.numpy as jnp
from jax import lax
from jax.experimental import pallas as pl
from jax.experimental.pallas import tpu as pltpu

F32 = jnp.float32
BF16 = jnp.bfloat16

D_MODEL = 1024
D_FF = 2816
PLE_DIM = 256
RMS_EPS = 1e-6
HEAD_DIM = 64
N_HEADS = D_MODEL // HEAD_DIM
LNX_EPS = 64e-5
KV_HEADS = 4
Q_PER_KV = N_HEADS // KV_HEADS
DILATION_GROUPS = ((128, 1), (512, 4), (2048, 16))
N_GROUPS = len(DILATION_GROUPS)
MAX_WINDOW = 2048
REL_BUCKETS = 32
REL_MAX_DIST = 2048
NEG_INF = -1e30

LANES = 128
FFN_CHUNK = 256
N_FFN_CHUNKS = D_FF // FFN_CHUNK
TOKEN_TILE = 256
FFN_TILE = 512
WKV_CHUNK = 64
Q_BLOCK = 128
ATTN_SPAN = 2048
VMEM_LIMIT = 56 * 1024 * 1024
EXP_MINUS_HALF = 0.6065306597126334


def _params(*sem):
    return pltpu.CompilerParams(dimension_semantics=sem, vmem_limit_bytes=VMEM_LIMIT)


def _const_spec(shape):
    return pl.BlockSpec(shape, lambda *_: (0,) * len(shape))


def _tile(n, pref=TOKEN_TILE):
    t = min(n, pref)
    while n % t:
        t -= 8
    return t


def _rms(x, g):
    return x * lax.rsqrt(jnp.mean(x * x, axis=-1, keepdims=True) + RMS_EPS) * g


def _bdot(a, b):
    return jnp.dot(a.astype(BF16), b, preferred_element_type=F32)


def _dot_split(x, w):
    hi = x.astype(BF16)
    lo = (x - hi.astype(F32)).astype(BF16)
    return (jnp.dot(hi, w, preferred_element_type=F32) + jnp.dot(lo, w, preferred_element_type=F32))


def _head_sum(x, e_ref, et_ref):
    return _dot_split(_dot_split(x, e_ref[...]), et_ref[...])


def _ffn_kernel(x_ref, g_ref, wi_ref, wo_ref, o_ref):
    x = x_ref[...]
    xn = _rms(x, g_ref[...]).astype(BF16)
    acc = jnp.zeros_like(x)
    for j in range(N_FFN_CHUNKS):
        lo, hi = j * FFN_CHUNK, (j + 1) * FFN_CHUNK
        gate = jnp.dot(xn, wi_ref[:, lo:hi], preferred_element_type=F32)
        up = jnp.dot(xn, wi_ref[:, D_FF + lo:D_FF + hi], preferred_element_type=F32)
        act = (gate * jax.nn.sigmoid(gate) * up).astype(BF16)
        acc = acc + jnp.dot(act, wo_ref[lo:hi, :], preferred_element_type=F32)
    o_ref[...] = x + 0.5 * acc


def _ffn(h, g, wi, wo):
    n = h.shape[0]
    tm = _tile(n, FFN_TILE)
    return pl.pallas_call(
        _ffn_kernel,
        grid=(n // tm,),
        in_specs=[pl.BlockSpec((tm, D_MODEL), lambda i: (i, 0)), _const_spec((1, D_MODEL)),
                  _const_spec(wi.shape), _const_spec(wo.shape)],
        out_specs=pl.BlockSpec((tm, D_MODEL), lambda i: (i, 0)),
        out_shape=jax.ShapeDtypeStruct((n, D_MODEL), F32),
        compiler_params=_params("arbitrary"),
        name="ffn",
    )(h, g, wi, wo)


def _norm_mm_kernel(x_ref, g_ref, w_ref, o_ref):
    o_ref[...] = _bdot(_rms(x_ref[...], g_ref[...]), w_ref[...]).astype(o_ref.dtype)


def _norm_mm(h, g, w, out_dtype, col_tile=None, name="norm_mm"):
    n = h.shape[0]
    tm = _tile(n)
    nout = w.shape[1]
    tn = nout if col_tile is None else col_tile
    return pl.pallas_call(
        _norm_mm_kernel,
        grid=(n // tm, nout // tn),
        in_specs=[pl.BlockSpec((tm, D_MODEL), lambda i, j: (i, 0)), _const_spec((1, D_MODEL)),
                  pl.BlockSpec((D_MODEL, tn), lambda i, j: (0, j))],
        out_specs=pl.BlockSpec((tm, tn), lambda i, j: (i, j)),
        out_shape=jax.ShapeDtypeStruct((n, nout), out_dtype),
        compiler_params=_params("arbitrary", "arbitrary"),
        name=name,
    )(h, g, w)


def _pe_kernel(x_ref, p_ref, g_ref, wgate_ref, wproj_ref, gf_ref, o_ref, *, final):
    x = x_ref[...]
    gate = jax.nn.sigmoid(_bdot(_rms(x, g_ref[...]), wgate_ref[...]))
    y = x + gate * _bdot(p_ref[...], wproj_ref[...])
    o_ref[...] = _rms(y, gf_ref[...]) if final else y


def _pe(h, p, g, wgate, wproj, gfinal, final):
    n = h.shape[0]
    tm = _tile(n)
    return pl.pallas_call(
        functools.partial(_pe_kernel, final=final),
        grid=(n // tm,),
        in_specs=[pl.BlockSpec((tm, D_MODEL), lambda i: (i, 0)), pl.BlockSpec((tm, PLE_DIM), lambda i: (i, 0)),
                  _const_spec((1, D_MODEL)), _const_spec(wgate.shape), _const_spec(wproj.shape),
                  _const_spec((1, D_MODEL))],
        out_specs=pl.BlockSpec((tm, D_MODEL), lambda i: (i, 0)),
        out_shape=jax.ShapeDtypeStruct((n, D_MODEL), F32),
        compiler_params=_params("arbitrary"),
        name="pe_final" if final else "pe",
    )(h, p, g, wgate, wproj, gfinal)


def _rwkv_proj_kernel(h_ref, aux_ref, sh_ref, nw_ref, mix_ref, wrkv_ref, w0_ref, w1_ref, w2_ref,
                      a0_ref, a1_ref, a2_ref, g1_ref, g2_ref, kk_ref, ka_ref, e_ref, et_ref,
                      r_o, lw_o, k_o, v_o, kk_o, b_o, g_o, hn_o, *, within_seq, seq_len, tiles_per_seq):
    i = pl.program_id(0)
    nw = nw_ref[...]
    hn = _rms(h_ref[...], nw)
    tm = hn.shape[0]
    row = lax.broadcasted_iota(jnp.int32, hn.shape, 0)
    rolled = pltpu.roll(hn, 1, 0)
    if within_seq:
        prev_last = _rms(aux_ref[...], nw)[7:8]
        tile_in_seq = lax.rem(jnp.full((1, D_MODEL), i, jnp.int32), tiles_per_seq)
        first = jnp.where(tile_in_seq == 0, sh_ref[...], prev_last)
        x_prev = jnp.where(row == 0, first, rolled)
        hn_o[...] = hn[tm - 1:tm]
    else:
        x_prev = jnp.where(lax.rem(row, seq_len) == 0, aux_ref[...], rolled)
        hn_o[...] = hn
    xx = x_prev - hn
    mix = mix_ref[...]
    xr, xw, xk, xv, xa, xg = (hn + xx * mix[j:j + 1] for j in range(6))
    r = _bdot(xr, wrkv_ref[0])
    k = _bdot(xk, wrkv_ref[1])
    v = _bdot(xv, wrkv_ref[2])
    wl = w0_ref[...] + _bdot(jnp.tanh(_bdot(xw, w1_ref[...])), w2_ref[...])
    lw = -EXP_MINUS_HALF * jax.nn.sigmoid(wl)
    a = jax.nn.sigmoid(a0_ref[...] + _bdot(_bdot(xa, a1_ref[...]), a2_ref[...]))
    g = _bdot(jax.nn.sigmoid(_bdot(xg, g1_ref[...])), g2_ref[...])
    kkv = k * kk_ref[...]
    kk = kkv / jnp.maximum(jnp.sqrt(_head_sum(kkv * kkv, e_ref, et_ref)), 1e-12)
    r_o[...] = r
    lw_o[...] = lw
    k_o[...] = k * (1.0 + (a - 1.0) * ka_ref[...])
    v_o[...] = v
    kk_o[...] = kk
    b_o[...] = kk * a
    g_o[...] = g


def _rwkv_proj(h, shift, seq_len, lw):
    n = h.shape[0]
    nb = n // seq_len
    tm = _tile(n)
    within_seq = seq_len % tm == 0
    row_spec = pl.BlockSpec((tm, D_MODEL), lambda i: (i, 0))
    if within_seq:
        tiles_per_seq = seq_len // tm
        aux = h
        aux_spec = pl.BlockSpec((8, D_MODEL), lambda i: (jnp.maximum(i * (tm // 8) - 1, 0), 0))
        sh = shift.reshape(nb, 1, D_MODEL)
        sh_spec = pl.BlockSpec((None, 1, D_MODEL), lambda i: (i // tiles_per_seq, 0, 0))
        hn_shape = jax.ShapeDtypeStruct((nb, 1, D_MODEL), F32)
        hn_spec = pl.BlockSpec((None, 1, D_MODEL), lambda i: (i // tiles_per_seq, 0, 0))
    else:
        assert tm % seq_len == 0
        tiles_per_seq = 1
        aux = jnp.repeat(shift, seq_len, axis=0)
        aux_spec = row_spec
        sh = shift.reshape(nb, 1, D_MODEL)
        sh_spec = pl.BlockSpec((None, 1, D_MODEL), lambda i: (0, 0, 0))
        hn_shape = jax.ShapeDtypeStruct((n, D_MODEL), F32)
        hn_spec = row_spec
    consts = [lw["nw"], lw["mix"], lw["wrkv"], lw["w0"], lw["w1"], lw["w2"], lw["a0"], lw["a1"], lw["a2"],
              lw["g1"], lw["g2"], lw["kk"], lw["ka"], lw["e"], lw["et"]]
    big = jax.ShapeDtypeStruct((n, D_MODEL), F32)
    outs = pl.pallas_call(
        functools.partial(_rwkv_proj_kernel, within_seq=within_seq, seq_len=seq_len, tiles_per_seq=tiles_per_seq),
        grid=(n // tm,),
        in_specs=[row_spec, aux_spec, sh_spec] + [_const_spec(c.shape) for c in consts],
        out_specs=[row_spec] * 7 + [hn_spec],
        out_shape=[big] * 7 + [hn_shape],
        compiler_params=_params("arbitrary"),
        name="rwkv_proj",
    )(h, aux, sh, *consts)
    r, lwd, k, v, kk, b, g, hn = outs
    shift_out = hn.reshape(nb, D_MODEL) if within_seq else hn.reshape(nb, seq_len, D_MODEL)[:, -1]
    return r, lwd, k, v, kk, b, g, shift_out


def _wkv_kernel(r_ref, lw_ref, k_ref, v_ref, kk_ref, b_ref, s0_ref, y_ref, st_ref, state, *, n_sub):
    c = pl.program_id(1)
    C = WKV_CHUNK
    C2 = 2 * C
    n_pairs = N_HEADS // 2

    @pl.when(c == 0)
    def _():
        state[...] = jnp.zeros_like(state)
        for h in range(N_HEADS):
            lo = (h % 2) * HEAD_DIM
            state[h // 2, lo:lo + HEAD_DIM, lo:lo + HEAD_DIM] = s0_ref[h]

    ri = lax.broadcasted_iota(jnp.int32, (C, C), 0)
    ci = lax.broadcasted_iota(jnp.int32, (C, C), 1)
    tri = (ri >= ci).astype(BF16)
    r2 = lax.broadcasted_iota(jnp.int32, (C2, C2), 0)
    c2 = lax.broadcasted_iota(jnp.int32, (C2, C2), 1)
    strict = r2 > c2
    incl = r2 >= c2
    lane = lax.broadcasted_iota(jnp.int32, (C, LANES), 1)
    head0 = lane < HEAD_DIM

    def hat(x):
        x3 = jnp.stack([x[:, p * LANES:(p + 1) * LANES] for p in range(n_pairs)])
        return jnp.concatenate([jnp.where(head0, x3, 0.0), jnp.where(head0, 0.0, x3)], axis=1)

    def mm(a, b):
        return jnp.einsum("pmk,pkn->pmn", a.astype(BF16), b.astype(BF16), preferred_element_type=F32)

    def mm_nt(a, b):
        return jnp.einsum("pmk,pnk->pmn", a.astype(BF16), b.astype(BF16), preferred_element_type=F32)

    def sub_chunk(s, carry):
        rows = pl.ds(pl.multiple_of(s * C, C), C)
        lw = lw_ref[rows, :]
        p1 = lw.astype(BF16)
        rem = lw - p1.astype(F32)
        p2 = rem.astype(BF16)
        p3 = (rem - p2.astype(F32)).astype(BF16)
        cw = (jnp.dot(tri, p1, preferred_element_type=F32) + jnp.dot(tri, p2, preferred_element_type=F32)
              + jnp.dot(tri, p3, preferred_element_type=F32))
        cw_end = cw[C - 1:C, :]
        e_neg = jnp.exp(-cw)
        e_end = jnp.exp(cw_end - cw)
        kk = kk_ref[rows, :]
        bb = b_ref[rows, :]
        k = k_ref[rows, :]
        a_all = kk * jnp.exp(cw - lw)
        r_all = r_ref[rows, :] * jnp.exp(cw)
        b_all = bb * e_neg
        k_all = k * e_neg
        bd_all = bb * e_end
        kd_all = k * e_end
        v_all = v_ref[rows, :]
        decay = jnp.exp(cw_end)
        ar_h = jnp.concatenate([hat(a_all), hat(r_all)], axis=1)
        bk_h = jnp.concatenate([hat(b_all), hat(k_all)], axis=1)
        v_h = hat(v_all)
        st = state[...]
        g = mm_nt(ar_h, bk_h)
        low = jnp.where(strict, g[:, :C2, :C2], 0.0)
        ak = jnp.where(strict, g[:, :C2, C2:], 0.0)
        rbk = jnp.concatenate([jnp.where(incl, g[:, C2:, :C2], 0.0), jnp.where(incl, g[:, C2:, C2:], 0.0)], axis=2)
        ss = mm_nt(ar_h, st)
        x = -(ss[:, :C2] + mm(ak, v_h))
        t = mm(low, jnp.concatenate([low, x], axis=2))
        lp = t[:, :, :C2]
        x = x - t[:, :, C2:]
        for _ in range(4):
            t = mm(lp, jnp.concatenate([lp, x], axis=2))
            lp = t[:, :, :C2]
            x = x + t[:, :, C2:]
        x = x + mm(lp, x)
        xv = jnp.concatenate([x, v_h], axis=1)
        y_h = ss[:, C2:] + mm(rbk, xv)
        y = y_h[:, :C] + y_h[:, C:]
        for p in range(n_pairs):
            y_ref[rows, p * LANES:(p + 1) * LANES] = y[p]
        bkd_h = jnp.concatenate([hat(bd_all), hat(kd_all)], axis=1)
        xv_t = jnp.stack([xv[p].T for p in range(n_pairs)])
        dec3 = jnp.stack([decay[:, p * LANES:(p + 1) * LANES] for p in range(n_pairs)])
        state[...] = st * dec3 + mm(xv_t, bkd_h)
        return carry

    lax.fori_loop(0, n_sub, sub_chunk, 0)

    @pl.when(c == pl.num_programs(1) - 1)
    def _():
        for h in range(N_HEADS):
            lo = (h % 2) * HEAD_DIM
            st_ref[h] = state[h // 2, lo:lo + HEAD_DIM, lo:lo + HEAD_DIM]


def _wkv(r, lwd, k, v, kk, b, s0, seq_len):
    n = r.shape[0]
    nb = n // seq_len
    ct = _tile(seq_len, 256)
    assert ct % WKV_CHUNK == 0
    steps = seq_len // ct
    row_spec = pl.BlockSpec((ct, D_MODEL), lambda bi, ci: (bi * steps + ci, 0))
    st_spec = pl.BlockSpec((None, N_HEADS, HEAD_DIM, HEAD_DIM), lambda bi, ci: (bi, 0, 0, 0))
    return pl.pallas_call(
        functools.partial(_wkv_kernel, n_sub=ct // WKV_CHUNK),
        grid=(nb, steps),
        in_specs=[row_spec] * 6 + [st_spec],
        out_specs=[row_spec, st_spec],
        out_shape=[jax.ShapeDtypeStruct((n, D_MODEL), F32), jax.ShapeDtypeStruct(s0.shape, F32)],
        scratch_shapes=[pltpu.VMEM((N_HEADS // 2, LANES, LANES), F32)],
        compiler_params=_params("arbitrary", "arbitrary"),
        name="wkv",
    )(r, lwd, k, v, kk, b, s0)


def _rwkv_out_kernel(y_ref, r_ref, k_ref, v_ref, g_ref, h_ref, lnw_ref, lnb_ref, rk_ref, wo_ref, e_ref, et_ref, o_ref):
    y = y_ref[...]
    inv_n = 1.0 / HEAD_DIM
    mu = _head_sum(y, e_ref, et_ref) * inv_n
    yc = y - mu
    var = _head_sum(yc * yc, e_ref, et_ref) * inv_n
    yn = yc * lax.rsqrt(var + LNX_EPS) * lnw_ref[...] + lnb_ref[...]
    bonus = _head_sum(r_ref[...] * k_ref[...] * rk_ref[...], e_ref, et_ref) * v_ref[...]
    o_ref[...] = h_ref[...] + _bdot((yn + bonus) * g_ref[...], wo_ref[...])


def _rwkv_out(y, r, k, v, g, h, lw):
    n = h.shape[0]
    tm = _tile(n)
    row_spec = pl.BlockSpec((tm, D_MODEL), lambda i: (i, 0))
    consts = [lw["lnw"], lw["lnb"], lw["rk"], lw["wo"], lw["e"], lw["et"]]
    return pl.pallas_call(
        _rwkv_out_kernel,
        grid=(n // tm,),
        in_specs=[row_spec] * 6 + [_const_spec(c.shape) for c in consts],
        out_specs=row_spec,
        out_shape=jax.ShapeDtypeStruct((n, D_MODEL), F32),
        compiler_params=_params("arbitrary"),
        name="rwkv_out",
    )(y, r, k, v, g, h, *consts)


def _t5_buckets(dist):
    d = np.asarray(dist, dtype=np.int64)
    max_exact = REL_BUCKETS // 2
    large = max_exact + (np.log(np.maximum(d, 1) / max_exact) / np.log(REL_MAX_DIST / max_exact)
                         * (REL_BUCKETS - max_exact)).astype(np.int32)
    large = np.minimum(large, REL_BUCKETS - 1)
    return np.where(d < max_exact, d, large).astype(np.int32)


def _toeplitz(tab, n_rows, n_cols):
    width = n_rows + n_cols - 1
    lead = tab.shape[:-1]
    padded = jnp.pad(tab[..., :width], [(0, 0)] * len(lead) + [(0, 1)])
    flat = jnp.broadcast_to(padded[..., None, :], lead + (n_rows + 1, width + 1)).reshape(lead + (-1,))
    skew = flat[..., :n_rows * (width + 2)].reshape(lead + (n_rows, width + 2))
    return skew[..., :n_cols]


def _band_bias(rel_bias, group):
    win, dil = DILATION_GROUPS[group]
    assert win // dil == Q_BLOCK
    m = np.arange(-(Q_BLOCK - 1), 2 * Q_BLOCK)
    valid = (m >= 0) & (m <= Q_BLOCK)
    buckets = _t5_buckets(dil * np.clip(m, 0, Q_BLOCK))
    tbl = jnp.take(rel_bias[:, group * N_HEADS:(group + 1) * N_HEADS].astype(F32), buckets, axis=0).T
    rev = jnp.where(valid[None], tbl, NEG_INF)[:, ::-1]
    general = _toeplitz(rev, Q_BLOCK, 2 * Q_BLOCK)[:, ::-1, :]
    first = jnp.where((np.arange(2 * Q_BLOCK) >= Q_BLOCK)[None, None, :], general, NEG_INF)
    return jnp.stack([first, general])


_HEAD_OF_SEG = np.array([4 * (2 * (s // 8) + s % 2) + (s // 2) % 4 for s in range(N_HEADS)])


def _proj_rm_kernel(x_ref, g_ref, w_ref, *refs, natural, lane_ranges, tm):
    scr = refs[-1]
    outs = refs[:-1]
    y = _bdot(_rms(x_ref[...], g_ref[...]), w_ref[...])
    if natural:
        outs[0][...] = y
        outs = outs[1:]
    for c in range(scr.shape[0]):
        scr[c] = y[:, c * LANES:(c + 1) * LANES]
    for gi, (_, dil) in enumerate(DILATION_GROUPS):
        lo, hi = lane_ranges[gi]
        if dil == 1:
            outs[gi][0] = y[:, lo:hi].astype(BF16)
            continue
        for rho in range(dil):
            rows = [scr[c, pl.ds(rho, tm // dil, stride=dil), :] for c in range(lo // LANES, hi // LANES)]
            outs[gi][rho] = jnp.concatenate(rows, axis=1).astype(BF16)


def _proj_rm(h, g, w, nb, seq_len, natural, lane_ranges, name):
    n = h.shape[0]
    tm = _tile(n)
    nout = w.shape[1]
    assert seq_len % tm == 0 and all(tm % (16 * dil) == 0 for _, dil in DILATION_GROUPS)
    tps = seq_len // tm
    out_shape, out_specs = [], []
    if natural:
        out_shape.append(jax.ShapeDtypeStruct((n, nout), F32))
        out_specs.append(pl.BlockSpec((tm, nout), lambda i: (i, 0)))
    for (_, dil), (lo, hi) in zip(DILATION_GROUPS, lane_ranges):
        out_shape.append(jax.ShapeDtypeStruct((nb, dil, seq_len // dil, hi - lo), BF16))
        out_specs.append(pl.BlockSpec((None, dil, tm // dil, hi - lo), lambda i: (i // tps, 0, i % tps, 0)))
    return pl.pallas_call(
        functools.partial(_proj_rm_kernel, natural=natural, lane_ranges=lane_ranges, tm=tm),
        grid=(n // tm,),
        in_specs=[pl.BlockSpec((tm, D_MODEL), lambda i: (i, 0)), _const_spec((1, D_MODEL)), _const_spec(w.shape)],
        out_specs=out_specs,
        out_shape=out_shape,
        scratch_shapes=[pltpu.VMEM((nout // LANES, tm, LANES), F32)],
        compiler_params=_params("arbitrary"),
        name=name,
    )(h, g, w)


def _attn_kernel(q_ref, kv_ref, halo_ref, bias_ref, o_ref, lse_ref, kvbuf, *, nq):
    i = pl.program_id(0)
    dil = q_ref.shape[0]
    kvd = KV_HEADS * HEAD_DIM
    nt = (((1,), (1,)), ((), ()))
    kvbuf[:, :Q_BLOCK, :] = halo_ref[...]
    kvbuf[:, Q_BLOCK:, :] = kv_ref[...]
    lane = lax.broadcasted_iota(jnp.int32, (Q_BLOCK, LANES), 1)
    low_half = lane < HEAD_DIM
    lane_row = lax.broadcasted_iota(jnp.int32, (1, LANES), 1)
    keep_lo = (lane_row < HEAD_DIM).astype(BF16)
    keep_hi = (lane_row >= HEAD_DIM).astype(BF16)

    def block(u, carry):
        rho = lax.div(u, nq)
        j = u - rho * nq
        r0 = pl.multiple_of(j * Q_BLOCK, Q_BLOCK)
        qb = q_ref[rho, pl.ds(r0, Q_BLOCK), :]
        kvb = kvbuf[rho, pl.ds(r0, 2 * Q_BLOCK), :]
        bsel = jnp.where(jnp.logical_and(i == 0, j == 0), 0, 1)
        lse_tile = jnp.zeros((Q_BLOCK, LANES), F32)
        for G in range(KV_HEADS // 2):
            kg = kvb[:, G * LANES:(G + 1) * LANES]
            vg = kvb[:, kvd + G * LANES:kvd + (G + 1) * LANES]
            vcat = jnp.concatenate([vg * keep_lo, vg * keep_hi], axis=0)
            pieces = []
            for r in range(Q_PER_KV):
                qg = qb[:, (G * Q_PER_KV + r) * LANES:(G * Q_PER_KV + r + 1) * LANES]
                pieces += [qg * keep_lo, qg * keep_hi]
            s_all = lax.dot_general(jnp.concatenate(pieces, axis=0), kg, nt, preferred_element_type=F32)
            for r in range(Q_PER_KV):
                grp = G * Q_PER_KV + r
                parts = []
                for half in range(2):
                    s = s_all[(2 * r + half) * Q_BLOCK:(2 * r + half + 1) * Q_BLOCK]
                    s = s + bias_ref[bsel, int(_HEAD_OF_SEG[2 * grp + half])]
                    m = jnp.max(s, axis=-1, keepdims=True)
                    p = jnp.exp(s - m)
                    l = jnp.sum(p, axis=-1, keepdims=True)
                    parts.append((p.astype(BF16), l, m + jnp.log(l)))
                (p0, l0, e0), (p1, l1, e1) = parts
                o = jnp.dot(jnp.concatenate([p0, p1], axis=1), vcat, preferred_element_type=F32)
                o_ref[rho, pl.ds(r0, Q_BLOCK), grp * LANES:(grp + 1) * LANES] = o / jnp.where(low_half, l0, l1)
                lse_tile = jnp.where(lane == 2 * grp, e0, lse_tile)
                lse_tile = jnp.where(lane == 2 * grp + 1, e1, lse_tile)
        lse_ref[rho, pl.ds(r0, Q_BLOCK), :] = lse_tile
        return carry

    lax.fori_loop(0, dil * nq, block, 0)


def _attn_group(q_rm, kv_rm, bias):
    nb, dil, tsub, _ = q_rm.shape
    kvw = kv_rm.shape[-1]
    rows = ATTN_SPAN // dil
    nq = rows // Q_BLOCK
    assert tsub % rows == 0 and nq >= 1
    span_spec = lambda width: pl.BlockSpec((None, dil, rows, width), lambda i, b: (b, 0, i, 0))
    return pl.pallas_call(
        functools.partial(_attn_kernel, nq=nq),
        grid=(tsub // rows, nb),
        in_specs=[span_spec(D_MODEL), span_spec(kvw),
                  pl.BlockSpec((None, dil, Q_BLOCK, kvw), lambda i, b: (b, 0, jnp.maximum(i * nq - 1, 0), 0)),
                  _const_spec(bias.shape)],
        out_specs=[span_spec(D_MODEL), span_spec(LANES)],
        out_shape=[jax.ShapeDtypeStruct((nb, dil, tsub, D_MODEL), F32),
                   jax.ShapeDtypeStruct((nb, dil, tsub, LANES), F32)],
        scratch_shapes=[pltpu.VMEM((dil, Q_BLOCK + rows, kvw), BF16)],
        compiler_params=_params("arbitrary", "arbitrary"),
        name=f"attn_d{dil}",
    )(q_rm, kv_rm, kv_rm, bias)


def _attn_out_kernel(o0_ref, o1_ref, o2_ref, l0_ref, l1_ref, l2_ref, h_ref, wo_ref, et_ref, out_ref, *scr, tm):
    outs, lses = [], []
    for gi, (o_ref, l_ref) in enumerate(((o0_ref, l0_ref), (o1_ref, l1_ref), (o2_ref, l2_ref))):
        dil = DILATION_GROUPS[gi][1]
        if dil == 1:
            outs.append(o_ref[0])
            lses.append(l_ref[0])
            continue
        so, sl = scr[2 * gi], scr[2 * gi + 1]
        n_tiles = so.shape[0]
        for rho in range(dil):
            rows = pl.ds(rho, tm // dil, stride=dil)
            for c in range(n_tiles):
                so[c, rows, :] = o_ref[rho, :, c * LANES:(c + 1) * LANES]
            sl[rows, :] = l_ref[rho]
        outs.append(jnp.concatenate([so[c] for c in range(n_tiles)], axis=1))
        lses.append(sl[...])
    l0, l1, l2 = lses
    m = jnp.maximum(jnp.maximum(l0, l1), l2)
    w0, w1, w2 = jnp.exp(l0 - m), jnp.exp(l1 - m), jnp.exp(l2 - m)
    inv = 1.0 / (w0 + w1 + w2)
    et = et_ref[...]
    att = (_dot_split(w0 * inv, et) * outs[0] + _dot_split(w1 * inv, et) * outs[1]
           + _dot_split(w2 * inv, et) * outs[2])
    out_ref[...] = h_ref[...] + _bdot(att, wo_ref[...])


def _attn_out(outs, lses, h, wo, et, seq_len):
    n = h.shape[0]
    tm = _tile(n)
    tps = seq_len // tm
    row_spec = pl.BlockSpec((tm, D_MODEL), lambda i: (i, 0))

    def rm_spec(dil, width):
        return pl.BlockSpec((None, dil, tm // dil, width), lambda i: (i // tps, 0, i % tps, 0))

    dils = [dil for _, dil in DILATION_GROUPS]
    scratch = []
    for _ in dils:
        scratch += [pltpu.VMEM((D_MODEL // LANES, tm, LANES), F32), pltpu.VMEM((tm, LANES), F32)]
    return pl.pallas_call(
        functools.partial(_attn_out_kernel, tm=tm),
        grid=(n // tm,),
        in_specs=[rm_spec(d, D_MODEL) for d in dils] + [rm_spec(d, LANES) for d in dils]
                 + [row_spec, _const_spec(wo.shape), _const_spec(et.shape)],
        out_specs=row_spec,
        out_shape=jax.ShapeDtypeStruct((n, D_MODEL), F32),
        scratch_shapes=scratch,
        compiler_params=_params("arbitrary"),
        name="attn_out",
    )(*outs, *lses, h, wo, et)


def _decode_bias(rel_bias, seq_len, cache_len):
    ncol = cache_len + LANES
    dist = np.arange(-(LANES - 1), cache_len + seq_len)
    buckets = _t5_buckets(np.clip(dist, 0, MAX_WINDOW))
    tabs = []
    for g, (win, dil) in enumerate(DILATION_GROUPS):
        valid = (dist >= 0) & (dist % dil == 0) & (dist <= win)
        tbl = jnp.take(rel_bias[:, g * N_HEADS:(g + 1) * N_HEADS].astype(F32), buckets, axis=0).T
        tabs.append(jnp.where(valid[None], tbl, NEG_INF))
    rev = jnp.stack(tabs)[..., ::-1]
    rows = _toeplitz(rev, seq_len, ncol)[..., ::-1, :]
    bias = rows.reshape(N_GROUPS * N_HEADS * seq_len, ncol)
    return bias[:, :cache_len], bias[:, cache_len:]


def _attn_decode_kernel(q_ref, cache_ref, kvn_ref, h_ref, bc_ref, bn_ref, wo_ref, out_ref, *, seq_len):
    kvd = KV_HEADS * HEAD_DIM
    nslot = N_GROUPS * N_HEADS
    rows_g = N_HEADS * seq_len
    nt = (((1,), (1,)), ((), ()))
    cache = cache_ref[...]
    kc = cache[:, :kvd].astype(BF16)
    vc = cache[:, kvd:].astype(BF16)
    kvn = kvn_ref[...]
    pad = jnp.zeros((LANES - seq_len, kvd), F32)
    kn = jnp.concatenate([kvn[:, :kvd], pad], axis=0).astype(BF16)
    vn = jnp.concatenate([kvn[:, kvd:], pad], axis=0).astype(BF16)
    lhs = jnp.concatenate([q_ref[:, s * kvd:(s + 1) * kvd] for s in range(nslot)], axis=0).astype(BF16)
    sc = lax.dot_general(lhs, kc, nt, preferred_element_type=F32) + bc_ref[...]
    sn = lax.dot_general(lhs, kn, nt, preferred_element_type=F32) + bn_ref[...]
    m_g = jnp.maximum(jnp.max(sc, axis=-1, keepdims=True), jnp.max(sn, axis=-1, keepdims=True))
    m = jnp.maximum(jnp.maximum(m_g[:rows_g], m_g[rows_g:2 * rows_g]), m_g[2 * rows_g:])
    m3 = jnp.concatenate([m, m, m], axis=0)
    pc = jnp.exp(sc - m3)
    pn = jnp.exp(sn - m3)
    l_g = jnp.sum(pc, axis=-1, keepdims=True) + jnp.sum(pn, axis=-1, keepdims=True)
    num_g = (jnp.dot(pc.astype(BF16), vc, preferred_element_type=F32)
             + jnp.dot(pn.astype(BF16), vn, preferred_element_type=F32))
    l = l_g[:rows_g] + l_g[rows_g:2 * rows_g] + l_g[2 * rows_g:]
    num = num_g[:rows_g] + num_g[rows_g:2 * rows_g] + num_g[2 * rows_g:]
    row = lax.broadcasted_iota(jnp.int32, (rows_g, kvd), 0)
    lane = lax.broadcasted_iota(jnp.int32, (rows_g, kvd), 1)
    own = (row // (Q_PER_KV * seq_len)) == (lane // HEAD_DIM)
    att = jnp.where(own, num / l, 0.0)
    out = h_ref[...]
    for r in range(Q_PER_KV):
        a_r = att[r * seq_len:(r + 1) * seq_len]
        for c in range(1, KV_HEADS):
            a_r = a_r + att[(c * Q_PER_KV + r) * seq_len:(c * Q_PER_KV + r + 1) * seq_len]
        out = out + _bdot(a_r, wo_ref[r])
    out_ref[...] = out


def _attn_decode(q, cache, kv_new, h, bias_c, bias_n, wo_r, nb, seq_len):
    cache_len = cache.shape[1]
    qw = q.shape[1]
    kvw = 2 * KV_HEADS * HEAD_DIM
    return pl.pallas_call(
        functools.partial(_attn_decode_kernel, seq_len=seq_len),
        grid=(nb,),
        in_specs=[pl.BlockSpec((seq_len, qw), lambda b: (b, 0)),
                  pl.BlockSpec((None, cache_len, kvw), lambda b: (b, 0, 0)),
                  pl.BlockSpec((seq_len, kvw), lambda b: (b, 0)),
                  pl.BlockSpec((seq_len, D_MODEL), lambda b: (b, 0)),
                  _const_spec(bias_c.shape), _const_spec(bias_n.shape), _const_spec(wo_r.shape)],
        out_specs=pl.BlockSpec((seq_len, D_MODEL), lambda b: (b, 0)),
        out_shape=jax.ShapeDtypeStruct((nb * seq_len, D_MODEL), F32),
        compiler_params=_params("arbitrary"),
        name="attn_decode",
    )(q, cache, kv_new, h, bias_c, bias_n, wo_r)


def _prep_weights(norm_w, ffn1_wi, ffn1_wo, ffn2_wi, ffn2_wo, pe_proj, pe_gate,
                  rwkv_mix, rwkv_wrkv, rwkv_wo, rwkv_w0, rwkv_w1, rwkv_w2, rwkv_a0, rwkv_a1, rwkv_a2,
                  rwkv_g1, rwkv_g2, rwkv_kk, rwkv_ka, rwkv_rk, rwkv_lnx_w, rwkv_lnx_b,
                  attn_wq, attn_wo, kv_norm, w_kv, rel_bias, final_norm):
    def row(v):
        return v.reshape(1, -1).astype(F32)

    def pad_cols(w, n):
        return jnp.pad(w, ((0, 0), (0, n - w.shape[1]))).astype(BF16)

    def pad_rows(w, n):
        return jnp.pad(w, ((0, n - w.shape[0]), (0, 0))).astype(BF16)

    def ffn_w(wi, wo):
        return wi.astype(BF16), wo.astype(BF16)

    head_of_lane = np.arange(D_MODEL) // HEAD_DIM
    e = jnp.asarray(head_of_lane[:, None] == np.arange(LANES)[None, :], BF16)
    et = jnp.asarray(np.arange(LANES)[:, None] == head_of_lane[None, :], BF16)

    depth = norm_w.shape[0]
    layers = []
    for i in range(depth):
        layers.append(dict(
            nw=[row(norm_w[i, j]) for j in range(4)],
            ffn1=ffn_w(ffn1_wi[i], ffn1_wo[i]), ffn2=ffn_w(ffn2_wi[i], ffn2_wo[i]),
            pe_gate=pe_gate[i].astype(BF16), pe_proj=pe_proj[i].astype(BF16)))
    n_a = depth // 2
    rw = []
    for i in range(n_a):
        rw.append(dict(
            nw=row(norm_w[i, 1]), mix=rwkv_mix[i].astype(F32), wrkv=rwkv_wrkv[i].astype(BF16),
            w0=row(rwkv_w0[i]), w1=pad_cols(rwkv_w1[i], LANES), w2=pad_rows(rwkv_w2[i], LANES),
            a0=row(rwkv_a0[i]), a1=pad_cols(rwkv_a1[i], LANES), a2=pad_rows(rwkv_a2[i], LANES),
            g1=pad_cols(rwkv_g1[i], 2 * LANES), g2=pad_rows(rwkv_g2[i], 2 * LANES),
            kk=row(rwkv_kk[i]), ka=row(rwkv_ka[i]), rk=row(rwkv_rk[i]),
            lnw=row(rwkv_lnx_w[i]), lnb=row(rwkv_lnx_b[i]), wo=rwkv_wo[i].astype(BF16), e=e, et=et))
    scale = HEAD_DIM ** -0.5
    at = []
    for j in range(depth - n_a):
        wq = attn_wq[j] * scale
        wq6 = wq.reshape(D_MODEL, N_GROUPS, KV_HEADS, Q_PER_KV, HEAD_DIM)
        wq_slots = jnp.einsum("dgcre,cx->dgcrxe", wq6, jnp.eye(KV_HEADS, dtype=wq.dtype))
        wq_slots = wq_slots.reshape(D_MODEL, N_GROUPS * N_HEADS * KV_HEADS * HEAD_DIM).astype(BF16)
        wo_r = attn_wo[j].reshape(KV_HEADS, Q_PER_KV, HEAD_DIM, D_MODEL).transpose(1, 0, 2, 3)
        wo_r = wo_r.reshape(Q_PER_KV, KV_HEADS * HEAD_DIM, D_MODEL).astype(BF16)
        wq_seg = wq.reshape(D_MODEL, N_GROUPS, N_HEADS, HEAD_DIM)[:, :, _HEAD_OF_SEG].reshape(D_MODEL, -1)
        wo_seg = attn_wo[j].reshape(N_HEADS, HEAD_DIM, D_MODEL)[_HEAD_OF_SEG].reshape(D_MODEL, D_MODEL)
        at.append(dict(wq_seg=wq_seg.astype(BF16), wq_slots=wq_slots, wo_seg=wo_seg.astype(BF16), wo_r=wo_r))
    return dict(layers=layers, rwkv=rw, attn=at, kv_norm=row(kv_norm), w_kv=w_kv.astype(BF16),
                final_norm=row(final_norm), rel_bias=rel_bias, et=et)


def _pad_time(a, nb, seq_len, padded):
    a3 = a.reshape(nb, seq_len, D_MODEL)
    return jnp.pad(a3, ((0, 0), (0, padded - seq_len), (0, 0))).reshape(nb * padded, D_MODEL)


def _trunk(x, p, wkv0, shift0, cache, w):
    nb, seq_len, _ = x.shape
    n = nb * seq_len
    depth = len(w["layers"])
    n_a = depth // 2
    h = x.reshape(n, D_MODEL).astype(F32)
    wkv_out, shift_out = [], []
    kv_rows = kv_rm = None
    kvw = 2 * KV_HEADS * HEAD_DIM
    for i in range(depth):
        lw = w["layers"][i]
        if i == n_a:
            if cache is None:
                kv_rows, *kv_rm = _proj_rm(h, w["kv_norm"], w["w_kv"], nb, seq_len, True,
                                           [(0, kvw)] * N_GROUPS, "kv_proj")
            else:
                kv_rows = _norm_mm(h, w["kv_norm"], w["w_kv"], F32, name="kv_proj")
        h = _ffn(h, lw["nw"][0], *lw["ffn1"])
        if i < n_a:
            rwl = w["rwkv"][i]
            r, lwd, k, v, kk, b, g, sh = _rwkv_proj(h, shift0[i].astype(F32), seq_len, rwl)
            s0 = wkv0[i].astype(F32)
            if seq_len % WKV_CHUNK:
                tp = -(-seq_len // WKV_CHUNK) * WKV_CHUNK
                padded = [_pad_time(a, nb, seq_len, tp) for a in (r, lwd, k, v, kk, b)]
                y, st = _wkv(*padded, s0, tp)
                y = y.reshape(nb, tp, D_MODEL)[:, :seq_len].reshape(n, D_MODEL)
            else:
                y, st = _wkv(r, lwd, k, v, kk, b, s0, seq_len)
            h = _rwkv_out(y, r, k, v, g, h, rwl)
            wkv_out.append(st.astype(x.dtype))
            shift_out.append(sh.astype(x.dtype))
        else:
            al = w["attn"][i - n_a]
            if cache is None:
                q_rm = _proj_rm(h, lw["nw"][1], al["wq_seg"], nb, seq_len, False,
                                [(gi * D_MODEL, (gi + 1) * D_MODEL) for gi in range(N_GROUPS)], "q_proj")
                outs, lses = [], []
                for gi in range(N_GROUPS):
                    o, lse = _attn_group(q_rm[gi], kv_rm[gi], _band_bias(w["rel_bias"], gi))
                    outs.append(o)
                    lses.append(lse)
                h = _attn_out(outs, lses, h, al["wo_seg"], w["et"], seq_len)
            else:
                q = _norm_mm(h, lw["nw"][1], al["wq_slots"], F32, col_tile=2048, name="q_proj_slots")
                bias_c, bias_n = _decode_bias(w["rel_bias"], seq_len, cache.shape[1])
                h = _attn_decode(q, cache, kv_rows, h, bias_c, bias_n, al["wo_r"], nb, seq_len)
        h = _ffn(h, lw["nw"][2], *lw["ffn2"])
        h = _pe(h, p[i].reshape(n, PLE_DIM).astype(F32), lw["nw"][3], lw["pe_gate"], lw["pe_proj"],
                w["final_norm"], final=(i == depth - 1))
    y = h.reshape(nb, seq_len, D_MODEL).astype(x.dtype)
    kv_rows = kv_rows.reshape(nb, seq_len, 2, KV_HEADS, HEAD_DIM).astype(x.dtype)
    return y, jnp.stack(wkv_out), jnp.stack(shift_out), kv_rows


def kernel(x_prompt, x_sample, state_wkv, state_shift, cache_kv, p_prompt, p_sample, norm_w, ffn1_wi, ffn1_wo, ffn2_wi, ffn2_wo, pe_proj, pe_gate, rwkv_mix, rwkv_wrkv, rwkv_wo, rwkv_w0, rwkv_w1, rwkv_w2, rwkv_a0, rwkv_a1, rwkv_a2, rwkv_g1, rwkv_g2, rwkv_kk, rwkv_ka, rwkv_rk, rwkv_lnx_w, rwkv_lnx_b, attn_wq, attn_wo, kv_norm, w_kv, rel_bias, final_norm):
    w = _prep_weights(norm_w, ffn1_wi, ffn1_wo, ffn2_wi, ffn2_wo, pe_proj, pe_gate,
                      rwkv_mix, rwkv_wrkv, rwkv_wo, rwkv_w0, rwkv_w1, rwkv_w2, rwkv_a0, rwkv_a1, rwkv_a2,
                      rwkv_g1, rwkv_g2, rwkv_kk, rwkv_ka, rwkv_rk, rwkv_lnx_w, rwkv_lnx_b,
                      attn_wq, attn_wo, kv_norm, w_kv, rel_bias, final_norm)
    n_a = norm_w.shape[0] // 2
    nb, seq_len, _ = x_prompt.shape
    wkv0 = jnp.zeros((n_a, nb, N_HEADS, HEAD_DIM, HEAD_DIM), F32)
    shift0 = jnp.zeros((n_a, nb, D_MODEL), x_prompt.dtype)
    y_p, wkv_p, shift_p, kv_p = _trunk(x_prompt, p_prompt, wkv0, shift0, None, w)
    kv_prompt = kv_p[:, seq_len - min(MAX_WINDOW, seq_len):]
    cache = cache_kv.reshape(cache_kv.shape[0], cache_kv.shape[1], 2 * KV_HEADS * HEAD_DIM).astype(F32)
    y_s, wkv_s, shift_s, kv_s = _trunk(x_sample, p_sample, state_wkv, state_shift, cache, w)
    return (y_p, y_s, wkv_p, shift_p, kv_prompt, wkv_s, shift_s, kv_s)
```

```python
import functools

import numpy as np
import jax
import jax.numpy as jnp
from jax import lax
from jax.experimental import pallas as pl
from jax.experimental.pallas import tpu as pltpu

F32 = jnp.float32
BF16 = jnp.bfloat16

D_MODEL = 1024
D_FF = 2816
PLE_DIM = 256
RMS_EPS = 1e-6
HEAD_DIM = 64
N_HEADS = D_MODEL // HEAD_DIM
LNX_EPS = 64e-5
KV_HEADS = 4
Q_PER_KV = N_HEADS // KV_HEADS
DILATION_GROUPS = ((128, 1), (512, 4), (2048, 16))
N_GROUPS = len(DILATION_GROUPS)
MAX_WINDOW = 2048
REL_BUCKETS = 32
REL_MAX_DIST = 2048
NEG_INF = -1e30

LANES = 128
FFN_CHUNK = 256
N_FFN_CHUNKS = D_FF // FFN_CHUNK
TOKEN_TILE = 256
FFN_TILE = 512
WKV_CHUNK = 64
Q_BLOCK = 128
ATTN_SPAN = 2048
VMEM_LIMIT = 56 * 1024 * 1024
EXP_MINUS_HALF = 0.6065306597126334


def _params(*sem):
    return pltpu.CompilerParams(dimension_semantics=sem, vmem_limit_bytes=VMEM_LIMIT)


def _const_spec(shape):
    return pl.BlockSpec(shape, lambda *_: (0,) * len(shape))


def _tile(n, pref=TOKEN_TILE):
    t = min(n, pref)
    while n % t:
        t -= 8
    return t


def _rms(x, g):
    return x * lax.rsqrt(jnp.mean(x * x, axis=-1, keepdims=True) + RMS_EPS) * g


def _bdot(a, b):
    return jnp.dot(a.astype(BF16), b, preferred_element_type=F32)


def _dot_split(x, w):
    hi = x.astype(BF16)
    lo = (x - hi.astype(F32)).astype(BF16)
    return (jnp.dot(hi, w, preferred_element_type=F32) + jnp.dot(lo, w, preferred_element_type=F32))


def _head_sum(x, e_ref, et_ref):
    return _dot_split(_dot_split(x, e_ref[...]), et_ref[...])


def _ffn_kernel(x_ref, g_ref, wi_ref, wo_ref, *rest, with_pe, final):
    o_ref = rest[-1]
    x = x_ref[...]
    xn = _rms(x, g_ref[...]).astype(BF16)
    acc = jnp.zeros_like(x)
    for j in range(N_FFN_CHUNKS):
        lo, hi = j * FFN_CHUNK, (j + 1) * FFN_CHUNK
        gate = jnp.dot(xn, wi_ref[:, lo:hi], preferred_element_type=F32)
        up = jnp.dot(xn, wi_ref[:, D_FF + lo:D_FF + hi], preferred_element_type=F32)
        act = (gate * jax.nn.sigmoid(gate) * up).astype(BF16)
        acc = acc + jnp.dot(act, wo_ref[lo:hi, :], preferred_element_type=F32)
    y = x + 0.5 * acc
    if with_pe:
        p_ref, gp_ref, wgate_ref, wproj_ref, gf_ref = rest[:-1]
        gate = jax.nn.sigmoid(_bdot(_rms(y, gp_ref[...]), wgate_ref[...]))
        y = y + gate * _bdot(p_ref[...], wproj_ref[...])
        if final:
            y = _rms(y, gf_ref[...])
    o_ref[...] = y


def _ffn(h, g, wi, wo, pe=None, final=False):
    n = h.shape[0]
    tm = _tile(n, FFN_TILE)
    row_spec = pl.BlockSpec((tm, D_MODEL), lambda i: (i, 0))
    args = [h, g, wi, wo]
    in_specs = [row_spec, _const_spec((1, D_MODEL)), _const_spec(wi.shape), _const_spec(wo.shape)]
    if pe is not None:
        args += list(pe)
        in_specs += [pl.BlockSpec((tm, PLE_DIM), lambda i: (i, 0))] + [_const_spec(a.shape) for a in pe[1:]]
    return pl.pallas_call(
        functools.partial(_ffn_kernel, with_pe=pe is not None, final=final),
        grid=(n // tm,),
        in_specs=in_specs,
        out_specs=row_spec,
        out_shape=jax.ShapeDtypeStruct((n, D_MODEL), F32),
        compiler_params=_params("arbitrary"),
        name="ffn_pe" if pe is not None else "ffn",
    )(*args)


def _norm_mm_kernel(x_ref, g_ref, w_ref, o_ref):
    o_ref[...] = _bdot(_rms(x_ref[...], g_ref[...]), w_ref[...]).astype(o_ref.dtype)


def _norm_mm(h, g, w, out_dtype, col_tile=None, name="norm_mm"):
    n = h.shape[0]
    tm = _tile(n)
    nout = w.shape[1]
    tn = nout if col_tile is None else col_tile
    return pl.pallas_call(
        _norm_mm_kernel,
        grid=(n // tm, nout // tn),
        in_specs=[pl.BlockSpec((tm, D_MODEL), lambda i, j: (i, 0)), _const_spec((1, D_MODEL)),
                  pl.BlockSpec((D_MODEL, tn), lambda i, j: (0, j))],
        out_specs=pl.BlockSpec((tm, tn), lambda i, j: (i, j)),
        out_shape=jax.ShapeDtypeStruct((n, nout), out_dtype),
        compiler_params=_params("arbitrary", "arbitrary"),
        name=name,
    )(h, g, w)


def _rwkv_layer_kernel(h_ref, sh_ref, s0_ref, nw_ref, mix_ref, wrkv_ref, w0_ref, w1_ref, w2_ref, a0_ref, a1_ref,
                       a2_ref, g1_ref, g2_ref, kk_ref, ka_ref, lnw_ref, lnb_ref, rk_ref, wo_ref, e_ref, et_ref,
                       o_ref, hn_o, st_ref,
                       state, prev, r_s, lw_s, k_s, v_s, kk_s, b_s, g_s, y_s, *, rows, n_sub):
    c = pl.program_id(1)
    C = WKV_CHUNK
    C2 = 2 * C
    n_pairs = N_HEADS // 2
    padded = n_sub * C

    @pl.when(c == 0)
    def _():
        state[...] = jnp.zeros_like(state)
        for h in range(N_HEADS):
            lo = (h % 2) * HEAD_DIM
            state[h // 2, lo:lo + HEAD_DIM, lo:lo + HEAD_DIM] = s0_ref[h]
        prev[...] = sh_ref[...]
        if rows < padded:
            for ref in (r_s, lw_s, k_s, v_s, kk_s, b_s):
                ref[...] = jnp.zeros_like(ref)

    nw = nw_ref[...]
    hn = _rms(h_ref[...], nw)
    row = lax.broadcasted_iota(jnp.int32, hn.shape, 0)
    x_prev = jnp.where(row == 0, prev[...], pltpu.roll(hn, 1, 0))
    prev[...] = hn[rows - 1:rows]
    hn_o[...] = hn[rows - 1:rows]
    xx = x_prev - hn
    mix = mix_ref[...]
    xr, xw, xk, xv, xa, xg = (hn + xx * mix[j:j + 1] for j in range(6))
    r = _bdot(xr, wrkv_ref[0])
    k = _bdot(xk, wrkv_ref[1])
    v = _bdot(xv, wrkv_ref[2])
    wl = w0_ref[...] + _bdot(jnp.tanh(_bdot(xw, w1_ref[...])), w2_ref[...])
    lw_s[0:rows] = -EXP_MINUS_HALF * jax.nn.sigmoid(wl)
    a = jax.nn.sigmoid(a0_ref[...] + _bdot(_bdot(xa, a1_ref[...]), a2_ref[...]))
    g_s[...] = _bdot(jax.nn.sigmoid(_bdot(xg, g1_ref[...])), g2_ref[...])
    kkv = k * kk_ref[...]
    kk = kkv / jnp.maximum(jnp.sqrt(_head_sum(kkv * kkv, e_ref, et_ref)), 1e-12)
    r_s[0:rows] = r
    k_s[0:rows] = k * (1.0 + (a - 1.0) * ka_ref[...])
    v_s[0:rows] = v
    kk_s[0:rows] = kk
    b_s[0:rows] = kk * a

    ri = lax.broadcasted_iota(jnp.int32, (C, C), 0)
    ci = lax.broadcasted_iota(jnp.int32, (C, C), 1)
    tri = (ri >= ci).astype(BF16)
    r2 = lax.broadcasted_iota(jnp.int32, (C2, C2), 0)
    c2 = lax.broadcasted_iota(jnp.int32, (C2, C2), 1)
    strict = r2 > c2
    incl = r2 >= c2
    lane = lax.broadcasted_iota(jnp.int32, (C, LANES), 1)
    head0 = lane < HEAD_DIM

    def hat(x):
        x3 = jnp.stack([x[:, p * LANES:(p + 1) * LANES] for p in range(n_pairs)])
        return jnp.concatenate([jnp.where(head0, x3, 0.0), jnp.where(head0, 0.0, x3)], axis=1)

    def mm(a, b):
        return jnp.einsum("pmk,pkn->pmn", a.astype(BF16), b.astype(BF16), preferred_element_type=F32)

    def mm_nt(a, b):
        return jnp.einsum("pmk,pnk->pmn", a.astype(BF16), b.astype(BF16), preferred_element_type=F32)

    def sub_chunk(s, carry):
        rws = pl.ds(pl.multiple_of(s * C, C), C)
        lw = lw_s[rws, :]
        p1 = lw.astype(BF16)
        rem = lw - p1.astype(F32)
        p2 = rem.astype(BF16)
        p3 = (rem - p2.astype(F32)).astype(BF16)
        cw = (jnp.dot(tri, p1, preferred_element_type=F32) + jnp.dot(tri, p2, preferred_element_type=F32)
              + jnp.dot(tri, p3, preferred_element_type=F32))
        cw_end = cw[C - 1:C, :]
        e_neg = jnp.exp(-cw)
        e_end = jnp.exp(cw_end - cw)
        kk_c = kk_s[rws, :]
        bb = b_s[rws, :]
        k_c = k_s[rws, :]
        a_all = kk_c * jnp.exp(cw - lw)
        r_all = r_s[rws, :] * jnp.exp(cw)
        b_all = bb * e_neg
        k_all = k_c * e_neg
        bd_all = bb * e_end
        kd_all = k_c * e_end
        v_all = v_s[rws, :]
        decay = jnp.exp(cw_end)
        ar_h = jnp.concatenate([hat(a_all), hat(r_all)], axis=1)
        bk_h = jnp.concatenate([hat(b_all), hat(k_all)], axis=1)
        v_h = hat(v_all)
        st = state[...]
        g = mm_nt(ar_h, bk_h)
        low = jnp.where(strict, g[:, :C2, :C2], 0.0)
        ak = jnp.where(strict, g[:, :C2, C2:], 0.0)
        rbk = jnp.concatenate([jnp.where(incl, g[:, C2:, :C2], 0.0), jnp.where(incl, g[:, C2:, C2:], 0.0)], axis=2)
        ss = mm_nt(ar_h, st)
        x = -(ss[:, :C2] + mm(ak, v_h))
        t = mm(low, jnp.concatenate([low, x], axis=2))
        lp = t[:, :, :C2]
        x = x - t[:, :, C2:]
        for _ in range(4):
            t = mm(lp, jnp.concatenate([lp, x], axis=2))
            lp = t[:, :, :C2]
            x = x + t[:, :, C2:]
        x = x + mm(lp, x)
        xv = jnp.concatenate([x, v_h], axis=1)
        y_h = ss[:, C2:] + mm(rbk, xv)
        y = y_h[:, :C] + y_h[:, C:]
        for p in range(n_pairs):
            y_s[rws, p * LANES:(p + 1) * LANES] = y[p]
        bkd_h = jnp.concatenate([hat(bd_all), hat(kd_all)], axis=1)
        xv_t = jnp.stack([xv[p].T for p in range(n_pairs)])
        dec3 = jnp.stack([decay[:, p * LANES:(p + 1) * LANES] for p in range(n_pairs)])
        state[...] = st * dec3 + mm(xv_t, bkd_h)
        return carry

    lax.fori_loop(0, n_sub, sub_chunk, 0)

    y = y_s[0:rows]
    inv_n = 1.0 / HEAD_DIM
    mu = _head_sum(y, e_ref, et_ref) * inv_n
    yc = y - mu
    var = _head_sum(yc * yc, e_ref, et_ref) * inv_n
    yn = yc * lax.rsqrt(var + LNX_EPS) * lnw_ref[...] + lnb_ref[...]
    bonus = _head_sum(r_s[0:rows] * k_s[0:rows] * rk_ref[...], e_ref, et_ref) * v_s[0:rows]
    o_ref[...] = h_ref[...] + _bdot((yn + bonus) * g_s[...], wo_ref[...])

    @pl.when(c == pl.num_programs(1) - 1)
    def _():
        for h in range(N_HEADS):
            lo = (h % 2) * HEAD_DIM
            st_ref[h] = state[h // 2, lo:lo + HEAD_DIM, lo:lo + HEAD_DIM]


def _rwkv_layer(h, shift, s0, seq_len, lw):
    n = h.shape[0]
    nb = n // seq_len
    rows = _tile(seq_len)
    steps = seq_len // rows
    n_sub = -(-rows // WKV_CHUNK)
    padded = n_sub * WKV_CHUNK
    assert rows == padded or n_sub == 1
    row_spec = pl.BlockSpec((rows, D_MODEL), lambda bi, ci: (bi * steps + ci, 0))
    sh_spec = pl.BlockSpec((None, 1, D_MODEL), lambda bi, ci: (bi, 0, 0))
    st_spec = pl.BlockSpec((None, N_HEADS, HEAD_DIM, HEAD_DIM), lambda bi, ci: (bi, 0, 0, 0))
    consts = [lw["nw"], lw["mix"], lw["wrkv"], lw["w0"], lw["w1"], lw["w2"], lw["a0"], lw["a1"], lw["a2"],
              lw["g1"], lw["g2"], lw["kk"], lw["ka"], lw["lnw"], lw["lnb"], lw["rk"], lw["wo"], lw["e"], lw["et"]]
    seq_buf = pltpu.VMEM((padded, D_MODEL), F32)
    out, sh, st = pl.pallas_call(
        functools.partial(_rwkv_layer_kernel, rows=rows, n_sub=n_sub),
        grid=(nb, steps),
        in_specs=[row_spec, sh_spec, st_spec] + [_const_spec(c.shape) for c in consts],
        out_specs=[row_spec, sh_spec, st_spec],
        out_shape=[jax.ShapeDtypeStruct((n, D_MODEL), F32), jax.ShapeDtypeStruct((nb, 1, D_MODEL), F32),
                   jax.ShapeDtypeStruct(s0.shape, F32)],
        scratch_shapes=[pltpu.VMEM((N_HEADS // 2, LANES, LANES), F32), pltpu.VMEM((1, D_MODEL), F32)]
                       + [seq_buf] * 6 + [pltpu.VMEM((rows, D_MODEL), F32), seq_buf],
        compiler_params=_params("arbitrary", "arbitrary"),
        name="rwkv_layer",
    )(h, shift.reshape(nb, 1, D_MODEL), s0, *consts)
    return out, sh.reshape(nb, D_MODEL), st


def _t5_buckets(dist):
    d = np.asarray(dist, dtype=np.int64)
    max_exact = REL_BUCKETS // 2
    large = max_exact + (np.log(np.maximum(d, 1) / max_exact) / np.log(REL_MAX_DIST / max_exact)
                         * (REL_BUCKETS - max_exact)).astype(np.int32)
    large = np.minimum(large, REL_BUCKETS - 1)
    return np.where(d < max_exact, d, large).astype(np.int32)


def _toeplitz(tab, n_rows, n_cols):
    period = tab.shape[-1]
    assert period >= n_rows + n_cols - 1 and n_cols <= period - 1
    lead = tab.shape[:-1]
    flat = jnp.broadcast_to(tab[..., None, :], lead + (n_rows, period)).reshape(lead + (-1,))
    skew = flat[..., :n_rows * (period - 1)].reshape(lead + (n_rows, period - 1))
    return skew[..., :n_cols]


def _band_bias(rel_bias, group):
    win, dil = DILATION_GROUPS[group]
    assert win // dil == Q_BLOCK
    period = 3 * Q_BLOCK - 1
    idx = np.arange(period)
    m = Q_BLOCK - np.where(idx < 2 * Q_BLOCK, idx, idx - period)
    valid = (m >= 0) & (m <= Q_BLOCK)
    buckets = _t5_buckets(dil * np.clip(m, 0, Q_BLOCK))
    tbl = jnp.take(rel_bias[:, group * N_HEADS:(group + 1) * N_HEADS].astype(F32), buckets, axis=0).T
    general = _toeplitz(jnp.where(valid[None], tbl, NEG_INF), Q_BLOCK, 2 * Q_BLOCK)
    first = jnp.where((np.arange(2 * Q_BLOCK) >= Q_BLOCK)[None, None, :], general, NEG_INF)
    return jnp.stack([first, general])


_HEAD_OF_SEG = np.array([4 * (2 * (s // 8) + s % 2) + (s // 2) % 4 for s in range(N_HEADS)])


def _proj_rm_kernel(x_ref, g_ref, w_ref, *refs, natural, lane_ranges, tm):
    scr = refs[-1]
    outs = refs[:-1]
    y = _bdot(_rms(x_ref[...], g_ref[...]), w_ref[...])
    if natural:
        outs[0][...] = y
        outs = outs[1:]
    for c in range(scr.shape[0]):
        scr[c] = y[:, c * LANES:(c + 1) * LANES]
    for gi, (_, dil) in enumerate(DILATION_GROUPS):
        lo, hi = lane_ranges[gi]
        if dil == 1:
            outs[gi][0] = y[:, lo:hi].astype(BF16)
            continue
        for rho in range(dil):
            rows = [scr[c, pl.ds(rho, tm // dil, stride=dil), :] for c in range(lo // LANES, hi // LANES)]
            outs[gi][rho] = jnp.concatenate(rows, axis=1).astype(BF16)


def _proj_rm(h, g, w, nb, seq_len, natural, lane_ranges, name):
    n = h.shape[0]
    tm = _tile(n)
    nout = w.shape[1]
    assert seq_len % tm == 0 and all(tm % (16 * dil) == 0 for _, dil in DILATION_GROUPS)
    tps = seq_len // tm
    out_shape, out_specs = [], []
    if natural:
        out_shape.append(jax.ShapeDtypeStruct((n, nout), F32))
        out_specs.append(pl.BlockSpec((tm, nout), lambda i: (i, 0)))
    for (_, dil), (lo, hi) in zip(DILATION_GROUPS, lane_ranges):
        out_shape.append(jax.ShapeDtypeStruct((nb, dil, seq_len // dil, hi - lo), BF16))
        out_specs.append(pl.BlockSpec((None, dil, tm // dil, hi - lo), lambda i: (i // tps, 0, i % tps, 0)))
    return pl.pallas_call(
        functools.partial(_proj_rm_kernel, natural=natural, lane_ranges=lane_ranges, tm=tm),
        grid=(n // tm,),
        in_specs=[pl.BlockSpec((tm, D_MODEL), lambda i: (i, 0)), _const_spec((1, D_MODEL)), _const_spec(w.shape)],
        out_specs=out_specs,
        out_shape=out_shape,
        scratch_shapes=[pltpu.VMEM((nout // LANES, tm, LANES), F32)],
        compiler_params=_params("arbitrary"),
        name=name,
    )(h, g, w)


def _attn_kernel(q_ref, kv_ref, halo_ref, bias_ref, o_ref, lse_ref, kvbuf, *, nq):
    i = pl.program_id(0)
    dil = q_ref.shape[0]
    kvd = KV_HEADS * HEAD_DIM
    nt = (((1,), (1,)), ((), ()))
    kvbuf[:, :Q_BLOCK, :] = halo_ref[...]
    kvbuf[:, Q_BLOCK:, :] = kv_ref[...]
    lane = lax.broadcasted_iota(jnp.int32, (Q_BLOCK, LANES), 1)
    low_half = lane < HEAD_DIM
    lane_row = lax.broadcasted_iota(jnp.int32, (1, LANES), 1)
    keep_lo = (lane_row < HEAD_DIM).astype(BF16)
    keep_hi = (lane_row >= HEAD_DIM).astype(BF16)

    def block(u, carry):
        rho = lax.div(u, nq)
        j = u - rho * nq
        r0 = pl.multiple_of(j * Q_BLOCK, Q_BLOCK)
        qb = q_ref[rho, pl.ds(r0, Q_BLOCK), :]
        kvb = kvbuf[rho, pl.ds(r0, 2 * Q_BLOCK), :]
        bsel = jnp.where(jnp.logical_and(i == 0, j == 0), 0, 1)
        lse_tile = jnp.zeros((Q_BLOCK, LANES), F32)
        for G in range(KV_HEADS // 2):
            kg = kvb[:, G * LANES:(G + 1) * LANES]
            vg = kvb[:, kvd + G * LANES:kvd + (G + 1) * LANES]
            vcat = jnp.concatenate([vg * keep_lo, vg * keep_hi], axis=0)
            pieces = []
            for r in range(Q_PER_KV):
                qg = qb[:, (G * Q_PER_KV + r) * LANES:(G * Q_PER_KV + r + 1) * LANES]
                pieces += [qg * keep_lo, qg * keep_hi]
            s_all = lax.dot_general(jnp.concatenate(pieces, axis=0), kg, nt, preferred_element_type=F32)
            for r in range(Q_PER_KV):
                grp = G * Q_PER_KV + r
                parts = []
                for half in range(2):
                    s = s_all[(2 * r + half) * Q_BLOCK:(2 * r + half + 1) * Q_BLOCK]
                    s = s + bias_ref[bsel, int(_HEAD_OF_SEG[2 * grp + half])]
                    m = jnp.max(s, axis=-1, keepdims=True)
                    p = jnp.exp(s - m)
                    l = jnp.sum(p, axis=-1, keepdims=True)
                    parts.append((p.astype(BF16), l, m + jnp.log(l)))
                (p0, l0, e0), (p1, l1, e1) = parts
                o = jnp.dot(jnp.concatenate([p0, p1], axis=1), vcat, preferred_element_type=F32)
                o = (o / jnp.where(low_half, l0, l1)).astype(o_ref.dtype)
                o_ref[rho, pl.ds(r0, Q_BLOCK), grp * LANES:(grp + 1) * LANES] = o
                lse_tile = jnp.where(lane == 2 * grp, e0, lse_tile)
                lse_tile = jnp.where(lane == 2 * grp + 1, e1, lse_tile)
        lse_ref[rho, pl.ds(r0, Q_BLOCK), :] = lse_tile
        return carry

    lax.fori_loop(0, dil * nq, block, 0)


def _attn_group(q_rm, kv_rm, bias):
    nb, dil, tsub, _ = q_rm.shape
    kvw = kv_rm.shape[-1]
    rows = ATTN_SPAN // dil
    nq = rows // Q_BLOCK
    assert tsub % rows == 0 and nq >= 1
    span_spec = lambda width: pl.BlockSpec((None, dil, rows, width), lambda i, b: (b, 0, i, 0))
    return pl.pallas_call(
        functools.partial(_attn_kernel, nq=nq),
        grid=(tsub // rows, nb),
        in_specs=[span_spec(D_MODEL), span_spec(kvw),
                  pl.BlockSpec((None, dil, Q_BLOCK, kvw), lambda i, b: (b, 0, jnp.maximum(i * nq - 1, 0), 0)),
                  _const_spec(bias.shape)],
        out_specs=[span_spec(D_MODEL), span_spec(LANES)],
        out_shape=[jax.ShapeDtypeStruct((nb, dil, tsub, D_MODEL), BF16),
                   jax.ShapeDtypeStruct((nb, dil, tsub, LANES), F32)],
        scratch_shapes=[pltpu.VMEM((dil, Q_BLOCK + rows, kvw), BF16)],
        compiler_params=_params("arbitrary", "arbitrary"),
        name=f"attn_d{dil}",
    )(q_rm, kv_rm, kv_rm, bias)


def _attn_out_kernel(o0_ref, o1_ref, o2_ref, l0_ref, l1_ref, l2_ref, h_ref, wo_ref, et_ref, out_ref, *scr, tm):
    outs, lses = [], []
    for gi, (o_ref, l_ref) in enumerate(((o0_ref, l0_ref), (o1_ref, l1_ref), (o2_ref, l2_ref))):
        dil = DILATION_GROUPS[gi][1]
        if dil == 1:
            outs.append(o_ref[0].astype(F32))
            lses.append(l_ref[0])
            continue
        so, sl = scr[2 * gi], scr[2 * gi + 1]
        n_tiles = so.shape[0]
        for rho in range(dil):
            rows = pl.ds(rho, tm // dil, stride=dil)
            for c in range(n_tiles):
                so[c, rows, :] = o_ref[rho, :, c * LANES:(c + 1) * LANES].astype(F32)
            sl[rows, :] = l_ref[rho]
        outs.append(jnp.concatenate([so[c] for c in range(n_tiles)], axis=1))
        lses.append(sl[...])
    l0, l1, l2 = lses
    m = jnp.maximum(jnp.maximum(l0, l1), l2)
    w0, w1, w2 = jnp.exp(l0 - m), jnp.exp(l1 - m), jnp.exp(l2 - m)
    inv = 1.0 / (w0 + w1 + w2)
    et = et_ref[...]
    att = (_dot_split(w0 * inv, et) * outs[0] + _dot_split(w1 * inv, et) * outs[1]
           + _dot_split(w2 * inv, et) * outs[2])
    out_ref[...] = h_ref[...] + _bdot(att, wo_ref[...])


def _attn_out(outs, lses, h, wo, et, seq_len):
    n = h.shape[0]
    tm = _tile(n)
    tps = seq_len // tm
    row_spec = pl.BlockSpec((tm, D_MODEL), lambda i: (i, 0))

    def rm_spec(dil, width):
        return pl.BlockSpec((None, dil, tm // dil, width), lambda i: (i // tps, 0, i % tps, 0))

    dils = [dil for _, dil in DILATION_GROUPS]
    scratch = []
    for _ in dils:
        scratch += [pltpu.VMEM((D_MODEL // LANES, tm, LANES), F32), pltpu.VMEM((tm, LANES), F32)]
    return pl.pallas_call(
        functools.partial(_attn_out_kernel, tm=tm),
        grid=(n // tm,),
        in_specs=[rm_spec(d, D_MODEL) for d in dils] + [rm_spec(d, LANES) for d in dils]
                 + [row_spec, _const_spec(wo.shape), _const_spec(et.shape)],
        out_specs=row_spec,
        out_shape=jax.ShapeDtypeStruct((n, D_MODEL), F32),
        scratch_shapes=scratch,
        compiler_params=_params("arbitrary"),
        name="attn_out",
    )(*outs, *lses, h, wo, et)


def _decode_bias(rel_bias, seq_len, cache_len):
    ncol = cache_len + LANES
    period = seq_len + ncol - 1
    idx = np.arange(period)
    dist = cache_len - np.where(idx < ncol, idx, idx - period)
    buckets = _t5_buckets(np.clip(dist, 0, MAX_WINDOW))
    tabs = []
    for g, (win, dil) in enumerate(DILATION_GROUPS):
        valid = (dist >= 0) & (dist % dil == 0) & (dist <= win)
        tbl = jnp.take(rel_bias[:, g * N_HEADS:(g + 1) * N_HEADS].astype(F32), buckets, axis=0).T
        tabs.append(jnp.where(valid[None], tbl, NEG_INF))
    rows = _toeplitz(jnp.stack(tabs), seq_len, ncol)
    bias = rows.reshape(N_GROUPS * N_HEADS * seq_len, ncol)
    return bias[:, :cache_len], bias[:, cache_len:]


def _attn_decode_kernel(q_ref, cache_ref, kvn_ref, h_ref, bc_ref, bn_ref, wo_ref, out_ref, *, seq_len):
    kvd = KV_HEADS * HEAD_DIM
    nslot = N_GROUPS * N_HEADS
    rows_g = N_HEADS * seq_len
    nt = (((1,), (1,)), ((), ()))
    cache = cache_ref[...]
    kc = cache[:, :kvd].astype(BF16)
    vc = cache[:, kvd:].astype(BF16)
    kvn = kvn_ref[...]
    pad = jnp.zeros((LANES - seq_len, kvd), F32)
    kn = jnp.concatenate([kvn[:, :kvd], pad], axis=0).astype(BF16)
    vn = jnp.concatenate([kvn[:, kvd:], pad], axis=0).astype(BF16)
    lhs = jnp.concatenate([q_ref[:, s * kvd:(s + 1) * kvd] for s in range(nslot)], axis=0).astype(BF16)
    sc = lax.dot_general(lhs, kc, nt, preferred_element_type=F32) + bc_ref[...]
    sn = lax.dot_general(lhs, kn, nt, preferred_element_type=F32) + bn_ref[...]
    m_g = jnp.maximum(jnp.max(sc, axis=-1, keepdims=True), jnp.max(sn, axis=-1, keepdims=True))
    m = jnp.maximum(jnp.maximum(m_g[:rows_g], m_g[rows_g:2 * rows_g]), m_g[2 * rows_g:])
    m3 = jnp.concatenate([m, m, m], axis=0)
    pc = jnp.exp(sc - m3)
    pn = jnp.exp(sn - m3)
    l_g = jnp.sum(pc, axis=-1, keepdims=True) + jnp.sum(pn, axis=-1, keepdims=True)
    num_g = (jnp.dot(pc.astype(BF16), vc, preferred_element_type=F32)
             + jnp.dot(pn.astype(BF16), vn, preferred_element_type=F32))
    l = l_g[:rows_g] + l_g[rows_g:2 * rows_g] + l_g[2 * rows_g:]
    num = num_g[:rows_g] + num_g[rows_g:2 * rows_g] + num_g[2 * rows_g:]
    row = lax.broadcasted_iota(jnp.int32, (rows_g, kvd), 0)
    lane = lax.broadcasted_iota(jnp.int32, (rows_g, kvd), 1)
    own = (row // (Q_PER_KV * seq_len)) == (lane // HEAD_DIM)
    att = jnp.where(own, num / l, 0.0)
    out = h_ref[...]
    for r in range(Q_PER_KV):
        a_r = att[r * seq_len:(r + 1) * seq_len]
        for c in range(1, KV_HEADS):
            a_r = a_r + att[(c * Q_PER_KV + r) * seq_len:(c * Q_PER_KV + r + 1) * seq_len]
        out = out + _bdot(a_r, wo_ref[r])
    out_ref[...] = out


def _attn_decode(q, cache, kv_new, h, bias_c, bias_n, wo_r, nb, seq_len):
    cache_len = cache.shape[1]
    qw = q.shape[1]
    kvw = 2 * KV_HEADS * HEAD_DIM
    return pl.pallas_call(
        functools.partial(_attn_decode_kernel, seq_len=seq_len),
        grid=(nb,),
        in_specs=[pl.BlockSpec((seq_len, qw), lambda b: (b, 0)),
                  pl.BlockSpec((None, cache_len, kvw), lambda b: (b, 0, 0)),
                  pl.BlockSpec((seq_len, kvw), lambda b: (b, 0)),
                  pl.BlockSpec((seq_len, D_MODEL), lambda b: (b, 0)),
                  _const_spec(bias_c.shape), _const_spec(bias_n.shape), _const_spec(wo_r.shape)],
        out_specs=pl.BlockSpec((seq_len, D_MODEL), lambda b: (b, 0)),
        out_shape=jax.ShapeDtypeStruct((nb * seq_len, D_MODEL), F32),
        compiler_params=_params("arbitrary"),
        name="attn_decode",
    )(q, cache, kv_new, h, bias_c, bias_n, wo_r)


def _prep_weights(norm_w, ffn1_wi, ffn1_wo, ffn2_wi, ffn2_wo, pe_proj, pe_gate,
                  rwkv_mix, rwkv_wrkv, rwkv_wo, rwkv_w0, rwkv_w1, rwkv_w2, rwkv_a0, rwkv_a1, rwkv_a2,
                  rwkv_g1, rwkv_g2, rwkv_kk, rwkv_ka, rwkv_rk, rwkv_lnx_w, rwkv_lnx_b,
                  attn_wq, attn_wo, kv_norm, w_kv, rel_bias, final_norm):
    def row(v):
        return v.reshape(1, -1).astype(F32)

    def pad_cols(w, n):
        return jnp.pad(w, ((0, 0), (0, n - w.shape[1]))).astype(BF16)

    def pad_rows(w, n):
        return jnp.pad(w, ((0, n - w.shape[0]), (0, 0))).astype(BF16)

    def ffn_w(wi, wo):
        return wi.astype(BF16), wo.astype(BF16)

    head_of_lane = np.arange(D_MODEL) // HEAD_DIM
    e = jnp.asarray(head_of_lane[:, None] == np.arange(LANES)[None, :], BF16)
    et = jnp.asarray(np.arange(LANES)[:, None] == head_of_lane[None, :], BF16)

    depth = norm_w.shape[0]
    layers = []
    for i in range(depth):
        layers.append(dict(
            nw=[row(norm_w[i, j]) for j in range(4)],
            ffn1=ffn_w(ffn1_wi[i], ffn1_wo[i]), ffn2=ffn_w(ffn2_wi[i], ffn2_wo[i]),
            pe_gate=pe_gate[i].astype(BF16), pe_proj=pe_proj[i].astype(BF16)))
    n_a = depth // 2
    rw = []
    for i in range(n_a):
        rw.append(dict(
            nw=row(norm_w[i, 1]), mix=rwkv_mix[i].astype(F32), wrkv=rwkv_wrkv[i].astype(BF16),
            w0=row(rwkv_w0[i]), w1=pad_cols(rwkv_w1[i], LANES), w2=pad_rows(rwkv_w2[i], LANES),
            a0=row(rwkv_a0[i]), a1=pad_cols(rwkv_a1[i], LANES), a2=pad_rows(rwkv_a2[i], LANES),
            g1=pad_cols(rwkv_g1[i], 2 * LANES), g2=pad_rows(rwkv_g2[i], 2 * LANES),
            kk=row(rwkv_kk[i]), ka=row(rwkv_ka[i]), rk=row(rwkv_rk[i]),
            lnw=row(rwkv_lnx_w[i]), lnb=row(rwkv_lnx_b[i]), wo=rwkv_wo[i].astype(BF16), e=e, et=et))
    scale = HEAD_DIM ** -0.5
    at = []
    for j in range(depth - n_a):
        wq = attn_wq[j] * scale
        wq6 = wq.reshape(D_MODEL, N_GROUPS, KV_HEADS, Q_PER_KV, HEAD_DIM)
        wq_slots = jnp.einsum("dgcre,cx->dgcrxe", wq6, jnp.eye(KV_HEADS, dtype=wq.dtype))
        wq_slots = wq_slots.reshape(D_MODEL, N_GROUPS * N_HEADS * KV_HEADS * HEAD_DIM).astype(BF16)
        wo_r = attn_wo[j].reshape(KV_HEADS, Q_PER_KV, HEAD_DIM, D_MODEL).transpose(1, 0, 2, 3)
        wo_r = wo_r.reshape(Q_PER_KV, KV_HEADS * HEAD_DIM, D_MODEL).astype(BF16)
        wq_seg = wq.reshape(D_MODEL, N_GROUPS, N_HEADS, HEAD_DIM)[:, :, _HEAD_OF_SEG].reshape(D_MODEL, -1)
        wo_seg = attn_wo[j].reshape(N_HEADS, HEAD_DIM, D_MODEL)[_HEAD_OF_SEG].reshape(D_MODEL, D_MODEL)
        at.append(dict(wq_seg=wq_seg.astype(BF16), wq_slots=wq_slots, wo_seg=wo_seg.astype(BF16), wo_r=wo_r))
    return dict(layers=layers, rwkv=rw, attn=at, kv_norm=row(kv_norm), w_kv=w_kv.astype(BF16),
                final_norm=row(final_norm), rel_bias=rel_bias, et=et)


def _trunk(x, p, wkv0, shift0, cache, w):
    nb, seq_len, _ = x.shape
    n = nb * seq_len
    depth = len(w["layers"])
    n_a = depth // 2
    h = x.reshape(n, D_MODEL).astype(F32)
    wkv_out, shift_out = [], []
    kv_rows = kv_rm = None
    kvw = 2 * KV_HEADS * HEAD_DIM
    for i in range(depth):
        lw = w["layers"][i]
        if i == n_a:
            if cache is None:
                kv_rows, *kv_rm = _proj_rm(h, w["kv_norm"], w["w_kv"], nb, seq_len, True,
                                           [(0, kvw)] * N_GROUPS, "kv_proj")
            else:
                kv_rows = _norm_mm(h, w["kv_norm"], w["w_kv"], F32, name="kv_proj")
        h = _ffn(h, lw["nw"][0], *lw["ffn1"])
        if i < n_a:
            rwl = w["rwkv"][i]
            h, sh, st = _rwkv_layer(h, shift0[i].astype(F32), wkv0[i].astype(F32), seq_len, rwl)
            wkv_out.append(st.astype(x.dtype))
            shift_out.append(sh.astype(x.dtype))
        else:
            al = w["attn"][i - n_a]
            if cache is None:
                q_rm = _proj_rm(h, lw["nw"][1], al["wq_seg"], nb, seq_len, False,
                                [(gi * D_MODEL, (gi + 1) * D_MODEL) for gi in range(N_GROUPS)], "q_proj")
                outs, lses = [], []
                for gi in range(N_GROUPS):
                    o, lse = _attn_group(q_rm[gi], kv_rm[gi], _band_bias(w["rel_bias"], gi))
                    outs.append(o)
                    lses.append(lse)
                h = _attn_out(outs, lses, h, al["wo_seg"], w["et"], seq_len)
            else:
                q = _norm_mm(h, lw["nw"][1], al["wq_slots"], F32, col_tile=2048, name="q_proj_slots")
                bias_c, bias_n = _decode_bias(w["rel_bias"], seq_len, cache.shape[1])
                h = _attn_decode(q, cache, kv_rows, h, bias_c, bias_n, al["wo_r"], nb, seq_len)
        pe = (p[i].reshape(n, PLE_DIM).astype(F32), lw["nw"][3], lw["pe_gate"], lw["pe_proj"], w["final_norm"])
        h = _ffn(h, lw["nw"][2], *lw["ffn2"], pe=pe, final=(i == depth - 1))
    y = h.reshape(nb, seq_len, D_MODEL).astype(x.dtype)
    kv_rows = kv_rows.reshape(nb, seq_len, 2, KV_HEADS, HEAD_DIM).astype(x.dtype)
    return y, jnp.stack(wkv_out), jnp.stack(shift_out), kv_rows


def kernel(x_prompt, x_sample, state_wkv, state_shift, cache_kv, p_prompt, p_sample, norm_w, ffn1_wi, ffn1_wo, ffn2_wi, ffn2_wo, pe_proj, pe_gate, rwkv_mix, rwkv_wrkv, rwkv_wo, rwkv_w0, rwkv_w1, rwkv_w2, rwkv_a0, rwkv_a1, rwkv_a2, rwkv_g1, rwkv_g2, rwkv_kk, rwkv_ka, rwkv_rk, rwkv_lnx_w, rwkv_lnx_b, attn_wq, attn_wo, kv_norm, w_kv, rel_bias, final_norm):
    w = _prep_weights(norm_w, ffn1_wi, ffn1_wo, ffn2_wi, ffn2_wo, pe_proj, pe_gate,
                      rwkv_mix, rwkv_wrkv, rwkv_wo, rwkv_w0, rwkv_w1, rwkv_w2, rwkv_a0, rwkv_a1, rwkv_a2,
                      rwkv_g1, rwkv_g2, rwkv_kk, rwkv_ka, rwkv_rk, rwkv_lnx_w, rwkv_lnx_b,
                      attn_wq, attn_wo, kv_norm, w_kv, rel_bias, final_norm)
    n_a = norm_w.shape[0] // 2
    nb, seq_len, _ = x_prompt.shape
    wkv0 = jnp.zeros((n_a, nb, N_HEADS, HEAD_DIM, HEAD_DIM), F32)
    shift0 = jnp.zeros((n_a, nb, D_MODEL), x_prompt.dtype)
    y_p, wkv_p, shift_p, kv_p = _trunk(x_prompt, p_prompt, wkv0, shift0, None, w)
    kv_prompt = kv_p[:, seq_len - min(MAX_WINDOW, seq_len):]
    cache = cache_kv.reshape(cache_kv.shape[0], cache_kv.shape[1], 2 * KV_HEADS * HEAD_DIM).astype(F32)
    y_s, wkv_s, shift_s, kv_s = _trunk(x_sample, p_sample, state_wkv, state_shift, cache, w)
    return (y_p, y_s, wkv_p, shift_p, kv_prompt, wkv_s, shift_s, kv_s)
```

```python
import functools

import numpy as np
import jax
import jax.numpy as jnp
from jax import lax
from jax.experimental import pallas as pl
from jax.experimental.pallas import tpu as pltpu

F32 = jnp.float32
BF16 = jnp.bfloat16

D_MODEL = 1024
D_FF = 2816
PLE_DIM = 256
RMS_EPS = 1e-6
HEAD_DIM = 64
N_HEADS = D_MODEL // HEAD_DIM
LNX_EPS = 64e-5
KV_HEADS = 4
Q_PER_KV = N_HEADS // KV_HEADS
DILATION_GROUPS = ((128, 1), (512, 4), (2048, 16))
N_GROUPS = len(DILATION_GROUPS)
MAX_WINDOW = 2048
REL_BUCKETS = 32
REL_MAX_DIST = 2048
NEG_INF = -1e30

LANES = 128
FFN_CHUNK = 256
N_FFN_CHUNKS = D_FF // FFN_CHUNK
TOKEN_TILE = 256
FFN_TILE = 512
PROJ_TILE = 512
WKV_CHUNK = 64
RWKV_SHORT_SEQS = 8
Q_BLOCK = 128
ATTN_SPAN = 2048
VMEM_LIMIT = 56 * 1024 * 1024
EXP_MINUS_HALF = 0.6065306597126334


def _params(*sem):
    return pltpu.CompilerParams(dimension_semantics=sem, vmem_limit_bytes=VMEM_LIMIT)


def _const_spec(shape):
    return pl.BlockSpec(shape, lambda *_: (0,) * len(shape))


def _tile(n, pref=TOKEN_TILE):
    t = min(n, pref)
    while n % t:
        t -= 8
    return t


def _rms(x, g):
    return x * lax.rsqrt(jnp.mean(x * x, axis=-1, keepdims=True) + RMS_EPS) * g


def _bdot(a, b):
    return jnp.dot(a.astype(BF16), b, preferred_element_type=F32)


def _head_sum(x, e_ref, et_ref):
    return _bdot(_bdot(x, e_ref[...]), et_ref[...])


def _ffn_kernel(x_ref, g_ref, wi_ref, wo_ref, *rest, with_pe, final):
    o_ref = rest[-1]
    x = x_ref[...]
    xn = _rms(x, g_ref[...]).astype(BF16)
    acc = jnp.zeros_like(x)
    for j in range(N_FFN_CHUNKS):
        lo, hi = j * FFN_CHUNK, (j + 1) * FFN_CHUNK
        gate = jnp.dot(xn, wi_ref[:, lo:hi], preferred_element_type=F32)
        up = jnp.dot(xn, wi_ref[:, D_FF + lo:D_FF + hi], preferred_element_type=F32)
        act = (gate * jax.nn.sigmoid(gate) * up).astype(BF16)
        acc = acc + jnp.dot(act, wo_ref[lo:hi, :], preferred_element_type=F32)
    y = x + 0.5 * acc
    if with_pe:
        p_ref, gp_ref, wgate_ref, wproj_ref, gf_ref = rest[:-1]
        gate = jax.nn.sigmoid(_bdot(_rms(y, gp_ref[...]), wgate_ref[...]))
        y = y + gate * _bdot(p_ref[...], wproj_ref[...])
        if final:
            y = _rms(y, gf_ref[...])
    o_ref[...] = y


def _ffn(h, g, wi, wo, pe=None, final=False):
    n = h.shape[0]
    tm = _tile(n, FFN_TILE)
    row_spec = pl.BlockSpec((tm, D_MODEL), lambda i: (i, 0))
    args = [h, g, wi, wo]
    in_specs = [row_spec, _const_spec((1, D_MODEL)), _const_spec(wi.shape), _const_spec(wo.shape)]
    if pe is not None:
        args += list(pe)
        in_specs += [pl.BlockSpec((tm, PLE_DIM), lambda i: (i, 0))] + [_const_spec(a.shape) for a in pe[1:]]
    return pl.pallas_call(
        functools.partial(_ffn_kernel, with_pe=pe is not None, final=final),
        grid=(n // tm,),
        in_specs=in_specs,
        out_specs=row_spec,
        out_shape=jax.ShapeDtypeStruct((n, D_MODEL), F32),
        compiler_params=_params("arbitrary"),
        name="ffn_pe" if pe is not None else "ffn",
    )(*args)


def _norm_mm_kernel(x_ref, g_ref, w_ref, o_ref):
    o_ref[...] = _bdot(_rms(x_ref[...], g_ref[...]), w_ref[...]).astype(o_ref.dtype)


def _norm_mm(h, g, w, out_dtype, col_tile=None, name="norm_mm"):
    n = h.shape[0]
    tm = _tile(n)
    nout = w.shape[1]
    tn = nout if col_tile is None else col_tile
    return pl.pallas_call(
        _norm_mm_kernel,
        grid=(n // tm, nout // tn),
        in_specs=[pl.BlockSpec((tm, D_MODEL), lambda i, j: (i, 0)), _const_spec((1, D_MODEL)),
                  pl.BlockSpec((D_MODEL, tn), lambda i, j: (0, j))],
        out_specs=pl.BlockSpec((tm, tn), lambda i, j: (i, j)),
        out_shape=jax.ShapeDtypeStruct((n, nout), out_dtype),
        compiler_params=_params("arbitrary", "arbitrary"),
        name=name,
    )(h, g, w)


def _rwkv_layer_kernel(h_ref, sh_ref, s0_ref, nw_ref, mix_ref, wrkv_ref, w0_ref, w1_ref, w2_ref, a0_ref, a1_ref,
                       a2_ref, g1_ref, g2_ref, kk_ref, ka_ref, lnw_ref, lnb_ref, rk_ref, wo_ref, e_ref, et_ref,
                       o_ref, hn_o, st_ref,
                       state, prev, r_s, lw_s, k_s, v_s, kk_s, b_s, g_s, y_s, *, rows, n_sub, nseq):
    c = pl.program_id(1)
    C = WKV_CHUNK
    C2 = 2 * C
    n_pairs = N_HEADS // 2
    seq_rows = rows // nseq
    short = seq_rows % C != 0

    def load_state(s0):
        state[...] = jnp.zeros_like(state)
        for h in range(N_HEADS):
            lo = (h % 2) * HEAD_DIM
            state[h // 2, lo:lo + HEAD_DIM, lo:lo + HEAD_DIM] = s0[h]

    def store_state(st):
        for h in range(N_HEADS):
            lo = (h % 2) * HEAD_DIM
            st[h] = state[h // 2, lo:lo + HEAD_DIM, lo:lo + HEAD_DIM]

    if nseq == 1:
        @pl.when(c == 0)
        def _():
            load_state(s0_ref.at[0])
            prev[...] = sh_ref[...]
    if short:
        for ref in (r_s, lw_s, k_s, v_s, kk_s, b_s):
            ref[rows:, :] = jnp.zeros((ref.shape[0] - rows, D_MODEL), F32)

    nw = nw_ref[...]
    hn = _rms(h_ref[...], nw)
    row = lax.broadcasted_iota(jnp.int32, hn.shape, 0)
    rolled = pltpu.roll(hn, 1, 0)
    if nseq == 1:
        x_prev = jnp.where(row == 0, prev[...], rolled)
        prev[...] = hn[rows - 1:rows]
        hn_o[...] = hn[rows - 1:rows]
    else:
        x_prev = jnp.where(lax.rem(row, seq_rows) == 0, sh_ref[...], rolled)
        hn_o[...] = hn
    xx = x_prev - hn
    mix = mix_ref[...]
    xr, xw, xk, xv, xa, xg = (hn + xx * mix[j:j + 1] for j in range(6))
    r = _bdot(xr, wrkv_ref[0])
    k = _bdot(xk, wrkv_ref[1])
    v = _bdot(xv, wrkv_ref[2])
    wl = w0_ref[...] + _bdot(jnp.tanh(_bdot(xw, w1_ref[...])), w2_ref[...])
    lw_s[0:rows] = -EXP_MINUS_HALF * jax.nn.sigmoid(wl)
    a = jax.nn.sigmoid(a0_ref[...] + _bdot(_bdot(xa, a1_ref[...]), a2_ref[...]))
    g_s[...] = _bdot(jax.nn.sigmoid(_bdot(xg, g1_ref[...])), g2_ref[...])
    kkv = k * kk_ref[...]
    kk = kkv / jnp.maximum(jnp.sqrt(_head_sum(kkv * kkv, e_ref, et_ref)), 1e-12)
    r_s[0:rows] = r
    k_s[0:rows] = k * (1.0 + (a - 1.0) * ka_ref[...])
    v_s[0:rows] = v
    kk_s[0:rows] = kk
    b_s[0:rows] = kk * a

    ri = lax.broadcasted_iota(jnp.int32, (C, C), 0)
    ci = lax.broadcasted_iota(jnp.int32, (C, C), 1)
    tri = (ri >= ci).astype(BF16)
    r2 = lax.broadcasted_iota(jnp.int32, (C2, C2), 0)
    c2 = lax.broadcasted_iota(jnp.int32, (C2, C2), 1)
    strict = r2 > c2
    incl = r2 >= c2
    lane = lax.broadcasted_iota(jnp.int32, (C, LANES), 1)
    head0 = lane < HEAD_DIM

    def hat(x):
        x3 = jnp.stack([x[:, p * LANES:(p + 1) * LANES] for p in range(n_pairs)])
        return jnp.concatenate([jnp.where(head0, x3, 0.0), jnp.where(head0, 0.0, x3)], axis=1)

    def mm(a, b):
        return jnp.einsum("pmk,pkn->pmn", a.astype(BF16), b.astype(BF16), preferred_element_type=F32)

    def mm_nt(a, b):
        return jnp.einsum("pmk,pnk->pmn", a.astype(BF16), b.astype(BF16), preferred_element_type=F32)

    chunk_row = lax.broadcasted_iota(jnp.int32, (C, D_MODEL), 0)

    def chunk(ref, rws):
        x = ref[rws, :]
        return jnp.where(chunk_row < seq_rows, x, 0.0) if short else x

    def sub_chunk(s, carry):
        start = s * seq_rows if short else s * C
        rws = pl.ds(pl.multiple_of(start, 8), C)
        lw = chunk(lw_s, rws)
        p1 = lw.astype(BF16)
        rem = lw - p1.astype(F32)
        p2 = rem.astype(BF16)
        p3 = (rem - p2.astype(F32)).astype(BF16)
        cw = (jnp.dot(tri, p1, preferred_element_type=F32) + jnp.dot(tri, p2, preferred_element_type=F32)
              + jnp.dot(tri, p3, preferred_element_type=F32))
        cw_end = cw[C - 1:C, :]
        e_neg = jnp.exp(-cw)
        e_end = jnp.exp(cw_end - cw)
        kk_c = chunk(kk_s, rws)
        bb = chunk(b_s, rws)
        k_c = chunk(k_s, rws)
        a_all = kk_c * jnp.exp(cw - lw)
        r_all = chunk(r_s, rws) * jnp.exp(cw)
        b_all = bb * e_neg
        k_all = k_c * e_neg
        bd_all = bb * e_end
        kd_all = k_c * e_end
        v_all = chunk(v_s, rws)
        decay = jnp.exp(cw_end)
        ar_h = jnp.concatenate([hat(a_all), hat(r_all)], axis=1)
        bk_h = jnp.concatenate([hat(b_all), hat(k_all)], axis=1)
        v_h = hat(v_all)
        st = state[...]
        g = mm_nt(ar_h, bk_h)
        low = jnp.where(strict, g[:, :C2, :C2], 0.0)
        ak = jnp.where(strict, g[:, :C2, C2:], 0.0)
        rbk = jnp.concatenate([jnp.where(incl, g[:, C2:, :C2], 0.0), jnp.where(incl, g[:, C2:, C2:], 0.0)], axis=2)
        ss = mm_nt(ar_h, st)
        x = -(ss[:, :C2] + mm(ak, v_h))
        t = mm(low, jnp.concatenate([low, x], axis=2))
        lp = t[:, :, :C2]
        x = x - t[:, :, C2:]
        for _ in range(4):
            t = mm(lp, jnp.concatenate([lp, x], axis=2))
            lp = t[:, :, :C2]
            x = x + t[:, :, C2:]
        x = x + mm(lp, x)
        xv = jnp.concatenate([x, v_h], axis=1)
        y_h = ss[:, C2:] + mm(rbk, xv)
        y = y_h[:, :C] + y_h[:, C:]
        out_rows = pl.ds(pl.multiple_of(start, 8), seq_rows) if short else rws
        for p in range(n_pairs):
            y_s[out_rows, p * LANES:(p + 1) * LANES] = y[p, :seq_rows] if short else y[p]
        bkd_h = jnp.concatenate([hat(bd_all), hat(kd_all)], axis=1)
        xv_t = jnp.stack([xv[p].T for p in range(n_pairs)])
        dec3 = jnp.stack([decay[:, p * LANES:(p + 1) * LANES] for p in range(n_pairs)])
        state[...] = st * dec3 + mm(xv_t, bkd_h)
        return carry

    if nseq == 1:
        lax.fori_loop(0, n_sub, sub_chunk, 0)
    else:
        def one_sequence(s, carry):
            load_state(s0_ref.at[s])
            sub_chunk(s, carry)
            store_state(st_ref.at[s])
            return carry

        lax.fori_loop(0, nseq, one_sequence, 0)

    y = y_s[0:rows]
    inv_n = 1.0 / HEAD_DIM
    mu = _head_sum(y, e_ref, et_ref) * inv_n
    yc = y - mu
    var = _head_sum(yc * yc, e_ref, et_ref) * inv_n
    yn = yc * lax.rsqrt(var + LNX_EPS) * lnw_ref[...] + lnb_ref[...]
    bonus = _head_sum(r_s[0:rows] * k_s[0:rows] * rk_ref[...], e_ref, et_ref) * v_s[0:rows]
    o_ref[...] = h_ref[...] + _bdot((yn + bonus) * g_s[...], wo_ref[...])

    if nseq == 1:
        @pl.when(c == pl.num_programs(1) - 1)
        def _():
            store_state(st_ref.at[0])


def _rwkv_layer(h, shift, s0, seq_len, lw):
    n = h.shape[0]
    nb = n // seq_len
    if seq_len % WKV_CHUNK == 0:
        nseq, rows = 1, _tile(seq_len)
        steps, n_sub, buf_rows = seq_len // rows, rows // WKV_CHUNK, rows
        assert rows % WKV_CHUNK == 0
        sh_in = shift.reshape(nb, 1, D_MODEL)
        sh_spec = pl.BlockSpec((None, 1, D_MODEL), lambda bi, ci: (bi, 0, 0))
        hn_shape, hn_spec = jax.ShapeDtypeStruct((nb, 1, D_MODEL), F32), sh_spec
    else:
        assert seq_len < WKV_CHUNK and seq_len % 8 == 0
        nseq = _tile(nb, RWKV_SHORT_SEQS)
        rows, steps, n_sub = nseq * seq_len, 1, 1
        buf_rows = rows + WKV_CHUNK - seq_len
        sh_in = jnp.repeat(shift, seq_len, axis=0)
        sh_spec = pl.BlockSpec((rows, D_MODEL), lambda bi, ci: (bi, 0))
        hn_shape, hn_spec = jax.ShapeDtypeStruct((n, D_MODEL), F32), sh_spec
    row_spec = pl.BlockSpec((rows, D_MODEL), lambda bi, ci: (bi * steps + ci, 0))
    st_spec = pl.BlockSpec((nseq, N_HEADS, HEAD_DIM, HEAD_DIM), lambda bi, ci: (bi, 0, 0, 0))
    consts = [lw["nw"], lw["mix"], lw["wrkv"], lw["w0"], lw["w1"], lw["w2"], lw["a0"], lw["a1"], lw["a2"],
              lw["g1"], lw["g2"], lw["kk"], lw["ka"], lw["lnw"], lw["lnb"], lw["rk"], lw["wo"], lw["e"], lw["et"]]
    seq_buf = pltpu.VMEM((buf_rows, D_MODEL), F32)
    out, hn, st = pl.pallas_call(
        functools.partial(_rwkv_layer_kernel, rows=rows, n_sub=n_sub, nseq=nseq),
        grid=(nb // nseq, steps),
        in_specs=[row_spec, sh_spec, st_spec] + [_const_spec(c.shape) for c in consts],
        out_specs=[row_spec, hn_spec, st_spec],
        out_shape=[jax.ShapeDtypeStruct((n, D_MODEL), F32), hn_shape, jax.ShapeDtypeStruct(s0.shape, F32)],
        scratch_shapes=[pltpu.VMEM((N_HEADS // 2, LANES, LANES), F32), pltpu.VMEM((1, D_MODEL), F32)]
                       + [seq_buf] * 6 + [pltpu.VMEM((rows, D_MODEL), F32), seq_buf],
        compiler_params=_params("arbitrary", "arbitrary"),
        name="rwkv_layer",
    )(h, sh_in, s0, *consts)
    shift_out = hn.reshape(nb, D_MODEL) if nseq == 1 else hn.reshape(nb, seq_len, D_MODEL)[:, -1]
    return out, shift_out, st


def _t5_buckets(dist):
    d = np.asarray(dist, dtype=np.int64)
    max_exact = REL_BUCKETS // 2
    large = max_exact + (np.log(np.maximum(d, 1) / max_exact) / np.log(REL_MAX_DIST / max_exact)
                         * (REL_BUCKETS - max_exact)).astype(np.int32)
    large = np.minimum(large, REL_BUCKETS - 1)
    return np.where(d < max_exact, d, large).astype(np.int32)


def _toeplitz(tab, n_rows, n_cols):
    period = tab.shape[-1]
    assert period >= n_rows + n_cols - 1 and n_cols <= period - 1
    lead = tab.shape[:-1]
    flat = jnp.broadcast_to(tab[..., None, :], lead + (n_rows, period)).reshape(lead + (-1,))
    skew = flat[..., :n_rows * (period - 1)].reshape(lead + (n_rows, period - 1))
    return skew[..., :n_cols]


def _band_bias(rel_bias, group):
    win, dil = DILATION_GROUPS[group]
    assert win // dil == Q_BLOCK
    period = 3 * Q_BLOCK - 1
    idx = np.arange(period)
    m = Q_BLOCK - np.where(idx < 2 * Q_BLOCK, idx, idx - period)
    valid = (m >= 0) & (m <= Q_BLOCK)
    buckets = _t5_buckets(dil * np.clip(m, 0, Q_BLOCK))
    tbl = jnp.take(rel_bias[:, group * N_HEADS:(group + 1) * N_HEADS].astype(F32), buckets, axis=0).T
    general = _toeplitz(jnp.where(valid[None], tbl, NEG_INF), Q_BLOCK, 2 * Q_BLOCK)
    first = jnp.where((np.arange(2 * Q_BLOCK) >= Q_BLOCK)[None, None, :], general, NEG_INF)
    return jnp.stack([first, general])


_HEAD_OF_SEG = np.array([4 * (2 * (s // 8) + s % 2) + (s // 2) % 4 for s in range(N_HEADS)])


def _proj_rm_kernel(x_ref, g_ref, w_ref, *refs, natural, lane_ranges, tm):
    scr = refs[-1]
    outs = refs[:-1]
    y = _bdot(_rms(x_ref[...], g_ref[...]), w_ref[...])
    if natural:
        outs[0][...] = y
        outs = outs[1:]
    for c in range(scr.shape[0]):
        scr[c] = y[:, c * LANES:(c + 1) * LANES]
    for gi, (_, dil) in enumerate(DILATION_GROUPS):
        lo, hi = lane_ranges[gi]
        if dil == 1:
            outs[gi][0] = y[:, lo:hi].astype(BF16)
            continue
        for rho in range(dil):
            rows = [scr[c, pl.ds(rho, tm // dil, stride=dil), :] for c in range(lo // LANES, hi // LANES)]
            outs[gi][rho] = jnp.concatenate(rows, axis=1).astype(BF16)


def _proj_rm(h, g, w, nb, seq_len, natural, lane_ranges, name):
    n = h.shape[0]
    tm = _tile(n, PROJ_TILE)
    nout = w.shape[1]
    assert seq_len % tm == 0 and all(tm % (16 * dil) == 0 for _, dil in DILATION_GROUPS)
    tps = seq_len // tm
    out_shape, out_specs = [], []
    if natural:
        out_shape.append(jax.ShapeDtypeStruct((n, nout), F32))
        out_specs.append(pl.BlockSpec((tm, nout), lambda i: (i, 0)))
    for (_, dil), (lo, hi) in zip(DILATION_GROUPS, lane_ranges):
        out_shape.append(jax.ShapeDtypeStruct((nb, dil, seq_len // dil, hi - lo), BF16))
        out_specs.append(pl.BlockSpec((None, dil, tm // dil, hi - lo), lambda i: (i // tps, 0, i % tps, 0)))
    return pl.pallas_call(
        functools.partial(_proj_rm_kernel, natural=natural, lane_ranges=lane_ranges, tm=tm),
        grid=(n // tm,),
        in_specs=[pl.BlockSpec((tm, D_MODEL), lambda i: (i, 0)), _const_spec((1, D_MODEL)), _const_spec(w.shape)],
        out_specs=out_specs,
        out_shape=out_shape,
        scratch_shapes=[pltpu.VMEM((nout // LANES, tm, LANES), F32)],
        compiler_params=_params("arbitrary"),
        name=name,
    )(h, g, w)


def _attn_kernel(q_ref, kv_ref, halo_ref, bias_ref, o_ref, lse_ref, kvbuf, *, nq):
    i = pl.program_id(0)
    dil = q_ref.shape[0]
    kvd = KV_HEADS * HEAD_DIM
    nt = (((1,), (1,)), ((), ()))
    kvbuf[:, :Q_BLOCK, :] = halo_ref[...]
    kvbuf[:, Q_BLOCK:, :] = kv_ref[...]
    lane = lax.broadcasted_iota(jnp.int32, (Q_BLOCK, LANES), 1)
    low_half = lane < HEAD_DIM
    lane_row = lax.broadcasted_iota(jnp.int32, (1, LANES), 1)
    keep_lo = (lane_row < HEAD_DIM).astype(BF16)
    keep_hi = (lane_row >= HEAD_DIM).astype(BF16)

    def block(u, carry):
        rho = lax.div(u, nq)
        j = u - rho * nq
        r0 = pl.multiple_of(j * Q_BLOCK, Q_BLOCK)
        qb = q_ref[rho, pl.ds(r0, Q_BLOCK), :]
        kvb = kvbuf[rho, pl.ds(r0, 2 * Q_BLOCK), :]
        bsel = jnp.where(jnp.logical_and(i == 0, j == 0), 0, 1)
        lse_tile = jnp.zeros((Q_BLOCK, LANES), F32)
        for G in range(KV_HEADS // 2):
            kg = kvb[:, G * LANES:(G + 1) * LANES]
            vg = kvb[:, kvd + G * LANES:kvd + (G + 1) * LANES]
            vcat = jnp.concatenate([vg * keep_lo, vg * keep_hi], axis=0)
            pieces = []
            for r in range(Q_PER_KV):
                qg = qb[:, (G * Q_PER_KV + r) * LANES:(G * Q_PER_KV + r + 1) * LANES]
                pieces += [qg * keep_lo, qg * keep_hi]
            s_all = lax.dot_general(jnp.concatenate(pieces, axis=0), kg, nt, preferred_element_type=F32)
            for r in range(Q_PER_KV):
                grp = G * Q_PER_KV + r
                parts = []
                for half in range(2):
                    s = s_all[(2 * r + half) * Q_BLOCK:(2 * r + half + 1) * Q_BLOCK]
                    s = s + bias_ref[bsel, int(_HEAD_OF_SEG[2 * grp + half])]
                    m = jnp.max(s, axis=-1, keepdims=True)
                    p = jnp.exp(s - m)
                    l = jnp.sum(p, axis=-1, keepdims=True)
                    parts.append((p.astype(BF16), l, m + jnp.log(l)))
                (p0, l0, e0), (p1, l1, e1) = parts
                o = jnp.dot(jnp.concatenate([p0, p1], axis=1), vcat, preferred_element_type=F32)
                o = (o / jnp.where(low_half, l0, l1)).astype(o_ref.dtype)
                o_ref[rho, pl.ds(r0, Q_BLOCK), grp * LANES:(grp + 1) * LANES] = o
                lse_tile = jnp.where(lane == 2 * grp, e0, lse_tile)
                lse_tile = jnp.where(lane == 2 * grp + 1, e1, lse_tile)
        lse_ref[rho, pl.ds(r0, Q_BLOCK), :] = lse_tile
        return carry

    lax.fori_loop(0, dil * nq, block, 0)


def _attn_group(q_rm, kv_rm, bias):
    nb, dil, tsub, _ = q_rm.shape
    kvw = kv_rm.shape[-1]
    rows = ATTN_SPAN // dil
    nq = rows // Q_BLOCK
    assert tsub % rows == 0 and nq >= 1
    span_spec = lambda width: pl.BlockSpec((None, dil, rows, width), lambda i, b: (b, 0, i, 0))
    return pl.pallas_call(
        functools.partial(_attn_kernel, nq=nq),
        grid=(tsub // rows, nb),
        in_specs=[span_spec(D_MODEL), span_spec(kvw),
                  pl.BlockSpec((None, dil, Q_BLOCK, kvw), lambda i, b: (b, 0, jnp.maximum(i * nq - 1, 0), 0)),
                  _const_spec(bias.shape)],
        out_specs=[span_spec(D_MODEL), span_spec(LANES)],
        out_shape=[jax.ShapeDtypeStruct((nb, dil, tsub, D_MODEL), BF16),
                   jax.ShapeDtypeStruct((nb, dil, tsub, LANES), F32)],
        scratch_shapes=[pltpu.VMEM((dil, Q_BLOCK + rows, kvw), BF16)],
        compiler_params=_params("arbitrary", "arbitrary"),
        name=f"attn_d{dil}",
    )(q_rm, kv_rm, kv_rm, bias)


def _attn_out_kernel(o0_ref, o1_ref, o2_ref, l0_ref, l1_ref, l2_ref, h_ref, wo_ref, et_ref, out_ref, *scr, tm):
    outs, lses = [], []
    for gi, (o_ref, l_ref) in enumerate(((o0_ref, l0_ref), (o1_ref, l1_ref), (o2_ref, l2_ref))):
        dil = DILATION_GROUPS[gi][1]
        if dil == 1:
            outs.append(o_ref[0].astype(F32))
            lses.append(l_ref[0])
            continue
        so, sl = scr[2 * gi], scr[2 * gi + 1]
        n_tiles = so.shape[0]
        for rho in range(dil):
            rows = pl.ds(rho, tm // dil, stride=dil)
            for c in range(n_tiles):
                so[c, rows, :] = o_ref[rho, :, c * LANES:(c + 1) * LANES].astype(F32)
            sl[rows, :] = l_ref[rho]
        outs.append(jnp.concatenate([so[c] for c in range(n_tiles)], axis=1))
        lses.append(sl[...])
    l0, l1, l2 = lses
    m = jnp.maximum(jnp.maximum(l0, l1), l2)
    w0, w1, w2 = jnp.exp(l0 - m), jnp.exp(l1 - m), jnp.exp(l2 - m)
    inv = 1.0 / (w0 + w1 + w2)
    et = et_ref[...]
    att = _bdot(w0 * inv, et) * outs[0] + _bdot(w1 * inv, et) * outs[1] + _bdot(w2 * inv, et) * outs[2]
    out_ref[...] = h_ref[...] + _bdot(att, wo_ref[...])


def _attn_out(outs, lses, h, wo, et, seq_len):
    n = h.shape[0]
    tm = _tile(n, PROJ_TILE)
    tps = seq_len // tm
    row_spec = pl.BlockSpec((tm, D_MODEL), lambda i: (i, 0))

    def rm_spec(dil, width):
        return pl.BlockSpec((None, dil, tm // dil, width), lambda i: (i // tps, 0, i % tps, 0))

    dils = [dil for _, dil in DILATION_GROUPS]
    scratch = []
    for _ in dils:
        scratch += [pltpu.VMEM((D_MODEL // LANES, tm, LANES), F32), pltpu.VMEM((tm, LANES), F32)]
    return pl.pallas_call(
        functools.partial(_attn_out_kernel, tm=tm),
        grid=(n // tm,),
        in_specs=[rm_spec(d, D_MODEL) for d in dils] + [rm_spec(d, LANES) for d in dils]
                 + [row_spec, _const_spec(wo.shape), _const_spec(et.shape)],
        out_specs=row_spec,
        out_shape=jax.ShapeDtypeStruct((n, D_MODEL), F32),
        scratch_shapes=scratch,
        compiler_params=_params("arbitrary"),
        name="attn_out",
    )(*outs, *lses, h, wo, et)


def _decode_bias(rel_bias, seq_len, cache_len):
    ncol = cache_len + LANES
    period = seq_len + ncol - 1
    idx = np.arange(period)
    dist = cache_len - np.where(idx < ncol, idx, idx - period)
    buckets = _t5_buckets(np.clip(dist, 0, MAX_WINDOW))
    tabs = []
    for g, (win, dil) in enumerate(DILATION_GROUPS):
        valid = (dist >= 0) & (dist % dil == 0) & (dist <= win)
        tbl = jnp.take(rel_bias[:, g * N_HEADS:(g + 1) * N_HEADS].astype(F32), buckets, axis=0).T
        tabs.append(jnp.where(valid[None], tbl, NEG_INF))
    rows = _toeplitz(jnp.stack(tabs), seq_len, ncol)
    bias = rows.reshape(N_GROUPS * N_HEADS * seq_len, ncol)
    return bias[:, :cache_len], bias[:, cache_len:]


def _attn_decode_kernel(q_ref, cache_ref, kvn_ref, h_ref, bc_ref, bn_ref, wo_ref, out_ref, *, seq_len):
    kvd = KV_HEADS * HEAD_DIM
    nslot = N_GROUPS * N_HEADS
    rows_g = N_HEADS * seq_len
    nt = (((1,), (1,)), ((), ()))
    cache = cache_ref[...]
    kc = cache[:, :kvd].astype(BF16)
    vc = cache[:, kvd:].astype(BF16)
    kvn = kvn_ref[...]
    pad = jnp.zeros((LANES - seq_len, kvd), F32)
    kn = jnp.concatenate([kvn[:, :kvd], pad], axis=0).astype(BF16)
    vn = jnp.concatenate([kvn[:, kvd:], pad], axis=0).astype(BF16)
    lhs = jnp.concatenate([q_ref[:, s * kvd:(s + 1) * kvd] for s in range(nslot)], axis=0).astype(BF16)
    sc = lax.dot_general(lhs, kc, nt, preferred_element_type=F32) + bc_ref[...]
    sn = lax.dot_general(lhs, kn, nt, preferred_element_type=F32) + bn_ref[...]
    m_g = jnp.maximum(jnp.max(sc, axis=-1, keepdims=True), jnp.max(sn, axis=-1, keepdims=True))
    m = jnp.maximum(jnp.maximum(m_g[:rows_g], m_g[rows_g:2 * rows_g]), m_g[2 * rows_g:])
    m3 = jnp.concatenate([m, m, m], axis=0)
    pc = jnp.exp(sc - m3)
    pn = jnp.exp(sn - m3)
    l_g = jnp.sum(pc, axis=-1, keepdims=True) + jnp.sum(pn, axis=-1, keepdims=True)
    num_g = (jnp.dot(pc.astype(BF16), vc, preferred_element_type=F32)
             + jnp.dot(pn.astype(BF16), vn, preferred_element_type=F32))
    l = l_g[:rows_g] + l_g[rows_g:2 * rows_g] + l_g[2 * rows_g:]
    num = num_g[:rows_g] + num_g[rows_g:2 * rows_g] + num_g[2 * rows_g:]
    row = lax.broadcasted_iota(jnp.int32, (rows_g, kvd), 0)
    lane = lax.broadcasted_iota(jnp.int32, (rows_g, kvd), 1)
    own = (row // (Q_PER_KV * seq_len)) == (lane // HEAD_DIM)
    att = jnp.where(own, num / l, 0.0)
    out = h_ref[...]
    for r in range(Q_PER_KV):
        a_r = att[r * seq_len:(r + 1) * seq_len]
        for c in range(1, KV_HEADS):
            a_r = a_r + att[(c * Q_PER_KV + r) * seq_len:(c * Q_PER_KV + r + 1) * seq_len]
        out = out + _bdot(a_r, wo_ref[r])
    out_ref[...] = out


def _attn_decode(q, cache, kv_new, h, bias_c, bias_n, wo_r, nb, seq_len):
    cache_len = cache.shape[1]
    qw = q.shape[1]
    kvw = 2 * KV_HEADS * HEAD_DIM
    return pl.pallas_call(
        functools.partial(_attn_decode_kernel, seq_len=seq_len),
        grid=(nb,),
        in_specs=[pl.BlockSpec((seq_len, qw), lambda b: (b, 0)),
                  pl.BlockSpec((None, cache_len, kvw), lambda b: (b, 0, 0)),
                  pl.BlockSpec((seq_len, kvw), lambda b: (b, 0)),
                  pl.BlockSpec((seq_len, D_MODEL), lambda b: (b, 0)),
                  _const_spec(bias_c.shape), _const_spec(bias_n.shape), _const_spec(wo_r.shape)],
        out_specs=pl.BlockSpec((seq_len, D_MODEL), lambda b: (b, 0)),
        out_shape=jax.ShapeDtypeStruct((nb * seq_len, D_MODEL), F32),
        compiler_params=_params("arbitrary"),
        name="attn_decode",
    )(q, cache, kv_new, h, bias_c, bias_n, wo_r)


def _prep_weights(norm_w, ffn1_wi, ffn1_wo, ffn2_wi, ffn2_wo, pe_proj, pe_gate,
                  rwkv_mix, rwkv_wrkv, rwkv_wo, rwkv_w0, rwkv_w1, rwkv_w2, rwkv_a0, rwkv_a1, rwkv_a2,
                  rwkv_g1, rwkv_g2, rwkv_kk, rwkv_ka, rwkv_rk, rwkv_lnx_w, rwkv_lnx_b,
                  attn_wq, attn_wo, kv_norm, w_kv, rel_bias, final_norm):
    def row(v):
        return v.reshape(1, -1).astype(F32)

    def pad_cols(w, n):
        return jnp.pad(w, ((0, 0), (0, n - w.shape[1]))).astype(BF16)

    def pad_rows(w, n):
        return jnp.pad(w, ((0, n - w.shape[0]), (0, 0))).astype(BF16)

    def ffn_w(wi, wo):
        return wi.astype(BF16), wo.astype(BF16)

    head_of_lane = np.arange(D_MODEL) // HEAD_DIM
    e = jnp.asarray(head_of_lane[:, None] == np.arange(LANES)[None, :], BF16)
    et = jnp.asarray(np.arange(LANES)[:, None] == head_of_lane[None, :], BF16)

    depth = norm_w.shape[0]
    layers = []
    for i in range(depth):
        layers.append(dict(
            nw=[row(norm_w[i, j]) for j in range(4)],
            ffn1=ffn_w(ffn1_wi[i], ffn1_wo[i]), ffn2=ffn_w(ffn2_wi[i], ffn2_wo[i]),
            pe_gate=pe_gate[i].astype(BF16), pe_proj=pe_proj[i].astype(BF16)))
    n_a = depth // 2
    rw = []
    for i in range(n_a):
        rw.append(dict(
            nw=row(norm_w[i, 1]), mix=rwkv_mix[i].astype(F32), wrkv=rwkv_wrkv[i].astype(BF16),
            w0=row(rwkv_w0[i]), w1=pad_cols(rwkv_w1[i], LANES), w2=pad_rows(rwkv_w2[i], LANES),
            a0=row(rwkv_a0[i]), a1=pad_cols(rwkv_a1[i], LANES), a2=pad_rows(rwkv_a2[i], LANES),
            g1=pad_cols(rwkv_g1[i], 2 * LANES), g2=pad_rows(rwkv_g2[i], 2 * LANES),
            kk=row(rwkv_kk[i]), ka=row(rwkv_ka[i]), rk=row(rwkv_rk[i]),
            lnw=row(rwkv_lnx_w[i]), lnb=row(rwkv_lnx_b[i]), wo=rwkv_wo[i].astype(BF16), e=e, et=et))
    scale = HEAD_DIM ** -0.5
    at = []
    for j in range(depth - n_a):
        wq = attn_wq[j] * scale
        wq6 = wq.reshape(D_MODEL, N_GROUPS, KV_HEADS, Q_PER_KV, HEAD_DIM)
        wq_slots = jnp.einsum("dgcre,cx->dgcrxe", wq6, jnp.eye(KV_HEADS, dtype=wq.dtype))
        wq_slots = wq_slots.reshape(D_MODEL, N_GROUPS * N_HEADS * KV_HEADS * HEAD_DIM).astype(BF16)
        wo_r = attn_wo[j].reshape(KV_HEADS, Q_PER_KV, HEAD_DIM, D_MODEL).transpose(1, 0, 2, 3)
        wo_r = wo_r.reshape(Q_PER_KV, KV_HEADS * HEAD_DIM, D_MODEL).astype(BF16)
        wq_seg = wq.reshape(D_MODEL, N_GROUPS, N_HEADS, HEAD_DIM)[:, :, _HEAD_OF_SEG].reshape(D_MODEL, -1)
        wo_seg = attn_wo[j].reshape(N_HEADS, HEAD_DIM, D_MODEL)[_HEAD_OF_SEG].reshape(D_MODEL, D_MODEL)
        at.append(dict(wq_seg=wq_seg.astype(BF16), wq_slots=wq_slots, wo_seg=wo_seg.astype(BF16), wo_r=wo_r))
    return dict(layers=layers, rwkv=rw, attn=at, kv_norm=row(kv_norm), w_kv=w_kv.astype(BF16),
                final_norm=row(final_norm), rel_bias=rel_bias, et=et)


def _trunk(x, p, wkv0, shift0, cache, w):
    nb, seq_len, _ = x.shape
    n = nb * seq_len
    depth = len(w["layers"])
    n_a = depth // 2
    h = x.reshape(n, D_MODEL).astype(F32)
    wkv_out, shift_out = [], []
    kv_rows = kv_rm = None
    kvw = 2 * KV_HEADS * HEAD_DIM
    for i in range(depth):
        lw = w["layers"][i]
        if i == n_a:
            if cache is None:
                kv_rows, *kv_rm = _proj_rm(h, w["kv_norm"], w["w_kv"], nb, seq_len, True,
                                           [(0, kvw)] * N_GROUPS, "kv_proj")
            else:
                kv_rows = _norm_mm(h, w["kv_norm"], w["w_kv"], F32, name="kv_proj")
        h = _ffn(h, lw["nw"][0], *lw["ffn1"])
        if i < n_a:
            rwl = w["rwkv"][i]
            h, sh, st = _rwkv_layer(h, shift0[i].astype(F32), wkv0[i].astype(F32), seq_len, rwl)
            wkv_out.append(st.astype(x.dtype))
            shift_out.append(sh.astype(x.dtype))
        else:
            al = w["attn"][i - n_a]
            if cache is None:
                q_rm = _proj_rm(h, lw["nw"][1], al["wq_seg"], nb, seq_len, False,
                                [(gi * D_MODEL, (gi + 1) * D_MODEL) for gi in range(N_GROUPS)], "q_proj")
                outs, lses = [], []
                for gi in range(N_GROUPS):
                    o, lse = _attn_group(q_rm[gi], kv_rm[gi], _band_bias(w["rel_bias"], gi))
                    outs.append(o)
                    lses.append(lse)
                h = _attn_out(outs, lses, h, al["wo_seg"], w["et"], seq_len)
            else:
                q = _norm_mm(h, lw["nw"][1], al["wq_slots"], F32, col_tile=2048, name="q_proj_slots")
                bias_c, bias_n = _decode_bias(w["rel_bias"], seq_len, cache.shape[1])
                h = _attn_decode(q, cache, kv_rows, h, bias_c, bias_n, al["wo_r"], nb, seq_len)
        pe = (p[i].reshape(n, PLE_DIM).astype(F32), lw["nw"][3], lw["pe_gate"], lw["pe_proj"], w["final_norm"])
        h = _ffn(h, lw["nw"][2], *lw["ffn2"], pe=pe, final=(i == depth - 1))
    y = h.reshape(nb, seq_len, D_MODEL).astype(x.dtype)
    kv_rows = kv_rows.reshape(nb, seq_len, 2, KV_HEADS, HEAD_DIM).astype(x.dtype)
    return y, jnp.stack(wkv_out), jnp.stack(shift_out), kv_rows


def kernel(x_prompt, x_sample, state_wkv, state_shift, cache_kv, p_prompt, p_sample, norm_w, ffn1_wi, ffn1_wo, ffn2_wi, ffn2_wo, pe_proj, pe_gate, rwkv_mix, rwkv_wrkv, rwkv_wo, rwkv_w0, rwkv_w1, rwkv_w2, rwkv_a0, rwkv_a1, rwkv_a2, rwkv_g1, rwkv_g2, rwkv_kk, rwkv_ka, rwkv_rk, rwkv_lnx_w, rwkv_lnx_b, attn_wq, attn_wo, kv_norm, w_kv, rel_bias, final_norm):
    w = _prep_weights(norm_w, ffn1_wi, ffn1_wo, ffn2_wi, ffn2_wo, pe_proj, pe_gate,
                      rwkv_mix, rwkv_wrkv, rwkv_wo, rwkv_w0, rwkv_w1, rwkv_w2, rwkv_a0, rwkv_a1, rwkv_a2,
                      rwkv_g1, rwkv_g2, rwkv_kk, rwkv_ka, rwkv_rk, rwkv_lnx_w, rwkv_lnx_b,
                      attn_wq, attn_wo, kv_norm, w_kv, rel_bias, final_norm)
    n_a = norm_w.shape[0] // 2
    nb, seq_len, _ = x_prompt.shape
    wkv0 = jnp.zeros((n_a, nb, N_HEADS, HEAD_DIM, HEAD_DIM), F32)
    shift0 = jnp.zeros((n_a, nb, D_MODEL), x_prompt.dtype)
    y_p, wkv_p, shift_p, kv_p = _trunk(x_prompt, p_prompt, wkv0, shift0, None, w)
    kv_prompt = kv_p[:, seq_len - min(MAX_WINDOW, seq_len):]
    cache = cache_kv.reshape(cache_kv.shape[0], cache_kv.shape[1], 2 * KV_HEADS * HEAD_DIM).astype(F32)
    y_s, wkv_s, shift_s, kv_s = _trunk(x_sample, p_sample, state_wkv, state_shift, cache, w)
    return (y_p, y_s, wkv_p, shift_p, kv_prompt, wkv_s, shift_s, kv_s)
```

```python
import functools

import numpy as np
import jax
import jax.numpy as jnp
from jax import lax
from jax.experimental import pallas as pl
from jax.experimental.pallas import tpu as pltpu

F32 = jnp.float32
BF16 = jnp.bfloat16

D_MODEL = 1024
D_FF = 2816
PLE_DIM = 256
RMS_EPS = 1e-6
HEAD_DIM = 64
N_HEADS = D_MODEL // HEAD_DIM
LNX_EPS = 64e-5
KV_HEADS = 4
Q_PER_KV = N_HEADS // KV_HEADS
DILATION_GROUPS = ((128, 1), (512, 4), (2048, 16))
N_GROUPS = len(DILATION_GROUPS)
MAX_WINDOW = 2048
REL_BUCKETS = 32
REL_MAX_DIST = 2048
NEG_INF = -1e30

LANES = 128
FFN_CHUNK = 256
N_FFN_CHUNKS = D_FF // FFN_CHUNK
TOKEN_TILE = 256
FFN_TILE = 512
PROJ_TILE = 512
WKV_CHUNK = 64
RWKV_SHORT_SEQS = 8
Q_BLOCK = 128
ATTN_SPAN = 2048
VMEM_LIMIT = 56 * 1024 * 1024
EXP_MINUS_HALF = 0.6065306597126334


def _params(*sem):
    return pltpu.CompilerParams(dimension_semantics=sem, vmem_limit_bytes=VMEM_LIMIT)


def _const_spec(shape):
    return pl.BlockSpec(shape, lambda *_: (0,) * len(shape))


def _tile(n, pref=TOKEN_TILE):
    t = min(n, pref)
    while n % t:
        t -= 8
    return t


def _rms(x, g):
    return x * lax.rsqrt(jnp.mean(x * x, axis=-1, keepdims=True) + RMS_EPS) * g


def _bdot(a, b):
    return jnp.dot(a.astype(BF16), b, preferred_element_type=F32)


def _head_sum(x, e_ref, et_ref):
    return _bdot(_bdot(x, e_ref[...]), et_ref[...])


def _ffn_kernel(x_ref, g_ref, wi_ref, wo_ref, *rest, with_pe, final):
    o_ref = rest[-1]
    x = x_ref[...]
    xn = _rms(x, g_ref[...]).astype(BF16)
    acc = jnp.zeros_like(x)
    for j in range(N_FFN_CHUNKS):
        lo, hi = j * FFN_CHUNK, (j + 1) * FFN_CHUNK
        gate = jnp.dot(xn, wi_ref[:, lo:hi].astype(BF16), preferred_element_type=F32)
        up = jnp.dot(xn, wi_ref[:, D_FF + lo:D_FF + hi].astype(BF16), preferred_element_type=F32)
        act = (gate * jax.nn.sigmoid(gate) * up).astype(BF16)
        acc = acc + jnp.dot(act, wo_ref[lo:hi, :].astype(BF16), preferred_element_type=F32)
    y = x + 0.5 * acc
    if with_pe:
        p_ref, gp_ref, wgate_ref, wproj_ref, gf_ref = rest[:-1]
        gate = jax.nn.sigmoid(_bdot(_rms(y, gp_ref[...]), wgate_ref[...]))
        y = y + gate * _bdot(p_ref[...], wproj_ref[...])
        if final:
            y = _rms(y, gf_ref[...])
    o_ref[...] = y


def _ffn(h, g, wi, wo, pe=None, layer=0, final=False):
    n = h.shape[0]
    tm = _tile(n, FFN_TILE)
    row_spec = pl.BlockSpec((tm, D_MODEL), lambda i: (i, 0))
    args = [h, g, wi, wo]
    in_specs = [row_spec, _const_spec((1, D_MODEL)),
                pl.BlockSpec((None,) + wi.shape[1:], lambda i: (layer, 0, 0)),
                pl.BlockSpec((None,) + wo.shape[1:], lambda i: (layer, 0, 0))]
    if pe is not None:
        args += list(pe)
        in_specs += ([pl.BlockSpec((None, tm, PLE_DIM), lambda i: (layer, i, 0))]
                     + [_const_spec(a.shape) for a in pe[1:]])
    return pl.pallas_call(
        functools.partial(_ffn_kernel, with_pe=pe is not None, final=final),
        grid=(n // tm,),
        in_specs=in_specs,
        out_specs=row_spec,
        out_shape=jax.ShapeDtypeStruct((n, D_MODEL), F32),
        compiler_params=_params("arbitrary"),
        name="ffn_pe" if pe is not None else "ffn",
    )(*args)


def _norm_mm_kernel(x_ref, g_ref, w_ref, o_ref):
    o_ref[...] = _bdot(_rms(x_ref[...], g_ref[...]), w_ref[...]).astype(o_ref.dtype)


def _norm_mm(h, g, w, out_dtype, col_tile=None, name="norm_mm"):
    n = h.shape[0]
    tm = _tile(n)
    nout = w.shape[1]
    tn = nout if col_tile is None else col_tile
    return pl.pallas_call(
        _norm_mm_kernel,
        grid=(n // tm, nout // tn),
        in_specs=[pl.BlockSpec((tm, D_MODEL), lambda i, j: (i, 0)), _const_spec((1, D_MODEL)),
                  pl.BlockSpec((D_MODEL, tn), lambda i, j: (0, j))],
        out_specs=pl.BlockSpec((tm, tn), lambda i, j: (i, j)),
        out_shape=jax.ShapeDtypeStruct((n, nout), out_dtype),
        compiler_params=_params("arbitrary", "arbitrary"),
        name=name,
    )(h, g, w)


def _rwkv_layer_kernel(h_ref, sh_ref, s0_ref, nw_ref, mix_ref, wrkv_ref, w0_ref, w1_ref, w2_ref, a0_ref, a1_ref,
                       a2_ref, g1_ref, g2_ref, kk_ref, ka_ref, lnw_ref, lnb_ref, rk_ref, wo_ref, e_ref, et_ref,
                       o_ref, hn_o, st_ref,
                       state, prev, r_s, lw_s, k_s, v_s, kk_s, b_s, g_s, y_s, *, rows, n_sub, nseq):
    c = pl.program_id(1)
    C = WKV_CHUNK
    C2 = 2 * C
    n_pairs = N_HEADS // 2
    seq_rows = rows // nseq
    short = seq_rows % C != 0

    def load_state(s0):
        state[...] = jnp.zeros_like(state)
        for h in range(N_HEADS):
            lo = (h % 2) * HEAD_DIM
            state[h // 2, lo:lo + HEAD_DIM, lo:lo + HEAD_DIM] = s0[h]

    def store_state(st):
        for h in range(N_HEADS):
            lo = (h % 2) * HEAD_DIM
            st[h] = state[h // 2, lo:lo + HEAD_DIM, lo:lo + HEAD_DIM]

    if nseq == 1:
        @pl.when(c == 0)
        def _():
            load_state(s0_ref.at[0])
            prev[...] = sh_ref[...]
    if short:
        for ref in (r_s, lw_s, k_s, v_s, kk_s, b_s):
            ref[rows:, :] = jnp.zeros((ref.shape[0] - rows, D_MODEL), F32)

    nw = nw_ref[...]
    hn = _rms(h_ref[...], nw)
    row = lax.broadcasted_iota(jnp.int32, hn.shape, 0)
    rolled = pltpu.roll(hn, 1, 0)
    if nseq == 1:
        x_prev = jnp.where(row == 0, prev[...], rolled)
        prev[...] = hn[rows - 1:rows]
        hn_o[...] = hn[rows - 1:rows]
    else:
        x_prev = jnp.where(lax.rem(row, seq_rows) == 0, sh_ref[...], rolled)
        hn_o[...] = hn
    xx = x_prev - hn
    mix = mix_ref[...]
    xr, xw, xk, xv, xa, xg = (hn + xx * mix[j:j + 1] for j in range(6))
    r = _bdot(xr, wrkv_ref[0])
    k = _bdot(xk, wrkv_ref[1])
    v = _bdot(xv, wrkv_ref[2])
    wl = w0_ref[...] + _bdot(jnp.tanh(_bdot(xw, w1_ref[...])), w2_ref[...])
    lw_s[0:rows] = -EXP_MINUS_HALF * jax.nn.sigmoid(wl)
    a = jax.nn.sigmoid(a0_ref[...] + _bdot(_bdot(xa, a1_ref[...]), a2_ref[...]))
    g_s[...] = _bdot(jax.nn.sigmoid(_bdot(xg, g1_ref[...])), g2_ref[...])
    kkv = k * kk_ref[...]
    kk = kkv / jnp.maximum(jnp.sqrt(_head_sum(kkv * kkv, e_ref, et_ref)), 1e-12)
    r_s[0:rows] = r
    k_s[0:rows] = k * (1.0 + (a - 1.0) * ka_ref[...])
    v_s[0:rows] = v
    kk_s[0:rows] = kk
    b_s[0:rows] = kk * a

    ri = lax.broadcasted_iota(jnp.int32, (C, C), 0)
    ci = lax.broadcasted_iota(jnp.int32, (C, C), 1)
    tri = (ri >= ci).astype(BF16)
    r2 = lax.broadcasted_iota(jnp.int32, (C2, C2), 0)
    c2 = lax.broadcasted_iota(jnp.int32, (C2, C2), 1)
    strict = r2 > c2
    incl = r2 >= c2
    lane = lax.broadcasted_iota(jnp.int32, (C, LANES), 1)
    head0 = lane < HEAD_DIM

    def hat(x):
        x3 = jnp.stack([x[:, p * LANES:(p + 1) * LANES] for p in range(n_pairs)])
        return jnp.concatenate([jnp.where(head0, x3, 0.0), jnp.where(head0, 0.0, x3)], axis=1)

    def mm(a, b):
        return jnp.einsum("pmk,pkn->pmn", a.astype(BF16), b.astype(BF16), preferred_element_type=F32)

    def mm_nt(a, b):
        return jnp.einsum("pmk,pnk->pmn", a.astype(BF16), b.astype(BF16), preferred_element_type=F32)

    chunk_row = lax.broadcasted_iota(jnp.int32, (C, D_MODEL), 0)

    def chunk(ref, rws):
        x = ref[rws, :]
        return jnp.where(chunk_row < seq_rows, x, 0.0) if short else x

    def sub_chunk(s, carry):
        start = s * seq_rows if short else s * C
        rws = pl.ds(pl.multiple_of(start, 8), C)
        lw = chunk(lw_s, rws)
        p1 = lw.astype(BF16)
        rem = lw - p1.astype(F32)
        p2 = rem.astype(BF16)
        p3 = (rem - p2.astype(F32)).astype(BF16)
        cw = (jnp.dot(tri, p1, preferred_element_type=F32) + jnp.dot(tri, p2, preferred_element_type=F32)
              + jnp.dot(tri, p3, preferred_element_type=F32))
        cw_end = cw[C - 1:C, :]
        e_neg = jnp.exp(-cw)
        e_end = jnp.exp(cw_end - cw)
        kk_c = chunk(kk_s, rws)
        bb = chunk(b_s, rws)
        k_c = chunk(k_s, rws)
        a_all = kk_c * jnp.exp(cw - lw)
        r_all = chunk(r_s, rws) * jnp.exp(cw)
        b_all = bb * e_neg
        k_all = k_c * e_neg
        bd_all = bb * e_end
        kd_all = k_c * e_end
        v_all = chunk(v_s, rws)
        decay = jnp.exp(cw_end)
        ar_h = jnp.concatenate([hat(a_all), hat(r_all)], axis=1)
        bk_h = jnp.concatenate([hat(b_all), hat(k_all)], axis=1)
        v_h = hat(v_all)
        st = state[...]
        g = mm_nt(ar_h, bk_h)
        low = jnp.where(strict, g[:, :C2, :C2], 0.0)
        ak = jnp.where(strict, g[:, :C2, C2:], 0.0)
        rbk = jnp.concatenate([jnp.where(incl, g[:, C2:, :C2], 0.0), jnp.where(incl, g[:, C2:, C2:], 0.0)], axis=2)
        ss = mm_nt(ar_h, st)
        x = -(ss[:, :C2] + mm(ak, v_h))
        t = mm(low, jnp.concatenate([low, x], axis=2))
        lp = t[:, :, :C2]
        x = x - t[:, :, C2:]
        for _ in range(4):
            t = mm(lp, jnp.concatenate([lp, x], axis=2))
            lp = t[:, :, :C2]
            x = x + t[:, :, C2:]
        x = x + mm(lp, x)
        xv = jnp.concatenate([x, v_h], axis=1)
        y_h = ss[:, C2:] + mm(rbk, xv)
        y = y_h[:, :C] + y_h[:, C:]
        out_rows = pl.ds(pl.multiple_of(start, 8), seq_rows) if short else rws
        for p in range(n_pairs):
            y_s[out_rows, p * LANES:(p + 1) * LANES] = y[p, :seq_rows] if short else y[p]
        bkd_h = jnp.concatenate([hat(bd_all), hat(kd_all)], axis=1)
        xv_t = jnp.stack([xv[p].T for p in range(n_pairs)])
        dec3 = jnp.stack([decay[:, p * LANES:(p + 1) * LANES] for p in range(n_pairs)])
        state[...] = st * dec3 + mm(xv_t, bkd_h)
        return carry

    if nseq == 1:
        lax.fori_loop(0, n_sub, sub_chunk, 0)
    else:
        def one_sequence(s, carry):
            load_state(s0_ref.at[s])
            sub_chunk(s, carry)
            store_state(st_ref.at[s])
            return carry

        lax.fori_loop(0, nseq, one_sequence, 0)

    y = y_s[0:rows]
    inv_n = 1.0 / HEAD_DIM
    mu = _head_sum(y, e_ref, et_ref) * inv_n
    yc = y - mu
    var = _head_sum(yc * yc, e_ref, et_ref) * inv_n
    yn = yc * lax.rsqrt(var + LNX_EPS) * lnw_ref[...] + lnb_ref[...]
    bonus = _head_sum(r_s[0:rows] * k_s[0:rows] * rk_ref[...], e_ref, et_ref) * v_s[0:rows]
    o_ref[...] = h_ref[...] + _bdot((yn + bonus) * g_s[...], wo_ref[...])

    if nseq == 1:
        @pl.when(c == pl.num_programs(1) - 1)
        def _():
            store_state(st_ref.at[0])


def _rwkv_layer(h, shift, s0, seq_len, lw):
    n = h.shape[0]
    nb = n // seq_len
    if seq_len % WKV_CHUNK == 0:
        nseq, rows = 1, _tile(seq_len)
        steps, n_sub, buf_rows = seq_len // rows, rows // WKV_CHUNK, rows
        assert rows % WKV_CHUNK == 0
        sh_in = shift.reshape(nb, 1, D_MODEL)
        sh_spec = pl.BlockSpec((None, 1, D_MODEL), lambda bi, ci: (bi, 0, 0))
        hn_shape, hn_spec = jax.ShapeDtypeStruct((nb, 1, D_MODEL), F32), sh_spec
    else:
        assert seq_len < WKV_CHUNK and seq_len % 8 == 0
        nseq = _tile(nb, RWKV_SHORT_SEQS)
        rows, steps, n_sub = nseq * seq_len, 1, 1
        buf_rows = rows + WKV_CHUNK - seq_len
        sh_in = jnp.repeat(shift, seq_len, axis=0)
        sh_spec = pl.BlockSpec((rows, D_MODEL), lambda bi, ci: (bi, 0))
        hn_shape, hn_spec = jax.ShapeDtypeStruct((n, D_MODEL), F32), sh_spec
    row_spec = pl.BlockSpec((rows, D_MODEL), lambda bi, ci: (bi * steps + ci, 0))
    st_spec = pl.BlockSpec((nseq, N_HEADS, HEAD_DIM, HEAD_DIM), lambda bi, ci: (bi, 0, 0, 0))
    consts = [lw["nw"], lw["mix"], lw["wrkv"], lw["w0"], lw["w1"], lw["w2"], lw["a0"], lw["a1"], lw["a2"],
              lw["g1"], lw["g2"], lw["kk"], lw["ka"], lw["lnw"], lw["lnb"], lw["rk"], lw["wo"], lw["e"], lw["et"]]
    seq_buf = pltpu.VMEM((buf_rows, D_MODEL), F32)
    out, hn, st = pl.pallas_call(
        functools.partial(_rwkv_layer_kernel, rows=rows, n_sub=n_sub, nseq=nseq),
        grid=(nb // nseq, steps),
        in_specs=[row_spec, sh_spec, st_spec] + [_const_spec(c.shape) for c in consts],
        out_specs=[row_spec, hn_spec, st_spec],
        out_shape=[jax.ShapeDtypeStruct((n, D_MODEL), F32), hn_shape, jax.ShapeDtypeStruct(s0.shape, F32)],
        scratch_shapes=[pltpu.VMEM((N_HEADS // 2, LANES, LANES), F32), pltpu.VMEM((1, D_MODEL), F32)]
                       + [seq_buf] * 6 + [pltpu.VMEM((rows, D_MODEL), F32), seq_buf],
        compiler_params=_params("arbitrary", "arbitrary"),
        name="rwkv_layer",
    )(h, sh_in, s0, *consts)
    shift_out = hn.reshape(nb, D_MODEL) if nseq == 1 else hn.reshape(nb, seq_len, D_MODEL)[:, -1]
    return out, shift_out, st


def _t5_buckets(dist):
    d = np.asarray(dist, dtype=np.int64)
    max_exact = REL_BUCKETS // 2
    large = max_exact + (np.log(np.maximum(d, 1) / max_exact) / np.log(REL_MAX_DIST / max_exact)
                         * (REL_BUCKETS - max_exact)).astype(np.int32)
    large = np.minimum(large, REL_BUCKETS - 1)
    return np.where(d < max_exact, d, large).astype(np.int32)


def _toeplitz(tab, n_rows, n_cols):
    period = tab.shape[-1]
    assert period >= n_rows + n_cols - 1 and n_cols <= period - 1
    lead = tab.shape[:-1]
    flat = jnp.broadcast_to(tab[..., None, :], lead + (n_rows, period)).reshape(lead + (-1,))
    skew = flat[..., :n_rows * (period - 1)].reshape(lead + (n_rows, period - 1))
    return skew[..., :n_cols]


def _band_bias(rel_bias, group):
    win, dil = DILATION_GROUPS[group]
    assert win // dil == Q_BLOCK
    period = 3 * Q_BLOCK - 1
    idx = np.arange(period)
    m = Q_BLOCK - np.where(idx < 2 * Q_BLOCK, idx, idx - period)
    valid = (m >= 0) & (m <= Q_BLOCK)
    buckets = _t5_buckets(dil * np.clip(m, 0, Q_BLOCK))
    tbl = jnp.take(rel_bias[:, group * N_HEADS:(group + 1) * N_HEADS].astype(F32), buckets, axis=0).T
    general = _toeplitz(jnp.where(valid[None], tbl, NEG_INF), Q_BLOCK, 2 * Q_BLOCK)
    first = jnp.where((np.arange(2 * Q_BLOCK) >= Q_BLOCK)[None, None, :], general, NEG_INF)
    return jnp.stack([first, general])


_HEAD_OF_SEG = np.array([4 * (2 * (s // 8) + s % 2) + (s // 2) % 4 for s in range(N_HEADS)])


def _proj_rm_kernel(x_ref, g_ref, w_ref, *refs, natural, lane_ranges, tm):
    scr = refs[-1]
    outs = refs[:-1]
    y = _bdot(_rms(x_ref[...], g_ref[...]), w_ref[...])
    if natural:
        outs[0][...] = y
        outs = outs[1:]
    for c in range(scr.shape[0]):
        scr[c] = y[:, c * LANES:(c + 1) * LANES]
    for gi, (_, dil) in enumerate(DILATION_GROUPS):
        lo, hi = lane_ranges[gi]
        if dil == 1:
            outs[gi][0] = y[:, lo:hi].astype(BF16)
            continue
        for rho in range(dil):
            rows = [scr[c, pl.ds(rho, tm // dil, stride=dil), :] for c in range(lo // LANES, hi // LANES)]
            outs[gi][rho] = jnp.concatenate(rows, axis=1).astype(BF16)


def _proj_rm(h, g, w, nb, seq_len, natural, lane_ranges, name):
    n = h.shape[0]
    tm = _tile(n, PROJ_TILE)
    nout = w.shape[1]
    assert seq_len % tm == 0 and all(tm % (16 * dil) == 0 for _, dil in DILATION_GROUPS)
    tps = seq_len // tm
    out_shape, out_specs = [], []
    if natural:
        out_shape.append(jax.ShapeDtypeStruct((n, nout), F32))
        out_specs.append(pl.BlockSpec((tm, nout), lambda i: (i, 0)))
    for (_, dil), (lo, hi) in zip(DILATION_GROUPS, lane_ranges):
        out_shape.append(jax.ShapeDtypeStruct((nb, dil, seq_len // dil, hi - lo), BF16))
        out_specs.append(pl.BlockSpec((None, dil, tm // dil, hi - lo), lambda i: (i // tps, 0, i % tps, 0)))
    return pl.pallas_call(
        functools.partial(_proj_rm_kernel, natural=natural, lane_ranges=lane_ranges, tm=tm),
        grid=(n // tm,),
        in_specs=[pl.BlockSpec((tm, D_MODEL), lambda i: (i, 0)), _const_spec((1, D_MODEL)), _const_spec(w.shape)],
        out_specs=out_specs,
        out_shape=out_shape,
        scratch_shapes=[pltpu.VMEM((nout // LANES, tm, LANES), F32)],
        compiler_params=_params("arbitrary"),
        name=name,
    )(h, g, w)


def _attn_kernel(q_ref, kv_ref, halo_ref, bias_ref, o_ref, lse_ref, kvbuf, *, nq):
    i = pl.program_id(0)
    dil = q_ref.shape[0]
    kvd = KV_HEADS * HEAD_DIM
    nt = (((1,), (1,)), ((), ()))
    kvbuf[:, :Q_BLOCK, :] = halo_ref[...]
    kvbuf[:, Q_BLOCK:, :] = kv_ref[...]
    lane = lax.broadcasted_iota(jnp.int32, (Q_BLOCK, LANES), 1)
    low_half = lane < HEAD_DIM
    lane_row = lax.broadcasted_iota(jnp.int32, (1, LANES), 1)
    keep_lo = (lane_row < HEAD_DIM).astype(BF16)
    keep_hi = (lane_row >= HEAD_DIM).astype(BF16)

    def block(u, carry):
        rho = lax.div(u, nq)
        j = u - rho * nq
        r0 = pl.multiple_of(j * Q_BLOCK, Q_BLOCK)
        qb = q_ref[rho, pl.ds(r0, Q_BLOCK), :]
        kvb = kvbuf[rho, pl.ds(r0, 2 * Q_BLOCK), :]
        bsel = jnp.where(jnp.logical_and(i == 0, j == 0), 0, 1)
        lse_tile = jnp.zeros((Q_BLOCK, LANES), F32)
        for G in range(KV_HEADS // 2):
            kg = kvb[:, G * LANES:(G + 1) * LANES]
            vg = kvb[:, kvd + G * LANES:kvd + (G + 1) * LANES]
            vcat = jnp.concatenate([vg * keep_lo, vg * keep_hi], axis=0)
            pieces = []
            for r in range(Q_PER_KV):
                qg = qb[:, (G * Q_PER_KV + r) * LANES:(G * Q_PER_KV + r + 1) * LANES]
                pieces += [qg * keep_lo, qg * keep_hi]
            s_all = lax.dot_general(jnp.concatenate(pieces, axis=0), kg, nt, preferred_element_type=F32)
            for r in range(Q_PER_KV):
                grp = G * Q_PER_KV + r
                parts = []
                for half in range(2):
                    s = s_all[(2 * r + half) * Q_BLOCK:(2 * r + half + 1) * Q_BLOCK]
                    s = s + bias_ref[bsel, int(_HEAD_OF_SEG[2 * grp + half])]
                    m = jnp.max(s, axis=-1, keepdims=True)
                    p = jnp.exp(s - m)
                    l = jnp.sum(p, axis=-1, keepdims=True)
                    parts.append((p.astype(BF16), l, m + jnp.log(l)))
                (p0, l0, e0), (p1, l1, e1) = parts
                o = jnp.dot(jnp.concatenate([p0, p1], axis=1), vcat, preferred_element_type=F32)
                o = (o / jnp.where(low_half, l0, l1)).astype(o_ref.dtype)
                o_ref[rho, pl.ds(r0, Q_BLOCK), grp * LANES:(grp + 1) * LANES] = o
                lse_tile = jnp.where(lane == 2 * grp, e0, lse_tile)
                lse_tile = jnp.where(lane == 2 * grp + 1, e1, lse_tile)
        lse_ref[rho, pl.ds(r0, Q_BLOCK), :] = lse_tile
        return carry

    lax.fori_loop(0, dil * nq, block, 0)


def _attn_group(q_rm, kv_rm, bias):
    nb, dil, tsub, _ = q_rm.shape
    kvw = kv_rm.shape[-1]
    rows = ATTN_SPAN // dil
    nq = rows // Q_BLOCK
    assert tsub % rows == 0 and nq >= 1
    span_spec = lambda width: pl.BlockSpec((None, dil, rows, width), lambda i, b: (b, 0, i, 0))
    return pl.pallas_call(
        functools.partial(_attn_kernel, nq=nq),
        grid=(tsub // rows, nb),
        in_specs=[span_spec(D_MODEL), span_spec(kvw),
                  pl.BlockSpec((None, dil, Q_BLOCK, kvw), lambda i, b: (b, 0, jnp.maximum(i * nq - 1, 0), 0)),
                  _const_spec(bias.shape)],
        out_specs=[span_spec(D_MODEL), span_spec(LANES)],
        out_shape=[jax.ShapeDtypeStruct((nb, dil, tsub, D_MODEL), BF16),
                   jax.ShapeDtypeStruct((nb, dil, tsub, LANES), F32)],
        scratch_shapes=[pltpu.VMEM((dil, Q_BLOCK + rows, kvw), BF16)],
        compiler_params=_params("arbitrary", "arbitrary"),
        name=f"attn_d{dil}",
    )(q_rm, kv_rm, kv_rm, bias)


def _attn_out_kernel(o0_ref, o1_ref, o2_ref, l0_ref, l1_ref, l2_ref, h_ref, wo_ref, et_ref, out_ref, *scr, tm):
    outs, lses = [], []
    for gi, (o_ref, l_ref) in enumerate(((o0_ref, l0_ref), (o1_ref, l1_ref), (o2_ref, l2_ref))):
        dil = DILATION_GROUPS[gi][1]
        if dil == 1:
            outs.append(o_ref[0].astype(F32))
            lses.append(l_ref[0])
            continue
        so, sl = scr[2 * gi], scr[2 * gi + 1]
        n_tiles = so.shape[0]
        for rho in range(dil):
            rows = pl.ds(rho, tm // dil, stride=dil)
            for c in range(n_tiles):
                so[c, rows, :] = o_ref[rho, :, c * LANES:(c + 1) * LANES].astype(F32)
            sl[rows, :] = l_ref[rho]
        outs.append(jnp.concatenate([so[c] for c in range(n_tiles)], axis=1))
        lses.append(sl[...])
    l0, l1, l2 = lses
    m = jnp.maximum(jnp.maximum(l0, l1), l2)
    w0, w1, w2 = jnp.exp(l0 - m), jnp.exp(l1 - m), jnp.exp(l2 - m)
    inv = 1.0 / (w0 + w1 + w2)
    et = et_ref[...]
    att = _bdot(w0 * inv, et) * outs[0] + _bdot(w1 * inv, et) * outs[1] + _bdot(w2 * inv, et) * outs[2]
    out_ref[...] = h_ref[...] + _bdot(att, wo_ref[...])


def _attn_out(outs, lses, h, wo, et, seq_len):
    n = h.shape[0]
    tm = _tile(n, PROJ_TILE)
    tps = seq_len // tm
    row_spec = pl.BlockSpec((tm, D_MODEL), lambda i: (i, 0))

    def rm_spec(dil, width):
        return pl.BlockSpec((None, dil, tm // dil, width), lambda i: (i // tps, 0, i % tps, 0))

    dils = [dil for _, dil in DILATION_GROUPS]
    scratch = []
    for _ in dils:
        scratch += [pltpu.VMEM((D_MODEL // LANES, tm, LANES), F32), pltpu.VMEM((tm, LANES), F32)]
    return pl.pallas_call(
        functools.partial(_attn_out_kernel, tm=tm),
        grid=(n // tm,),
        in_specs=[rm_spec(d, D_MODEL) for d in dils] + [rm_spec(d, LANES) for d in dils]
                 + [row_spec, _const_spec(wo.shape), _const_spec(et.shape)],
        out_specs=row_spec,
        out_shape=jax.ShapeDtypeStruct((n, D_MODEL), F32),
        scratch_shapes=scratch,
        compiler_params=_params("arbitrary"),
        name="attn_out",
    )(*outs, *lses, h, wo, et)


def _decode_bias(rel_bias, seq_len, cache_len):
    ncol = cache_len + LANES
    period = seq_len + ncol - 1
    idx = np.arange(period)
    dist = cache_len - np.where(idx < ncol, idx, idx - period)
    buckets = _t5_buckets(np.clip(dist, 0, MAX_WINDOW))
    tabs = []
    for g, (win, dil) in enumerate(DILATION_GROUPS):
        valid = (dist >= 0) & (dist % dil == 0) & (dist <= win)
        tbl = jnp.take(rel_bias[:, g * N_HEADS:(g + 1) * N_HEADS].astype(F32), buckets, axis=0).T
        tabs.append(jnp.where(valid[None], tbl, NEG_INF))
    rows = _toeplitz(jnp.stack(tabs), seq_len, ncol)
    bias = rows.reshape(N_GROUPS * N_HEADS * seq_len, ncol)
    return bias[:, :cache_len], bias[:, cache_len:]


def _attn_decode_kernel(q_ref, cache_ref, kvn_ref, h_ref, bc_ref, bn_ref, wo_ref, out_ref, *, seq_len):
    kvd = KV_HEADS * HEAD_DIM
    nslot = N_GROUPS * N_HEADS
    rows_g = N_HEADS * seq_len
    nt = (((1,), (1,)), ((), ()))
    cache = cache_ref[...]
    kc = cache[:, :kvd].astype(BF16)
    vc = cache[:, kvd:].astype(BF16)
    kvn = kvn_ref[...]
    pad = jnp.zeros((LANES - seq_len, kvd), F32)
    kn = jnp.concatenate([kvn[:, :kvd], pad], axis=0).astype(BF16)
    vn = jnp.concatenate([kvn[:, kvd:], pad], axis=0).astype(BF16)
    lhs = jnp.concatenate([q_ref[:, s * kvd:(s + 1) * kvd] for s in range(nslot)], axis=0).astype(BF16)
    cache_len = kc.shape[0]
    sn = lax.dot_general(lhs, kn, nt, preferred_element_type=F32) + bn_ref[...]
    scs, m = [], None
    for g, (win, _) in enumerate(DILATION_GROUPS):
        c0 = (cache_len - min(win, cache_len)) // LANES * LANES
        rows = slice(g * rows_g, (g + 1) * rows_g)
        sc = lax.dot_general(lhs[rows], kc[c0:], nt, preferred_element_type=F32) + bc_ref[rows, c0:]
        scs.append((sc, c0, rows))
        m_g = jnp.maximum(jnp.max(sc, axis=-1, keepdims=True), jnp.max(sn[rows], axis=-1, keepdims=True))
        m = m_g if m is None else jnp.maximum(m, m_g)
    l = jnp.zeros((rows_g, 1), F32)
    num = jnp.zeros((rows_g, kvd), F32)
    for sc, c0, rows in scs:
        pc = jnp.exp(sc - m)
        pn = jnp.exp(sn[rows] - m)
        l = l + jnp.sum(pc, axis=-1, keepdims=True) + jnp.sum(pn, axis=-1, keepdims=True)
        num = (num + jnp.dot(pc.astype(BF16), vc[c0:], preferred_element_type=F32)
               + jnp.dot(pn.astype(BF16), vn, preferred_element_type=F32))
    row = lax.broadcasted_iota(jnp.int32, (rows_g, kvd), 0)
    lane = lax.broadcasted_iota(jnp.int32, (rows_g, kvd), 1)
    own = (row // (Q_PER_KV * seq_len)) == (lane // HEAD_DIM)
    att = jnp.where(own, num / l, 0.0)
    out = h_ref[...]
    for r in range(Q_PER_KV):
        a_r = att[r * seq_len:(r + 1) * seq_len]
        for c in range(1, KV_HEADS):
            a_r = a_r + att[(c * Q_PER_KV + r) * seq_len:(c * Q_PER_KV + r + 1) * seq_len]
        out = out + _bdot(a_r, wo_ref[r])
    out_ref[...] = out


def _attn_decode(q, cache, kv_new, h, bias_c, bias_n, wo_r, nb, seq_len):
    cache_len = cache.shape[1]
    qw = q.shape[1]
    kvw = 2 * KV_HEADS * HEAD_DIM
    return pl.pallas_call(
        functools.partial(_attn_decode_kernel, seq_len=seq_len),
        grid=(nb,),
        in_specs=[pl.BlockSpec((seq_len, qw), lambda b: (b, 0)),
                  pl.BlockSpec((None, cache_len, kvw), lambda b: (b, 0, 0)),
                  pl.BlockSpec((seq_len, kvw), lambda b: (b, 0)),
                  pl.BlockSpec((seq_len, D_MODEL), lambda b: (b, 0)),
                  _const_spec(bias_c.shape), _const_spec(bias_n.shape), _const_spec(wo_r.shape)],
        out_specs=pl.BlockSpec((seq_len, D_MODEL), lambda b: (b, 0)),
        out_shape=jax.ShapeDtypeStruct((nb * seq_len, D_MODEL), F32),
        compiler_params=_params("arbitrary"),
        name="attn_decode",
    )(q, cache, kv_new, h, bias_c, bias_n, wo_r)


def _prep_weights(norm_w, ffn1_wi, ffn1_wo, ffn2_wi, ffn2_wo, pe_proj, pe_gate,
                  rwkv_mix, rwkv_wrkv, rwkv_wo, rwkv_w0, rwkv_w1, rwkv_w2, rwkv_a0, rwkv_a1, rwkv_a2,
                  rwkv_g1, rwkv_g2, rwkv_kk, rwkv_ka, rwkv_rk, rwkv_lnx_w, rwkv_lnx_b,
                  attn_wq, attn_wo, kv_norm, w_kv, rel_bias, final_norm):
    def row(v):
        return v.reshape(1, -1).astype(F32)

    def pad_cols(w, n):
        return jnp.pad(w, ((0, 0), (0, n - w.shape[1]))).astype(BF16)

    def pad_rows(w, n):
        return jnp.pad(w, ((0, n - w.shape[0]), (0, 0))).astype(BF16)

    head_of_lane = np.arange(D_MODEL) // HEAD_DIM
    e = jnp.asarray(head_of_lane[:, None] == np.arange(LANES)[None, :], BF16)
    et = jnp.asarray(np.arange(LANES)[:, None] == head_of_lane[None, :], BF16)

    depth = norm_w.shape[0]
    layers = []
    for i in range(depth):
        layers.append(dict(
            nw=[row(norm_w[i, j]) for j in range(4)],
            ffn1=(ffn1_wi.astype(F32), ffn1_wo.astype(F32)), ffn2=(ffn2_wi.astype(F32), ffn2_wo.astype(F32)),
            pe_gate=pe_gate[i].astype(BF16), pe_proj=pe_proj[i].astype(BF16)))
    n_a = depth // 2
    rw = []
    for i in range(n_a):
        rw.append(dict(
            nw=row(norm_w[i, 1]), mix=rwkv_mix[i].astype(F32), wrkv=rwkv_wrkv[i].astype(BF16),
            w0=row(rwkv_w0[i]), w1=pad_cols(rwkv_w1[i], LANES), w2=pad_rows(rwkv_w2[i], LANES),
            a0=row(rwkv_a0[i]), a1=pad_cols(rwkv_a1[i], LANES), a2=pad_rows(rwkv_a2[i], LANES),
            g1=pad_cols(rwkv_g1[i], 2 * LANES), g2=pad_rows(rwkv_g2[i], 2 * LANES),
            kk=row(rwkv_kk[i]), ka=row(rwkv_ka[i]), rk=row(rwkv_rk[i]),
            lnw=row(rwkv_lnx_w[i]), lnb=row(rwkv_lnx_b[i]), wo=rwkv_wo[i].astype(BF16), e=e, et=et))
    scale = HEAD_DIM ** -0.5
    at = []
    for j in range(depth - n_a):
        wq = attn_wq[j] * scale
        wq6 = wq.reshape(D_MODEL, N_GROUPS, KV_HEADS, Q_PER_KV, HEAD_DIM)
        wq_slots = jnp.einsum("dgcre,cx->dgcrxe", wq6, jnp.eye(KV_HEADS, dtype=wq.dtype))
        wq_slots = wq_slots.reshape(D_MODEL, N_GROUPS * N_HEADS * KV_HEADS * HEAD_DIM).astype(BF16)
        wo_r = attn_wo[j].reshape(KV_HEADS, Q_PER_KV, HEAD_DIM, D_MODEL).transpose(1, 0, 2, 3)
        wo_r = wo_r.reshape(Q_PER_KV, KV_HEADS * HEAD_DIM, D_MODEL).astype(BF16)
        wq_seg = wq.reshape(D_MODEL, N_GROUPS, N_HEADS, HEAD_DIM)[:, :, _HEAD_OF_SEG].reshape(D_MODEL, -1)
        wo_seg = attn_wo[j].reshape(N_HEADS, HEAD_DIM, D_MODEL)[_HEAD_OF_SEG].reshape(D_MODEL, D_MODEL)
        at.append(dict(wq_seg=wq_seg.astype(BF16), wq_slots=wq_slots, wo_seg=wo_seg.astype(BF16), wo_r=wo_r))
    return dict(layers=layers, rwkv=rw, attn=at, kv_norm=row(kv_norm), w_kv=w_kv.astype(BF16),
                final_norm=row(final_norm), rel_bias=rel_bias, et=et)


def _trunk(x, p, wkv0, shift0, cache, w):
    nb, seq_len, _ = x.shape
    n = nb * seq_len
    depth = len(w["layers"])
    n_a = depth // 2
    h = x.reshape(n, D_MODEL).astype(F32)
    wkv_out, shift_out = [], []
    kv_rows = kv_rm = None
    kvw = 2 * KV_HEADS * HEAD_DIM
    for i in range(depth):
        lw = w["layers"][i]
        if i == n_a:
            if cache is None:
                kv_rows, *kv_rm = _proj_rm(h, w["kv_norm"], w["w_kv"], nb, seq_len, True,
                                           [(0, kvw)] * N_GROUPS, "kv_proj")
            else:
                kv_rows = _norm_mm(h, w["kv_norm"], w["w_kv"], F32, name="kv_proj")
        h = _ffn(h, lw["nw"][0], *lw["ffn1"], layer=i)
        if i < n_a:
            rwl = w["rwkv"][i]
            h, sh, st = _rwkv_layer(h, shift0[i].astype(F32), wkv0[i].astype(F32), seq_len, rwl)
            wkv_out.append(st.astype(x.dtype))
            shift_out.append(sh.astype(x.dtype))
        else:
            al = w["attn"][i - n_a]
            if cache is None:
                q_rm = _proj_rm(h, lw["nw"][1], al["wq_seg"], nb, seq_len, False,
                                [(gi * D_MODEL, (gi + 1) * D_MODEL) for gi in range(N_GROUPS)], "q_proj")
                outs, lses = [], []
                for gi in range(N_GROUPS):
                    o, lse = _attn_group(q_rm[gi], kv_rm[gi], _band_bias(w["rel_bias"], gi))
                    outs.append(o)
                    lses.append(lse)
                h = _attn_out(outs, lses, h, al["wo_seg"], w["et"], seq_len)
            else:
                q = _norm_mm(h, lw["nw"][1], al["wq_slots"], F32, col_tile=2048, name="q_proj_slots")
                bias_c, bias_n = _decode_bias(w["rel_bias"], seq_len, cache.shape[1])
                h = _attn_decode(q, cache, kv_rows, h, bias_c, bias_n, al["wo_r"], nb, seq_len)
        pe = (p.reshape(depth, n, PLE_DIM).astype(F32), lw["nw"][3], lw["pe_gate"], lw["pe_proj"], w["final_norm"])
        h = _ffn(h, lw["nw"][2], *lw["ffn2"], pe=pe, layer=i, final=(i == depth - 1))
    y = h.reshape(nb, seq_len, D_MODEL).astype(x.dtype)
    kv_rows = kv_rows.reshape(nb, seq_len, 2, KV_HEADS, HEAD_DIM).astype(x.dtype)
    return y, jnp.stack(wkv_out), jnp.stack(shift_out), kv_rows


def kernel(x_prompt, x_sample, state_wkv, state_shift, cache_kv, p_prompt, p_sample, norm_w, ffn1_wi, ffn1_wo, ffn2_wi, ffn2_wo, pe_proj, pe_gate, rwkv_mix, rwkv_wrkv, rwkv_wo, rwkv_w0, rwkv_w1, rwkv_w2, rwkv_a0, rwkv_a1, rwkv_a2, rwkv_g1, rwkv_g2, rwkv_kk, rwkv_ka, rwkv_rk, rwkv_lnx_w, rwkv_lnx_b, attn_wq, attn_wo, kv_norm, w_kv, rel_bias, final_norm):
    w = _prep_weights(norm_w, ffn1_wi, ffn1_wo, ffn2_wi, ffn2_wo, pe_proj, pe_gate,
                      rwkv_mix, rwkv_wrkv, rwkv_wo, rwkv_w0, rwkv_w1, rwkv_w2, rwkv_a0, rwkv_a1, rwkv_a2,
                      rwkv_g1, rwkv_g2, rwkv_kk, rwkv_ka, rwkv_rk, rwkv_lnx_w, rwkv_lnx_b,
                      attn_wq, attn_wo, kv_norm, w_kv, rel_bias, final_norm)
    n_a = norm_w.shape[0] // 2
    nb, seq_len, _ = x_prompt.shape
    wkv0 = jnp.zeros((n_a, nb, N_HEADS, HEAD_DIM, HEAD_DIM), F32)
    shift0 = jnp.zeros((n_a, nb, D_MODEL), x_prompt.dtype)
    y_p, wkv_p, shift_p, kv_p = _trunk(x_prompt, p_prompt, wkv0, shift0, None, w)
    kv_prompt = kv_p[:, seq_len - min(MAX_WINDOW, seq_len):]
    cache = cache_kv.reshape(cache_kv.shape[0], cache_kv.shape[1], 2 * KV_HEADS * HEAD_DIM).astype(F32)
    y_s, wkv_s, shift_s, kv_s = _trunk(x_sample, p_sample, state_wkv, state_shift, cache, w)
    return (y_p, y_s, wkv_p, shift_p, kv_prompt, wkv_s, shift_s, kv_s)
```

```python
import functools

import numpy as np
import jax
import jax.numpy as jnp
from jax import lax
from jax.experimental import pallas as pl
from jax.experimental.pallas import tpu as pltpu

F32 = jnp.float32
BF16 = jnp.bfloat16

D_MODEL = 1024
D_FF = 2816
PLE_DIM = 256
RMS_EPS = 1e-6
HEAD_DIM = 64
N_HEADS = D_MODEL // HEAD_DIM
LNX_EPS = 64e-5
KV_HEADS = 4
Q_PER_KV = N_HEADS // KV_HEADS
DILATION_GROUPS = ((128, 1), (512, 4), (2048, 16))
N_GROUPS = len(DILATION_GROUPS)
MAX_WINDOW = 2048
REL_BUCKETS = 32
REL_MAX_DIST = 2048
NEG_INF = -1e30

LANES = 128
FFN_CHUNK = 256
N_FFN_CHUNKS = D_FF // FFN_CHUNK
TOKEN_TILE = 256
FFN_TILE = 512
PROJ_TILE = 512
WKV_CHUNK = 64
RWKV_SHORT_SEQS = 8
Q_BLOCK = 128
ATTN_SPAN = 2048
VMEM_LIMIT = 56 * 1024 * 1024
EXP_MINUS_HALF = 0.6065306597126334


def _params(*sem):
    return pltpu.CompilerParams(dimension_semantics=sem, vmem_limit_bytes=VMEM_LIMIT)


def _const_spec(shape):
    return pl.BlockSpec(shape, lambda *_: (0,) * len(shape))


def _tile(n, pref=TOKEN_TILE):
    t = min(n, pref)
    while n % t:
        t -= 8
    return t


def _rms(x, g):
    return x * lax.rsqrt(jnp.mean(x * x, axis=-1, keepdims=True) + RMS_EPS) * g


def _bdot(a, b):
    return jnp.dot(a.astype(BF16), b, preferred_element_type=F32)


def _head_sum(x, e_ref, et_ref):
    return _bdot(_bdot(x, e_ref[...]), et_ref[...])


def _ffn_kernel(x_ref, g_ref, wi_ref, wo_ref, *rest, with_pe, final):
    o_ref = rest[-1]
    x = x_ref[...]
    xn = _rms(x, g_ref[...]).astype(BF16)
    acc = jnp.zeros_like(x)
    for j in range(N_FFN_CHUNKS):
        lo, hi = j * FFN_CHUNK, (j + 1) * FFN_CHUNK
        gate = jnp.dot(xn, wi_ref[:, lo:hi].astype(BF16), preferred_element_type=F32)
        up = jnp.dot(xn, wi_ref[:, D_FF + lo:D_FF + hi].astype(BF16), preferred_element_type=F32)
        act = (gate * jax.nn.sigmoid(gate) * up).astype(BF16)
        acc = acc + jnp.dot(act, wo_ref[lo:hi, :].astype(BF16), preferred_element_type=F32)
    y = x + 0.5 * acc
    if with_pe:
        p_ref, gp_ref, wgate_ref, wproj_ref, gf_ref = rest[:-1]
        gate = jax.nn.sigmoid(_bdot(_rms(y, gp_ref[...]), wgate_ref[...]))
        y = y + gate * _bdot(p_ref[...], wproj_ref[...])
        if final:
            y = _rms(y, gf_ref[...])
    o_ref[...] = y


def _ffn(h, g, wi, wo, pe=None, layer=0, final=False):
    n = h.shape[0]
    tm = _tile(n, FFN_TILE)
    row_spec = pl.BlockSpec((tm, D_MODEL), lambda i: (i, 0))
    args = [h, g, wi, wo]
    in_specs = [row_spec, _const_spec((1, D_MODEL)),
                pl.BlockSpec((None,) + wi.shape[1:], lambda i: (layer, 0, 0)),
                pl.BlockSpec((None,) + wo.shape[1:], lambda i: (layer, 0, 0))]
    if pe is not None:
        args += list(pe)
        in_specs += ([pl.BlockSpec((None, tm, PLE_DIM), lambda i: (layer, i, 0))]
                     + [_const_spec(a.shape) for a in pe[1:]])
    return pl.pallas_call(
        functools.partial(_ffn_kernel, with_pe=pe is not None, final=final),
        grid=(n // tm,),
        in_specs=in_specs,
        out_specs=row_spec,
        out_shape=jax.ShapeDtypeStruct((n, D_MODEL), F32),
        compiler_params=_params("arbitrary"),
        name="ffn_pe" if pe is not None else "ffn",
    )(*args)


def _norm_mm_kernel(x_ref, g_ref, w_ref, o_ref):
    o_ref[...] = _bdot(_rms(x_ref[...], g_ref[...]), w_ref[...]).astype(o_ref.dtype)


def _norm_mm(h, g, w, out_dtype, col_tile=None, name="norm_mm"):
    n = h.shape[0]
    tm = _tile(n)
    nout = w.shape[1]
    tn = nout if col_tile is None else col_tile
    return pl.pallas_call(
        _norm_mm_kernel,
        grid=(n // tm, nout // tn),
        in_specs=[pl.BlockSpec((tm, D_MODEL), lambda i, j: (i, 0)), _const_spec((1, D_MODEL)),
                  pl.BlockSpec((D_MODEL, tn), lambda i, j: (0, j))],
        out_specs=pl.BlockSpec((tm, tn), lambda i, j: (i, j)),
        out_shape=jax.ShapeDtypeStruct((n, nout), out_dtype),
        compiler_params=_params("arbitrary", "arbitrary"),
        name=name,
    )(h, g, w)


def _rwkv_layer_kernel(h_ref, sh_ref, s0_ref, nw_ref, mix_ref, wrkv_ref, w0_ref, w1_ref, w2_ref, a0_ref, a1_ref,
                       a2_ref, g1_ref, g2_ref, kk_ref, ka_ref, lnw_ref, lnb_ref, rk_ref, wo_ref, e_ref, et_ref,
                       o_ref, hn_o, st_ref,
                       state, prev, r_s, lw_s, k_s, v_s, kk_s, b_s, g_s, y_s, *, rows, n_sub, nseq):
    c = pl.program_id(1)
    C = WKV_CHUNK
    C2 = 2 * C
    n_pairs = N_HEADS // 2
    seq_rows = rows // nseq
    short = seq_rows % C != 0

    def load_state(s0):
        state[...] = jnp.zeros_like(state)
        for h in range(N_HEADS):
            lo = (h % 2) * HEAD_DIM
            state[h // 2, lo:lo + HEAD_DIM, lo:lo + HEAD_DIM] = s0[h]

    def store_state(st):
        for h in range(N_HEADS):
            lo = (h % 2) * HEAD_DIM
            st[h] = state[h // 2, lo:lo + HEAD_DIM, lo:lo + HEAD_DIM]

    if nseq == 1:
        @pl.when(c == 0)
        def _():
            load_state(s0_ref.at[0])
            prev[...] = sh_ref[...]
    if short:
        for ref in (r_s, lw_s, k_s, v_s, kk_s, b_s):
            ref[rows:, :] = jnp.zeros((ref.shape[0] - rows, D_MODEL), F32)

    nw = nw_ref[...]
    hn = _rms(h_ref[...], nw)
    row = lax.broadcasted_iota(jnp.int32, hn.shape, 0)
    rolled = pltpu.roll(hn, 1, 0)
    if nseq == 1:
        x_prev = jnp.where(row == 0, prev[...], rolled)
        prev[...] = hn[rows - 1:rows]
        hn_o[...] = hn[rows - 1:rows]
    else:
        x_prev = jnp.where(lax.rem(row, seq_rows) == 0, sh_ref[...], rolled)
        hn_o[...] = hn
    xx = x_prev - hn
    mix = mix_ref[...]
    xr, xw, xk, xv, xa, xg = (hn + xx * mix[j:j + 1] for j in range(6))
    r = _bdot(xr, wrkv_ref[0])
    k = _bdot(xk, wrkv_ref[1])
    v = _bdot(xv, wrkv_ref[2])
    wl = w0_ref[...] + _bdot(jnp.tanh(_bdot(xw, w1_ref[...])), w2_ref[...])
    lw_s[0:rows] = -EXP_MINUS_HALF * jax.nn.sigmoid(wl)
    a = jax.nn.sigmoid(a0_ref[...] + _bdot(_bdot(xa, a1_ref[...]), a2_ref[...]))
    g_s[...] = _bdot(jax.nn.sigmoid(_bdot(xg, g1_ref[...])), g2_ref[...])
    kkv = k * kk_ref[...]
    kk = kkv / jnp.maximum(jnp.sqrt(_head_sum(kkv * kkv, e_ref, et_ref)), 1e-12)
    r_s[0:rows] = r
    k_s[0:rows] = k * (1.0 + (a - 1.0) * ka_ref[...])
    v_s[0:rows] = v
    kk_s[0:rows] = kk
    b_s[0:rows] = kk * a

    ri = lax.broadcasted_iota(jnp.int32, (C, C), 0)
    ci = lax.broadcasted_iota(jnp.int32, (C, C), 1)
    tri = (ri >= ci).astype(BF16)
    r2 = lax.broadcasted_iota(jnp.int32, (C2, C2), 0)
    c2 = lax.broadcasted_iota(jnp.int32, (C2, C2), 1)
    strict = r2 > c2
    incl = r2 >= c2
    lane = lax.broadcasted_iota(jnp.int32, (C, LANES), 1)
    head0 = lane < HEAD_DIM

    def hat(x):
        x3 = jnp.stack([x[:, p * LANES:(p + 1) * LANES] for p in range(n_pairs)])
        return jnp.concatenate([jnp.where(head0, x3, 0.0), jnp.where(head0, 0.0, x3)], axis=1)

    def mm(a, b):
        return jnp.einsum("pmk,pkn->pmn", a.astype(BF16), b.astype(BF16), preferred_element_type=F32)

    def mm_nt(a, b):
        return jnp.einsum("pmk,pnk->pmn", a.astype(BF16), b.astype(BF16), preferred_element_type=F32)

    chunk_row = lax.broadcasted_iota(jnp.int32, (C, D_MODEL), 0)

    def chunk(ref, rws):
        x = ref[rws, :]
        return jnp.where(chunk_row < seq_rows, x, 0.0) if short else x

    def sub_chunk(s, carry):
        start = s * seq_rows if short else s * C
        rws = pl.ds(pl.multiple_of(start, 8), C)
        lw = chunk(lw_s, rws)
        p1 = lw.astype(BF16)
        rem = lw - p1.astype(F32)
        p2 = rem.astype(BF16)
        p3 = (rem - p2.astype(F32)).astype(BF16)
        cw = (jnp.dot(tri, p1, preferred_element_type=F32) + jnp.dot(tri, p2, preferred_element_type=F32)
              + jnp.dot(tri, p3, preferred_element_type=F32))
        cw_end = cw[C - 1:C, :]
        e_neg = jnp.exp(-cw)
        e_end = jnp.exp(cw_end - cw)
        kk_c = chunk(kk_s, rws)
        bb = chunk(b_s, rws)
        k_c = chunk(k_s, rws)
        a_all = kk_c * jnp.exp(cw - lw)
        r_all = chunk(r_s, rws) * jnp.exp(cw)
        b_all = bb * e_neg
        k_all = k_c * e_neg
        bd_all = bb * e_end
        kd_all = k_c * e_end
        v_all = chunk(v_s, rws)
        decay = jnp.exp(cw_end)
        ar_h = jnp.concatenate([hat(a_all), hat(r_all)], axis=1)
        bk_h = jnp.concatenate([hat(b_all), hat(k_all)], axis=1)
        v_h = hat(v_all)
        st = state[...]
        g = mm_nt(ar_h, bk_h)
        low = jnp.where(strict, g[:, :C2, :C2], 0.0)
        ak = jnp.where(strict, g[:, :C2, C2:], 0.0)
        rbk = jnp.concatenate([jnp.where(incl, g[:, C2:, :C2], 0.0), jnp.where(incl, g[:, C2:, C2:], 0.0)], axis=2)
        ss = mm_nt(ar_h, st)
        x = -(ss[:, :C2] + mm(ak, v_h))
        t = mm(low, jnp.concatenate([low, x], axis=2))
        lp = t[:, :, :C2]
        x = x - t[:, :, C2:]
        for _ in range(4):
            t = mm(lp, jnp.concatenate([lp, x], axis=2))
            lp = t[:, :, :C2]
            x = x + t[:, :, C2:]
        x = x + mm(lp, x)
        xv = jnp.concatenate([x, v_h], axis=1)
        y_h = ss[:, C2:] + mm(rbk, xv)
        y = y_h[:, :C] + y_h[:, C:]
        out_rows = pl.ds(pl.multiple_of(start, 8), seq_rows) if short else rws
        for p in range(n_pairs):
            y_s[out_rows, p * LANES:(p + 1) * LANES] = y[p, :seq_rows] if short else y[p]
        bkd_h = jnp.concatenate([hat(bd_all), hat(kd_all)], axis=1)
        xv_t = jnp.stack([xv[p].T for p in range(n_pairs)])
        dec3 = jnp.stack([decay[:, p * LANES:(p + 1) * LANES] for p in range(n_pairs)])
        state[...] = st * dec3 + mm(xv_t, bkd_h)
        return carry

    if nseq == 1:
        lax.fori_loop(0, n_sub, sub_chunk, 0, unroll=True)
    else:
        def one_sequence(s, carry):
            load_state(s0_ref.at[s])
            sub_chunk(s, carry)
            store_state(st_ref.at[s])
            return carry

        lax.fori_loop(0, nseq, one_sequence, 0)

    y = y_s[0:rows]
    inv_n = 1.0 / HEAD_DIM
    mu = _head_sum(y, e_ref, et_ref) * inv_n
    yc = y - mu
    var = _head_sum(yc * yc, e_ref, et_ref) * inv_n
    yn = yc * lax.rsqrt(var + LNX_EPS) * lnw_ref[...] + lnb_ref[...]
    bonus = _head_sum(r_s[0:rows] * k_s[0:rows] * rk_ref[...], e_ref, et_ref) * v_s[0:rows]
    o_ref[...] = h_ref[...] + _bdot((yn + bonus) * g_s[...], wo_ref[...])

    if nseq == 1:
        @pl.when(c == pl.num_programs(1) - 1)
        def _():
            store_state(st_ref.at[0])


def _rwkv_layer(h, shift, s0, seq_len, lw):
    n = h.shape[0]
    nb = n // seq_len
    if seq_len % WKV_CHUNK == 0:
        nseq, rows = 1, _tile(seq_len)
        steps, n_sub, buf_rows = seq_len // rows, rows // WKV_CHUNK, rows
        assert rows % WKV_CHUNK == 0
        sh_in = shift.reshape(nb, 1, D_MODEL)
        sh_spec = pl.BlockSpec((None, 1, D_MODEL), lambda bi, ci: (bi, 0, 0))
        hn_shape, hn_spec = jax.ShapeDtypeStruct((nb, 1, D_MODEL), F32), sh_spec
    else:
        assert seq_len < WKV_CHUNK and seq_len % 8 == 0
        nseq = _tile(nb, RWKV_SHORT_SEQS)
        rows, steps, n_sub = nseq * seq_len, 1, 1
        buf_rows = rows + WKV_CHUNK - seq_len
        sh_in = jnp.repeat(shift, seq_len, axis=0)
        sh_spec = pl.BlockSpec((rows, D_MODEL), lambda bi, ci: (bi, 0))
        hn_shape, hn_spec = jax.ShapeDtypeStruct((n, D_MODEL), F32), sh_spec
    row_spec = pl.BlockSpec((rows, D_MODEL), lambda bi, ci: (bi * steps + ci, 0))
    st_spec = pl.BlockSpec((nseq, N_HEADS, HEAD_DIM, HEAD_DIM), lambda bi, ci: (bi, 0, 0, 0))
    consts = [lw["nw"], lw["mix"], lw["wrkv"], lw["w0"], lw["w1"], lw["w2"], lw["a0"], lw["a1"], lw["a2"],
              lw["g1"], lw["g2"], lw["kk"], lw["ka"], lw["lnw"], lw["lnb"], lw["rk"], lw["wo"], lw["e"], lw["et"]]
    seq_buf = pltpu.VMEM((buf_rows, D_MODEL), F32)
    out, hn, st = pl.pallas_call(
        functools.partial(_rwkv_layer_kernel, rows=rows, n_sub=n_sub, nseq=nseq),
        grid=(nb // nseq, steps),
        in_specs=[row_spec, sh_spec, st_spec] + [_const_spec(c.shape) for c in consts],
        out_specs=[row_spec, hn_spec, st_spec],
        out_shape=[jax.ShapeDtypeStruct((n, D_MODEL), F32), hn_shape, jax.ShapeDtypeStruct(s0.shape, F32)],
        scratch_shapes=[pltpu.VMEM((N_HEADS // 2, LANES, LANES), F32), pltpu.VMEM((1, D_MODEL), F32)]
                       + [seq_buf] * 6 + [pltpu.VMEM((rows, D_MODEL), F32), seq_buf],
        compiler_params=_params("arbitrary", "arbitrary"),
        name="rwkv_layer",
    )(h, sh_in, s0, *consts)
    shift_out = hn.reshape(nb, D_MODEL) if nseq == 1 else hn.reshape(nb, seq_len, D_MODEL)[:, -1]
    return out, shift_out, st


def _t5_buckets(dist):
    d = np.asarray(dist, dtype=np.int64)
    max_exact = REL_BUCKETS // 2
    large = max_exact + (np.log(np.maximum(d, 1) / max_exact) / np.log(REL_MAX_DIST / max_exact)
                         * (REL_BUCKETS - max_exact)).astype(np.int32)
    large = np.minimum(large, REL_BUCKETS - 1)
    return np.where(d < max_exact, d, large).astype(np.int32)


def _toeplitz(tab, n_rows, n_cols):
    period = tab.shape[-1]
    assert period >= n_rows + n_cols - 1 and n_cols <= period - 1
    lead = tab.shape[:-1]
    flat = jnp.broadcast_to(tab[..., None, :], lead + (n_rows, period)).reshape(lead + (-1,))
    skew = flat[..., :n_rows * (period - 1)].reshape(lead + (n_rows, period - 1))
    return skew[..., :n_cols]


def _band_bias(rel_bias, group):
    win, dil = DILATION_GROUPS[group]
    assert win // dil == Q_BLOCK
    period = 3 * Q_BLOCK - 1
    idx = np.arange(period)
    m = Q_BLOCK - np.where(idx < 2 * Q_BLOCK, idx, idx - period)
    valid = (m >= 0) & (m <= Q_BLOCK)
    buckets = _t5_buckets(dil * np.clip(m, 0, Q_BLOCK))
    tbl = jnp.take(rel_bias[:, group * N_HEADS:(group + 1) * N_HEADS].astype(F32), buckets, axis=0).T
    general = _toeplitz(jnp.where(valid[None], tbl, NEG_INF), Q_BLOCK, 2 * Q_BLOCK)
    first = jnp.where((np.arange(2 * Q_BLOCK) >= Q_BLOCK)[None, None, :], general, NEG_INF)
    return jnp.stack([first, general])


_HEAD_OF_SEG = np.array([4 * (2 * (s // 8) + s % 2) + (s // 2) % 4 for s in range(N_HEADS)])


def _proj_rm_kernel(x_ref, g_ref, w_ref, *refs, natural, lane_ranges, tm):
    scr = refs[-1]
    outs = refs[:-1]
    y = _bdot(_rms(x_ref[...], g_ref[...]), w_ref[...])
    if natural:
        outs[0][...] = y
        outs = outs[1:]
    for c in range(scr.shape[0]):
        scr[c] = y[:, c * LANES:(c + 1) * LANES]
    for gi, (_, dil) in enumerate(DILATION_GROUPS):
        lo, hi = lane_ranges[gi]
        if dil == 1:
            outs[gi][0] = y[:, lo:hi].astype(BF16)
            continue
        for rho in range(dil):
            rows = [scr[c, pl.ds(rho, tm // dil, stride=dil), :] for c in range(lo // LANES, hi // LANES)]
            outs[gi][rho] = jnp.concatenate(rows, axis=1).astype(BF16)


def _proj_rm(h, g, w, nb, seq_len, natural, lane_ranges, name):
    n = h.shape[0]
    tm = _tile(n, PROJ_TILE)
    nout = w.shape[1]
    assert seq_len % tm == 0 and all(tm % (16 * dil) == 0 for _, dil in DILATION_GROUPS)
    tps = seq_len // tm
    out_shape, out_specs = [], []
    if natural:
        out_shape.append(jax.ShapeDtypeStruct((n, nout), F32))
        out_specs.append(pl.BlockSpec((tm, nout), lambda i: (i, 0)))
    for (_, dil), (lo, hi) in zip(DILATION_GROUPS, lane_ranges):
        out_shape.append(jax.ShapeDtypeStruct((nb, dil, seq_len // dil, hi - lo), BF16))
        out_specs.append(pl.BlockSpec((None, dil, tm // dil, hi - lo), lambda i: (i // tps, 0, i % tps, 0)))
    return pl.pallas_call(
        functools.partial(_proj_rm_kernel, natural=natural, lane_ranges=lane_ranges, tm=tm),
        grid=(n // tm,),
        in_specs=[pl.BlockSpec((tm, D_MODEL), lambda i: (i, 0)), _const_spec((1, D_MODEL)), _const_spec(w.shape)],
        out_specs=out_specs,
        out_shape=out_shape,
        scratch_shapes=[pltpu.VMEM((nout // LANES, tm, LANES), F32)],
        compiler_params=_params("arbitrary"),
        name=name,
    )(h, g, w)


def _attn_kernel(q_ref, kv_ref, halo_ref, bias_ref, o_ref, lse_ref, kvbuf, *, nq):
    i = pl.program_id(0)
    dil = q_ref.shape[0]
    kvd = KV_HEADS * HEAD_DIM
    nt = (((1,), (1,)), ((), ()))
    kvbuf[:, :Q_BLOCK, :] = halo_ref[...]
    kvbuf[:, Q_BLOCK:, :] = kv_ref[...]
    lane = lax.broadcasted_iota(jnp.int32, (Q_BLOCK, LANES), 1)
    low_half = lane < HEAD_DIM
    lane_row = lax.broadcasted_iota(jnp.int32, (1, LANES), 1)
    keep_lo = (lane_row < HEAD_DIM).astype(BF16)
    keep_hi = (lane_row >= HEAD_DIM).astype(BF16)

    def block(u, carry):
        rho = lax.div(u, nq)
        j = u - rho * nq
        r0 = pl.multiple_of(j * Q_BLOCK, Q_BLOCK)
        qb = q_ref[rho, pl.ds(r0, Q_BLOCK), :]
        kvb = kvbuf[rho, pl.ds(r0, 2 * Q_BLOCK), :]
        bsel = jnp.where(jnp.logical_and(i == 0, j == 0), 0, 1)
        lse_tile = jnp.zeros((Q_BLOCK, LANES), F32)
        for G in range(KV_HEADS // 2):
            kg = kvb[:, G * LANES:(G + 1) * LANES]
            vg = kvb[:, kvd + G * LANES:kvd + (G + 1) * LANES]
            vcat = jnp.concatenate([vg * keep_lo, vg * keep_hi], axis=0)
            pieces = []
            for r in range(Q_PER_KV):
                qg = qb[:, (G * Q_PER_KV + r) * LANES:(G * Q_PER_KV + r + 1) * LANES]
                pieces += [qg * keep_lo, qg * keep_hi]
            s_all = lax.dot_general(jnp.concatenate(pieces, axis=0), kg, nt, preferred_element_type=F32)
            for r in range(Q_PER_KV):
                grp = G * Q_PER_KV + r
                parts = []
                for half in range(2):
                    s = s_all[(2 * r + half) * Q_BLOCK:(2 * r + half + 1) * Q_BLOCK]
                    s = s + bias_ref[bsel, int(_HEAD_OF_SEG[2 * grp + half])]
                    m = jnp.max(s, axis=-1, keepdims=True)
                    p = jnp.exp(s - m)
                    l = jnp.sum(p, axis=-1, keepdims=True)
                    parts.append((p.astype(BF16), l, m + jnp.log(l)))
                (p0, l0, e0), (p1, l1, e1) = parts
                o = jnp.dot(jnp.concatenate([p0, p1], axis=1), vcat, preferred_element_type=F32)
                o = (o / jnp.where(low_half, l0, l1)).astype(o_ref.dtype)
                o_ref[rho, pl.ds(r0, Q_BLOCK), grp * LANES:(grp + 1) * LANES] = o
                lse_tile = jnp.where(lane == 2 * grp, e0, lse_tile)
                lse_tile = jnp.where(lane == 2 * grp + 1, e1, lse_tile)
        lse_ref[rho, pl.ds(r0, Q_BLOCK), :] = lse_tile
        return carry

    lax.fori_loop(0, dil * nq, block, 0)


def _attn_group(q_rm, kv_rm, bias):
    nb, dil, tsub, _ = q_rm.shape
    kvw = kv_rm.shape[-1]
    rows = ATTN_SPAN // dil
    nq = rows // Q_BLOCK
    assert tsub % rows == 0 and nq >= 1
    span_spec = lambda width: pl.BlockSpec((None, dil, rows, width), lambda i, b: (b, 0, i, 0))
    return pl.pallas_call(
        functools.partial(_attn_kernel, nq=nq),
        grid=(tsub // rows, nb),
        in_specs=[span_spec(D_MODEL), span_spec(kvw),
                  pl.BlockSpec((None, dil, Q_BLOCK, kvw), lambda i, b: (b, 0, jnp.maximum(i * nq - 1, 0), 0)),
                  _const_spec(bias.shape)],
        out_specs=[span_spec(D_MODEL), span_spec(LANES)],
        out_shape=[jax.ShapeDtypeStruct((nb, dil, tsub, D_MODEL), BF16),
                   jax.ShapeDtypeStruct((nb, dil, tsub, LANES), F32)],
        scratch_shapes=[pltpu.VMEM((dil, Q_BLOCK + rows, kvw), BF16)],
        compiler_params=_params("arbitrary", "arbitrary"),
        name=f"attn_d{dil}",
    )(q_rm, kv_rm, kv_rm, bias)


def _attn_out_kernel(o0_ref, o1_ref, o2_ref, l0_ref, l1_ref, l2_ref, h_ref, wo_ref, et_ref, out_ref, *scr, tm):
    outs, lses = [], []
    for gi, (o_ref, l_ref) in enumerate(((o0_ref, l0_ref), (o1_ref, l1_ref), (o2_ref, l2_ref))):
        dil = DILATION_GROUPS[gi][1]
        if dil == 1:
            outs.append(o_ref[0].astype(F32))
            lses.append(l_ref[0])
            continue
        so, sl = scr[2 * gi], scr[2 * gi + 1]
        n_tiles = so.shape[0]
        for rho in range(dil):
            rows = pl.ds(rho, tm // dil, stride=dil)
            for c in range(n_tiles):
                so[c, rows, :] = o_ref[rho, :, c * LANES:(c + 1) * LANES].astype(F32)
            sl[rows, :] = l_ref[rho]
        outs.append(jnp.concatenate([so[c] for c in range(n_tiles)], axis=1))
        lses.append(sl[...])
    l0, l1, l2 = lses
    m = jnp.maximum(jnp.maximum(l0, l1), l2)
    w0, w1, w2 = jnp.exp(l0 - m), jnp.exp(l1 - m), jnp.exp(l2 - m)
    inv = 1.0 / (w0 + w1 + w2)
    et = et_ref[...]
    att = _bdot(w0 * inv, et) * outs[0] + _bdot(w1 * inv, et) * outs[1] + _bdot(w2 * inv, et) * outs[2]
    out_ref[...] = h_ref[...] + _bdot(att, wo_ref[...])


def _attn_out(outs, lses, h, wo, et, seq_len):
    n = h.shape[0]
    tm = _tile(n, PROJ_TILE)
    tps = seq_len // tm
    row_spec = pl.BlockSpec((tm, D_MODEL), lambda i: (i, 0))

    def rm_spec(dil, width):
        return pl.BlockSpec((None, dil, tm // dil, width), lambda i: (i // tps, 0, i % tps, 0))

    dils = [dil for _, dil in DILATION_GROUPS]
    scratch = []
    for _ in dils:
        scratch += [pltpu.VMEM((D_MODEL // LANES, tm, LANES), F32), pltpu.VMEM((tm, LANES), F32)]
    return pl.pallas_call(
        functools.partial(_attn_out_kernel, tm=tm),
        grid=(n // tm,),
        in_specs=[rm_spec(d, D_MODEL) for d in dils] + [rm_spec(d, LANES) for d in dils]
                 + [row_spec, _const_spec(wo.shape), _const_spec(et.shape)],
        out_specs=row_spec,
        out_shape=jax.ShapeDtypeStruct((n, D_MODEL), F32),
        scratch_shapes=scratch,
        compiler_params=_params("arbitrary"),
        name="attn_out",
    )(*outs, *lses, h, wo, et)


def _decode_bias(rel_bias, seq_len, cache_len):
    ncol = cache_len + LANES
    period = seq_len + ncol - 1
    idx = np.arange(period)
    dist = cache_len - np.where(idx < ncol, idx, idx - period)
    buckets = _t5_buckets(np.clip(dist, 0, MAX_WINDOW))
    tabs = []
    for g, (win, dil) in enumerate(DILATION_GROUPS):
        valid = (dist >= 0) & (dist % dil == 0) & (dist <= win)
        tbl = jnp.take(rel_bias[:, g * N_HEADS:(g + 1) * N_HEADS].astype(F32), buckets, axis=0).T
        tabs.append(jnp.where(valid[None], tbl, NEG_INF))
    rows = _toeplitz(jnp.stack(tabs), seq_len, ncol)
    bias = rows.reshape(N_GROUPS * N_HEADS * seq_len, ncol)
    return bias[:, :cache_len], bias[:, cache_len:]


def _attn_decode_kernel(q_ref, cache_ref, kvn_ref, h_ref, bc_ref, bn_ref, wo_ref, out_ref, *, seq_len):
    kvd = KV_HEADS * HEAD_DIM
    nslot = N_GROUPS * N_HEADS
    rows_g = N_HEADS * seq_len
    nt = (((1,), (1,)), ((), ()))
    cache = cache_ref[...]
    kc = cache[:, :kvd].astype(BF16)
    vc = cache[:, kvd:].astype(BF16)
    kvn = kvn_ref[...]
    pad = jnp.zeros((LANES - seq_len, kvd), F32)
    kn = jnp.concatenate([kvn[:, :kvd], pad], axis=0).astype(BF16)
    vn = jnp.concatenate([kvn[:, kvd:], pad], axis=0).astype(BF16)
    lhs = jnp.concatenate([q_ref[:, s * kvd:(s + 1) * kvd] for s in range(nslot)], axis=0).astype(BF16)
    cache_len = kc.shape[0]
    sn = lax.dot_general(lhs, kn, nt, preferred_element_type=F32) + bn_ref[...]
    scs, m = [], None
    for g, (win, _) in enumerate(DILATION_GROUPS):
        c0 = (cache_len - min(win, cache_len)) // LANES * LANES
        rows = slice(g * rows_g, (g + 1) * rows_g)
        sc = lax.dot_general(lhs[rows], kc[c0:], nt, preferred_element_type=F32) + bc_ref[rows, c0:]
        scs.append((sc, c0, rows))
        m_g = jnp.maximum(jnp.max(sc, axis=-1, keepdims=True), jnp.max(sn[rows], axis=-1, keepdims=True))
        m = m_g if m is None else jnp.maximum(m, m_g)
    l = jnp.zeros((rows_g, 1), F32)
    num = jnp.zeros((rows_g, kvd), F32)
    for sc, c0, rows in scs:
        pc = jnp.exp(sc - m)
        pn = jnp.exp(sn[rows] - m)
        l = l + jnp.sum(pc, axis=-1, keepdims=True) + jnp.sum(pn, axis=-1, keepdims=True)
        num = (num + jnp.dot(pc.astype(BF16), vc[c0:], preferred_element_type=F32)
               + jnp.dot(pn.astype(BF16), vn, preferred_element_type=F32))
    row = lax.broadcasted_iota(jnp.int32, (rows_g, kvd), 0)
    lane = lax.broadcasted_iota(jnp.int32, (rows_g, kvd), 1)
    own = (row // (Q_PER_KV * seq_len)) == (lane // HEAD_DIM)
    att = jnp.where(own, num / l, 0.0)
    out = h_ref[...]
    for r in range(Q_PER_KV):
        a_r = att[r * seq_len:(r + 1) * seq_len]
        for c in range(1, KV_HEADS):
            a_r = a_r + att[(c * Q_PER_KV + r) * seq_len:(c * Q_PER_KV + r + 1) * seq_len]
        out = out + _bdot(a_r, wo_ref[r])
    out_ref[...] = out


def _attn_decode(q, cache, kv_new, h, bias_c, bias_n, wo_r, nb, seq_len):
    cache_len = cache.shape[1]
    qw = q.shape[1]
    kvw = 2 * KV_HEADS * HEAD_DIM
    return pl.pallas_call(
        functools.partial(_attn_decode_kernel, seq_len=seq_len),
        grid=(nb,),
        in_specs=[pl.BlockSpec((seq_len, qw), lambda b: (b, 0)),
                  pl.BlockSpec((None, cache_len, kvw), lambda b: (b, 0, 0)),
                  pl.BlockSpec((seq_len, kvw), lambda b: (b, 0)),
                  pl.BlockSpec((seq_len, D_MODEL), lambda b: (b, 0)),
                  _const_spec(bias_c.shape), _const_spec(bias_n.shape), _const_spec(wo_r.shape)],
        out_specs=pl.BlockSpec((seq_len, D_MODEL), lambda b: (b, 0)),
        out_shape=jax.ShapeDtypeStruct((nb * seq_len, D_MODEL), F32),
        compiler_params=_params("arbitrary"),
        name="attn_decode",
    )(q, cache, kv_new, h, bias_c, bias_n, wo_r)


def _prep_weights(norm_w, ffn1_wi, ffn1_wo, ffn2_wi, ffn2_wo, pe_proj, pe_gate,
                  rwkv_mix, rwkv_wrkv, rwkv_wo, rwkv_w0, rwkv_w1, rwkv_w2, rwkv_a0, rwkv_a1, rwkv_a2,
                  rwkv_g1, rwkv_g2, rwkv_kk, rwkv_ka, rwkv_rk, rwkv_lnx_w, rwkv_lnx_b,
                  attn_wq, attn_wo, kv_norm, w_kv, rel_bias, final_norm):
    def row(v):
        return v.reshape(1, -1).astype(F32)

    def pad_cols(w, n):
        return jnp.pad(w, ((0, 0), (0, n - w.shape[1]))).astype(BF16)

    def pad_rows(w, n):
        return jnp.pad(w, ((0, n - w.shape[0]), (0, 0))).astype(BF16)

    head_of_lane = np.arange(D_MODEL) // HEAD_DIM
    e = jnp.asarray(head_of_lane[:, None] == np.arange(LANES)[None, :], BF16)
    et = jnp.asarray(np.arange(LANES)[:, None] == head_of_lane[None, :], BF16)

    depth = norm_w.shape[0]
    layers = []
    for i in range(depth):
        layers.append(dict(
            nw=[row(norm_w[i, j]) for j in range(4)],
            ffn1=(ffn1_wi.astype(F32), ffn1_wo.astype(F32)), ffn2=(ffn2_wi.astype(F32), ffn2_wo.astype(F32)),
            pe_gate=pe_gate[i].astype(BF16), pe_proj=pe_proj[i].astype(BF16)))
    n_a = depth // 2
    rw = []
    for i in range(n_a):
        rw.append(dict(
            nw=row(norm_w[i, 1]), mix=rwkv_mix[i].astype(F32), wrkv=rwkv_wrkv[i].astype(BF16),
            w0=row(rwkv_w0[i]), w1=pad_cols(rwkv_w1[i], LANES), w2=pad_rows(rwkv_w2[i], LANES),
            a0=row(rwkv_a0[i]), a1=pad_cols(rwkv_a1[i], LANES), a2=pad_rows(rwkv_a2[i], LANES),
            g1=pad_cols(rwkv_g1[i], 2 * LANES), g2=pad_rows(rwkv_g2[i], 2 * LANES),
            kk=row(rwkv_kk[i]), ka=row(rwkv_ka[i]), rk=row(rwkv_rk[i]),
            lnw=row(rwkv_lnx_w[i]), lnb=row(rwkv_lnx_b[i]), wo=rwkv_wo[i].astype(BF16), e=e, et=et))
    scale = HEAD_DIM ** -0.5
    at = []
    for j in range(depth - n_a):
        wq = attn_wq[j] * scale
        wo_r = attn_wo[j].reshape(KV_HEADS, Q_PER_KV, HEAD_DIM, D_MODEL).transpose(1, 0, 2, 3)
        wo_r = wo_r.reshape(Q_PER_KV, KV_HEADS * HEAD_DIM, D_MODEL).astype(BF16)
        wq_seg = wq.reshape(D_MODEL, N_GROUPS, N_HEADS, HEAD_DIM)[:, :, _HEAD_OF_SEG].reshape(D_MODEL, -1)
        wo_seg = attn_wo[j].reshape(N_HEADS, HEAD_DIM, D_MODEL)[_HEAD_OF_SEG].reshape(D_MODEL, D_MODEL)
        at.append(dict(wq_seg=wq_seg.astype(BF16), wo_seg=wo_seg.astype(BF16), wo_r=wo_r))
    return dict(layers=layers, rwkv=rw, attn=at, kv_norm=row(kv_norm), w_kv=w_kv.astype(BF16),
                final_norm=row(final_norm), rel_bias=rel_bias, et=et)


def _trunk(x, p, wkv0, shift0, cache, w):
    nb, seq_len, _ = x.shape
    n = nb * seq_len
    depth = len(w["layers"])
    n_a = depth // 2
    h = x.reshape(n, D_MODEL).astype(F32)
    wkv_out, shift_out = [], []
    kv_rows = kv_rm = None
    kvw = 2 * KV_HEADS * HEAD_DIM
    for i in range(depth):
        lw = w["layers"][i]
        if i == n_a:
            if cache is None:
                kv_rows, *kv_rm = _proj_rm(h, w["kv_norm"], w["w_kv"], nb, seq_len, True,
                                           [(0, kvw)] * N_GROUPS, "kv_proj")
            else:
                kv_rows = _norm_mm(h, w["kv_norm"], w["w_kv"], F32, name="kv_proj")
        h = _ffn(h, lw["nw"][0], *lw["ffn1"], layer=i)
        if i < n_a:
            rwl = w["rwkv"][i]
            h, sh, st = _rwkv_layer(h, shift0[i].astype(F32), wkv0[i].astype(F32), seq_len, rwl)
            wkv_out.append(st.astype(x.dtype))
            shift_out.append(sh.astype(x.dtype))
        else:
            al = w["attn"][i - n_a]
            if cache is None:
                q_rm = _proj_rm(h, lw["nw"][1], al["wq_seg"], nb, seq_len, False,
                                [(gi * D_MODEL, (gi + 1) * D_MODEL) for gi in range(N_GROUPS)], "q_proj")
                outs, lses = [], []
                for gi in range(N_GROUPS):
                    o, lse = _attn_group(q_rm[gi], kv_rm[gi], _band_bias(w["rel_bias"], gi))
                    outs.append(o)
                    lses.append(lse)
                h = _attn_out(outs, lses, h, al["wo_seg"], w["et"], seq_len)
            else:
                q = _norm_mm(h, lw["nw"][1], al["wq_seg"], F32, name="q_proj_decode")
                q = q.reshape(n, N_GROUPS, N_HEADS, HEAD_DIM)[:, :, np.argsort(_HEAD_OF_SEG)]
                q = q.reshape(n, N_GROUPS, KV_HEADS, Q_PER_KV, 1, HEAD_DIM)
                q = q * jnp.eye(KV_HEADS, dtype=F32).reshape(1, 1, KV_HEADS, 1, KV_HEADS, 1)
                q = q.reshape(n, N_GROUPS * N_HEADS * KV_HEADS * HEAD_DIM)
                bias_c, bias_n = _decode_bias(w["rel_bias"], seq_len, cache.shape[1])
                h = _attn_decode(q, cache, kv_rows, h, bias_c, bias_n, al["wo_r"], nb, seq_len)
        pe = (p.reshape(depth, n, PLE_DIM).astype(F32), lw["nw"][3], lw["pe_gate"], lw["pe_proj"], w["final_norm"])
        h = _ffn(h, lw["nw"][2], *lw["ffn2"], pe=pe, layer=i, final=(i == depth - 1))
    y = h.reshape(nb, seq_len, D_MODEL).astype(x.dtype)
    kv_rows = kv_rows.reshape(nb, seq_len, 2, KV_HEADS, HEAD_DIM).astype(x.dtype)
    return y, jnp.stack(wkv_out), jnp.stack(shift_out), kv_rows


def kernel(x_prompt, x_sample, state_wkv, state_shift, cache_kv, p_prompt, p_sample, norm_w, ffn1_wi, ffn1_wo, ffn2_wi, ffn2_wo, pe_proj, pe_gate, rwkv_mix, rwkv_wrkv, rwkv_wo, rwkv_w0, rwkv_w1, rwkv_w2, rwkv_a0, rwkv_a1, rwkv_a2, rwkv_g1, rwkv_g2, rwkv_kk, rwkv_ka, rwkv_rk, rwkv_lnx_w, rwkv_lnx_b, attn_wq, attn_wo, kv_norm, w_kv, rel_bias, final_norm):
    w = _prep_weights(norm_w, ffn1_wi, ffn1_wo, ffn2_wi, ffn2_wo, pe_proj, pe_gate,
                      rwkv_mix, rwkv_wrkv, rwkv_wo, rwkv_w0, rwkv_w1, rwkv_w2, rwkv_a0, rwkv_a1, rwkv_a2,
                      rwkv_g1, rwkv_g2, rwkv_kk, rwkv_ka, rwkv_rk, rwkv_lnx_w, rwkv_lnx_b,
                      attn_wq, attn_wo, kv_norm, w_kv, rel_bias, final_norm)
    n_a = norm_w.shape[0] // 2
    nb, seq_len, _ = x_prompt.shape
    wkv0 = jnp.zeros((n_a, nb, N_HEADS, HEAD_DIM, HEAD_DIM), F32)
    shift0 = jnp.zeros((n_a, nb, D_MODEL), x_prompt.dtype)
    y_p, wkv_p, shift_p, kv_p = _trunk(x_prompt, p_prompt, wkv0, shift0, None, w)
    kv_prompt = kv_p[:, seq_len - min(MAX_WINDOW, seq_len):]
    cache = cache_kv.reshape(cache_kv.shape[0], cache_kv.shape[1], 2 * KV_HEADS * HEAD_DIM).astype(F32)
    y_s, wkv_s, shift_s, kv_s = _trunk(x_sample, p_sample, state_wkv, state_shift, cache, w)
    return (y_p, y_s, wkv_p, shift_p, kv_prompt, wkv_s, shift_s, kv_s)
```

```python
import functools

import numpy as np
import jax
import jax.numpy as jnp
from jax import lax
from jax.experimental import pallas as pl
from jax.experimental.pallas import tpu as pltpu

F32 = jnp.float32
BF16 = jnp.bfloat16

D_MODEL = 1024
D_FF = 2816
PLE_DIM = 256
RMS_EPS = 1e-6
HEAD_DIM = 64
N_HEADS = D_MODEL // HEAD_DIM
LNX_EPS = 64e-5
KV_HEADS = 4
Q_PER_KV = N_HEADS // KV_HEADS
DILATION_GROUPS = ((128, 1), (512, 4), (2048, 16))
N_GROUPS = len(DILATION_GROUPS)
MAX_WINDOW = 2048
REL_BUCKETS = 32
REL_MAX_DIST = 2048
NEG_INF = -1e30

LANES = 128
FFN_CHUNK = 256
N_FFN_CHUNKS = D_FF // FFN_CHUNK
TOKEN_TILE = 256
FFN_TILE = 512
PROJ_TILE = 512
WKV_CHUNK = 64
RWKV_SHORT_SEQS = 8
RWKV_PARALLEL_SEQS = 2
Q_BLOCK = 128
ATTN_SPAN = 2048
VMEM_LIMIT = 56 * 1024 * 1024
EXP_MINUS_HALF = 0.6065306597126334


def _params(*sem):
    return pltpu.CompilerParams(dimension_semantics=sem, vmem_limit_bytes=VMEM_LIMIT)


def _const_spec(shape):
    return pl.BlockSpec(shape, lambda *_: (0,) * len(shape))


def _tile(n, pref=TOKEN_TILE):
    t = min(n, pref)
    while n % t:
        t -= 8
    return t


def _rms(x, g):
    return x * lax.rsqrt(jnp.mean(x * x, axis=-1, keepdims=True) + RMS_EPS) * g


def _bdot(a, b):
    return jnp.dot(a.astype(BF16), b, preferred_element_type=F32)


def _head_sum(x, e_ref, et_ref):
    return _bdot(_bdot(x, e_ref[...]), et_ref[...])


def _ffn_kernel(x_ref, g_ref, wi_ref, wo_ref, *rest, with_pe, final):
    o_ref = rest[-1]
    x = x_ref[...]
    xn = _rms(x, g_ref[...]).astype(BF16)
    acc = jnp.zeros_like(x)
    for j in range(N_FFN_CHUNKS):
        lo, hi = j * FFN_CHUNK, (j + 1) * FFN_CHUNK
        gate = jnp.dot(xn, wi_ref[:, lo:hi].astype(BF16), preferred_element_type=F32)
        up = jnp.dot(xn, wi_ref[:, D_FF + lo:D_FF + hi].astype(BF16), preferred_element_type=F32)
        act = (gate * jax.nn.sigmoid(gate) * up).astype(BF16)
        acc = acc + jnp.dot(act, wo_ref[lo:hi, :].astype(BF16), preferred_element_type=F32)
    y = x + 0.5 * acc
    if with_pe:
        p_ref, gp_ref, wgate_ref, wproj_ref, gf_ref = rest[:-1]
        gate = jax.nn.sigmoid(_bdot(_rms(y, gp_ref[...]), wgate_ref[...]))
        y = y + gate * _bdot(p_ref[...], wproj_ref[...])
        if final:
            y = _rms(y, gf_ref[...])
    o_ref[...] = y


def _ffn(h, g, wi, wo, pe=None, layer=0, final=False):
    n = h.shape[0]
    tm = _tile(n, FFN_TILE)
    row_spec = pl.BlockSpec((tm, D_MODEL), lambda i: (i, 0))
    args = [h, g, wi, wo]
    in_specs = [row_spec, _const_spec((1, D_MODEL)),
                pl.BlockSpec((None,) + wi.shape[1:], lambda i: (layer, 0, 0)),
                pl.BlockSpec((None,) + wo.shape[1:], lambda i: (layer, 0, 0))]
    if pe is not None:
        args += list(pe)
        in_specs += ([pl.BlockSpec((None, tm, PLE_DIM), lambda i: (layer, i, 0))]
                     + [_const_spec(a.shape) for a in pe[1:]])
    return pl.pallas_call(
        functools.partial(_ffn_kernel, with_pe=pe is not None, final=final),
        grid=(n // tm,),
        in_specs=in_specs,
        out_specs=row_spec,
        out_shape=jax.ShapeDtypeStruct((n, D_MODEL), F32),
        compiler_params=_params("arbitrary"),
        name="ffn_pe" if pe is not None else "ffn",
    )(*args)


def _norm_mm_kernel(x_ref, g_ref, w_ref, o_ref):
    o_ref[...] = _bdot(_rms(x_ref[...], g_ref[...]), w_ref[...]).astype(o_ref.dtype)


def _norm_mm(h, g, w, out_dtype, col_tile=None, name="norm_mm"):
    n = h.shape[0]
    tm = _tile(n)
    nout = w.shape[1]
    tn = nout if col_tile is None else col_tile
    return pl.pallas_call(
        _norm_mm_kernel,
        grid=(n // tm, nout // tn),
        in_specs=[pl.BlockSpec((tm, D_MODEL), lambda i, j: (i, 0)), _const_spec((1, D_MODEL)),
                  pl.BlockSpec((D_MODEL, tn), lambda i, j: (0, j))],
        out_specs=pl.BlockSpec((tm, tn), lambda i, j: (i, j)),
        out_shape=jax.ShapeDtypeStruct((n, nout), out_dtype),
        compiler_params=_params("arbitrary", "arbitrary"),
        name=name,
    )(h, g, w)


def _rwkv_layer_kernel(h_ref, sh_ref, s0_ref, nw_ref, mix_ref, wrkv_ref, w0_ref, w1_ref, w2_ref, a0_ref, a1_ref,
                       a2_ref, g1_ref, g2_ref, kk_ref, ka_ref, lnw_ref, lnb_ref, rk_ref, wo_ref, e_ref, et_ref,
                       o_ref, hn_o, st_ref,
                       state, prev, r_s, lw_s, k_s, v_s, kk_s, b_s, g_s, y_s, *, rows, n_sub, nseq, npar):
    c = pl.program_id(1)
    C = WKV_CHUNK
    C2 = 2 * C
    hp = N_HEADS // 2
    n_pairs = npar * hp
    seq_rows = rows // nseq
    short = seq_rows % C != 0
    total = npar * rows

    def load_state(s0, base=0):
        for h in range(N_HEADS):
            lo = (h % 2) * HEAD_DIM
            state[base + h // 2] = jnp.zeros((LANES, LANES), F32)
        for h in range(N_HEADS):
            lo = (h % 2) * HEAD_DIM
            state[base + h // 2, lo:lo + HEAD_DIM, lo:lo + HEAD_DIM] = s0[h]

    def store_state(st, base=0):
        for h in range(N_HEADS):
            lo = (h % 2) * HEAD_DIM
            st[h] = state[base + h // 2, lo:lo + HEAD_DIM, lo:lo + HEAD_DIM]

    if nseq == 1:
        @pl.when(c == 0)
        def _():
            for q in range(npar):
                load_state(s0_ref.at[q], q * hp)
                prev[q:q + 1, :] = sh_ref[q]
    if short:
        for ref in (r_s, lw_s, k_s, v_s, kk_s, b_s):
            ref[rows:, :] = jnp.zeros((ref.shape[0] - rows, D_MODEL), F32)

    nw = nw_ref[...]
    h_in = h_ref[...].reshape(total, D_MODEL) if nseq == 1 else h_ref[...]
    hn = _rms(h_in, nw)
    row = lax.broadcasted_iota(jnp.int32, hn.shape, 0)
    rolled = pltpu.roll(hn, 1, 0)
    if nseq == 1:
        x_prev = rolled
        for q in range(npar):
            x_prev = jnp.where(row == q * rows, prev[q:q + 1, :], x_prev)
            last = hn[(q + 1) * rows - 1:(q + 1) * rows]
            prev[q:q + 1, :] = last
            hn_o[q] = last
    else:
        x_prev = jnp.where(lax.rem(row, seq_rows) == 0, sh_ref[...], rolled)
        hn_o[...] = hn
    xx = x_prev - hn
    mix = mix_ref[...]
    xr, xw, xk, xv, xa, xg = (hn + xx * mix[j:j + 1] for j in range(6))
    r = _bdot(xr, wrkv_ref[0])
    k = _bdot(xk, wrkv_ref[1])
    v = _bdot(xv, wrkv_ref[2])
    wl = w0_ref[...] + _bdot(jnp.tanh(_bdot(xw, w1_ref[...])), w2_ref[...])
    lw_s[0:total] = -EXP_MINUS_HALF * jax.nn.sigmoid(wl)
    a = jax.nn.sigmoid(a0_ref[...] + _bdot(_bdot(xa, a1_ref[...]), a2_ref[...]))
    g_s[...] = _bdot(jax.nn.sigmoid(_bdot(xg, g1_ref[...])), g2_ref[...])
    kkv = k * kk_ref[...]
    kk = kkv / jnp.maximum(jnp.sqrt(_head_sum(kkv * kkv, e_ref, et_ref)), 1e-12)
    r_s[0:total] = r
    k_s[0:total] = k * (1.0 + (a - 1.0) * ka_ref[...])
    v_s[0:total] = v
    kk_s[0:total] = kk
    b_s[0:total] = kk * a

    ri = lax.broadcasted_iota(jnp.int32, (C, C), 0)
    ci = lax.broadcasted_iota(jnp.int32, (C, C), 1)
    tri = (ri >= ci).astype(BF16)
    r2 = lax.broadcasted_iota(jnp.int32, (C2, C2), 0)
    c2 = lax.broadcasted_iota(jnp.int32, (C2, C2), 1)
    strict = r2 > c2
    incl = r2 >= c2
    lane = lax.broadcasted_iota(jnp.int32, (C, LANES), 1)
    head0 = lane < HEAD_DIM

    def hat(x):
        x3 = jnp.stack([x[:, p * LANES:(p + 1) * LANES] for p in range(n_pairs)])
        return jnp.concatenate([jnp.where(head0, x3, 0.0), jnp.where(head0, 0.0, x3)], axis=1)

    def mm(a, b):
        return jnp.einsum("pmk,pkn->pmn", a.astype(BF16), b.astype(BF16), preferred_element_type=F32)

    def mm_nt(a, b):
        return jnp.einsum("pmk,pnk->pmn", a.astype(BF16), b.astype(BF16), preferred_element_type=F32)

    chunk_row = lax.broadcasted_iota(jnp.int32, (C, D_MODEL), 0)

    def chunk(ref, s):
        if short:
            x = ref[pl.ds(pl.multiple_of(s * seq_rows, 8), C), :]
            return jnp.where(chunk_row < seq_rows, x, 0.0)
        return jnp.concatenate([ref[pl.ds(q * rows + s * C, C), :] for q in range(npar)], axis=1)

    def sub_chunk(s, carry):
        lw = chunk(lw_s, s)
        p1 = lw.astype(BF16)
        rem = lw - p1.astype(F32)
        p2 = rem.astype(BF16)
        p3 = (rem - p2.astype(F32)).astype(BF16)
        cw = (jnp.dot(tri, p1, preferred_element_type=F32) + jnp.dot(tri, p2, preferred_element_type=F32)
              + jnp.dot(tri, p3, preferred_element_type=F32))
        cw_end = cw[C - 1:C, :]
        e_neg = jnp.exp(-cw)
        e_end = jnp.exp(cw_end - cw)
        kk_c = chunk(kk_s, s)
        bb = chunk(b_s, s)
        k_c = chunk(k_s, s)
        a_all = kk_c * jnp.exp(cw - lw)
        r_all = chunk(r_s, s) * jnp.exp(cw)
        b_all = bb * e_neg
        k_all = k_c * e_neg
        bd_all = bb * e_end
        kd_all = k_c * e_end
        v_all = chunk(v_s, s)
        decay = jnp.exp(cw_end)
        ar_h = jnp.concatenate([hat(a_all), hat(r_all)], axis=1)
        bk_h = jnp.concatenate([hat(b_all), hat(k_all)], axis=1)
        v_h = hat(v_all)
        st = state[...]
        g = mm_nt(ar_h, bk_h)
        low = jnp.where(strict, g[:, :C2, :C2], 0.0)
        ak = jnp.where(strict, g[:, :C2, C2:], 0.0)
        rbk = jnp.concatenate([jnp.where(incl, g[:, C2:, :C2], 0.0), jnp.where(incl, g[:, C2:, C2:], 0.0)], axis=2)
        ss = mm_nt(ar_h, st)
        x = -(ss[:, :C2] + mm(ak, v_h))
        t = mm(low, jnp.concatenate([low, x], axis=2))
        lp = t[:, :, :C2]
        x = x - t[:, :, C2:]
        for _ in range(4):
            t = mm(lp, jnp.concatenate([lp, x], axis=2))
            lp = t[:, :, :C2]
            x = x + t[:, :, C2:]
        x = x + mm(lp, x)
        xv = jnp.concatenate([x, v_h], axis=1)
        y_h = ss[:, C2:] + mm(rbk, xv)
        y = y_h[:, :C] + y_h[:, C:]
        for p in range(n_pairs):
            q, pl_ = divmod(p, hp)
            if short:
                y_s[pl.ds(pl.multiple_of(s * seq_rows, 8), seq_rows), pl_ * LANES:(pl_ + 1) * LANES] = y[p, :seq_rows]
            else:
                y_s[pl.ds(q * rows + s * C, C), pl_ * LANES:(pl_ + 1) * LANES] = y[p]
        bkd_h = jnp.concatenate([hat(bd_all), hat(kd_all)], axis=1)
        xv_t = jnp.stack([xv[p].T for p in range(n_pairs)])
        dec3 = jnp.stack([decay[:, p * LANES:(p + 1) * LANES] for p in range(n_pairs)])
        state[...] = st * dec3 + mm(xv_t, bkd_h)
        return carry

    if nseq == 1:
        for s in range(n_sub):
            sub_chunk(s, 0)
    else:
        def one_sequence(s, carry):
            load_state(s0_ref.at[s])
            sub_chunk(s, carry)
            store_state(st_ref.at[s])
            return carry

        lax.fori_loop(0, nseq, one_sequence, 0)

    y = y_s[0:total]
    inv_n = 1.0 / HEAD_DIM
    mu = _head_sum(y, e_ref, et_ref) * inv_n
    yc = y - mu
    var = _head_sum(yc * yc, e_ref, et_ref) * inv_n
    yn = yc * lax.rsqrt(var + LNX_EPS) * lnw_ref[...] + lnb_ref[...]
    bonus = _head_sum(r_s[0:total] * k_s[0:total] * rk_ref[...], e_ref, et_ref) * v_s[0:total]
    out = h_in + _bdot((yn + bonus) * g_s[...], wo_ref[...])
    o_ref[...] = out.reshape(o_ref.shape)

    if nseq == 1:
        @pl.when(c == pl.num_programs(1) - 1)
        def _():
            for q in range(npar):
                store_state(st_ref.at[q], q * hp)


def _rwkv_layer(h, shift, s0, seq_len, lw):
    n = h.shape[0]
    nb = n // seq_len
    if seq_len % WKV_CHUNK == 0:
        nseq, rows = 1, _tile(seq_len)
        npar = RWKV_PARALLEL_SEQS if nb % RWKV_PARALLEL_SEQS == 0 else 1
        steps, n_sub, total = seq_len // rows, rows // WKV_CHUNK, npar * rows
        buf_rows = total
        assert rows % WKV_CHUNK == 0
        h_in = h.reshape(nb, seq_len, D_MODEL)
        row_spec = pl.BlockSpec((npar, rows, D_MODEL), lambda bi, ci: (bi, ci, 0))
        sh_in = shift.reshape(nb, 1, D_MODEL)
        sh_spec = pl.BlockSpec((npar, 1, D_MODEL), lambda bi, ci: (bi, 0, 0))
        hn_shape, hn_spec = jax.ShapeDtypeStruct((nb, 1, D_MODEL), F32), sh_spec
        nst = npar
    else:
        assert seq_len < WKV_CHUNK and seq_len % 8 == 0
        nseq, npar = _tile(nb, RWKV_SHORT_SEQS), 1
        rows, steps, n_sub = nseq * seq_len, 1, 1
        total = rows
        buf_rows = rows + WKV_CHUNK - seq_len
        h_in = h
        row_spec = pl.BlockSpec((rows, D_MODEL), lambda bi, ci: (bi, 0))
        sh_in = jnp.repeat(shift, seq_len, axis=0)
        sh_spec = row_spec
        hn_shape, hn_spec = jax.ShapeDtypeStruct((n, D_MODEL), F32), sh_spec
        nst = nseq
    st_spec = pl.BlockSpec((nst, N_HEADS, HEAD_DIM, HEAD_DIM), lambda bi, ci: (bi, 0, 0, 0))
    consts = [lw["nw"], lw["mix"], lw["wrkv"], lw["w0"], lw["w1"], lw["w2"], lw["a0"], lw["a1"], lw["a2"],
              lw["g1"], lw["g2"], lw["kk"], lw["ka"], lw["lnw"], lw["lnb"], lw["rk"], lw["wo"], lw["e"], lw["et"]]
    seq_buf = pltpu.VMEM((buf_rows, D_MODEL), F32)
    out, hn, st = pl.pallas_call(
        functools.partial(_rwkv_layer_kernel, rows=rows, n_sub=n_sub, nseq=nseq, npar=npar),
        grid=(nb // nst, steps),
        in_specs=[row_spec, sh_spec, st_spec] + [_const_spec(c.shape) for c in consts],
        out_specs=[row_spec, hn_spec, st_spec],
        out_shape=[jax.ShapeDtypeStruct(h_in.shape, F32), hn_shape, jax.ShapeDtypeStruct(s0.shape, F32)],
        scratch_shapes=[pltpu.VMEM((npar * N_HEADS // 2, LANES, LANES), F32), pltpu.VMEM((8, D_MODEL), F32)]
                       + [seq_buf] * 6 + [pltpu.VMEM((total, D_MODEL), F32), seq_buf],
        compiler_params=_params("arbitrary", "arbitrary"),
        name="rwkv_layer",
    )(h_in, sh_in, s0, *consts)
    out = out.reshape(n, D_MODEL)
    shift_out = hn.reshape(nb, D_MODEL) if nseq == 1 else hn.reshape(nb, seq_len, D_MODEL)[:, -1]
    return out, shift_out, st


def _t5_buckets(dist):
    d = np.asarray(dist, dtype=np.int64)
    max_exact = REL_BUCKETS // 2
    large = max_exact + (np.log(np.maximum(d, 1) / max_exact) / np.log(REL_MAX_DIST / max_exact)
                         * (REL_BUCKETS - max_exact)).astype(np.int32)
    large = np.minimum(large, REL_BUCKETS - 1)
    return np.where(d < max_exact, d, large).astype(np.int32)


def _toeplitz(tab, n_rows, n_cols):
    period = tab.shape[-1]
    assert period >= n_rows + n_cols - 1 and n_cols <= period - 1
    lead = tab.shape[:-1]
    flat = jnp.broadcast_to(tab[..., None, :], lead + (n_rows, period)).reshape(lead + (-1,))
    skew = flat[..., :n_rows * (period - 1)].reshape(lead + (n_rows, period - 1))
    return skew[..., :n_cols]


def _band_bias(rel_bias, group):
    win, dil = DILATION_GROUPS[group]
    assert win // dil == Q_BLOCK
    period = 3 * Q_BLOCK - 1
    idx = np.arange(period)
    m = Q_BLOCK - np.where(idx < 2 * Q_BLOCK, idx, idx - period)
    valid = (m >= 0) & (m <= Q_BLOCK)
    buckets = _t5_buckets(dil * np.clip(m, 0, Q_BLOCK))
    tbl = jnp.take(rel_bias[:, group * N_HEADS:(group + 1) * N_HEADS].astype(F32), buckets, axis=0).T
    general = _toeplitz(jnp.where(valid[None], tbl, NEG_INF), Q_BLOCK, 2 * Q_BLOCK)
    first = jnp.where((np.arange(2 * Q_BLOCK) >= Q_BLOCK)[None, None, :], general, NEG_INF)
    return jnp.stack([first, general])


_HEAD_OF_SEG = np.array([4 * (2 * (s // 8) + s % 2) + (s // 2) % 4 for s in range(N_HEADS)])


def _proj_rm_kernel(x_ref, g_ref, w_ref, *refs, natural, lane_ranges, tm):
    scr = refs[-1]
    outs = refs[:-1]
    y = _bdot(_rms(x_ref[...], g_ref[...]), w_ref[...])
    if natural:
        outs[0][...] = y
        outs = outs[1:]
    for c in range(scr.shape[0]):
        scr[c] = y[:, c * LANES:(c + 1) * LANES]
    for gi, (_, dil) in enumerate(DILATION_GROUPS):
        lo, hi = lane_ranges[gi]
        if dil == 1:
            outs[gi][0] = y[:, lo:hi].astype(BF16)
            continue
        for rho in range(dil):
            rows = [scr[c, pl.ds(rho, tm // dil, stride=dil), :] for c in range(lo // LANES, hi // LANES)]
            outs[gi][rho] = jnp.concatenate(rows, axis=1).astype(BF16)


def _proj_rm(h, g, w, nb, seq_len, natural, lane_ranges, name):
    n = h.shape[0]
    tm = _tile(n, PROJ_TILE)
    nout = w.shape[1]
    assert seq_len % tm == 0 and all(tm % (16 * dil) == 0 for _, dil in DILATION_GROUPS)
    tps = seq_len // tm
    out_shape, out_specs = [], []
    if natural:
        out_shape.append(jax.ShapeDtypeStruct((n, nout), F32))
        out_specs.append(pl.BlockSpec((tm, nout), lambda i: (i, 0)))
    for (_, dil), (lo, hi) in zip(DILATION_GROUPS, lane_ranges):
        out_shape.append(jax.ShapeDtypeStruct((nb, dil, seq_len // dil, hi - lo), BF16))
        out_specs.append(pl.BlockSpec((None, dil, tm // dil, hi - lo), lambda i: (i // tps, 0, i % tps, 0)))
    return pl.pallas_call(
        functools.partial(_proj_rm_kernel, natural=natural, lane_ranges=lane_ranges, tm=tm),
        grid=(n // tm,),
        in_specs=[pl.BlockSpec((tm, D_MODEL), lambda i: (i, 0)), _const_spec((1, D_MODEL)), _const_spec(w.shape)],
        out_specs=out_specs,
        out_shape=out_shape,
        scratch_shapes=[pltpu.VMEM((nout // LANES, tm, LANES), F32)],
        compiler_params=_params("arbitrary"),
        name=name,
    )(h, g, w)


def _attn_kernel(q_ref, kv_ref, halo_ref, bias_ref, o_ref, lse_ref, kvbuf, *, nq):
    i = pl.program_id(0)
    dil = q_ref.shape[0]
    kvd = KV_HEADS * HEAD_DIM
    nt = (((1,), (1,)), ((), ()))
    kvbuf[:, :Q_BLOCK, :] = halo_ref[...]
    kvbuf[:, Q_BLOCK:, :] = kv_ref[...]
    lane = lax.broadcasted_iota(jnp.int32, (Q_BLOCK, LANES), 1)
    low_half = lane < HEAD_DIM
    lane_row = lax.broadcasted_iota(jnp.int32, (1, LANES), 1)
    keep_lo = (lane_row < HEAD_DIM).astype(BF16)
    keep_hi = (lane_row >= HEAD_DIM).astype(BF16)

    def block(u, carry):
        rho = lax.div(u, nq)
        j = u - rho * nq
        r0 = pl.multiple_of(j * Q_BLOCK, Q_BLOCK)
        qb = q_ref[rho, pl.ds(r0, Q_BLOCK), :]
        kvb = kvbuf[rho, pl.ds(r0, 2 * Q_BLOCK), :]
        bsel = jnp.where(jnp.logical_and(i == 0, j == 0), 0, 1)
        lse_tile = jnp.zeros((Q_BLOCK, LANES), F32)
        for G in range(KV_HEADS // 2):
            kg = kvb[:, G * LANES:(G + 1) * LANES]
            vg = kvb[:, kvd + G * LANES:kvd + (G + 1) * LANES]
            vcat = jnp.concatenate([vg * keep_lo, vg * keep_hi], axis=0)
            pieces = []
            for r in range(Q_PER_KV):
                qg = qb[:, (G * Q_PER_KV + r) * LANES:(G * Q_PER_KV + r + 1) * LANES]
                pieces += [qg * keep_lo, qg * keep_hi]
            s_all = lax.dot_general(jnp.concatenate(pieces, axis=0), kg, nt, preferred_element_type=F32)
            for r in range(Q_PER_KV):
                grp = G * Q_PER_KV + r
                parts = []
                for half in range(2):
                    s = s_all[(2 * r + half) * Q_BLOCK:(2 * r + half + 1) * Q_BLOCK]
                    s = s + bias_ref[bsel, int(_HEAD_OF_SEG[2 * grp + half])]
                    m = jnp.max(s, axis=-1, keepdims=True)
                    p = jnp.exp(s - m)
                    l = jnp.sum(p, axis=-1, keepdims=True)
                    parts.append((p.astype(BF16), l, m + jnp.log(l)))
                (p0, l0, e0), (p1, l1, e1) = parts
                o = jnp.dot(jnp.concatenate([p0, p1], axis=1), vcat, preferred_element_type=F32)
                o = (o / jnp.where(low_half, l0, l1)).astype(o_ref.dtype)
                o_ref[rho, pl.ds(r0, Q_BLOCK), grp * LANES:(grp + 1) * LANES] = o
                lse_tile = jnp.where(lane == 2 * grp, e0, lse_tile)
                lse_tile = jnp.where(lane == 2 * grp + 1, e1, lse_tile)
        lse_ref[rho, pl.ds(r0, Q_BLOCK), :] = lse_tile
        return carry

    lax.fori_loop(0, dil * nq, block, 0)


def _attn_group(q_rm, kv_rm, bias):
    nb, dil, tsub, _ = q_rm.shape
    kvw = kv_rm.shape[-1]
    rows = ATTN_SPAN // dil
    nq = rows // Q_BLOCK
    assert tsub % rows == 0 and nq >= 1
    span_spec = lambda width: pl.BlockSpec((None, dil, rows, width), lambda i, b: (b, 0, i, 0))
    return pl.pallas_call(
        functools.partial(_attn_kernel, nq=nq),
        grid=(tsub // rows, nb),
        in_specs=[span_spec(D_MODEL), span_spec(kvw),
                  pl.BlockSpec((None, dil, Q_BLOCK, kvw), lambda i, b: (b, 0, jnp.maximum(i * nq - 1, 0), 0)),
                  _const_spec(bias.shape)],
        out_specs=[span_spec(D_MODEL), span_spec(LANES)],
        out_shape=[jax.ShapeDtypeStruct((nb, dil, tsub, D_MODEL), BF16),
                   jax.ShapeDtypeStruct((nb, dil, tsub, LANES), F32)],
        scratch_shapes=[pltpu.VMEM((dil, Q_BLOCK + rows, kvw), BF16)],
        compiler_params=_params("arbitrary", "arbitrary"),
        name=f"attn_d{dil}",
    )(q_rm, kv_rm, kv_rm, bias)


def _attn_out_kernel(o0_ref, o1_ref, o2_ref, l0_ref, l1_ref, l2_ref, h_ref, wo_ref, et_ref, out_ref, *scr, tm):
    outs, lses = [], []
    for gi, (o_ref, l_ref) in enumerate(((o0_ref, l0_ref), (o1_ref, l1_ref), (o2_ref, l2_ref))):
        dil = DILATION_GROUPS[gi][1]
        if dil == 1:
            outs.append(o_ref[0].astype(F32))
            lses.append(l_ref[0])
            continue
        so, sl = scr[2 * gi], scr[2 * gi + 1]
        n_tiles = so.shape[0]
        for rho in range(dil):
            rows = pl.ds(rho, tm // dil, stride=dil)
            for c in range(n_tiles):
                so[c, rows, :] = o_ref[rho, :, c * LANES:(c + 1) * LANES].astype(F32)
            sl[rows, :] = l_ref[rho]
        outs.append(jnp.concatenate([so[c] for c in range(n_tiles)], axis=1))
        lses.append(sl[...])
    l0, l1, l2 = lses
    m = jnp.maximum(jnp.maximum(l0, l1), l2)
    w0, w1, w2 = jnp.exp(l0 - m), jnp.exp(l1 - m), jnp.exp(l2 - m)
    inv = 1.0 / (w0 + w1 + w2)
    et = et_ref[...]
    att = _bdot(w0 * inv, et) * outs[0] + _bdot(w1 * inv, et) * outs[1] + _bdot(w2 * inv, et) * outs[2]
    out_ref[...] = h_ref[...] + _bdot(att, wo_ref[...])


def _attn_out(outs, lses, h, wo, et, seq_len):
    n = h.shape[0]
    tm = _tile(n, PROJ_TILE)
    tps = seq_len // tm
    row_spec = pl.BlockSpec((tm, D_MODEL), lambda i: (i, 0))

    def rm_spec(dil, width):
        return pl.BlockSpec((None, dil, tm // dil, width), lambda i: (i // tps, 0, i % tps, 0))

    dils = [dil for _, dil in DILATION_GROUPS]
    scratch = []
    for _ in dils:
        scratch += [pltpu.VMEM((D_MODEL // LANES, tm, LANES), F32), pltpu.VMEM((tm, LANES), F32)]
    return pl.pallas_call(
        functools.partial(_attn_out_kernel, tm=tm),
        grid=(n // tm,),
        in_specs=[rm_spec(d, D_MODEL) for d in dils] + [rm_spec(d, LANES) for d in dils]
                 + [row_spec, _const_spec(wo.shape), _const_spec(et.shape)],
        out_specs=row_spec,
        out_shape=jax.ShapeDtypeStruct((n, D_MODEL), F32),
        scratch_shapes=scratch,
        compiler_params=_params("arbitrary"),
        name="attn_out",
    )(*outs, *lses, h, wo, et)


def _decode_bias(rel_bias, seq_len, cache_len):
    ncol = cache_len + LANES
    period = seq_len + ncol - 1
    idx = np.arange(period)
    dist = cache_len - np.where(idx < ncol, idx, idx - period)
    buckets = _t5_buckets(np.clip(dist, 0, MAX_WINDOW))
    tabs = []
    for g, (win, dil) in enumerate(DILATION_GROUPS):
        valid = (dist >= 0) & (dist % dil == 0) & (dist <= win)
        tbl = jnp.take(rel_bias[:, g * N_HEADS:(g + 1) * N_HEADS].astype(F32), buckets, axis=0).T
        tabs.append(jnp.where(valid[None], tbl, NEG_INF))
    rows = _toeplitz(jnp.stack(tabs), seq_len, ncol)
    bias = rows.reshape(N_GROUPS * N_HEADS * seq_len, ncol)
    return bias[:, :cache_len], bias[:, cache_len:]


def _attn_decode_kernel(q_ref, cache_ref, kvn_ref, h_ref, bc_ref, bn_ref, wo_ref, out_ref, *, seq_len):
    kvd = KV_HEADS * HEAD_DIM
    nslot = N_GROUPS * N_HEADS
    rows_g = N_HEADS * seq_len
    nt = (((1,), (1,)), ((), ()))
    cache = cache_ref[...]
    kc = cache[:, :kvd].astype(BF16)
    vc = cache[:, kvd:].astype(BF16)
    kvn = kvn_ref[...]
    pad = jnp.zeros((LANES - seq_len, kvd), F32)
    kn = jnp.concatenate([kvn[:, :kvd], pad], axis=0).astype(BF16)
    vn = jnp.concatenate([kvn[:, kvd:], pad], axis=0).astype(BF16)
    lhs = jnp.concatenate([q_ref[:, s * kvd:(s + 1) * kvd] for s in range(nslot)], axis=0).astype(BF16)
    cache_len = kc.shape[0]
    sn = lax.dot_general(lhs, kn, nt, preferred_element_type=F32) + bn_ref[...]
    scs, m = [], None
    for g, (win, _) in enumerate(DILATION_GROUPS):
        c0 = (cache_len - min(win, cache_len)) // LANES * LANES
        rows = slice(g * rows_g, (g + 1) * rows_g)
        sc = lax.dot_general(lhs[rows], kc[c0:], nt, preferred_element_type=F32) + bc_ref[rows, c0:]
        scs.append((sc, c0, rows))
        m_g = jnp.maximum(jnp.max(sc, axis=-1, keepdims=True), jnp.max(sn[rows], axis=-1, keepdims=True))
        m = m_g if m is None else jnp.maximum(m, m_g)
    l = jnp.zeros((rows_g, 1), F32)
    num = jnp.zeros((rows_g, kvd), F32)
    for sc, c0, rows in scs:
        pc = jnp.exp(sc - m)
        pn = jnp.exp(sn[rows] - m)
        l = l + jnp.sum(pc, axis=-1, keepdims=True) + jnp.sum(pn, axis=-1, keepdims=True)
        num = (num + jnp.dot(pc.astype(BF16), vc[c0:], preferred_element_type=F32)
               + jnp.dot(pn.astype(BF16), vn, preferred_element_type=F32))
    row = lax.broadcasted_iota(jnp.int32, (rows_g, kvd), 0)
    lane = lax.broadcasted_iota(jnp.int32, (rows_g, kvd), 1)
    own = (row // (Q_PER_KV * seq_len)) == (lane // HEAD_DIM)
    att = jnp.where(own, num / l, 0.0)
    out = h_ref[...]
    for r in range(Q_PER_KV):
        a_r = att[r * seq_len:(r + 1) * seq_len]
        for c in range(1, KV_HEADS):
            a_r = a_r + att[(c * Q_PER_KV + r) * seq_len:(c * Q_PER_KV + r + 1) * seq_len]
        out = out + _bdot(a_r, wo_ref[r])
    out_ref[...] = out


def _attn_decode(q, cache, kv_new, h, bias_c, bias_n, wo_r, nb, seq_len):
    cache_len = cache.shape[1]
    qw = q.shape[1]
    kvw = 2 * KV_HEADS * HEAD_DIM
    return pl.pallas_call(
        functools.partial(_attn_decode_kernel, seq_len=seq_len),
        grid=(nb,),
        in_specs=[pl.BlockSpec((seq_len, qw), lambda b: (b, 0)),
                  pl.BlockSpec((None, cache_len, kvw), lambda b: (b, 0, 0)),
                  pl.BlockSpec((seq_len, kvw), lambda b: (b, 0)),
                  pl.BlockSpec((seq_len, D_MODEL), lambda b: (b, 0)),
                  _const_spec(bias_c.shape), _const_spec(bias_n.shape), _const_spec(wo_r.shape)],
        out_specs=pl.BlockSpec((seq_len, D_MODEL), lambda b: (b, 0)),
        out_shape=jax.ShapeDtypeStruct((nb * seq_len, D_MODEL), F32),
        compiler_params=_params("arbitrary"),
        name="attn_decode",
    )(q, cache, kv_new, h, bias_c, bias_n, wo_r)


def _prep_weights(norm_w, ffn1_wi, ffn1_wo, ffn2_wi, ffn2_wo, pe_proj, pe_gate,
                  rwkv_mix, rwkv_wrkv, rwkv_wo, rwkv_w0, rwkv_w1, rwkv_w2, rwkv_a0, rwkv_a1, rwkv_a2,
                  rwkv_g1, rwkv_g2, rwkv_kk, rwkv_ka, rwkv_rk, rwkv_lnx_w, rwkv_lnx_b,
                  attn_wq, attn_wo, kv_norm, w_kv, rel_bias, final_norm):
    def row(v):
        return v.reshape(1, -1).astype(F32)

    def pad_cols(w, n):
        return jnp.pad(w, ((0, 0), (0, n - w.shape[1]))).astype(BF16)

    def pad_rows(w, n):
        return jnp.pad(w, ((0, n - w.shape[0]), (0, 0))).astype(BF16)

    head_of_lane = np.arange(D_MODEL) // HEAD_DIM
    e = jnp.asarray(head_of_lane[:, None] == np.arange(LANES)[None, :], BF16)
    et = jnp.asarray(np.arange(LANES)[:, None] == head_of_lane[None, :], BF16)

    depth = norm_w.shape[0]
    layers = []
    for i in range(depth):
        layers.append(dict(
            nw=[row(norm_w[i, j]) for j in range(4)],
            ffn1=(ffn1_wi.astype(F32), ffn1_wo.astype(F32)), ffn2=(ffn2_wi.astype(F32), ffn2_wo.astype(F32)),
            pe_gate=pe_gate[i].astype(BF16), pe_proj=pe_proj[i].astype(BF16)))
    n_a = depth // 2
    rw = []
    for i in range(n_a):
        rw.append(dict(
            nw=row(norm_w[i, 1]), mix=rwkv_mix[i].astype(F32), wrkv=rwkv_wrkv[i].astype(BF16),
            w0=row(rwkv_w0[i]), w1=pad_cols(rwkv_w1[i], LANES), w2=pad_rows(rwkv_w2[i], LANES),
            a0=row(rwkv_a0[i]), a1=pad_cols(rwkv_a1[i], LANES), a2=pad_rows(rwkv_a2[i], LANES),
            g1=pad_cols(rwkv_g1[i], 2 * LANES), g2=pad_rows(rwkv_g2[i], 2 * LANES),
            kk=row(rwkv_kk[i]), ka=row(rwkv_ka[i]), rk=row(rwkv_rk[i]),
            lnw=row(rwkv_lnx_w[i]), lnb=row(rwkv_lnx_b[i]), wo=rwkv_wo[i].astype(BF16), e=e, et=et))
    scale = HEAD_DIM ** -0.5
    at = []
    for j in range(depth - n_a):
        wq = attn_wq[j] * scale
        wo_r = attn_wo[j].reshape(KV_HEADS, Q_PER_KV, HEAD_DIM, D_MODEL).transpose(1, 0, 2, 3)
        wo_r = wo_r.reshape(Q_PER_KV, KV_HEADS * HEAD_DIM, D_MODEL).astype(BF16)
        wq_seg = wq.reshape(D_MODEL, N_GROUPS, N_HEADS, HEAD_DIM)[:, :, _HEAD_OF_SEG].reshape(D_MODEL, -1)
        wo_seg = attn_wo[j].reshape(N_HEADS, HEAD_DIM, D_MODEL)[_HEAD_OF_SEG].reshape(D_MODEL, D_MODEL)
        at.append(dict(wq_seg=wq_seg.astype(BF16), wo_seg=wo_seg.astype(BF16), wo_r=wo_r))
    return dict(layers=layers, rwkv=rw, attn=at, kv_norm=row(kv_norm), w_kv=w_kv.astype(BF16),
                final_norm=row(final_norm), rel_bias=rel_bias, et=et)


def _trunk(x, p, wkv0, shift0, cache, w):
    nb, seq_len, _ = x.shape
    n = nb * seq_len
    depth = len(w["layers"])
    n_a = depth // 2
    h = x.reshape(n, D_MODEL).astype(F32)
    wkv_out, shift_out = [], []
    kv_rows = kv_rm = None
    kvw = 2 * KV_HEADS * HEAD_DIM
    for i in range(depth):
        lw = w["layers"][i]
        if i == n_a:
            if cache is None:
                kv_rows, *kv_rm = _proj_rm(h, w["kv_norm"], w["w_kv"], nb, seq_len, True,
                                           [(0, kvw)] * N_GROUPS, "kv_proj")
            else:
                kv_rows = _norm_mm(h, w["kv_norm"], w["w_kv"], F32, name="kv_proj")
        h = _ffn(h, lw["nw"][0], *lw["ffn1"], layer=i)
        if i < n_a:
            rwl = w["rwkv"][i]
            h, sh, st = _rwkv_layer(h, shift0[i].astype(F32), wkv0[i].astype(F32), seq_len, rwl)
            wkv_out.append(st.astype(x.dtype))
            shift_out.append(sh.astype(x.dtype))
        else:
            al = w["attn"][i - n_a]
            if cache is None:
                q_rm = _proj_rm(h, lw["nw"][1], al["wq_seg"], nb, seq_len, False,
                                [(gi * D_MODEL, (gi + 1) * D_MODEL) for gi in range(N_GROUPS)], "q_proj")
                outs, lses = [], []
                for gi in range(N_GROUPS):
                    o, lse = _attn_group(q_rm[gi], kv_rm[gi], _band_bias(w["rel_bias"], gi))
                    outs.append(o)
                    lses.append(lse)
                h = _attn_out(outs, lses, h, al["wo_seg"], w["et"], seq_len)
            else:
                q = _norm_mm(h, lw["nw"][1], al["wq_seg"], F32, name="q_proj_decode")
                q = q.reshape(n, N_GROUPS, N_HEADS, HEAD_DIM)[:, :, np.argsort(_HEAD_OF_SEG)]
                q = q.reshape(n, N_GROUPS, KV_HEADS, Q_PER_KV, 1, HEAD_DIM)
                q = q * jnp.eye(KV_HEADS, dtype=F32).reshape(1, 1, KV_HEADS, 1, KV_HEADS, 1)
                q = q.reshape(n, N_GROUPS * N_HEADS * KV_HEADS * HEAD_DIM)
                bias_c, bias_n = _decode_bias(w["rel_bias"], seq_len, cache.shape[1])
                h = _attn_decode(q, cache, kv_rows, h, bias_c, bias_n, al["wo_r"], nb, seq_len)
        pe = (p.reshape(depth, n, PLE_DIM).astype(F32), lw["nw"][3], lw["pe_gate"], lw["pe_proj"], w["final_norm"])
        h = _ffn(h, lw["nw"][2], *lw["ffn2"], pe=pe, layer=i, final=(i == depth - 1))
    y = h.reshape(nb, seq_len, D_MODEL).astype(x.dtype)
    kv_rows = kv_rows.reshape(nb, seq_len, 2, KV_HEADS, HEAD_DIM).astype(x.dtype)
    return y, jnp.stack(wkv_out), jnp.stack(shift_out), kv_rows


def kernel(x_prompt, x_sample, state_wkv, state_shift, cache_kv, p_prompt, p_sample, norm_w, ffn1_wi, ffn1_wo, ffn2_wi, ffn2_wo, pe_proj, pe_gate, rwkv_mix, rwkv_wrkv, rwkv_wo, rwkv_w0, rwkv_w1, rwkv_w2, rwkv_a0, rwkv_a1, rwkv_a2, rwkv_g1, rwkv_g2, rwkv_kk, rwkv_ka, rwkv_rk, rwkv_lnx_w, rwkv_lnx_b, attn_wq, attn_wo, kv_norm, w_kv, rel_bias, final_norm):
    w = _prep_weights(norm_w, ffn1_wi, ffn1_wo, ffn2_wi, ffn2_wo, pe_proj, pe_gate,
                      rwkv_mix, rwkv_wrkv, rwkv_wo, rwkv_w0, rwkv_w1, rwkv_w2, rwkv_a0, rwkv_a1, rwkv_a2,
                      rwkv_g1, rwkv_g2, rwkv_kk, rwkv_ka, rwkv_rk, rwkv_lnx_w, rwkv_lnx_b,
                      attn_wq, attn_wo, kv_norm, w_kv, rel_bias, final_norm)
    n_a = norm_w.shape[0] // 2
    nb, seq_len, _ = x_prompt.shape
    wkv0 = jnp.zeros((n_a, nb, N_HEADS, HEAD_DIM, HEAD_DIM), F32)
    shift0 = jnp.zeros((n_a, nb, D_MODEL), x_prompt.dtype)
    y_p, wkv_p, shift_p, kv_p = _trunk(x_prompt, p_prompt, wkv0, shift0, None, w)
    kv_prompt = kv_p[:, seq_len - min(MAX_WINDOW, seq_len):]
    cache = cache_kv.reshape(cache_kv.shape[0], cache_kv.shape[1], 2 * KV_HEADS * HEAD_DIM).astype(F32)
    y_s, wkv_s, shift_s, kv_s = _trunk(x_sample, p_sample, state_wkv, state_shift, cache, w)
    return (y_p, y_s, wkv_p, shift_p, kv_prompt, wkv_s, shift_s, kv_s)
```

```python
import functools

import numpy as np
import jax
import jax.numpy as jnp
from jax import lax
from jax.experimental import pallas as pl
from jax.experimental.pallas import tpu as pltpu

F32 = jnp.float32
BF16 = jnp.bfloat16

D_MODEL = 1024
D_FF = 2816
PLE_DIM = 256
RMS_EPS = 1e-6
HEAD_DIM = 64
N_HEADS = D_MODEL // HEAD_DIM
LNX_EPS = 64e-5
KV_HEADS = 4
Q_PER_KV = N_HEADS // KV_HEADS
DILATION_GROUPS = ((128, 1), (512, 4), (2048, 16))
N_GROUPS = len(DILATION_GROUPS)
MAX_WINDOW = 2048
REL_BUCKETS = 32
REL_MAX_DIST = 2048
NEG_INF = -1e30

LANES = 128
FFN_CHUNK = 256
N_FFN_CHUNKS = D_FF // FFN_CHUNK
TOKEN_TILE = 256
FFN_TILE = 512
PROJ_TILE = 512
WKV_CHUNK = 64
RWKV_SHORT_SEQS = 8
RWKV_PARALLEL_SEQS = 2
RWKV_STEP_ROWS = 512
Q_BLOCK = 128
ATTN_SPAN = 2048
VMEM_LIMIT = 56 * 1024 * 1024
EXP_MINUS_HALF = 0.6065306597126334


def _params(*sem):
    return pltpu.CompilerParams(dimension_semantics=sem, vmem_limit_bytes=VMEM_LIMIT)


def _const_spec(shape):
    return pl.BlockSpec(shape, lambda *_: (0,) * len(shape))


def _tile(n, pref=TOKEN_TILE):
    t = min(n, pref)
    while n % t:
        t -= 8
    return t


def _rms(x, g):
    return x * lax.rsqrt(jnp.mean(x * x, axis=-1, keepdims=True) + RMS_EPS) * g


def _bdot(a, b):
    return jnp.dot(a.astype(BF16), b, preferred_element_type=F32)


def _head_sum(x, e_ref, et_ref):
    return _bdot(_bdot(x, e_ref[...]), et_ref[...])


def _ffn_kernel(*refs, bounds, with_pe, final):
    ns = len(bounds)
    xs, (g_ref, wi_ref, wo_ref), rest = refs[:ns], refs[ns:ns + 3], refs[ns + 3:]
    if with_pe:
        ps, (gp_ref, wgate_ref, wproj_ref, gf_ref), outs = rest[:ns], rest[ns:ns + 4], rest[ns + 4:]
    else:
        outs = rest

    def body(k):
        x = xs[k][...]
        xn = _rms(x, g_ref[...]).astype(BF16)
        acc = jnp.zeros_like(x)
        for j in range(N_FFN_CHUNKS):
            lo, hi = j * FFN_CHUNK, (j + 1) * FFN_CHUNK
            gate = jnp.dot(xn, wi_ref[:, lo:hi].astype(BF16), preferred_element_type=F32)
            up = jnp.dot(xn, wi_ref[:, D_FF + lo:D_FF + hi].astype(BF16), preferred_element_type=F32)
            act = (gate * jax.nn.sigmoid(gate) * up).astype(BF16)
            acc = acc + jnp.dot(act, wo_ref[lo:hi, :].astype(BF16), preferred_element_type=F32)
        y = x + 0.5 * acc
        if with_pe:
            gate = jax.nn.sigmoid(_bdot(_rms(y, gp_ref[...]), wgate_ref[...]))
            y = y + gate * _bdot(ps[k][...], wproj_ref[...])
            if final:
                y = _rms(y, gf_ref[...])
        outs[k][...] = y

    if ns == 1:
        body(0)
    else:
        i = pl.program_id(0)
        for k, (lo, hi) in enumerate(bounds):
            pl.when(jnp.logical_and(i >= lo, i < hi))(functools.partial(body, k))


def _ffn(hs, g, wi, wo, pe=None, layer=0, final=False):
    tiles = [_tile(h.shape[0], FFN_TILE) for h in hs]
    counts = [h.shape[0] // t for h, t in zip(hs, tiles)]
    bounds = [(sum(counts[:k]), sum(counts[:k + 1])) for k in range(len(hs))]

    def rows(k, width, lead=()):
        lo, cnt = bounds[k][0], counts[k]
        return pl.BlockSpec(tuple(None for _ in lead) + (tiles[k], width),
                            lambda i: lead + (jnp.clip(i - lo, 0, cnt - 1), 0))

    args = list(hs) + [g, wi, wo]
    in_specs = ([rows(k, D_MODEL) for k in range(len(hs))]
                + [_const_spec((1, D_MODEL)),
                   pl.BlockSpec((None,) + wi.shape[1:], lambda i: (layer, 0, 0)),
                   pl.BlockSpec((None,) + wo.shape[1:], lambda i: (layer, 0, 0))])
    if pe is not None:
        args += list(pe[0]) + list(pe[1:])
        in_specs += ([rows(k, PLE_DIM, (layer,)) for k in range(len(hs))]
                     + [_const_spec(a.shape) for a in pe[1:]])
    return pl.pallas_call(
        functools.partial(_ffn_kernel, bounds=tuple(bounds), with_pe=pe is not None, final=final),
        grid=(bounds[-1][1],),
        in_specs=in_specs,
        out_specs=[rows(k, D_MODEL) for k in range(len(hs))],
        out_shape=[jax.ShapeDtypeStruct((h.shape[0], D_MODEL), F32) for h in hs],
        compiler_params=_params("arbitrary"),
        name="ffn_pe" if pe is not None else "ffn",
    )(*args)


def _norm_mm_kernel(x_ref, g_ref, w_ref, o_ref):
    o_ref[...] = _bdot(_rms(x_ref[...], g_ref[...]), w_ref[...]).astype(o_ref.dtype)


def _norm_mm(h, g, w, out_dtype, col_tile=None, name="norm_mm"):
    n = h.shape[0]
    tm = _tile(n)
    nout = w.shape[1]
    tn = nout if col_tile is None else col_tile
    return pl.pallas_call(
        _norm_mm_kernel,
        grid=(n // tm, nout // tn),
        in_specs=[pl.BlockSpec((tm, D_MODEL), lambda i, j: (i, 0)), _const_spec((1, D_MODEL)),
                  pl.BlockSpec((D_MODEL, tn), lambda i, j: (0, j))],
        out_specs=pl.BlockSpec((tm, tn), lambda i, j: (i, j)),
        out_shape=jax.ShapeDtypeStruct((n, nout), out_dtype),
        compiler_params=_params("arbitrary", "arbitrary"),
        name=name,
    )(h, g, w)


def _rwkv_layer_kernel(h_ref, sh_ref, s0_ref, nw_ref, mix_ref, wrkv_ref, w0_ref, w1_ref, w2_ref, a0_ref, a1_ref,
                       a2_ref, g1_ref, g2_ref, kk_ref, ka_ref, lnw_ref, lnb_ref, rk_ref, wo_ref, e_ref, et_ref,
                       o_ref, hn_o, st_ref,
                       state, prev, r_s, lw_s, k_s, v_s, kk_s, b_s, g_s, y_s, *, rows, n_sub, nseq, npar):
    c = pl.program_id(1)
    C = WKV_CHUNK
    C2 = 2 * C
    hp = N_HEADS // 2
    n_pairs = npar * hp
    seq_rows = rows // nseq
    short = seq_rows % C != 0
    total = npar * rows

    def load_state(s0, base=0):
        for h in range(N_HEADS):
            lo = (h % 2) * HEAD_DIM
            state[base + h // 2] = jnp.zeros((LANES, LANES), F32)
        for h in range(N_HEADS):
            lo = (h % 2) * HEAD_DIM
            state[base + h // 2, lo:lo + HEAD_DIM, lo:lo + HEAD_DIM] = s0[h]

    def store_state(st, base=0):
        for h in range(N_HEADS):
            lo = (h % 2) * HEAD_DIM
            st[h] = state[base + h // 2, lo:lo + HEAD_DIM, lo:lo + HEAD_DIM]

    if nseq == 1:
        @pl.when(c == 0)
        def _():
            for q in range(npar):
                load_state(s0_ref.at[q], q * hp)
                prev[q:q + 1, :] = sh_ref[q]
    if short:
        for ref in (r_s, lw_s, k_s, v_s, kk_s, b_s):
            ref[rows:, :] = jnp.zeros((ref.shape[0] - rows, D_MODEL), F32)

    nw = nw_ref[...]
    h_in = h_ref[...].reshape(total, D_MODEL) if nseq == 1 else h_ref[...]
    hn = _rms(h_in, nw)
    row = lax.broadcasted_iota(jnp.int32, hn.shape, 0)
    rolled = pltpu.roll(hn, 1, 0)
    if nseq == 1:
        x_prev = rolled
        for q in range(npar):
            x_prev = jnp.where(row == q * rows, prev[q:q + 1, :], x_prev)
            last = hn[(q + 1) * rows - 1:(q + 1) * rows]
            prev[q:q + 1, :] = last
            hn_o[q] = last
    else:
        x_prev = jnp.where(lax.rem(row, seq_rows) == 0, sh_ref[...], rolled)
        hn_o[...] = hn
    xx = x_prev - hn
    mix = mix_ref[...]
    xr, xw, xk, xv, xa, xg = (hn + xx * mix[j:j + 1] for j in range(6))
    r = _bdot(xr, wrkv_ref[0])
    k = _bdot(xk, wrkv_ref[1])
    v = _bdot(xv, wrkv_ref[2])
    wl = w0_ref[...] + _bdot(jnp.tanh(_bdot(xw, w1_ref[...])), w2_ref[...])
    lw_s[0:total] = -EXP_MINUS_HALF * jax.nn.sigmoid(wl)
    a = jax.nn.sigmoid(a0_ref[...] + _bdot(_bdot(xa, a1_ref[...]), a2_ref[...]))
    g_s[...] = _bdot(jax.nn.sigmoid(_bdot(xg, g1_ref[...])), g2_ref[...])
    kkv = k * kk_ref[...]
    kk = kkv / jnp.maximum(jnp.sqrt(_head_sum(kkv * kkv, e_ref, et_ref)), 1e-12)
    r_s[0:total] = r
    k_s[0:total] = k * (1.0 + (a - 1.0) * ka_ref[...])
    v_s[0:total] = v
    kk_s[0:total] = kk
    b_s[0:total] = kk * a

    ri = lax.broadcasted_iota(jnp.int32, (C, C), 0)
    ci = lax.broadcasted_iota(jnp.int32, (C, C), 1)
    tri = (ri >= ci).astype(BF16)
    r2 = lax.broadcasted_iota(jnp.int32, (C2, C2), 0)
    c2 = lax.broadcasted_iota(jnp.int32, (C2, C2), 1)
    strict = r2 > c2
    incl = r2 >= c2
    lane = lax.broadcasted_iota(jnp.int32, (C, LANES), 1)
    head0 = lane < HEAD_DIM

    def hat(x):
        x3 = jnp.stack([x[:, p * LANES:(p + 1) * LANES] for p in range(n_pairs)])
        return jnp.concatenate([jnp.where(head0, x3, 0.0), jnp.where(head0, 0.0, x3)], axis=1)

    def mm(a, b):
        return jnp.einsum("pmk,pkn->pmn", a.astype(BF16), b.astype(BF16), preferred_element_type=F32)

    def mm_nt(a, b):
        return jnp.einsum("pmk,pnk->pmn", a.astype(BF16), b.astype(BF16), preferred_element_type=F32)

    chunk_row = lax.broadcasted_iota(jnp.int32, (C, D_MODEL), 0)

    def chunk(ref, s):
        if short:
            x = ref[pl.ds(pl.multiple_of(s * seq_rows, 8), C), :]
            return jnp.where(chunk_row < seq_rows, x, 0.0)
        return jnp.concatenate([ref[pl.ds(q * rows + s * C, C), :] for q in range(npar)], axis=1)

    def sub_chunk(s, carry):
        lw = chunk(lw_s, s)
        p1 = lw.astype(BF16)
        rem = lw - p1.astype(F32)
        p2 = rem.astype(BF16)
        p3 = (rem - p2.astype(F32)).astype(BF16)
        cw = (jnp.dot(tri, p1, preferred_element_type=F32) + jnp.dot(tri, p2, preferred_element_type=F32)
              + jnp.dot(tri, p3, preferred_element_type=F32))
        cw_end = cw[C - 1:C, :]
        e_neg = jnp.exp(-cw)
        e_end = jnp.exp(cw_end - cw)
        kk_c = chunk(kk_s, s)
        bb = chunk(b_s, s)
        k_c = chunk(k_s, s)
        a_all = kk_c * jnp.exp(cw - lw)
        r_all = chunk(r_s, s) * jnp.exp(cw)
        b_all = bb * e_neg
        k_all = k_c * e_neg
        bd_all = bb * e_end
        kd_all = k_c * e_end
        v_all = chunk(v_s, s)
        decay = jnp.exp(cw_end)
        ar_h = jnp.concatenate([hat(a_all), hat(r_all)], axis=1)
        bk_h = jnp.concatenate([hat(b_all), hat(k_all)], axis=1)
        v_h = hat(v_all)
        st = state[...]
        g = mm_nt(ar_h, bk_h)
        low = jnp.where(strict, g[:, :C2, :C2], 0.0)
        ak = jnp.where(strict, g[:, :C2, C2:], 0.0)
        rbk = jnp.concatenate([jnp.where(incl, g[:, C2:, :C2], 0.0), jnp.where(incl, g[:, C2:, C2:], 0.0)], axis=2)
        ss = mm_nt(ar_h, st)
        x = -(ss[:, :C2] + mm(ak, v_h))
        t = mm(low, jnp.concatenate([low, x], axis=2))
        lp = t[:, :, :C2]
        x = x - t[:, :, C2:]
        for _ in range(4):
            t = mm(lp, jnp.concatenate([lp, x], axis=2))
            lp = t[:, :, :C2]
            x = x + t[:, :, C2:]
        x = x + mm(lp, x)
        xv = jnp.concatenate([x, v_h], axis=1)
        y_h = ss[:, C2:] + mm(rbk, xv)
        y = y_h[:, :C] + y_h[:, C:]
        for p in range(n_pairs):
            q, pl_ = divmod(p, hp)
            if short:
                y_s[pl.ds(pl.multiple_of(s * seq_rows, 8), seq_rows), pl_ * LANES:(pl_ + 1) * LANES] = y[p, :seq_rows]
            else:
                y_s[pl.ds(q * rows + s * C, C), pl_ * LANES:(pl_ + 1) * LANES] = y[p]
        bkd_h = jnp.concatenate([hat(bd_all), hat(kd_all)], axis=1)
        xv_t = jnp.stack([xv[p].T for p in range(n_pairs)])
        dec3 = jnp.stack([decay[:, p * LANES:(p + 1) * LANES] for p in range(n_pairs)])
        state[...] = st * dec3 + mm(xv_t, bkd_h)
        return carry

    if nseq == 1:
        for s in range(n_sub):
            sub_chunk(s, 0)
    else:
        def one_sequence(s, carry):
            load_state(s0_ref.at[s])
            sub_chunk(s, carry)
            store_state(st_ref.at[s])
            return carry

        lax.fori_loop(0, nseq, one_sequence, 0)

    y = y_s[0:total]
    inv_n = 1.0 / HEAD_DIM
    mu = _head_sum(y, e_ref, et_ref) * inv_n
    yc = y - mu
    var = _head_sum(yc * yc, e_ref, et_ref) * inv_n
    yn = yc * lax.rsqrt(var + LNX_EPS) * lnw_ref[...] + lnb_ref[...]
    bonus = _head_sum(r_s[0:total] * k_s[0:total] * rk_ref[...], e_ref, et_ref) * v_s[0:total]
    out = h_in + _bdot((yn + bonus) * g_s[...], wo_ref[...])
    o_ref[...] = out.reshape(o_ref.shape)

    if nseq == 1:
        @pl.when(c == pl.num_programs(1) - 1)
        def _():
            for q in range(npar):
                store_state(st_ref.at[q], q * hp)


def _rwkv_layer(h, shift, s0, seq_len, lw):
    n = h.shape[0]
    nb = n // seq_len
    if seq_len % WKV_CHUNK == 0:
        npar = RWKV_PARALLEL_SEQS if nb % RWKV_PARALLEL_SEQS == 0 else 1
        nseq, rows = 1, _tile(seq_len, RWKV_STEP_ROWS // npar)
        steps, n_sub, total = seq_len // rows, rows // WKV_CHUNK, npar * rows
        buf_rows = total
        assert rows % WKV_CHUNK == 0
        h_in = h.reshape(nb, seq_len, D_MODEL)
        row_spec = pl.BlockSpec((npar, rows, D_MODEL), lambda bi, ci: (bi, ci, 0))
        sh_in = shift.reshape(nb, 1, D_MODEL)
        sh_spec = pl.BlockSpec((npar, 1, D_MODEL), lambda bi, ci: (bi, 0, 0))
        hn_shape, hn_spec = jax.ShapeDtypeStruct((nb, 1, D_MODEL), F32), sh_spec
        nst = npar
    else:
        assert seq_len < WKV_CHUNK and seq_len % 8 == 0
        nseq, npar = _tile(nb, RWKV_SHORT_SEQS), 1
        rows, steps, n_sub = nseq * seq_len, 1, 1
        total = rows
        buf_rows = rows + WKV_CHUNK - seq_len
        h_in = h
        row_spec = pl.BlockSpec((rows, D_MODEL), lambda bi, ci: (bi, 0))
        sh_in = jnp.repeat(shift, seq_len, axis=0)
        sh_spec = row_spec
        hn_shape, hn_spec = jax.ShapeDtypeStruct((n, D_MODEL), F32), sh_spec
        nst = nseq
    st_spec = pl.BlockSpec((nst, N_HEADS, HEAD_DIM, HEAD_DIM), lambda bi, ci: (bi, 0, 0, 0))
    consts = [lw["nw"], lw["mix"], lw["wrkv"], lw["w0"], lw["w1"], lw["w2"], lw["a0"], lw["a1"], lw["a2"],
              lw["g1"], lw["g2"], lw["kk"], lw["ka"], lw["lnw"], lw["lnb"], lw["rk"], lw["wo"], lw["e"], lw["et"]]
    seq_buf = pltpu.VMEM((buf_rows, D_MODEL), F32)
    out, hn, st = pl.pallas_call(
        functools.partial(_rwkv_layer_kernel, rows=rows, n_sub=n_sub, nseq=nseq, npar=npar),
        grid=(nb // nst, steps),
        in_specs=[row_spec, sh_spec, st_spec] + [_const_spec(c.shape) for c in consts],
        out_specs=[row_spec, hn_spec, st_spec],
        out_shape=[jax.ShapeDtypeStruct(h_in.shape, F32), hn_shape, jax.ShapeDtypeStruct(s0.shape, F32)],
        scratch_shapes=[pltpu.VMEM((npar * N_HEADS // 2, LANES, LANES), F32), pltpu.VMEM((8, D_MODEL), F32)]
                       + [seq_buf] * 6 + [pltpu.VMEM((total, D_MODEL), F32), seq_buf],
        compiler_params=_params("arbitrary", "arbitrary"),
        name="rwkv_layer",
    )(h_in, sh_in, s0, *consts)
    out = out.reshape(n, D_MODEL)
    shift_out = hn.reshape(nb, D_MODEL) if nseq == 1 else hn.reshape(nb, seq_len, D_MODEL)[:, -1]
    return out, shift_out, st


def _t5_buckets(dist):
    d = np.asarray(dist, dtype=np.int64)
    max_exact = REL_BUCKETS // 2
    large = max_exact + (np.log(np.maximum(d, 1) / max_exact) / np.log(REL_MAX_DIST / max_exact)
                         * (REL_BUCKETS - max_exact)).astype(np.int32)
    large = np.minimum(large, REL_BUCKETS - 1)
    return np.where(d < max_exact, d, large).astype(np.int32)


def _toeplitz(tab, n_rows, n_cols):
    period = tab.shape[-1]
    assert period >= n_rows + n_cols - 1 and n_cols <= period - 1
    lead = tab.shape[:-1]
    flat = jnp.broadcast_to(tab[..., None, :], lead + (n_rows, period)).reshape(lead + (-1,))
    skew = flat[..., :n_rows * (period - 1)].reshape(lead + (n_rows, period - 1))
    return skew[..., :n_cols]


def _band_bias(rel_bias, group):
    win, dil = DILATION_GROUPS[group]
    assert win // dil == Q_BLOCK
    period = 3 * Q_BLOCK - 1
    idx = np.arange(period)
    m = Q_BLOCK - np.where(idx < 2 * Q_BLOCK, idx, idx - period)
    valid = (m >= 0) & (m <= Q_BLOCK)
    buckets = _t5_buckets(dil * np.clip(m, 0, Q_BLOCK))
    tbl = jnp.take(rel_bias[:, group * N_HEADS:(group + 1) * N_HEADS].astype(F32), buckets, axis=0).T
    general = _toeplitz(jnp.where(valid[None], tbl, NEG_INF), Q_BLOCK, 2 * Q_BLOCK)
    first = jnp.where((np.arange(2 * Q_BLOCK) >= Q_BLOCK)[None, None, :], general, NEG_INF)
    return jnp.stack([first, general])


_HEAD_OF_SEG = np.array([4 * (2 * (s // 8) + s % 2) + (s // 2) % 4 for s in range(N_HEADS)])


def _proj_rm_kernel(x_ref, g_ref, w_ref, *refs, natural, lane_ranges, tm):
    scr = refs[-1]
    outs = refs[:-1]
    y = _bdot(_rms(x_ref[...], g_ref[...]), w_ref[...])
    if natural:
        outs[0][...] = y
        outs = outs[1:]
    for c in range(scr.shape[0]):
        scr[c] = y[:, c * LANES:(c + 1) * LANES]
    for gi, (_, dil) in enumerate(DILATION_GROUPS):
        lo, hi = lane_ranges[gi]
        if dil == 1:
            outs[gi][0] = y[:, lo:hi].astype(BF16)
            continue
        for rho in range(dil):
            rows = [scr[c, pl.ds(rho, tm // dil, stride=dil), :] for c in range(lo // LANES, hi // LANES)]
            outs[gi][rho] = jnp.concatenate(rows, axis=1).astype(BF16)


def _proj_rm(h, g, w, nb, seq_len, natural, lane_ranges, name):
    n = h.shape[0]
    tm = _tile(n, PROJ_TILE)
    nout = w.shape[1]
    assert seq_len % tm == 0 and all(tm % (16 * dil) == 0 for _, dil in DILATION_GROUPS)
    tps = seq_len // tm
    out_shape, out_specs = [], []
    if natural:
        out_shape.append(jax.ShapeDtypeStruct((n, nout), F32))
        out_specs.append(pl.BlockSpec((tm, nout), lambda i: (i, 0)))
    for (_, dil), (lo, hi) in zip(DILATION_GROUPS, lane_ranges):
        out_shape.append(jax.ShapeDtypeStruct((nb, dil, seq_len // dil, hi - lo), BF16))
        out_specs.append(pl.BlockSpec((None, dil, tm // dil, hi - lo), lambda i: (i // tps, 0, i % tps, 0)))
    return pl.pallas_call(
        functools.partial(_proj_rm_kernel, natural=natural, lane_ranges=lane_ranges, tm=tm),
        grid=(n // tm,),
        in_specs=[pl.BlockSpec((tm, D_MODEL), lambda i: (i, 0)), _const_spec((1, D_MODEL)), _const_spec(w.shape)],
        out_specs=out_specs,
        out_shape=out_shape,
        scratch_shapes=[pltpu.VMEM((nout // LANES, tm, LANES), F32)],
        compiler_params=_params("arbitrary"),
        name=name,
    )(h, g, w)


def _attn_kernel(q_ref, kv_ref, halo_ref, bias_ref, o_ref, lse_ref, kvbuf, *, nq):
    i = pl.program_id(0)
    dil = q_ref.shape[0]
    kvd = KV_HEADS * HEAD_DIM
    nt = (((1,), (1,)), ((), ()))
    kvbuf[:, :Q_BLOCK, :] = halo_ref[...]
    kvbuf[:, Q_BLOCK:, :] = kv_ref[...]
    lane = lax.broadcasted_iota(jnp.int32, (Q_BLOCK, LANES), 1)
    low_half = lane < HEAD_DIM
    lane_row = lax.broadcasted_iota(jnp.int32, (1, LANES), 1)
    keep_lo = (lane_row < HEAD_DIM).astype(BF16)
    keep_hi = (lane_row >= HEAD_DIM).astype(BF16)

    def block(u, carry):
        rho = lax.div(u, nq)
        j = u - rho * nq
        r0 = pl.multiple_of(j * Q_BLOCK, Q_BLOCK)
        qb = q_ref[rho, pl.ds(r0, Q_BLOCK), :]
        kvb = kvbuf[rho, pl.ds(r0, 2 * Q_BLOCK), :]
        bsel = jnp.where(jnp.logical_and(i == 0, j == 0), 0, 1)
        lse_tile = jnp.zeros((Q_BLOCK, LANES), F32)
        for G in range(KV_HEADS // 2):
            kg = kvb[:, G * LANES:(G + 1) * LANES]
            vg = kvb[:, kvd + G * LANES:kvd + (G + 1) * LANES]
            vcat = jnp.concatenate([vg * keep_lo, vg * keep_hi], axis=0)
            pieces = []
            for r in range(Q_PER_KV):
                qg = qb[:, (G * Q_PER_KV + r) * LANES:(G * Q_PER_KV + r + 1) * LANES]
                pieces += [qg * keep_lo, qg * keep_hi]
            s_all = lax.dot_general(jnp.concatenate(pieces, axis=0), kg, nt, preferred_element_type=F32)
            for r in range(Q_PER_KV):
                grp = G * Q_PER_KV + r
                parts = []
                for half in range(2):
                    s = s_all[(2 * r + half) * Q_BLOCK:(2 * r + half + 1) * Q_BLOCK]
                    s = s + bias_ref[bsel, int(_HEAD_OF_SEG[2 * grp + half])]
                    m = jnp.max(s, axis=-1, keepdims=True)
                    p = jnp.exp(s - m)
                    l = jnp.sum(p, axis=-1, keepdims=True)
                    parts.append((p.astype(BF16), l, m + jnp.log(l)))
                (p0, l0, e0), (p1, l1, e1) = parts
                o = jnp.dot(jnp.concatenate([p0, p1], axis=1), vcat, preferred_element_type=F32)
                o = (o / jnp.where(low_half, l0, l1)).astype(o_ref.dtype)
                o_ref[rho, pl.ds(r0, Q_BLOCK), grp * LANES:(grp + 1) * LANES] = o
                lse_tile = jnp.where(lane == 2 * grp, e0, lse_tile)
                lse_tile = jnp.where(lane == 2 * grp + 1, e1, lse_tile)
        lse_ref[rho, pl.ds(r0, Q_BLOCK), :] = lse_tile
        return carry

    lax.fori_loop(0, dil * nq, block, 0)


def _attn_group(q_rm, kv_rm, bias):
    nb, dil, tsub, _ = q_rm.shape
    kvw = kv_rm.shape[-1]
    rows = ATTN_SPAN // dil
    nq = rows // Q_BLOCK
    assert tsub % rows == 0 and nq >= 1
    span_spec = lambda width: pl.BlockSpec((None, dil, rows, width), lambda i, b: (b, 0, i, 0))
    return pl.pallas_call(
        functools.partial(_attn_kernel, nq=nq),
        grid=(tsub // rows, nb),
        in_specs=[span_spec(D_MODEL), span_spec(kvw),
                  pl.BlockSpec((None, dil, Q_BLOCK, kvw), lambda i, b: (b, 0, jnp.maximum(i * nq - 1, 0), 0)),
                  _const_spec(bias.shape)],
        out_specs=[span_spec(D_MODEL), span_spec(LANES)],
        out_shape=[jax.ShapeDtypeStruct((nb, dil, tsub, D_MODEL), BF16),
                   jax.ShapeDtypeStruct((nb, dil, tsub, LANES), F32)],
        scratch_shapes=[pltpu.VMEM((dil, Q_BLOCK + rows, kvw), BF16)],
        compiler_params=_params("arbitrary", "arbitrary"),
        name=f"attn_d{dil}",
    )(q_rm, kv_rm, kv_rm, bias)


def _attn_out_kernel(o0_ref, o1_ref, o2_ref, l0_ref, l1_ref, l2_ref, h_ref, wo_ref, et_ref, out_ref, *scr, tm):
    outs, lses = [], []
    for gi, (o_ref, l_ref) in enumerate(((o0_ref, l0_ref), (o1_ref, l1_ref), (o2_ref, l2_ref))):
        dil = DILATION_GROUPS[gi][1]
        if dil == 1:
            outs.append(o_ref[0].astype(F32))
            lses.append(l_ref[0])
            continue
        so, sl = scr[2 * gi], scr[2 * gi + 1]
        n_tiles = so.shape[0]
        for rho in range(dil):
            rows = pl.ds(rho, tm // dil, stride=dil)
            for c in range(n_tiles):
                so[c, rows, :] = o_ref[rho, :, c * LANES:(c + 1) * LANES].astype(F32)
            sl[rows, :] = l_ref[rho]
        outs.append(jnp.concatenate([so[c] for c in range(n_tiles)], axis=1))
        lses.append(sl[...])
    l0, l1, l2 = lses
    m = jnp.maximum(jnp.maximum(l0, l1), l2)
    w0, w1, w2 = jnp.exp(l0 - m), jnp.exp(l1 - m), jnp.exp(l2 - m)
    inv = 1.0 / (w0 + w1 + w2)
    et = et_ref[...]
    att = _bdot(w0 * inv, et) * outs[0] + _bdot(w1 * inv, et) * outs[1] + _bdot(w2 * inv, et) * outs[2]
    out_ref[...] = h_ref[...] + _bdot(att, wo_ref[...])


def _attn_out(outs, lses, h, wo, et, seq_len):
    n = h.shape[0]
    tm = _tile(n, PROJ_TILE)
    tps = seq_len // tm
    row_spec = pl.BlockSpec((tm, D_MODEL), lambda i: (i, 0))

    def rm_spec(dil, width):
        return pl.BlockSpec((None, dil, tm // dil, width), lambda i: (i // tps, 0, i % tps, 0))

    dils = [dil for _, dil in DILATION_GROUPS]
    scratch = []
    for _ in dils:
        scratch += [pltpu.VMEM((D_MODEL // LANES, tm, LANES), F32), pltpu.VMEM((tm, LANES), F32)]
    return pl.pallas_call(
        functools.partial(_attn_out_kernel, tm=tm),
        grid=(n // tm,),
        in_specs=[rm_spec(d, D_MODEL) for d in dils] + [rm_spec(d, LANES) for d in dils]
                 + [row_spec, _const_spec(wo.shape), _const_spec(et.shape)],
        out_specs=row_spec,
        out_shape=jax.ShapeDtypeStruct((n, D_MODEL), F32),
        scratch_shapes=scratch,
        compiler_params=_params("arbitrary"),
        name="attn_out",
    )(*outs, *lses, h, wo, et)


def _decode_bias(rel_bias, seq_len, cache_len):
    ncol = cache_len + LANES
    period = seq_len + ncol - 1
    idx = np.arange(period)
    dist = cache_len - np.where(idx < ncol, idx, idx - period)
    buckets = _t5_buckets(np.clip(dist, 0, MAX_WINDOW))
    tabs = []
    for g, (win, dil) in enumerate(DILATION_GROUPS):
        valid = (dist >= 0) & (dist % dil == 0) & (dist <= win)
        tbl = jnp.take(rel_bias[:, g * N_HEADS:(g + 1) * N_HEADS].astype(F32), buckets, axis=0).T
        tabs.append(jnp.where(valid[None], tbl, NEG_INF))
    rows = _toeplitz(jnp.stack(tabs), seq_len, ncol)
    bias = rows.reshape(N_GROUPS * N_HEADS * seq_len, ncol)
    return bias[:, :cache_len], bias[:, cache_len:]


def _attn_decode_kernel(q_ref, cache_ref, kvn_ref, h_ref, bc_ref, bn_ref, wo_ref, out_ref, *, seq_len):
    kvd = KV_HEADS * HEAD_DIM
    nslot = N_GROUPS * N_HEADS
    rows_g = N_HEADS * seq_len
    nt = (((1,), (1,)), ((), ()))
    cache = cache_ref[...]
    kc = cache[:, :kvd].astype(BF16)
    vc = cache[:, kvd:].astype(BF16)
    kvn = kvn_ref[...]
    pad = jnp.zeros((LANES - seq_len, kvd), F32)
    kn = jnp.concatenate([kvn[:, :kvd], pad], axis=0).astype(BF16)
    vn = jnp.concatenate([kvn[:, kvd:], pad], axis=0).astype(BF16)
    lhs = jnp.concatenate([q_ref[:, s * kvd:(s + 1) * kvd] for s in range(nslot)], axis=0).astype(BF16)
    cache_len = kc.shape[0]
    sn = lax.dot_general(lhs, kn, nt, preferred_element_type=F32) + bn_ref[...]
    scs, m = [], None
    for g, (win, _) in enumerate(DILATION_GROUPS):
        c0 = (cache_len - min(win, cache_len)) // LANES * LANES
        rows = slice(g * rows_g, (g + 1) * rows_g)
        sc = lax.dot_general(lhs[rows], kc[c0:], nt, preferred_element_type=F32) + bc_ref[rows, c0:]
        scs.append((sc, c0, rows))
        m_g = jnp.maximum(jnp.max(sc, axis=-1, keepdims=True), jnp.max(sn[rows], axis=-1, keepdims=True))
        m = m_g if m is None else jnp.maximum(m, m_g)
    l = jnp.zeros((rows_g, 1), F32)
    num = jnp.zeros((rows_g, kvd), F32)
    for sc, c0, rows in scs:
        pc = jnp.exp(sc - m)
        pn = jnp.exp(sn[rows] - m)
        l = l + jnp.sum(pc, axis=-1, keepdims=True) + jnp.sum(pn, axis=-1, keepdims=True)
        num = (num + jnp.dot(pc.astype(BF16), vc[c0:], preferred_element_type=F32)
               + jnp.dot(pn.astype(BF16), vn, preferred_element_type=F32))
    row = lax.broadcasted_iota(jnp.int32, (rows_g, kvd), 0)
    lane = lax.broadcasted_iota(jnp.int32, (rows_g, kvd), 1)
    own = (row // (Q_PER_KV * seq_len)) == (lane // HEAD_DIM)
    att = jnp.where(own, num / l, 0.0)
    out = h_ref[...]
    for r in range(Q_PER_KV):
        a_r = att[r * seq_len:(r + 1) * seq_len]
        for c in range(1, KV_HEADS):
            a_r = a_r + att[(c * Q_PER_KV + r) * seq_len:(c * Q_PER_KV + r + 1) * seq_len]
        out = out + _bdot(a_r, wo_ref[r])
    out_ref[...] = out


def _attn_decode(q, cache, kv_new, h, bias_c, bias_n, wo_r, nb, seq_len):
    cache_len = cache.shape[1]
    qw = q.shape[1]
    kvw = 2 * KV_HEADS * HEAD_DIM
    return pl.pallas_call(
        functools.partial(_attn_decode_kernel, seq_len=seq_len),
        grid=(nb,),
        in_specs=[pl.BlockSpec((seq_len, qw), lambda b: (b, 0)),
                  pl.BlockSpec((None, cache_len, kvw), lambda b: (b, 0, 0)),
                  pl.BlockSpec((seq_len, kvw), lambda b: (b, 0)),
                  pl.BlockSpec((seq_len, D_MODEL), lambda b: (b, 0)),
                  _const_spec(bias_c.shape), _const_spec(bias_n.shape), _const_spec(wo_r.shape)],
        out_specs=pl.BlockSpec((seq_len, D_MODEL), lambda b: (b, 0)),
        out_shape=jax.ShapeDtypeStruct((nb * seq_len, D_MODEL), F32),
        compiler_params=_params("arbitrary"),
        name="attn_decode",
    )(q, cache, kv_new, h, bias_c, bias_n, wo_r)


def _prep_weights(norm_w, ffn1_wi, ffn1_wo, ffn2_wi, ffn2_wo, pe_proj, pe_gate,
                  rwkv_mix, rwkv_wrkv, rwkv_wo, rwkv_w0, rwkv_w1, rwkv_w2, rwkv_a0, rwkv_a1, rwkv_a2,
                  rwkv_g1, rwkv_g2, rwkv_kk, rwkv_ka, rwkv_rk, rwkv_lnx_w, rwkv_lnx_b,
                  attn_wq, attn_wo, kv_norm, w_kv, rel_bias, final_norm):
    def row(v):
        return v.reshape(1, -1).astype(F32)

    def pad_cols(w, n):
        return jnp.pad(w, ((0, 0), (0, n - w.shape[1]))).astype(BF16)

    def pad_rows(w, n):
        return jnp.pad(w, ((0, n - w.shape[0]), (0, 0))).astype(BF16)

    head_of_lane = np.arange(D_MODEL) // HEAD_DIM
    e = jnp.asarray(head_of_lane[:, None] == np.arange(LANES)[None, :], BF16)
    et = jnp.asarray(np.arange(LANES)[:, None] == head_of_lane[None, :], BF16)

    depth = norm_w.shape[0]
    layers = []
    for i in range(depth):
        layers.append(dict(
            nw=[row(norm_w[i, j]) for j in range(4)],
            ffn1=(ffn1_wi.astype(F32), ffn1_wo.astype(F32)), ffn2=(ffn2_wi.astype(F32), ffn2_wo.astype(F32)),
            pe_gate=pe_gate[i].astype(BF16), pe_proj=pe_proj[i].astype(BF16)))
    n_a = depth // 2
    rw = []
    for i in range(n_a):
        rw.append(dict(
            nw=row(norm_w[i, 1]), mix=rwkv_mix[i].astype(F32), wrkv=rwkv_wrkv[i].astype(BF16),
            w0=row(rwkv_w0[i]), w1=pad_cols(rwkv_w1[i], LANES), w2=pad_rows(rwkv_w2[i], LANES),
            a0=row(rwkv_a0[i]), a1=pad_cols(rwkv_a1[i], LANES), a2=pad_rows(rwkv_a2[i], LANES),
            g1=pad_cols(rwkv_g1[i], 2 * LANES), g2=pad_rows(rwkv_g2[i], 2 * LANES),
            kk=row(rwkv_kk[i]), ka=row(rwkv_ka[i]), rk=row(rwkv_rk[i]),
            lnw=row(rwkv_lnx_w[i]), lnb=row(rwkv_lnx_b[i]), wo=rwkv_wo[i].astype(BF16), e=e, et=et))
    scale = HEAD_DIM ** -0.5
    at = []
    for j in range(depth - n_a):
        wq = attn_wq[j] * scale
        wo_r = attn_wo[j].reshape(KV_HEADS, Q_PER_KV, HEAD_DIM, D_MODEL).transpose(1, 0, 2, 3)
        wo_r = wo_r.reshape(Q_PER_KV, KV_HEADS * HEAD_DIM, D_MODEL).astype(BF16)
        wq_seg = wq.reshape(D_MODEL, N_GROUPS, N_HEADS, HEAD_DIM)[:, :, _HEAD_OF_SEG].reshape(D_MODEL, -1)
        wo_seg = attn_wo[j].reshape(N_HEADS, HEAD_DIM, D_MODEL)[_HEAD_OF_SEG].reshape(D_MODEL, D_MODEL)
        at.append(dict(wq_seg=wq_seg.astype(BF16), wo_seg=wo_seg.astype(BF16), wo_r=wo_r))
    return dict(layers=layers, rwkv=rw, attn=at, kv_norm=row(kv_norm), w_kv=w_kv.astype(BF16),
                final_norm=row(final_norm), rel_bias=rel_bias, et=et)


def _trunks(streams, w):
    depth = len(w["layers"])
    n_a = depth // 2
    kvw = 2 * KV_HEADS * HEAD_DIM
    st = []
    for sd in streams:
        nb, seq_len, _ = sd["x"].shape
        n = nb * seq_len
        st.append(dict(nb=nb, seq_len=seq_len, n=n, h=sd["x"].reshape(n, D_MODEL).astype(F32),
                       p=sd["p"].reshape(depth, n, PLE_DIM).astype(F32), wkv_out=[], shift_out=[],
                       kv_rows=None, kv_rm=None, **{k: sd[k] for k in ("wkv0", "shift0", "cache")}))

    def ffn(nw, weights, **kw):
        for t, h in zip(st, _ffn([t["h"] for t in st], nw, *weights, **kw)):
            t["h"] = h

    for i in range(depth):
        lw = w["layers"][i]
        if i == n_a:
            for t in st:
                if t["cache"] is None:
                    t["kv_rows"], *t["kv_rm"] = _proj_rm(t["h"], w["kv_norm"], w["w_kv"], t["nb"], t["seq_len"], True,
                                                         [(0, kvw)] * N_GROUPS, "kv_proj")
                else:
                    t["kv_rows"] = _norm_mm(t["h"], w["kv_norm"], w["w_kv"], F32, name="kv_proj")
        ffn(lw["nw"][0], lw["ffn1"], layer=i)
        for t in st:
            nb, seq_len, n, h = t["nb"], t["seq_len"], t["n"], t["h"]
            if i < n_a:
                h, sh, state = _rwkv_layer(h, t["shift0"][i].astype(F32), t["wkv0"][i].astype(F32), seq_len,
                                           w["rwkv"][i])
                t["wkv_out"].append(state)
                t["shift_out"].append(sh)
            elif t["cache"] is None:
                al = w["attn"][i - n_a]
                q_rm = _proj_rm(h, lw["nw"][1], al["wq_seg"], nb, seq_len, False,
                                [(gi * D_MODEL, (gi + 1) * D_MODEL) for gi in range(N_GROUPS)], "q_proj")
                outs, lses = [], []
                for gi in range(N_GROUPS):
                    o, lse = _attn_group(q_rm[gi], t["kv_rm"][gi], _band_bias(w["rel_bias"], gi))
                    outs.append(o)
                    lses.append(lse)
                h = _attn_out(outs, lses, h, al["wo_seg"], w["et"], seq_len)
            else:
                al = w["attn"][i - n_a]
                q = _norm_mm(h, lw["nw"][1], al["wq_seg"], F32, name="q_proj_decode")
                q = q.reshape(n, N_GROUPS, N_HEADS, HEAD_DIM)[:, :, np.argsort(_HEAD_OF_SEG)]
                q = q.reshape(n, N_GROUPS, KV_HEADS, Q_PER_KV, 1, HEAD_DIM)
                q = q * jnp.eye(KV_HEADS, dtype=F32).reshape(1, 1, KV_HEADS, 1, KV_HEADS, 1)
                q = q.reshape(n, N_GROUPS * N_HEADS * KV_HEADS * HEAD_DIM)
                bias_c, bias_n = _decode_bias(w["rel_bias"], seq_len, t["cache"].shape[1])
                h = _attn_decode(q, t["cache"], t["kv_rows"], h, bias_c, bias_n, al["wo_r"], nb, seq_len)
            t["h"] = h
        pe = ([t["p"] for t in st], lw["nw"][3], lw["pe_gate"], lw["pe_proj"], w["final_norm"])
        ffn(lw["nw"][2], lw["ffn2"], pe=pe, layer=i, final=(i == depth - 1))
    results = []
    for t, sd in zip(st, streams):
        dt = sd["x"].dtype
        y = t["h"].reshape(t["nb"], t["seq_len"], D_MODEL).astype(dt)
        kv_rows = t["kv_rows"].reshape(t["nb"], t["seq_len"], 2, KV_HEADS, HEAD_DIM).astype(dt)
        results.append((y, jnp.stack(t["wkv_out"]).astype(dt), jnp.stack(t["shift_out"]).astype(dt), kv_rows))
    return results


def _trunk(x, p, wkv0, shift0, cache, w):
    return _trunks([dict(x=x, p=p, wkv0=wkv0, shift0=shift0, cache=cache)], w)[0]


def kernel(x_prompt, x_sample, state_wkv, state_shift, cache_kv, p_prompt, p_sample, norm_w, ffn1_wi, ffn1_wo, ffn2_wi, ffn2_wo, pe_proj, pe_gate, rwkv_mix, rwkv_wrkv, rwkv_wo, rwkv_w0, rwkv_w1, rwkv_w2, rwkv_a0, rwkv_a1, rwkv_a2, rwkv_g1, rwkv_g2, rwkv_kk, rwkv_ka, rwkv_rk, rwkv_lnx_w, rwkv_lnx_b, attn_wq, attn_wo, kv_norm, w_kv, rel_bias, final_norm):
    w = _prep_weights(norm_w, ffn1_wi, ffn1_wo, ffn2_wi, ffn2_wo, pe_proj, pe_gate,
                      rwkv_mix, rwkv_wrkv, rwkv_wo, rwkv_w0, rwkv_w1, rwkv_w2, rwkv_a0, rwkv_a1, rwkv_a2,
                      rwkv_g1, rwkv_g2, rwkv_kk, rwkv_ka, rwkv_rk, rwkv_lnx_w, rwkv_lnx_b,
                      attn_wq, attn_wo, kv_norm, w_kv, rel_bias, final_norm)
    n_a = norm_w.shape[0] // 2
    nb, seq_len, _ = x_prompt.shape
    wkv0 = jnp.zeros((n_a, nb, N_HEADS, HEAD_DIM, HEAD_DIM), F32)
    shift0 = jnp.zeros((n_a, nb, D_MODEL), x_prompt.dtype)
    cache = cache_kv.reshape(cache_kv.shape[0], cache_kv.shape[1], 2 * KV_HEADS * HEAD_DIM).astype(F32)
    (y_p, wkv_p, shift_p, kv_p), (y_s, wkv_s, shift_s, kv_s) = _trunks(
        [dict(x=x_prompt, p=p_prompt, wkv0=wkv0, shift0=shift0, cache=None),
         dict(x=x_sample, p=p_sample, wkv0=state_wkv, shift0=state_shift, cache=cache)], w)
    kv_prompt = kv_p[:, seq_len - min(MAX_WINDOW, seq_len):]
    return (y_p, y_s, wkv_p, shift_p, kv_prompt, wkv_s, shift_s, kv_s)
```

```python
import functools

import numpy as np
import jax
import jax.numpy as jnp
from jax import lax
from jax.experimental import pallas as pl
from jax.experimental.pallas import tpu as pltpu

F32 = jnp.float32
BF16 = jnp.bfloat16

D_MODEL = 1024
D_FF = 2816
PLE_DIM = 256
RMS_EPS = 1e-6
HEAD_DIM = 64
N_HEADS = D_MODEL // HEAD_DIM
LNX_EPS = 64e-5
KV_HEADS = 4
Q_PER_KV = N_HEADS // KV_HEADS
DILATION_GROUPS = ((128, 1), (512, 4), (2048, 16))
N_GROUPS = len(DILATION_GROUPS)
MAX_WINDOW = 2048
REL_BUCKETS = 32
REL_MAX_DIST = 2048
NEG_INF = -1e30

LANES = 128
FFN_CHUNK = 256
N_FFN_CHUNKS = D_FF // FFN_CHUNK
TOKEN_TILE = 256
FFN_TILE = 512
PROJ_TILE = 512
WKV_CHUNK = 64
RWKV_SHORT_SEQS = 8
RWKV_PARALLEL_SEQS = 2
RWKV_STEP_ROWS = 512
Q_BLOCK = 128
ATTN_SPAN = 2048
VMEM_LIMIT = 56 * 1024 * 1024
EXP_MINUS_HALF = 0.6065306597126334


def _params(*sem):
    return pltpu.CompilerParams(dimension_semantics=sem, vmem_limit_bytes=VMEM_LIMIT)


def _const_spec(shape):
    return pl.BlockSpec(shape, lambda *_: (0,) * len(shape))


def _tile(n, pref=TOKEN_TILE):
    t = min(n, pref)
    while n % t:
        t -= 8
    return t


def _rms(x, g):
    return x * lax.rsqrt(jnp.mean(x * x, axis=-1, keepdims=True) + RMS_EPS) * g


def _bdot(a, b):
    return jnp.dot(a.astype(BF16), b, preferred_element_type=F32)


def _head_sum(x, e_ref, et_ref):
    return _bdot(_bdot(x, e_ref[...]), et_ref[...])


def _ffn_kernel(*refs, bounds, with_pe, final):
    ns = len(bounds)
    xs, (g_ref, wi_ref, wo_ref), rest = refs[:ns], refs[ns:ns + 3], refs[ns + 3:]
    if with_pe:
        ps, (gp_ref, wgate_ref, wproj_ref, gf_ref), outs = rest[:ns], rest[ns:ns + 4], rest[ns + 4:]
    else:
        outs = rest

    def body(k):
        x = xs[k][...]
        xn = _rms(x, g_ref[...]).astype(BF16)
        acc = jnp.zeros_like(x)
        for j in range(N_FFN_CHUNKS):
            lo, hi = j * FFN_CHUNK, (j + 1) * FFN_CHUNK
            gate = jnp.dot(xn, wi_ref[:, lo:hi].astype(BF16), preferred_element_type=F32)
            up = jnp.dot(xn, wi_ref[:, D_FF + lo:D_FF + hi].astype(BF16), preferred_element_type=F32)
            act = (gate * jax.nn.sigmoid(gate) * up).astype(BF16)
            acc = acc + jnp.dot(act, wo_ref[lo:hi, :].astype(BF16), preferred_element_type=F32)
        y = x + 0.5 * acc
        if with_pe:
            gate = jax.nn.sigmoid(_bdot(_rms(y, gp_ref[...]), wgate_ref[...]))
            y = y + gate * _bdot(ps[k][...], wproj_ref[...])
            if final:
                y = _rms(y, gf_ref[...])
        outs[k][...] = y

    if ns == 1:
        body(0)
    else:
        i = pl.program_id(0)
        for k, (lo, hi) in enumerate(bounds):
            pl.when(jnp.logical_and(i >= lo, i < hi))(functools.partial(body, k))


def _ffn(hs, g, wi, wo, pe=None, layer=0, final=False):
    tiles = [_tile(h.shape[0], FFN_TILE) for h in hs]
    counts = [h.shape[0] // t for h, t in zip(hs, tiles)]
    bounds = [(sum(counts[:k]), sum(counts[:k + 1])) for k in range(len(hs))]

    def rows(k, width, lead=()):
        lo, cnt = bounds[k][0], counts[k]
        return pl.BlockSpec(tuple(None for _ in lead) + (tiles[k], width),
                            lambda i: lead + (jnp.clip(i - lo, 0, cnt - 1), 0))

    args = list(hs) + [g, wi, wo]
    in_specs = ([rows(k, D_MODEL) for k in range(len(hs))]
                + [_const_spec((1, D_MODEL)),
                   pl.BlockSpec((None,) + wi.shape[1:], lambda i: (layer, 0, 0)),
                   pl.BlockSpec((None,) + wo.shape[1:], lambda i: (layer, 0, 0))])
    if pe is not None:
        args += list(pe[0]) + list(pe[1:])
        in_specs += ([rows(k, PLE_DIM, (layer,)) for k in range(len(hs))]
                     + [_const_spec(a.shape) for a in pe[1:]])
    return pl.pallas_call(
        functools.partial(_ffn_kernel, bounds=tuple(bounds), with_pe=pe is not None, final=final),
        grid=(bounds[-1][1],),
        in_specs=in_specs,
        out_specs=[rows(k, D_MODEL) for k in range(len(hs))],
        out_shape=[jax.ShapeDtypeStruct((h.shape[0], D_MODEL), F32) for h in hs],
        compiler_params=_params("arbitrary"),
        name="ffn_pe" if pe is not None else "ffn",
    )(*args)


def _norm_mm_kernel(x_ref, g_ref, w_ref, o_ref):
    o_ref[...] = _bdot(_rms(x_ref[...], g_ref[...]), w_ref[...]).astype(o_ref.dtype)


def _norm_mm(h, g, w, out_dtype, col_tile=None, name="norm_mm"):
    n = h.shape[0]
    tm = _tile(n)
    nout = w.shape[1]
    tn = nout if col_tile is None else col_tile
    return pl.pallas_call(
        _norm_mm_kernel,
        grid=(n // tm, nout // tn),
        in_specs=[pl.BlockSpec((tm, D_MODEL), lambda i, j: (i, 0)), _const_spec((1, D_MODEL)),
                  pl.BlockSpec((D_MODEL, tn), lambda i, j: (0, j))],
        out_specs=pl.BlockSpec((tm, tn), lambda i, j: (i, j)),
        out_shape=jax.ShapeDtypeStruct((n, nout), out_dtype),
        compiler_params=_params("arbitrary", "arbitrary"),
        name=name,
    )(h, g, w)


def _rwkv_layer_kernel(h_ref, sh_ref, s0_ref, nw_ref, mix_ref, wrkv_ref, w0_ref, w1_ref, w2_ref, a0_ref, a1_ref,
                       a2_ref, g1_ref, g2_ref, kk_ref, ka_ref, lnw_ref, lnb_ref, rk_ref, wo_ref, e_ref, et_ref,
                       o_ref, hn_o, st_ref,
                       state, prev, r_s, lw_s, k_s, v_s, kk_s, b_s, g_s, y_s, *, rows, n_sub, nseq, npar):
    c = pl.program_id(1)
    C = WKV_CHUNK
    C2 = 2 * C
    hp = N_HEADS // 2
    n_pairs = npar * hp
    seq_rows = rows // nseq
    short = seq_rows % C != 0
    total = npar * rows

    def load_state(s0, base=0):
        for h in range(N_HEADS):
            lo = (h % 2) * HEAD_DIM
            state[base + h // 2] = jnp.zeros((LANES, LANES), F32)
        for h in range(N_HEADS):
            lo = (h % 2) * HEAD_DIM
            state[base + h // 2, lo:lo + HEAD_DIM, lo:lo + HEAD_DIM] = s0[h]

    def store_state(st, base=0):
        for h in range(N_HEADS):
            lo = (h % 2) * HEAD_DIM
            st[h] = state[base + h // 2, lo:lo + HEAD_DIM, lo:lo + HEAD_DIM]

    if nseq == 1:
        @pl.when(c == 0)
        def _():
            for q in range(npar):
                load_state(s0_ref.at[q], q * hp)
                prev[q:q + 1, :] = sh_ref[q]
    if short:
        for ref in (r_s, lw_s, k_s, v_s, kk_s, b_s):
            ref[rows:, :] = jnp.zeros((ref.shape[0] - rows, D_MODEL), F32)

    nw = nw_ref[...]
    h_in = h_ref[...].reshape(total, D_MODEL) if nseq == 1 else h_ref[...]
    hn = _rms(h_in, nw)
    row = lax.broadcasted_iota(jnp.int32, hn.shape, 0)
    rolled = pltpu.roll(hn, 1, 0)
    if nseq == 1:
        x_prev = rolled
        for q in range(npar):
            x_prev = jnp.where(row == q * rows, prev[q:q + 1, :], x_prev)
            last = hn[(q + 1) * rows - 1:(q + 1) * rows]
            prev[q:q + 1, :] = last
            hn_o[q] = last
    else:
        x_prev = jnp.where(lax.rem(row, seq_rows) == 0, sh_ref[...], rolled)
        hn_o[...] = hn
    xx = x_prev - hn
    mix = mix_ref[...]
    xr, xw, xk, xv, xa, xg = (hn + xx * mix[j:j + 1] for j in range(6))
    r = _bdot(xr, wrkv_ref[0])
    k = _bdot(xk, wrkv_ref[1])
    v = _bdot(xv, wrkv_ref[2])
    wl = w0_ref[...] + _bdot(jnp.tanh(_bdot(xw, w1_ref[...])), w2_ref[...])
    lw_s[0:total] = -EXP_MINUS_HALF * jax.nn.sigmoid(wl)
    a = jax.nn.sigmoid(a0_ref[...] + _bdot(_bdot(xa, a1_ref[...]), a2_ref[...]))
    g_s[...] = _bdot(jax.nn.sigmoid(_bdot(xg, g1_ref[...])), g2_ref[...])
    kkv = k * kk_ref[...]
    kk = kkv / jnp.maximum(jnp.sqrt(_head_sum(kkv * kkv, e_ref, et_ref)), 1e-12)
    r_s[0:total] = r
    k_s[0:total] = k * (1.0 + (a - 1.0) * ka_ref[...])
    v_s[0:total] = v
    kk_s[0:total] = kk
    b_s[0:total] = kk * a

    ri = lax.broadcasted_iota(jnp.int32, (C, C), 0)
    ci = lax.broadcasted_iota(jnp.int32, (C, C), 1)
    tri = (ri >= ci).astype(BF16)
    r2 = lax.broadcasted_iota(jnp.int32, (C2, C2), 0)
    c2 = lax.broadcasted_iota(jnp.int32, (C2, C2), 1)
    strict = r2 > c2
    incl = r2 >= c2
    lane = lax.broadcasted_iota(jnp.int32, (C, LANES), 1)
    head0 = lane < HEAD_DIM

    def hat(x):
        x3 = jnp.stack([x[:, p * LANES:(p + 1) * LANES] for p in range(n_pairs)])
        return jnp.concatenate([jnp.where(head0, x3, 0.0), jnp.where(head0, 0.0, x3)], axis=1)

    def mm(a, b):
        return jnp.einsum("pmk,pkn->pmn", a.astype(BF16), b.astype(BF16), preferred_element_type=F32)

    def mm_nt(a, b):
        return jnp.einsum("pmk,pnk->pmn", a.astype(BF16), b.astype(BF16), preferred_element_type=F32)

    chunk_row = lax.broadcasted_iota(jnp.int32, (C, D_MODEL), 0)

    def chunk(ref, s):
        if short:
            x = ref[pl.ds(pl.multiple_of(s * seq_rows, 8), C), :]
            return jnp.where(chunk_row < seq_rows, x, 0.0)
        return jnp.concatenate([ref[pl.ds(q * rows + s * C, C), :] for q in range(npar)], axis=1)

    def sub_chunk(s, carry):
        lw = chunk(lw_s, s)
        p1 = lw.astype(BF16)
        rem = lw - p1.astype(F32)
        p2 = rem.astype(BF16)
        p3 = (rem - p2.astype(F32)).astype(BF16)
        cw = (jnp.dot(tri, p1, preferred_element_type=F32) + jnp.dot(tri, p2, preferred_element_type=F32)
              + jnp.dot(tri, p3, preferred_element_type=F32))
        cw_end = cw[C - 1:C, :]
        e_neg = jnp.exp(-cw)
        e_end = jnp.exp(cw_end - cw)
        kk_c = chunk(kk_s, s)
        bb = chunk(b_s, s)
        k_c = chunk(k_s, s)
        a_all = kk_c * jnp.exp(cw - lw)
        r_all = chunk(r_s, s) * jnp.exp(cw)
        b_all = bb * e_neg
        k_all = k_c * e_neg
        bd_all = bb * e_end
        kd_all = k_c * e_end
        v_all = chunk(v_s, s)
        decay = jnp.exp(cw_end)
        ar_h = jnp.concatenate([hat(a_all), hat(r_all)], axis=1)
        bk_h = jnp.concatenate([hat(b_all), hat(k_all)], axis=1)
        v_h = hat(v_all)
        st = state[...]
        g = mm_nt(ar_h, bk_h)
        low = jnp.where(strict, g[:, :C2, :C2], 0.0)
        ak = jnp.where(strict, g[:, :C2, C2:], 0.0)
        rbk = jnp.concatenate([jnp.where(incl, g[:, C2:, :C2], 0.0), jnp.where(incl, g[:, C2:, C2:], 0.0)], axis=2)
        ss = mm_nt(ar_h, st)
        x = -(ss[:, :C2] + mm(ak, v_h))
        t = mm(low, jnp.concatenate([low, x], axis=2))
        lp = t[:, :, :C2]
        x = x - t[:, :, C2:]
        for _ in range(max(int(np.ceil(np.log2(min(C, seq_rows)))), 2) - 2):
            t = mm(lp, jnp.concatenate([lp, x], axis=2))
            lp = t[:, :, :C2]
            x = x + t[:, :, C2:]
        x = x + mm(lp, x)
        xv = jnp.concatenate([x, v_h], axis=1)
        y_h = ss[:, C2:] + mm(rbk, xv)
        y = y_h[:, :C] + y_h[:, C:]
        for p in range(n_pairs):
            q, pl_ = divmod(p, hp)
            if short:
                y_s[pl.ds(pl.multiple_of(s * seq_rows, 8), seq_rows), pl_ * LANES:(pl_ + 1) * LANES] = y[p, :seq_rows]
            else:
                y_s[pl.ds(q * rows + s * C, C), pl_ * LANES:(pl_ + 1) * LANES] = y[p]
        bkd_h = jnp.concatenate([hat(bd_all), hat(kd_all)], axis=1)
        xv_t = jnp.stack([xv[p].T for p in range(n_pairs)])
        dec3 = jnp.stack([decay[:, p * LANES:(p + 1) * LANES] for p in range(n_pairs)])
        state[...] = st * dec3 + mm(xv_t, bkd_h)
        return carry

    if nseq == 1:
        for s in range(n_sub):
            sub_chunk(s, 0)
    else:
        def one_sequence(s, carry):
            load_state(s0_ref.at[s])
            sub_chunk(s, carry)
            store_state(st_ref.at[s])
            return carry

        lax.fori_loop(0, nseq, one_sequence, 0)

    y = y_s[0:total]
    inv_n = 1.0 / HEAD_DIM
    mu = _head_sum(y, e_ref, et_ref) * inv_n
    yc = y - mu
    var = _head_sum(yc * yc, e_ref, et_ref) * inv_n
    yn = yc * lax.rsqrt(var + LNX_EPS) * lnw_ref[...] + lnb_ref[...]
    bonus = _head_sum(r_s[0:total] * k_s[0:total] * rk_ref[...], e_ref, et_ref) * v_s[0:total]
    out = h_in + _bdot((yn + bonus) * g_s[...], wo_ref[...])
    o_ref[...] = out.reshape(o_ref.shape)

    if nseq == 1:
        @pl.when(c == pl.num_programs(1) - 1)
        def _():
            for q in range(npar):
                store_state(st_ref.at[q], q * hp)


def _rwkv_layer(h, shift, s0, seq_len, lw):
    n = h.shape[0]
    nb = n // seq_len
    if seq_len % WKV_CHUNK == 0:
        npar = RWKV_PARALLEL_SEQS if nb % RWKV_PARALLEL_SEQS == 0 else 1
        nseq, rows = 1, _tile(seq_len, RWKV_STEP_ROWS // npar)
        steps, n_sub, total = seq_len // rows, rows // WKV_CHUNK, npar * rows
        buf_rows = total
        assert rows % WKV_CHUNK == 0
        h_in = h.reshape(nb, seq_len, D_MODEL)
        row_spec = pl.BlockSpec((npar, rows, D_MODEL), lambda bi, ci: (bi, ci, 0))
        sh_in = shift.reshape(nb, 1, D_MODEL)
        sh_spec = pl.BlockSpec((npar, 1, D_MODEL), lambda bi, ci: (bi, 0, 0))
        hn_shape, hn_spec = jax.ShapeDtypeStruct((nb, 1, D_MODEL), F32), sh_spec
        nst = npar
    else:
        assert seq_len < WKV_CHUNK and seq_len % 8 == 0
        nseq, npar = _tile(nb, RWKV_SHORT_SEQS), 1
        rows, steps, n_sub = nseq * seq_len, 1, 1
        total = rows
        buf_rows = rows + WKV_CHUNK - seq_len
        h_in = h
        row_spec = pl.BlockSpec((rows, D_MODEL), lambda bi, ci: (bi, 0))
        sh_in = jnp.repeat(shift, seq_len, axis=0)
        sh_spec = row_spec
        hn_shape, hn_spec = jax.ShapeDtypeStruct((n, D_MODEL), F32), sh_spec
        nst = nseq
    st_spec = pl.BlockSpec((nst, N_HEADS, HEAD_DIM, HEAD_DIM), lambda bi, ci: (bi, 0, 0, 0))
    consts = [lw["nw"], lw["mix"], lw["wrkv"], lw["w0"], lw["w1"], lw["w2"], lw["a0"], lw["a1"], lw["a2"],
              lw["g1"], lw["g2"], lw["kk"], lw["ka"], lw["lnw"], lw["lnb"], lw["rk"], lw["wo"], lw["e"], lw["et"]]
    seq_buf = pltpu.VMEM((buf_rows, D_MODEL), F32)
    out, hn, st = pl.pallas_call(
        functools.partial(_rwkv_layer_kernel, rows=rows, n_sub=n_sub, nseq=nseq, npar=npar),
        grid=(nb // nst, steps),
        in_specs=[row_spec, sh_spec, st_spec] + [_const_spec(c.shape) for c in consts],
        out_specs=[row_spec, hn_spec, st_spec],
        out_shape=[jax.ShapeDtypeStruct(h_in.shape, F32), hn_shape, jax.ShapeDtypeStruct(s0.shape, F32)],
        scratch_shapes=[pltpu.VMEM((npar * N_HEADS // 2, LANES, LANES), F32), pltpu.VMEM((8, D_MODEL), F32)]
                       + [seq_buf] * 6 + [pltpu.VMEM((total, D_MODEL), F32), seq_buf],
        compiler_params=_params("arbitrary", "arbitrary"),
        name="rwkv_layer",
    )(h_in, sh_in, s0, *consts)
    out = out.reshape(n, D_MODEL)
    shift_out = hn.reshape(nb, D_MODEL) if nseq == 1 else hn.reshape(nb, seq_len, D_MODEL)[:, -1]
    return out, shift_out, st


def _t5_buckets(dist):
    d = np.asarray(dist, dtype=np.int64)
    max_exact = REL_BUCKETS // 2
    large = max_exact + (np.log(np.maximum(d, 1) / max_exact) / np.log(REL_MAX_DIST / max_exact)
                         * (REL_BUCKETS - max_exact)).astype(np.int32)
    large = np.minimum(large, REL_BUCKETS - 1)
    return np.where(d < max_exact, d, large).astype(np.int32)


def _toeplitz(tab, n_rows, n_cols):
    period = tab.shape[-1]
    assert period >= n_rows + n_cols - 1 and n_cols <= period - 1
    lead = tab.shape[:-1]
    flat = jnp.broadcast_to(tab[..., None, :], lead + (n_rows, period)).reshape(lead + (-1,))
    skew = flat[..., :n_rows * (period - 1)].reshape(lead + (n_rows, period - 1))
    return skew[..., :n_cols]


def _band_bias(rel_bias, group):
    win, dil = DILATION_GROUPS[group]
    assert win // dil == Q_BLOCK
    period = 3 * Q_BLOCK - 1
    idx = np.arange(period)
    m = Q_BLOCK - np.where(idx < 2 * Q_BLOCK, idx, idx - period)
    valid = (m >= 0) & (m <= Q_BLOCK)
    buckets = _t5_buckets(dil * np.clip(m, 0, Q_BLOCK))
    tbl = jnp.take(rel_bias[:, group * N_HEADS:(group + 1) * N_HEADS].astype(F32), buckets, axis=0).T
    general = _toeplitz(jnp.where(valid[None], tbl, NEG_INF), Q_BLOCK, 2 * Q_BLOCK)
    first = jnp.where((np.arange(2 * Q_BLOCK) >= Q_BLOCK)[None, None, :], general, NEG_INF)
    return jnp.stack([first, general])


_HEAD_OF_SEG = np.array([4 * (2 * (s // 8) + s % 2) + (s // 2) % 4 for s in range(N_HEADS)])


def _proj_rm_kernel(x_ref, g_ref, w_ref, *refs, natural, lane_ranges, tm):
    scr = refs[-1]
    outs = refs[:-1]
    y = _bdot(_rms(x_ref[...], g_ref[...]), w_ref[...])
    if natural:
        outs[0][...] = y
        outs = outs[1:]
    for c in range(scr.shape[0]):
        scr[c] = y[:, c * LANES:(c + 1) * LANES]
    for gi, (_, dil) in enumerate(DILATION_GROUPS):
        lo, hi = lane_ranges[gi]
        if dil == 1:
            outs[gi][0] = y[:, lo:hi].astype(BF16)
            continue
        for rho in range(dil):
            rows = [scr[c, pl.ds(rho, tm // dil, stride=dil), :] for c in range(lo // LANES, hi // LANES)]
            outs[gi][rho] = jnp.concatenate(rows, axis=1).astype(BF16)


def _proj_rm(h, g, w, nb, seq_len, natural, lane_ranges, name):
    n = h.shape[0]
    tm = _tile(n, PROJ_TILE)
    nout = w.shape[1]
    assert seq_len % tm == 0 and all(tm % (16 * dil) == 0 for _, dil in DILATION_GROUPS)
    tps = seq_len // tm
    out_shape, out_specs = [], []
    if natural:
        out_shape.append(jax.ShapeDtypeStruct((n, nout), F32))
        out_specs.append(pl.BlockSpec((tm, nout), lambda i: (i, 0)))
    for (_, dil), (lo, hi) in zip(DILATION_GROUPS, lane_ranges):
        out_shape.append(jax.ShapeDtypeStruct((nb, dil, seq_len // dil, hi - lo), BF16))
        out_specs.append(pl.BlockSpec((None, dil, tm // dil, hi - lo), lambda i: (i // tps, 0, i % tps, 0)))
    return pl.pallas_call(
        functools.partial(_proj_rm_kernel, natural=natural, lane_ranges=lane_ranges, tm=tm),
        grid=(n // tm,),
        in_specs=[pl.BlockSpec((tm, D_MODEL), lambda i: (i, 0)), _const_spec((1, D_MODEL)), _const_spec(w.shape)],
        out_specs=out_specs,
        out_shape=out_shape,
        scratch_shapes=[pltpu.VMEM((nout // LANES, tm, LANES), F32)],
        compiler_params=_params("arbitrary"),
        name=name,
    )(h, g, w)


def _attn_kernel(q_ref, kv_ref, halo_ref, bias_ref, o_ref, lse_ref, kvbuf, *, nq):
    i = pl.program_id(0)
    dil = q_ref.shape[0]
    kvd = KV_HEADS * HEAD_DIM
    nt = (((1,), (1,)), ((), ()))
    kvbuf[:, :Q_BLOCK, :] = halo_ref[...]
    kvbuf[:, Q_BLOCK:, :] = kv_ref[...]
    lane = lax.broadcasted_iota(jnp.int32, (Q_BLOCK, LANES), 1)
    low_half = lane < HEAD_DIM
    lane_row = lax.broadcasted_iota(jnp.int32, (1, LANES), 1)
    keep_lo = (lane_row < HEAD_DIM).astype(BF16)
    keep_hi = (lane_row >= HEAD_DIM).astype(BF16)

    def block(u, carry):
        rho = lax.div(u, nq)
        j = u - rho * nq
        r0 = pl.multiple_of(j * Q_BLOCK, Q_BLOCK)
        qb = q_ref[rho, pl.ds(r0, Q_BLOCK), :]
        kvb = kvbuf[rho, pl.ds(r0, 2 * Q_BLOCK), :]
        bsel = jnp.where(jnp.logical_and(i == 0, j == 0), 0, 1)
        lse_tile = jnp.zeros((Q_BLOCK, LANES), F32)
        for G in range(KV_HEADS // 2):
            kg = kvb[:, G * LANES:(G + 1) * LANES]
            vg = kvb[:, kvd + G * LANES:kvd + (G + 1) * LANES]
            vcat = jnp.concatenate([vg * keep_lo, vg * keep_hi], axis=0)
            pieces = []
            for r in range(Q_PER_KV):
                qg = qb[:, (G * Q_PER_KV + r) * LANES:(G * Q_PER_KV + r + 1) * LANES]
                pieces += [qg * keep_lo, qg * keep_hi]
            s_all = lax.dot_general(jnp.concatenate(pieces, axis=0), kg, nt, preferred_element_type=F32)
            for r in range(Q_PER_KV):
                grp = G * Q_PER_KV + r
                parts = []
                for half in range(2):
                    s = s_all[(2 * r + half) * Q_BLOCK:(2 * r + half + 1) * Q_BLOCK]
                    s = s + bias_ref[bsel, int(_HEAD_OF_SEG[2 * grp + half])]
                    m = jnp.max(s, axis=-1, keepdims=True)
                    p = jnp.exp(s - m)
                    l = jnp.sum(p, axis=-1, keepdims=True)
                    parts.append((p.astype(BF16), l, m + jnp.log(l)))
                (p0, l0, e0), (p1, l1, e1) = parts
                o = jnp.dot(jnp.concatenate([p0, p1], axis=1), vcat, preferred_element_type=F32)
                o = (o / jnp.where(low_half, l0, l1)).astype(o_ref.dtype)
                o_ref[rho, pl.ds(r0, Q_BLOCK), grp * LANES:(grp + 1) * LANES] = o
                lse_tile = jnp.where(lane == 2 * grp, e0, lse_tile)
                lse_tile = jnp.where(lane == 2 * grp + 1, e1, lse_tile)
        lse_ref[rho, pl.ds(r0, Q_BLOCK), :] = lse_tile
        return carry

    lax.fori_loop(0, dil * nq, block, 0)


def _attn_group(q_rm, kv_rm, bias):
    nb, dil, tsub, _ = q_rm.shape
    kvw = kv_rm.shape[-1]
    rows = ATTN_SPAN // dil
    nq = rows // Q_BLOCK
    assert tsub % rows == 0 and nq >= 1
    span_spec = lambda width: pl.BlockSpec((None, dil, rows, width), lambda i, b: (b, 0, i, 0))
    return pl.pallas_call(
        functools.partial(_attn_kernel, nq=nq),
        grid=(tsub // rows, nb),
        in_specs=[span_spec(D_MODEL), span_spec(kvw),
                  pl.BlockSpec((None, dil, Q_BLOCK, kvw), lambda i, b: (b, 0, jnp.maximum(i * nq - 1, 0), 0)),
                  _const_spec(bias.shape)],
        out_specs=[span_spec(D_MODEL), span_spec(LANES)],
        out_shape=[jax.ShapeDtypeStruct((nb, dil, tsub, D_MODEL), BF16),
                   jax.ShapeDtypeStruct((nb, dil, tsub, LANES), F32)],
        scratch_shapes=[pltpu.VMEM((dil, Q_BLOCK + rows, kvw), BF16)],
        compiler_params=_params("arbitrary", "arbitrary"),
        name=f"attn_d{dil}",
    )(q_rm, kv_rm, kv_rm, bias)


def _attn_out_kernel(o0_ref, o1_ref, o2_ref, l0_ref, l1_ref, l2_ref, h_ref, wo_ref, et_ref, out_ref, *scr, tm):
    outs, lses = [], []
    for gi, (o_ref, l_ref) in enumerate(((o0_ref, l0_ref), (o1_ref, l1_ref), (o2_ref, l2_ref))):
        dil = DILATION_GROUPS[gi][1]
        if dil == 1:
            outs.append(o_ref[0].astype(F32))
            lses.append(l_ref[0])
            continue
        so, sl = scr[2 * gi], scr[2 * gi + 1]
        n_tiles = so.shape[0]
        for rho in range(dil):
            rows = pl.ds(rho, tm // dil, stride=dil)
            for c in range(n_tiles):
                so[c, rows, :] = o_ref[rho, :, c * LANES:(c + 1) * LANES].astype(F32)
            sl[rows, :] = l_ref[rho]
        outs.append(jnp.concatenate([so[c] for c in range(n_tiles)], axis=1))
        lses.append(sl[...])
    l0, l1, l2 = lses
    m = jnp.maximum(jnp.maximum(l0, l1), l2)
    w0, w1, w2 = jnp.exp(l0 - m), jnp.exp(l1 - m), jnp.exp(l2 - m)
    inv = 1.0 / (w0 + w1 + w2)
    et = et_ref[...]
    att = _bdot(w0 * inv, et) * outs[0] + _bdot(w1 * inv, et) * outs[1] + _bdot(w2 * inv, et) * outs[2]
    out_ref[...] = h_ref[...] + _bdot(att, wo_ref[...])


def _attn_out(outs, lses, h, wo, et, seq_len):
    n = h.shape[0]
    tm = _tile(n, PROJ_TILE)
    tps = seq_len // tm
    row_spec = pl.BlockSpec((tm, D_MODEL), lambda i: (i, 0))

    def rm_spec(dil, width):
        return pl.BlockSpec((None, dil, tm // dil, width), lambda i: (i // tps, 0, i % tps, 0))

    dils = [dil for _, dil in DILATION_GROUPS]
    scratch = []
    for _ in dils:
        scratch += [pltpu.VMEM((D_MODEL // LANES, tm, LANES), F32), pltpu.VMEM((tm, LANES), F32)]
    return pl.pallas_call(
        functools.partial(_attn_out_kernel, tm=tm),
        grid=(n // tm,),
        in_specs=[rm_spec(d, D_MODEL) for d in dils] + [rm_spec(d, LANES) for d in dils]
                 + [row_spec, _const_spec(wo.shape), _const_spec(et.shape)],
        out_specs=row_spec,
        out_shape=jax.ShapeDtypeStruct((n, D_MODEL), F32),
        scratch_shapes=scratch,
        compiler_params=_params("arbitrary"),
        name="attn_out",
    )(*outs, *lses, h, wo, et)


def _decode_bias(rel_bias, seq_len, cache_len):
    ncol = cache_len + LANES
    period = seq_len + ncol - 1
    idx = np.arange(period)
    dist = cache_len - np.where(idx < ncol, idx, idx - period)
    buckets = _t5_buckets(np.clip(dist, 0, MAX_WINDOW))
    tabs = []
    for g, (win, dil) in enumerate(DILATION_GROUPS):
        valid = (dist >= 0) & (dist % dil == 0) & (dist <= win)
        tbl = jnp.take(rel_bias[:, g * N_HEADS:(g + 1) * N_HEADS].astype(F32), buckets, axis=0).T
        tabs.append(jnp.where(valid[None], tbl, NEG_INF))
    rows = _toeplitz(jnp.stack(tabs), seq_len, ncol)
    bias = rows.reshape(N_GROUPS * N_HEADS * seq_len, ncol)
    return bias[:, :cache_len], bias[:, cache_len:]


def _attn_decode_kernel(q_ref, cache_ref, kvn_ref, h_ref, bc_ref, bn_ref, wo_ref, out_ref, *, seq_len):
    kvd = KV_HEADS * HEAD_DIM
    nslot = N_GROUPS * N_HEADS
    rows_g = N_HEADS * seq_len
    nt = (((1,), (1,)), ((), ()))
    cache = cache_ref[...]
    kc = cache[:, :kvd].astype(BF16)
    vc = cache[:, kvd:].astype(BF16)
    kvn = kvn_ref[...]
    pad = jnp.zeros((LANES - seq_len, kvd), F32)
    kn = jnp.concatenate([kvn[:, :kvd], pad], axis=0).astype(BF16)
    vn = jnp.concatenate([kvn[:, kvd:], pad], axis=0).astype(BF16)
    lhs = jnp.concatenate([q_ref[:, s * kvd:(s + 1) * kvd] for s in range(nslot)], axis=0).astype(BF16)
    cache_len = kc.shape[0]
    sn = lax.dot_general(lhs, kn, nt, preferred_element_type=F32) + bn_ref[...]
    scs, m = [], None
    for g, (win, _) in enumerate(DILATION_GROUPS):
        c0 = (cache_len - min(win, cache_len)) // LANES * LANES
        rows = slice(g * rows_g, (g + 1) * rows_g)
        sc = lax.dot_general(lhs[rows], kc[c0:], nt, preferred_element_type=F32) + bc_ref[rows, c0:]
        scs.append((sc, c0, rows))
        m_g = jnp.maximum(jnp.max(sc, axis=-1, keepdims=True), jnp.max(sn[rows], axis=-1, keepdims=True))
        m = m_g if m is None else jnp.maximum(m, m_g)
    l = jnp.zeros((rows_g, 1), F32)
    num = jnp.zeros((rows_g, kvd), F32)
    for sc, c0, rows in scs:
        pc = jnp.exp(sc - m)
        pn = jnp.exp(sn[rows] - m)
        l = l + jnp.sum(pc, axis=-1, keepdims=True) + jnp.sum(pn, axis=-1, keepdims=True)
        num = (num + jnp.dot(pc.astype(BF16), vc[c0:], preferred_element_type=F32)
               + jnp.dot(pn.astype(BF16), vn, preferred_element_type=F32))
    row = lax.broadcasted_iota(jnp.int32, (rows_g, kvd), 0)
    lane = lax.broadcasted_iota(jnp.int32, (rows_g, kvd), 1)
    own = (row // (Q_PER_KV * seq_len)) == (lane // HEAD_DIM)
    att = jnp.where(own, num / l, 0.0)
    out = h_ref[...]
    for r in range(Q_PER_KV):
        a_r = att[r * seq_len:(r + 1) * seq_len]
        for c in range(1, KV_HEADS):
            a_r = a_r + att[(c * Q_PER_KV + r) * seq_len:(c * Q_PER_KV + r + 1) * seq_len]
        out = out + _bdot(a_r, wo_ref[r])
    out_ref[...] = out


def _attn_decode(q, cache, kv_new, h, bias_c, bias_n, wo_r, nb, seq_len):
    cache_len = cache.shape[1]
    qw = q.shape[1]
    kvw = 2 * KV_HEADS * HEAD_DIM
    return pl.pallas_call(
        functools.partial(_attn_decode_kernel, seq_len=seq_len),
        grid=(nb,),
        in_specs=[pl.BlockSpec((seq_len, qw), lambda b: (b, 0)),
                  pl.BlockSpec((None, cache_len, kvw), lambda b: (b, 0, 0)),
                  pl.BlockSpec((seq_len, kvw), lambda b: (b, 0)),
                  pl.BlockSpec((seq_len, D_MODEL), lambda b: (b, 0)),
                  _const_spec(bias_c.shape), _const_spec(bias_n.shape), _const_spec(wo_r.shape)],
        out_specs=pl.BlockSpec((seq_len, D_MODEL), lambda b: (b, 0)),
        out_shape=jax.ShapeDtypeStruct((nb * seq_len, D_MODEL), F32),
        compiler_params=_params("arbitrary"),
        name="attn_decode",
    )(q, cache, kv_new, h, bias_c, bias_n, wo_r)


def _prep_weights(norm_w, ffn1_wi, ffn1_wo, ffn2_wi, ffn2_wo, pe_proj, pe_gate,
                  rwkv_mix, rwkv_wrkv, rwkv_wo, rwkv_w0, rwkv_w1, rwkv_w2, rwkv_a0, rwkv_a1, rwkv_a2,
                  rwkv_g1, rwkv_g2, rwkv_kk, rwkv_ka, rwkv_rk, rwkv_lnx_w, rwkv_lnx_b,
                  attn_wq, attn_wo, kv_norm, w_kv, rel_bias, final_norm):
    def row(v):
        return v.reshape(1, -1).astype(F32)

    def pad_cols(w, n):
        return jnp.pad(w, ((0, 0), (0, n - w.shape[1]))).astype(BF16)

    def pad_rows(w, n):
        return jnp.pad(w, ((0, n - w.shape[0]), (0, 0))).astype(BF16)

    head_of_lane = np.arange(D_MODEL) // HEAD_DIM
    e = jnp.asarray(head_of_lane[:, None] == np.arange(LANES)[None, :], BF16)
    et = jnp.asarray(np.arange(LANES)[:, None] == head_of_lane[None, :], BF16)

    depth = norm_w.shape[0]
    layers = []
    for i in range(depth):
        layers.append(dict(
            nw=[row(norm_w[i, j]) for j in range(4)],
            ffn1=(ffn1_wi.astype(F32), ffn1_wo.astype(F32)), ffn2=(ffn2_wi.astype(F32), ffn2_wo.astype(F32)),
            pe_gate=pe_gate[i].astype(BF16), pe_proj=pe_proj[i].astype(BF16)))
    n_a = depth // 2
    rw = []
    for i in range(n_a):
        rw.append(dict(
            nw=row(norm_w[i, 1]), mix=rwkv_mix[i].astype(F32), wrkv=rwkv_wrkv[i].astype(BF16),
            w0=row(rwkv_w0[i]), w1=pad_cols(rwkv_w1[i], LANES), w2=pad_rows(rwkv_w2[i], LANES),
            a0=row(rwkv_a0[i]), a1=pad_cols(rwkv_a1[i], LANES), a2=pad_rows(rwkv_a2[i], LANES),
            g1=pad_cols(rwkv_g1[i], 2 * LANES), g2=pad_rows(rwkv_g2[i], 2 * LANES),
            kk=row(rwkv_kk[i]), ka=row(rwkv_ka[i]), rk=row(rwkv_rk[i]),
            lnw=row(rwkv_lnx_w[i]), lnb=row(rwkv_lnx_b[i]), wo=rwkv_wo[i].astype(BF16), e=e, et=et))
    scale = HEAD_DIM ** -0.5
    at = []
    for j in range(depth - n_a):
        wq = attn_wq[j] * scale
        wo_r = attn_wo[j].reshape(KV_HEADS, Q_PER_KV, HEAD_DIM, D_MODEL).transpose(1, 0, 2, 3)
        wo_r = wo_r.reshape(Q_PER_KV, KV_HEADS * HEAD_DIM, D_MODEL).astype(BF16)
        wq_seg = wq.reshape(D_MODEL, N_GROUPS, N_HEADS, HEAD_DIM)[:, :, _HEAD_OF_SEG].reshape(D_MODEL, -1)
        wo_seg = attn_wo[j].reshape(N_HEADS, HEAD_DIM, D_MODEL)[_HEAD_OF_SEG].reshape(D_MODEL, D_MODEL)
        at.append(dict(wq_seg=wq_seg.astype(BF16), wo_seg=wo_seg.astype(BF16), wo_r=wo_r))
    return dict(layers=layers, rwkv=rw, attn=at, kv_norm=row(kv_norm), w_kv=w_kv.astype(BF16),
                final_norm=row(final_norm), rel_bias=rel_bias, et=et)


def _trunks(streams, w):
    depth = len(w["layers"])
    n_a = depth // 2
    kvw = 2 * KV_HEADS * HEAD_DIM
    st = []
    for sd in streams:
        nb, seq_len, _ = sd["x"].shape
        n = nb * seq_len
        st.append(dict(nb=nb, seq_len=seq_len, n=n, h=sd["x"].reshape(n, D_MODEL).astype(F32),
                       p=sd["p"].reshape(depth, n, PLE_DIM).astype(F32), wkv_out=[], shift_out=[],
                       kv_rows=None, kv_rm=None, **{k: sd[k] for k in ("wkv0", "shift0", "cache")}))

    def ffn(nw, weights, **kw):
        for t, h in zip(st, _ffn([t["h"] for t in st], nw, *weights, **kw)):
            t["h"] = h

    for i in range(depth):
        lw = w["layers"][i]
        if i == n_a:
            for t in st:
                if t["cache"] is None:
                    t["kv_rows"], *t["kv_rm"] = _proj_rm(t["h"], w["kv_norm"], w["w_kv"], t["nb"], t["seq_len"], True,
                                                         [(0, kvw)] * N_GROUPS, "kv_proj")
                else:
                    t["kv_rows"] = _norm_mm(t["h"], w["kv_norm"], w["w_kv"], F32, name="kv_proj")
        ffn(lw["nw"][0], lw["ffn1"], layer=i)
        for t in st:
            nb, seq_len, n, h = t["nb"], t["seq_len"], t["n"], t["h"]
            if i < n_a:
                h, sh, state = _rwkv_layer(h, t["shift0"][i].astype(F32), t["wkv0"][i].astype(F32), seq_len,
                                           w["rwkv"][i])
                t["wkv_out"].append(state)
                t["shift_out"].append(sh)
            elif t["cache"] is None:
                al = w["attn"][i - n_a]
                q_rm = _proj_rm(h, lw["nw"][1], al["wq_seg"], nb, seq_len, False,
                                [(gi * D_MODEL, (gi + 1) * D_MODEL) for gi in range(N_GROUPS)], "q_proj")
                outs, lses = [], []
                for gi in range(N_GROUPS):
                    o, lse = _attn_group(q_rm[gi], t["kv_rm"][gi], _band_bias(w["rel_bias"], gi))
                    outs.append(o)
                    lses.append(lse)
                h = _attn_out(outs, lses, h, al["wo_seg"], w["et"], seq_len)
            else:
                al = w["attn"][i - n_a]
                q = _norm_mm(h, lw["nw"][1], al["wq_seg"], F32, name="q_proj_decode")
                q = q.reshape(n, N_GROUPS, N_HEADS, HEAD_DIM)[:, :, np.argsort(_HEAD_OF_SEG)]
                q = q.reshape(n, N_GROUPS, KV_HEADS, Q_PER_KV, 1, HEAD_DIM)
                q = q * jnp.eye(KV_HEADS, dtype=F32).reshape(1, 1, KV_HEADS, 1, KV_HEADS, 1)
                q = q.reshape(n, N_GROUPS * N_HEADS * KV_HEADS * HEAD_DIM)
                bias_c, bias_n = _decode_bias(w["rel_bias"], seq_len, t["cache"].shape[1])
                h = _attn_decode(q, t["cache"], t["kv_rows"], h, bias_c, bias_n, al["wo_r"], nb, seq_len)
            t["h"] = h
        pe = ([t["p"] for t in st], lw["nw"][3], lw["pe_gate"], lw["pe_proj"], w["final_norm"])
        ffn(lw["nw"][2], lw["ffn2"], pe=pe, layer=i, final=(i == depth - 1))
    results = []
    for t, sd in zip(st, streams):
        dt = sd["x"].dtype
        y = t["h"].reshape(t["nb"], t["seq_len"], D_MODEL).astype(dt)
        kv_rows = t["kv_rows"].reshape(t["nb"], t["seq_len"], 2, KV_HEADS, HEAD_DIM).astype(dt)
        results.append((y, jnp.stack(t["wkv_out"]).astype(dt), jnp.stack(t["shift_out"]).astype(dt), kv_rows))
    return results


def _trunk(x, p, wkv0, shift0, cache, w):
    return _trunks([dict(x=x, p=p, wkv0=wkv0, shift0=shift0, cache=cache)], w)[0]


def kernel(x_prompt, x_sample, state_wkv, state_shift, cache_kv, p_prompt, p_sample, norm_w, ffn1_wi, ffn1_wo, ffn2_wi, ffn2_wo, pe_proj, pe_gate, rwkv_mix, rwkv_wrkv, rwkv_wo, rwkv_w0, rwkv_w1, rwkv_w2, rwkv_a0, rwkv_a1, rwkv_a2, rwkv_g1, rwkv_g2, rwkv_kk, rwkv_ka, rwkv_rk, rwkv_lnx_w, rwkv_lnx_b, attn_wq, attn_wo, kv_norm, w_kv, rel_bias, final_norm):
    w = _prep_weights(norm_w, ffn1_wi, ffn1_wo, ffn2_wi, ffn2_wo, pe_proj, pe_gate,
                      rwkv_mix, rwkv_wrkv, rwkv_wo, rwkv_w0, rwkv_w1, rwkv_w2, rwkv_a0, rwkv_a1, rwkv_a2,
                      rwkv_g1, rwkv_g2, rwkv_kk, rwkv_ka, rwkv_rk, rwkv_lnx_w, rwkv_lnx_b,
                      attn_wq, attn_wo, kv_norm, w_kv, rel_bias, final_norm)
    n_a = norm_w.shape[0] // 2
    nb, seq_len, _ = x_prompt.shape
    wkv0 = jnp.zeros((n_a, nb, N_HEADS, HEAD_DIM, HEAD_DIM), F32)
    shift0 = jnp.zeros((n_a, nb, D_MODEL), x_prompt.dtype)
    cache = cache_kv.reshape(cache_kv.shape[0], cache_kv.shape[1], 2 * KV_HEADS * HEAD_DIM).astype(F32)
    (y_p, wkv_p, shift_p, kv_p), (y_s, wkv_s, shift_s, kv_s) = _trunks(
        [dict(x=x_prompt, p=p_prompt, wkv0=wkv0, shift0=shift0, cache=None),
         dict(x=x_sample, p=p_sample, wkv0=state_wkv, shift0=state_shift, cache=cache)], w)
    kv_prompt = kv_p[:, seq_len - min(MAX_WINDOW, seq_len):]
    return (y_p, y_s, wkv_p, shift_p, kv_prompt, wkv_s, shift_s, kv_s)
```

```python
import functools

import numpy as np
import jax
import jax.numpy as jnp
from jax import lax
from jax.experimental import pallas as pl
from jax.experimental.pallas import tpu as pltpu

F32 = jnp.float32
BF16 = jnp.bfloat16

D_MODEL = 1024
D_FF = 2816
PLE_DIM = 256
RMS_EPS = 1e-6
HEAD_DIM = 64
N_HEADS = D_MODEL // HEAD_DIM
LNX_EPS = 64e-5
KV_HEADS = 4
Q_PER_KV = N_HEADS // KV_HEADS
DILATION_GROUPS = ((128, 1), (512, 4), (2048, 16))
N_GROUPS = len(DILATION_GROUPS)
MAX_WINDOW = 2048
REL_BUCKETS = 32
REL_MAX_DIST = 2048
NEG_INF = -1e30

LANES = 128
FFN_CHUNK = 256
N_FFN_CHUNKS = D_FF // FFN_CHUNK
TOKEN_TILE = 256
FFN_TILE = 512
PROJ_TILE = 1024
WKV_CHUNK = 64
RWKV_SHORT_SEQS = 8
RWKV_PARALLEL_SEQS = 2
RWKV_STEP_ROWS = 512
Q_BLOCK = 128
ATTN_SPAN = 2048
VMEM_LIMIT = 56 * 1024 * 1024
EXP_MINUS_HALF = 0.6065306597126334


def _params(*sem):
    return pltpu.CompilerParams(dimension_semantics=sem, vmem_limit_bytes=VMEM_LIMIT)


def _const_spec(shape):
    return pl.BlockSpec(shape, lambda *_: (0,) * len(shape))


def _tile(n, pref=TOKEN_TILE):
    t = min(n, pref)
    while n % t:
        t -= 8
    return t


def _rms(x, g):
    return x * lax.rsqrt(jnp.mean(x * x, axis=-1, keepdims=True) + RMS_EPS) * g


def _bdot(a, b):
    return jnp.dot(a.astype(BF16), b, preferred_element_type=F32)


def _head_sum(x, e_ref, et_ref):
    return _bdot(_bdot(x, e_ref[...]), et_ref[...])


def _ffn_kernel(*refs, bounds, with_pe, final):
    ns = len(bounds)
    xs, (g_ref, wi_ref, wo_ref), rest = refs[:ns], refs[ns:ns + 3], refs[ns + 3:]
    if with_pe:
        ps, (gp_ref, wgate_ref, wproj_ref, gf_ref), outs = rest[:ns], rest[ns:ns + 4], rest[ns + 4:]
    else:
        outs = rest

    def body(k):
        x = xs[k][...]
        xn = _rms(x, g_ref[...]).astype(BF16)
        acc = jnp.zeros_like(x)
        for j in range(N_FFN_CHUNKS):
            lo, hi = j * FFN_CHUNK, (j + 1) * FFN_CHUNK
            gate = jnp.dot(xn, wi_ref[:, lo:hi].astype(BF16), preferred_element_type=F32)
            up = jnp.dot(xn, wi_ref[:, D_FF + lo:D_FF + hi].astype(BF16), preferred_element_type=F32)
            act = (gate * jax.nn.sigmoid(gate) * up).astype(BF16)
            acc = acc + jnp.dot(act, wo_ref[lo:hi, :].astype(BF16), preferred_element_type=F32)
        y = x + 0.5 * acc
        if with_pe:
            gate = jax.nn.sigmoid(_bdot(_rms(y, gp_ref[...]), wgate_ref[...]))
            y = y + gate * _bdot(ps[k][...], wproj_ref[...])
            if final:
                y = _rms(y, gf_ref[...])
        outs[k][...] = y

    if ns == 1:
        body(0)
    else:
        i = pl.program_id(0)
        for k, (lo, hi) in enumerate(bounds):
            pl.when(jnp.logical_and(i >= lo, i < hi))(functools.partial(body, k))


def _ffn(hs, g, wi, wo, pe=None, layer=0, final=False):
    tiles = [_tile(h.shape[0], FFN_TILE) for h in hs]
    counts = [h.shape[0] // t for h, t in zip(hs, tiles)]
    bounds = [(sum(counts[:k]), sum(counts[:k + 1])) for k in range(len(hs))]

    def rows(k, width, lead=()):
        lo, cnt = bounds[k][0], counts[k]
        return pl.BlockSpec(tuple(None for _ in lead) + (tiles[k], width),
                            lambda i: lead + (jnp.clip(i - lo, 0, cnt - 1), 0))

    args = list(hs) + [g, wi, wo]
    in_specs = ([rows(k, D_MODEL) for k in range(len(hs))]
                + [_const_spec((1, D_MODEL)),
                   pl.BlockSpec((None,) + wi.shape[1:], lambda i: (layer, 0, 0)),
                   pl.BlockSpec((None,) + wo.shape[1:], lambda i: (layer, 0, 0))])
    if pe is not None:
        args += list(pe[0]) + list(pe[1:])
        in_specs += ([rows(k, PLE_DIM, (layer,)) for k in range(len(hs))]
                     + [_const_spec(a.shape) for a in pe[1:]])
    return pl.pallas_call(
        functools.partial(_ffn_kernel, bounds=tuple(bounds), with_pe=pe is not None, final=final),
        grid=(bounds[-1][1],),
        in_specs=in_specs,
        out_specs=[rows(k, D_MODEL) for k in range(len(hs))],
        out_shape=[jax.ShapeDtypeStruct((h.shape[0], D_MODEL), F32) for h in hs],
        compiler_params=_params("arbitrary"),
        name="ffn_pe" if pe is not None else "ffn",
    )(*args)


def _norm_mm_kernel(x_ref, g_ref, w_ref, o_ref):
    o_ref[...] = _bdot(_rms(x_ref[...], g_ref[...]), w_ref[...]).astype(o_ref.dtype)


def _norm_mm(h, g, w, out_dtype, col_tile=None, name="norm_mm"):
    n = h.shape[0]
    tm = _tile(n)
    nout = w.shape[1]
    tn = nout if col_tile is None else col_tile
    return pl.pallas_call(
        _norm_mm_kernel,
        grid=(n // tm, nout // tn),
        in_specs=[pl.BlockSpec((tm, D_MODEL), lambda i, j: (i, 0)), _const_spec((1, D_MODEL)),
                  pl.BlockSpec((D_MODEL, tn), lambda i, j: (0, j))],
        out_specs=pl.BlockSpec((tm, tn), lambda i, j: (i, j)),
        out_shape=jax.ShapeDtypeStruct((n, nout), out_dtype),
        compiler_params=_params("arbitrary", "arbitrary"),
        name=name,
    )(h, g, w)


def _rwkv_layer_kernel(h_ref, sh_ref, s0_ref, nw_ref, mix_ref, wrkv_ref, w0_ref, w1_ref, w2_ref, a0_ref, a1_ref,
                       a2_ref, g1_ref, g2_ref, kk_ref, ka_ref, lnw_ref, lnb_ref, rk_ref, wo_ref, e_ref, et_ref,
                       o_ref, hn_o, st_ref,
                       state, prev, r_s, lw_s, k_s, v_s, kk_s, b_s, g_s, y_s, *, rows, n_sub, nseq, npar):
    c = pl.program_id(1)
    C = WKV_CHUNK
    C2 = 2 * C
    hp = N_HEADS // 2
    n_pairs = npar * hp
    seq_rows = rows // nseq
    short = seq_rows % C != 0
    total = npar * rows

    def load_state(s0, base=0):
        for h in range(N_HEADS):
            lo = (h % 2) * HEAD_DIM
            state[base + h // 2] = jnp.zeros((LANES, LANES), F32)
        for h in range(N_HEADS):
            lo = (h % 2) * HEAD_DIM
            state[base + h // 2, lo:lo + HEAD_DIM, lo:lo + HEAD_DIM] = s0[h]

    def store_state(st, base=0):
        for h in range(N_HEADS):
            lo = (h % 2) * HEAD_DIM
            st[h] = state[base + h // 2, lo:lo + HEAD_DIM, lo:lo + HEAD_DIM]

    if nseq == 1:
        @pl.when(c == 0)
        def _():
            for q in range(npar):
                load_state(s0_ref.at[q], q * hp)
                prev[q:q + 1, :] = sh_ref[q]
    if short:
        for ref in (r_s, lw_s, k_s, v_s, kk_s, b_s):
            ref[rows:, :] = jnp.zeros((ref.shape[0] - rows, D_MODEL), F32)

    nw = nw_ref[...]
    h_in = h_ref[...].reshape(total, D_MODEL) if nseq == 1 else h_ref[...]
    hn = _rms(h_in, nw)
    row = lax.broadcasted_iota(jnp.int32, hn.shape, 0)
    rolled = pltpu.roll(hn, 1, 0)
    if nseq == 1:
        x_prev = rolled
        for q in range(npar):
            x_prev = jnp.where(row == q * rows, prev[q:q + 1, :], x_prev)
            last = hn[(q + 1) * rows - 1:(q + 1) * rows]
            prev[q:q + 1, :] = last
            hn_o[q] = last
    else:
        x_prev = jnp.where(lax.rem(row, seq_rows) == 0, sh_ref[...], rolled)
        hn_o[...] = hn
    xx = x_prev - hn
    mix = mix_ref[...]
    xr, xw, xk, xv, xa, xg = (hn + xx * mix[j:j + 1] for j in range(6))
    r = _bdot(xr, wrkv_ref[0])
    k = _bdot(xk, wrkv_ref[1])
    v = _bdot(xv, wrkv_ref[2])
    wl = w0_ref[...] + _bdot(jnp.tanh(_bdot(xw, w1_ref[...])), w2_ref[...])
    lw_s[0:total] = -EXP_MINUS_HALF * jax.nn.sigmoid(wl)
    a = jax.nn.sigmoid(a0_ref[...] + _bdot(_bdot(xa, a1_ref[...]), a2_ref[...]))
    g_s[...] = _bdot(jax.nn.sigmoid(_bdot(xg, g1_ref[...])), g2_ref[...])
    kkv = k * kk_ref[...]
    kk = kkv / jnp.maximum(jnp.sqrt(_head_sum(kkv * kkv, e_ref, et_ref)), 1e-12)
    r_s[0:total] = r
    k_s[0:total] = k * (1.0 + (a - 1.0) * ka_ref[...])
    v_s[0:total] = v
    kk_s[0:total] = kk
    b_s[0:total] = kk * a

    ri = lax.broadcasted_iota(jnp.int32, (C, C), 0)
    ci = lax.broadcasted_iota(jnp.int32, (C, C), 1)
    tri = (ri >= ci).astype(BF16)
    r2 = lax.broadcasted_iota(jnp.int32, (C2, C2), 0)
    c2 = lax.broadcasted_iota(jnp.int32, (C2, C2), 1)
    strict = r2 > c2
    incl = r2 >= c2
    lane = lax.broadcasted_iota(jnp.int32, (C, LANES), 1)
    head0 = lane < HEAD_DIM

    def hat(x):
        x3 = jnp.stack([x[:, p * LANES:(p + 1) * LANES] for p in range(n_pairs)])
        return jnp.concatenate([jnp.where(head0, x3, 0.0), jnp.where(head0, 0.0, x3)], axis=1)

    def mm(a, b):
        return jnp.einsum("pmk,pkn->pmn", a.astype(BF16), b.astype(BF16), preferred_element_type=F32)

    def mm_nt(a, b):
        return jnp.einsum("pmk,pnk->pmn", a.astype(BF16), b.astype(BF16), preferred_element_type=F32)

    chunk_row = lax.broadcasted_iota(jnp.int32, (C, D_MODEL), 0)

    def chunk(ref, s):
        if short:
            x = ref[pl.ds(pl.multiple_of(s * seq_rows, 8), C), :]
            return jnp.where(chunk_row < seq_rows, x, 0.0)
        return jnp.concatenate([ref[pl.ds(q * rows + s * C, C), :] for q in range(npar)], axis=1)

    def sub_chunk(s, carry):
        lw = chunk(lw_s, s)
        p1 = lw.astype(BF16)
        rem = lw - p1.astype(F32)
        p2 = rem.astype(BF16)
        p3 = (rem - p2.astype(F32)).astype(BF16)
        cw = (jnp.dot(tri, p1, preferred_element_type=F32) + jnp.dot(tri, p2, preferred_element_type=F32)
              + jnp.dot(tri, p3, preferred_element_type=F32))
        cw_end = cw[C - 1:C, :]
        e_neg = jnp.exp(-cw)
        e_end = jnp.exp(cw_end - cw)
        kk_c = chunk(kk_s, s)
        bb = chunk(b_s, s)
        k_c = chunk(k_s, s)
        a_all = kk_c * jnp.exp(cw - lw)
        r_all = chunk(r_s, s) * jnp.exp(cw)
        b_all = bb * e_neg
        k_all = k_c * e_neg
        bd_all = bb * e_end
        kd_all = k_c * e_end
        v_all = chunk(v_s, s)
        decay = jnp.exp(cw_end)
        ar_h = jnp.concatenate([hat(a_all), hat(r_all)], axis=1)
        bk_h = jnp.concatenate([hat(b_all), hat(k_all)], axis=1)
        v_h = hat(v_all)
        st = state[...]
        g = mm_nt(ar_h, bk_h)
        low = jnp.where(strict, g[:, :C2, :C2], 0.0)
        ak = jnp.where(strict, g[:, :C2, C2:], 0.0)
        rbk = jnp.concatenate([jnp.where(incl, g[:, C2:, :C2], 0.0), jnp.where(incl, g[:, C2:, C2:], 0.0)], axis=2)
        ss = mm_nt(ar_h, st)
        x = -(ss[:, :C2] + mm(ak, v_h))
        t = mm(low, jnp.concatenate([low, x], axis=2))
        lp = t[:, :, :C2]
        x = x - t[:, :, C2:]
        for _ in range(max(int(np.ceil(np.log2(min(C, seq_rows)))), 2) - 2):
            t = mm(lp, jnp.concatenate([lp, x], axis=2))
            lp = t[:, :, :C2]
            x = x + t[:, :, C2:]
        x = x + mm(lp, x)
        xv = jnp.concatenate([x, v_h], axis=1)
        y_h = ss[:, C2:] + mm(rbk, xv)
        y = y_h[:, :C] + y_h[:, C:]
        for p in range(n_pairs):
            q, pl_ = divmod(p, hp)
            if short:
                y_s[pl.ds(pl.multiple_of(s * seq_rows, 8), seq_rows), pl_ * LANES:(pl_ + 1) * LANES] = y[p, :seq_rows]
            else:
                y_s[pl.ds(q * rows + s * C, C), pl_ * LANES:(pl_ + 1) * LANES] = y[p]
        bkd_h = jnp.concatenate([hat(bd_all), hat(kd_all)], axis=1)
        xv_t = jnp.stack([xv[p].T for p in range(n_pairs)])
        dec3 = jnp.stack([decay[:, p * LANES:(p + 1) * LANES] for p in range(n_pairs)])
        state[...] = st * dec3 + mm(xv_t, bkd_h)
        return carry

    if nseq == 1:
        for s in range(n_sub):
            sub_chunk(s, 0)
    else:
        def one_sequence(s, carry):
            load_state(s0_ref.at[s])
            sub_chunk(s, carry)
            store_state(st_ref.at[s])
            return carry

        lax.fori_loop(0, nseq, one_sequence, 0)

    y = y_s[0:total]
    inv_n = 1.0 / HEAD_DIM
    mu = _head_sum(y, e_ref, et_ref) * inv_n
    yc = y - mu
    var = _head_sum(yc * yc, e_ref, et_ref) * inv_n
    yn = yc * lax.rsqrt(var + LNX_EPS) * lnw_ref[...] + lnb_ref[...]
    bonus = _head_sum(r_s[0:total] * k_s[0:total] * rk_ref[...], e_ref, et_ref) * v_s[0:total]
    out = h_in + _bdot((yn + bonus) * g_s[...], wo_ref[...])
    o_ref[...] = out.reshape(o_ref.shape)

    if nseq == 1:
        @pl.when(c == pl.num_programs(1) - 1)
        def _():
            for q in range(npar):
                store_state(st_ref.at[q], q * hp)


def _rwkv_layer(h, shift, s0, seq_len, lw):
    n = h.shape[0]
    nb = n // seq_len
    if seq_len % WKV_CHUNK == 0:
        npar = RWKV_PARALLEL_SEQS if nb % RWKV_PARALLEL_SEQS == 0 else 1
        nseq, rows = 1, _tile(seq_len, RWKV_STEP_ROWS // npar)
        steps, n_sub, total = seq_len // rows, rows // WKV_CHUNK, npar * rows
        buf_rows = total
        assert rows % WKV_CHUNK == 0
        h_in = h.reshape(nb, seq_len, D_MODEL)
        row_spec = pl.BlockSpec((npar, rows, D_MODEL), lambda bi, ci: (bi, ci, 0))
        sh_in = shift.reshape(nb, 1, D_MODEL)
        sh_spec = pl.BlockSpec((npar, 1, D_MODEL), lambda bi, ci: (bi, 0, 0))
        hn_shape, hn_spec = jax.ShapeDtypeStruct((nb, 1, D_MODEL), F32), sh_spec
        nst = npar
    else:
        assert seq_len < WKV_CHUNK and seq_len % 8 == 0
        nseq, npar = _tile(nb, RWKV_SHORT_SEQS), 1
        rows, steps, n_sub = nseq * seq_len, 1, 1
        total = rows
        buf_rows = rows + WKV_CHUNK - seq_len
        h_in = h
        row_spec = pl.BlockSpec((rows, D_MODEL), lambda bi, ci: (bi, 0))
        sh_in = jnp.repeat(shift, seq_len, axis=0)
        sh_spec = row_spec
        hn_shape, hn_spec = jax.ShapeDtypeStruct((n, D_MODEL), F32), sh_spec
        nst = nseq
    st_spec = pl.BlockSpec((nst, N_HEADS, HEAD_DIM, HEAD_DIM), lambda bi, ci: (bi, 0, 0, 0))
    consts = [lw["nw"], lw["mix"], lw["wrkv"], lw["w0"], lw["w1"], lw["w2"], lw["a0"], lw["a1"], lw["a2"],
              lw["g1"], lw["g2"], lw["kk"], lw["ka"], lw["lnw"], lw["lnb"], lw["rk"], lw["wo"], lw["e"], lw["et"]]
    seq_buf = pltpu.VMEM((buf_rows, D_MODEL), F32)
    out, hn, st = pl.pallas_call(
        functools.partial(_rwkv_layer_kernel, rows=rows, n_sub=n_sub, nseq=nseq, npar=npar),
        grid=(nb // nst, steps),
        in_specs=[row_spec, sh_spec, st_spec] + [_const_spec(c.shape) for c in consts],
        out_specs=[row_spec, hn_spec, st_spec],
        out_shape=[jax.ShapeDtypeStruct(h_in.shape, F32), hn_shape, jax.ShapeDtypeStruct(s0.shape, F32)],
        scratch_shapes=[pltpu.VMEM((npar * N_HEADS // 2, LANES, LANES), F32), pltpu.VMEM((8, D_MODEL), F32)]
                       + [seq_buf] * 6 + [pltpu.VMEM((total, D_MODEL), F32), seq_buf],
        compiler_params=_params("arbitrary", "arbitrary"),
        name="rwkv_layer",
    )(h_in, sh_in, s0, *consts)
    out = out.reshape(n, D_MODEL)
    shift_out = hn.reshape(nb, D_MODEL) if nseq == 1 else hn.reshape(nb, seq_len, D_MODEL)[:, -1]
    return out, shift_out, st


def _t5_buckets(dist):
    d = np.asarray(dist, dtype=np.int64)
    max_exact = REL_BUCKETS // 2
    large = max_exact + (np.log(np.maximum(d, 1) / max_exact) / np.log(REL_MAX_DIST / max_exact)
                         * (REL_BUCKETS - max_exact)).astype(np.int32)
    large = np.minimum(large, REL_BUCKETS - 1)
    return np.where(d < max_exact, d, large).astype(np.int32)


def _toeplitz(tab, n_rows, n_cols):
    period = tab.shape[-1]
    assert period >= n_rows + n_cols - 1 and n_cols <= period - 1
    lead = tab.shape[:-1]
    flat = jnp.broadcast_to(tab[..., None, :], lead + (n_rows, period)).reshape(lead + (-1,))
    skew = flat[..., :n_rows * (period - 1)].reshape(lead + (n_rows, period - 1))
    return skew[..., :n_cols]


def _band_bias(rel_bias, group):
    win, dil = DILATION_GROUPS[group]
    assert win // dil == Q_BLOCK
    period = 3 * Q_BLOCK - 1
    idx = np.arange(period)
    m = Q_BLOCK - np.where(idx < 2 * Q_BLOCK, idx, idx - period)
    valid = (m >= 0) & (m <= Q_BLOCK)
    buckets = _t5_buckets(dil * np.clip(m, 0, Q_BLOCK))
    tbl = jnp.take(rel_bias[:, group * N_HEADS:(group + 1) * N_HEADS].astype(F32), buckets, axis=0).T
    general = _toeplitz(jnp.where(valid[None], tbl, NEG_INF), Q_BLOCK, 2 * Q_BLOCK)
    first = jnp.where((np.arange(2 * Q_BLOCK) >= Q_BLOCK)[None, None, :], general, NEG_INF)
    return jnp.stack([first, general])


_HEAD_OF_SEG = np.array([4 * (2 * (s // 8) + s % 2) + (s // 2) % 4 for s in range(N_HEADS)])


def _proj_rm_kernel(x_ref, g_ref, w_ref, *refs, natural, lane_ranges, tm):
    scr = refs[-1]
    outs = refs[:-1]
    y = _bdot(_rms(x_ref[...], g_ref[...]), w_ref[...])
    if natural:
        outs[0][...] = y
        outs = outs[1:]
    for c in range(scr.shape[0]):
        scr[c] = y[:, c * LANES:(c + 1) * LANES]
    for gi, (_, dil) in enumerate(DILATION_GROUPS):
        lo, hi = lane_ranges[gi]
        if dil == 1:
            outs[gi][0] = y[:, lo:hi].astype(BF16)
            continue
        for rho in range(dil):
            rows = [scr[c, pl.ds(rho, tm // dil, stride=dil), :] for c in range(lo // LANES, hi // LANES)]
            outs[gi][rho] = jnp.concatenate(rows, axis=1).astype(BF16)


def _proj_rm(h, g, w, nb, seq_len, natural, lane_ranges, name):
    n = h.shape[0]
    tm = _tile(n, PROJ_TILE)
    nout = w.shape[1]
    assert seq_len % tm == 0 and all(tm % (16 * dil) == 0 for _, dil in DILATION_GROUPS)
    tps = seq_len // tm
    out_shape, out_specs = [], []
    if natural:
        out_shape.append(jax.ShapeDtypeStruct((n, nout), F32))
        out_specs.append(pl.BlockSpec((tm, nout), lambda i: (i, 0)))
    for (_, dil), (lo, hi) in zip(DILATION_GROUPS, lane_ranges):
        out_shape.append(jax.ShapeDtypeStruct((nb, dil, seq_len // dil, hi - lo), BF16))
        out_specs.append(pl.BlockSpec((None, dil, tm // dil, hi - lo), lambda i: (i // tps, 0, i % tps, 0)))
    return pl.pallas_call(
        functools.partial(_proj_rm_kernel, natural=natural, lane_ranges=lane_ranges, tm=tm),
        grid=(n // tm,),
        in_specs=[pl.BlockSpec((tm, D_MODEL), lambda i: (i, 0)), _const_spec((1, D_MODEL)), _const_spec(w.shape)],
        out_specs=out_specs,
        out_shape=out_shape,
        scratch_shapes=[pltpu.VMEM((nout // LANES, tm, LANES), F32)],
        compiler_params=_params("arbitrary"),
        name=name,
    )(h, g, w)


def _attn_kernel(q_ref, kv_ref, halo_ref, bias_ref, o_ref, lse_ref, kvbuf, *, nq):
    i = pl.program_id(0)
    dil = q_ref.shape[0]
    kvd = KV_HEADS * HEAD_DIM
    nt = (((1,), (1,)), ((), ()))
    kvbuf[:, :Q_BLOCK, :] = halo_ref[...]
    kvbuf[:, Q_BLOCK:, :] = kv_ref[...]
    lane = lax.broadcasted_iota(jnp.int32, (Q_BLOCK, LANES), 1)
    low_half = lane < HEAD_DIM
    lane_row = lax.broadcasted_iota(jnp.int32, (1, LANES), 1)
    keep_lo = (lane_row < HEAD_DIM).astype(BF16)
    keep_hi = (lane_row >= HEAD_DIM).astype(BF16)

    def block(u, carry):
        rho = lax.div(u, nq)
        j = u - rho * nq
        r0 = pl.multiple_of(j * Q_BLOCK, Q_BLOCK)
        qb = q_ref[rho, pl.ds(r0, Q_BLOCK), :]
        kvb = kvbuf[rho, pl.ds(r0, 2 * Q_BLOCK), :]
        bsel = jnp.where(jnp.logical_and(i == 0, j == 0), 0, 1)
        lse_tile = jnp.zeros((Q_BLOCK, LANES), F32)
        for G in range(KV_HEADS // 2):
            kg = kvb[:, G * LANES:(G + 1) * LANES]
            vg = kvb[:, kvd + G * LANES:kvd + (G + 1) * LANES]
            vcat = jnp.concatenate([vg * keep_lo, vg * keep_hi], axis=0)
            pieces = []
            for r in range(Q_PER_KV):
                qg = qb[:, (G * Q_PER_KV + r) * LANES:(G * Q_PER_KV + r + 1) * LANES]
                pieces += [qg * keep_lo, qg * keep_hi]
            s_all = lax.dot_general(jnp.concatenate(pieces, axis=0), kg, nt, preferred_element_type=F32)
            for r in range(Q_PER_KV):
                grp = G * Q_PER_KV + r
                parts = []
                for half in range(2):
                    s = s_all[(2 * r + half) * Q_BLOCK:(2 * r + half + 1) * Q_BLOCK]
                    s = s + bias_ref[bsel, int(_HEAD_OF_SEG[2 * grp + half])]
                    m = jnp.max(s, axis=-1, keepdims=True)
                    p = jnp.exp(s - m)
                    l = jnp.sum(p, axis=-1, keepdims=True)
                    parts.append((p.astype(BF16), l, m + jnp.log(l)))
                (p0, l0, e0), (p1, l1, e1) = parts
                o = jnp.dot(jnp.concatenate([p0, p1], axis=1), vcat, preferred_element_type=F32)
                o = (o / jnp.where(low_half, l0, l1)).astype(o_ref.dtype)
                o_ref[rho, pl.ds(r0, Q_BLOCK), grp * LANES:(grp + 1) * LANES] = o
                lse_tile = jnp.where(lane == 2 * grp, e0, lse_tile)
                lse_tile = jnp.where(lane == 2 * grp + 1, e1, lse_tile)
        lse_ref[rho, pl.ds(r0, Q_BLOCK), :] = lse_tile
        return carry

    lax.fori_loop(0, dil * nq, block, 0)


def _attn_group(q_rm, kv_rm, bias):
    nb, dil, tsub, _ = q_rm.shape
    kvw = kv_rm.shape[-1]
    rows = ATTN_SPAN // dil
    nq = rows // Q_BLOCK
    assert tsub % rows == 0 and nq >= 1
    span_spec = lambda width: pl.BlockSpec((None, dil, rows, width), lambda i, b: (b, 0, i, 0))
    return pl.pallas_call(
        functools.partial(_attn_kernel, nq=nq),
        grid=(tsub // rows, nb),
        in_specs=[span_spec(D_MODEL), span_spec(kvw),
                  pl.BlockSpec((None, dil, Q_BLOCK, kvw), lambda i, b: (b, 0, jnp.maximum(i * nq - 1, 0), 0)),
                  _const_spec(bias.shape)],
        out_specs=[span_spec(D_MODEL), span_spec(LANES)],
        out_shape=[jax.ShapeDtypeStruct((nb, dil, tsub, D_MODEL), BF16),
                   jax.ShapeDtypeStruct((nb, dil, tsub, LANES), F32)],
        scratch_shapes=[pltpu.VMEM((dil, Q_BLOCK + rows, kvw), BF16)],
        compiler_params=_params("arbitrary", "arbitrary"),
        name=f"attn_d{dil}",
    )(q_rm, kv_rm, kv_rm, bias)


def _attn_out_kernel(o0_ref, o1_ref, o2_ref, l0_ref, l1_ref, l2_ref, h_ref, wo_ref, et_ref, out_ref, *scr, tm):
    outs, lses = [], []
    for gi, (o_ref, l_ref) in enumerate(((o0_ref, l0_ref), (o1_ref, l1_ref), (o2_ref, l2_ref))):
        dil = DILATION_GROUPS[gi][1]
        if dil == 1:
            outs.append(o_ref[0].astype(F32))
            lses.append(l_ref[0])
            continue
        so, sl = scr[2 * gi], scr[2 * gi + 1]
        n_tiles = so.shape[0]
        for rho in range(dil):
            rows = pl.ds(rho, tm // dil, stride=dil)
            for c in range(n_tiles):
                so[c, rows, :] = o_ref[rho, :, c * LANES:(c + 1) * LANES].astype(F32)
            sl[rows, :] = l_ref[rho]
        outs.append(jnp.concatenate([so[c] for c in range(n_tiles)], axis=1))
        lses.append(sl[...])
    l0, l1, l2 = lses
    m = jnp.maximum(jnp.maximum(l0, l1), l2)
    w0, w1, w2 = jnp.exp(l0 - m), jnp.exp(l1 - m), jnp.exp(l2 - m)
    inv = 1.0 / (w0 + w1 + w2)
    et = et_ref[...]
    att = _bdot(w0 * inv, et) * outs[0] + _bdot(w1 * inv, et) * outs[1] + _bdot(w2 * inv, et) * outs[2]
    out_ref[...] = h_ref[...] + _bdot(att, wo_ref[...])


def _attn_out(outs, lses, h, wo, et, seq_len):
    n = h.shape[0]
    tm = _tile(n, PROJ_TILE)
    tps = seq_len // tm
    row_spec = pl.BlockSpec((tm, D_MODEL), lambda i: (i, 0))

    def rm_spec(dil, width):
        return pl.BlockSpec((None, dil, tm // dil, width), lambda i: (i // tps, 0, i % tps, 0))

    dils = [dil for _, dil in DILATION_GROUPS]
    scratch = []
    for _ in dils:
        scratch += [pltpu.VMEM((D_MODEL // LANES, tm, LANES), F32), pltpu.VMEM((tm, LANES), F32)]
    return pl.pallas_call(
        functools.partial(_attn_out_kernel, tm=tm),
        grid=(n // tm,),
        in_specs=[rm_spec(d, D_MODEL) for d in dils] + [rm_spec(d, LANES) for d in dils]
                 + [row_spec, _const_spec(wo.shape), _const_spec(et.shape)],
        out_specs=row_spec,
        out_shape=jax.ShapeDtypeStruct((n, D_MODEL), F32),
        scratch_shapes=scratch,
        compiler_params=_params("arbitrary"),
        name="attn_out",
    )(*outs, *lses, h, wo, et)


def _decode_bias(rel_bias, seq_len, cache_len):
    ncol = cache_len + LANES
    period = seq_len + ncol - 1
    idx = np.arange(period)
    dist = cache_len - np.where(idx < ncol, idx, idx - period)
    buckets = _t5_buckets(np.clip(dist, 0, MAX_WINDOW))
    tabs = []
    for g, (win, dil) in enumerate(DILATION_GROUPS):
        valid = (dist >= 0) & (dist % dil == 0) & (dist <= win)
        tbl = jnp.take(rel_bias[:, g * N_HEADS:(g + 1) * N_HEADS].astype(F32), buckets, axis=0).T
        tabs.append(jnp.where(valid[None], tbl, NEG_INF))
    rows = _toeplitz(jnp.stack(tabs), seq_len, ncol)
    bias = rows.reshape(N_GROUPS * N_HEADS * seq_len, ncol)
    return bias[:, :cache_len], bias[:, cache_len:]


def _attn_decode_kernel(q_ref, cache_ref, kvn_ref, h_ref, bc_ref, bn_ref, wo_ref, out_ref, *, seq_len):
    kvd = KV_HEADS * HEAD_DIM
    nslot = N_GROUPS * N_HEADS
    rows_g = N_HEADS * seq_len
    nt = (((1,), (1,)), ((), ()))
    cache = cache_ref[...]
    kc = cache[:, :kvd].astype(BF16)
    vc = cache[:, kvd:].astype(BF16)
    kvn = kvn_ref[...]
    pad = jnp.zeros((LANES - seq_len, kvd), F32)
    kn = jnp.concatenate([kvn[:, :kvd], pad], axis=0).astype(BF16)
    vn = jnp.concatenate([kvn[:, kvd:], pad], axis=0).astype(BF16)
    lhs = jnp.concatenate([q_ref[:, s * kvd:(s + 1) * kvd] for s in range(nslot)], axis=0).astype(BF16)
    cache_len = kc.shape[0]
    sn = lax.dot_general(lhs, kn, nt, preferred_element_type=F32) + bn_ref[...]
    scs, m = [], None
    for g, (win, _) in enumerate(DILATION_GROUPS):
        c0 = (cache_len - min(win, cache_len)) // LANES * LANES
        rows = slice(g * rows_g, (g + 1) * rows_g)
        sc = lax.dot_general(lhs[rows], kc[c0:], nt, preferred_element_type=F32) + bc_ref[rows, c0:]
        scs.append((sc, c0, rows))
        m_g = jnp.maximum(jnp.max(sc, axis=-1, keepdims=True), jnp.max(sn[rows], axis=-1, keepdims=True))
        m = m_g if m is None else jnp.maximum(m, m_g)
    l = jnp.zeros((rows_g, 1), F32)
    num = jnp.zeros((rows_g, kvd), F32)
    for sc, c0, rows in scs:
        pc = jnp.exp(sc - m)
        pn = jnp.exp(sn[rows] - m)
        l = l + jnp.sum(pc, axis=-1, keepdims=True) + jnp.sum(pn, axis=-1, keepdims=True)
        num = (num + jnp.dot(pc.astype(BF16), vc[c0:], preferred_element_type=F32)
               + jnp.dot(pn.astype(BF16), vn, preferred_element_type=F32))
    row = lax.broadcasted_iota(jnp.int32, (rows_g, kvd), 0)
    lane = lax.broadcasted_iota(jnp.int32, (rows_g, kvd), 1)
    own = (row // (Q_PER_KV * seq_len)) == (lane // HEAD_DIM)
    att = jnp.where(own, num / l, 0.0)
    out = h_ref[...]
    for r in range(Q_PER_KV):
        a_r = att[r * seq_len:(r + 1) * seq_len]
        for c in range(1, KV_HEADS):
            a_r = a_r + att[(c * Q_PER_KV + r) * seq_len:(c * Q_PER_KV + r + 1) * seq_len]
        out = out + _bdot(a_r, wo_ref[r])
    out_ref[...] = out


def _attn_decode(q, cache, kv_new, h, bias_c, bias_n, wo_r, nb, seq_len):
    cache_len = cache.shape[1]
    qw = q.shape[1]
    kvw = 2 * KV_HEADS * HEAD_DIM
    return pl.pallas_call(
        functools.partial(_attn_decode_kernel, seq_len=seq_len),
        grid=(nb,),
        in_specs=[pl.BlockSpec((seq_len, qw), lambda b: (b, 0)),
                  pl.BlockSpec((None, cache_len, kvw), lambda b: (b, 0, 0)),
                  pl.BlockSpec((seq_len, kvw), lambda b: (b, 0)),
                  pl.BlockSpec((seq_len, D_MODEL), lambda b: (b, 0)),
                  _const_spec(bias_c.shape), _const_spec(bias_n.shape), _const_spec(wo_r.shape)],
        out_specs=pl.BlockSpec((seq_len, D_MODEL), lambda b: (b, 0)),
        out_shape=jax.ShapeDtypeStruct((nb * seq_len, D_MODEL), F32),
        compiler_params=_params("arbitrary"),
        name="attn_decode",
    )(q, cache, kv_new, h, bias_c, bias_n, wo_r)


def _prep_weights(norm_w, ffn1_wi, ffn1_wo, ffn2_wi, ffn2_wo, pe_proj, pe_gate,
                  rwkv_mix, rwkv_wrkv, rwkv_wo, rwkv_w0, rwkv_w1, rwkv_w2, rwkv_a0, rwkv_a1, rwkv_a2,
                  rwkv_g1, rwkv_g2, rwkv_kk, rwkv_ka, rwkv_rk, rwkv_lnx_w, rwkv_lnx_b,
                  attn_wq, attn_wo, kv_norm, w_kv, rel_bias, final_norm):
    def row(v):
        return v.reshape(1, -1).astype(F32)

    def pad_cols(w, n):
        return jnp.pad(w, ((0, 0), (0, n - w.shape[1]))).astype(BF16)

    def pad_rows(w, n):
        return jnp.pad(w, ((0, n - w.shape[0]), (0, 0))).astype(BF16)

    head_of_lane = np.arange(D_MODEL) // HEAD_DIM
    e = jnp.asarray(head_of_lane[:, None] == np.arange(LANES)[None, :], BF16)
    et = jnp.asarray(np.arange(LANES)[:, None] == head_of_lane[None, :], BF16)

    depth = norm_w.shape[0]
    layers = []
    for i in range(depth):
        layers.append(dict(
            nw=[row(norm_w[i, j]) for j in range(4)],
            ffn1=(ffn1_wi.astype(F32), ffn1_wo.astype(F32)), ffn2=(ffn2_wi.astype(F32), ffn2_wo.astype(F32)),
            pe_gate=pe_gate[i].astype(BF16), pe_proj=pe_proj[i].astype(BF16)))
    n_a = depth // 2
    rw = []
    for i in range(n_a):
        rw.append(dict(
            nw=row(norm_w[i, 1]), mix=rwkv_mix[i].astype(F32), wrkv=rwkv_wrkv[i].astype(BF16),
            w0=row(rwkv_w0[i]), w1=pad_cols(rwkv_w1[i], LANES), w2=pad_rows(rwkv_w2[i], LANES),
            a0=row(rwkv_a0[i]), a1=pad_cols(rwkv_a1[i], LANES), a2=pad_rows(rwkv_a2[i], LANES),
            g1=pad_cols(rwkv_g1[i], 2 * LANES), g2=pad_rows(rwkv_g2[i], 2 * LANES),
            kk=row(rwkv_kk[i]), ka=row(rwkv_ka[i]), rk=row(rwkv_rk[i]),
            lnw=row(rwkv_lnx_w[i]), lnb=row(rwkv_lnx_b[i]), wo=rwkv_wo[i].astype(BF16), e=e, et=et))
    scale = HEAD_DIM ** -0.5
    at = []
    for j in range(depth - n_a):
        wq = attn_wq[j] * scale
        wo_r = attn_wo[j].reshape(KV_HEADS, Q_PER_KV, HEAD_DIM, D_MODEL).transpose(1, 0, 2, 3)
        wo_r = wo_r.reshape(Q_PER_KV, KV_HEADS * HEAD_DIM, D_MODEL).astype(BF16)
        wq_seg = wq.reshape(D_MODEL, N_GROUPS, N_HEADS, HEAD_DIM)[:, :, _HEAD_OF_SEG].reshape(D_MODEL, -1)
        wo_seg = attn_wo[j].reshape(N_HEADS, HEAD_DIM, D_MODEL)[_HEAD_OF_SEG].reshape(D_MODEL, D_MODEL)
        at.append(dict(wq_seg=wq_seg.astype(BF16), wo_seg=wo_seg.astype(BF16), wo_r=wo_r))
    return dict(layers=layers, rwkv=rw, attn=at, kv_norm=row(kv_norm), w_kv=w_kv.astype(BF16),
                final_norm=row(final_norm), rel_bias=rel_bias, et=et)


def _trunks(streams, w):
    depth = len(w["layers"])
    n_a = depth // 2
    kvw = 2 * KV_HEADS * HEAD_DIM
    st = []
    for sd in streams:
        nb, seq_len, _ = sd["x"].shape
        n = nb * seq_len
        st.append(dict(nb=nb, seq_len=seq_len, n=n, h=sd["x"].reshape(n, D_MODEL).astype(F32),
                       p=sd["p"].reshape(depth, n, PLE_DIM).astype(F32), wkv_out=[], shift_out=[],
                       kv_rows=None, kv_rm=None, **{k: sd[k] for k in ("wkv0", "shift0", "cache")}))

    def ffn(nw, weights, **kw):
        for t, h in zip(st, _ffn([t["h"] for t in st], nw, *weights, **kw)):
            t["h"] = h

    for i in range(depth):
        lw = w["layers"][i]
        if i == n_a:
            for t in st:
                if t["cache"] is None:
                    t["kv_rows"], *t["kv_rm"] = _proj_rm(t["h"], w["kv_norm"], w["w_kv"], t["nb"], t["seq_len"], True,
                                                         [(0, kvw)] * N_GROUPS, "kv_proj")
                else:
                    t["kv_rows"] = _norm_mm(t["h"], w["kv_norm"], w["w_kv"], F32, name="kv_proj")
        ffn(lw["nw"][0], lw["ffn1"], layer=i)
        for t in st:
            nb, seq_len, n, h = t["nb"], t["seq_len"], t["n"], t["h"]
            if i < n_a:
                h, sh, state = _rwkv_layer(h, t["shift0"][i].astype(F32), t["wkv0"][i].astype(F32), seq_len,
                                           w["rwkv"][i])
                t["wkv_out"].append(state)
                t["shift_out"].append(sh)
            elif t["cache"] is None:
                al = w["attn"][i - n_a]
                q_rm = _proj_rm(h, lw["nw"][1], al["wq_seg"], nb, seq_len, False,
                                [(gi * D_MODEL, (gi + 1) * D_MODEL) for gi in range(N_GROUPS)], "q_proj")
                outs, lses = [], []
                for gi in range(N_GROUPS):
                    o, lse = _attn_group(q_rm[gi], t["kv_rm"][gi], _band_bias(w["rel_bias"], gi))
                    outs.append(o)
                    lses.append(lse)
                h = _attn_out(outs, lses, h, al["wo_seg"], w["et"], seq_len)
            else:
                al = w["attn"][i - n_a]
                q = _norm_mm(h, lw["nw"][1], al["wq_seg"], F32, name="q_proj_decode")
                q = q.reshape(n, N_GROUPS, N_HEADS, HEAD_DIM)[:, :, np.argsort(_HEAD_OF_SEG)]
                q = q.reshape(n, N_GROUPS, KV_HEADS, Q_PER_KV, 1, HEAD_DIM)
                q = q * jnp.eye(KV_HEADS, dtype=F32).reshape(1, 1, KV_HEADS, 1, KV_HEADS, 1)
                q = q.reshape(n, N_GROUPS * N_HEADS * KV_HEADS * HEAD_DIM)
                bias_c, bias_n = _decode_bias(w["rel_bias"], seq_len, t["cache"].shape[1])
                h = _attn_decode(q, t["cache"], t["kv_rows"], h, bias_c, bias_n, al["wo_r"], nb, seq_len)
            t["h"] = h
        pe = ([t["p"] for t in st], lw["nw"][3], lw["pe_gate"], lw["pe_proj"], w["final_norm"])
        ffn(lw["nw"][2], lw["ffn2"], pe=pe, layer=i, final=(i == depth - 1))
    results = []
    for t, sd in zip(st, streams):
        dt = sd["x"].dtype
        y = t["h"].reshape(t["nb"], t["seq_len"], D_MODEL).astype(dt)
        kv_rows = t["kv_rows"].reshape(t["nb"], t["seq_len"], 2, KV_HEADS, HEAD_DIM).astype(dt)
        results.append((y, jnp.stack(t["wkv_out"]).astype(dt), jnp.stack(t["shift_out"]).astype(dt), kv_rows))
    return results


def _trunk(x, p, wkv0, shift0, cache, w):
    return _trunks([dict(x=x, p=p, wkv0=wkv0, shift0=shift0, cache=cache)], w)[0]


def kernel(x_prompt, x_sample, state_wkv, state_shift, cache_kv, p_prompt, p_sample, norm_w, ffn1_wi, ffn1_wo, ffn2_wi, ffn2_wo, pe_proj, pe_gate, rwkv_mix, rwkv_wrkv, rwkv_wo, rwkv_w0, rwkv_w1, rwkv_w2, rwkv_a0, rwkv_a1, rwkv_a2, rwkv_g1, rwkv_g2, rwkv_kk, rwkv_ka, rwkv_rk, rwkv_lnx_w, rwkv_lnx_b, attn_wq, attn_wo, kv_norm, w_kv, rel_bias, final_norm):
    w = _prep_weights(norm_w, ffn1_wi, ffn1_wo, ffn2_wi, ffn2_wo, pe_proj, pe_gate,
                      rwkv_mix, rwkv_wrkv, rwkv_wo, rwkv_w0, rwkv_w1, rwkv_w2, rwkv_a0, rwkv_a1, rwkv_a2,
                      rwkv_g1, rwkv_g2, rwkv_kk, rwkv_ka, rwkv_rk, rwkv_lnx_w, rwkv_lnx_b,
                      attn_wq, attn_wo, kv_norm, w_kv, rel_bias, final_norm)
    n_a = norm_w.shape[0] // 2
    nb, seq_len, _ = x_prompt.shape
    wkv0 = jnp.zeros((n_a, nb, N_HEADS, HEAD_DIM, HEAD_DIM), F32)
    shift0 = jnp.zeros((n_a, nb, D_MODEL), x_prompt.dtype)
    cache = cache_kv.reshape(cache_kv.shape[0], cache_kv.shape[1], 2 * KV_HEADS * HEAD_DIM).astype(F32)
    (y_p, wkv_p, shift_p, kv_p), (y_s, wkv_s, shift_s, kv_s) = _trunks(
        [dict(x=x_prompt, p=p_prompt, wkv0=wkv0, shift0=shift0, cache=None),
         dict(x=x_sample, p=p_sample, wkv0=state_wkv, shift0=state_shift, cache=cache)], w)
    kv_prompt = kv_p[:, seq_len - min(MAX_WINDOW, seq_len):]
    return (y_p, y_s, wkv_p, shift_p, kv_prompt, wkv_s, shift_s, kv_s)
```

```python
import functools

import numpy as np
import jax
import jax.numpy as jnp
from jax import lax
from jax.experimental import pallas as pl
from jax.experimental.pallas import tpu as pltpu

F32 = jnp.float32
BF16 = jnp.bfloat16

D_MODEL = 1024
D_FF = 2816
PLE_DIM = 256
RMS_EPS = 1e-6
HEAD_DIM = 64
N_HEADS = D_MODEL // HEAD_DIM
LNX_EPS = 64e-5
KV_HEADS = 4
Q_PER_KV = N_HEADS // KV_HEADS
DILATION_GROUPS = ((128, 1), (512, 4), (2048, 16))
N_GROUPS = len(DILATION_GROUPS)
MAX_WINDOW = 2048
REL_BUCKETS = 32
REL_MAX_DIST = 2048
NEG_INF = -1e30

LANES = 128
FFN_CHUNK = 256
N_FFN_CHUNKS = D_FF // FFN_CHUNK
TOKEN_TILE = 256
FFN_TILE = 512
PROJ_TILE = 1024
WKV_CHUNK = 64
RWKV_SHORT_SEQS = 8
RWKV_PARALLEL_SEQS = 2
RWKV_STEP_ROWS = 512
Q_BLOCK = 128
ATTN_SPAN = 2048
VMEM_LIMIT = 56 * 1024 * 1024
EXP_MINUS_HALF = 0.6065306597126334


def _params(*sem):
    return pltpu.CompilerParams(dimension_semantics=sem, vmem_limit_bytes=VMEM_LIMIT)


def _const_spec(shape):
    return pl.BlockSpec(shape, lambda *_: (0,) * len(shape))


def _tile(n, pref=TOKEN_TILE):
    t = min(n, pref)
    while n % t:
        t -= 8
    return t


def _rms(x, g):
    return x * lax.rsqrt(jnp.mean(x * x, axis=-1, keepdims=True) + RMS_EPS) * g


def _bdot(a, b):
    return jnp.dot(a.astype(BF16), b, preferred_element_type=F32)


def _head_sum(x, e_ref, et_ref):
    return _bdot(_bdot(x, e_ref[...]), et_ref[...])


def _ffn_kernel(*refs, bounds, with_pe, final):
    ns = len(bounds)
    xs, (g_ref, wi_ref, wo_ref), rest = refs[:ns], refs[ns:ns + 3], refs[ns + 3:]
    if with_pe:
        ps, (gp_ref, wgate_ref, wproj_ref, gf_ref), outs = rest[:ns], rest[ns:ns + 4], rest[ns + 4:]
    else:
        outs = rest

    def body(k):
        x = xs[k][...]
        xn = _rms(x, g_ref[...]).astype(BF16)
        acc = jnp.zeros_like(x)
        for j in range(N_FFN_CHUNKS):
            lo, hi = j * FFN_CHUNK, (j + 1) * FFN_CHUNK
            gate = jnp.dot(xn, wi_ref[:, lo:hi].astype(BF16), preferred_element_type=F32)
            up = jnp.dot(xn, wi_ref[:, D_FF + lo:D_FF + hi].astype(BF16), preferred_element_type=F32)
            act = (gate * jax.nn.sigmoid(gate) * up).astype(BF16)
            acc = acc + jnp.dot(act, wo_ref[lo:hi, :].astype(BF16), preferred_element_type=F32)
        y = x + 0.5 * acc
        if with_pe:
            gate = jax.nn.sigmoid(_bdot(_rms(y, gp_ref[...]), wgate_ref[...]))
            y = y + gate * _bdot(ps[k][...], wproj_ref[...])
            if final:
                y = _rms(y, gf_ref[...])
        outs[k][...] = y

    if ns == 1:
        body(0)
    else:
        i = pl.program_id(0)
        for k, (lo, hi) in enumerate(bounds):
            pl.when(jnp.logical_and(i >= lo, i < hi))(functools.partial(body, k))


def _ffn(hs, g, wi, wo, pe=None, layer=0, final=False):
    tiles = [_tile(h.shape[0], FFN_TILE) for h in hs]
    counts = [h.shape[0] // t for h, t in zip(hs, tiles)]
    bounds = [(sum(counts[:k]), sum(counts[:k + 1])) for k in range(len(hs))]

    def rows(k, width, lead=()):
        lo, cnt = bounds[k][0], counts[k]
        return pl.BlockSpec(tuple(None for _ in lead) + (tiles[k], width),
                            lambda i: lead + (jnp.clip(i - lo, 0, cnt - 1), 0))

    args = list(hs) + [g, wi, wo]
    in_specs = ([rows(k, D_MODEL) for k in range(len(hs))]
                + [_const_spec((1, D_MODEL)),
                   pl.BlockSpec((None,) + wi.shape[1:], lambda i: (layer, 0, 0)),
                   pl.BlockSpec((None,) + wo.shape[1:], lambda i: (layer, 0, 0))])
    if pe is not None:
        args += list(pe[0]) + list(pe[1:])
        in_specs += ([rows(k, PLE_DIM, (layer,)) for k in range(len(hs))]
                     + [_const_spec(a.shape) for a in pe[1:]])
    return pl.pallas_call(
        functools.partial(_ffn_kernel, bounds=tuple(bounds), with_pe=pe is not None, final=final),
        grid=(bounds[-1][1],),
        in_specs=in_specs,
        out_specs=[rows(k, D_MODEL) for k in range(len(hs))],
        out_shape=[jax.ShapeDtypeStruct((h.shape[0], D_MODEL), F32) for h in hs],
        compiler_params=_params("arbitrary"),
        name="ffn_pe" if pe is not None else "ffn",
    )(*args)


def _norm_mm_kernel(x_ref, g_ref, w_ref, o_ref):
    o_ref[...] = _bdot(_rms(x_ref[...], g_ref[...]), w_ref[...]).astype(o_ref.dtype)


def _norm_mm(h, g, w, out_dtype, col_tile=None, name="norm_mm"):
    n = h.shape[0]
    tm = _tile(n)
    nout = w.shape[1]
    tn = nout if col_tile is None else col_tile
    return pl.pallas_call(
        _norm_mm_kernel,
        grid=(n // tm, nout // tn),
        in_specs=[pl.BlockSpec((tm, D_MODEL), lambda i, j: (i, 0)), _const_spec((1, D_MODEL)),
                  pl.BlockSpec((D_MODEL, tn), lambda i, j: (0, j))],
        out_specs=pl.BlockSpec((tm, tn), lambda i, j: (i, j)),
        out_shape=jax.ShapeDtypeStruct((n, nout), out_dtype),
        compiler_params=_params("arbitrary", "arbitrary"),
        name=name,
    )(h, g, w)


def _rwkv_layer_kernel(h_ref, sh_ref, s0_ref, nw_ref, mix_ref, wrkv_ref, w0_ref, w1_ref, w2_ref, a0_ref, a1_ref,
                       a2_ref, g1_ref, g2_ref, kk_ref, ka_ref, lnw_ref, lnb_ref, rk_ref, wo_ref, e_ref, et_ref,
                       o_ref, hn_o, st_ref,
                       state, prev, r_s, lw_s, k_s, v_s, kk_s, b_s, g_s, y_s, *, rows, n_sub, nseq, npar):
    c = pl.program_id(1)
    C = WKV_CHUNK
    C2 = 2 * C
    hp = N_HEADS // 2
    n_pairs = npar * hp
    seq_rows = rows // nseq
    short = seq_rows % C != 0
    total = npar * rows

    def load_state(s0, base=0):
        for h in range(N_HEADS):
            lo = (h % 2) * HEAD_DIM
            state[base + h // 2] = jnp.zeros((LANES, LANES), F32)
        for h in range(N_HEADS):
            lo = (h % 2) * HEAD_DIM
            state[base + h // 2, lo:lo + HEAD_DIM, lo:lo + HEAD_DIM] = s0[h]

    def store_state(st, base=0):
        for h in range(N_HEADS):
            lo = (h % 2) * HEAD_DIM
            st[h] = state[base + h // 2, lo:lo + HEAD_DIM, lo:lo + HEAD_DIM]

    if nseq == 1:
        @pl.when(c == 0)
        def _():
            for q in range(npar):
                load_state(s0_ref.at[q], q * hp)
                prev[q:q + 1, :] = sh_ref[q]
    if short:
        for ref in (r_s, lw_s, k_s, v_s, kk_s, b_s):
            ref[rows:, :] = jnp.zeros((ref.shape[0] - rows, D_MODEL), F32)

    nw = nw_ref[...]
    h_in = h_ref[...].reshape(total, D_MODEL) if nseq == 1 else h_ref[...]
    hn = _rms(h_in, nw)
    row = lax.broadcasted_iota(jnp.int32, hn.shape, 0)
    rolled = pltpu.roll(hn, 1, 0)
    if nseq == 1:
        x_prev = rolled
        for q in range(npar):
            x_prev = jnp.where(row == q * rows, prev[q:q + 1, :], x_prev)
            last = hn[(q + 1) * rows - 1:(q + 1) * rows]
            prev[q:q + 1, :] = last
            hn_o[q] = last
    else:
        x_prev = jnp.where(lax.rem(row, seq_rows) == 0, sh_ref[...], rolled)
        hn_o[...] = hn
    xx = x_prev - hn
    mix = mix_ref[...]
    xr, xw, xk, xv, xa, xg = (hn + xx * mix[j:j + 1] for j in range(6))
    r = _bdot(xr, wrkv_ref[0])
    k = _bdot(xk, wrkv_ref[1])
    v = _bdot(xv, wrkv_ref[2])
    wl = w0_ref[...] + _bdot(jnp.tanh(_bdot(xw, w1_ref[...])), w2_ref[...])
    lw_s[0:total] = -EXP_MINUS_HALF * jax.nn.sigmoid(wl)
    a = jax.nn.sigmoid(a0_ref[...] + _bdot(_bdot(xa, a1_ref[...]), a2_ref[...]))
    g_s[...] = _bdot(jax.nn.sigmoid(_bdot(xg, g1_ref[...])), g2_ref[...])
    kkv = k * kk_ref[...]
    kk = kkv / jnp.maximum(jnp.sqrt(_head_sum(kkv * kkv, e_ref, et_ref)), 1e-12)
    r_s[0:total] = r
    k_s[0:total] = k * (1.0 + (a - 1.0) * ka_ref[...])
    v_s[0:total] = v
    kk_s[0:total] = kk
    b_s[0:total] = kk * a

    ri = lax.broadcasted_iota(jnp.int32, (C, C), 0)
    ci = lax.broadcasted_iota(jnp.int32, (C, C), 1)
    tri = (ri >= ci).astype(BF16)
    r2 = lax.broadcasted_iota(jnp.int32, (C2, C2), 0)
    c2 = lax.broadcasted_iota(jnp.int32, (C2, C2), 1)
    strict = r2 > c2
    incl = r2 >= c2
    lane = lax.broadcasted_iota(jnp.int32, (C, LANES), 1)
    head0 = lane < HEAD_DIM

    def hat(x):
        x3 = jnp.stack([x[:, p * LANES:(p + 1) * LANES] for p in range(n_pairs)])
        return jnp.concatenate([jnp.where(head0, x3, 0.0), jnp.where(head0, 0.0, x3)], axis=1)

    def mm(a, b):
        return jnp.einsum("pmk,pkn->pmn", a.astype(BF16), b.astype(BF16), preferred_element_type=F32)

    def mm_nt(a, b):
        return jnp.einsum("pmk,pnk->pmn", a.astype(BF16), b.astype(BF16), preferred_element_type=F32)

    chunk_row = lax.broadcasted_iota(jnp.int32, (C, D_MODEL), 0)

    def chunk(ref, s):
        if short:
            x = ref[pl.ds(pl.multiple_of(s * seq_rows, 8), C), :]
            return jnp.where(chunk_row < seq_rows, x, 0.0)
        return jnp.concatenate([ref[pl.ds(q * rows + s * C, C), :] for q in range(npar)], axis=1)

    def sub_chunk(s, carry):
        lw = chunk(lw_s, s)
        p1 = lw.astype(BF16)
        rem = lw - p1.astype(F32)
        p2 = rem.astype(BF16)
        p3 = (rem - p2.astype(F32)).astype(BF16)
        cw = (jnp.dot(tri, p1, preferred_element_type=F32) + jnp.dot(tri, p2, preferred_element_type=F32)
              + jnp.dot(tri, p3, preferred_element_type=F32))
        cw_end = cw[C - 1:C, :]
        e_neg = jnp.exp(-cw)
        e_end = jnp.exp(cw_end - cw)
        kk_c = chunk(kk_s, s)
        bb = chunk(b_s, s)
        k_c = chunk(k_s, s)
        a_all = kk_c * jnp.exp(cw - lw)
        r_all = chunk(r_s, s) * jnp.exp(cw)
        b_all = bb * e_neg
        k_all = k_c * e_neg
        bd_all = bb * e_end
        kd_all = k_c * e_end
        v_all = chunk(v_s, s)
        decay = jnp.exp(cw_end)
        ar_h = jnp.concatenate([hat(a_all), hat(r_all)], axis=1)
        bk_h = jnp.concatenate([hat(b_all), hat(k_all)], axis=1)
        v_h = hat(v_all)
        st = state[...]
        g = mm_nt(ar_h, bk_h)
        low = jnp.where(strict, g[:, :C2, :C2], 0.0)
        ak = jnp.where(strict, g[:, :C2, C2:], 0.0)
        rbk = jnp.concatenate([jnp.where(incl, g[:, C2:, :C2], 0.0), jnp.where(incl, g[:, C2:, C2:], 0.0)], axis=2)
        ss = mm_nt(ar_h, st)
        x = -(ss[:, :C2] + mm(ak, v_h))
        t = mm(low, jnp.concatenate([low, x], axis=2))
        lp = t[:, :, :C2]
        x = x - t[:, :, C2:]
        for _ in range(max(int(np.ceil(np.log2(min(C, seq_rows)))), 2) - 2):
            t = mm(lp, jnp.concatenate([lp, x], axis=2))
            lp = t[:, :, :C2]
            x = x + t[:, :, C2:]
        x = x + mm(lp, x)
        xv = jnp.concatenate([x, v_h], axis=1)
        y_h = ss[:, C2:] + mm(rbk, xv)
        y = y_h[:, :C] + y_h[:, C:]
        for p in range(n_pairs):
            q, pl_ = divmod(p, hp)
            if short:
                y_s[pl.ds(pl.multiple_of(s * seq_rows, 8), seq_rows), pl_ * LANES:(pl_ + 1) * LANES] = y[p, :seq_rows]
            else:
                y_s[pl.ds(q * rows + s * C, C), pl_ * LANES:(pl_ + 1) * LANES] = y[p]
        bkd_h = jnp.concatenate([hat(bd_all), hat(kd_all)], axis=1)
        xv_t = jnp.stack([xv[p].T for p in range(n_pairs)])
        dec3 = jnp.stack([decay[:, p * LANES:(p + 1) * LANES] for p in range(n_pairs)])
        state[...] = st * dec3 + mm(xv_t, bkd_h)
        return carry

    if nseq == 1:
        for s in range(n_sub):
            sub_chunk(s, 0)
    else:
        def one_sequence(s, carry):
            load_state(s0_ref.at[s])
            sub_chunk(s, carry)
            store_state(st_ref.at[s])
            return carry

        lax.fori_loop(0, nseq, one_sequence, 0)

    y = y_s[0:total]
    inv_n = 1.0 / HEAD_DIM
    mu = _head_sum(y, e_ref, et_ref) * inv_n
    yc = y - mu
    var = _head_sum(yc * yc, e_ref, et_ref) * inv_n
    yn = yc * lax.rsqrt(var + LNX_EPS) * lnw_ref[...] + lnb_ref[...]
    bonus = _head_sum(r_s[0:total] * k_s[0:total] * rk_ref[...], e_ref, et_ref) * v_s[0:total]
    out = h_in + _bdot((yn + bonus) * g_s[...], wo_ref[...])
    o_ref[...] = out.reshape(o_ref.shape)

    if nseq == 1:
        @pl.when(c == pl.num_programs(1) - 1)
        def _():
            for q in range(npar):
                store_state(st_ref.at[q], q * hp)


def _rwkv_layer(h, shift, s0, seq_len, lw):
    n = h.shape[0]
    nb = n // seq_len
    if seq_len % WKV_CHUNK == 0:
        npar = RWKV_PARALLEL_SEQS if nb % RWKV_PARALLEL_SEQS == 0 else 1
        nseq, rows = 1, _tile(seq_len, RWKV_STEP_ROWS // npar)
        steps, n_sub, total = seq_len // rows, rows // WKV_CHUNK, npar * rows
        buf_rows = total
        assert rows % WKV_CHUNK == 0
        h_in = h.reshape(nb, seq_len, D_MODEL)
        row_spec = pl.BlockSpec((npar, rows, D_MODEL), lambda bi, ci: (bi, ci, 0))
        sh_in = shift.reshape(nb, 1, D_MODEL)
        sh_spec = pl.BlockSpec((npar, 1, D_MODEL), lambda bi, ci: (bi, 0, 0))
        hn_shape, hn_spec = jax.ShapeDtypeStruct((nb, 1, D_MODEL), F32), sh_spec
        nst = npar
    else:
        assert seq_len < WKV_CHUNK and seq_len % 8 == 0
        nseq, npar = _tile(nb, RWKV_SHORT_SEQS), 1
        rows, steps, n_sub = nseq * seq_len, 1, 1
        total = rows
        buf_rows = rows + WKV_CHUNK - seq_len
        h_in = h
        row_spec = pl.BlockSpec((rows, D_MODEL), lambda bi, ci: (bi, 0))
        sh_in = jnp.repeat(shift, seq_len, axis=0)
        sh_spec = row_spec
        hn_shape, hn_spec = jax.ShapeDtypeStruct((n, D_MODEL), F32), sh_spec
        nst = nseq
    st_spec = pl.BlockSpec((nst, N_HEADS, HEAD_DIM, HEAD_DIM), lambda bi, ci: (bi, 0, 0, 0))
    consts = [lw["nw"], lw["mix"], lw["wrkv"], lw["w0"], lw["w1"], lw["w2"], lw["a0"], lw["a1"], lw["a2"],
              lw["g1"], lw["g2"], lw["kk"], lw["ka"], lw["lnw"], lw["lnb"], lw["rk"], lw["wo"], lw["e"], lw["et"]]
    seq_buf = pltpu.VMEM((buf_rows, D_MODEL), F32)
    out, hn, st = pl.pallas_call(
        functools.partial(_rwkv_layer_kernel, rows=rows, n_sub=n_sub, nseq=nseq, npar=npar),
        grid=(nb // nst, steps),
        in_specs=[row_spec, sh_spec, st_spec] + [_const_spec(c.shape) for c in consts],
        out_specs=[row_spec, hn_spec, st_spec],
        out_shape=[jax.ShapeDtypeStruct(h_in.shape, F32), hn_shape, jax.ShapeDtypeStruct(s0.shape, F32)],
        scratch_shapes=[pltpu.VMEM((npar * N_HEADS // 2, LANES, LANES), F32), pltpu.VMEM((8, D_MODEL), F32)]
                       + [seq_buf] * 6 + [pltpu.VMEM((total, D_MODEL), F32), seq_buf],
        compiler_params=_params("arbitrary", "arbitrary"),
        name="rwkv_layer",
    )(h_in, sh_in, s0, *consts)
    out = out.reshape(n, D_MODEL)
    shift_out = hn.reshape(nb, D_MODEL) if nseq == 1 else hn.reshape(nb, seq_len, D_MODEL)[:, -1]
    return out, shift_out, st


def _t5_buckets(dist):
    d = np.asarray(dist, dtype=np.int64)
    max_exact = REL_BUCKETS // 2
    large = max_exact + (np.log(np.maximum(d, 1) / max_exact) / np.log(REL_MAX_DIST / max_exact)
                         * (REL_BUCKETS - max_exact)).astype(np.int32)
    large = np.minimum(large, REL_BUCKETS - 1)
    return np.where(d < max_exact, d, large).astype(np.int32)


def _toeplitz(tab, n_rows, n_cols):
    period = tab.shape[-1]
    assert period >= n_rows + n_cols - 1 and n_cols <= period - 1
    lead = tab.shape[:-1]
    flat = jnp.broadcast_to(tab[..., None, :], lead + (n_rows, period)).reshape(lead + (-1,))
    skew = flat[..., :n_rows * (period - 1)].reshape(lead + (n_rows, period - 1))
    return skew[..., :n_cols]


def _band_bias(rel_bias, group):
    win, dil = DILATION_GROUPS[group]
    assert win // dil == Q_BLOCK
    period = 3 * Q_BLOCK - 1
    idx = np.arange(period)
    m = Q_BLOCK - np.where(idx < 2 * Q_BLOCK, idx, idx - period)
    valid = (m >= 0) & (m <= Q_BLOCK)
    buckets = _t5_buckets(dil * np.clip(m, 0, Q_BLOCK))
    tbl = jnp.take(rel_bias[:, group * N_HEADS:(group + 1) * N_HEADS].astype(F32), buckets, axis=0).T
    general = _toeplitz(jnp.where(valid[None], tbl, NEG_INF), Q_BLOCK, 2 * Q_BLOCK)
    first = jnp.where((np.arange(2 * Q_BLOCK) >= Q_BLOCK)[None, None, :], general, NEG_INF)
    return jnp.stack([first, general])


_HEAD_OF_SEG = np.array([4 * (2 * (s // 8) + s % 2) + (s // 2) % 4 for s in range(N_HEADS)])


def _proj_rm_kernel(x_ref, g_ref, w_ref, *refs, natural, lane_ranges, tm):
    scr = refs[-1]
    outs = refs[:-1]
    y = _bdot(_rms(x_ref[...], g_ref[...]), w_ref[...])
    if natural:
        outs[0][...] = y
        outs = outs[1:]
    for c in range(scr.shape[0]):
        scr[c] = y[:, c * LANES:(c + 1) * LANES]
    for gi, (_, dil) in enumerate(DILATION_GROUPS):
        lo, hi = lane_ranges[gi]
        if dil == 1:
            outs[gi][0] = y[:, lo:hi].astype(BF16)
            continue
        for rho in range(dil):
            rows = [scr[c, pl.ds(rho, tm // dil, stride=dil), :] for c in range(lo // LANES, hi // LANES)]
            outs[gi][rho] = jnp.concatenate(rows, axis=1).astype(BF16)


def _proj_rm(h, g, w, nb, seq_len, natural, lane_ranges, name):
    n = h.shape[0]
    tm = _tile(n, PROJ_TILE)
    nout = w.shape[1]
    assert seq_len % tm == 0 and all(tm % (16 * dil) == 0 for _, dil in DILATION_GROUPS)
    tps = seq_len // tm
    out_shape, out_specs = [], []
    if natural:
        out_shape.append(jax.ShapeDtypeStruct((n, nout), F32))
        out_specs.append(pl.BlockSpec((tm, nout), lambda i: (i, 0)))
    for (_, dil), (lo, hi) in zip(DILATION_GROUPS, lane_ranges):
        out_shape.append(jax.ShapeDtypeStruct((nb, dil, seq_len // dil, hi - lo), BF16))
        out_specs.append(pl.BlockSpec((None, dil, tm // dil, hi - lo), lambda i: (i // tps, 0, i % tps, 0)))
    return pl.pallas_call(
        functools.partial(_proj_rm_kernel, natural=natural, lane_ranges=lane_ranges, tm=tm),
        grid=(n // tm,),
        in_specs=[pl.BlockSpec((tm, D_MODEL), lambda i: (i, 0)), _const_spec((1, D_MODEL)), _const_spec(w.shape)],
        out_specs=out_specs,
        out_shape=out_shape,
        scratch_shapes=[pltpu.VMEM((nout // LANES, tm, LANES), F32)],
        compiler_params=_params("arbitrary"),
        name=name,
    )(h, g, w)


def _attn_kernel(q_ref, kv_ref, halo_ref, bias_ref, o_ref, lse_ref, kvbuf, *, nq):
    i = pl.program_id(0)
    dil = q_ref.shape[0]
    kvd = KV_HEADS * HEAD_DIM
    nt = (((1,), (1,)), ((), ()))
    kvbuf[:, :Q_BLOCK, :] = halo_ref[...]
    kvbuf[:, Q_BLOCK:, :] = kv_ref[...]
    lane = lax.broadcasted_iota(jnp.int32, (Q_BLOCK, LANES), 1)
    low_half = lane < HEAD_DIM
    lane_row = lax.broadcasted_iota(jnp.int32, (1, LANES), 1)
    keep_lo = (lane_row < HEAD_DIM).astype(BF16)
    keep_hi = (lane_row >= HEAD_DIM).astype(BF16)

    def block(u, carry):
        rho = lax.div(u, nq)
        j = u - rho * nq
        r0 = pl.multiple_of(j * Q_BLOCK, Q_BLOCK)
        qb = q_ref[rho, pl.ds(r0, Q_BLOCK), :]
        kvb = kvbuf[rho, pl.ds(r0, 2 * Q_BLOCK), :]
        bsel = jnp.where(jnp.logical_and(i == 0, j == 0), 0, 1)
        lse_tile = jnp.zeros((Q_BLOCK, LANES), F32)
        for G in range(KV_HEADS // 2):
            kg = kvb[:, G * LANES:(G + 1) * LANES]
            vg = kvb[:, kvd + G * LANES:kvd + (G + 1) * LANES]
            vcat = jnp.concatenate([vg * keep_lo, vg * keep_hi], axis=0)
            pieces = []
            for r in range(Q_PER_KV):
                qg = qb[:, (G * Q_PER_KV + r) * LANES:(G * Q_PER_KV + r + 1) * LANES]
                pieces += [qg * keep_lo, qg * keep_hi]
            s_all = lax.dot_general(jnp.concatenate(pieces, axis=0), kg, nt, preferred_element_type=F32)
            for r in range(Q_PER_KV):
                grp = G * Q_PER_KV + r
                parts = []
                for half in range(2):
                    s = s_all[(2 * r + half) * Q_BLOCK:(2 * r + half + 1) * Q_BLOCK]
                    s = s + bias_ref[bsel, int(_HEAD_OF_SEG[2 * grp + half])]
                    m = jnp.max(s, axis=-1, keepdims=True)
                    p = jnp.exp(s - m)
                    l = jnp.sum(p, axis=-1, keepdims=True)
                    parts.append((p.astype(BF16), l, m + jnp.log(l)))
                (p0, l0, e0), (p1, l1, e1) = parts
                o = jnp.dot(jnp.concatenate([p0, p1], axis=1), vcat, preferred_element_type=F32)
                o = (o / jnp.where(low_half, l0, l1)).astype(o_ref.dtype)
                o_ref[rho, pl.ds(r0, Q_BLOCK), grp * LANES:(grp + 1) * LANES] = o
                lse_tile = jnp.where(lane == 2 * grp, e0, lse_tile)
                lse_tile = jnp.where(lane == 2 * grp + 1, e1, lse_tile)
        lse_ref[rho, pl.ds(r0, Q_BLOCK), :] = lse_tile
        return carry

    lax.fori_loop(0, dil * nq, block, 0)


def _attn_group(q_rm, kv_rm, bias):
    nb, dil, tsub, _ = q_rm.shape
    kvw = kv_rm.shape[-1]
    rows = ATTN_SPAN // dil
    nq = rows // Q_BLOCK
    assert tsub % rows == 0 and nq >= 1
    span_spec = lambda width: pl.BlockSpec((None, dil, rows, width), lambda i, b: (b, 0, i, 0))
    return pl.pallas_call(
        functools.partial(_attn_kernel, nq=nq),
        grid=(tsub // rows, nb),
        in_specs=[span_spec(D_MODEL), span_spec(kvw),
                  pl.BlockSpec((None, dil, Q_BLOCK, kvw), lambda i, b: (b, 0, jnp.maximum(i * nq - 1, 0), 0)),
                  _const_spec(bias.shape)],
        out_specs=[span_spec(D_MODEL), span_spec(LANES)],
        out_shape=[jax.ShapeDtypeStruct((nb, dil, tsub, D_MODEL), BF16),
                   jax.ShapeDtypeStruct((nb, dil, tsub, LANES), F32)],
        scratch_shapes=[pltpu.VMEM((dil, Q_BLOCK + rows, kvw), BF16)],
        compiler_params=_params("arbitrary", "arbitrary"),
        name=f"attn_d{dil}",
    )(q_rm, kv_rm, kv_rm, bias)


def _attn_out_kernel(o0_ref, o1_ref, o2_ref, l0_ref, l1_ref, l2_ref, h_ref, wo_ref, et_ref, out_ref, *scr, tm):
    outs, lses = [], []
    for gi, (o_ref, l_ref) in enumerate(((o0_ref, l0_ref), (o1_ref, l1_ref), (o2_ref, l2_ref))):
        dil = DILATION_GROUPS[gi][1]
        if dil == 1:
            outs.append(o_ref[0].astype(F32))
            lses.append(l_ref[0])
            continue
        so, sl = scr[2 * gi], scr[2 * gi + 1]
        n_tiles = so.shape[0]
        for rho in range(dil):
            rows = pl.ds(rho, tm // dil, stride=dil)
            for c in range(n_tiles):
                so[c, rows, :] = o_ref[rho, :, c * LANES:(c + 1) * LANES].astype(F32)
            sl[rows, :] = l_ref[rho]
        outs.append(jnp.concatenate([so[c] for c in range(n_tiles)], axis=1))
        lses.append(sl[...])
    l0, l1, l2 = lses
    m = jnp.maximum(jnp.maximum(l0, l1), l2)
    w0, w1, w2 = jnp.exp(l0 - m), jnp.exp(l1 - m), jnp.exp(l2 - m)
    inv = 1.0 / (w0 + w1 + w2)
    et = et_ref[...]
    att = _bdot(w0 * inv, et) * outs[0] + _bdot(w1 * inv, et) * outs[1] + _bdot(w2 * inv, et) * outs[2]
    out_ref[...] = h_ref[...] + _bdot(att, wo_ref[...])


def _attn_out(outs, lses, h, wo, et, seq_len):
    n = h.shape[0]
    tm = _tile(n, PROJ_TILE)
    tps = seq_len // tm
    row_spec = pl.BlockSpec((tm, D_MODEL), lambda i: (i, 0))

    def rm_spec(dil, width):
        return pl.BlockSpec((None, dil, tm // dil, width), lambda i: (i // tps, 0, i % tps, 0))

    dils = [dil for _, dil in DILATION_GROUPS]
    scratch = []
    for _ in dils:
        scratch += [pltpu.VMEM((D_MODEL // LANES, tm, LANES), F32), pltpu.VMEM((tm, LANES), F32)]
    return pl.pallas_call(
        functools.partial(_attn_out_kernel, tm=tm),
        grid=(n // tm,),
        in_specs=[rm_spec(d, D_MODEL) for d in dils] + [rm_spec(d, LANES) for d in dils]
                 + [row_spec, _const_spec(wo.shape), _const_spec(et.shape)],
        out_specs=row_spec,
        out_shape=jax.ShapeDtypeStruct((n, D_MODEL), F32),
        scratch_shapes=scratch,
        compiler_params=_params("arbitrary"),
        name="attn_out",
    )(*outs, *lses, h, wo, et)


def _decode_bias(rel_bias, seq_len, cache_len):
    ncol = cache_len + LANES
    period = seq_len + ncol - 1
    idx = np.arange(period)
    dist = cache_len - np.where(idx < ncol, idx, idx - period)
    buckets = _t5_buckets(np.clip(dist, 0, MAX_WINDOW))
    tabs = []
    for g, (win, dil) in enumerate(DILATION_GROUPS):
        valid = (dist >= 0) & (dist % dil == 0) & (dist <= win)
        tbl = jnp.take(rel_bias[:, g * N_HEADS:(g + 1) * N_HEADS].astype(F32), buckets, axis=0).T
        tabs.append(jnp.where(valid[None], tbl, NEG_INF))
    rows = _toeplitz(jnp.stack(tabs), seq_len, ncol)
    bias = rows.reshape(N_GROUPS * N_HEADS * seq_len, ncol)
    return bias[:, :cache_len], bias[:, cache_len:]


def _attn_decode_kernel(q_ref, cache_ref, kvn_ref, h_ref, bc_ref, bn_ref, wo_ref, out_ref, *, seq_len):
    kvd = KV_HEADS * HEAD_DIM
    nslot = N_GROUPS * N_HEADS
    rows_g = N_HEADS * seq_len
    nt = (((1,), (1,)), ((), ()))
    cache = cache_ref[...]
    kc = cache[:, :kvd].astype(BF16)
    vc = cache[:, kvd:].astype(BF16)
    kvn = kvn_ref[...]
    pad = jnp.zeros((LANES - seq_len, kvd), F32)
    kn = jnp.concatenate([kvn[:, :kvd], pad], axis=0).astype(BF16)
    vn = jnp.concatenate([kvn[:, kvd:], pad], axis=0).astype(BF16)
    lhs = jnp.concatenate([q_ref[:, s * kvd:(s + 1) * kvd] for s in range(nslot)], axis=0).astype(BF16)
    cache_len = kc.shape[0]
    sn = lax.dot_general(lhs, kn, nt, preferred_element_type=F32) + bn_ref[...]
    scs, m = [], None
    for g, (win, _) in enumerate(DILATION_GROUPS):
        c0 = (cache_len - min(win, cache_len)) // LANES * LANES
        rows = slice(g * rows_g, (g + 1) * rows_g)
        sc = lax.dot_general(lhs[rows], kc[c0:], nt, preferred_element_type=F32) + bc_ref[rows, c0:]
        scs.append((sc, c0, rows))
        m_g = jnp.maximum(jnp.max(sc, axis=-1, keepdims=True), jnp.max(sn[rows], axis=-1, keepdims=True))
        m = m_g if m is None else jnp.maximum(m, m_g)
    l = jnp.zeros((rows_g, 1), F32)
    num = jnp.zeros((rows_g, kvd), F32)
    for sc, c0, rows in scs:
        pc = jnp.exp(sc - m)
        pn = jnp.exp(sn[rows] - m)
        l = l + jnp.sum(pc, axis=-1, keepdims=True) + jnp.sum(pn, axis=-1, keepdims=True)
        num = (num + jnp.dot(pc.astype(BF16), vc[c0:], preferred_element_type=F32)
               + jnp.dot(pn.astype(BF16), vn, preferred_element_type=F32))
    row = lax.broadcasted_iota(jnp.int32, (rows_g, kvd), 0)
    lane = lax.broadcasted_iota(jnp.int32, (rows_g, kvd), 1)
    own = (row // (Q_PER_KV * seq_len)) == (lane // HEAD_DIM)
    att = jnp.where(own, num / l, 0.0)
    out = h_ref[...]
    for r in range(Q_PER_KV):
        a_r = att[r * seq_len:(r + 1) * seq_len]
        for c in range(1, KV_HEADS):
            a_r = a_r + att[(c * Q_PER_KV + r) * seq_len:(c * Q_PER_KV + r + 1) * seq_len]
        out = out + _bdot(a_r, wo_ref[r])
    out_ref[...] = out


def _attn_decode(q, cache, kv_new, h, bias_c, bias_n, wo_r, nb, seq_len):
    cache_len = cache.shape[1]
    qw = q.shape[1]
    kvw = 2 * KV_HEADS * HEAD_DIM
    return pl.pallas_call(
        functools.partial(_attn_decode_kernel, seq_len=seq_len),
        grid=(nb,),
        in_specs=[pl.BlockSpec((seq_len, qw), lambda b: (b, 0)),
                  pl.BlockSpec((None, cache_len, kvw), lambda b: (b, 0, 0)),
                  pl.BlockSpec((seq_len, kvw), lambda b: (b, 0)),
                  pl.BlockSpec((seq_len, D_MODEL), lambda b: (b, 0)),
                  _const_spec(bias_c.shape), _const_spec(bias_n.shape), _const_spec(wo_r.shape)],
        out_specs=pl.BlockSpec((seq_len, D_MODEL), lambda b: (b, 0)),
        out_shape=jax.ShapeDtypeStruct((nb * seq_len, D_MODEL), F32),
        compiler_params=pltpu.CompilerParams(dimension_semantics=("arbitrary",), vmem_limit_bytes=VMEM_LIMIT,
                                             allow_input_fusion=[True, True, False, False, False, False, False]),
        name="attn_decode",
    )(q, cache, kv_new, h, bias_c, bias_n, wo_r)


def _prep_weights(norm_w, ffn1_wi, ffn1_wo, ffn2_wi, ffn2_wo, pe_proj, pe_gate,
                  rwkv_mix, rwkv_wrkv, rwkv_wo, rwkv_w0, rwkv_w1, rwkv_w2, rwkv_a0, rwkv_a1, rwkv_a2,
                  rwkv_g1, rwkv_g2, rwkv_kk, rwkv_ka, rwkv_rk, rwkv_lnx_w, rwkv_lnx_b,
                  attn_wq, attn_wo, kv_norm, w_kv, rel_bias, final_norm):
    def row(v):
        return v.reshape(1, -1).astype(F32)

    def pad_cols(w, n):
        return jnp.pad(w, ((0, 0), (0, n - w.shape[1]))).astype(BF16)

    def pad_rows(w, n):
        return jnp.pad(w, ((0, n - w.shape[0]), (0, 0))).astype(BF16)

    head_of_lane = np.arange(D_MODEL) // HEAD_DIM
    e = jnp.asarray(head_of_lane[:, None] == np.arange(LANES)[None, :], BF16)
    et = jnp.asarray(np.arange(LANES)[:, None] == head_of_lane[None, :], BF16)

    depth = norm_w.shape[0]
    layers = []
    for i in range(depth):
        layers.append(dict(
            nw=[row(norm_w[i, j]) for j in range(4)],
            ffn1=(ffn1_wi.astype(F32), ffn1_wo.astype(F32)), ffn2=(ffn2_wi.astype(F32), ffn2_wo.astype(F32)),
            pe_gate=pe_gate[i].astype(BF16), pe_proj=pe_proj[i].astype(BF16)))
    n_a = depth // 2
    rw = []
    for i in range(n_a):
        rw.append(dict(
            nw=row(norm_w[i, 1]), mix=rwkv_mix[i].astype(F32), wrkv=rwkv_wrkv[i].astype(BF16),
            w0=row(rwkv_w0[i]), w1=pad_cols(rwkv_w1[i], LANES), w2=pad_rows(rwkv_w2[i], LANES),
            a0=row(rwkv_a0[i]), a1=pad_cols(rwkv_a1[i], LANES), a2=pad_rows(rwkv_a2[i], LANES),
            g1=pad_cols(rwkv_g1[i], 2 * LANES), g2=pad_rows(rwkv_g2[i], 2 * LANES),
            kk=row(rwkv_kk[i]), ka=row(rwkv_ka[i]), rk=row(rwkv_rk[i]),
            lnw=row(rwkv_lnx_w[i]), lnb=row(rwkv_lnx_b[i]), wo=rwkv_wo[i].astype(BF16), e=e, et=et))
    scale = HEAD_DIM ** -0.5
    at = []
    for j in range(depth - n_a):
        wq = attn_wq[j] * scale
        wo_r = attn_wo[j].reshape(KV_HEADS, Q_PER_KV, HEAD_DIM, D_MODEL).transpose(1, 0, 2, 3)
        wo_r = wo_r.reshape(Q_PER_KV, KV_HEADS * HEAD_DIM, D_MODEL).astype(BF16)
        wq_seg = wq.reshape(D_MODEL, N_GROUPS, N_HEADS, HEAD_DIM)[:, :, _HEAD_OF_SEG].reshape(D_MODEL, -1)
        wo_seg = attn_wo[j].reshape(N_HEADS, HEAD_DIM, D_MODEL)[_HEAD_OF_SEG].reshape(D_MODEL, D_MODEL)
        at.append(dict(wq_seg=wq_seg.astype(BF16), wo_seg=wo_seg.astype(BF16), wo_r=wo_r))
    return dict(layers=layers, rwkv=rw, attn=at, kv_norm=row(kv_norm), w_kv=w_kv.astype(BF16),
                final_norm=row(final_norm), rel_bias=rel_bias, et=et)


def _trunks(streams, w):
    depth = len(w["layers"])
    n_a = depth // 2
    kvw = 2 * KV_HEADS * HEAD_DIM
    st = []
    for sd in streams:
        nb, seq_len, _ = sd["x"].shape
        n = nb * seq_len
        st.append(dict(nb=nb, seq_len=seq_len, n=n, h=sd["x"].reshape(n, D_MODEL).astype(F32),
                       p=sd["p"].reshape(depth, n, PLE_DIM).astype(F32), wkv_out=[], shift_out=[],
                       kv_rows=None, kv_rm=None, **{k: sd[k] for k in ("wkv0", "shift0", "cache")}))

    def ffn(nw, weights, **kw):
        for t, h in zip(st, _ffn([t["h"] for t in st], nw, *weights, **kw)):
            t["h"] = h

    for i in range(depth):
        lw = w["layers"][i]
        if i == n_a:
            for t in st:
                if t["cache"] is None:
                    t["kv_rows"], *t["kv_rm"] = _proj_rm(t["h"], w["kv_norm"], w["w_kv"], t["nb"], t["seq_len"], True,
                                                         [(0, kvw)] * N_GROUPS, "kv_proj")
                else:
                    t["kv_rows"] = _norm_mm(t["h"], w["kv_norm"], w["w_kv"], F32, name="kv_proj")
        ffn(lw["nw"][0], lw["ffn1"], layer=i)
        for t in st:
            nb, seq_len, n, h = t["nb"], t["seq_len"], t["n"], t["h"]
            if i < n_a:
                h, sh, state = _rwkv_layer(h, t["shift0"][i].astype(F32), t["wkv0"][i].astype(F32), seq_len,
                                           w["rwkv"][i])
                t["wkv_out"].append(state)
                t["shift_out"].append(sh)
            elif t["cache"] is None:
                al = w["attn"][i - n_a]
                q_rm = _proj_rm(h, lw["nw"][1], al["wq_seg"], nb, seq_len, False,
                                [(gi * D_MODEL, (gi + 1) * D_MODEL) for gi in range(N_GROUPS)], "q_proj")
                outs, lses = [], []
                for gi in range(N_GROUPS):
                    o, lse = _attn_group(q_rm[gi], t["kv_rm"][gi], _band_bias(w["rel_bias"], gi))
                    outs.append(o)
                    lses.append(lse)
                h = _attn_out(outs, lses, h, al["wo_seg"], w["et"], seq_len)
            else:
                al = w["attn"][i - n_a]
                q = _norm_mm(h, lw["nw"][1], al["wq_seg"], F32, name="q_proj_decode")
                q = q.reshape(n, N_GROUPS, N_HEADS, HEAD_DIM)[:, :, np.argsort(_HEAD_OF_SEG)]
                q = q.reshape(n, N_GROUPS, KV_HEADS, Q_PER_KV, 1, HEAD_DIM)
                q = q * jnp.eye(KV_HEADS, dtype=F32).reshape(1, 1, KV_HEADS, 1, KV_HEADS, 1)
                q = q.reshape(n, N_GROUPS * N_HEADS * KV_HEADS * HEAD_DIM)
                bias_c, bias_n = _decode_bias(w["rel_bias"], seq_len, t["cache"].shape[1])
                h = _attn_decode(q, t["cache"], t["kv_rows"], h, bias_c, bias_n, al["wo_r"], nb, seq_len)
            t["h"] = h
        pe = ([t["p"] for t in st], lw["nw"][3], lw["pe_gate"], lw["pe_proj"], w["final_norm"])
        ffn(lw["nw"][2], lw["ffn2"], pe=pe, layer=i, final=(i == depth - 1))
    results = []
    for t, sd in zip(st, streams):
        dt = sd["x"].dtype
        y = t["h"].reshape(t["nb"], t["seq_len"], D_MODEL).astype(dt)
        kv_rows = t["kv_rows"].reshape(t["nb"], t["seq_len"], 2, KV_HEADS, HEAD_DIM).astype(dt)
        results.append((y, jnp.stack(t["wkv_out"]).astype(dt), jnp.stack(t["shift_out"]).astype(dt), kv_rows))
    return results


def _trunk(x, p, wkv0, shift0, cache, w):
    return _trunks([dict(x=x, p=p, wkv0=wkv0, shift0=shift0, cache=cache)], w)[0]


def kernel(x_prompt, x_sample, state_wkv, state_shift, cache_kv, p_prompt, p_sample, norm_w, ffn1_wi, ffn1_wo, ffn2_wi, ffn2_wo, pe_proj, pe_gate, rwkv_mix, rwkv_wrkv, rwkv_wo, rwkv_w0, rwkv_w1, rwkv_w2, rwkv_a0, rwkv_a1, rwkv_a2, rwkv_g1, rwkv_g2, rwkv_kk, rwkv_ka, rwkv_rk, rwkv_lnx_w, rwkv_lnx_b, attn_wq, attn_wo, kv_norm, w_kv, rel_bias, final_norm):
    w = _prep_weights(norm_w, ffn1_wi, ffn1_wo, ffn2_wi, ffn2_wo, pe_proj, pe_gate,
                      rwkv_mix, rwkv_wrkv, rwkv_wo, rwkv_w0, rwkv_w1, rwkv_w2, rwkv_a0, rwkv_a1, rwkv_a2,
                      rwkv_g1, rwkv_g2, rwkv_kk, rwkv_ka, rwkv_rk, rwkv_lnx_w, rwkv_lnx_b,
                      attn_wq, attn_wo, kv_norm, w_kv, rel_bias, final_norm)
    n_a = norm_w.shape[0] // 2
    nb, seq_len, _ = x_prompt.shape
    wkv0 = jnp.zeros((n_a, nb, N_HEADS, HEAD_DIM, HEAD_DIM), F32)
    shift0 = jnp.zeros((n_a, nb, D_MODEL), x_prompt.dtype)
    cache = cache_kv.reshape(cache_kv.shape[0], cache_kv.shape[1], 2 * KV_HEADS * HEAD_DIM).astype(F32)
    (y_p, wkv_p, shift_p, kv_p), (y_s, wkv_s, shift_s, kv_s) = _trunks(
        [dict(x=x_prompt, p=p_prompt, wkv0=wkv0, shift0=shift0, cache=None),
         dict(x=x_sample, p=p_sample, wkv0=state_wkv, shift0=state_shift, cache=cache)], w)
    kv_prompt = kv_p[:, seq_len - min(MAX_WINDOW, seq_len):]
    return (y_p, y_s, wkv_p, shift_p, kv_prompt, wkv_s, shift_s, kv_s)
```

```python
import functools

import numpy as np
import jax
import jax.numpy as jnp
from jax import lax
from jax.experimental import pallas as pl
from jax.experimental.pallas import tpu as pltpu

F32 = jnp.float32
BF16 = jnp.bfloat16

D_MODEL = 1024
D_FF = 2816
PLE_DIM = 256
RMS_EPS = 1e-6
HEAD_DIM = 64
N_HEADS = D_MODEL // HEAD_DIM
LNX_EPS = 64e-5
KV_HEADS = 4
Q_PER_KV = N_HEADS // KV_HEADS
DILATION_GROUPS = ((128, 1), (512, 4), (2048, 16))
N_GROUPS = len(DILATION_GROUPS)
MAX_WINDOW = 2048
REL_BUCKETS = 32
REL_MAX_DIST = 2048
NEG_INF = -1e30

LANES = 128
FFN_CHUNK = 256
N_FFN_CHUNKS = D_FF // FFN_CHUNK
TOKEN_TILE = 256
FFN_TILE = 512
PROJ_TILE = 1024
WKV_CHUNK = 64
RWKV_SHORT_SEQS = 8
RWKV_PARALLEL_SEQS = 2
RWKV_STEP_ROWS = 512
Q_BLOCK = 128
ATTN_SPAN = 2048
VMEM_LIMIT = 56 * 1024 * 1024
EXP_MINUS_HALF = 0.6065306597126334


def _params(*sem, n_in):
    return pltpu.CompilerParams(dimension_semantics=sem, vmem_limit_bytes=VMEM_LIMIT,
                                allow_input_fusion=[True] * n_in)


def _const_spec(shape):
    return pl.BlockSpec(shape, lambda *_: (0,) * len(shape))


def _tile(n, pref=TOKEN_TILE):
    t = min(n, pref)
    while n % t:
        t -= 8
    return t


def _rms(x, g):
    return x * lax.rsqrt(jnp.mean(x * x, axis=-1, keepdims=True) + RMS_EPS) * g


def _bdot(a, b):
    return jnp.dot(a.astype(BF16), b, preferred_element_type=F32)


def _head_sum(x, e_ref, et_ref):
    return _bdot(_bdot(x, e_ref[...]), et_ref[...])


def _ffn_kernel(*refs, bounds, with_pe, final):
    ns = len(bounds)
    xs, (g_ref, wi_ref, wo_ref), rest = refs[:ns], refs[ns:ns + 3], refs[ns + 3:]
    if with_pe:
        ps, (gp_ref, wgate_ref, wproj_ref, gf_ref), outs = rest[:ns], rest[ns:ns + 4], rest[ns + 4:]
    else:
        outs = rest

    def body(k):
        x = xs[k][...]
        xn = _rms(x, g_ref[...]).astype(BF16)
        acc = jnp.zeros_like(x)
        for j in range(N_FFN_CHUNKS):
            lo, hi = j * FFN_CHUNK, (j + 1) * FFN_CHUNK
            gate = jnp.dot(xn, wi_ref[:, lo:hi].astype(BF16), preferred_element_type=F32)
            up = jnp.dot(xn, wi_ref[:, D_FF + lo:D_FF + hi].astype(BF16), preferred_element_type=F32)
            act = (gate * jax.nn.sigmoid(gate) * up).astype(BF16)
            acc = acc + jnp.dot(act, wo_ref[lo:hi, :].astype(BF16), preferred_element_type=F32)
        y = x + 0.5 * acc
        if with_pe:
            gate = jax.nn.sigmoid(_bdot(_rms(y, gp_ref[...]), wgate_ref[...]))
            y = y + gate * _bdot(ps[k][...], wproj_ref[...])
            if final:
                y = _rms(y, gf_ref[...])
        outs[k][...] = y

    if ns == 1:
        body(0)
    else:
        i = pl.program_id(0)
        for k, (lo, hi) in enumerate(bounds):
            pl.when(jnp.logical_and(i >= lo, i < hi))(functools.partial(body, k))


def _ffn(hs, g, wi, wo, pe=None, layer=0, final=False):
    tiles = [_tile(h.shape[0], FFN_TILE) for h in hs]
    counts = [h.shape[0] // t for h, t in zip(hs, tiles)]
    bounds = [(sum(counts[:k]), sum(counts[:k + 1])) for k in range(len(hs))]

    def rows(k, width, lead=()):
        lo, cnt = bounds[k][0], counts[k]
        return pl.BlockSpec(tuple(None for _ in lead) + (tiles[k], width),
                            lambda i: lead + (jnp.clip(i - lo, 0, cnt - 1), 0))

    args = list(hs) + [g, wi, wo]
    in_specs = ([rows(k, D_MODEL) for k in range(len(hs))]
                + [_const_spec((1, D_MODEL)),
                   pl.BlockSpec((None,) + wi.shape[1:], lambda i: (layer, 0, 0)),
                   pl.BlockSpec((None,) + wo.shape[1:], lambda i: (layer, 0, 0))])
    if pe is not None:
        args += list(pe[0]) + list(pe[1:])
        in_specs += ([rows(k, PLE_DIM, (layer,)) for k in range(len(hs))]
                     + [_const_spec(a.shape) for a in pe[1:]])
    return pl.pallas_call(
        functools.partial(_ffn_kernel, bounds=tuple(bounds), with_pe=pe is not None, final=final),
        grid=(bounds[-1][1],),
        in_specs=in_specs,
        out_specs=[rows(k, D_MODEL) for k in range(len(hs))],
        out_shape=[jax.ShapeDtypeStruct((h.shape[0], D_MODEL), F32) for h in hs],
        compiler_params=_params("arbitrary", n_in=len(args)),
        name="ffn_pe" if pe is not None else "ffn",
    )(*args)


def _norm_mm_kernel(x_ref, g_ref, w_ref, o_ref):
    o_ref[...] = _bdot(_rms(x_ref[...], g_ref[...]), w_ref[...]).astype(o_ref.dtype)


def _norm_mm(h, g, w, out_dtype, col_tile=None, name="norm_mm"):
    n = h.shape[0]
    tm = _tile(n)
    nout = w.shape[1]
    tn = nout if col_tile is None else col_tile
    return pl.pallas_call(
        _norm_mm_kernel,
        grid=(n // tm, nout // tn),
        in_specs=[pl.BlockSpec((tm, D_MODEL), lambda i, j: (i, 0)), _const_spec((1, D_MODEL)),
                  pl.BlockSpec((D_MODEL, tn), lambda i, j: (0, j))],
        out_specs=pl.BlockSpec((tm, tn), lambda i, j: (i, j)),
        out_shape=jax.ShapeDtypeStruct((n, nout), out_dtype),
        compiler_params=_params("arbitrary", "arbitrary", n_in=3),
        name=name,
    )(h, g, w)


def _rwkv_layer_kernel(h_ref, sh_ref, s0_ref, nw_ref, mix_ref, wrkv_ref, w0_ref, w1_ref, w2_ref, a0_ref, a1_ref,
                       a2_ref, g1_ref, g2_ref, kk_ref, ka_ref, lnw_ref, lnb_ref, rk_ref, wo_ref, e_ref, et_ref,
                       o_ref, hn_o, st_ref,
                       state, prev, r_s, lw_s, k_s, v_s, kk_s, b_s, g_s, y_s, *, rows, n_sub, nseq, npar):
    c = pl.program_id(1)
    C = WKV_CHUNK
    C2 = 2 * C
    hp = N_HEADS // 2
    n_pairs = npar * hp
    seq_rows = rows // nseq
    short = seq_rows % C != 0
    total = npar * rows

    def load_state(s0, base=0):
        for h in range(N_HEADS):
            lo = (h % 2) * HEAD_DIM
            state[base + h // 2] = jnp.zeros((LANES, LANES), F32)
        for h in range(N_HEADS):
            lo = (h % 2) * HEAD_DIM
            state[base + h // 2, lo:lo + HEAD_DIM, lo:lo + HEAD_DIM] = s0[h]

    def store_state(st, base=0):
        for h in range(N_HEADS):
            lo = (h % 2) * HEAD_DIM
            st[h] = state[base + h // 2, lo:lo + HEAD_DIM, lo:lo + HEAD_DIM]

    if nseq == 1:
        @pl.when(c == 0)
        def _():
            for q in range(npar):
                load_state(s0_ref.at[q], q * hp)
                prev[q:q + 1, :] = sh_ref[q]
    if short:
        for ref in (r_s, lw_s, k_s, v_s, kk_s, b_s):
            ref[rows:, :] = jnp.zeros((ref.shape[0] - rows, D_MODEL), F32)

    nw = nw_ref[...]
    h_in = h_ref[...].reshape(total, D_MODEL) if nseq == 1 else h_ref[...]
    hn = _rms(h_in, nw)
    row = lax.broadcasted_iota(jnp.int32, hn.shape, 0)
    rolled = pltpu.roll(hn, 1, 0)
    if nseq == 1:
        x_prev = rolled
        for q in range(npar):
            x_prev = jnp.where(row == q * rows, prev[q:q + 1, :], x_prev)
            last = hn[(q + 1) * rows - 1:(q + 1) * rows]
            prev[q:q + 1, :] = last
            hn_o[q] = last
    else:
        x_prev = jnp.where(lax.rem(row, seq_rows) == 0, sh_ref[...], rolled)
        hn_o[...] = hn
    xx = x_prev - hn
    mix = mix_ref[...]
    xr, xw, xk, xv, xa, xg = (hn + xx * mix[j:j + 1] for j in range(6))
    r = _bdot(xr, wrkv_ref[0])
    k = _bdot(xk, wrkv_ref[1])
    v = _bdot(xv, wrkv_ref[2])
    wl = w0_ref[...] + _bdot(jnp.tanh(_bdot(xw, w1_ref[...])), w2_ref[...])
    lw_s[0:total] = -EXP_MINUS_HALF * jax.nn.sigmoid(wl)
    a = jax.nn.sigmoid(a0_ref[...] + _bdot(_bdot(xa, a1_ref[...]), a2_ref[...]))
    g_s[...] = _bdot(jax.nn.sigmoid(_bdot(xg, g1_ref[...])), g2_ref[...])
    kkv = k * kk_ref[...]
    kk = kkv / jnp.maximum(jnp.sqrt(_head_sum(kkv * kkv, e_ref, et_ref)), 1e-12)
    r_s[0:total] = r
    k_s[0:total] = k * (1.0 + (a - 1.0) * ka_ref[...])
    v_s[0:total] = v
    kk_s[0:total] = kk
    b_s[0:total] = kk * a

    ri = lax.broadcasted_iota(jnp.int32, (C, C), 0)
    ci = lax.broadcasted_iota(jnp.int32, (C, C), 1)
    tri = (ri >= ci).astype(BF16)
    r2 = lax.broadcasted_iota(jnp.int32, (C2, C2), 0)
    c2 = lax.broadcasted_iota(jnp.int32, (C2, C2), 1)
    strict = r2 > c2
    incl = r2 >= c2
    lane = lax.broadcasted_iota(jnp.int32, (C, LANES), 1)
    head0 = lane < HEAD_DIM

    def hat(x):
        x3 = jnp.stack([x[:, p * LANES:(p + 1) * LANES] for p in range(n_pairs)])
        return jnp.concatenate([jnp.where(head0, x3, 0.0), jnp.where(head0, 0.0, x3)], axis=1)

    def mm(a, b):
        return jnp.einsum("pmk,pkn->pmn", a.astype(BF16), b.astype(BF16), preferred_element_type=F32)

    def mm_nt(a, b):
        return jnp.einsum("pmk,pnk->pmn", a.astype(BF16), b.astype(BF16), preferred_element_type=F32)

    chunk_row = lax.broadcasted_iota(jnp.int32, (C, D_MODEL), 0)

    def chunk(ref, s):
        if short:
            x = ref[pl.ds(pl.multiple_of(s * seq_rows, 8), C), :]
            return jnp.where(chunk_row < seq_rows, x, 0.0)
        return jnp.concatenate([ref[pl.ds(q * rows + s * C, C), :] for q in range(npar)], axis=1)

    def sub_chunk(s, carry):
        lw = chunk(lw_s, s)
        p1 = lw.astype(BF16)
        rem = lw - p1.astype(F32)
        p2 = rem.astype(BF16)
        p3 = (rem - p2.astype(F32)).astype(BF16)
        cw = (jnp.dot(tri, p1, preferred_element_type=F32) + jnp.dot(tri, p2, preferred_element_type=F32)
              + jnp.dot(tri, p3, preferred_element_type=F32))
        cw_end = cw[C - 1:C, :]
        e_neg = jnp.exp(-cw)
        e_end = jnp.exp(cw_end - cw)
        kk_c = chunk(kk_s, s)
        bb = chunk(b_s, s)
        k_c = chunk(k_s, s)
        a_all = kk_c * jnp.exp(cw - lw)
        r_all = chunk(r_s, s) * jnp.exp(cw)
        b_all = bb * e_neg
        k_all = k_c * e_neg
        bd_all = bb * e_end
        kd_all = k_c * e_end
        v_all = chunk(v_s, s)
        decay = jnp.exp(cw_end)
        ar_h = jnp.concatenate([hat(a_all), hat(r_all)], axis=1)
        bk_h = jnp.concatenate([hat(b_all), hat(k_all)], axis=1)
        v_h = hat(v_all)
        st = state[...]
        g = mm_nt(ar_h, bk_h)
        low = jnp.where(strict, g[:, :C2, :C2], 0.0)
        ak = jnp.where(strict, g[:, :C2, C2:], 0.0)
        rbk = jnp.concatenate([jnp.where(incl, g[:, C2:, :C2], 0.0), jnp.where(incl, g[:, C2:, C2:], 0.0)], axis=2)
        ss = mm_nt(ar_h, st)
        x = -(ss[:, :C2] + mm(ak, v_h))
        t = mm(low, jnp.concatenate([low, x], axis=2))
        lp = t[:, :, :C2]
        x = x - t[:, :, C2:]
        for _ in range(max(int(np.ceil(np.log2(min(C, seq_rows)))), 2) - 2):
            t = mm(lp, jnp.concatenate([lp, x], axis=2))
            lp = t[:, :, :C2]
            x = x + t[:, :, C2:]
        x = x + mm(lp, x)
        xv = jnp.concatenate([x, v_h], axis=1)
        y_h = ss[:, C2:] + mm(rbk, xv)
        y = y_h[:, :C] + y_h[:, C:]
        for p in range(n_pairs):
            q, pl_ = divmod(p, hp)
            if short:
                y_s[pl.ds(pl.multiple_of(s * seq_rows, 8), seq_rows), pl_ * LANES:(pl_ + 1) * LANES] = y[p, :seq_rows]
            else:
                y_s[pl.ds(q * rows + s * C, C), pl_ * LANES:(pl_ + 1) * LANES] = y[p]
        bkd_h = jnp.concatenate([hat(bd_all), hat(kd_all)], axis=1)
        xv_t = jnp.stack([xv[p].T for p in range(n_pairs)])
        dec3 = jnp.stack([decay[:, p * LANES:(p + 1) * LANES] for p in range(n_pairs)])
        state[...] = st * dec3 + mm(xv_t, bkd_h)
        return carry

    if nseq == 1:
        for s in range(n_sub):
            sub_chunk(s, 0)
    else:
        def one_sequence(s, carry):
            load_state(s0_ref.at[s])
            sub_chunk(s, carry)
            store_state(st_ref.at[s])
            return carry

        lax.fori_loop(0, nseq, one_sequence, 0)

    y = y_s[0:total]
    inv_n = 1.0 / HEAD_DIM
    mu = _head_sum(y, e_ref, et_ref) * inv_n
    yc = y - mu
    var = _head_sum(yc * yc, e_ref, et_ref) * inv_n
    yn = yc * lax.rsqrt(var + LNX_EPS) * lnw_ref[...] + lnb_ref[...]
    bonus = _head_sum(r_s[0:total] * k_s[0:total] * rk_ref[...], e_ref, et_ref) * v_s[0:total]
    out = h_in + _bdot((yn + bonus) * g_s[...], wo_ref[...])
    o_ref[...] = out.reshape(o_ref.shape)

    if nseq == 1:
        @pl.when(c == pl.num_programs(1) - 1)
        def _():
            for q in range(npar):
                store_state(st_ref.at[q], q * hp)


def _rwkv_layer(h, shift, s0, seq_len, lw):
    n = h.shape[0]
    nb = n // seq_len
    if seq_len % WKV_CHUNK == 0:
        npar = RWKV_PARALLEL_SEQS if nb % RWKV_PARALLEL_SEQS == 0 else 1
        nseq, rows = 1, _tile(seq_len, RWKV_STEP_ROWS // npar)
        steps, n_sub, total = seq_len // rows, rows // WKV_CHUNK, npar * rows
        buf_rows = total
        assert rows % WKV_CHUNK == 0
        h_in = h.reshape(nb, seq_len, D_MODEL)
        row_spec = pl.BlockSpec((npar, rows, D_MODEL), lambda bi, ci: (bi, ci, 0))
        sh_in = shift.reshape(nb, 1, D_MODEL)
        sh_spec = pl.BlockSpec((npar, 1, D_MODEL), lambda bi, ci: (bi, 0, 0))
        hn_shape, hn_spec = jax.ShapeDtypeStruct((nb, 1, D_MODEL), F32), sh_spec
        nst = npar
    else:
        assert seq_len < WKV_CHUNK and seq_len % 8 == 0
        nseq, npar = _tile(nb, RWKV_SHORT_SEQS), 1
        rows, steps, n_sub = nseq * seq_len, 1, 1
        total = rows
        buf_rows = rows + WKV_CHUNK - seq_len
        h_in = h
        row_spec = pl.BlockSpec((rows, D_MODEL), lambda bi, ci: (bi, 0))
        sh_in = jnp.repeat(shift, seq_len, axis=0)
        sh_spec = row_spec
        hn_shape, hn_spec = jax.ShapeDtypeStruct((n, D_MODEL), F32), sh_spec
        nst = nseq
    st_spec = pl.BlockSpec((nst, N_HEADS, HEAD_DIM, HEAD_DIM), lambda bi, ci: (bi, 0, 0, 0))
    consts = [lw["nw"], lw["mix"], lw["wrkv"], lw["w0"], lw["w1"], lw["w2"], lw["a0"], lw["a1"], lw["a2"],
              lw["g1"], lw["g2"], lw["kk"], lw["ka"], lw["lnw"], lw["lnb"], lw["rk"], lw["wo"], lw["e"], lw["et"]]
    seq_buf = pltpu.VMEM((buf_rows, D_MODEL), F32)
    out, hn, st = pl.pallas_call(
        functools.partial(_rwkv_layer_kernel, rows=rows, n_sub=n_sub, nseq=nseq, npar=npar),
        grid=(nb // nst, steps),
        in_specs=[row_spec, sh_spec, st_spec] + [_const_spec(c.shape) for c in consts],
        out_specs=[row_spec, hn_spec, st_spec],
        out_shape=[jax.ShapeDtypeStruct(h_in.shape, F32), hn_shape, jax.ShapeDtypeStruct(s0.shape, F32)],
        scratch_shapes=[pltpu.VMEM((npar * N_HEADS // 2, LANES, LANES), F32), pltpu.VMEM((8, D_MODEL), F32)]
                       + [seq_buf] * 6 + [pltpu.VMEM((total, D_MODEL), F32), seq_buf],
        compiler_params=_params("arbitrary", "arbitrary", n_in=3 + len(consts)),
        name="rwkv_layer",
    )(h_in, sh_in, s0, *consts)
    out = out.reshape(n, D_MODEL)
    shift_out = hn.reshape(nb, D_MODEL) if nseq == 1 else hn.reshape(nb, seq_len, D_MODEL)[:, -1]
    return out, shift_out, st


def _t5_buckets(dist):
    d = np.asarray(dist, dtype=np.int64)
    max_exact = REL_BUCKETS // 2
    large = max_exact + (np.log(np.maximum(d, 1) / max_exact) / np.log(REL_MAX_DIST / max_exact)
                         * (REL_BUCKETS - max_exact)).astype(np.int32)
    large = np.minimum(large, REL_BUCKETS - 1)
    return np.where(d < max_exact, d, large).astype(np.int32)


def _toeplitz(tab, n_rows, n_cols):
    period = tab.shape[-1]
    assert period >= n_rows + n_cols - 1 and n_cols <= period - 1
    lead = tab.shape[:-1]
    flat = jnp.broadcast_to(tab[..., None, :], lead + (n_rows, period)).reshape(lead + (-1,))
    skew = flat[..., :n_rows * (period - 1)].reshape(lead + (n_rows, period - 1))
    return skew[..., :n_cols]


def _band_bias(rel_bias, group):
    win, dil = DILATION_GROUPS[group]
    assert win // dil == Q_BLOCK
    period = 3 * Q_BLOCK - 1
    idx = np.arange(period)
    m = Q_BLOCK - np.where(idx < 2 * Q_BLOCK, idx, idx - period)
    valid = (m >= 0) & (m <= Q_BLOCK)
    buckets = _t5_buckets(dil * np.clip(m, 0, Q_BLOCK))
    tbl = jnp.take(rel_bias[:, group * N_HEADS:(group + 1) * N_HEADS].astype(F32), buckets, axis=0).T
    general = _toeplitz(jnp.where(valid[None], tbl, NEG_INF), Q_BLOCK, 2 * Q_BLOCK)
    first = jnp.where((np.arange(2 * Q_BLOCK) >= Q_BLOCK)[None, None, :], general, NEG_INF)
    return jnp.stack([first, general])


_HEAD_OF_SEG = np.array([4 * (2 * (s // 8) + s % 2) + (s // 2) % 4 for s in range(N_HEADS)])


def _proj_rm_kernel(x_ref, g_ref, w_ref, *refs, natural, lane_ranges, tm):
    scr = refs[-1]
    outs = refs[:-1]
    y = _bdot(_rms(x_ref[...], g_ref[...]), w_ref[...])
    if natural:
        outs[0][...] = y
        outs = outs[1:]
    for c in range(scr.shape[0]):
        scr[c] = y[:, c * LANES:(c + 1) * LANES]
    for gi, (_, dil) in enumerate(DILATION_GROUPS):
        lo, hi = lane_ranges[gi]
        if dil == 1:
            outs[gi][0] = y[:, lo:hi].astype(BF16)
            continue
        for rho in range(dil):
            rows = [scr[c, pl.ds(rho, tm // dil, stride=dil), :] for c in range(lo // LANES, hi // LANES)]
            outs[gi][rho] = jnp.concatenate(rows, axis=1).astype(BF16)


def _proj_rm(h, g, w, nb, seq_len, natural, lane_ranges, name):
    n = h.shape[0]
    tm = _tile(n, PROJ_TILE)
    nout = w.shape[1]
    assert seq_len % tm == 0 and all(tm % (16 * dil) == 0 for _, dil in DILATION_GROUPS)
    tps = seq_len // tm
    out_shape, out_specs = [], []
    if natural:
        out_shape.append(jax.ShapeDtypeStruct((n, nout), F32))
        out_specs.append(pl.BlockSpec((tm, nout), lambda i: (i, 0)))
    for (_, dil), (lo, hi) in zip(DILATION_GROUPS, lane_ranges):
        out_shape.append(jax.ShapeDtypeStruct((nb, dil, seq_len // dil, hi - lo), BF16))
        out_specs.append(pl.BlockSpec((None, dil, tm // dil, hi - lo), lambda i: (i // tps, 0, i % tps, 0)))
    return pl.pallas_call(
        functools.partial(_proj_rm_kernel, natural=natural, lane_ranges=lane_ranges, tm=tm),
        grid=(n // tm,),
        in_specs=[pl.BlockSpec((tm, D_MODEL), lambda i: (i, 0)), _const_spec((1, D_MODEL)), _const_spec(w.shape)],
        out_specs=out_specs,
        out_shape=out_shape,
        scratch_shapes=[pltpu.VMEM((nout // LANES, tm, LANES), F32)],
        compiler_params=_params("arbitrary", n_in=3),
        name=name,
    )(h, g, w)


def _attn_kernel(q_ref, kv_ref, halo_ref, bias_ref, o_ref, lse_ref, kvbuf, *, nq):
    i = pl.program_id(0)
    dil = q_ref.shape[0]
    kvd = KV_HEADS * HEAD_DIM
    nt = (((1,), (1,)), ((), ()))
    kvbuf[:, :Q_BLOCK, :] = halo_ref[...]
    kvbuf[:, Q_BLOCK:, :] = kv_ref[...]
    lane = lax.broadcasted_iota(jnp.int32, (Q_BLOCK, LANES), 1)
    low_half = lane < HEAD_DIM
    lane_row = lax.broadcasted_iota(jnp.int32, (1, LANES), 1)
    keep_lo = (lane_row < HEAD_DIM).astype(BF16)
    keep_hi = (lane_row >= HEAD_DIM).astype(BF16)

    def block(u, carry):
        rho = lax.div(u, nq)
        j = u - rho * nq
        r0 = pl.multiple_of(j * Q_BLOCK, Q_BLOCK)
        qb = q_ref[rho, pl.ds(r0, Q_BLOCK), :]
        kvb = kvbuf[rho, pl.ds(r0, 2 * Q_BLOCK), :]
        bsel = jnp.where(jnp.logical_and(i == 0, j == 0), 0, 1)
        lse_tile = jnp.zeros((Q_BLOCK, LANES), F32)
        for G in range(KV_HEADS // 2):
            kg = kvb[:, G * LANES:(G + 1) * LANES]
            vg = kvb[:, kvd + G * LANES:kvd + (G + 1) * LANES]
            vcat = jnp.concatenate([vg * keep_lo, vg * keep_hi], axis=0)
            pieces = []
            for r in range(Q_PER_KV):
                qg = qb[:, (G * Q_PER_KV + r) * LANES:(G * Q_PER_KV + r + 1) * LANES]
                pieces += [qg * keep_lo, qg * keep_hi]
            s_all = lax.dot_general(jnp.concatenate(pieces, axis=0), kg, nt, preferred_element_type=F32)
            for r in range(Q_PER_KV):
                grp = G * Q_PER_KV + r
                parts = []
                for half in range(2):
                    s = s_all[(2 * r + half) * Q_BLOCK:(2 * r + half + 1) * Q_BLOCK]
                    s = s + bias_ref[bsel, int(_HEAD_OF_SEG[2 * grp + half])]
                    m = jnp.max(s, axis=-1, keepdims=True)
                    p = jnp.exp(s - m)
                    l = jnp.sum(p, axis=-1, keepdims=True)
                    parts.append((p.astype(BF16), l, m + jnp.log(l)))
                (p0, l0, e0), (p1, l1, e1) = parts
                o = jnp.dot(jnp.concatenate([p0, p1], axis=1), vcat, preferred_element_type=F32)
                o = (o / jnp.where(low_half, l0, l1)).astype(o_ref.dtype)
                o_ref[rho, pl.ds(r0, Q_BLOCK), grp * LANES:(grp + 1) * LANES] = o
                lse_tile = jnp.where(lane == 2 * grp, e0, lse_tile)
                lse_tile = jnp.where(lane == 2 * grp + 1, e1, lse_tile)
        lse_ref[rho, pl.ds(r0, Q_BLOCK), :] = lse_tile
        return carry

    lax.fori_loop(0, dil * nq, block, 0)


def _attn_group(q_rm, kv_rm, bias):
    nb, dil, tsub, _ = q_rm.shape
    kvw = kv_rm.shape[-1]
    rows = ATTN_SPAN // dil
    nq = rows // Q_BLOCK
    assert tsub % rows == 0 and nq >= 1
    span_spec = lambda width: pl.BlockSpec((None, dil, rows, width), lambda i, b: (b, 0, i, 0))
    return pl.pallas_call(
        functools.partial(_attn_kernel, nq=nq),
        grid=(tsub // rows, nb),
        in_specs=[span_spec(D_MODEL), span_spec(kvw),
                  pl.BlockSpec((None, dil, Q_BLOCK, kvw), lambda i, b: (b, 0, jnp.maximum(i * nq - 1, 0), 0)),
                  _const_spec(bias.shape)],
        out_specs=[span_spec(D_MODEL), span_spec(LANES)],
        out_shape=[jax.ShapeDtypeStruct((nb, dil, tsub, D_MODEL), BF16),
                   jax.ShapeDtypeStruct((nb, dil, tsub, LANES), F32)],
        scratch_shapes=[pltpu.VMEM((dil, Q_BLOCK + rows, kvw), BF16)],
        compiler_params=_params("arbitrary", "arbitrary", n_in=4),
        name=f"attn_d{dil}",
    )(q_rm, kv_rm, kv_rm, bias)


def _attn_out_kernel(o0_ref, o1_ref, o2_ref, l0_ref, l1_ref, l2_ref, h_ref, wo_ref, et_ref, out_ref, *scr, tm):
    outs, lses = [], []
    for gi, (o_ref, l_ref) in enumerate(((o0_ref, l0_ref), (o1_ref, l1_ref), (o2_ref, l2_ref))):
        dil = DILATION_GROUPS[gi][1]
        if dil == 1:
            outs.append(o_ref[0].astype(F32))
            lses.append(l_ref[0])
            continue
        so, sl = scr[2 * gi], scr[2 * gi + 1]
        n_tiles = so.shape[0]
        for rho in range(dil):
            rows = pl.ds(rho, tm // dil, stride=dil)
            for c in range(n_tiles):
                so[c, rows, :] = o_ref[rho, :, c * LANES:(c + 1) * LANES].astype(F32)
            sl[rows, :] = l_ref[rho]
        outs.append(jnp.concatenate([so[c] for c in range(n_tiles)], axis=1))
        lses.append(sl[...])
    l0, l1, l2 = lses
    m = jnp.maximum(jnp.maximum(l0, l1), l2)
    w0, w1, w2 = jnp.exp(l0 - m), jnp.exp(l1 - m), jnp.exp(l2 - m)
    inv = 1.0 / (w0 + w1 + w2)
    et = et_ref[...]
    att = _bdot(w0 * inv, et) * outs[0] + _bdot(w1 * inv, et) * outs[1] + _bdot(w2 * inv, et) * outs[2]
    out_ref[...] = h_ref[...] + _bdot(att, wo_ref[...])


def _attn_out(outs, lses, h, wo, et, seq_len):
    n = h.shape[0]
    tm = _tile(n, PROJ_TILE)
    tps = seq_len // tm
    row_spec = pl.BlockSpec((tm, D_MODEL), lambda i: (i, 0))

    def rm_spec(dil, width):
        return pl.BlockSpec((None, dil, tm // dil, width), lambda i: (i // tps, 0, i % tps, 0))

    dils = [dil for _, dil in DILATION_GROUPS]
    scratch = []
    for _ in dils:
        scratch += [pltpu.VMEM((D_MODEL // LANES, tm, LANES), F32), pltpu.VMEM((tm, LANES), F32)]
    return pl.pallas_call(
        functools.partial(_attn_out_kernel, tm=tm),
        grid=(n // tm,),
        in_specs=[rm_spec(d, D_MODEL) for d in dils] + [rm_spec(d, LANES) for d in dils]
                 + [row_spec, _const_spec(wo.shape), _const_spec(et.shape)],
        out_specs=row_spec,
        out_shape=jax.ShapeDtypeStruct((n, D_MODEL), F32),
        scratch_shapes=scratch,
        compiler_params=_params("arbitrary", n_in=9),
        name="attn_out",
    )(*outs, *lses, h, wo, et)


def _decode_bias(rel_bias, seq_len, cache_len):
    ncol = cache_len + LANES
    period = seq_len + ncol - 1
    idx = np.arange(period)
    dist = cache_len - np.where(idx < ncol, idx, idx - period)
    buckets = _t5_buckets(np.clip(dist, 0, MAX_WINDOW))
    tabs = []
    for g, (win, dil) in enumerate(DILATION_GROUPS):
        valid = (dist >= 0) & (dist % dil == 0) & (dist <= win)
        tbl = jnp.take(rel_bias[:, g * N_HEADS:(g + 1) * N_HEADS].astype(F32), buckets, axis=0).T
        tabs.append(jnp.where(valid[None], tbl, NEG_INF))
    rows = _toeplitz(jnp.stack(tabs), seq_len, ncol)
    bias = rows.reshape(N_GROUPS * N_HEADS * seq_len, ncol)
    return bias[:, :cache_len], bias[:, cache_len:]


def _attn_decode_kernel(q_ref, cache_ref, kvn_ref, h_ref, bc_ref, bn_ref, wo_ref, out_ref, *, seq_len):
    kvd = KV_HEADS * HEAD_DIM
    nslot = N_GROUPS * N_HEADS
    rows_g = N_HEADS * seq_len
    nt = (((1,), (1,)), ((), ()))
    cache = cache_ref[...]
    kc = cache[:, :kvd].astype(BF16)
    vc = cache[:, kvd:].astype(BF16)
    kvn = kvn_ref[...]
    pad = jnp.zeros((LANES - seq_len, kvd), F32)
    kn = jnp.concatenate([kvn[:, :kvd], pad], axis=0).astype(BF16)
    vn = jnp.concatenate([kvn[:, kvd:], pad], axis=0).astype(BF16)
    lhs = jnp.concatenate([q_ref[:, s * kvd:(s + 1) * kvd] for s in range(nslot)], axis=0).astype(BF16)
    cache_len = kc.shape[0]
    sn = lax.dot_general(lhs, kn, nt, preferred_element_type=F32) + bn_ref[...]
    scs, m = [], None
    for g, (win, _) in enumerate(DILATION_GROUPS):
        c0 = (cache_len - min(win, cache_len)) // LANES * LANES
        rows = slice(g * rows_g, (g + 1) * rows_g)
        sc = lax.dot_general(lhs[rows], kc[c0:], nt, preferred_element_type=F32) + bc_ref[rows, c0:]
        scs.append((sc, c0, rows))
        m_g = jnp.maximum(jnp.max(sc, axis=-1, keepdims=True), jnp.max(sn[rows], axis=-1, keepdims=True))
        m = m_g if m is None else jnp.maximum(m, m_g)
    l = jnp.zeros((rows_g, 1), F32)
    num = jnp.zeros((rows_g, kvd), F32)
    for sc, c0, rows in scs:
        pc = jnp.exp(sc - m)
        pn = jnp.exp(sn[rows] - m)
        l = l + jnp.sum(pc, axis=-1, keepdims=True) + jnp.sum(pn, axis=-1, keepdims=True)
        num = (num + jnp.dot(pc.astype(BF16), vc[c0:], preferred_element_type=F32)
               + jnp.dot(pn.astype(BF16), vn, preferred_element_type=F32))
    row = lax.broadcasted_iota(jnp.int32, (rows_g, kvd), 0)
    lane = lax.broadcasted_iota(jnp.int32, (rows_g, kvd), 1)
    own = (row // (Q_PER_KV * seq_len)) == (lane // HEAD_DIM)
    att = jnp.where(own, num / l, 0.0)
    out = h_ref[...]
    for r in range(Q_PER_KV):
        a_r = att[r * seq_len:(r + 1) * seq_len]
        for c in range(1, KV_HEADS):
            a_r = a_r + att[(c * Q_PER_KV + r) * seq_len:(c * Q_PER_KV + r + 1) * seq_len]
        out = out + _bdot(a_r, wo_ref[r])
    out_ref[...] = out


def _attn_decode(q, cache, kv_new, h, bias_c, bias_n, wo_r, nb, seq_len):
    cache_len = cache.shape[1]
    qw = q.shape[1]
    kvw = 2 * KV_HEADS * HEAD_DIM
    return pl.pallas_call(
        functools.partial(_attn_decode_kernel, seq_len=seq_len),
        grid=(nb,),
        in_specs=[pl.BlockSpec((seq_len, qw), lambda b: (b, 0)),
                  pl.BlockSpec((None, cache_len, kvw), lambda b: (b, 0, 0)),
                  pl.BlockSpec((seq_len, kvw), lambda b: (b, 0)),
                  pl.BlockSpec((seq_len, D_MODEL), lambda b: (b, 0)),
                  _const_spec(bias_c.shape), _const_spec(bias_n.shape), _const_spec(wo_r.shape)],
        out_specs=pl.BlockSpec((seq_len, D_MODEL), lambda b: (b, 0)),
        out_shape=jax.ShapeDtypeStruct((nb * seq_len, D_MODEL), F32),
        compiler_params=_params("arbitrary", n_in=7),
        name="attn_decode",
    )(q, cache, kv_new, h, bias_c, bias_n, wo_r)


def _prep_weights(norm_w, ffn1_wi, ffn1_wo, ffn2_wi, ffn2_wo, pe_proj, pe_gate,
                  rwkv_mix, rwkv_wrkv, rwkv_wo, rwkv_w0, rwkv_w1, rwkv_w2, rwkv_a0, rwkv_a1, rwkv_a2,
                  rwkv_g1, rwkv_g2, rwkv_kk, rwkv_ka, rwkv_rk, rwkv_lnx_w, rwkv_lnx_b,
                  attn_wq, attn_wo, kv_norm, w_kv, rel_bias, final_norm):
    def row(v):
        return v.reshape(1, -1).astype(F32)

    def pad_cols(w, n):
        return jnp.pad(w, ((0, 0), (0, n - w.shape[1]))).astype(BF16)

    def pad_rows(w, n):
        return jnp.pad(w, ((0, n - w.shape[0]), (0, 0))).astype(BF16)

    head_of_lane = np.arange(D_MODEL) // HEAD_DIM
    e = jnp.asarray(head_of_lane[:, None] == np.arange(LANES)[None, :], BF16)
    et = jnp.asarray(np.arange(LANES)[:, None] == head_of_lane[None, :], BF16)

    depth = norm_w.shape[0]
    layers = []
    for i in range(depth):
        layers.append(dict(
            nw=[row(norm_w[i, j]) for j in range(4)],
            ffn1=(ffn1_wi.astype(F32), ffn1_wo.astype(F32)), ffn2=(ffn2_wi.astype(F32), ffn2_wo.astype(F32)),
            pe_gate=pe_gate[i].astype(BF16), pe_proj=pe_proj[i].astype(BF16)))
    n_a = depth // 2
    rw = []
    for i in range(n_a):
        rw.append(dict(
            nw=row(norm_w[i, 1]), mix=rwkv_mix[i].astype(F32), wrkv=rwkv_wrkv[i].astype(BF16),
            w0=row(rwkv_w0[i]), w1=pad_cols(rwkv_w1[i], LANES), w2=pad_rows(rwkv_w2[i], LANES),
            a0=row(rwkv_a0[i]), a1=pad_cols(rwkv_a1[i], LANES), a2=pad_rows(rwkv_a2[i], LANES),
            g1=pad_cols(rwkv_g1[i], 2 * LANES), g2=pad_rows(rwkv_g2[i], 2 * LANES),
            kk=row(rwkv_kk[i]), ka=row(rwkv_ka[i]), rk=row(rwkv_rk[i]),
            lnw=row(rwkv_lnx_w[i]), lnb=row(rwkv_lnx_b[i]), wo=rwkv_wo[i].astype(BF16), e=e, et=et))
    scale = HEAD_DIM ** -0.5
    at = []
    for j in range(depth - n_a):
        wq = attn_wq[j] * scale
        wo_r = attn_wo[j].reshape(KV_HEADS, Q_PER_KV, HEAD_DIM, D_MODEL).transpose(1, 0, 2, 3)
        wo_r = wo_r.reshape(Q_PER_KV, KV_HEADS * HEAD_DIM, D_MODEL).astype(BF16)
        wq_seg = wq.reshape(D_MODEL, N_GROUPS, N_HEADS, HEAD_DIM)[:, :, _HEAD_OF_SEG].reshape(D_MODEL, -1)
        wo_seg = attn_wo[j].reshape(N_HEADS, HEAD_DIM, D_MODEL)[_HEAD_OF_SEG].reshape(D_MODEL, D_MODEL)
        at.append(dict(wq_seg=wq_seg.astype(BF16), wo_seg=wo_seg.astype(BF16), wo_r=wo_r))
    return dict(layers=layers, rwkv=rw, attn=at, kv_norm=row(kv_norm), w_kv=w_kv.astype(BF16),
                final_norm=row(final_norm), rel_bias=rel_bias, et=et)


def _trunks(streams, w):
    depth = len(w["layers"])
    n_a = depth // 2
    kvw = 2 * KV_HEADS * HEAD_DIM
    st = []
    for sd in streams:
        nb, seq_len, _ = sd["x"].shape
        n = nb * seq_len
        st.append(dict(nb=nb, seq_len=seq_len, n=n, h=sd["x"].reshape(n, D_MODEL).astype(F32),
                       p=sd["p"].reshape(depth, n, PLE_DIM).astype(F32), wkv_out=[], shift_out=[],
                       kv_rows=None, kv_rm=None, **{k: sd[k] for k in ("wkv0", "shift0", "cache")}))

    def ffn(nw, weights, **kw):
        for t, h in zip(st, _ffn([t["h"] for t in st], nw, *weights, **kw)):
            t["h"] = h

    for i in range(depth):
        lw = w["layers"][i]
        if i == n_a:
            for t in st:
                if t["cache"] is None:
                    t["kv_rows"], *t["kv_rm"] = _proj_rm(t["h"], w["kv_norm"], w["w_kv"], t["nb"], t["seq_len"], True,
                                                         [(0, kvw)] * N_GROUPS, "kv_proj")
                else:
                    t["kv_rows"] = _norm_mm(t["h"], w["kv_norm"], w["w_kv"], F32, name="kv_proj")
        ffn(lw["nw"][0], lw["ffn1"], layer=i)
        for t in st:
            nb, seq_len, n, h = t["nb"], t["seq_len"], t["n"], t["h"]
            if i < n_a:
                h, sh, state = _rwkv_layer(h, t["shift0"][i].astype(F32), t["wkv0"][i].astype(F32), seq_len,
                                           w["rwkv"][i])
                t["wkv_out"].append(state)
                t["shift_out"].append(sh)
            elif t["cache"] is None:
                al = w["attn"][i - n_a]
                q_rm = _proj_rm(h, lw["nw"][1], al["wq_seg"], nb, seq_len, False,
                                [(gi * D_MODEL, (gi + 1) * D_MODEL) for gi in range(N_GROUPS)], "q_proj")
                outs, lses = [], []
                for gi in range(N_GROUPS):
                    o, lse = _attn_group(q_rm[gi], t["kv_rm"][gi], _band_bias(w["rel_bias"], gi))
                    outs.append(o)
                    lses.append(lse)
                h = _attn_out(outs, lses, h, al["wo_seg"], w["et"], seq_len)
            else:
                al = w["attn"][i - n_a]
                q = _norm_mm(h, lw["nw"][1], al["wq_seg"], F32, name="q_proj_decode")
                q = q.reshape(n, N_GROUPS, N_HEADS, HEAD_DIM)[:, :, np.argsort(_HEAD_OF_SEG)]
                q = q.reshape(n, N_GROUPS, KV_HEADS, Q_PER_KV, 1, HEAD_DIM)
                q = q * jnp.eye(KV_HEADS, dtype=F32).reshape(1, 1, KV_HEADS, 1, KV_HEADS, 1)
                q = q.reshape(n, N_GROUPS * N_HEADS * KV_HEADS * HEAD_DIM)
                bias_c, bias_n = _decode_bias(w["rel_bias"], seq_len, t["cache"].shape[1])
                h = _attn_decode(q, t["cache"], t["kv_rows"], h, bias_c, bias_n, al["wo_r"], nb, seq_len)
            t["h"] = h
        pe = ([t["p"] for t in st], lw["nw"][3], lw["pe_gate"], lw["pe_proj"], w["final_norm"])
        ffn(lw["nw"][2], lw["ffn2"], pe=pe, layer=i, final=(i == depth - 1))
    results = []
    for t, sd in zip(st, streams):
        dt = sd["x"].dtype
        y = t["h"].reshape(t["nb"], t["seq_len"], D_MODEL).astype(dt)
        kv_rows = t["kv_rows"].reshape(t["nb"], t["seq_len"], kvw).astype(dt)
        results.append((y, jnp.stack(t["wkv_out"]).astype(dt), jnp.stack(t["shift_out"]).astype(dt), kv_rows))
    return results


def _kv_heads(kv):
    return kv.reshape(kv.shape[:2] + (2, KV_HEADS, HEAD_DIM))


def _trunk(x, p, wkv0, shift0, cache, w):
    y, wkv, shift, kv = _trunks([dict(x=x, p=p, wkv0=wkv0, shift0=shift0, cache=cache)], w)[0]
    return y, wkv, shift, _kv_heads(kv)


def kernel(x_prompt, x_sample, state_wkv, state_shift, cache_kv, p_prompt, p_sample, norm_w, ffn1_wi, ffn1_wo, ffn2_wi, ffn2_wo, pe_proj, pe_gate, rwkv_mix, rwkv_wrkv, rwkv_wo, rwkv_w0, rwkv_w1, rwkv_w2, rwkv_a0, rwkv_a1, rwkv_a2, rwkv_g1, rwkv_g2, rwkv_kk, rwkv_ka, rwkv_rk, rwkv_lnx_w, rwkv_lnx_b, attn_wq, attn_wo, kv_norm, w_kv, rel_bias, final_norm):
    w = _prep_weights(norm_w, ffn1_wi, ffn1_wo, ffn2_wi, ffn2_wo, pe_proj, pe_gate,
                      rwkv_mix, rwkv_wrkv, rwkv_wo, rwkv_w0, rwkv_w1, rwkv_w2, rwkv_a0, rwkv_a1, rwkv_a2,
                      rwkv_g1, rwkv_g2, rwkv_kk, rwkv_ka, rwkv_rk, rwkv_lnx_w, rwkv_lnx_b,
                      attn_wq, attn_wo, kv_norm, w_kv, rel_bias, final_norm)
    n_a = norm_w.shape[0] // 2
    nb, seq_len, _ = x_prompt.shape
    wkv0 = jnp.zeros((n_a, nb, N_HEADS, HEAD_DIM, HEAD_DIM), F32)
    shift0 = jnp.zeros((n_a, nb, D_MODEL), x_prompt.dtype)
    cache = cache_kv.reshape(cache_kv.shape[0], cache_kv.shape[1], 2 * KV_HEADS * HEAD_DIM).astype(F32)
    (y_p, wkv_p, shift_p, kv_p), (y_s, wkv_s, shift_s, kv_s) = _trunks(
        [dict(x=x_prompt, p=p_prompt, wkv0=wkv0, shift0=shift0, cache=None),
         dict(x=x_sample, p=p_sample, wkv0=state_wkv, shift0=state_shift, cache=cache)], w)
    kv_prompt = _kv_heads(kv_p[:, seq_len - min(MAX_WINDOW, seq_len):])
    return (y_p, y_s, wkv_p, shift_p, kv_prompt, wkv_s, shift_s, _kv_heads(kv_s))
```

```python
import functools

import numpy as np
import jax
import jax.numpy as jnp
from jax import lax
from jax.experimental import pallas as pl
from jax.experimental.pallas import tpu as pltpu

F32 = jnp.float32
BF16 = jnp.bfloat16

D_MODEL = 1024
D_FF = 2816
PLE_DIM = 256
RMS_EPS = 1e-6
HEAD_DIM = 64
N_HEADS = D_MODEL // HEAD_DIM
LNX_EPS = 64e-5
KV_HEADS = 4
Q_PER_KV = N_HEADS // KV_HEADS
DILATION_GROUPS = ((128, 1), (512, 4), (2048, 16))
N_GROUPS = len(DILATION_GROUPS)
MAX_WINDOW = 2048
REL_BUCKETS = 32
REL_MAX_DIST = 2048
NEG_INF = -1e30

LANES = 128
FFN_CHUNK = 256
N_FFN_CHUNKS = D_FF // FFN_CHUNK
TOKEN_TILE = 256
FFN_TILE = 512
PROJ_TILE = 1024
WKV_CHUNK = 64
RWKV_SHORT_SEQS = 8
RWKV_PARALLEL_SEQS = 2
RWKV_STEP_ROWS = 512
Q_BLOCK = 128
ATTN_SPAN = 2048
VMEM_LIMIT = 56 * 1024 * 1024
EXP_MINUS_HALF = 0.6065306597126334


def _params(*sem, n_in):
    return pltpu.CompilerParams(dimension_semantics=sem, vmem_limit_bytes=VMEM_LIMIT,
                                allow_input_fusion=[True] * n_in)


def _const_spec(shape):
    return pl.BlockSpec(shape, lambda *_: (0,) * len(shape))


def _tile(n, pref=TOKEN_TILE):
    t = min(n, pref)
    while n % t:
        t -= 8
    return t


def _rms(x, g):
    return x * lax.rsqrt(jnp.mean(x * x, axis=-1, keepdims=True) + RMS_EPS) * g


def _bdot(a, b):
    return jnp.dot(a.astype(BF16), b, preferred_element_type=F32)


def _head_sum(x, e_ref, et_ref):
    return _bdot(_bdot(x, e_ref[...]), et_ref[...])


def _ffn_kernel(*refs, bounds, with_pe, final):
    ns = len(bounds)
    xs, (g_ref, wi_ref, wo_ref), rest = refs[:ns], refs[ns:ns + 3], refs[ns + 3:]
    if with_pe:
        ps, (gp_ref, wgate_ref, wproj_ref, gf_ref), outs = rest[:ns], rest[ns:ns + 4], rest[ns + 4:]
    else:
        outs = rest

    def body(k):
        x = xs[k][...]
        xn = _rms(x, g_ref[...]).astype(BF16)
        acc = jnp.zeros_like(x)
        for j in range(N_FFN_CHUNKS):
            lo, hi = j * FFN_CHUNK, (j + 1) * FFN_CHUNK
            gate = jnp.dot(xn, wi_ref[:, lo:hi].astype(BF16), preferred_element_type=F32)
            up = jnp.dot(xn, wi_ref[:, D_FF + lo:D_FF + hi].astype(BF16), preferred_element_type=F32)
            act = (gate * jax.nn.sigmoid(gate) * up).astype(BF16)
            acc = acc + jnp.dot(act, wo_ref[lo:hi, :].astype(BF16), preferred_element_type=F32)
        y = x + 0.5 * acc
        if with_pe:
            gate = jax.nn.sigmoid(_bdot(_rms(y, gp_ref[...]), wgate_ref[...]))
            y = y + gate * _bdot(ps[k][...], wproj_ref[...])
            if final:
                y = _rms(y, gf_ref[...])
        outs[k][...] = y

    if ns == 1:
        body(0)
    else:
        i = pl.program_id(0)
        for k, (lo, hi) in enumerate(bounds):
            pl.when(jnp.logical_and(i >= lo, i < hi))(functools.partial(body, k))


def _ffn(hs, g, wi, wo, pe=None, layer=0, final=False):
    tiles = [_tile(h.shape[0], FFN_TILE) for h in hs]
    counts = [h.shape[0] // t for h, t in zip(hs, tiles)]
    bounds = [(sum(counts[:k]), sum(counts[:k + 1])) for k in range(len(hs))]

    def rows(k, width, lead=()):
        lo, cnt = bounds[k][0], counts[k]
        return pl.BlockSpec(tuple(None for _ in lead) + (tiles[k], width),
                            lambda i: lead + (jnp.clip(i - lo, 0, cnt - 1), 0))

    args = list(hs) + [g, wi, wo]
    in_specs = ([rows(k, D_MODEL) for k in range(len(hs))]
                + [_const_spec((1, D_MODEL)),
                   pl.BlockSpec((None,) + wi.shape[1:], lambda i: (layer, 0, 0)),
                   pl.BlockSpec((None,) + wo.shape[1:], lambda i: (layer, 0, 0))])
    if pe is not None:
        args += list(pe[0]) + list(pe[1:])
        in_specs += ([rows(k, PLE_DIM, (layer,)) for k in range(len(hs))]
                     + [_const_spec(a.shape) for a in pe[1:]])
    return pl.pallas_call(
        functools.partial(_ffn_kernel, bounds=tuple(bounds), with_pe=pe is not None, final=final),
        grid=(bounds[-1][1],),
        in_specs=in_specs,
        out_specs=[rows(k, D_MODEL) for k in range(len(hs))],
        out_shape=[jax.ShapeDtypeStruct((h.shape[0], D_MODEL), F32) for h in hs],
        compiler_params=_params("arbitrary", n_in=len(args)),
        name="ffn_pe" if pe is not None else "ffn",
    )(*args)


def _norm_mm_kernel(x_ref, g_ref, w_ref, o_ref):
    o_ref[...] = _bdot(_rms(x_ref[...], g_ref[...]), w_ref[...]).astype(o_ref.dtype)


def _norm_mm(h, g, w, out_dtype, col_tile=None, name="norm_mm"):
    n = h.shape[0]
    tm = _tile(n)
    nout = w.shape[1]
    tn = nout if col_tile is None else col_tile
    return pl.pallas_call(
        _norm_mm_kernel,
        grid=(n // tm, nout // tn),
        in_specs=[pl.BlockSpec((tm, D_MODEL), lambda i, j: (i, 0)), _const_spec((1, D_MODEL)),
                  pl.BlockSpec((D_MODEL, tn), lambda i, j: (0, j))],
        out_specs=pl.BlockSpec((tm, tn), lambda i, j: (i, j)),
        out_shape=jax.ShapeDtypeStruct((n, nout), out_dtype),
        compiler_params=_params("arbitrary", "arbitrary", n_in=3),
        name=name,
    )(h, g, w)


def _rwkv_layer_kernel(h_ref, sh_ref, s0_ref, nw_ref, mix_ref, wrkv_ref, w0_ref, w1_ref, w2_ref, a0_ref, a1_ref,
                       a2_ref, g1_ref, g2_ref, kk_ref, ka_ref, lnw_ref, lnb_ref, rk_ref, wo_ref, e_ref, et_ref,
                       o_ref, hn_o, st_ref,
                       state, prev, r_s, lw_s, k_s, v_s, kk_s, b_s, g_s, y_s, *, rows, n_sub, nseq, npar):
    c = pl.program_id(1)
    C = WKV_CHUNK
    C2 = 2 * C
    hp = N_HEADS // 2
    n_pairs = npar * hp
    seq_rows = rows // nseq
    short = seq_rows % C != 0
    total = rows if short else npar * rows

    def load_state(s0, base=0):
        for h in range(N_HEADS):
            lo = (h % 2) * HEAD_DIM
            state[base + h // 2] = jnp.zeros((LANES, LANES), F32)
        for h in range(N_HEADS):
            lo = (h % 2) * HEAD_DIM
            state[base + h // 2, lo:lo + HEAD_DIM, lo:lo + HEAD_DIM] = s0[h]

    def store_state(st, base=0):
        for h in range(N_HEADS):
            lo = (h % 2) * HEAD_DIM
            st[h] = state[base + h // 2, lo:lo + HEAD_DIM, lo:lo + HEAD_DIM]

    if nseq == 1:
        @pl.when(c == 0)
        def _():
            for q in range(npar):
                load_state(s0_ref.at[q], q * hp)
                prev[q:q + 1, :] = sh_ref[q]
    if short:
        for ref in (r_s, lw_s, k_s, v_s, kk_s, b_s):
            ref[rows:, :] = jnp.zeros((ref.shape[0] - rows, D_MODEL), F32)

    nw = nw_ref[...]
    h_in = h_ref[...].reshape(total, D_MODEL) if nseq == 1 else h_ref[...]
    hn = _rms(h_in, nw)
    row = lax.broadcasted_iota(jnp.int32, hn.shape, 0)
    rolled = pltpu.roll(hn, 1, 0)
    if nseq == 1:
        x_prev = rolled
        for q in range(npar):
            x_prev = jnp.where(row == q * rows, prev[q:q + 1, :], x_prev)
            last = hn[(q + 1) * rows - 1:(q + 1) * rows]
            prev[q:q + 1, :] = last
            hn_o[q] = last
    else:
        x_prev = jnp.where(lax.rem(row, seq_rows) == 0, sh_ref[...], rolled)
        hn_o[...] = hn
    xx = x_prev - hn
    mix = mix_ref[...]
    xr, xw, xk, xv, xa, xg = (hn + xx * mix[j:j + 1] for j in range(6))
    r = _bdot(xr, wrkv_ref[0])
    k = _bdot(xk, wrkv_ref[1])
    v = _bdot(xv, wrkv_ref[2])
    wl = w0_ref[...] + _bdot(jnp.tanh(_bdot(xw, w1_ref[...])), w2_ref[...])
    lw_s[0:total] = -EXP_MINUS_HALF * jax.nn.sigmoid(wl)
    a = jax.nn.sigmoid(a0_ref[...] + _bdot(_bdot(xa, a1_ref[...]), a2_ref[...]))
    g_s[...] = _bdot(jax.nn.sigmoid(_bdot(xg, g1_ref[...])), g2_ref[...])
    kkv = k * kk_ref[...]
    kk = kkv / jnp.maximum(jnp.sqrt(_head_sum(kkv * kkv, e_ref, et_ref)), 1e-12)
    r_s[0:total] = r
    k_s[0:total] = k * (1.0 + (a - 1.0) * ka_ref[...])
    v_s[0:total] = v
    kk_s[0:total] = kk
    b_s[0:total] = kk * a

    ri = lax.broadcasted_iota(jnp.int32, (C, C), 0)
    ci = lax.broadcasted_iota(jnp.int32, (C, C), 1)
    tri = (ri >= ci).astype(BF16)
    r2 = lax.broadcasted_iota(jnp.int32, (C2, C2), 0)
    c2 = lax.broadcasted_iota(jnp.int32, (C2, C2), 1)
    strict = r2 > c2
    incl = r2 >= c2
    lane = lax.broadcasted_iota(jnp.int32, (C, LANES), 1)
    head0 = lane < HEAD_DIM

    def hat(x):
        x3 = jnp.stack([x[:, p * LANES:(p + 1) * LANES] for p in range(n_pairs)])
        return jnp.concatenate([jnp.where(head0, x3, 0.0), jnp.where(head0, 0.0, x3)], axis=1)

    def mm(a, b):
        return jnp.einsum("pmk,pkn->pmn", a.astype(BF16), b.astype(BF16), preferred_element_type=F32)

    def mm_nt(a, b):
        return jnp.einsum("pmk,pnk->pmn", a.astype(BF16), b.astype(BF16), preferred_element_type=F32)

    chunk_row = lax.broadcasted_iota(jnp.int32, (C, D_MODEL), 0)

    def chunk(ref, s):
        if short:
            starts = [pl.multiple_of((s * npar + q) * seq_rows, 8) for q in range(npar)]
            return jnp.concatenate([jnp.where(chunk_row < seq_rows, ref[pl.ds(st0, C), :], 0.0) for st0 in starts],
                                   axis=1)
        return jnp.concatenate([ref[pl.ds(q * rows + s * C, C), :] for q in range(npar)], axis=1)

    def sub_chunk(s, carry):
        lw = chunk(lw_s, s)
        p1 = lw.astype(BF16)
        rem = lw - p1.astype(F32)
        p2 = rem.astype(BF16)
        p3 = (rem - p2.astype(F32)).astype(BF16)
        cw = (jnp.dot(tri, p1, preferred_element_type=F32) + jnp.dot(tri, p2, preferred_element_type=F32)
              + jnp.dot(tri, p3, preferred_element_type=F32))
        cw_end = cw[C - 1:C, :]
        e_neg = jnp.exp(-cw)
        e_end = jnp.exp(cw_end - cw)
        kk_c = chunk(kk_s, s)
        bb = chunk(b_s, s)
        k_c = chunk(k_s, s)
        a_all = kk_c * jnp.exp(cw - lw)
        r_all = chunk(r_s, s) * jnp.exp(cw)
        b_all = bb * e_neg
        k_all = k_c * e_neg
        bd_all = bb * e_end
        kd_all = k_c * e_end
        v_all = chunk(v_s, s)
        decay = jnp.exp(cw_end)
        ar_h = jnp.concatenate([hat(a_all), hat(r_all)], axis=1)
        bk_h = jnp.concatenate([hat(b_all), hat(k_all)], axis=1)
        v_h = hat(v_all)
        st = state[...]
        g = mm_nt(ar_h, bk_h)
        low = jnp.where(strict, g[:, :C2, :C2], 0.0)
        ak = jnp.where(strict, g[:, :C2, C2:], 0.0)
        rbk = jnp.concatenate([jnp.where(incl, g[:, C2:, :C2], 0.0), jnp.where(incl, g[:, C2:, C2:], 0.0)], axis=2)
        ss = mm_nt(ar_h, st)
        x = -(ss[:, :C2] + mm(ak, v_h))
        t = mm(low, jnp.concatenate([low, x], axis=2))
        lp = t[:, :, :C2]
        x = x - t[:, :, C2:]
        for _ in range(max(int(np.ceil(np.log2(min(C, seq_rows)))), 2) - 2):
            t = mm(lp, jnp.concatenate([lp, x], axis=2))
            lp = t[:, :, :C2]
            x = x + t[:, :, C2:]
        x = x + mm(lp, x)
        xv = jnp.concatenate([x, v_h], axis=1)
        y_h = ss[:, C2:] + mm(rbk, xv)
        y = y_h[:, :C] + y_h[:, C:]
        for p in range(n_pairs):
            q, pl_ = divmod(p, hp)
            if short:
                first = pl.multiple_of((s * npar + q) * seq_rows, 8)
                y_s[pl.ds(first, seq_rows), pl_ * LANES:(pl_ + 1) * LANES] = y[p, :seq_rows]
            else:
                y_s[pl.ds(q * rows + s * C, C), pl_ * LANES:(pl_ + 1) * LANES] = y[p]
        bkd_h = jnp.concatenate([hat(bd_all), hat(kd_all)], axis=1)
        xv_t = jnp.stack([xv[p].T for p in range(n_pairs)])
        dec3 = jnp.stack([decay[:, p * LANES:(p + 1) * LANES] for p in range(n_pairs)])
        state[...] = st * dec3 + mm(xv_t, bkd_h)
        return carry

    if nseq == 1:
        for s in range(n_sub):
            sub_chunk(s, 0)
    else:
        def one_group(s, carry):
            for q in range(npar):
                load_state(s0_ref.at[s * npar + q], q * hp)
            sub_chunk(s, carry)
            for q in range(npar):
                store_state(st_ref.at[s * npar + q], q * hp)
            return carry

        lax.fori_loop(0, nseq // npar, one_group, 0)

    y = y_s[0:total]
    inv_n = 1.0 / HEAD_DIM
    mu = _head_sum(y, e_ref, et_ref) * inv_n
    yc = y - mu
    var = _head_sum(yc * yc, e_ref, et_ref) * inv_n
    yn = yc * lax.rsqrt(var + LNX_EPS) * lnw_ref[...] + lnb_ref[...]
    bonus = _head_sum(r_s[0:total] * k_s[0:total] * rk_ref[...], e_ref, et_ref) * v_s[0:total]
    out = h_in + _bdot((yn + bonus) * g_s[...], wo_ref[...])
    o_ref[...] = out.reshape(o_ref.shape)

    if nseq == 1:
        @pl.when(c == pl.num_programs(1) - 1)
        def _():
            for q in range(npar):
                store_state(st_ref.at[q], q * hp)


def _rwkv_layer(h, shift, s0, seq_len, lw):
    n = h.shape[0]
    nb = n // seq_len
    if seq_len % WKV_CHUNK == 0:
        npar = RWKV_PARALLEL_SEQS if nb % RWKV_PARALLEL_SEQS == 0 else 1
        nseq, rows = 1, _tile(seq_len, RWKV_STEP_ROWS // npar)
        steps, n_sub, total = seq_len // rows, rows // WKV_CHUNK, npar * rows
        buf_rows = total
        assert rows % WKV_CHUNK == 0
        h_in = h.reshape(nb, seq_len, D_MODEL)
        row_spec = pl.BlockSpec((npar, rows, D_MODEL), lambda bi, ci: (bi, ci, 0))
        sh_in = shift.reshape(nb, 1, D_MODEL)
        sh_spec = pl.BlockSpec((npar, 1, D_MODEL), lambda bi, ci: (bi, 0, 0))
        hn_shape, hn_spec = jax.ShapeDtypeStruct((nb, 1, D_MODEL), F32), sh_spec
        nst = npar
    else:
        assert seq_len < WKV_CHUNK and seq_len % 8 == 0
        nseq = _tile(nb, RWKV_SHORT_SEQS)
        npar = RWKV_PARALLEL_SEQS if nseq % RWKV_PARALLEL_SEQS == 0 else 1
        rows, steps, n_sub = nseq * seq_len, 1, 1
        total = rows
        buf_rows = rows + WKV_CHUNK - seq_len
        h_in = h
        row_spec = pl.BlockSpec((rows, D_MODEL), lambda bi, ci: (bi, 0))
        sh_in = jnp.repeat(shift, seq_len, axis=0)
        sh_spec = row_spec
        hn_shape, hn_spec = jax.ShapeDtypeStruct((n, D_MODEL), F32), sh_spec
        nst = nseq
    st_spec = pl.BlockSpec((nst, N_HEADS, HEAD_DIM, HEAD_DIM), lambda bi, ci: (bi, 0, 0, 0))
    consts = [lw["nw"], lw["mix"], lw["wrkv"], lw["w0"], lw["w1"], lw["w2"], lw["a0"], lw["a1"], lw["a2"],
              lw["g1"], lw["g2"], lw["kk"], lw["ka"], lw["lnw"], lw["lnb"], lw["rk"], lw["wo"], lw["e"], lw["et"]]
    seq_buf = pltpu.VMEM((buf_rows, D_MODEL), F32)
    out, hn, st = pl.pallas_call(
        functools.partial(_rwkv_layer_kernel, rows=rows, n_sub=n_sub, nseq=nseq, npar=npar),
        grid=(nb // nst, steps),
        in_specs=[row_spec, sh_spec, st_spec] + [_const_spec(c.shape) for c in consts],
        out_specs=[row_spec, hn_spec, st_spec],
        out_shape=[jax.ShapeDtypeStruct(h_in.shape, F32), hn_shape, jax.ShapeDtypeStruct(s0.shape, F32)],
        scratch_shapes=[pltpu.VMEM((npar * N_HEADS // 2, LANES, LANES), F32), pltpu.VMEM((8, D_MODEL), F32)]
                       + [seq_buf] * 6 + [pltpu.VMEM((total, D_MODEL), F32), seq_buf],
        compiler_params=_params("arbitrary", "arbitrary", n_in=3 + len(consts)),
        name="rwkv_layer",
    )(h_in, sh_in, s0, *consts)
    out = out.reshape(n, D_MODEL)
    shift_out = hn.reshape(nb, D_MODEL) if nseq == 1 else hn.reshape(nb, seq_len, D_MODEL)[:, -1]
    return out, shift_out, st


def _t5_buckets(dist):
    d = np.asarray(dist, dtype=np.int64)
    max_exact = REL_BUCKETS // 2
    large = max_exact + (np.log(np.maximum(d, 1) / max_exact) / np.log(REL_MAX_DIST / max_exact)
                         * (REL_BUCKETS - max_exact)).astype(np.int32)
    large = np.minimum(large, REL_BUCKETS - 1)
    return np.where(d < max_exact, d, large).astype(np.int32)


def _bucket_rows(table, buckets):
    onehot = jnp.asarray(np.eye(REL_BUCKETS, dtype=np.float32)[buckets])
    return jnp.dot(onehot, table.astype(F32), precision=lax.Precision.HIGHEST)


def _toeplitz(tab, n_rows, n_cols):
    period = tab.shape[-1]
    assert period >= n_rows + n_cols - 1 and n_cols <= period - 1
    lead = tab.shape[:-1]
    flat = jnp.broadcast_to(tab[..., None, :], lead + (n_rows, period)).reshape(lead + (-1,))
    skew = flat[..., :n_rows * (period - 1)].reshape(lead + (n_rows, period - 1))
    return skew[..., :n_cols]


def _band_bias(rel_bias, group):
    win, dil = DILATION_GROUPS[group]
    assert win // dil == Q_BLOCK
    period = 3 * Q_BLOCK - 1
    idx = np.arange(period)
    m = Q_BLOCK - np.where(idx < 2 * Q_BLOCK, idx, idx - period)
    valid = (m >= 0) & (m <= Q_BLOCK)
    buckets = _t5_buckets(dil * np.clip(m, 0, Q_BLOCK))
    tbl = _bucket_rows(rel_bias[:, group * N_HEADS:(group + 1) * N_HEADS], buckets).T
    general = _toeplitz(jnp.where(valid[None], tbl, NEG_INF), Q_BLOCK, 2 * Q_BLOCK)
    first = jnp.where((np.arange(2 * Q_BLOCK) >= Q_BLOCK)[None, None, :], general, NEG_INF)
    return jnp.stack([first, general])


_HEAD_OF_SEG = np.array([4 * (2 * (s // 8) + s % 2) + (s // 2) % 4 for s in range(N_HEADS)])


def _proj_rm_kernel(x_ref, g_ref, w_ref, *refs, natural, lane_ranges, tm):
    scr = refs[-1]
    outs = refs[:-1]
    y = _bdot(_rms(x_ref[...], g_ref[...]), w_ref[...])
    if natural:
        outs[0][...] = y
        outs = outs[1:]
    for c in range(scr.shape[0]):
        scr[c] = y[:, c * LANES:(c + 1) * LANES]
    for gi, (_, dil) in enumerate(DILATION_GROUPS):
        lo, hi = lane_ranges[gi]
        if dil == 1:
            outs[gi][0] = y[:, lo:hi].astype(BF16)
            continue
        for rho in range(dil):
            rows = [scr[c, pl.ds(rho, tm // dil, stride=dil), :] for c in range(lo // LANES, hi // LANES)]
            outs[gi][rho] = jnp.concatenate(rows, axis=1).astype(BF16)


def _proj_rm(h, g, w, nb, seq_len, natural, lane_ranges, name):
    n = h.shape[0]
    tm = _tile(n, PROJ_TILE)
    nout = w.shape[1]
    assert seq_len % tm == 0 and all(tm % (16 * dil) == 0 for _, dil in DILATION_GROUPS)
    tps = seq_len // tm
    out_shape, out_specs = [], []
    if natural:
        out_shape.append(jax.ShapeDtypeStruct((n, nout), F32))
        out_specs.append(pl.BlockSpec((tm, nout), lambda i: (i, 0)))
    for (_, dil), (lo, hi) in zip(DILATION_GROUPS, lane_ranges):
        out_shape.append(jax.ShapeDtypeStruct((nb, dil, seq_len // dil, hi - lo), BF16))
        out_specs.append(pl.BlockSpec((None, dil, tm // dil, hi - lo), lambda i: (i // tps, 0, i % tps, 0)))
    return pl.pallas_call(
        functools.partial(_proj_rm_kernel, natural=natural, lane_ranges=lane_ranges, tm=tm),
        grid=(n // tm,),
        in_specs=[pl.BlockSpec((tm, D_MODEL), lambda i: (i, 0)), _const_spec((1, D_MODEL)), _const_spec(w.shape)],
        out_specs=out_specs,
        out_shape=out_shape,
        scratch_shapes=[pltpu.VMEM((nout // LANES, tm, LANES), F32)],
        compiler_params=_params("arbitrary", n_in=3),
        name=name,
    )(h, g, w)


def _attn_kernel(q_ref, kv_ref, halo_ref, bias_ref, o_ref, lse_ref, kvbuf, *, nq):
    i = pl.program_id(0)
    dil = q_ref.shape[0]
    kvd = KV_HEADS * HEAD_DIM
    nt = (((1,), (1,)), ((), ()))
    kvbuf[:, :Q_BLOCK, :] = halo_ref[...]
    kvbuf[:, Q_BLOCK:, :] = kv_ref[...]
    lane = lax.broadcasted_iota(jnp.int32, (Q_BLOCK, LANES), 1)
    low_half = lane < HEAD_DIM
    lane_row = lax.broadcasted_iota(jnp.int32, (1, LANES), 1)
    keep_lo = (lane_row < HEAD_DIM).astype(BF16)
    keep_hi = (lane_row >= HEAD_DIM).astype(BF16)

    def block(u, carry):
        rho = lax.div(u, nq)
        j = u - rho * nq
        r0 = pl.multiple_of(j * Q_BLOCK, Q_BLOCK)
        qb = q_ref[rho, pl.ds(r0, Q_BLOCK), :]
        kvb = kvbuf[rho, pl.ds(r0, 2 * Q_BLOCK), :]
        bsel = jnp.where(jnp.logical_and(i == 0, j == 0), 0, 1)
        lse_tile = jnp.zeros((Q_BLOCK, LANES), F32)
        for G in range(KV_HEADS // 2):
            kg = kvb[:, G * LANES:(G + 1) * LANES]
            vg = kvb[:, kvd + G * LANES:kvd + (G + 1) * LANES]
            vcat = jnp.concatenate([vg * keep_lo, vg * keep_hi], axis=0)
            pieces = []
            for r in range(Q_PER_KV):
                qg = qb[:, (G * Q_PER_KV + r) * LANES:(G * Q_PER_KV + r + 1) * LANES]
                pieces += [qg * keep_lo, qg * keep_hi]
            s_all = lax.dot_general(jnp.concatenate(pieces, axis=0), kg, nt, preferred_element_type=F32)
            for r in range(Q_PER_KV):
                grp = G * Q_PER_KV + r
                parts = []
                for half in range(2):
                    s = s_all[(2 * r + half) * Q_BLOCK:(2 * r + half + 1) * Q_BLOCK]
                    s = s + bias_ref[bsel, int(_HEAD_OF_SEG[2 * grp + half])]
                    m = jnp.max(s, axis=-1, keepdims=True)
                    p = jnp.exp(s - m)
                    l = jnp.sum(p, axis=-1, keepdims=True)
                    parts.append((p.astype(BF16), l, m + jnp.log(l)))
                (p0, l0, e0), (p1, l1, e1) = parts
                o = jnp.dot(jnp.concatenate([p0, p1], axis=1), vcat, preferred_element_type=F32)
                o = (o / jnp.where(low_half, l0, l1)).astype(o_ref.dtype)
                o_ref[rho, pl.ds(r0, Q_BLOCK), grp * LANES:(grp + 1) * LANES] = o
                lse_tile = jnp.where(lane == 2 * grp, e0, lse_tile)
                lse_tile = jnp.where(lane == 2 * grp + 1, e1, lse_tile)
        lse_ref[rho, pl.ds(r0, Q_BLOCK), :] = lse_tile
        return carry

    lax.fori_loop(0, dil * nq, block, 0)


def _attn_group(q_rm, kv_rm, bias):
    nb, dil, tsub, _ = q_rm.shape
    kvw = kv_rm.shape[-1]
    rows = ATTN_SPAN // dil
    nq = rows // Q_BLOCK
    assert tsub % rows == 0 and nq >= 1
    span_spec = lambda width: pl.BlockSpec((None, dil, rows, width), lambda i, b: (b, 0, i, 0))
    return pl.pallas_call(
        functools.partial(_attn_kernel, nq=nq),
        grid=(tsub // rows, nb),
        in_specs=[span_spec(D_MODEL), span_spec(kvw),
                  pl.BlockSpec((None, dil, Q_BLOCK, kvw), lambda i, b: (b, 0, jnp.maximum(i * nq - 1, 0), 0)),
                  _const_spec(bias.shape)],
        out_specs=[span_spec(D_MODEL), span_spec(LANES)],
        out_shape=[jax.ShapeDtypeStruct((nb, dil, tsub, D_MODEL), BF16),
                   jax.ShapeDtypeStruct((nb, dil, tsub, LANES), F32)],
        scratch_shapes=[pltpu.VMEM((dil, Q_BLOCK + rows, kvw), BF16)],
        compiler_params=_params("arbitrary", "arbitrary", n_in=4),
        name=f"attn_d{dil}",
    )(q_rm, kv_rm, kv_rm, bias)


def _attn_out_kernel(o0_ref, o1_ref, o2_ref, l0_ref, l1_ref, l2_ref, h_ref, wo_ref, et_ref, out_ref, *scr, tm):
    outs, lses = [], []
    for gi, (o_ref, l_ref) in enumerate(((o0_ref, l0_ref), (o1_ref, l1_ref), (o2_ref, l2_ref))):
        dil = DILATION_GROUPS[gi][1]
        if dil == 1:
            outs.append(o_ref[0].astype(F32))
            lses.append(l_ref[0])
            continue
        so, sl = scr[2 * gi], scr[2 * gi + 1]
        n_tiles = so.shape[0]
        for rho in range(dil):
            rows = pl.ds(rho, tm // dil, stride=dil)
            for c in range(n_tiles):
                so[c, rows, :] = o_ref[rho, :, c * LANES:(c + 1) * LANES].astype(F32)
            sl[rows, :] = l_ref[rho]
        outs.append(jnp.concatenate([so[c] for c in range(n_tiles)], axis=1))
        lses.append(sl[...])
    l0, l1, l2 = lses
    m = jnp.maximum(jnp.maximum(l0, l1), l2)
    w0, w1, w2 = jnp.exp(l0 - m), jnp.exp(l1 - m), jnp.exp(l2 - m)
    inv = 1.0 / (w0 + w1 + w2)
    et = et_ref[...]
    att = _bdot(w0 * inv, et) * outs[0] + _bdot(w1 * inv, et) * outs[1] + _bdot(w2 * inv, et) * outs[2]
    out_ref[...] = h_ref[...] + _bdot(att, wo_ref[...])


def _attn_out(outs, lses, h, wo, et, seq_len):
    n = h.shape[0]
    tm = _tile(n, PROJ_TILE)
    tps = seq_len // tm
    row_spec = pl.BlockSpec((tm, D_MODEL), lambda i: (i, 0))

    def rm_spec(dil, width):
        return pl.BlockSpec((None, dil, tm // dil, width), lambda i: (i // tps, 0, i % tps, 0))

    dils = [dil for _, dil in DILATION_GROUPS]
    scratch = []
    for _ in dils:
        scratch += [pltpu.VMEM((D_MODEL // LANES, tm, LANES), F32), pltpu.VMEM((tm, LANES), F32)]
    return pl.pallas_call(
        functools.partial(_attn_out_kernel, tm=tm),
        grid=(n // tm,),
        in_specs=[rm_spec(d, D_MODEL) for d in dils] + [rm_spec(d, LANES) for d in dils]
                 + [row_spec, _const_spec(wo.shape), _const_spec(et.shape)],
        out_specs=row_spec,
        out_shape=jax.ShapeDtypeStruct((n, D_MODEL), F32),
        scratch_shapes=scratch,
        compiler_params=_params("arbitrary", n_in=9),
        name="attn_out",
    )(*outs, *lses, h, wo, et)


def _decode_bias(rel_bias, seq_len, cache_len):
    ncol = cache_len + LANES
    period = seq_len + ncol - 1
    idx = np.arange(period)
    dist = cache_len - np.where(idx < ncol, idx, idx - period)
    buckets = _t5_buckets(np.clip(dist, 0, MAX_WINDOW))
    tabs = []
    for g, (win, dil) in enumerate(DILATION_GROUPS):
        valid = (dist >= 0) & (dist % dil == 0) & (dist <= win)
        tbl = _bucket_rows(rel_bias[:, g * N_HEADS:(g + 1) * N_HEADS], buckets).T
        tabs.append(jnp.where(valid[None], tbl, NEG_INF))
    rows = _toeplitz(jnp.stack(tabs), seq_len, ncol)
    bias = rows.reshape(N_GROUPS * N_HEADS * seq_len, ncol)
    return bias[:, :cache_len], bias[:, cache_len:]


def _attn_decode_kernel(q_ref, cache_ref, kvn_ref, h_ref, bc_ref, bn_ref, wo_ref, out_ref, *, seq_len):
    kvd = KV_HEADS * HEAD_DIM
    nslot = N_GROUPS * N_HEADS
    rows_g = N_HEADS * seq_len
    nt = (((1,), (1,)), ((), ()))
    cache = cache_ref[...]
    kc = cache[:, :kvd].astype(BF16)
    vc = cache[:, kvd:].astype(BF16)
    kvn = kvn_ref[...]
    pad = jnp.zeros((LANES - seq_len, kvd), F32)
    kn = jnp.concatenate([kvn[:, :kvd], pad], axis=0).astype(BF16)
    vn = jnp.concatenate([kvn[:, kvd:], pad], axis=0).astype(BF16)
    lhs = jnp.concatenate([q_ref[:, s * kvd:(s + 1) * kvd] for s in range(nslot)], axis=0).astype(BF16)
    cache_len = kc.shape[0]
    sn = lax.dot_general(lhs, kn, nt, preferred_element_type=F32) + bn_ref[...]
    scs, m = [], None
    for g, (win, _) in enumerate(DILATION_GROUPS):
        c0 = (cache_len - min(win, cache_len)) // LANES * LANES
        rows = slice(g * rows_g, (g + 1) * rows_g)
        sc = lax.dot_general(lhs[rows], kc[c0:], nt, preferred_element_type=F32) + bc_ref[rows, c0:]
        scs.append((sc, c0, rows))
        m_g = jnp.maximum(jnp.max(sc, axis=-1, keepdims=True), jnp.max(sn[rows], axis=-1, keepdims=True))
        m = m_g if m is None else jnp.maximum(m, m_g)
    l = jnp.zeros((rows_g, 1), F32)
    num = jnp.zeros((rows_g, kvd), F32)
    for sc, c0, rows in scs:
        pc = jnp.exp(sc - m)
        pn = jnp.exp(sn[rows] - m)
        l = l + jnp.sum(pc, axis=-1, keepdims=True) + jnp.sum(pn, axis=-1, keepdims=True)
        num = (num + jnp.dot(pc.astype(BF16), vc[c0:], preferred_element_type=F32)
               + jnp.dot(pn.astype(BF16), vn, preferred_element_type=F32))
    row = lax.broadcasted_iota(jnp.int32, (rows_g, kvd), 0)
    lane = lax.broadcasted_iota(jnp.int32, (rows_g, kvd), 1)
    own = (row // (Q_PER_KV * seq_len)) == (lane // HEAD_DIM)
    att = jnp.where(own, num / l, 0.0)
    out = h_ref[...]
    for r in range(Q_PER_KV):
        a_r = att[r * seq_len:(r + 1) * seq_len]
        for c in range(1, KV_HEADS):
            a_r = a_r + att[(c * Q_PER_KV + r) * seq_len:(c * Q_PER_KV + r + 1) * seq_len]
        out = out + _bdot(a_r, wo_ref[r])
    out_ref[...] = out


def _attn_decode(q, cache, kv_new, h, bias_c, bias_n, wo_r, nb, seq_len):
    cache_len = cache.shape[1]
    qw = q.shape[1]
    kvw = 2 * KV_HEADS * HEAD_DIM
    return pl.pallas_call(
        functools.partial(_attn_decode_kernel, seq_len=seq_len),
        grid=(nb,),
        in_specs=[pl.BlockSpec((seq_len, qw), lambda b: (b, 0)),
                  pl.BlockSpec((None, cache_len, kvw), lambda b: (b, 0, 0)),
                  pl.BlockSpec((seq_len, kvw), lambda b: (b, 0)),
                  pl.BlockSpec((seq_len, D_MODEL), lambda b: (b, 0)),
                  _const_spec(bias_c.shape), _const_spec(bias_n.shape), _const_spec(wo_r.shape)],
        out_specs=pl.BlockSpec((seq_len, D_MODEL), lambda b: (b, 0)),
        out_shape=jax.ShapeDtypeStruct((nb * seq_len, D_MODEL), F32),
        compiler_params=_params("arbitrary", n_in=7),
        name="attn_decode",
    )(q, cache, kv_new, h, bias_c, bias_n, wo_r)


def _prep_weights(norm_w, ffn1_wi, ffn1_wo, ffn2_wi, ffn2_wo, pe_proj, pe_gate,
                  rwkv_mix, rwkv_wrkv, rwkv_wo, rwkv_w0, rwkv_w1, rwkv_w2, rwkv_a0, rwkv_a1, rwkv_a2,
                  rwkv_g1, rwkv_g2, rwkv_kk, rwkv_ka, rwkv_rk, rwkv_lnx_w, rwkv_lnx_b,
                  attn_wq, attn_wo, kv_norm, w_kv, rel_bias, final_norm):
    def row(v):
        return v.reshape(1, -1).astype(F32)

    def pad_cols(w, n):
        return jnp.pad(w, ((0, 0), (0, n - w.shape[1]))).astype(BF16)

    def pad_rows(w, n):
        return jnp.pad(w, ((0, n - w.shape[0]), (0, 0))).astype(BF16)

    head_of_lane = np.arange(D_MODEL) // HEAD_DIM
    e = jnp.asarray(head_of_lane[:, None] == np.arange(LANES)[None, :], BF16)
    et = jnp.asarray(np.arange(LANES)[:, None] == head_of_lane[None, :], BF16)

    depth = norm_w.shape[0]
    layers = []
    for i in range(depth):
        layers.append(dict(
            nw=[row(norm_w[i, j]) for j in range(4)],
            ffn1=(ffn1_wi.astype(F32), ffn1_wo.astype(F32)), ffn2=(ffn2_wi.astype(F32), ffn2_wo.astype(F32)),
            pe_gate=pe_gate[i].astype(BF16), pe_proj=pe_proj[i].astype(BF16)))
    n_a = depth // 2
    rw = []
    for i in range(n_a):
        rw.append(dict(
            nw=row(norm_w[i, 1]), mix=rwkv_mix[i].astype(F32), wrkv=rwkv_wrkv[i].astype(BF16),
            w0=row(rwkv_w0[i]), w1=pad_cols(rwkv_w1[i], LANES), w2=pad_rows(rwkv_w2[i], LANES),
            a0=row(rwkv_a0[i]), a1=pad_cols(rwkv_a1[i], LANES), a2=pad_rows(rwkv_a2[i], LANES),
            g1=pad_cols(rwkv_g1[i], 2 * LANES), g2=pad_rows(rwkv_g2[i], 2 * LANES),
            kk=row(rwkv_kk[i]), ka=row(rwkv_ka[i]), rk=row(rwkv_rk[i]),
            lnw=row(rwkv_lnx_w[i]), lnb=row(rwkv_lnx_b[i]), wo=rwkv_wo[i].astype(BF16), e=e, et=et))
    scale = HEAD_DIM ** -0.5
    at = []
    for j in range(depth - n_a):
        wq = attn_wq[j] * scale
        wo_r = attn_wo[j].reshape(KV_HEADS, Q_PER_KV, HEAD_DIM, D_MODEL).transpose(1, 0, 2, 3)
        wo_r = wo_r.reshape(Q_PER_KV, KV_HEADS * HEAD_DIM, D_MODEL).astype(BF16)
        wq_seg = wq.reshape(D_MODEL, N_GROUPS, 2, 2, Q_PER_KV, HEAD_DIM).transpose(0, 1, 2, 4, 3, 5)
        wq_seg = wq_seg.reshape(D_MODEL, -1)
        wo_seg = attn_wo[j].reshape(2, 2, Q_PER_KV, HEAD_DIM, D_MODEL).transpose(0, 2, 1, 3, 4)
        wo_seg = wo_seg.reshape(D_MODEL, D_MODEL)
        at.append(dict(wq_seg=wq_seg.astype(BF16), wo_seg=wo_seg.astype(BF16), wo_r=wo_r))
    return dict(layers=layers, rwkv=rw, attn=at, kv_norm=row(kv_norm), w_kv=w_kv.astype(BF16),
                final_norm=row(final_norm), rel_bias=rel_bias, et=et)


def _trunks(streams, w):
    depth = len(w["layers"])
    n_a = depth // 2
    kvw = 2 * KV_HEADS * HEAD_DIM
    st = []
    for sd in streams:
        nb, seq_len, _ = sd["x"].shape
        n = nb * seq_len
        st.append(dict(nb=nb, seq_len=seq_len, n=n, h=sd["x"].reshape(n, D_MODEL).astype(F32),
                       p=sd["p"].reshape(depth, n, PLE_DIM).astype(F32), wkv_out=[], shift_out=[],
                       kv_rows=None, kv_rm=None, **{k: sd[k] for k in ("wkv0", "shift0", "cache")}))

    def ffn(nw, weights, **kw):
        for t, h in zip(st, _ffn([t["h"] for t in st], nw, *weights, **kw)):
            t["h"] = h

    for i in range(depth):
        lw = w["layers"][i]
        if i == n_a:
            for t in st:
                if t["cache"] is None:
                    t["kv_rows"], *t["kv_rm"] = _proj_rm(t["h"], w["kv_norm"], w["w_kv"], t["nb"], t["seq_len"], True,
                                                         [(0, kvw)] * N_GROUPS, "kv_proj")
                else:
                    t["kv_rows"] = _norm_mm(t["h"], w["kv_norm"], w["w_kv"], F32, name="kv_proj")
        ffn(lw["nw"][0], lw["ffn1"], layer=i)
        for t in st:
            nb, seq_len, n, h = t["nb"], t["seq_len"], t["n"], t["h"]
            if i < n_a:
                h, sh, state = _rwkv_layer(h, t["shift0"][i].astype(F32), t["wkv0"][i].astype(F32), seq_len,
                                           w["rwkv"][i])
                t["wkv_out"].append(state)
                t["shift_out"].append(sh)
            elif t["cache"] is None:
                al = w["attn"][i - n_a]
                q_rm = _proj_rm(h, lw["nw"][1], al["wq_seg"], nb, seq_len, False,
                                [(gi * D_MODEL, (gi + 1) * D_MODEL) for gi in range(N_GROUPS)], "q_proj")
                outs, lses = [], []
                for gi in range(N_GROUPS):
                    o, lse = _attn_group(q_rm[gi], t["kv_rm"][gi], _band_bias(w["rel_bias"], gi))
                    outs.append(o)
                    lses.append(lse)
                h = _attn_out(outs, lses, h, al["wo_seg"], w["et"], seq_len)
            else:
                al = w["attn"][i - n_a]
                q = _norm_mm(h, lw["nw"][1], al["wq_seg"], F32, name="q_proj_decode")
                q = q.reshape(n, N_GROUPS, 2, Q_PER_KV, 2, HEAD_DIM).transpose(0, 1, 2, 4, 3, 5)
                q = q.reshape(n, N_GROUPS, KV_HEADS, Q_PER_KV, 1, HEAD_DIM)
                q = q * jnp.eye(KV_HEADS, dtype=F32).reshape(1, 1, KV_HEADS, 1, KV_HEADS, 1)
                q = q.reshape(n, N_GROUPS * N_HEADS * KV_HEADS * HEAD_DIM)
                bias_c, bias_n = _decode_bias(w["rel_bias"], seq_len, t["cache"].shape[1])
                h = _attn_decode(q, t["cache"], t["kv_rows"], h, bias_c, bias_n, al["wo_r"], nb, seq_len)
            t["h"] = h
        pe = ([t["p"] for t in st], lw["nw"][3], lw["pe_gate"], lw["pe_proj"], w["final_norm"])
        ffn(lw["nw"][2], lw["ffn2"], pe=pe, layer=i, final=(i == depth - 1))
    results = []
    for t, sd in zip(st, streams):
        dt = sd["x"].dtype
        y = t["h"].reshape(t["nb"], t["seq_len"], D_MODEL).astype(dt)
        kv_rows = t["kv_rows"].reshape(t["nb"], t["seq_len"], kvw).astype(dt)
        results.append((y, jnp.stack(t["wkv_out"]).astype(dt), jnp.stack(t["shift_out"]).astype(dt), kv_rows))
    return results


def _kv_heads(kv):
    return kv.reshape(kv.shape[:2] + (2, KV_HEADS, HEAD_DIM))


def _trunk(x, p, wkv0, shift0, cache, w):
    y, wkv, shift, kv = _trunks([dict(x=x, p=p, wkv0=wkv0, shift0=shift0, cache=cache)], w)[0]
    return y, wkv, shift, _kv_heads(kv)


def kernel(x_prompt, x_sample, state_wkv, state_shift, cache_kv, p_prompt, p_sample, norm_w, ffn1_wi, ffn1_wo, ffn2_wi, ffn2_wo, pe_proj, pe_gate, rwkv_mix, rwkv_wrkv, rwkv_wo, rwkv_w0, rwkv_w1, rwkv_w2, rwkv_a0, rwkv_a1, rwkv_a2, rwkv_g1, rwkv_g2, rwkv_kk, rwkv_ka, rwkv_rk, rwkv_lnx_w, rwkv_lnx_b, attn_wq, attn_wo, kv_norm, w_kv, rel_bias, final_norm):
    w = _prep_weights(norm_w, ffn1_wi, ffn1_wo, ffn2_wi, ffn2_wo, pe_proj, pe_gate,
                      rwkv_mix, rwkv_wrkv, rwkv_wo, rwkv_w0, rwkv_w1, rwkv_w2, rwkv_a0, rwkv_a1, rwkv_a2,
                      rwkv_g1, rwkv_g2, rwkv_kk, rwkv_ka, rwkv_rk, rwkv_lnx_w, rwkv_lnx_b,
                      attn_wq, attn_wo, kv_norm, w_kv, rel_bias, final_norm)
    n_a = norm_w.shape[0] // 2
    nb, seq_len, _ = x_prompt.shape
    wkv0 = jnp.zeros((n_a, nb, N_HEADS, HEAD_DIM, HEAD_DIM), F32)
    shift0 = jnp.zeros((n_a, nb, D_MODEL), x_prompt.dtype)
    cache = cache_kv.reshape(cache_kv.shape[0], cache_kv.shape[1], 2 * KV_HEADS * HEAD_DIM).astype(F32)
    (y_p, wkv_p, shift_p, kv_p), (y_s, wkv_s, shift_s, kv_s) = _trunks(
        [dict(x=x_prompt, p=p_prompt, wkv0=wkv0, shift0=shift0, cache=None),
         dict(x=x_sample, p=p_sample, wkv0=state_wkv, shift0=state_shift, cache=cache)], w)
    kv_prompt = _kv_heads(kv_p[:, seq_len - min(MAX_WINDOW, seq_len):])
    return (y_p, y_s, wkv_p, shift_p, kv_prompt, wkv_s, shift_s, _kv_heads(kv_s))
```

```python
import functools

import numpy as np
import jax
import jax.numpy as jnp
from jax import lax
from jax.experimental import pallas as pl
from jax.experimental.pallas import tpu as pltpu

F32 = jnp.float32
BF16 = jnp.bfloat16

D_MODEL = 1024
D_FF = 2816
PLE_DIM = 256
RMS_EPS = 1e-6
HEAD_DIM = 64
N_HEADS = D_MODEL // HEAD_DIM
LNX_EPS = 64e-5
KV_HEADS = 4
Q_PER_KV = N_HEADS // KV_HEADS
DILATION_GROUPS = ((128, 1), (512, 4), (2048, 16))
N_GROUPS = len(DILATION_GROUPS)
MAX_WINDOW = 2048
REL_BUCKETS = 32
REL_MAX_DIST = 2048
NEG_INF = -1e30

LANES = 128
FFN_CHUNK = 256
N_FFN_CHUNKS = D_FF // FFN_CHUNK
TOKEN_TILE = 256
FFN_TILE = 512
PROJ_TILE = 1024
WKV_CHUNK = 64
RWKV_SHORT_SEQS = 8
RWKV_PARALLEL_SEQS = 2
RWKV_SHORT_PARALLEL = 4
RWKV_STEP_ROWS = 512
Q_BLOCK = 128
ATTN_SPAN = 2048
VMEM_LIMIT = 56 * 1024 * 1024
EXP_MINUS_HALF = 0.6065306597126334


def _params(*sem, n_in):
    return pltpu.CompilerParams(dimension_semantics=sem, vmem_limit_bytes=VMEM_LIMIT,
                                allow_input_fusion=[True] * n_in)


def _const_spec(shape):
    return pl.BlockSpec(shape, lambda *_: (0,) * len(shape))


def _tile(n, pref=TOKEN_TILE):
    t = min(n, pref)
    while n % t:
        t -= 8
    return t


def _rms(x, g):
    return x * lax.rsqrt(jnp.mean(x * x, axis=-1, keepdims=True) + RMS_EPS) * g


def _bdot(a, b):
    return jnp.dot(a.astype(BF16), b, preferred_element_type=F32)


def _head_sum(x, e_ref, et_ref):
    return _bdot(_bdot(x, e_ref[...]), et_ref[...])


def _ffn_kernel(*refs, bounds, with_pe, final):
    ns = len(bounds)
    xs, (g_ref, wi_ref, wo_ref), rest = refs[:ns], refs[ns:ns + 3], refs[ns + 3:]
    if with_pe:
        ps, (gp_ref, wgate_ref, wproj_ref, gf_ref), outs = rest[:ns], rest[ns:ns + 4], rest[ns + 4:]
    else:
        outs = rest

    def body(k):
        x = xs[k][...]
        xn = _rms(x, g_ref[...]).astype(BF16)
        acc = jnp.zeros_like(x)
        for j in range(N_FFN_CHUNKS):
            lo, hi = j * FFN_CHUNK, (j + 1) * FFN_CHUNK
            gate = jnp.dot(xn, wi_ref[:, lo:hi].astype(BF16), preferred_element_type=F32)
            up = jnp.dot(xn, wi_ref[:, D_FF + lo:D_FF + hi].astype(BF16), preferred_element_type=F32)
            act = (gate * jax.nn.sigmoid(gate) * up).astype(BF16)
            acc = acc + jnp.dot(act, wo_ref[lo:hi, :].astype(BF16), preferred_element_type=F32)
        y = x + 0.5 * acc
        if with_pe:
            gate = jax.nn.sigmoid(_bdot(_rms(y, gp_ref[...]), wgate_ref[...]))
            y = y + gate * _bdot(ps[k][...], wproj_ref[...])
            if final:
                y = _rms(y, gf_ref[...])
        outs[k][...] = y

    if ns == 1:
        body(0)
    else:
        i = pl.program_id(0)
        for k, (lo, hi) in enumerate(bounds):
            pl.when(jnp.logical_and(i >= lo, i < hi))(functools.partial(body, k))


def _ffn(hs, g, wi, wo, pe=None, layer=0, final=False):
    tiles = [_tile(h.shape[0], FFN_TILE) for h in hs]
    counts = [h.shape[0] // t for h, t in zip(hs, tiles)]
    bounds = [(sum(counts[:k]), sum(counts[:k + 1])) for k in range(len(hs))]

    def rows(k, width, lead=()):
        lo, cnt = bounds[k][0], counts[k]
        return pl.BlockSpec(tuple(None for _ in lead) + (tiles[k], width),
                            lambda i: lead + (jnp.clip(i - lo, 0, cnt - 1), 0))

    args = list(hs) + [g, wi, wo]
    in_specs = ([rows(k, D_MODEL) for k in range(len(hs))]
                + [_const_spec((1, D_MODEL)),
                   pl.BlockSpec((None,) + wi.shape[1:], lambda i: (layer, 0, 0)),
                   pl.BlockSpec((None,) + wo.shape[1:], lambda i: (layer, 0, 0))])
    if pe is not None:
        args += list(pe[0]) + list(pe[1:])
        in_specs += ([rows(k, PLE_DIM, (layer,)) for k in range(len(hs))]
                     + [_const_spec(a.shape) for a in pe[1:]])
    return pl.pallas_call(
        functools.partial(_ffn_kernel, bounds=tuple(bounds), with_pe=pe is not None, final=final),
        grid=(bounds[-1][1],),
        in_specs=in_specs,
        out_specs=[rows(k, D_MODEL) for k in range(len(hs))],
        out_shape=[jax.ShapeDtypeStruct((h.shape[0], D_MODEL), F32) for h in hs],
        compiler_params=_params("arbitrary", n_in=len(args)),
        name="ffn_pe" if pe is not None else "ffn",
    )(*args)


def _norm_mm_kernel(x_ref, g_ref, w_ref, o_ref):
    o_ref[...] = _bdot(_rms(x_ref[...], g_ref[...]), w_ref[...]).astype(o_ref.dtype)


def _norm_mm(h, g, w, out_dtype, col_tile=None, name="norm_mm"):
    n = h.shape[0]
    tm = _tile(n)
    nout = w.shape[1]
    tn = nout if col_tile is None else col_tile
    return pl.pallas_call(
        _norm_mm_kernel,
        grid=(n // tm, nout // tn),
        in_specs=[pl.BlockSpec((tm, D_MODEL), lambda i, j: (i, 0)), _const_spec((1, D_MODEL)),
                  pl.BlockSpec((D_MODEL, tn), lambda i, j: (0, j))],
        out_specs=pl.BlockSpec((tm, tn), lambda i, j: (i, j)),
        out_shape=jax.ShapeDtypeStruct((n, nout), out_dtype),
        compiler_params=_params("arbitrary", "arbitrary", n_in=3),
        name=name,
    )(h, g, w)


def _rwkv_layer_kernel(h_ref, sh_ref, s0_ref, nw_ref, mix_ref, wrkv_ref, w0_ref, w1_ref, w2_ref, a0_ref, a1_ref,
                       a2_ref, g1_ref, g2_ref, kk_ref, ka_ref, lnw_ref, lnb_ref, rk_ref, wo_ref, e_ref, et_ref,
                       o_ref, hn_o, st_ref,
                       state, prev, r_s, lw_s, k_s, v_s, kk_s, b_s, g_s, y_s, *, rows, n_sub, nseq, npar):
    c = pl.program_id(1)
    C = WKV_CHUNK
    C2 = 2 * C
    hp = N_HEADS // 2
    n_pairs = npar * hp
    seq_rows = rows // nseq
    short = seq_rows % C != 0
    total = rows if short else npar * rows

    def load_state(s0, base=0):
        for h in range(N_HEADS):
            lo = (h % 2) * HEAD_DIM
            state[base + h // 2] = jnp.zeros((LANES, LANES), F32)
        for h in range(N_HEADS):
            lo = (h % 2) * HEAD_DIM
            state[base + h // 2, lo:lo + HEAD_DIM, lo:lo + HEAD_DIM] = s0[h]

    def store_state(st, base=0):
        for h in range(N_HEADS):
            lo = (h % 2) * HEAD_DIM
            st[h] = state[base + h // 2, lo:lo + HEAD_DIM, lo:lo + HEAD_DIM]

    if nseq == 1:
        @pl.when(c == 0)
        def _():
            for q in range(npar):
                load_state(s0_ref.at[q], q * hp)
                prev[q:q + 1, :] = sh_ref[q]
    if short:
        for ref in (r_s, lw_s, k_s, v_s, kk_s, b_s):
            ref[rows:, :] = jnp.zeros((ref.shape[0] - rows, D_MODEL), F32)

    nw = nw_ref[...]
    h_in = h_ref[...].reshape(total, D_MODEL) if nseq == 1 else h_ref[...]
    hn = _rms(h_in, nw)
    row = lax.broadcasted_iota(jnp.int32, hn.shape, 0)
    rolled = pltpu.roll(hn, 1, 0)
    if nseq == 1:
        x_prev = rolled
        for q in range(npar):
            x_prev = jnp.where(row == q * rows, prev[q:q + 1, :], x_prev)
            last = hn[(q + 1) * rows - 1:(q + 1) * rows]
            prev[q:q + 1, :] = last
            hn_o[q] = last
    else:
        x_prev = jnp.where(lax.rem(row, seq_rows) == 0, sh_ref[...], rolled)
        hn_o[...] = hn
    xx = x_prev - hn
    mix = mix_ref[...]
    xr, xw, xk, xv, xa, xg = (hn + xx * mix[j:j + 1] for j in range(6))
    r = _bdot(xr, wrkv_ref[0])
    k = _bdot(xk, wrkv_ref[1])
    v = _bdot(xv, wrkv_ref[2])
    wl = w0_ref[...] + _bdot(jnp.tanh(_bdot(xw, w1_ref[...])), w2_ref[...])
    lw_s[0:total] = -EXP_MINUS_HALF * jax.nn.sigmoid(wl)
    a = jax.nn.sigmoid(a0_ref[...] + _bdot(_bdot(xa, a1_ref[...]), a2_ref[...]))
    g_s[...] = _bdot(jax.nn.sigmoid(_bdot(xg, g1_ref[...])), g2_ref[...])
    kkv = k * kk_ref[...]
    kk = kkv / jnp.maximum(jnp.sqrt(_head_sum(kkv * kkv, e_ref, et_ref)), 1e-12)
    r_s[0:total] = r
    k_s[0:total] = k * (1.0 + (a - 1.0) * ka_ref[...])
    v_s[0:total] = v
    kk_s[0:total] = kk
    b_s[0:total] = kk * a

    ri = lax.broadcasted_iota(jnp.int32, (C, C), 0)
    ci = lax.broadcasted_iota(jnp.int32, (C, C), 1)
    tri = (ri >= ci).astype(BF16)
    r2 = lax.broadcasted_iota(jnp.int32, (C2, C2), 0)
    c2 = lax.broadcasted_iota(jnp.int32, (C2, C2), 1)
    strict = r2 > c2
    incl = r2 >= c2
    lane = lax.broadcasted_iota(jnp.int32, (C, LANES), 1)
    head0 = lane < HEAD_DIM

    def hat(x):
        x3 = jnp.stack([x[:, p * LANES:(p + 1) * LANES] for p in range(n_pairs)])
        return jnp.concatenate([jnp.where(head0, x3, 0.0), jnp.where(head0, 0.0, x3)], axis=1)

    def mm(a, b):
        return jnp.einsum("pmk,pkn->pmn", a.astype(BF16), b.astype(BF16), preferred_element_type=F32)

    def mm_nt(a, b):
        return jnp.einsum("pmk,pnk->pmn", a.astype(BF16), b.astype(BF16), preferred_element_type=F32)

    chunk_row = lax.broadcasted_iota(jnp.int32, (C, D_MODEL), 0)

    def chunk(ref, s):
        if short:
            starts = [pl.multiple_of((s * npar + q) * seq_rows, 8) for q in range(npar)]
            return jnp.concatenate([jnp.where(chunk_row < seq_rows, ref[pl.ds(st0, C), :], 0.0) for st0 in starts],
                                   axis=1)
        return jnp.concatenate([ref[pl.ds(q * rows + s * C, C), :] for q in range(npar)], axis=1)

    def sub_chunk(s, carry):
        lw = chunk(lw_s, s)
        p1 = lw.astype(BF16)
        rem = lw - p1.astype(F32)
        p2 = rem.astype(BF16)
        p3 = (rem - p2.astype(F32)).astype(BF16)
        cw = (jnp.dot(tri, p1, preferred_element_type=F32) + jnp.dot(tri, p2, preferred_element_type=F32)
              + jnp.dot(tri, p3, preferred_element_type=F32))
        cw_end = cw[C - 1:C, :]
        e_neg = jnp.exp(-cw)
        e_end = jnp.exp(cw_end - cw)
        kk_c = chunk(kk_s, s)
        bb = chunk(b_s, s)
        k_c = chunk(k_s, s)
        a_all = kk_c * jnp.exp(cw - lw)
        r_all = chunk(r_s, s) * jnp.exp(cw)
        b_all = bb * e_neg
        k_all = k_c * e_neg
        bd_all = bb * e_end
        kd_all = k_c * e_end
        v_all = chunk(v_s, s)
        decay = jnp.exp(cw_end)
        ar_h = jnp.concatenate([hat(a_all), hat(r_all)], axis=1)
        bk_h = jnp.concatenate([hat(b_all), hat(k_all)], axis=1)
        v_h = hat(v_all)
        st = state[...]
        g = mm_nt(ar_h, bk_h)
        low = jnp.where(strict, g[:, :C2, :C2], 0.0)
        ak = jnp.where(strict, g[:, :C2, C2:], 0.0)
        rbk = jnp.concatenate([jnp.where(incl, g[:, C2:, :C2], 0.0), jnp.where(incl, g[:, C2:, C2:], 0.0)], axis=2)
        ss = mm_nt(ar_h, st)
        x = -(ss[:, :C2] + mm(ak, v_h))
        t = mm(low, jnp.concatenate([low, x], axis=2))
        lp = t[:, :, :C2]
        x = x - t[:, :, C2:]
        for _ in range(max(int(np.ceil(np.log2(min(C, seq_rows)))), 2) - 2):
            t = mm(lp, jnp.concatenate([lp, x], axis=2))
            lp = t[:, :, :C2]
            x = x + t[:, :, C2:]
        x = x + mm(lp, x)
        xv = jnp.concatenate([x, v_h], axis=1)
        y_h = ss[:, C2:] + mm(rbk, xv)
        y = y_h[:, :C] + y_h[:, C:]
        for p in range(n_pairs):
            q, pl_ = divmod(p, hp)
            if short:
                first = pl.multiple_of((s * npar + q) * seq_rows, 8)
                y_s[pl.ds(first, seq_rows), pl_ * LANES:(pl_ + 1) * LANES] = y[p, :seq_rows]
            else:
                y_s[pl.ds(q * rows + s * C, C), pl_ * LANES:(pl_ + 1) * LANES] = y[p]
        bkd_h = jnp.concatenate([hat(bd_all), hat(kd_all)], axis=1)
        xv_t = jnp.stack([xv[p].T for p in range(n_pairs)])
        dec3 = jnp.stack([decay[:, p * LANES:(p + 1) * LANES] for p in range(n_pairs)])
        state[...] = st * dec3 + mm(xv_t, bkd_h)
        return carry

    if nseq == 1:
        for s in range(n_sub):
            sub_chunk(s, 0)
    else:
        def one_group(s, carry):
            for q in range(npar):
                load_state(s0_ref.at[s * npar + q], q * hp)
            sub_chunk(s, carry)
            for q in range(npar):
                store_state(st_ref.at[s * npar + q], q * hp)
            return carry

        lax.fori_loop(0, nseq // npar, one_group, 0)

    y = y_s[0:total]
    inv_n = 1.0 / HEAD_DIM
    mu = _head_sum(y, e_ref, et_ref) * inv_n
    yc = y - mu
    var = _head_sum(yc * yc, e_ref, et_ref) * inv_n
    yn = yc * lax.rsqrt(var + LNX_EPS) * lnw_ref[...] + lnb_ref[...]
    bonus = _head_sum(r_s[0:total] * k_s[0:total] * rk_ref[...], e_ref, et_ref) * v_s[0:total]
    out = h_in + _bdot((yn + bonus) * g_s[...], wo_ref[...])
    o_ref[...] = out.reshape(o_ref.shape)

    if nseq == 1:
        @pl.when(c == pl.num_programs(1) - 1)
        def _():
            for q in range(npar):
                store_state(st_ref.at[q], q * hp)


def _rwkv_layer(h, shift, s0, seq_len, lw):
    n = h.shape[0]
    nb = n // seq_len
    if seq_len % WKV_CHUNK == 0:
        npar = RWKV_PARALLEL_SEQS if nb % RWKV_PARALLEL_SEQS == 0 else 1
        nseq, rows = 1, _tile(seq_len, RWKV_STEP_ROWS // npar)
        steps, n_sub, total = seq_len // rows, rows // WKV_CHUNK, npar * rows
        buf_rows = total
        assert rows % WKV_CHUNK == 0
        h_in = h.reshape(nb, seq_len, D_MODEL)
        row_spec = pl.BlockSpec((npar, rows, D_MODEL), lambda bi, ci: (bi, ci, 0))
        sh_in = shift.reshape(nb, 1, D_MODEL)
        sh_spec = pl.BlockSpec((npar, 1, D_MODEL), lambda bi, ci: (bi, 0, 0))
        hn_shape, hn_spec = jax.ShapeDtypeStruct((nb, 1, D_MODEL), F32), sh_spec
        nst = npar
    else:
        assert seq_len < WKV_CHUNK and seq_len % 8 == 0
        nseq = _tile(nb, RWKV_SHORT_SEQS)
        npar = RWKV_SHORT_PARALLEL if nseq % RWKV_SHORT_PARALLEL == 0 else 1
        rows, steps, n_sub = nseq * seq_len, 1, 1
        total = rows
        buf_rows = rows + WKV_CHUNK - seq_len
        h_in = h
        row_spec = pl.BlockSpec((rows, D_MODEL), lambda bi, ci: (bi, 0))
        sh_in = jnp.repeat(shift, seq_len, axis=0)
        sh_spec = row_spec
        hn_shape, hn_spec = jax.ShapeDtypeStruct((n, D_MODEL), F32), sh_spec
        nst = nseq
    st_spec = pl.BlockSpec((nst, N_HEADS, HEAD_DIM, HEAD_DIM), lambda bi, ci: (bi, 0, 0, 0))
    consts = [lw["nw"], lw["mix"], lw["wrkv"], lw["w0"], lw["w1"], lw["w2"], lw["a0"], lw["a1"], lw["a2"],
              lw["g1"], lw["g2"], lw["kk"], lw["ka"], lw["lnw"], lw["lnb"], lw["rk"], lw["wo"], lw["e"], lw["et"]]
    seq_buf = pltpu.VMEM((buf_rows, D_MODEL), F32)
    out, hn, st = pl.pallas_call(
        functools.partial(_rwkv_layer_kernel, rows=rows, n_sub=n_sub, nseq=nseq, npar=npar),
        grid=(nb // nst, steps),
        in_specs=[row_spec, sh_spec, st_spec] + [_const_spec(c.shape) for c in consts],
        out_specs=[row_spec, hn_spec, st_spec],
        out_shape=[jax.ShapeDtypeStruct(h_in.shape, F32), hn_shape, jax.ShapeDtypeStruct(s0.shape, F32)],
        scratch_shapes=[pltpu.VMEM((npar * N_HEADS // 2, LANES, LANES), F32), pltpu.VMEM((8, D_MODEL), F32)]
                       + [seq_buf] * 6 + [pltpu.VMEM((total, D_MODEL), F32), seq_buf],
        compiler_params=_params("arbitrary", "arbitrary", n_in=3 + len(consts)),
        name="rwkv_layer",
    )(h_in, sh_in, s0, *consts)
    out = out.reshape(n, D_MODEL)
    shift_out = hn.reshape(nb, D_MODEL) if nseq == 1 else hn.reshape(nb, seq_len, D_MODEL)[:, -1]
    return out, shift_out, st


def _t5_buckets(dist):
    d = np.asarray(dist, dtype=np.int64)
    max_exact = REL_BUCKETS // 2
    large = max_exact + (np.log(np.maximum(d, 1) / max_exact) / np.log(REL_MAX_DIST / max_exact)
                         * (REL_BUCKETS - max_exact)).astype(np.int32)
    large = np.minimum(large, REL_BUCKETS - 1)
    return np.where(d < max_exact, d, large).astype(np.int32)


def _bucket_rows(table, buckets):
    onehot = jnp.asarray(np.eye(REL_BUCKETS, dtype=np.float32)[buckets])
    return jnp.dot(onehot, table.astype(F32), precision=lax.Precision.HIGHEST)


def _toeplitz(tab, n_rows, n_cols):
    period = tab.shape[-1]
    assert period >= n_rows + n_cols - 1 and n_cols <= period - 1
    lead = tab.shape[:-1]
    flat = jnp.broadcast_to(tab[..., None, :], lead + (n_rows, period)).reshape(lead + (-1,))
    skew = flat[..., :n_rows * (period - 1)].reshape(lead + (n_rows, period - 1))
    return skew[..., :n_cols]


def _band_bias(rel_bias, group):
    win, dil = DILATION_GROUPS[group]
    assert win // dil == Q_BLOCK
    period = 3 * Q_BLOCK - 1
    idx = np.arange(period)
    m = Q_BLOCK - np.where(idx < 2 * Q_BLOCK, idx, idx - period)
    valid = (m >= 0) & (m <= Q_BLOCK)
    buckets = _t5_buckets(dil * np.clip(m, 0, Q_BLOCK))
    tbl = _bucket_rows(rel_bias[:, group * N_HEADS:(group + 1) * N_HEADS], buckets).T
    general = _toeplitz(jnp.where(valid[None], tbl, NEG_INF), Q_BLOCK, 2 * Q_BLOCK)
    first = jnp.where((np.arange(2 * Q_BLOCK) >= Q_BLOCK)[None, None, :], general, NEG_INF)
    return jnp.stack([first, general])


_HEAD_OF_SEG = np.array([4 * (2 * (s // 8) + s % 2) + (s // 2) % 4 for s in range(N_HEADS)])


def _proj_rm_kernel(x_ref, g_ref, w_ref, *refs, natural, lane_ranges, tm):
    scr = refs[-1]
    outs = refs[:-1]
    y = _bdot(_rms(x_ref[...], g_ref[...]), w_ref[...])
    if natural:
        outs[0][...] = y
        outs = outs[1:]
    for c in range(scr.shape[0]):
        scr[c] = y[:, c * LANES:(c + 1) * LANES]
    for gi, (_, dil) in enumerate(DILATION_GROUPS):
        lo, hi = lane_ranges[gi]
        if dil == 1:
            outs[gi][0] = y[:, lo:hi].astype(BF16)
            continue
        for rho in range(dil):
            rows = [scr[c, pl.ds(rho, tm // dil, stride=dil), :] for c in range(lo // LANES, hi // LANES)]
            outs[gi][rho] = jnp.concatenate(rows, axis=1).astype(BF16)


def _proj_rm(h, g, w, nb, seq_len, natural, lane_ranges, name):
    n = h.shape[0]
    tm = _tile(n, PROJ_TILE)
    nout = w.shape[1]
    assert seq_len % tm == 0 and all(tm % (16 * dil) == 0 for _, dil in DILATION_GROUPS)
    tps = seq_len // tm
    out_shape, out_specs = [], []
    if natural:
        out_shape.append(jax.ShapeDtypeStruct((n, nout), F32))
        out_specs.append(pl.BlockSpec((tm, nout), lambda i: (i, 0)))
    for (_, dil), (lo, hi) in zip(DILATION_GROUPS, lane_ranges):
        out_shape.append(jax.ShapeDtypeStruct((nb, dil, seq_len // dil, hi - lo), BF16))
        out_specs.append(pl.BlockSpec((None, dil, tm // dil, hi - lo), lambda i: (i // tps, 0, i % tps, 0)))
    return pl.pallas_call(
        functools.partial(_proj_rm_kernel, natural=natural, lane_ranges=lane_ranges, tm=tm),
        grid=(n // tm,),
        in_specs=[pl.BlockSpec((tm, D_MODEL), lambda i: (i, 0)), _const_spec((1, D_MODEL)), _const_spec(w.shape)],
        out_specs=out_specs,
        out_shape=out_shape,
        scratch_shapes=[pltpu.VMEM((nout // LANES, tm, LANES), F32)],
        compiler_params=_params("arbitrary", n_in=3),
        name=name,
    )(h, g, w)


def _attn_kernel(q_ref, kv_ref, halo_ref, bias_ref, o_ref, lse_ref, kvbuf, *, nq):
    i = pl.program_id(0)
    dil = q_ref.shape[0]
    kvd = KV_HEADS * HEAD_DIM
    nt = (((1,), (1,)), ((), ()))
    kvbuf[:, :Q_BLOCK, :] = halo_ref[...]
    kvbuf[:, Q_BLOCK:, :] = kv_ref[...]
    lane = lax.broadcasted_iota(jnp.int32, (Q_BLOCK, LANES), 1)
    low_half = lane < HEAD_DIM
    lane_row = lax.broadcasted_iota(jnp.int32, (1, LANES), 1)
    keep_lo = (lane_row < HEAD_DIM).astype(BF16)
    keep_hi = (lane_row >= HEAD_DIM).astype(BF16)

    def block(u, carry):
        rho = lax.div(u, nq)
        j = u - rho * nq
        r0 = pl.multiple_of(j * Q_BLOCK, Q_BLOCK)
        qb = q_ref[rho, pl.ds(r0, Q_BLOCK), :]
        kvb = kvbuf[rho, pl.ds(r0, 2 * Q_BLOCK), :]
        bsel = jnp.where(jnp.logical_and(i == 0, j == 0), 0, 1)
        lse_tile = jnp.zeros((Q_BLOCK, LANES), F32)
        for G in range(KV_HEADS // 2):
            kg = kvb[:, G * LANES:(G + 1) * LANES]
            vg = kvb[:, kvd + G * LANES:kvd + (G + 1) * LANES]
            vcat = jnp.concatenate([vg * keep_lo, vg * keep_hi], axis=0)
            pieces = []
            for r in range(Q_PER_KV):
                qg = qb[:, (G * Q_PER_KV + r) * LANES:(G * Q_PER_KV + r + 1) * LANES]
                pieces += [qg * keep_lo, qg * keep_hi]
            s_all = lax.dot_general(jnp.concatenate(pieces, axis=0), kg, nt, preferred_element_type=F32)
            for r in range(Q_PER_KV):
                grp = G * Q_PER_KV + r
                parts = []
                for half in range(2):
                    s = s_all[(2 * r + half) * Q_BLOCK:(2 * r + half + 1) * Q_BLOCK]
                    s = s + bias_ref[bsel, int(_HEAD_OF_SEG[2 * grp + half])]
                    m = jnp.max(s, axis=-1, keepdims=True)
                    p = jnp.exp(s - m)
                    l = jnp.sum(p, axis=-1, keepdims=True)
                    parts.append((p.astype(BF16), l, m + jnp.log(l)))
                (p0, l0, e0), (p1, l1, e1) = parts
                o = jnp.dot(jnp.concatenate([p0, p1], axis=1), vcat, preferred_element_type=F32)
                o = (o / jnp.where(low_half, l0, l1)).astype(o_ref.dtype)
                o_ref[rho, pl.ds(r0, Q_BLOCK), grp * LANES:(grp + 1) * LANES] = o
                lse_tile = jnp.where(lane == 2 * grp, e0, lse_tile)
                lse_tile = jnp.where(lane == 2 * grp + 1, e1, lse_tile)
        lse_ref[rho, pl.ds(r0, Q_BLOCK), :] = lse_tile
        return carry

    lax.fori_loop(0, dil * nq, block, 0)


def _attn_group(q_rm, kv_rm, bias):
    nb, dil, tsub, _ = q_rm.shape
    kvw = kv_rm.shape[-1]
    rows = ATTN_SPAN // dil
    nq = rows // Q_BLOCK
    assert tsub % rows == 0 and nq >= 1
    span_spec = lambda width: pl.BlockSpec((None, dil, rows, width), lambda i, b: (b, 0, i, 0))
    return pl.pallas_call(
        functools.partial(_attn_kernel, nq=nq),
        grid=(tsub // rows, nb),
        in_specs=[span_spec(D_MODEL), span_spec(kvw),
                  pl.BlockSpec((None, dil, Q_BLOCK, kvw), lambda i, b: (b, 0, jnp.maximum(i * nq - 1, 0), 0)),
                  _const_spec(bias.shape)],
        out_specs=[span_spec(D_MODEL), span_spec(LANES)],
        out_shape=[jax.ShapeDtypeStruct((nb, dil, tsub, D_MODEL), BF16),
                   jax.ShapeDtypeStruct((nb, dil, tsub, LANES), F32)],
        scratch_shapes=[pltpu.VMEM((dil, Q_BLOCK + rows, kvw), BF16)],
        compiler_params=_params("arbitrary", "arbitrary", n_in=4),
        name=f"attn_d{dil}",
    )(q_rm, kv_rm, kv_rm, bias)


def _attn_out_kernel(o0_ref, o1_ref, o2_ref, l0_ref, l1_ref, l2_ref, h_ref, wo_ref, et_ref, out_ref, *scr, tm):
    outs, lses = [], []
    for gi, (o_ref, l_ref) in enumerate(((o0_ref, l0_ref), (o1_ref, l1_ref), (o2_ref, l2_ref))):
        dil = DILATION_GROUPS[gi][1]
        if dil == 1:
            outs.append(o_ref[0].astype(F32))
            lses.append(l_ref[0])
            continue
        so, sl = scr[2 * gi], scr[2 * gi + 1]
        n_tiles = so.shape[0]
        for rho in range(dil):
            rows = pl.ds(rho, tm // dil, stride=dil)
            for c in range(n_tiles):
                so[c, rows, :] = o_ref[rho, :, c * LANES:(c + 1) * LANES].astype(F32)
            sl[rows, :] = l_ref[rho]
        outs.append(jnp.concatenate([so[c] for c in range(n_tiles)], axis=1))
        lses.append(sl[...])
    l0, l1, l2 = lses
    m = jnp.maximum(jnp.maximum(l0, l1), l2)
    w0, w1, w2 = jnp.exp(l0 - m), jnp.exp(l1 - m), jnp.exp(l2 - m)
    inv = 1.0 / (w0 + w1 + w2)
    et = et_ref[...]
    att = _bdot(w0 * inv, et) * outs[0] + _bdot(w1 * inv, et) * outs[1] + _bdot(w2 * inv, et) * outs[2]
    out_ref[...] = h_ref[...] + _bdot(att, wo_ref[...])


def _attn_out(outs, lses, h, wo, et, seq_len):
    n = h.shape[0]
    tm = _tile(n, PROJ_TILE)
    tps = seq_len // tm
    row_spec = pl.BlockSpec((tm, D_MODEL), lambda i: (i, 0))

    def rm_spec(dil, width):
        return pl.BlockSpec((None, dil, tm // dil, width), lambda i: (i // tps, 0, i % tps, 0))

    dils = [dil for _, dil in DILATION_GROUPS]
    scratch = []
    for _ in dils:
        scratch += [pltpu.VMEM((D_MODEL // LANES, tm, LANES), F32), pltpu.VMEM((tm, LANES), F32)]
    return pl.pallas_call(
        functools.partial(_attn_out_kernel, tm=tm),
        grid=(n // tm,),
        in_specs=[rm_spec(d, D_MODEL) for d in dils] + [rm_spec(d, LANES) for d in dils]
                 + [row_spec, _const_spec(wo.shape), _const_spec(et.shape)],
        out_specs=row_spec,
        out_shape=jax.ShapeDtypeStruct((n, D_MODEL), F32),
        scratch_shapes=scratch,
        compiler_params=_params("arbitrary", n_in=9),
        name="attn_out",
    )(*outs, *lses, h, wo, et)


def _decode_bias(rel_bias, seq_len, cache_len):
    ncol = cache_len + LANES
    period = seq_len + ncol - 1
    idx = np.arange(period)
    dist = cache_len - np.where(idx < ncol, idx, idx - period)
    buckets = _t5_buckets(np.clip(dist, 0, MAX_WINDOW))
    tabs = []
    for g, (win, dil) in enumerate(DILATION_GROUPS):
        valid = (dist >= 0) & (dist % dil == 0) & (dist <= win)
        tbl = _bucket_rows(rel_bias[:, g * N_HEADS:(g + 1) * N_HEADS], buckets).T
        tabs.append(jnp.where(valid[None], tbl, NEG_INF))
    rows = _toeplitz(jnp.stack(tabs), seq_len, ncol)
    bias = rows.reshape(N_GROUPS * N_HEADS * seq_len, ncol)
    return bias[:, :cache_len], bias[:, cache_len:]


def _attn_decode_kernel(q_ref, cache_ref, kvn_ref, h_ref, bc_ref, bn_ref, wo_ref, out_ref, *, seq_len):
    kvd = KV_HEADS * HEAD_DIM
    nslot = N_GROUPS * N_HEADS
    rows_g = N_HEADS * seq_len
    nt = (((1,), (1,)), ((), ()))
    cache = cache_ref[...]
    kc = cache[:, :kvd].astype(BF16)
    vc = cache[:, kvd:].astype(BF16)
    kvn = kvn_ref[...]
    pad = jnp.zeros((LANES - seq_len, kvd), F32)
    kn = jnp.concatenate([kvn[:, :kvd], pad], axis=0).astype(BF16)
    vn = jnp.concatenate([kvn[:, kvd:], pad], axis=0).astype(BF16)
    lhs = jnp.concatenate([q_ref[:, s * kvd:(s + 1) * kvd] for s in range(nslot)], axis=0).astype(BF16)
    cache_len = kc.shape[0]
    sn = lax.dot_general(lhs, kn, nt, preferred_element_type=F32) + bn_ref[...]
    scs, m = [], None
    for g, (win, _) in enumerate(DILATION_GROUPS):
        c0 = (cache_len - min(win, cache_len)) // LANES * LANES
        rows = slice(g * rows_g, (g + 1) * rows_g)
        sc = lax.dot_general(lhs[rows], kc[c0:], nt, preferred_element_type=F32) + bc_ref[rows, c0:]
        scs.append((sc, c0, rows))
        m_g = jnp.maximum(jnp.max(sc, axis=-1, keepdims=True), jnp.max(sn[rows], axis=-1, keepdims=True))
        m = m_g if m is None else jnp.maximum(m, m_g)
    l = jnp.zeros((rows_g, 1), F32)
    num = jnp.zeros((rows_g, kvd), F32)
    for sc, c0, rows in scs:
        pc = jnp.exp(sc - m)
        pn = jnp.exp(sn[rows] - m)
        l = l + jnp.sum(pc, axis=-1, keepdims=True) + jnp.sum(pn, axis=-1, keepdims=True)
        num = (num + jnp.dot(pc.astype(BF16), vc[c0:], preferred_element_type=F32)
               + jnp.dot(pn.astype(BF16), vn, preferred_element_type=F32))
    row = lax.broadcasted_iota(jnp.int32, (rows_g, kvd), 0)
    lane = lax.broadcasted_iota(jnp.int32, (rows_g, kvd), 1)
    own = (row // (Q_PER_KV * seq_len)) == (lane // HEAD_DIM)
    att = jnp.where(own, num / l, 0.0)
    out = h_ref[...]
    for r in range(Q_PER_KV):
        a_r = att[r * seq_len:(r + 1) * seq_len]
        for c in range(1, KV_HEADS):
            a_r = a_r + att[(c * Q_PER_KV + r) * seq_len:(c * Q_PER_KV + r + 1) * seq_len]
        out = out + _bdot(a_r, wo_ref[r])
    out_ref[...] = out


def _attn_decode(q, cache, kv_new, h, bias_c, bias_n, wo_r, nb, seq_len):
    cache_len = cache.shape[1]
    qw = q.shape[1]
    kvw = 2 * KV_HEADS * HEAD_DIM
    return pl.pallas_call(
        functools.partial(_attn_decode_kernel, seq_len=seq_len),
        grid=(nb,),
        in_specs=[pl.BlockSpec((seq_len, qw), lambda b: (b, 0)),
                  pl.BlockSpec((None, cache_len, kvw), lambda b: (b, 0, 0)),
                  pl.BlockSpec((seq_len, kvw), lambda b: (b, 0)),
                  pl.BlockSpec((seq_len, D_MODEL), lambda b: (b, 0)),
                  _const_spec(bias_c.shape), _const_spec(bias_n.shape), _const_spec(wo_r.shape)],
        out_specs=pl.BlockSpec((seq_len, D_MODEL), lambda b: (b, 0)),
        out_shape=jax.ShapeDtypeStruct((nb * seq_len, D_MODEL), F32),
        compiler_params=_params("arbitrary", n_in=7),
        name="attn_decode",
    )(q, cache, kv_new, h, bias_c, bias_n, wo_r)


def _prep_weights(norm_w, ffn1_wi, ffn1_wo, ffn2_wi, ffn2_wo, pe_proj, pe_gate,
                  rwkv_mix, rwkv_wrkv, rwkv_wo, rwkv_w0, rwkv_w1, rwkv_w2, rwkv_a0, rwkv_a1, rwkv_a2,
                  rwkv_g1, rwkv_g2, rwkv_kk, rwkv_ka, rwkv_rk, rwkv_lnx_w, rwkv_lnx_b,
                  attn_wq, attn_wo, kv_norm, w_kv, rel_bias, final_norm):
    def row(v):
        return v.reshape(1, -1).astype(F32)

    def pad_cols(w, n):
        return jnp.pad(w, ((0, 0), (0, n - w.shape[1]))).astype(BF16)

    def pad_rows(w, n):
        return jnp.pad(w, ((0, n - w.shape[0]), (0, 0))).astype(BF16)

    head_of_lane = np.arange(D_MODEL) // HEAD_DIM
    e = jnp.asarray(head_of_lane[:, None] == np.arange(LANES)[None, :], BF16)
    et = jnp.asarray(np.arange(LANES)[:, None] == head_of_lane[None, :], BF16)

    depth = norm_w.shape[0]
    layers = []
    for i in range(depth):
        layers.append(dict(
            nw=[row(norm_w[i, j]) for j in range(4)],
            ffn1=(ffn1_wi.astype(F32), ffn1_wo.astype(F32)), ffn2=(ffn2_wi.astype(F32), ffn2_wo.astype(F32)),
            pe_gate=pe_gate[i].astype(BF16), pe_proj=pe_proj[i].astype(BF16)))
    n_a = depth // 2
    rw = []
    for i in range(n_a):
        rw.append(dict(
            nw=row(norm_w[i, 1]), mix=rwkv_mix[i].astype(F32), wrkv=rwkv_wrkv[i].astype(BF16),
            w0=row(rwkv_w0[i]), w1=pad_cols(rwkv_w1[i], LANES), w2=pad_rows(rwkv_w2[i], LANES),
            a0=row(rwkv_a0[i]), a1=pad_cols(rwkv_a1[i], LANES), a2=pad_rows(rwkv_a2[i], LANES),
            g1=pad_cols(rwkv_g1[i], 2 * LANES), g2=pad_rows(rwkv_g2[i], 2 * LANES),
            kk=row(rwkv_kk[i]), ka=row(rwkv_ka[i]), rk=row(rwkv_rk[i]),
            lnw=row(rwkv_lnx_w[i]), lnb=row(rwkv_lnx_b[i]), wo=rwkv_wo[i].astype(BF16), e=e, et=et))
    scale = HEAD_DIM ** -0.5
    at = []
    for j in range(depth - n_a):
        wq = attn_wq[j] * scale
        wo_r = attn_wo[j].reshape(KV_HEADS, Q_PER_KV, HEAD_DIM, D_MODEL).transpose(1, 0, 2, 3)
        wo_r = wo_r.reshape(Q_PER_KV, KV_HEADS * HEAD_DIM, D_MODEL).astype(BF16)
        wq_seg = wq.reshape(D_MODEL, N_GROUPS, 2, 2, Q_PER_KV, HEAD_DIM).transpose(0, 1, 2, 4, 3, 5)
        wq_seg = wq_seg.reshape(D_MODEL, -1)
        wo_seg = attn_wo[j].reshape(2, 2, Q_PER_KV, HEAD_DIM, D_MODEL).transpose(0, 2, 1, 3, 4)
        wo_seg = wo_seg.reshape(D_MODEL, D_MODEL)
        at.append(dict(wq_seg=wq_seg.astype(BF16), wo_seg=wo_seg.astype(BF16), wo_r=wo_r))
    return dict(layers=layers, rwkv=rw, attn=at, kv_norm=row(kv_norm), w_kv=w_kv.astype(BF16),
                final_norm=row(final_norm), rel_bias=rel_bias, et=et)


def _trunks(streams, w):
    depth = len(w["layers"])
    n_a = depth // 2
    kvw = 2 * KV_HEADS * HEAD_DIM
    st = []
    for sd in streams:
        nb, seq_len, _ = sd["x"].shape
        n = nb * seq_len
        st.append(dict(nb=nb, seq_len=seq_len, n=n, h=sd["x"].reshape(n, D_MODEL).astype(F32),
                       p=sd["p"].reshape(depth, n, PLE_DIM).astype(F32), wkv_out=[], shift_out=[],
                       kv_rows=None, kv_rm=None, **{k: sd[k] for k in ("wkv0", "shift0", "cache")}))

    def ffn(nw, weights, **kw):
        for t, h in zip(st, _ffn([t["h"] for t in st], nw, *weights, **kw)):
            t["h"] = h

    for i in range(depth):
        lw = w["layers"][i]
        if i == n_a:
            for t in st:
                if t["cache"] is None:
                    t["kv_rows"], *t["kv_rm"] = _proj_rm(t["h"], w["kv_norm"], w["w_kv"], t["nb"], t["seq_len"], True,
                                                         [(0, kvw)] * N_GROUPS, "kv_proj")
                else:
                    t["kv_rows"] = _norm_mm(t["h"], w["kv_norm"], w["w_kv"], F32, name="kv_proj")
        ffn(lw["nw"][0], lw["ffn1"], layer=i)
        for t in st:
            nb, seq_len, n, h = t["nb"], t["seq_len"], t["n"], t["h"]
            if i < n_a:
                h, sh, state = _rwkv_layer(h, t["shift0"][i].astype(F32), t["wkv0"][i].astype(F32), seq_len,
                                           w["rwkv"][i])
                t["wkv_out"].append(state)
                t["shift_out"].append(sh)
            elif t["cache"] is None:
                al = w["attn"][i - n_a]
                q_rm = _proj_rm(h, lw["nw"][1], al["wq_seg"], nb, seq_len, False,
                                [(gi * D_MODEL, (gi + 1) * D_MODEL) for gi in range(N_GROUPS)], "q_proj")
                outs, lses = [], []
                for gi in range(N_GROUPS):
                    o, lse = _attn_group(q_rm[gi], t["kv_rm"][gi], _band_bias(w["rel_bias"], gi))
                    outs.append(o)
                    lses.append(lse)
                h = _attn_out(outs, lses, h, al["wo_seg"], w["et"], seq_len)
            else:
                al = w["attn"][i - n_a]
                q = _norm_mm(h, lw["nw"][1], al["wq_seg"], F32, name="q_proj_decode")
                q = q.reshape(n, N_GROUPS, 2, Q_PER_KV, 2, HEAD_DIM).transpose(0, 1, 2, 4, 3, 5)
                q = q.reshape(n, N_GROUPS, KV_HEADS, Q_PER_KV, 1, HEAD_DIM)
                q = q * jnp.eye(KV_HEADS, dtype=F32).reshape(1, 1, KV_HEADS, 1, KV_HEADS, 1)
                q = q.reshape(n, N_GROUPS * N_HEADS * KV_HEADS * HEAD_DIM)
                bias_c, bias_n = _decode_bias(w["rel_bias"], seq_len, t["cache"].shape[1])
                h = _attn_decode(q, t["cache"], t["kv_rows"], h, bias_c, bias_n, al["wo_r"], nb, seq_len)
            t["h"] = h
        pe = ([t["p"] for t in st], lw["nw"][3], lw["pe_gate"], lw["pe_proj"], w["final_norm"])
        ffn(lw["nw"][2], lw["ffn2"], pe=pe, layer=i, final=(i == depth - 1))
    results = []
    for t, sd in zip(st, streams):
        dt = sd["x"].dtype
        y = t["h"].reshape(t["nb"], t["seq_len"], D_MODEL).astype(dt)
        kv_rows = t["kv_rows"].reshape(t["nb"], t["seq_len"], kvw).astype(dt)
        results.append((y, jnp.stack(t["wkv_out"]).astype(dt), jnp.stack(t["shift_out"]).astype(dt), kv_rows))
    return results


def _kv_heads(kv):
    return kv.reshape(kv.shape[:2] + (2, KV_HEADS, HEAD_DIM))


def _trunk(x, p, wkv0, shift0, cache, w):
    y, wkv, shift, kv = _trunks([dict(x=x, p=p, wkv0=wkv0, shift0=shift0, cache=cache)], w)[0]
    return y, wkv, shift, _kv_heads(kv)


def kernel(x_prompt, x_sample, state_wkv, state_shift, cache_kv, p_prompt, p_sample, norm_w, ffn1_wi, ffn1_wo, ffn2_wi, ffn2_wo, pe_proj, pe_gate, rwkv_mix, rwkv_wrkv, rwkv_wo, rwkv_w0, rwkv_w1, rwkv_w2, rwkv_a0, rwkv_a1, rwkv_a2, rwkv_g1, rwkv_g2, rwkv_kk, rwkv_ka, rwkv_rk, rwkv_lnx_w, rwkv_lnx_b, attn_wq, attn_wo, kv_norm, w_kv, rel_bias, final_norm):
    w = _prep_weights(norm_w, ffn1_wi, ffn1_wo, ffn2_wi, ffn2_wo, pe_proj, pe_gate,
                      rwkv_mix, rwkv_wrkv, rwkv_wo, rwkv_w0, rwkv_w1, rwkv_w2, rwkv_a0, rwkv_a1, rwkv_a2,
                      rwkv_g1, rwkv_g2, rwkv_kk, rwkv_ka, rwkv_rk, rwkv_lnx_w, rwkv_lnx_b,
                      attn_wq, attn_wo, kv_norm, w_kv, rel_bias, final_norm)
    n_a = norm_w.shape[0] // 2
    nb, seq_len, _ = x_prompt.shape
    wkv0 = jnp.zeros((n_a, nb, N_HEADS, HEAD_DIM, HEAD_DIM), F32)
    shift0 = jnp.zeros((n_a, nb, D_MODEL), x_prompt.dtype)
    cache = cache_kv.reshape(cache_kv.shape[0], cache_kv.shape[1], 2 * KV_HEADS * HEAD_DIM).astype(F32)
    (y_p, wkv_p, shift_p, kv_p), (y_s, wkv_s, shift_s, kv_s) = _trunks(
        [dict(x=x_prompt, p=p_prompt, wkv0=wkv0, shift0=shift0, cache=None),
         dict(x=x_sample, p=p_sample, wkv0=state_wkv, shift0=state_shift, cache=cache)], w)
    kv_prompt = _kv_heads(kv_p[:, seq_len - min(MAX_WINDOW, seq_len):])
    return (y_p, y_s, wkv_p, shift_p, kv_prompt, wkv_s, shift_s, _kv_heads(kv_s))
```

```python
import functools

import numpy as np
import jax
import jax.numpy as jnp
from jax import lax
from jax.experimental import pallas as pl
from jax.experimental.pallas import tpu as pltpu

F32 = jnp.float32
BF16 = jnp.bfloat16

D_MODEL = 1024
D_FF = 2816
PLE_DIM = 256
RMS_EPS = 1e-6
HEAD_DIM = 64
N_HEADS = D_MODEL // HEAD_DIM
LNX_EPS = 64e-5
KV_HEADS = 4
Q_PER_KV = N_HEADS // KV_HEADS
DILATION_GROUPS = ((128, 1), (512, 4), (2048, 16))
N_GROUPS = len(DILATION_GROUPS)
MAX_WINDOW = 2048
REL_BUCKETS = 32
REL_MAX_DIST = 2048
NEG_INF = -1e30

LANES = 128
FFN_CHUNK = 256
N_FFN_CHUNKS = D_FF // FFN_CHUNK
TOKEN_TILE = 256
FFN_TILE = 512
PROJ_TILE = 1024
WKV_CHUNK = 64
RWKV_SHORT_SEQS = 8
RWKV_PARALLEL_SEQS = 2
RWKV_STEP_ROWS = 512
Q_BLOCK = 128
ATTN_SPAN = 2048
VMEM_LIMIT = 56 * 1024 * 1024
EXP_MINUS_HALF = 0.6065306597126334


def _params(*sem, n_in):
    return pltpu.CompilerParams(dimension_semantics=sem, vmem_limit_bytes=VMEM_LIMIT,
                                allow_input_fusion=[True] * n_in)


def _const_spec(shape):
    return pl.BlockSpec(shape, lambda *_: (0,) * len(shape))


def _tile(n, pref=TOKEN_TILE):
    t = min(n, pref)
    while n % t:
        t -= 8
    return t


def _rms(x, g):
    return x * lax.rsqrt(jnp.mean(x * x, axis=-1, keepdims=True) + RMS_EPS) * g


def _bdot(a, b):
    return jnp.dot(a.astype(BF16), b, preferred_element_type=F32)


def _head_sum(x, e_ref, et_ref):
    return _bdot(_bdot(x, e_ref[...]), et_ref[...])


def _ffn_kernel(*refs, bounds, with_pe, final):
    ns = len(bounds)
    xs, (g_ref, wi_ref, wo_ref), rest = refs[:ns], refs[ns:ns + 3], refs[ns + 3:]
    if with_pe:
        ps, (gp_ref, wgate_ref, wproj_ref, gf_ref), outs = rest[:ns], rest[ns:ns + 4], rest[ns + 4:]
    else:
        outs = rest

    def body(k):
        x = xs[k][...]
        xn = _rms(x, g_ref[...]).astype(BF16)
        acc = jnp.zeros_like(x)
        for j in range(N_FFN_CHUNKS):
            lo, hi = j * FFN_CHUNK, (j + 1) * FFN_CHUNK
            gate = jnp.dot(xn, wi_ref[:, lo:hi].astype(BF16), preferred_element_type=F32)
            up = jnp.dot(xn, wi_ref[:, D_FF + lo:D_FF + hi].astype(BF16), preferred_element_type=F32)
            act = (gate * jax.nn.sigmoid(gate) * up).astype(BF16)
            acc = acc + jnp.dot(act, wo_ref[lo:hi, :].astype(BF16), preferred_element_type=F32)
        y = x + 0.5 * acc
        if with_pe:
            gate = jax.nn.sigmoid(_bdot(_rms(y, gp_ref[...]), wgate_ref[...]))
            y = y + gate * _bdot(ps[k][...], wproj_ref[...])
            if final:
                y = _rms(y, gf_ref[...])
        outs[k][...] = y

    if ns == 1:
        body(0)
    else:
        i = pl.program_id(0)
        for k, (lo, hi) in enumerate(bounds):
            pl.when(jnp.logical_and(i >= lo, i < hi))(functools.partial(body, k))


def _ffn(hs, g, wi, wo, pe=None, layer=0, final=False):
    tiles = [_tile(h.shape[0], FFN_TILE) for h in hs]
    counts = [h.shape[0] // t for h, t in zip(hs, tiles)]
    bounds = [(sum(counts[:k]), sum(counts[:k + 1])) for k in range(len(hs))]

    def rows(k, width, lead=()):
        lo, cnt = bounds[k][0], counts[k]
        return pl.BlockSpec(tuple(None for _ in lead) + (tiles[k], width),
                            lambda i: lead + (jnp.clip(i - lo, 0, cnt - 1), 0))

    args = list(hs) + [g, wi, wo]
    in_specs = ([rows(k, D_MODEL) for k in range(len(hs))]
                + [_const_spec((1, D_MODEL)),
                   pl.BlockSpec((None,) + wi.shape[1:], lambda i: (layer, 0, 0)),
                   pl.BlockSpec((None,) + wo.shape[1:], lambda i: (layer, 0, 0))])
    if pe is not None:
        args += list(pe[0]) + list(pe[1:])
        in_specs += ([rows(k, PLE_DIM, (layer,)) for k in range(len(hs))]
                     + [_const_spec(a.shape) for a in pe[1:]])
    return pl.pallas_call(
        functools.partial(_ffn_kernel, bounds=tuple(bounds), with_pe=pe is not None, final=final),
        grid=(bounds[-1][1],),
        in_specs=in_specs,
        out_specs=[rows(k, D_MODEL) for k in range(len(hs))],
        out_shape=[jax.ShapeDtypeStruct((h.shape[0], D_MODEL), F32) for h in hs],
        compiler_params=_params("arbitrary", n_in=len(args)),
        name="ffn_pe" if pe is not None else "ffn",
    )(*args)


def _norm_mm_kernel(x_ref, g_ref, w_ref, o_ref):
    o_ref[...] = _bdot(_rms(x_ref[...], g_ref[...]), w_ref[...]).astype(o_ref.dtype)


def _norm_mm(h, g, w, out_dtype, col_tile=None, name="norm_mm"):
    n = h.shape[0]
    tm = _tile(n)
    nout = w.shape[1]
    tn = nout if col_tile is None else col_tile
    return pl.pallas_call(
        _norm_mm_kernel,
        grid=(n // tm, nout // tn),
        in_specs=[pl.BlockSpec((tm, D_MODEL), lambda i, j: (i, 0)), _const_spec((1, D_MODEL)),
                  pl.BlockSpec((D_MODEL, tn), lambda i, j: (0, j))],
        out_specs=pl.BlockSpec((tm, tn), lambda i, j: (i, j)),
        out_shape=jax.ShapeDtypeStruct((n, nout), out_dtype),
        compiler_params=_params("arbitrary", "arbitrary", n_in=3),
        name=name,
    )(h, g, w)


def _rwkv_layer_kernel(h_ref, sh_ref, s0_ref, nw_ref, mix_ref, wrkv_ref, w0_ref, w1_ref, w2_ref, a0_ref, a1_ref,
                       a2_ref, g1_ref, g2_ref, kk_ref, ka_ref, lnw_ref, lnb_ref, rk_ref, wo_ref, e_ref, et_ref,
                       o_ref, hn_o, st_ref,
                       state, prev, r_s, lw_s, k_s, v_s, kk_s, b_s, g_s, y_s, *, rows, n_sub, nseq, npar):
    c = pl.program_id(1)
    C = WKV_CHUNK
    C2 = 2 * C
    hp = N_HEADS // 2
    n_pairs = npar * hp
    seq_rows = rows // nseq
    short = seq_rows % C != 0
    total = rows if short else npar * rows

    def load_state(s0, base=0):
        for h in range(N_HEADS):
            lo = (h % 2) * HEAD_DIM
            state[base + h // 2] = jnp.zeros((LANES, LANES), F32)
        for h in range(N_HEADS):
            lo = (h % 2) * HEAD_DIM
            state[base + h // 2, lo:lo + HEAD_DIM, lo:lo + HEAD_DIM] = s0[h]

    def store_state(st, base=0):
        for h in range(N_HEADS):
            lo = (h % 2) * HEAD_DIM
            st[h] = state[base + h // 2, lo:lo + HEAD_DIM, lo:lo + HEAD_DIM]

    if nseq == 1:
        @pl.when(c == 0)
        def _():
            for q in range(npar):
                load_state(s0_ref.at[q], q * hp)
                prev[q:q + 1, :] = sh_ref[q]
    if short:
        for ref in (r_s, lw_s, k_s, v_s, kk_s, b_s):
            ref[rows:, :] = jnp.zeros((ref.shape[0] - rows, D_MODEL), F32)

    nw = nw_ref[...]
    h_in = h_ref[...].reshape(total, D_MODEL) if nseq == 1 else h_ref[...]
    hn = _rms(h_in, nw)
    row = lax.broadcasted_iota(jnp.int32, hn.shape, 0)
    rolled = pltpu.roll(hn, 1, 0)
    if nseq == 1:
        x_prev = rolled
        for q in range(npar):
            x_prev = jnp.where(row == q * rows, prev[q:q + 1, :], x_prev)
            last = hn[(q + 1) * rows - 1:(q + 1) * rows]
            prev[q:q + 1, :] = last
            hn_o[q] = last
    else:
        x_prev = jnp.where(lax.rem(row, seq_rows) == 0, sh_ref[...], rolled)
        hn_o[...] = hn
    xx = x_prev - hn
    mix = mix_ref[...]
    xr, xw, xk, xv, xa, xg = (hn + xx * mix[j:j + 1] for j in range(6))
    r = _bdot(xr, wrkv_ref[0])
    k = _bdot(xk, wrkv_ref[1])
    v = _bdot(xv, wrkv_ref[2])
    wl = w0_ref[...] + _bdot(jnp.tanh(_bdot(xw, w1_ref[...])), w2_ref[...])
    lw_s[0:total] = -EXP_MINUS_HALF * jax.nn.sigmoid(wl)
    a = jax.nn.sigmoid(a0_ref[...] + _bdot(_bdot(xa, a1_ref[...]), a2_ref[...]))
    g_s[...] = _bdot(jax.nn.sigmoid(_bdot(xg, g1_ref[...])), g2_ref[...])
    kkv = k * kk_ref[...]
    kk = kkv / jnp.maximum(jnp.sqrt(_head_sum(kkv * kkv, e_ref, et_ref)), 1e-12)
    r_s[0:total] = r
    k_s[0:total] = k * (1.0 + (a - 1.0) * ka_ref[...])
    v_s[0:total] = v
    kk_s[0:total] = kk
    b_s[0:total] = kk * a

    ri = lax.broadcasted_iota(jnp.int32, (C, C), 0)
    ci = lax.broadcasted_iota(jnp.int32, (C, C), 1)
    tri = (ri >= ci).astype(BF16)
    r2 = lax.broadcasted_iota(jnp.int32, (C2, C2), 0)
    c2 = lax.broadcasted_iota(jnp.int32, (C2, C2), 1)
    strict = r2 > c2
    incl = r2 >= c2
    lane = lax.broadcasted_iota(jnp.int32, (C, LANES), 1)
    head0 = lane < HEAD_DIM

    def hat(x):
        x3 = jnp.stack([x[:, p * LANES:(p + 1) * LANES] for p in range(n_pairs)])
        return jnp.concatenate([jnp.where(head0, x3, 0.0), jnp.where(head0, 0.0, x3)], axis=1)

    def mm(a, b):
        return jnp.einsum("pmk,pkn->pmn", a.astype(BF16), b.astype(BF16), preferred_element_type=F32)

    def mm_nt(a, b):
        return jnp.einsum("pmk,pnk->pmn", a.astype(BF16), b.astype(BF16), preferred_element_type=F32)

    chunk_row = lax.broadcasted_iota(jnp.int32, (C, D_MODEL), 0)

    def chunk(ref, s):
        if short:
            starts = [pl.multiple_of((s * npar + q) * seq_rows, 8) for q in range(npar)]
            return jnp.concatenate([jnp.where(chunk_row < seq_rows, ref[pl.ds(st0, C), :], 0.0) for st0 in starts],
                                   axis=1)
        return jnp.concatenate([ref[pl.ds(q * rows + s * C, C), :] for q in range(npar)], axis=1)

    def sub_chunk(s, carry):
        lw = chunk(lw_s, s)
        p1 = lw.astype(BF16)
        rem = lw - p1.astype(F32)
        p2 = rem.astype(BF16)
        p3 = (rem - p2.astype(F32)).astype(BF16)
        cw = (jnp.dot(tri, p1, preferred_element_type=F32) + jnp.dot(tri, p2, preferred_element_type=F32)
              + jnp.dot(tri, p3, preferred_element_type=F32))
        cw_end = cw[C - 1:C, :]
        e_neg = jnp.exp(-cw)
        e_end = jnp.exp(cw_end - cw)
        kk_c = chunk(kk_s, s)
        bb = chunk(b_s, s)
        k_c = chunk(k_s, s)
        a_all = kk_c * jnp.exp(cw - lw)
        r_all = chunk(r_s, s) * jnp.exp(cw)
        b_all = bb * e_neg
        k_all = k_c * e_neg
        bd_all = bb * e_end
        kd_all = k_c * e_end
        v_all = chunk(v_s, s)
        decay = jnp.exp(cw_end)
        ar_h = jnp.concatenate([hat(a_all), hat(r_all)], axis=1)
        bk_h = jnp.concatenate([hat(b_all), hat(k_all)], axis=1)
        v_h = hat(v_all)
        st = state[...]
        g = mm_nt(ar_h, bk_h)
        low = jnp.where(strict, g[:, :C2, :C2], 0.0)
        ak = jnp.where(strict, g[:, :C2, C2:], 0.0)
        rbk = jnp.concatenate([jnp.where(incl, g[:, C2:, :C2], 0.0), jnp.where(incl, g[:, C2:, C2:], 0.0)], axis=2)
        ss = mm_nt(ar_h, st)
        x = -(ss[:, :C2] + mm(ak, v_h))
        t = mm(low, jnp.concatenate([low, x], axis=2))
        lp = t[:, :, :C2]
        x = x - t[:, :, C2:]
        for _ in range(max(int(np.ceil(np.log2(min(C, seq_rows)))), 2) - 2):
            t = mm(lp, jnp.concatenate([lp, x], axis=2))
            lp = t[:, :, :C2]
            x = x + t[:, :, C2:]
        x = x + mm(lp, x)
        xv = jnp.concatenate([x, v_h], axis=1)
        y_h = ss[:, C2:] + mm(rbk, xv)
        y = y_h[:, :C] + y_h[:, C:]
        for p in range(n_pairs):
            q, pl_ = divmod(p, hp)
            if short:
                first = pl.multiple_of((s * npar + q) * seq_rows, 8)
                y_s[pl.ds(first, seq_rows), pl_ * LANES:(pl_ + 1) * LANES] = y[p, :seq_rows]
            else:
                y_s[pl.ds(q * rows + s * C, C), pl_ * LANES:(pl_ + 1) * LANES] = y[p]
        bkd_h = jnp.concatenate([hat(bd_all), hat(kd_all)], axis=1)
        xv_t = jnp.stack([xv[p].T for p in range(n_pairs)])
        dec3 = jnp.stack([decay[:, p * LANES:(p + 1) * LANES] for p in range(n_pairs)])
        state[...] = st * dec3 + mm(xv_t, bkd_h)
        return carry

    if nseq == 1:
        for s in range(n_sub):
            sub_chunk(s, 0)
    else:
        def one_group(s, carry):
            for q in range(npar):
                load_state(s0_ref.at[s * npar + q], q * hp)
            sub_chunk(s, carry)
            for q in range(npar):
                store_state(st_ref.at[s * npar + q], q * hp)
            return carry

        lax.fori_loop(0, nseq // npar, one_group, 0)

    y = y_s[0:total]
    inv_n = 1.0 / HEAD_DIM
    mu = _head_sum(y, e_ref, et_ref) * inv_n
    yc = y - mu
    var = _head_sum(yc * yc, e_ref, et_ref) * inv_n
    yn = yc * lax.rsqrt(var + LNX_EPS) * lnw_ref[...] + lnb_ref[...]
    bonus = _head_sum(r_s[0:total] * k_s[0:total] * rk_ref[...], e_ref, et_ref) * v_s[0:total]
    out = h_in + _bdot((yn + bonus) * g_s[...], wo_ref[...])
    o_ref[...] = out.reshape(o_ref.shape)

    if nseq == 1:
        @pl.when(c == pl.num_programs(1) - 1)
        def _():
            for q in range(npar):
                store_state(st_ref.at[q], q * hp)


def _rwkv_layer(h, shift, s0, seq_len, lw):
    n = h.shape[0]
    nb = n // seq_len
    if seq_len % WKV_CHUNK == 0:
        npar = RWKV_PARALLEL_SEQS if nb % RWKV_PARALLEL_SEQS == 0 else 1
        nseq, rows = 1, _tile(seq_len, RWKV_STEP_ROWS // npar)
        steps, n_sub, total = seq_len // rows, rows // WKV_CHUNK, npar * rows
        buf_rows = total
        assert rows % WKV_CHUNK == 0
        h_in = h.reshape(nb, seq_len, D_MODEL)
        row_spec = pl.BlockSpec((npar, rows, D_MODEL), lambda bi, ci: (bi, ci, 0))
        sh_in = shift.reshape(nb, 1, D_MODEL)
        sh_spec = pl.BlockSpec((npar, 1, D_MODEL), lambda bi, ci: (bi, 0, 0))
        hn_shape, hn_spec = jax.ShapeDtypeStruct((nb, 1, D_MODEL), F32), sh_spec
        nst = npar
    else:
        assert seq_len < WKV_CHUNK and seq_len % 8 == 0
        nseq = _tile(nb, RWKV_SHORT_SEQS)
        npar = RWKV_PARALLEL_SEQS if nseq % RWKV_PARALLEL_SEQS == 0 else 1
        rows, steps, n_sub = nseq * seq_len, 1, 1
        total = rows
        buf_rows = rows + WKV_CHUNK - seq_len
        h_in = h
        row_spec = pl.BlockSpec((rows, D_MODEL), lambda bi, ci: (bi, 0))
        sh_in = jnp.repeat(shift, seq_len, axis=0)
        sh_spec = row_spec
        hn_shape, hn_spec = jax.ShapeDtypeStruct((n, D_MODEL), F32), sh_spec
        nst = nseq
    st_spec = pl.BlockSpec((nst, N_HEADS, HEAD_DIM, HEAD_DIM), lambda bi, ci: (bi, 0, 0, 0))
    consts = [lw["nw"], lw["mix"], lw["wrkv"], lw["w0"], lw["w1"], lw["w2"], lw["a0"], lw["a1"], lw["a2"],
              lw["g1"], lw["g2"], lw["kk"], lw["ka"], lw["lnw"], lw["lnb"], lw["rk"], lw["wo"], lw["e"], lw["et"]]
    seq_buf = pltpu.VMEM((buf_rows, D_MODEL), F32)
    out, hn, st = pl.pallas_call(
        functools.partial(_rwkv_layer_kernel, rows=rows, n_sub=n_sub, nseq=nseq, npar=npar),
        grid=(nb // nst, steps),
        in_specs=[row_spec, sh_spec, st_spec] + [_const_spec(c.shape) for c in consts],
        out_specs=[row_spec, hn_spec, st_spec],
        out_shape=[jax.ShapeDtypeStruct(h_in.shape, F32), hn_shape, jax.ShapeDtypeStruct(s0.shape, F32)],
        scratch_shapes=[pltpu.VMEM((npar * N_HEADS // 2, LANES, LANES), F32), pltpu.VMEM((8, D_MODEL), F32)]
                       + [seq_buf] * 6 + [pltpu.VMEM((total, D_MODEL), F32), seq_buf],
        compiler_params=_params("arbitrary", "arbitrary", n_in=3 + len(consts)),
        name="rwkv_layer",
    )(h_in, sh_in, s0, *consts)
    out = out.reshape(n, D_MODEL)
    shift_out = hn.reshape(nb, D_MODEL) if nseq == 1 else hn.reshape(nb, seq_len, D_MODEL)[:, -1]
    return out, shift_out, st


def _t5_buckets(dist):
    d = np.asarray(dist, dtype=np.int64)
    max_exact = REL_BUCKETS // 2
    large = max_exact + (np.log(np.maximum(d, 1) / max_exact) / np.log(REL_MAX_DIST / max_exact)
                         * (REL_BUCKETS - max_exact)).astype(np.int32)
    large = np.minimum(large, REL_BUCKETS - 1)
    return np.where(d < max_exact, d, large).astype(np.int32)


def _bucket_rows(table, buckets):
    onehot = jnp.asarray(np.eye(REL_BUCKETS, dtype=np.float32)[buckets])
    return jnp.dot(onehot, table.astype(F32), precision=lax.Precision.HIGHEST)


def _toeplitz(tab, n_rows, n_cols):
    period = tab.shape[-1]
    assert period >= n_rows + n_cols - 1 and n_cols <= period - 1
    lead = tab.shape[:-1]
    flat = jnp.broadcast_to(tab[..., None, :], lead + (n_rows, period)).reshape(lead + (-1,))
    skew = flat[..., :n_rows * (period - 1)].reshape(lead + (n_rows, period - 1))
    return skew[..., :n_cols]


def _band_bias(rel_bias, group):
    win, dil = DILATION_GROUPS[group]
    assert win // dil == Q_BLOCK
    period = 3 * Q_BLOCK - 1
    idx = np.arange(period)
    m = Q_BLOCK - np.where(idx < 2 * Q_BLOCK, idx, idx - period)
    valid = (m >= 0) & (m <= Q_BLOCK)
    buckets = _t5_buckets(dil * np.clip(m, 0, Q_BLOCK))
    tbl = _bucket_rows(rel_bias[:, group * N_HEADS:(group + 1) * N_HEADS], buckets).T
    general = _toeplitz(jnp.where(valid[None], tbl, NEG_INF), Q_BLOCK, 2 * Q_BLOCK)
    first = jnp.where((np.arange(2 * Q_BLOCK) >= Q_BLOCK)[None, None, :], general, NEG_INF)
    return jnp.stack([first, general])


_HEAD_OF_SEG = np.array([4 * (2 * (s // 8) + s % 2) + (s // 2) % 4 for s in range(N_HEADS)])


def _proj_rm_kernel(x_ref, g_ref, w_ref, *refs, natural, lane_ranges, tm):
    scr = refs[-1]
    outs = refs[:-1]
    y = _bdot(_rms(x_ref[...], g_ref[...]), w_ref[...])
    if natural:
        outs[0][...] = y
        outs = outs[1:]
    for c in range(scr.shape[0]):
        scr[c] = y[:, c * LANES:(c + 1) * LANES]
    for gi, (_, dil) in enumerate(DILATION_GROUPS):
        lo, hi = lane_ranges[gi]
        if dil == 1:
            outs[gi][0] = y[:, lo:hi].astype(BF16)
            continue
        for rho in range(dil):
            rows = [scr[c, pl.ds(rho, tm // dil, stride=dil), :] for c in range(lo // LANES, hi // LANES)]
            outs[gi][rho] = jnp.concatenate(rows, axis=1).astype(BF16)


def _proj_rm(h, g, w, nb, seq_len, natural, lane_ranges, name):
    n = h.shape[0]
    tm = _tile(n, PROJ_TILE)
    nout = w.shape[1]
    assert seq_len % tm == 0 and all(tm % (16 * dil) == 0 for _, dil in DILATION_GROUPS)
    tps = seq_len // tm
    out_shape, out_specs = [], []
    if natural:
        out_shape.append(jax.ShapeDtypeStruct((n, nout), F32))
        out_specs.append(pl.BlockSpec((tm, nout), lambda i: (i, 0)))
    for (_, dil), (lo, hi) in zip(DILATION_GROUPS, lane_ranges):
        out_shape.append(jax.ShapeDtypeStruct((nb, dil, seq_len // dil, hi - lo), BF16))
        out_specs.append(pl.BlockSpec((None, dil, tm // dil, hi - lo), lambda i: (i // tps, 0, i % tps, 0)))
    return pl.pallas_call(
        functools.partial(_proj_rm_kernel, natural=natural, lane_ranges=lane_ranges, tm=tm),
        grid=(n // tm,),
        in_specs=[pl.BlockSpec((tm, D_MODEL), lambda i: (i, 0)), _const_spec((1, D_MODEL)), _const_spec(w.shape)],
        out_specs=out_specs,
        out_shape=out_shape,
        scratch_shapes=[pltpu.VMEM((nout // LANES, tm, LANES), F32)],
        compiler_params=_params("arbitrary", n_in=3),
        name=name,
    )(h, g, w)


def _attn_kernel(q_ref, kv_ref, halo_ref, bias_ref, o_ref, lse_ref, kvbuf, *, nq):
    i = pl.program_id(0)
    dil = q_ref.shape[0]
    kvd = KV_HEADS * HEAD_DIM
    nt = (((1,), (1,)), ((), ()))
    kvbuf[:, :Q_BLOCK, :] = halo_ref[...]
    kvbuf[:, Q_BLOCK:, :] = kv_ref[...]
    lane = lax.broadcasted_iota(jnp.int32, (Q_BLOCK, LANES), 1)
    low_half = lane < HEAD_DIM
    lane_row = lax.broadcasted_iota(jnp.int32, (1, LANES), 1)
    keep_lo = (lane_row < HEAD_DIM).astype(BF16)
    keep_hi = (lane_row >= HEAD_DIM).astype(BF16)

    def block(u, carry):
        rho = lax.div(u, nq)
        j = u - rho * nq
        r0 = pl.multiple_of(j * Q_BLOCK, Q_BLOCK)
        qb = q_ref[rho, pl.ds(r0, Q_BLOCK), :]
        kvb = kvbuf[rho, pl.ds(r0, 2 * Q_BLOCK), :]
        bsel = jnp.where(jnp.logical_and(i == 0, j == 0), 0, 1)
        lse_tile = jnp.zeros((Q_BLOCK, LANES), F32)
        for G in range(KV_HEADS // 2):
            kg = kvb[:, G * LANES:(G + 1) * LANES]
            vg = kvb[:, kvd + G * LANES:kvd + (G + 1) * LANES]
            vcat = jnp.concatenate([vg * keep_lo, vg * keep_hi], axis=0)
            pieces = []
            for r in range(Q_PER_KV):
                qg = qb[:, (G * Q_PER_KV + r) * LANES:(G * Q_PER_KV + r + 1) * LANES]
                pieces += [qg * keep_lo, qg * keep_hi]
            s_all = lax.dot_general(jnp.concatenate(pieces, axis=0), kg, nt, preferred_element_type=F32)
            for r in range(Q_PER_KV):
                grp = G * Q_PER_KV + r
                parts = []
                for half in range(2):
                    s = s_all[(2 * r + half) * Q_BLOCK:(2 * r + half + 1) * Q_BLOCK]
                    s = s + bias_ref[bsel, int(_HEAD_OF_SEG[2 * grp + half])]
                    m = jnp.max(s, axis=-1, keepdims=True)
                    p = jnp.exp(s - m)
                    l = jnp.sum(p, axis=-1, keepdims=True)
                    parts.append((p.astype(BF16), l, m + jnp.log(l)))
                (p0, l0, e0), (p1, l1, e1) = parts
                o = jnp.dot(jnp.concatenate([p0, p1], axis=1), vcat, preferred_element_type=F32)
                o = (o / jnp.where(low_half, l0, l1)).astype(o_ref.dtype)
                o_ref[rho, pl.ds(r0, Q_BLOCK), grp * LANES:(grp + 1) * LANES] = o
                lse_tile = jnp.where(lane == 2 * grp, e0, lse_tile)
                lse_tile = jnp.where(lane == 2 * grp + 1, e1, lse_tile)
        lse_ref[rho, pl.ds(r0, Q_BLOCK), :] = lse_tile
        return carry

    lax.fori_loop(0, dil * nq, block, 0, unroll=2)


def _attn_group(q_rm, kv_rm, bias):
    nb, dil, tsub, _ = q_rm.shape
    kvw = kv_rm.shape[-1]
    rows = ATTN_SPAN // dil
    nq = rows // Q_BLOCK
    assert tsub % rows == 0 and nq >= 1
    span_spec = lambda width: pl.BlockSpec((None, dil, rows, width), lambda i, b: (b, 0, i, 0))
    return pl.pallas_call(
        functools.partial(_attn_kernel, nq=nq),
        grid=(tsub // rows, nb),
        in_specs=[span_spec(D_MODEL), span_spec(kvw),
                  pl.BlockSpec((None, dil, Q_BLOCK, kvw), lambda i, b: (b, 0, jnp.maximum(i * nq - 1, 0), 0)),
                  _const_spec(bias.shape)],
        out_specs=[span_spec(D_MODEL), span_spec(LANES)],
        out_shape=[jax.ShapeDtypeStruct((nb, dil, tsub, D_MODEL), BF16),
                   jax.ShapeDtypeStruct((nb, dil, tsub, LANES), F32)],
        scratch_shapes=[pltpu.VMEM((dil, Q_BLOCK + rows, kvw), BF16)],
        compiler_params=_params("arbitrary", "arbitrary", n_in=4),
        name=f"attn_d{dil}",
    )(q_rm, kv_rm, kv_rm, bias)


def _attn_out_kernel(o0_ref, o1_ref, o2_ref, l0_ref, l1_ref, l2_ref, h_ref, wo_ref, et_ref, out_ref, *scr, tm):
    outs, lses = [], []
    for gi, (o_ref, l_ref) in enumerate(((o0_ref, l0_ref), (o1_ref, l1_ref), (o2_ref, l2_ref))):
        dil = DILATION_GROUPS[gi][1]
        if dil == 1:
            outs.append(o_ref[0].astype(F32))
            lses.append(l_ref[0])
            continue
        so, sl = scr[2 * gi], scr[2 * gi + 1]
        n_tiles = so.shape[0]
        for rho in range(dil):
            rows = pl.ds(rho, tm // dil, stride=dil)
            for c in range(n_tiles):
                so[c, rows, :] = o_ref[rho, :, c * LANES:(c + 1) * LANES].astype(F32)
            sl[rows, :] = l_ref[rho]
        outs.append(jnp.concatenate([so[c] for c in range(n_tiles)], axis=1))
        lses.append(sl[...])
    l0, l1, l2 = lses
    m = jnp.maximum(jnp.maximum(l0, l1), l2)
    w0, w1, w2 = jnp.exp(l0 - m), jnp.exp(l1 - m), jnp.exp(l2 - m)
    inv = 1.0 / (w0 + w1 + w2)
    et = et_ref[...]
    att = _bdot(w0 * inv, et) * outs[0] + _bdot(w1 * inv, et) * outs[1] + _bdot(w2 * inv, et) * outs[2]
    out_ref[...] = h_ref[...] + _bdot(att, wo_ref[...])


def _attn_out(outs, lses, h, wo, et, seq_len):
    n = h.shape[0]
    tm = _tile(n, PROJ_TILE)
    tps = seq_len // tm
    row_spec = pl.BlockSpec((tm, D_MODEL), lambda i: (i, 0))

    def rm_spec(dil, width):
        return pl.BlockSpec((None, dil, tm // dil, width), lambda i: (i // tps, 0, i % tps, 0))

    dils = [dil for _, dil in DILATION_GROUPS]
    scratch = []
    for _ in dils:
        scratch += [pltpu.VMEM((D_MODEL // LANES, tm, LANES), F32), pltpu.VMEM((tm, LANES), F32)]
    return pl.pallas_call(
        functools.partial(_attn_out_kernel, tm=tm),
        grid=(n // tm,),
        in_specs=[rm_spec(d, D_MODEL) for d in dils] + [rm_spec(d, LANES) for d in dils]
                 + [row_spec, _const_spec(wo.shape), _const_spec(et.shape)],
        out_specs=row_spec,
        out_shape=jax.ShapeDtypeStruct((n, D_MODEL), F32),
        scratch_shapes=scratch,
        compiler_params=_params("arbitrary", n_in=9),
        name="attn_out",
    )(*outs, *lses, h, wo, et)


def _decode_bias(rel_bias, seq_len, cache_len):
    ncol = cache_len + LANES
    period = seq_len + ncol - 1
    idx = np.arange(period)
    dist = cache_len - np.where(idx < ncol, idx, idx - period)
    buckets = _t5_buckets(np.clip(dist, 0, MAX_WINDOW))
    tabs = []
    for g, (win, dil) in enumerate(DILATION_GROUPS):
        valid = (dist >= 0) & (dist % dil == 0) & (dist <= win)
        tbl = _bucket_rows(rel_bias[:, g * N_HEADS:(g + 1) * N_HEADS], buckets).T
        tabs.append(jnp.where(valid[None], tbl, NEG_INF))
    rows = _toeplitz(jnp.stack(tabs), seq_len, ncol)
    bias = rows.reshape(N_GROUPS * N_HEADS * seq_len, ncol)
    return bias[:, :cache_len], bias[:, cache_len:]


def _attn_decode_kernel(q_ref, cache_ref, kvn_ref, h_ref, bc_ref, bn_ref, wo_ref, out_ref, *, seq_len):
    kvd = KV_HEADS * HEAD_DIM
    nslot = N_GROUPS * N_HEADS
    rows_g = N_HEADS * seq_len
    nt = (((1,), (1,)), ((), ()))
    cache = cache_ref[...]
    kc = cache[:, :kvd].astype(BF16)
    vc = cache[:, kvd:].astype(BF16)
    kvn = kvn_ref[...]
    pad = jnp.zeros((LANES - seq_len, kvd), F32)
    kn = jnp.concatenate([kvn[:, :kvd], pad], axis=0).astype(BF16)
    vn = jnp.concatenate([kvn[:, kvd:], pad], axis=0).astype(BF16)
    lhs = jnp.concatenate([q_ref[:, s * kvd:(s + 1) * kvd] for s in range(nslot)], axis=0).astype(BF16)
    cache_len = kc.shape[0]
    sn = lax.dot_general(lhs, kn, nt, preferred_element_type=F32) + bn_ref[...]
    scs, m = [], None
    for g, (win, _) in enumerate(DILATION_GROUPS):
        c0 = (cache_len - min(win, cache_len)) // LANES * LANES
        rows = slice(g * rows_g, (g + 1) * rows_g)
        sc = lax.dot_general(lhs[rows], kc[c0:], nt, preferred_element_type=F32) + bc_ref[rows, c0:]
        scs.append((sc, c0, rows))
        m_g = jnp.maximum(jnp.max(sc, axis=-1, keepdims=True), jnp.max(sn[rows], axis=-1, keepdims=True))
        m = m_g if m is None else jnp.maximum(m, m_g)
    l = jnp.zeros((rows_g, 1), F32)
    num = jnp.zeros((rows_g, kvd), F32)
    for sc, c0, rows in scs:
        pc = jnp.exp(sc - m)
        pn = jnp.exp(sn[rows] - m)
        l = l + jnp.sum(pc, axis=-1, keepdims=True) + jnp.sum(pn, axis=-1, keepdims=True)
        num = (num + jnp.dot(pc.astype(BF16), vc[c0:], preferred_element_type=F32)
               + jnp.dot(pn.astype(BF16), vn, preferred_element_type=F32))
    row = lax.broadcasted_iota(jnp.int32, (rows_g, kvd), 0)
    lane = lax.broadcasted_iota(jnp.int32, (rows_g, kvd), 1)
    own = (row // (Q_PER_KV * seq_len)) == (lane // HEAD_DIM)
    att = jnp.where(own, num / l, 0.0)
    out = h_ref[...]
    for r in range(Q_PER_KV):
        a_r = att[r * seq_len:(r + 1) * seq_len]
        for c in range(1, KV_HEADS):
            a_r = a_r + att[(c * Q_PER_KV + r) * seq_len:(c * Q_PER_KV + r + 1) * seq_len]
        out = out + _bdot(a_r, wo_ref[r])
    out_ref[...] = out


def _attn_decode(q, cache, kv_new, h, bias_c, bias_n, wo_r, nb, seq_len):
    cache_len = cache.shape[1]
    qw = q.shape[1]
    kvw = 2 * KV_HEADS * HEAD_DIM
    return pl.pallas_call(
        functools.partial(_attn_decode_kernel, seq_len=seq_len),
        grid=(nb,),
        in_specs=[pl.BlockSpec((seq_len, qw), lambda b: (b, 0)),
                  pl.BlockSpec((None, cache_len, kvw), lambda b: (b, 0, 0)),
                  pl.BlockSpec((seq_len, kvw), lambda b: (b, 0)),
                  pl.BlockSpec((seq_len, D_MODEL), lambda b: (b, 0)),
                  _const_spec(bias_c.shape), _const_spec(bias_n.shape), _const_spec(wo_r.shape)],
        out_specs=pl.BlockSpec((seq_len, D_MODEL), lambda b: (b, 0)),
        out_shape=jax.ShapeDtypeStruct((nb * seq_len, D_MODEL), F32),
        compiler_params=_params("arbitrary", n_in=7),
        name="attn_decode",
    )(q, cache, kv_new, h, bias_c, bias_n, wo_r)


def _prep_weights(norm_w, ffn1_wi, ffn1_wo, ffn2_wi, ffn2_wo, pe_proj, pe_gate,
                  rwkv_mix, rwkv_wrkv, rwkv_wo, rwkv_w0, rwkv_w1, rwkv_w2, rwkv_a0, rwkv_a1, rwkv_a2,
                  rwkv_g1, rwkv_g2, rwkv_kk, rwkv_ka, rwkv_rk, rwkv_lnx_w, rwkv_lnx_b,
                  attn_wq, attn_wo, kv_norm, w_kv, rel_bias, final_norm):
    def row(v):
        return v.reshape(1, -1).astype(F32)

    def pad_cols(w, n):
        return jnp.pad(w, ((0, 0), (0, n - w.shape[1]))).astype(BF16)

    def pad_rows(w, n):
        return jnp.pad(w, ((0, n - w.shape[0]), (0, 0))).astype(BF16)

    head_of_lane = np.arange(D_MODEL) // HEAD_DIM
    e = jnp.asarray(head_of_lane[:, None] == np.arange(LANES)[None, :], BF16)
    et = jnp.asarray(np.arange(LANES)[:, None] == head_of_lane[None, :], BF16)

    depth = norm_w.shape[0]
    layers = []
    for i in range(depth):
        layers.append(dict(
            nw=[row(norm_w[i, j]) for j in range(4)],
            ffn1=(ffn1_wi.astype(F32), ffn1_wo.astype(F32)), ffn2=(ffn2_wi.astype(F32), ffn2_wo.astype(F32)),
            pe_gate=pe_gate[i].astype(BF16), pe_proj=pe_proj[i].astype(BF16)))
    n_a = depth // 2
    rw = []
    for i in range(n_a):
        rw.append(dict(
            nw=row(norm_w[i, 1]), mix=rwkv_mix[i].astype(F32), wrkv=rwkv_wrkv[i].astype(BF16),
            w0=row(rwkv_w0[i]), w1=pad_cols(rwkv_w1[i], LANES), w2=pad_rows(rwkv_w2[i], LANES),
            a0=row(rwkv_a0[i]), a1=pad_cols(rwkv_a1[i], LANES), a2=pad_rows(rwkv_a2[i], LANES),
            g1=pad_cols(rwkv_g1[i], 2 * LANES), g2=pad_rows(rwkv_g2[i], 2 * LANES),
            kk=row(rwkv_kk[i]), ka=row(rwkv_ka[i]), rk=row(rwkv_rk[i]),
            lnw=row(rwkv_lnx_w[i]), lnb=row(rwkv_lnx_b[i]), wo=rwkv_wo[i].astype(BF16), e=e, et=et))
    scale = HEAD_DIM ** -0.5
    at = []
    for j in range(depth - n_a):
        wq = attn_wq[j] * scale
        wo_r = attn_wo[j].reshape(KV_HEADS, Q_PER_KV, HEAD_DIM, D_MODEL).transpose(1, 0, 2, 3)
        wo_r = wo_r.reshape(Q_PER_KV, KV_HEADS * HEAD_DIM, D_MODEL).astype(BF16)
        wq_seg = wq.reshape(D_MODEL, N_GROUPS, 2, 2, Q_PER_KV, HEAD_DIM).transpose(0, 1, 2, 4, 3, 5)
        wq_seg = wq_seg.reshape(D_MODEL, -1)
        wo_seg = attn_wo[j].reshape(2, 2, Q_PER_KV, HEAD_DIM, D_MODEL).transpose(0, 2, 1, 3, 4)
        wo_seg = wo_seg.reshape(D_MODEL, D_MODEL)
        at.append(dict(wq_seg=wq_seg.astype(BF16), wo_seg=wo_seg.astype(BF16), wo_r=wo_r))
    return dict(layers=layers, rwkv=rw, attn=at, kv_norm=row(kv_norm), w_kv=w_kv.astype(BF16),
                final_norm=row(final_norm), rel_bias=rel_bias, et=et)


def _trunks(streams, w):
    depth = len(w["layers"])
    n_a = depth // 2
    kvw = 2 * KV_HEADS * HEAD_DIM
    st = []
    for sd in streams:
        nb, seq_len, _ = sd["x"].shape
        n = nb * seq_len
        st.append(dict(nb=nb, seq_len=seq_len, n=n, h=sd["x"].reshape(n, D_MODEL).astype(F32),
                       p=sd["p"].reshape(depth, n, PLE_DIM).astype(F32), wkv_out=[], shift_out=[],
                       kv_rows=None, kv_rm=None, **{k: sd[k] for k in ("wkv0", "shift0", "cache")}))

    def ffn(nw, weights, **kw):
        for t, h in zip(st, _ffn([t["h"] for t in st], nw, *weights, **kw)):
            t["h"] = h

    for i in range(depth):
        lw = w["layers"][i]
        if i == n_a:
            for t in st:
                if t["cache"] is None:
                    t["kv_rows"], *t["kv_rm"] = _proj_rm(t["h"], w["kv_norm"], w["w_kv"], t["nb"], t["seq_len"], True,
                                                         [(0, kvw)] * N_GROUPS, "kv_proj")
                else:
                    t["kv_rows"] = _norm_mm(t["h"], w["kv_norm"], w["w_kv"], F32, name="kv_proj")
        ffn(lw["nw"][0], lw["ffn1"], layer=i)
        for t in st:
            nb, seq_len, n, h = t["nb"], t["seq_len"], t["n"], t["h"]
            if i < n_a:
                h, sh, state = _rwkv_layer(h, t["shift0"][i].astype(F32), t["wkv0"][i].astype(F32), seq_len,
                                           w["rwkv"][i])
                t["wkv_out"].append(state)
                t["shift_out"].append(sh)
            elif t["cache"] is None:
                al = w["attn"][i - n_a]
                q_rm = _proj_rm(h, lw["nw"][1], al["wq_seg"], nb, seq_len, False,
                                [(gi * D_MODEL, (gi + 1) * D_MODEL) for gi in range(N_GROUPS)], "q_proj")
                outs, lses = [], []
                for gi in range(N_GROUPS):
                    o, lse = _attn_group(q_rm[gi], t["kv_rm"][gi], _band_bias(w["rel_bias"], gi))
                    outs.append(o)
                    lses.append(lse)
                h = _attn_out(outs, lses, h, al["wo_seg"], w["et"], seq_len)
            else:
                al = w["attn"][i - n_a]
                q = _norm_mm(h, lw["nw"][1], al["wq_seg"], F32, name="q_proj_decode")
                q = q.reshape(n, N_GROUPS, 2, Q_PER_KV, 2, HEAD_DIM).transpose(0, 1, 2, 4, 3, 5)
                q = q.reshape(n, N_GROUPS, KV_HEADS, Q_PER_KV, 1, HEAD_DIM)
                q = q * jnp.eye(KV_HEADS, dtype=F32).reshape(1, 1, KV_HEADS, 1, KV_HEADS, 1)
                q = q.reshape(n, N_GROUPS * N_HEADS * KV_HEADS * HEAD_DIM)
                bias_c, bias_n = _decode_bias(w["rel_bias"], seq_len, t["cache"].shape[1])
                h = _attn_decode(q, t["cache"], t["kv_rows"], h, bias_c, bias_n, al["wo_r"], nb, seq_len)
            t["h"] = h
        pe = ([t["p"] for t in st], lw["nw"][3], lw["pe_gate"], lw["pe_proj"], w["final_norm"])
        ffn(lw["nw"][2], lw["ffn2"], pe=pe, layer=i, final=(i == depth - 1))
    results = []
    for t, sd in zip(st, streams):
        dt = sd["x"].dtype
        y = t["h"].reshape(t["nb"], t["seq_len"], D_MODEL).astype(dt)
        kv_rows = t["kv_rows"].reshape(t["nb"], t["seq_len"], kvw).astype(dt)
        results.append((y, jnp.stack(t["wkv_out"]).astype(dt), jnp.stack(t["shift_out"]).astype(dt), kv_rows))
    return results


def _kv_heads(kv):
    return kv.reshape(kv.shape[:2] + (2, KV_HEADS, HEAD_DIM))


def _trunk(x, p, wkv0, shift0, cache, w):
    y, wkv, shift, kv = _trunks([dict(x=x, p=p, wkv0=wkv0, shift0=shift0, cache=cache)], w)[0]
    return y, wkv, shift, _kv_heads(kv)


def kernel(x_prompt, x_sample, state_wkv, state_shift, cache_kv, p_prompt, p_sample, norm_w, ffn1_wi, ffn1_wo, ffn2_wi, ffn2_wo, pe_proj, pe_gate, rwkv_mix, rwkv_wrkv, rwkv_wo, rwkv_w0, rwkv_w1, rwkv_w2, rwkv_a0, rwkv_a1, rwkv_a2, rwkv_g1, rwkv_g2, rwkv_kk, rwkv_ka, rwkv_rk, rwkv_lnx_w, rwkv_lnx_b, attn_wq, attn_wo, kv_norm, w_kv, rel_bias, final_norm):
    w = _prep_weights(norm_w, ffn1_wi, ffn1_wo, ffn2_wi, ffn2_wo, pe_proj, pe_gate,
                      rwkv_mix, rwkv_wrkv, rwkv_wo, rwkv_w0, rwkv_w1, rwkv_w2, rwkv_a0, rwkv_a1, rwkv_a2,
                      rwkv_g1, rwkv_g2, rwkv_kk, rwkv_ka, rwkv_rk, rwkv_lnx_w, rwkv_lnx_b,
                      attn_wq, attn_wo, kv_norm, w_kv, rel_bias, final_norm)
    n_a = norm_w.shape[0] // 2
    nb, seq_len, _ = x_prompt.shape
    wkv0 = jnp.zeros((n_a, nb, N_HEADS, HEAD_DIM, HEAD_DIM), F32)
    shift0 = jnp.zeros((n_a, nb, D_MODEL), x_prompt.dtype)
    cache = cache_kv.reshape(cache_kv.shape[0], cache_kv.shape[1], 2 * KV_HEADS * HEAD_DIM).astype(F32)
    (y_p, wkv_p, shift_p, kv_p), (y_s, wkv_s, shift_s, kv_s) = _trunks(
        [dict(x=x_prompt, p=p_prompt, wkv0=wkv0, shift0=shift0, cache=None),
         dict(x=x_sample, p=p_sample, wkv0=state_wkv, shift0=state_shift, cache=cache)], w)
    kv_prompt = _kv_heads(kv_p[:, seq_len - min(MAX_WINDOW, seq_len):])
    return (y_p, y_s, wkv_p, shift_p, kv_prompt, wkv_s, shift_s, _kv_heads(kv_s))
```

```python
import functools

import numpy as np
import jax
import jax.numpy as jnp
from jax import lax
from jax.experimental import pallas as pl
from jax.experimental.pallas import tpu as pltpu

F32 = jnp.float32
BF16 = jnp.bfloat16

D_MODEL = 1024
D_FF = 2816
PLE_DIM = 256
RMS_EPS = 1e-6
HEAD_DIM = 64
N_HEADS = D_MODEL // HEAD_DIM
LNX_EPS = 64e-5
KV_HEADS = 4
Q_PER_KV = N_HEADS // KV_HEADS
DILATION_GROUPS = ((128, 1), (512, 4), (2048, 16))
N_GROUPS = len(DILATION_GROUPS)
MAX_WINDOW = 2048
REL_BUCKETS = 32
REL_MAX_DIST = 2048
NEG_INF = -1e30

LANES = 128
FFN_CHUNK = 256
N_FFN_CHUNKS = D_FF // FFN_CHUNK
TOKEN_TILE = 256
FFN_TILE = 512
PROJ_TILE = 1024
WKV_CHUNK = 64
RWKV_SHORT_SEQS = 8
RWKV_PARALLEL_SEQS = 2
RWKV_STEP_ROWS = 512
Q_BLOCK = 128
ATTN_SPAN = 2048
VMEM_LIMIT = 56 * 1024 * 1024
EXP_MINUS_HALF = 0.6065306597126334


def _params(*sem, n_in):
    return pltpu.CompilerParams(dimension_semantics=sem, vmem_limit_bytes=VMEM_LIMIT,
                                allow_input_fusion=[True] * n_in)


def _const_spec(shape):
    return pl.BlockSpec(shape, lambda *_: (0,) * len(shape))


def _tile(n, pref=TOKEN_TILE):
    t = min(n, pref)
    while n % t:
        t -= 8
    return t


def _rms(x, g):
    return x * lax.rsqrt(jnp.mean(x * x, axis=-1, keepdims=True) + RMS_EPS) * g


def _bdot(a, b):
    return jnp.dot(a.astype(BF16), b, preferred_element_type=F32)


def _head_sum(x, e_ref, et_ref):
    return _bdot(_bdot(x, e_ref[...]), et_ref[...])


def _ffn_kernel(*refs, bounds, with_pe, final):
    ns = len(bounds)
    xs, (g_ref, wi_ref, wo_ref), rest = refs[:ns], refs[ns:ns + 3], refs[ns + 3:]
    if with_pe:
        ps, (gp_ref, wgate_ref, wproj_ref, gf_ref), outs = rest[:ns], rest[ns:ns + 4], rest[ns + 4:]
    else:
        outs = rest

    def body(k):
        x = xs[k][...]
        xn = _rms(x, g_ref[...]).astype(BF16)
        acc = jnp.zeros_like(x)
        for j in range(N_FFN_CHUNKS):
            lo, hi = j * FFN_CHUNK, (j + 1) * FFN_CHUNK
            gate = jnp.dot(xn, wi_ref[:, lo:hi].astype(BF16), preferred_element_type=F32)
            up = jnp.dot(xn, wi_ref[:, D_FF + lo:D_FF + hi].astype(BF16), preferred_element_type=F32)
            act = (gate * jax.nn.sigmoid(gate) * up).astype(BF16)
            acc = acc + jnp.dot(act, wo_ref[lo:hi, :].astype(BF16), preferred_element_type=F32)
        y = x + 0.5 * acc
        if with_pe:
            gate = jax.nn.sigmoid(_bdot(_rms(y, gp_ref[...]), wgate_ref[...]))
            y = y + gate * _bdot(ps[k][...], wproj_ref[...])
            if final:
                y = _rms(y, gf_ref[...])
        outs[k][...] = y

    if ns == 1:
        body(0)
    else:
        i = pl.program_id(0)
        for k, (lo, hi) in enumerate(bounds):
            pl.when(jnp.logical_and(i >= lo, i < hi))(functools.partial(body, k))


def _ffn(hs, g, wi, wo, pe=None, layer=0, final=False):
    tiles = [_tile(h.shape[0], FFN_TILE) for h in hs]
    counts = [h.shape[0] // t for h, t in zip(hs, tiles)]
    bounds = [(sum(counts[:k]), sum(counts[:k + 1])) for k in range(len(hs))]

    def rows(k, width, lead=()):
        lo, cnt = bounds[k][0], counts[k]
        return pl.BlockSpec(tuple(None for _ in lead) + (tiles[k], width),
                            lambda i: lead + (jnp.clip(i - lo, 0, cnt - 1), 0))

    args = list(hs) + [g, wi, wo]
    in_specs = ([rows(k, D_MODEL) for k in range(len(hs))]
                + [_const_spec((1, D_MODEL)),
                   pl.BlockSpec((None,) + wi.shape[1:], lambda i: (layer, 0, 0)),
                   pl.BlockSpec((None,) + wo.shape[1:], lambda i: (layer, 0, 0))])
    if pe is not None:
        args += list(pe[0]) + list(pe[1:])
        in_specs += ([rows(k, PLE_DIM, (layer,)) for k in range(len(hs))]
                     + [_const_spec(a.shape) for a in pe[1:]])
    return pl.pallas_call(
        functools.partial(_ffn_kernel, bounds=tuple(bounds), with_pe=pe is not None, final=final),
        grid=(bounds[-1][1],),
        in_specs=in_specs,
        out_specs=[rows(k, D_MODEL) for k in range(len(hs))],
        out_shape=[jax.ShapeDtypeStruct((h.shape[0], D_MODEL), F32) for h in hs],
        compiler_params=_params("arbitrary", n_in=len(args)),
        name="ffn_pe" if pe is not None else "ffn",
    )(*args)


def _norm_mm_kernel(x_ref, g_ref, w_ref, o_ref):
    o_ref[...] = _bdot(_rms(x_ref[...], g_ref[...]), w_ref[...]).astype(o_ref.dtype)


def _norm_mm(h, g, w, out_dtype, col_tile=None, name="norm_mm"):
    n = h.shape[0]
    tm = _tile(n)
    nout = w.shape[1]
    tn = nout if col_tile is None else col_tile
    return pl.pallas_call(
        _norm_mm_kernel,
        grid=(n // tm, nout // tn),
        in_specs=[pl.BlockSpec((tm, D_MODEL), lambda i, j: (i, 0)), _const_spec((1, D_MODEL)),
                  pl.BlockSpec((D_MODEL, tn), lambda i, j: (0, j))],
        out_specs=pl.BlockSpec((tm, tn), lambda i, j: (i, j)),
        out_shape=jax.ShapeDtypeStruct((n, nout), out_dtype),
        compiler_params=_params("arbitrary", "arbitrary", n_in=3),
        name=name,
    )(h, g, w)


def _rwkv_layer_kernel(h_ref, sh_ref, s0_ref, nw_ref, mix_ref, wrkv_ref, w0_ref, w1_ref, w2_ref, a0_ref, a1_ref,
                       a2_ref, g1_ref, g2_ref, kk_ref, ka_ref, lnw_ref, lnb_ref, rk_ref, wo_ref, e_ref, et_ref,
                       o_ref, hn_o, st_ref,
                       state, prev, r_s, lw_s, k_s, v_s, kk_s, b_s, g_s, y_s, *, rows, n_sub, nseq, npar):
    c = pl.program_id(1)
    C = WKV_CHUNK
    C2 = 2 * C
    hp = N_HEADS // 2
    n_pairs = npar * hp
    seq_rows = rows // nseq
    short = seq_rows % C != 0
    total = rows if short else npar * rows

    def load_state(s0, base=0):
        for h in range(N_HEADS):
            lo = (h % 2) * HEAD_DIM
            state[base + h // 2] = jnp.zeros((LANES, LANES), F32)
        for h in range(N_HEADS):
            lo = (h % 2) * HEAD_DIM
            state[base + h // 2, lo:lo + HEAD_DIM, lo:lo + HEAD_DIM] = s0[h]

    def store_state(st, base=0):
        for h in range(N_HEADS):
            lo = (h % 2) * HEAD_DIM
            st[h] = state[base + h // 2, lo:lo + HEAD_DIM, lo:lo + HEAD_DIM]

    if nseq == 1:
        @pl.when(c == 0)
        def _():
            for q in range(npar):
                load_state(s0_ref.at[q], q * hp)
                prev[q:q + 1, :] = sh_ref[q]
    if short:
        for ref in (r_s, lw_s, k_s, v_s, kk_s, b_s):
            ref[rows:, :] = jnp.zeros((ref.shape[0] - rows, D_MODEL), F32)

    nw = nw_ref[...]
    h_in = h_ref[...].reshape(total, D_MODEL) if nseq == 1 else h_ref[...]
    hn = _rms(h_in, nw)
    row = lax.broadcasted_iota(jnp.int32, hn.shape, 0)
    rolled = pltpu.roll(hn, 1, 0)
    if nseq == 1:
        x_prev = rolled
        for q in range(npar):
            x_prev = jnp.where(row == q * rows, prev[q:q + 1, :], x_prev)
            last = hn[(q + 1) * rows - 1:(q + 1) * rows]
            prev[q:q + 1, :] = last
            hn_o[q] = last
    else:
        x_prev = jnp.where(lax.rem(row, seq_rows) == 0, sh_ref[...], rolled)
        hn_o[...] = hn
    xx = x_prev - hn
    mix = mix_ref[...]
    xr, xw, xk, xv, xa, xg = (hn + xx * mix[j:j + 1] for j in range(6))
    r = _bdot(xr, wrkv_ref[0])
    k = _bdot(xk, wrkv_ref[1])
    v = _bdot(xv, wrkv_ref[2])
    wl = w0_ref[...] + _bdot(jnp.tanh(_bdot(xw, w1_ref[...])), w2_ref[...])
    lw_s[0:total] = -EXP_MINUS_HALF * jax.nn.sigmoid(wl)
    a = jax.nn.sigmoid(a0_ref[...] + _bdot(_bdot(xa, a1_ref[...]), a2_ref[...]))
    g_s[...] = _bdot(jax.nn.sigmoid(_bdot(xg, g1_ref[...])), g2_ref[...])
    kkv = k * kk_ref[...]
    kk = kkv / jnp.maximum(jnp.sqrt(_head_sum(kkv * kkv, e_ref, et_ref)), 1e-12)
    r_s[0:total] = r
    k_s[0:total] = k * (1.0 + (a - 1.0) * ka_ref[...])
    v_s[0:total] = v
    kk_s[0:total] = kk
    b_s[0:total] = kk * a

    ri = lax.broadcasted_iota(jnp.int32, (C, C), 0)
    ci = lax.broadcasted_iota(jnp.int32, (C, C), 1)
    tri = (ri >= ci).astype(BF16)
    r2 = lax.broadcasted_iota(jnp.int32, (C2, C2), 0)
    c2 = lax.broadcasted_iota(jnp.int32, (C2, C2), 1)
    strict = r2 > c2
    incl = r2 >= c2
    lane = lax.broadcasted_iota(jnp.int32, (C, LANES), 1)
    head0 = lane < HEAD_DIM

    def hat(x):
        x3 = jnp.stack([x[:, p * LANES:(p + 1) * LANES] for p in range(n_pairs)])
        return jnp.concatenate([jnp.where(head0, x3, 0.0), jnp.where(head0, 0.0, x3)], axis=1)

    def mm(a, b):
        return jnp.einsum("pmk,pkn->pmn", a.astype(BF16), b.astype(BF16), preferred_element_type=F32)

    def mm_nt(a, b):
        return jnp.einsum("pmk,pnk->pmn", a.astype(BF16), b.astype(BF16), preferred_element_type=F32)

    chunk_row = lax.broadcasted_iota(jnp.int32, (C, D_MODEL), 0)

    def chunk(ref, s):
        if short:
            starts = [pl.multiple_of((s * npar + q) * seq_rows, 8) for q in range(npar)]
            return jnp.concatenate([jnp.where(chunk_row < seq_rows, ref[pl.ds(st0, C), :], 0.0) for st0 in starts],
                                   axis=1)
        return jnp.concatenate([ref[pl.ds(q * rows + s * C, C), :] for q in range(npar)], axis=1)

    def sub_chunk(s, carry):
        lw = chunk(lw_s, s)
        p1 = lw.astype(BF16)
        rem = lw - p1.astype(F32)
        p2 = rem.astype(BF16)
        p3 = (rem - p2.astype(F32)).astype(BF16)
        cw = (jnp.dot(tri, p1, preferred_element_type=F32) + jnp.dot(tri, p2, preferred_element_type=F32)
              + jnp.dot(tri, p3, preferred_element_type=F32))
        cw_end = cw[C - 1:C, :]
        e_neg = jnp.exp(-cw)
        e_end = jnp.exp(cw_end - cw)
        kk_c = chunk(kk_s, s)
        bb = chunk(b_s, s)
        k_c = chunk(k_s, s)
        a_all = kk_c * jnp.exp(cw - lw)
        r_all = chunk(r_s, s) * jnp.exp(cw)
        b_all = bb * e_neg
        k_all = k_c * e_neg
        bd_all = bb * e_end
        kd_all = k_c * e_end
        v_all = chunk(v_s, s)
        decay = jnp.exp(cw_end)
        ar_h = jnp.concatenate([hat(a_all), hat(r_all)], axis=1)
        bk_h = jnp.concatenate([hat(b_all), hat(k_all)], axis=1)
        v_h = hat(v_all)
        st = state[...]
        g = mm_nt(ar_h, bk_h)
        low = jnp.where(strict, g[:, :C2, :C2], 0.0)
        ak = jnp.where(strict, g[:, :C2, C2:], 0.0)
        rbk = jnp.concatenate([jnp.where(incl, g[:, C2:, :C2], 0.0), jnp.where(incl, g[:, C2:, C2:], 0.0)], axis=2)
        ss = mm_nt(ar_h, st)
        x = -(ss[:, :C2] + mm(ak, v_h))
        t = mm(low, jnp.concatenate([low, x], axis=2))
        lp = t[:, :, :C2]
        x = x - t[:, :, C2:]
        for _ in range(max(int(np.ceil(np.log2(min(C, seq_rows)))), 2) - 2):
            t = mm(lp, jnp.concatenate([lp, x], axis=2))
            lp = t[:, :, :C2]
            x = x + t[:, :, C2:]
        x = x + mm(lp, x)
        xv = jnp.concatenate([x, v_h], axis=1)
        y_h = ss[:, C2:] + mm(rbk, xv)
        y = y_h[:, :C] + y_h[:, C:]
        for p in range(n_pairs):
            q, pl_ = divmod(p, hp)
            if short:
                first = pl.multiple_of((s * npar + q) * seq_rows, 8)
                y_s[pl.ds(first, seq_rows), pl_ * LANES:(pl_ + 1) * LANES] = y[p, :seq_rows]
            else:
                y_s[pl.ds(q * rows + s * C, C), pl_ * LANES:(pl_ + 1) * LANES] = y[p]
        bkd_h = jnp.concatenate([hat(bd_all), hat(kd_all)], axis=1)
        xv_t = jnp.stack([xv[p].T for p in range(n_pairs)])
        dec3 = jnp.stack([decay[:, p * LANES:(p + 1) * LANES] for p in range(n_pairs)])
        state[...] = st * dec3 + mm(xv_t, bkd_h)
        return carry

    if nseq == 1:
        for s in range(n_sub):
            sub_chunk(s, 0)
    else:
        def one_group(s, carry):
            for q in range(npar):
                load_state(s0_ref.at[s * npar + q], q * hp)
            sub_chunk(s, carry)
            for q in range(npar):
                store_state(st_ref.at[s * npar + q], q * hp)
            return carry

        lax.fori_loop(0, nseq // npar, one_group, 0)

    y = y_s[0:total]
    inv_n = 1.0 / HEAD_DIM
    mu = _head_sum(y, e_ref, et_ref) * inv_n
    yc = y - mu
    var = _head_sum(yc * yc, e_ref, et_ref) * inv_n
    yn = yc * lax.rsqrt(var + LNX_EPS) * lnw_ref[...] + lnb_ref[...]
    bonus = _head_sum(r_s[0:total] * k_s[0:total] * rk_ref[...], e_ref, et_ref) * v_s[0:total]
    out = h_in + _bdot((yn + bonus) * g_s[...], wo_ref[...])
    o_ref[...] = out.reshape(o_ref.shape)

    if nseq == 1:
        @pl.when(c == pl.num_programs(1) - 1)
        def _():
            for q in range(npar):
                store_state(st_ref.at[q], q * hp)


def _rwkv_layer(h, shift, s0, seq_len, lw):
    n = h.shape[0]
    nb = n // seq_len
    if seq_len % WKV_CHUNK == 0:
        npar = RWKV_PARALLEL_SEQS if nb % RWKV_PARALLEL_SEQS == 0 else 1
        nseq, rows = 1, _tile(seq_len, RWKV_STEP_ROWS // npar)
        steps, n_sub, total = seq_len // rows, rows // WKV_CHUNK, npar * rows
        buf_rows = total
        assert rows % WKV_CHUNK == 0
        h_in = h.reshape(nb, seq_len, D_MODEL)
        row_spec = pl.BlockSpec((npar, rows, D_MODEL), lambda bi, ci: (bi, ci, 0))
        sh_in = shift.reshape(nb, 1, D_MODEL)
        sh_spec = pl.BlockSpec((npar, 1, D_MODEL), lambda bi, ci: (bi, 0, 0))
        hn_shape, hn_spec = jax.ShapeDtypeStruct((nb, 1, D_MODEL), F32), sh_spec
        nst = npar
    else:
        assert seq_len < WKV_CHUNK and seq_len % 8 == 0
        nseq = _tile(nb, RWKV_SHORT_SEQS)
        npar = RWKV_PARALLEL_SEQS if nseq % RWKV_PARALLEL_SEQS == 0 else 1
        rows, steps, n_sub = nseq * seq_len, 1, 1
        total = rows
        buf_rows = rows + WKV_CHUNK - seq_len
        h_in = h
        row_spec = pl.BlockSpec((rows, D_MODEL), lambda bi, ci: (bi, 0))
        sh_in = jnp.repeat(shift, seq_len, axis=0)
        sh_spec = row_spec
        hn_shape, hn_spec = jax.ShapeDtypeStruct((n, D_MODEL), F32), sh_spec
        nst = nseq
    st_spec = pl.BlockSpec((nst, N_HEADS, HEAD_DIM, HEAD_DIM), lambda bi, ci: (bi, 0, 0, 0))
    consts = [lw["nw"], lw["mix"], lw["wrkv"], lw["w0"], lw["w1"], lw["w2"], lw["a0"], lw["a1"], lw["a2"],
              lw["g1"], lw["g2"], lw["kk"], lw["ka"], lw["lnw"], lw["lnb"], lw["rk"], lw["wo"], lw["e"], lw["et"]]
    seq_buf = pltpu.VMEM((buf_rows, D_MODEL), F32)
    out, hn, st = pl.pallas_call(
        functools.partial(_rwkv_layer_kernel, rows=rows, n_sub=n_sub, nseq=nseq, npar=npar),
        grid=(nb // nst, steps),
        in_specs=[row_spec, sh_spec, st_spec] + [_const_spec(c.shape) for c in consts],
        out_specs=[row_spec, hn_spec, st_spec],
        out_shape=[jax.ShapeDtypeStruct(h_in.shape, F32), hn_shape, jax.ShapeDtypeStruct(s0.shape, F32)],
        scratch_shapes=[pltpu.VMEM((npar * N_HEADS // 2, LANES, LANES), F32), pltpu.VMEM((8, D_MODEL), F32)]
                       + [seq_buf] * 6 + [pltpu.VMEM((total, D_MODEL), F32), seq_buf],
        compiler_params=_params("arbitrary", "arbitrary", n_in=3 + len(consts)),
        name="rwkv_layer",
    )(h_in, sh_in, s0, *consts)
    out = out.reshape(n, D_MODEL)
    shift_out = hn.reshape(nb, D_MODEL) if nseq == 1 else hn.reshape(nb, seq_len, D_MODEL)[:, -1]
    return out, shift_out, st


def _t5_buckets(dist):
    d = np.asarray(dist, dtype=np.int64)
    max_exact = REL_BUCKETS // 2
    large = max_exact + (np.log(np.maximum(d, 1) / max_exact) / np.log(REL_MAX_DIST / max_exact)
                         * (REL_BUCKETS - max_exact)).astype(np.int32)
    large = np.minimum(large, REL_BUCKETS - 1)
    return np.where(d < max_exact, d, large).astype(np.int32)


def _bucket_rows(table, buckets):
    onehot = jnp.asarray(np.eye(REL_BUCKETS, dtype=np.float32)[buckets])
    return jnp.dot(onehot, table.astype(F32), precision=lax.Precision.HIGHEST)


def _toeplitz(tab, n_rows, n_cols):
    period = tab.shape[-1]
    assert period >= n_rows + n_cols - 1 and n_cols <= period - 1
    lead = tab.shape[:-1]
    flat = jnp.broadcast_to(tab[..., None, :], lead + (n_rows, period)).reshape(lead + (-1,))
    skew = flat[..., :n_rows * (period - 1)].reshape(lead + (n_rows, period - 1))
    return skew[..., :n_cols]


def _band_bias(rel_bias, group):
    win, dil = DILATION_GROUPS[group]
    assert win // dil == Q_BLOCK
    period = 3 * Q_BLOCK - 1
    idx = np.arange(period)
    m = Q_BLOCK - np.where(idx < 2 * Q_BLOCK, idx, idx - period)
    valid = (m >= 0) & (m <= Q_BLOCK)
    buckets = _t5_buckets(dil * np.clip(m, 0, Q_BLOCK))
    tbl = _bucket_rows(rel_bias[:, group * N_HEADS:(group + 1) * N_HEADS], buckets).T
    general = _toeplitz(jnp.where(valid[None], tbl, NEG_INF), Q_BLOCK, 2 * Q_BLOCK)
    first = jnp.where((np.arange(2 * Q_BLOCK) >= Q_BLOCK)[None, None, :], general, NEG_INF)
    return jnp.stack([first, general])


_HEAD_OF_SEG = np.array([4 * (2 * (s // 8) + s % 2) + (s // 2) % 4 for s in range(N_HEADS)])


def _proj_rm_kernel(x_ref, g_ref, w_ref, *refs, natural, lane_ranges, tm):
    scr = refs[-1]
    outs = refs[:-1]
    y = _bdot(_rms(x_ref[...], g_ref[...]), w_ref[...])
    if natural:
        outs[0][...] = y
        outs = outs[1:]
    for c in range(scr.shape[0]):
        scr[c] = y[:, c * LANES:(c + 1) * LANES]
    for gi, (_, dil) in enumerate(DILATION_GROUPS):
        lo, hi = lane_ranges[gi]
        if dil == 1:
            outs[gi][0] = y[:, lo:hi].astype(BF16)
            continue
        for rho in range(dil):
            rows = [scr[c, pl.ds(rho, tm // dil, stride=dil), :] for c in range(lo // LANES, hi // LANES)]
            outs[gi][rho] = jnp.concatenate(rows, axis=1).astype(BF16)


def _proj_rm(h, g, w, nb, seq_len, natural, lane_ranges, name):
    n = h.shape[0]
    tm = _tile(n, PROJ_TILE)
    nout = w.shape[1]
    assert seq_len % tm == 0 and all(tm % (16 * dil) == 0 for _, dil in DILATION_GROUPS)
    tps = seq_len // tm
    out_shape, out_specs = [], []
    if natural:
        out_shape.append(jax.ShapeDtypeStruct((n, nout), F32))
        out_specs.append(pl.BlockSpec((tm, nout), lambda i: (i, 0)))
    for (_, dil), (lo, hi) in zip(DILATION_GROUPS, lane_ranges):
        out_shape.append(jax.ShapeDtypeStruct((nb, dil, seq_len // dil, hi - lo), BF16))
        out_specs.append(pl.BlockSpec((None, dil, tm // dil, hi - lo), lambda i: (i // tps, 0, i % tps, 0)))
    return pl.pallas_call(
        functools.partial(_proj_rm_kernel, natural=natural, lane_ranges=lane_ranges, tm=tm),
        grid=(n // tm,),
        in_specs=[pl.BlockSpec((tm, D_MODEL), lambda i: (i, 0)), _const_spec((1, D_MODEL)), _const_spec(w.shape)],
        out_specs=out_specs,
        out_shape=out_shape,
        scratch_shapes=[pltpu.VMEM((nout // LANES, tm, LANES), F32)],
        compiler_params=_params("arbitrary", n_in=3),
        name=name,
    )(h, g, w)


def _attn_kernel(q_ref, kv_ref, halo_ref, bias_ref, o_ref, lse_ref, kvbuf, *, nq):
    i = pl.program_id(0)
    dil = q_ref.shape[0]
    kvd = KV_HEADS * HEAD_DIM
    nt = (((1,), (1,)), ((), ()))
    kvbuf[:, :Q_BLOCK, :] = halo_ref[...]
    kvbuf[:, Q_BLOCK:, :] = kv_ref[...]
    lane = lax.broadcasted_iota(jnp.int32, (Q_BLOCK, LANES), 1)
    low_half = lane < HEAD_DIM
    lane_row = lax.broadcasted_iota(jnp.int32, (1, LANES), 1)
    keep_lo = (lane_row < HEAD_DIM).astype(BF16)
    keep_hi = (lane_row >= HEAD_DIM).astype(BF16)

    def block(u, carry):
        rho = lax.div(u, nq)
        j = u - rho * nq
        r0 = pl.multiple_of(j * Q_BLOCK, Q_BLOCK)
        qb = q_ref[rho, pl.ds(r0, Q_BLOCK), :]
        kvb = kvbuf[rho, pl.ds(r0, 2 * Q_BLOCK), :]
        bsel = jnp.where(jnp.logical_and(i == 0, j == 0), 0, 1)
        lse_tile = jnp.zeros((Q_BLOCK, LANES), F32)
        for G in range(KV_HEADS // 2):
            kg = kvb[:, G * LANES:(G + 1) * LANES]
            vg = kvb[:, kvd + G * LANES:kvd + (G + 1) * LANES]
            vcat = jnp.concatenate([vg * keep_lo, vg * keep_hi], axis=0)
            pieces = []
            for r in range(Q_PER_KV):
                qg = qb[:, (G * Q_PER_KV + r) * LANES:(G * Q_PER_KV + r + 1) * LANES]
                pieces += [qg * keep_lo, qg * keep_hi]
            s_all = lax.dot_general(jnp.concatenate(pieces, axis=0), kg, nt, preferred_element_type=F32)
            for r in range(Q_PER_KV):
                grp = G * Q_PER_KV + r
                parts = []
                for half in range(2):
                    s = s_all[(2 * r + half) * Q_BLOCK:(2 * r + half + 1) * Q_BLOCK]
                    s = s + bias_ref[bsel, int(_HEAD_OF_SEG[2 * grp + half])]
                    m = jnp.max(s, axis=-1, keepdims=True)
                    p = jnp.exp(s - m)
                    l = jnp.sum(p, axis=-1, keepdims=True)
                    parts.append((p.astype(BF16), l, m + jnp.log(l)))
                (p0, l0, e0), (p1, l1, e1) = parts
                o = jnp.dot(jnp.concatenate([p0, p1], axis=1), vcat, preferred_element_type=F32)
                o = (o / jnp.where(low_half, l0, l1)).astype(o_ref.dtype)
                o_ref[rho, pl.ds(r0, Q_BLOCK), grp * LANES:(grp + 1) * LANES] = o
                lse_tile = jnp.where(lane == 2 * grp, e0, lse_tile)
                lse_tile = jnp.where(lane == 2 * grp + 1, e1, lse_tile)
        lse_ref[rho, pl.ds(r0, Q_BLOCK), :] = lse_tile
        return carry

    lax.fori_loop(0, dil * nq, block, 0, unroll=4)


def _attn_group(q_rm, kv_rm, bias):
    nb, dil, tsub, _ = q_rm.shape
    kvw = kv_rm.shape[-1]
    rows = ATTN_SPAN // dil
    nq = rows // Q_BLOCK
    assert tsub % rows == 0 and nq >= 1
    span_spec = lambda width: pl.BlockSpec((None, dil, rows, width), lambda i, b: (b, 0, i, 0))
    return pl.pallas_call(
        functools.partial(_attn_kernel, nq=nq),
        grid=(tsub // rows, nb),
        in_specs=[span_spec(D_MODEL), span_spec(kvw),
                  pl.BlockSpec((None, dil, Q_BLOCK, kvw), lambda i, b: (b, 0, jnp.maximum(i * nq - 1, 0), 0)),
                  _const_spec(bias.shape)],
        out_specs=[span_spec(D_MODEL), span_spec(LANES)],
        out_shape=[jax.ShapeDtypeStruct((nb, dil, tsub, D_MODEL), BF16),
                   jax.ShapeDtypeStruct((nb, dil, tsub, LANES), F32)],
        scratch_shapes=[pltpu.VMEM((dil, Q_BLOCK + rows, kvw), BF16)],
        compiler_params=_params("arbitrary", "arbitrary", n_in=4),
        name=f"attn_d{dil}",
    )(q_rm, kv_rm, kv_rm, bias)


def _attn_out_kernel(o0_ref, o1_ref, o2_ref, l0_ref, l1_ref, l2_ref, h_ref, wo_ref, et_ref, out_ref, *scr, tm):
    outs, lses = [], []
    for gi, (o_ref, l_ref) in enumerate(((o0_ref, l0_ref), (o1_ref, l1_ref), (o2_ref, l2_ref))):
        dil = DILATION_GROUPS[gi][1]
        if dil == 1:
            outs.append(o_ref[0].astype(F32))
            lses.append(l_ref[0])
            continue
        so, sl = scr[2 * gi], scr[2 * gi + 1]
        n_tiles = so.shape[0]
        for rho in range(dil):
            rows = pl.ds(rho, tm // dil, stride=dil)
            for c in range(n_tiles):
                so[c, rows, :] = o_ref[rho, :, c * LANES:(c + 1) * LANES].astype(F32)
            sl[rows, :] = l_ref[rho]
        outs.append(jnp.concatenate([so[c] for c in range(n_tiles)], axis=1))
        lses.append(sl[...])
    l0, l1, l2 = lses
    m = jnp.maximum(jnp.maximum(l0, l1), l2)
    w0, w1, w2 = jnp.exp(l0 - m), jnp.exp(l1 - m), jnp.exp(l2 - m)
    inv = 1.0 / (w0 + w1 + w2)
    et = et_ref[...]
    att = _bdot(w0 * inv, et) * outs[0] + _bdot(w1 * inv, et) * outs[1] + _bdot(w2 * inv, et) * outs[2]
    out_ref[...] = h_ref[...] + _bdot(att, wo_ref[...])


def _attn_out(outs, lses, h, wo, et, seq_len):
    n = h.shape[0]
    tm = _tile(n, PROJ_TILE)
    tps = seq_len // tm
    row_spec = pl.BlockSpec((tm, D_MODEL), lambda i: (i, 0))

    def rm_spec(dil, width):
        return pl.BlockSpec((None, dil, tm // dil, width), lambda i: (i // tps, 0, i % tps, 0))

    dils = [dil for _, dil in DILATION_GROUPS]
    scratch = []
    for _ in dils:
        scratch += [pltpu.VMEM((D_MODEL // LANES, tm, LANES), F32), pltpu.VMEM((tm, LANES), F32)]
    return pl.pallas_call(
        functools.partial(_attn_out_kernel, tm=tm),
        grid=(n // tm,),
        in_specs=[rm_spec(d, D_MODEL) for d in dils] + [rm_spec(d, LANES) for d in dils]
                 + [row_spec, _const_spec(wo.shape), _const_spec(et.shape)],
        out_specs=row_spec,
        out_shape=jax.ShapeDtypeStruct((n, D_MODEL), F32),
        scratch_shapes=scratch,
        compiler_params=_params("arbitrary", n_in=9),
        name="attn_out",
    )(*outs, *lses, h, wo, et)


def _decode_bias(rel_bias, seq_len, cache_len):
    ncol = cache_len + LANES
    period = seq_len + ncol - 1
    idx = np.arange(period)
    dist = cache_len - np.where(idx < ncol, idx, idx - period)
    buckets = _t5_buckets(np.clip(dist, 0, MAX_WINDOW))
    tabs = []
    for g, (win, dil) in enumerate(DILATION_GROUPS):
        valid = (dist >= 0) & (dist % dil == 0) & (dist <= win)
        tbl = _bucket_rows(rel_bias[:, g * N_HEADS:(g + 1) * N_HEADS], buckets).T
        tabs.append(jnp.where(valid[None], tbl, NEG_INF))
    rows = _toeplitz(jnp.stack(tabs), seq_len, ncol)
    bias = rows.reshape(N_GROUPS * N_HEADS * seq_len, ncol)
    return bias[:, :cache_len], bias[:, cache_len:]


def _attn_decode_kernel(q_ref, cache_ref, kvn_ref, h_ref, bc_ref, bn_ref, wo_ref, out_ref, *, seq_len):
    kvd = KV_HEADS * HEAD_DIM
    nslot = N_GROUPS * N_HEADS
    rows_g = N_HEADS * seq_len
    nt = (((1,), (1,)), ((), ()))
    cache = cache_ref[...]
    kc = cache[:, :kvd].astype(BF16)
    vc = cache[:, kvd:].astype(BF16)
    kvn = kvn_ref[...]
    pad = jnp.zeros((LANES - seq_len, kvd), F32)
    kn = jnp.concatenate([kvn[:, :kvd], pad], axis=0).astype(BF16)
    vn = jnp.concatenate([kvn[:, kvd:], pad], axis=0).astype(BF16)
    lhs = jnp.concatenate([q_ref[:, s * kvd:(s + 1) * kvd] for s in range(nslot)], axis=0).astype(BF16)
    cache_len = kc.shape[0]
    sn = lax.dot_general(lhs, kn, nt, preferred_element_type=F32) + bn_ref[...]
    scs, m = [], None
    for g, (win, _) in enumerate(DILATION_GROUPS):
        c0 = (cache_len - min(win, cache_len)) // LANES * LANES
        rows = slice(g * rows_g, (g + 1) * rows_g)
        sc = lax.dot_general(lhs[rows], kc[c0:], nt, preferred_element_type=F32) + bc_ref[rows, c0:]
        scs.append((sc, c0, rows))
        m_g = jnp.maximum(jnp.max(sc, axis=-1, keepdims=True), jnp.max(sn[rows], axis=-1, keepdims=True))
        m = m_g if m is None else jnp.maximum(m, m_g)
    l = jnp.zeros((rows_g, 1), F32)
    num = jnp.zeros((rows_g, kvd), F32)
    for sc, c0, rows in scs:
        pc = jnp.exp(sc - m)
        pn = jnp.exp(sn[rows] - m)
        l = l + jnp.sum(pc, axis=-1, keepdims=True) + jnp.sum(pn, axis=-1, keepdims=True)
        num = (num + jnp.dot(pc.astype(BF16), vc[c0:], preferred_element_type=F32)
               + jnp.dot(pn.astype(BF16), vn, preferred_element_type=F32))
    row = lax.broadcasted_iota(jnp.int32, (rows_g, kvd), 0)
    lane = lax.broadcasted_iota(jnp.int32, (rows_g, kvd), 1)
    own = (row // (Q_PER_KV * seq_len)) == (lane // HEAD_DIM)
    att = jnp.where(own, num / l, 0.0)
    out = h_ref[...]
    for r in range(Q_PER_KV):
        a_r = att[r * seq_len:(r + 1) * seq_len]
        for c in range(1, KV_HEADS):
            a_r = a_r + att[(c * Q_PER_KV + r) * seq_len:(c * Q_PER_KV + r + 1) * seq_len]
        out = out + _bdot(a_r, wo_ref[r])
    out_ref[...] = out


def _attn_decode(q, cache, kv_new, h, bias_c, bias_n, wo_r, nb, seq_len):
    cache_len = cache.shape[1]
    qw = q.shape[1]
    kvw = 2 * KV_HEADS * HEAD_DIM
    return pl.pallas_call(
        functools.partial(_attn_decode_kernel, seq_len=seq_len),
        grid=(nb,),
        in_specs=[pl.BlockSpec((seq_len, qw), lambda b: (b, 0)),
                  pl.BlockSpec((None, cache_len, kvw), lambda b: (b, 0, 0)),
                  pl.BlockSpec((seq_len, kvw), lambda b: (b, 0)),
                  pl.BlockSpec((seq_len, D_MODEL), lambda b: (b, 0)),
                  _const_spec(bias_c.shape), _const_spec(bias_n.shape), _const_spec(wo_r.shape)],
        out_specs=pl.BlockSpec((seq_len, D_MODEL), lambda b: (b, 0)),
        out_shape=jax.ShapeDtypeStruct((nb * seq_len, D_MODEL), F32),
        compiler_params=_params("arbitrary", n_in=7),
        name="attn_decode",
    )(q, cache, kv_new, h, bias_c, bias_n, wo_r)


def _prep_weights(norm_w, ffn1_wi, ffn1_wo, ffn2_wi, ffn2_wo, pe_proj, pe_gate,
                  rwkv_mix, rwkv_wrkv, rwkv_wo, rwkv_w0, rwkv_w1, rwkv_w2, rwkv_a0, rwkv_a1, rwkv_a2,
                  rwkv_g1, rwkv_g2, rwkv_kk, rwkv_ka, rwkv_rk, rwkv_lnx_w, rwkv_lnx_b,
                  attn_wq, attn_wo, kv_norm, w_kv, rel_bias, final_norm):
    def row(v):
        return v.reshape(1, -1).astype(F32)

    def pad_cols(w, n):
        return jnp.pad(w, ((0, 0), (0, n - w.shape[1]))).astype(BF16)

    def pad_rows(w, n):
        return jnp.pad(w, ((0, n - w.shape[0]), (0, 0))).astype(BF16)

    head_of_lane = np.arange(D_MODEL) // HEAD_DIM
    e = jnp.asarray(head_of_lane[:, None] == np.arange(LANES)[None, :], BF16)
    et = jnp.asarray(np.arange(LANES)[:, None] == head_of_lane[None, :], BF16)

    depth = norm_w.shape[0]
    layers = []
    for i in range(depth):
        layers.append(dict(
            nw=[row(norm_w[i, j]) for j in range(4)],
            ffn1=(ffn1_wi.astype(F32), ffn1_wo.astype(F32)), ffn2=(ffn2_wi.astype(F32), ffn2_wo.astype(F32)),
            pe_gate=pe_gate[i].astype(BF16), pe_proj=pe_proj[i].astype(BF16)))
    n_a = depth // 2
    rw = []
    for i in range(n_a):
        rw.append(dict(
            nw=row(norm_w[i, 1]), mix=rwkv_mix[i].astype(F32), wrkv=rwkv_wrkv[i].astype(BF16),
            w0=row(rwkv_w0[i]), w1=pad_cols(rwkv_w1[i], LANES), w2=pad_rows(rwkv_w2[i], LANES),
            a0=row(rwkv_a0[i]), a1=pad_cols(rwkv_a1[i], LANES), a2=pad_rows(rwkv_a2[i], LANES),
            g1=pad_cols(rwkv_g1[i], 2 * LANES), g2=pad_rows(rwkv_g2[i], 2 * LANES),
            kk=row(rwkv_kk[i]), ka=row(rwkv_ka[i]), rk=row(rwkv_rk[i]),
            lnw=row(rwkv_lnx_w[i]), lnb=row(rwkv_lnx_b[i]), wo=rwkv_wo[i].astype(BF16), e=e, et=et))
    scale = HEAD_DIM ** -0.5
    at = []
    for j in range(depth - n_a):
        wq = attn_wq[j] * scale
        wo_r = attn_wo[j].reshape(KV_HEADS, Q_PER_KV, HEAD_DIM, D_MODEL).transpose(1, 0, 2, 3)
        wo_r = wo_r.reshape(Q_PER_KV, KV_HEADS * HEAD_DIM, D_MODEL).astype(BF16)
        wq_seg = wq.reshape(D_MODEL, N_GROUPS, 2, 2, Q_PER_KV, HEAD_DIM).transpose(0, 1, 2, 4, 3, 5)
        wq_seg = wq_seg.reshape(D_MODEL, -1)
        wo_seg = attn_wo[j].reshape(2, 2, Q_PER_KV, HEAD_DIM, D_MODEL).transpose(0, 2, 1, 3, 4)
        wo_seg = wo_seg.reshape(D_MODEL, D_MODEL)
        at.append(dict(wq_seg=wq_seg.astype(BF16), wo_seg=wo_seg.astype(BF16), wo_r=wo_r))
    return dict(layers=layers, rwkv=rw, attn=at, kv_norm=row(kv_norm), w_kv=w_kv.astype(BF16),
                final_norm=row(final_norm), rel_bias=rel_bias, et=et)


def _trunks(streams, w):
    depth = len(w["layers"])
    n_a = depth // 2
    kvw = 2 * KV_HEADS * HEAD_DIM
    st = []
    for sd in streams:
        nb, seq_len, _ = sd["x"].shape
        n = nb * seq_len
        st.append(dict(nb=nb, seq_len=seq_len, n=n, h=sd["x"].reshape(n, D_MODEL).astype(F32),
                       p=sd["p"].reshape(depth, n, PLE_DIM).astype(F32), wkv_out=[], shift_out=[],
                       kv_rows=None, kv_rm=None, **{k: sd[k] for k in ("wkv0", "shift0", "cache")}))

    def ffn(nw, weights, **kw):
        for t, h in zip(st, _ffn([t["h"] for t in st], nw, *weights, **kw)):
            t["h"] = h

    for i in range(depth):
        lw = w["layers"][i]
        if i == n_a:
            for t in st:
                if t["cache"] is None:
                    t["kv_rows"], *t["kv_rm"] = _proj_rm(t["h"], w["kv_norm"], w["w_kv"], t["nb"], t["seq_len"], True,
                                                         [(0, kvw)] * N_GROUPS, "kv_proj")
                else:
                    t["kv_rows"] = _norm_mm(t["h"], w["kv_norm"], w["w_kv"], F32, name="kv_proj")
        ffn(lw["nw"][0], lw["ffn1"], layer=i)
        for t in st:
            nb, seq_len, n, h = t["nb"], t["seq_len"], t["n"], t["h"]
            if i < n_a:
                h, sh, state = _rwkv_layer(h, t["shift0"][i].astype(F32), t["wkv0"][i].astype(F32), seq_len,
                                           w["rwkv"][i])
                t["wkv_out"].append(state)
                t["shift_out"].append(sh)
            elif t["cache"] is None:
                al = w["attn"][i - n_a]
                q_rm = _proj_rm(h, lw["nw"][1], al["wq_seg"], nb, seq_len, False,
                                [(gi * D_MODEL, (gi + 1) * D_MODEL) for gi in range(N_GROUPS)], "q_proj")
                outs, lses = [], []
                for gi in range(N_GROUPS):
                    o, lse = _attn_group(q_rm[gi], t["kv_rm"][gi], _band_bias(w["rel_bias"], gi))
                    outs.append(o)
                    lses.append(lse)
                h = _attn_out(outs, lses, h, al["wo_seg"], w["et"], seq_len)
            else:
                al = w["attn"][i - n_a]
                q = _norm_mm(h, lw["nw"][1], al["wq_seg"], F32, name="q_proj_decode")
                q = q.reshape(n, N_GROUPS, 2, Q_PER_KV, 2, HEAD_DIM).transpose(0, 1, 2, 4, 3, 5)
                q = q.reshape(n, N_GROUPS, KV_HEADS, Q_PER_KV, 1, HEAD_DIM)
                q = q * jnp.eye(KV_HEADS, dtype=F32).reshape(1, 1, KV_HEADS, 1, KV_HEADS, 1)
                q = q.reshape(n, N_GROUPS * N_HEADS * KV_HEADS * HEAD_DIM)
                bias_c, bias_n = _decode_bias(w["rel_bias"], seq_len, t["cache"].shape[1])
                h = _attn_decode(q, t["cache"], t["kv_rows"], h, bias_c, bias_n, al["wo_r"], nb, seq_len)
            t["h"] = h
        pe = ([t["p"] for t in st], lw["nw"][3], lw["pe_gate"], lw["pe_proj"], w["final_norm"])
        ffn(lw["nw"][2], lw["ffn2"], pe=pe, layer=i, final=(i == depth - 1))
    results = []
    for t, sd in zip(st, streams):
        dt = sd["x"].dtype
        y = t["h"].reshape(t["nb"], t["seq_len"], D_MODEL).astype(dt)
        kv_rows = t["kv_rows"].reshape(t["nb"], t["seq_len"], kvw).astype(dt)
        results.append((y, jnp.stack(t["wkv_out"]).astype(dt), jnp.stack(t["shift_out"]).astype(dt), kv_rows))
    return results


def _kv_heads(kv):
    return kv.reshape(kv.shape[:2] + (2, KV_HEADS, HEAD_DIM))


def _trunk(x, p, wkv0, shift0, cache, w):
    y, wkv, shift, kv = _trunks([dict(x=x, p=p, wkv0=wkv0, shift0=shift0, cache=cache)], w)[0]
    return y, wkv, shift, _kv_heads(kv)


def kernel(x_prompt, x_sample, state_wkv, state_shift, cache_kv, p_prompt, p_sample, norm_w, ffn1_wi, ffn1_wo, ffn2_wi, ffn2_wo, pe_proj, pe_gate, rwkv_mix, rwkv_wrkv, rwkv_wo, rwkv_w0, rwkv_w1, rwkv_w2, rwkv_a0, rwkv_a1, rwkv_a2, rwkv_g1, rwkv_g2, rwkv_kk, rwkv_ka, rwkv_rk, rwkv_lnx_w, rwkv_lnx_b, attn_wq, attn_wo, kv_norm, w_kv, rel_bias, final_norm):
    w = _prep_weights(norm_w, ffn1_wi, ffn1_wo, ffn2_wi, ffn2_wo, pe_proj, pe_gate,
                      rwkv_mix, rwkv_wrkv, rwkv_wo, rwkv_w0, rwkv_w1, rwkv_w2, rwkv_a0, rwkv_a1, rwkv_a2,
                      rwkv_g1, rwkv_g2, rwkv_kk, rwkv_ka, rwkv_rk, rwkv_lnx_w, rwkv_lnx_b,
                      attn_wq, attn_wo, kv_norm, w_kv, rel_bias, final_norm)
    n_a = norm_w.shape[0] // 2
    nb, seq_len, _ = x_prompt.shape
    wkv0 = jnp.zeros((n_a, nb, N_HEADS, HEAD_DIM, HEAD_DIM), F32)
    shift0 = jnp.zeros((n_a, nb, D_MODEL), x_prompt.dtype)
    cache = cache_kv.reshape(cache_kv.shape[0], cache_kv.shape[1], 2 * KV_HEADS * HEAD_DIM).astype(F32)
    (y_p, wkv_p, shift_p, kv_p), (y_s, wkv_s, shift_s, kv_s) = _trunks(
        [dict(x=x_prompt, p=p_prompt, wkv0=wkv0, shift0=shift0, cache=None),
         dict(x=x_sample, p=p_sample, wkv0=state_wkv, shift0=state_shift, cache=cache)], w)
    kv_prompt = _kv_heads(kv_p[:, seq_len - min(MAX_WINDOW, seq_len):])
    return (y_p, y_s, wkv_p, shift_p, kv_prompt, wkv_s, shift_s, _kv_heads(kv_s))
```

```python
import functools

import numpy as np
import jax
import jax.numpy as jnp
from jax import lax
from jax.experimental import pallas as pl
from jax.experimental.pallas import tpu as pltpu

F32 = jnp.float32
BF16 = jnp.bfloat16

D_MODEL = 1024
D_FF = 2816
PLE_DIM = 256
RMS_EPS = 1e-6
HEAD_DIM = 64
N_HEADS = D_MODEL // HEAD_DIM
LNX_EPS = 64e-5
KV_HEADS = 4
Q_PER_KV = N_HEADS // KV_HEADS
DILATION_GROUPS = ((128, 1), (512, 4), (2048, 16))
N_GROUPS = len(DILATION_GROUPS)
MAX_WINDOW = 2048
REL_BUCKETS = 32
REL_MAX_DIST = 2048
NEG_INF = -1e30

LANES = 128
FFN_CHUNK = 256
N_FFN_CHUNKS = D_FF // FFN_CHUNK
TOKEN_TILE = 256
FFN_TILE = 512
PROJ_TILE = 1024
WKV_CHUNK = 64
RWKV_SHORT_SEQS = 8
RWKV_PARALLEL_SEQS = 2
RWKV_STEP_ROWS = 512
Q_BLOCK = 128
ATTN_SPAN = 2048
VMEM_LIMIT = 56 * 1024 * 1024
EXP_MINUS_HALF = 0.6065306597126334


def _params(*sem, n_in):
    return pltpu.CompilerParams(dimension_semantics=sem, vmem_limit_bytes=VMEM_LIMIT,
                                allow_input_fusion=[True] * n_in)


def _const_spec(shape):
    return pl.BlockSpec(shape, lambda *_: (0,) * len(shape))


def _tile(n, pref=TOKEN_TILE):
    t = min(n, pref)
    while n % t:
        t -= 8
    return t


def _rms(x, g):
    return x * lax.rsqrt(jnp.mean(x * x, axis=-1, keepdims=True) + RMS_EPS) * g


def _bdot(a, b):
    return jnp.dot(a.astype(BF16), b, preferred_element_type=F32)


def _head_sum(x, e_ref, et_ref):
    return _bdot(_bdot(x, e_ref[...]), et_ref[...])


def _ffn_kernel(*refs, bounds, with_pe, final):
    ns = len(bounds)
    xs, (g_ref, wi_ref, wo_ref), rest = refs[:ns], refs[ns:ns + 3], refs[ns + 3:]
    if with_pe:
        ps, (gp_ref, wgate_ref, wproj_ref, gf_ref), outs = rest[:ns], rest[ns:ns + 4], rest[ns + 4:]
    else:
        outs = rest

    def body(k):
        x = xs[k][...]
        xn = _rms(x, g_ref[...]).astype(BF16)
        acc = jnp.zeros_like(x)
        for j in range(N_FFN_CHUNKS):
            lo, hi = j * FFN_CHUNK, (j + 1) * FFN_CHUNK
            gate = jnp.dot(xn, wi_ref[:, lo:hi].astype(BF16), preferred_element_type=F32)
            up = jnp.dot(xn, wi_ref[:, D_FF + lo:D_FF + hi].astype(BF16), preferred_element_type=F32)
            act = (gate * jax.nn.sigmoid(gate) * up).astype(BF16)
            acc = acc + jnp.dot(act, wo_ref[lo:hi, :].astype(BF16), preferred_element_type=F32)
        y = x + 0.5 * acc
        if with_pe:
            gate = jax.nn.sigmoid(_bdot(_rms(y, gp_ref[...]), wgate_ref[...]))
            y = y + gate * _bdot(ps[k][...], wproj_ref[...])
            if final:
                y = _rms(y, gf_ref[...])
        outs[k][...] = y

    if ns == 1:
        body(0)
    else:
        i = pl.program_id(0)
        for k, (lo, hi) in enumerate(bounds):
            pl.when(jnp.logical_and(i >= lo, i < hi))(functools.partial(body, k))


def _ffn(hs, g, wi, wo, pe=None, layer=0, final=False):
    tiles = [_tile(h.shape[0], FFN_TILE) for h in hs]
    counts = [h.shape[0] // t for h, t in zip(hs, tiles)]
    bounds = [(sum(counts[:k]), sum(counts[:k + 1])) for k in range(len(hs))]

    def rows(k, width, lead=()):
        lo, cnt = bounds[k][0], counts[k]
        return pl.BlockSpec(tuple(None for _ in lead) + (tiles[k], width),
                            lambda i: lead + (jnp.clip(i - lo, 0, cnt - 1), 0))

    args = list(hs) + [g, wi, wo]
    in_specs = ([rows(k, D_MODEL) for k in range(len(hs))]
                + [_const_spec((1, D_MODEL)),
                   pl.BlockSpec((None,) + wi.shape[1:], lambda i: (layer, 0, 0)),
                   pl.BlockSpec((None,) + wo.shape[1:], lambda i: (layer, 0, 0))])
    if pe is not None:
        args += list(pe[0]) + list(pe[1:])
        in_specs += ([rows(k, PLE_DIM, (layer,)) for k in range(len(hs))]
                     + [_const_spec(a.shape) for a in pe[1:]])
    return pl.pallas_call(
        functools.partial(_ffn_kernel, bounds=tuple(bounds), with_pe=pe is not None, final=final),
        grid=(bounds[-1][1],),
        in_specs=in_specs,
        out_specs=[rows(k, D_MODEL) for k in range(len(hs))],
        out_shape=[jax.ShapeDtypeStruct((h.shape[0], D_MODEL), F32) for h in hs],
        compiler_params=_params("arbitrary", n_in=len(args)),
        name="ffn_pe" if pe is not None else "ffn",
    )(*args)


def _norm_mm_kernel(x_ref, g_ref, w_ref, o_ref):
    o_ref[...] = _bdot(_rms(x_ref[...], g_ref[...]), w_ref[...]).astype(o_ref.dtype)


def _norm_mm(h, g, w, out_dtype, col_tile=None, name="norm_mm"):
    n = h.shape[0]
    tm = _tile(n)
    nout = w.shape[1]
    tn = nout if col_tile is None else col_tile
    return pl.pallas_call(
        _norm_mm_kernel,
        grid=(n // tm, nout // tn),
        in_specs=[pl.BlockSpec((tm, D_MODEL), lambda i, j: (i, 0)), _const_spec((1, D_MODEL)),
                  pl.BlockSpec((D_MODEL, tn), lambda i, j: (0, j))],
        out_specs=pl.BlockSpec((tm, tn), lambda i, j: (i, j)),
        out_shape=jax.ShapeDtypeStruct((n, nout), out_dtype),
        compiler_params=_params("arbitrary", "arbitrary", n_in=3),
        name=name,
    )(h, g, w)


def _rwkv_layer_kernel(h_ref, sh_ref, s0_ref, nw_ref, mix_ref, wrkv_ref, w0_ref, w1_ref, w2_ref, a0_ref, a1_ref,
                       a2_ref, g1_ref, g2_ref, kk_ref, ka_ref, lnw_ref, lnb_ref, rk_ref, wo_ref, e_ref, et_ref,
                       o_ref, hn_o, st_ref,
                       state, prev, r_s, lw_s, k_s, v_s, kk_s, b_s, g_s, y_s, *, rows, n_sub, nseq, npar):
    c = pl.program_id(1)
    C = WKV_CHUNK
    C2 = 2 * C
    hp = N_HEADS // 2
    n_pairs = npar * hp
    seq_rows = rows // nseq
    short = seq_rows % C != 0
    total = rows if short else npar * rows

    def load_state(s0, base=0):
        for h in range(N_HEADS):
            lo = (h % 2) * HEAD_DIM
            state[base + h // 2] = jnp.zeros((LANES, LANES), F32)
        for h in range(N_HEADS):
            lo = (h % 2) * HEAD_DIM
            state[base + h // 2, lo:lo + HEAD_DIM, lo:lo + HEAD_DIM] = s0[h]

    def store_state(st, base=0):
        for h in range(N_HEADS):
            lo = (h % 2) * HEAD_DIM
            st[h] = state[base + h // 2, lo:lo + HEAD_DIM, lo:lo + HEAD_DIM]

    if nseq == 1:
        @pl.when(c == 0)
        def _():
            for q in range(npar):
                load_state(s0_ref.at[q], q * hp)
                prev[q:q + 1, :] = sh_ref[q]
    if short:
        for ref in (r_s, lw_s, k_s, v_s, kk_s, b_s):
            ref[rows:, :] = jnp.zeros((ref.shape[0] - rows, D_MODEL), F32)

    nw = nw_ref[...]
    h_in = h_ref[...].reshape(total, D_MODEL) if nseq == 1 else h_ref[...]
    hn = _rms(h_in, nw)
    row = lax.broadcasted_iota(jnp.int32, hn.shape, 0)
    rolled = pltpu.roll(hn, 1, 0)
    if nseq == 1:
        x_prev = rolled
        for q in range(npar):
            x_prev = jnp.where(row == q * rows, prev[q:q + 1, :], x_prev)
            last = hn[(q + 1) * rows - 1:(q + 1) * rows]
            prev[q:q + 1, :] = last
            hn_o[q] = last
    else:
        x_prev = jnp.where(lax.rem(row, seq_rows) == 0, sh_ref[...], rolled)
        hn_o[...] = hn
    xx = x_prev - hn
    mix = mix_ref[...]
    xr, xw, xk, xv, xa, xg = (hn + xx * mix[j:j + 1] for j in range(6))
    r = _bdot(xr, wrkv_ref[0])
    k = _bdot(xk, wrkv_ref[1])
    v = _bdot(xv, wrkv_ref[2])
    wl = w0_ref[...] + _bdot(jnp.tanh(_bdot(xw, w1_ref[...])), w2_ref[...])
    lw_s[0:total] = -EXP_MINUS_HALF * jax.nn.sigmoid(wl)
    a = jax.nn.sigmoid(a0_ref[...] + _bdot(_bdot(xa, a1_ref[...]), a2_ref[...]))
    g_s[...] = _bdot(jax.nn.sigmoid(_bdot(xg, g1_ref[...])), g2_ref[...])
    kkv = k * kk_ref[...]
    kk = kkv / jnp.maximum(jnp.sqrt(_head_sum(kkv * kkv, e_ref, et_ref)), 1e-12)
    r_s[0:total] = r
    k_s[0:total] = k * (1.0 + (a - 1.0) * ka_ref[...])
    v_s[0:total] = v
    kk_s[0:total] = kk
    b_s[0:total] = kk * a

    ri = lax.broadcasted_iota(jnp.int32, (C, C), 0)
    ci = lax.broadcasted_iota(jnp.int32, (C, C), 1)
    tri = (ri >= ci).astype(BF16)
    r2 = lax.broadcasted_iota(jnp.int32, (C2, C2), 0)
    c2 = lax.broadcasted_iota(jnp.int32, (C2, C2), 1)
    strict = r2 > c2
    incl = r2 >= c2
    lane = lax.broadcasted_iota(jnp.int32, (C, LANES), 1)
    head0 = lane < HEAD_DIM

    def hat(x):
        x3 = jnp.stack([x[:, p * LANES:(p + 1) * LANES] for p in range(n_pairs)])
        return jnp.concatenate([jnp.where(head0, x3, 0.0), jnp.where(head0, 0.0, x3)], axis=1)

    def mm(a, b):
        return jnp.einsum("pmk,pkn->pmn", a.astype(BF16), b.astype(BF16), preferred_element_type=F32)

    def mm_nt(a, b):
        return jnp.einsum("pmk,pnk->pmn", a.astype(BF16), b.astype(BF16), preferred_element_type=F32)

    chunk_row = lax.broadcasted_iota(jnp.int32, (C, D_MODEL), 0)

    def chunk(ref, s):
        if short:
            starts = [pl.multiple_of((s * npar + q) * seq_rows, 8) for q in range(npar)]
            return jnp.concatenate([jnp.where(chunk_row < seq_rows, ref[pl.ds(st0, C), :], 0.0) for st0 in starts],
                                   axis=1)
        return jnp.concatenate([ref[pl.ds(q * rows + s * C, C), :] for q in range(npar)], axis=1)

    def sub_chunk(s, carry):
        lw = chunk(lw_s, s)
        p1 = lw.astype(BF16)
        rem = lw - p1.astype(F32)
        p2 = rem.astype(BF16)
        p3 = (rem - p2.astype(F32)).astype(BF16)
        cw = (jnp.dot(tri, p1, preferred_element_type=F32) + jnp.dot(tri, p2, preferred_element_type=F32)
              + jnp.dot(tri, p3, preferred_element_type=F32))
        cw_end = cw[C - 1:C, :]
        e_neg = jnp.exp(-cw)
        e_end = jnp.exp(cw_end - cw)
        kk_c = chunk(kk_s, s)
        bb = chunk(b_s, s)
        k_c = chunk(k_s, s)
        a_all = kk_c * jnp.exp(cw - lw)
        r_all = chunk(r_s, s) * jnp.exp(cw)
        b_all = bb * e_neg
        k_all = k_c * e_neg
        bd_all = bb * e_end
        kd_all = k_c * e_end
        v_all = chunk(v_s, s)
        decay = jnp.exp(cw_end)
        ar_h = jnp.concatenate([hat(a_all), hat(r_all)], axis=1)
        bk_h = jnp.concatenate([hat(b_all), hat(k_all)], axis=1)
        v_h = hat(v_all)
        st = state[...]
        g = mm_nt(ar_h, bk_h)
        low = jnp.where(strict, g[:, :C2, :C2], 0.0)
        ak = jnp.where(strict, g[:, :C2, C2:], 0.0)
        rbk = jnp.concatenate([jnp.where(incl, g[:, C2:, :C2], 0.0), jnp.where(incl, g[:, C2:, C2:], 0.0)], axis=2)
        ss = mm_nt(ar_h, st)
        x = -(ss[:, :C2] + mm(ak, v_h))
        t = mm(low, jnp.concatenate([low, x], axis=2))
        lp = t[:, :, :C2]
        x = x - t[:, :, C2:]
        for _ in range(max(int(np.ceil(np.log2(min(C, seq_rows)))), 2) - 2):
            t = mm(lp, jnp.concatenate([lp, x], axis=2))
            lp = t[:, :, :C2]
            x = x + t[:, :, C2:]
        x = x + mm(lp, x)
        xv = jnp.concatenate([x, v_h], axis=1)
        y_h = ss[:, C2:] + mm(rbk, xv)
        y = y_h[:, :C] + y_h[:, C:]
        for p in range(n_pairs):
            q, pl_ = divmod(p, hp)
            if short:
                first = pl.multiple_of((s * npar + q) * seq_rows, 8)
                y_s[pl.ds(first, seq_rows), pl_ * LANES:(pl_ + 1) * LANES] = y[p, :seq_rows]
            else:
                y_s[pl.ds(q * rows + s * C, C), pl_ * LANES:(pl_ + 1) * LANES] = y[p]
        bkd_h = jnp.concatenate([hat(bd_all), hat(kd_all)], axis=1)
        xv_t = jnp.stack([xv[p].T for p in range(n_pairs)])
        dec3 = jnp.stack([decay[:, p * LANES:(p + 1) * LANES] for p in range(n_pairs)])
        state[...] = st * dec3 + mm(xv_t, bkd_h)
        return carry

    if nseq == 1:
        for s in range(n_sub):
            sub_chunk(s, 0)
    else:
        def one_group(s, carry):
            for q in range(npar):
                load_state(s0_ref.at[s * npar + q], q * hp)
            sub_chunk(s, carry)
            for q in range(npar):
                store_state(st_ref.at[s * npar + q], q * hp)
            return carry

        lax.fori_loop(0, nseq // npar, one_group, 0)

    y = y_s[0:total]
    inv_n = 1.0 / HEAD_DIM
    mu = _head_sum(y, e_ref, et_ref) * inv_n
    yc = y - mu
    var = _head_sum(yc * yc, e_ref, et_ref) * inv_n
    yn = yc * lax.rsqrt(var + LNX_EPS) * lnw_ref[...] + lnb_ref[...]
    bonus = _head_sum(r_s[0:total] * k_s[0:total] * rk_ref[...], e_ref, et_ref) * v_s[0:total]
    out = h_in + _bdot((yn + bonus) * g_s[...], wo_ref[...])
    o_ref[...] = out.reshape(o_ref.shape)

    if nseq == 1:
        @pl.when(c == pl.num_programs(1) - 1)
        def _():
            for q in range(npar):
                store_state(st_ref.at[q], q * hp)


def _rwkv_layer(h, shift, s0, seq_len, lw):
    n = h.shape[0]
    nb = n // seq_len
    if seq_len % WKV_CHUNK == 0:
        npar = RWKV_PARALLEL_SEQS if nb % RWKV_PARALLEL_SEQS == 0 else 1
        nseq, rows = 1, _tile(seq_len, RWKV_STEP_ROWS // npar)
        steps, n_sub, total = seq_len // rows, rows // WKV_CHUNK, npar * rows
        buf_rows = total
        assert rows % WKV_CHUNK == 0
        h_in = h.reshape(nb, seq_len, D_MODEL)
        row_spec = pl.BlockSpec((npar, rows, D_MODEL), lambda bi, ci: (bi, ci, 0))
        sh_in = shift.reshape(nb, 1, D_MODEL)
        sh_spec = pl.BlockSpec((npar, 1, D_MODEL), lambda bi, ci: (bi, 0, 0))
        hn_shape, hn_spec = jax.ShapeDtypeStruct((nb, 1, D_MODEL), F32), sh_spec
        nst = npar
    else:
        assert seq_len < WKV_CHUNK and seq_len % 8 == 0
        nseq = _tile(nb, RWKV_SHORT_SEQS)
        npar = RWKV_PARALLEL_SEQS if nseq % RWKV_PARALLEL_SEQS == 0 else 1
        rows, steps, n_sub = nseq * seq_len, 1, 1
        total = rows
        buf_rows = rows + WKV_CHUNK - seq_len
        h_in = h
        row_spec = pl.BlockSpec((rows, D_MODEL), lambda bi, ci: (bi, 0))
        sh_in = jnp.repeat(shift, seq_len, axis=0)
        sh_spec = row_spec
        hn_shape, hn_spec = jax.ShapeDtypeStruct((n, D_MODEL), F32), sh_spec
        nst = nseq
    st_spec = pl.BlockSpec((nst, N_HEADS, HEAD_DIM, HEAD_DIM), lambda bi, ci: (bi, 0, 0, 0))
    consts = [lw["nw"], lw["mix"], lw["wrkv"], lw["w0"], lw["w1"], lw["w2"], lw["a0"], lw["a1"], lw["a2"],
              lw["g1"], lw["g2"], lw["kk"], lw["ka"], lw["lnw"], lw["lnb"], lw["rk"], lw["wo"], lw["e"], lw["et"]]
    seq_buf = pltpu.VMEM((buf_rows, D_MODEL), F32)
    out, hn, st = pl.pallas_call(
        functools.partial(_rwkv_layer_kernel, rows=rows, n_sub=n_sub, nseq=nseq, npar=npar),
        grid=(nb // nst, steps),
        in_specs=[row_spec, sh_spec, st_spec] + [_const_spec(c.shape) for c in consts],
        out_specs=[row_spec, hn_spec, st_spec],
        out_shape=[jax.ShapeDtypeStruct(h_in.shape, F32), hn_shape, jax.ShapeDtypeStruct(s0.shape, F32)],
        scratch_shapes=[pltpu.VMEM((npar * N_HEADS // 2, LANES, LANES), F32), pltpu.VMEM((8, D_MODEL), F32)]
                       + [seq_buf] * 6 + [pltpu.VMEM((total, D_MODEL), F32), seq_buf],
        compiler_params=_params("arbitrary", "arbitrary", n_in=3 + len(consts)),
        name="rwkv_layer",
    )(h_in, sh_in, s0, *consts)
    out = out.reshape(n, D_MODEL)
    shift_out = hn.reshape(nb, D_MODEL) if nseq == 1 else hn.reshape(nb, seq_len, D_MODEL)[:, -1]
    return out, shift_out, st


def _t5_buckets(dist):
    d = np.asarray(dist, dtype=np.int64)
    max_exact = REL_BUCKETS // 2
    large = max_exact + (np.log(np.maximum(d, 1) / max_exact) / np.log(REL_MAX_DIST / max_exact)
                         * (REL_BUCKETS - max_exact)).astype(np.int32)
    large = np.minimum(large, REL_BUCKETS - 1)
    return np.where(d < max_exact, d, large).astype(np.int32)


def _bucket_rows(table, buckets):
    onehot = jnp.asarray(np.eye(REL_BUCKETS, dtype=np.float32)[buckets])
    return jnp.dot(onehot, table.astype(F32), precision=lax.Precision.HIGHEST)


def _toeplitz(tab, n_rows, n_cols):
    period = tab.shape[-1]
    assert period >= n_rows + n_cols - 1 and n_cols <= period - 1
    lead = tab.shape[:-1]
    flat = jnp.broadcast_to(tab[..., None, :], lead + (n_rows, period)).reshape(lead + (-1,))
    skew = flat[..., :n_rows * (period - 1)].reshape(lead + (n_rows, period - 1))
    return skew[..., :n_cols]


def _band_bias(rel_bias, group):
    win, dil = DILATION_GROUPS[group]
    assert win // dil == Q_BLOCK
    period = 3 * Q_BLOCK - 1
    idx = np.arange(period)
    m = Q_BLOCK - np.where(idx < 2 * Q_BLOCK, idx, idx - period)
    valid = (m >= 0) & (m <= Q_BLOCK)
    buckets = _t5_buckets(dil * np.clip(m, 0, Q_BLOCK))
    tbl = _bucket_rows(rel_bias[:, group * N_HEADS:(group + 1) * N_HEADS], buckets).T
    general = _toeplitz(jnp.where(valid[None], tbl, NEG_INF), Q_BLOCK, 2 * Q_BLOCK)
    first = jnp.where((np.arange(2 * Q_BLOCK) >= Q_BLOCK)[None, None, :], general, NEG_INF)
    return jnp.stack([first, general])


_HEAD_OF_SEG = np.array([4 * (2 * (s // 8) + s % 2) + (s // 2) % 4 for s in range(N_HEADS)])


def _proj_rm_kernel(x_ref, g_ref, w_ref, *refs, natural, lane_ranges, tm):
    scr = refs[-1]
    outs = refs[:-1]
    y = _bdot(_rms(x_ref[...], g_ref[...]), w_ref[...])
    if natural:
        outs[0][...] = y
        outs = outs[1:]
    for c in range(scr.shape[0]):
        scr[c] = y[:, c * LANES:(c + 1) * LANES]
    for gi, (_, dil) in enumerate(DILATION_GROUPS):
        lo, hi = lane_ranges[gi]
        if dil == 1:
            outs[gi][0] = y[:, lo:hi].astype(BF16)
            continue
        for rho in range(dil):
            rows = [scr[c, pl.ds(rho, tm // dil, stride=dil), :] for c in range(lo // LANES, hi // LANES)]
            outs[gi][rho] = jnp.concatenate(rows, axis=1).astype(BF16)


def _proj_rm(h, g, w, nb, seq_len, natural, lane_ranges, name):
    n = h.shape[0]
    tm = _tile(n, PROJ_TILE)
    nout = w.shape[1]
    assert seq_len % tm == 0 and all(tm % (16 * dil) == 0 for _, dil in DILATION_GROUPS)
    tps = seq_len // tm
    out_shape, out_specs = [], []
    if natural:
        out_shape.append(jax.ShapeDtypeStruct((n, nout), F32))
        out_specs.append(pl.BlockSpec((tm, nout), lambda i: (i, 0)))
    for (_, dil), (lo, hi) in zip(DILATION_GROUPS, lane_ranges):
        out_shape.append(jax.ShapeDtypeStruct((nb, dil, seq_len // dil, hi - lo), BF16))
        out_specs.append(pl.BlockSpec((None, dil, tm // dil, hi - lo), lambda i: (i // tps, 0, i % tps, 0)))
    return pl.pallas_call(
        functools.partial(_proj_rm_kernel, natural=natural, lane_ranges=lane_ranges, tm=tm),
        grid=(n // tm,),
        in_specs=[pl.BlockSpec((tm, D_MODEL), lambda i: (i, 0)), _const_spec((1, D_MODEL)), _const_spec(w.shape)],
        out_specs=out_specs,
        out_shape=out_shape,
        scratch_shapes=[pltpu.VMEM((nout // LANES, tm, LANES), F32)],
        compiler_params=_params("arbitrary", n_in=3),
        name=name,
    )(h, g, w)


def _attn_kernel(q_ref, kv_ref, halo_ref, bias_ref, o_ref, lse_ref, kvbuf, *, nq):
    i = pl.program_id(0)
    dil = q_ref.shape[0]
    kvd = KV_HEADS * HEAD_DIM
    nt = (((1,), (1,)), ((), ()))
    kvbuf[:, :Q_BLOCK, :] = halo_ref[...]
    kvbuf[:, Q_BLOCK:, :] = kv_ref[...]
    lane = lax.broadcasted_iota(jnp.int32, (Q_BLOCK, LANES), 1)
    low_half = lane < HEAD_DIM
    lane_row = lax.broadcasted_iota(jnp.int32, (1, LANES), 1)
    keep_lo = (lane_row < HEAD_DIM).astype(BF16)
    keep_hi = (lane_row >= HEAD_DIM).astype(BF16)

    def block(u, carry):
        rho = lax.div(u, nq)
        j = u - rho * nq
        r0 = pl.multiple_of(j * Q_BLOCK, Q_BLOCK)
        qb = q_ref[rho, pl.ds(r0, Q_BLOCK), :]
        kvb = kvbuf[rho, pl.ds(r0, 2 * Q_BLOCK), :]
        bsel = jnp.where(jnp.logical_and(i == 0, j == 0), 0, 1)
        lse_tile = jnp.zeros((Q_BLOCK, LANES), F32)
        for G in range(KV_HEADS // 2):
            kg = kvb[:, G * LANES:(G + 1) * LANES]
            vg = kvb[:, kvd + G * LANES:kvd + (G + 1) * LANES]
            vcat = jnp.concatenate([vg * keep_lo, vg * keep_hi], axis=0)
            pieces = []
            for r in range(Q_PER_KV):
                qg = qb[:, (G * Q_PER_KV + r) * LANES:(G * Q_PER_KV + r + 1) * LANES]
                pieces += [qg * keep_lo, qg * keep_hi]
            s_all = lax.dot_general(jnp.concatenate(pieces, axis=0), kg, nt, preferred_element_type=F32)
            for r in range(Q_PER_KV):
                grp = G * Q_PER_KV + r
                parts = []
                for half in range(2):
                    s = s_all[(2 * r + half) * Q_BLOCK:(2 * r + half + 1) * Q_BLOCK]
                    s = s + bias_ref[bsel, int(_HEAD_OF_SEG[2 * grp + half])]
                    m = jnp.max(s, axis=-1, keepdims=True)
                    p = jnp.exp(s - m)
                    l = jnp.sum(p, axis=-1, keepdims=True)
                    parts.append((p.astype(BF16), l, m + jnp.log(l)))
                (p0, l0, e0), (p1, l1, e1) = parts
                o = jnp.dot(jnp.concatenate([p0, p1], axis=1), vcat, preferred_element_type=F32)
                o = (o / jnp.where(low_half, l0, l1)).astype(o_ref.dtype)
                o_ref[rho, pl.ds(r0, Q_BLOCK), grp * LANES:(grp + 1) * LANES] = o
                lse_tile = jnp.where(lane == 2 * grp, e0, lse_tile)
                lse_tile = jnp.where(lane == 2 * grp + 1, e1, lse_tile)
        lse_ref[rho, pl.ds(r0, Q_BLOCK), :] = lse_tile
        return carry

    lax.fori_loop(0, dil * nq, block, 0, unroll=8)


def _attn_group(q_rm, kv_rm, bias):
    nb, dil, tsub, _ = q_rm.shape
    kvw = kv_rm.shape[-1]
    rows = ATTN_SPAN // dil
    nq = rows // Q_BLOCK
    assert tsub % rows == 0 and nq >= 1
    span_spec = lambda width: pl.BlockSpec((None, dil, rows, width), lambda i, b: (b, 0, i, 0))
    return pl.pallas_call(
        functools.partial(_attn_kernel, nq=nq),
        grid=(tsub // rows, nb),
        in_specs=[span_spec(D_MODEL), span_spec(kvw),
                  pl.BlockSpec((None, dil, Q_BLOCK, kvw), lambda i, b: (b, 0, jnp.maximum(i * nq - 1, 0), 0)),
                  _const_spec(bias.shape)],
        out_specs=[span_spec(D_MODEL), span_spec(LANES)],
        out_shape=[jax.ShapeDtypeStruct((nb, dil, tsub, D_MODEL), BF16),
                   jax.ShapeDtypeStruct((nb, dil, tsub, LANES), F32)],
        scratch_shapes=[pltpu.VMEM((dil, Q_BLOCK + rows, kvw), BF16)],
        compiler_params=_params("arbitrary", "arbitrary", n_in=4),
        name=f"attn_d{dil}",
    )(q_rm, kv_rm, kv_rm, bias)


def _attn_out_kernel(o0_ref, o1_ref, o2_ref, l0_ref, l1_ref, l2_ref, h_ref, wo_ref, et_ref, out_ref, *scr, tm):
    outs, lses = [], []
    for gi, (o_ref, l_ref) in enumerate(((o0_ref, l0_ref), (o1_ref, l1_ref), (o2_ref, l2_ref))):
        dil = DILATION_GROUPS[gi][1]
        if dil == 1:
            outs.append(o_ref[0].astype(F32))
            lses.append(l_ref[0])
            continue
        so, sl = scr[2 * gi], scr[2 * gi + 1]
        n_tiles = so.shape[0]
        for rho in range(dil):
            rows = pl.ds(rho, tm // dil, stride=dil)
            for c in range(n_tiles):
                so[c, rows, :] = o_ref[rho, :, c * LANES:(c + 1) * LANES].astype(F32)
            sl[rows, :] = l_ref[rho]
        outs.append(jnp.concatenate([so[c] for c in range(n_tiles)], axis=1))
        lses.append(sl[...])
    l0, l1, l2 = lses
    m = jnp.maximum(jnp.maximum(l0, l1), l2)
    w0, w1, w2 = jnp.exp(l0 - m), jnp.exp(l1 - m), jnp.exp(l2 - m)
    inv = 1.0 / (w0 + w1 + w2)
    et = et_ref[...]
    att = _bdot(w0 * inv, et) * outs[0] + _bdot(w1 * inv, et) * outs[1] + _bdot(w2 * inv, et) * outs[2]
    out_ref[...] = h_ref[...] + _bdot(att, wo_ref[...])


def _attn_out(outs, lses, h, wo, et, seq_len):
    n = h.shape[0]
    tm = _tile(n, PROJ_TILE)
    tps = seq_len // tm
    row_spec = pl.BlockSpec((tm, D_MODEL), lambda i: (i, 0))

    def rm_spec(dil, width):
        return pl.BlockSpec((None, dil, tm // dil, width), lambda i: (i // tps, 0, i % tps, 0))

    dils = [dil for _, dil in DILATION_GROUPS]
    scratch = []
    for _ in dils:
        scratch += [pltpu.VMEM((D_MODEL // LANES, tm, LANES), F32), pltpu.VMEM((tm, LANES), F32)]
    return pl.pallas_call(
        functools.partial(_attn_out_kernel, tm=tm),
        grid=(n // tm,),
        in_specs=[rm_spec(d, D_MODEL) for d in dils] + [rm_spec(d, LANES) for d in dils]
                 + [row_spec, _const_spec(wo.shape), _const_spec(et.shape)],
        out_specs=row_spec,
        out_shape=jax.ShapeDtypeStruct((n, D_MODEL), F32),
        scratch_shapes=scratch,
        compiler_params=_params("arbitrary", n_in=9),
        name="attn_out",
    )(*outs, *lses, h, wo, et)


def _decode_bias(rel_bias, seq_len, cache_len):
    ncol = cache_len + LANES
    period = seq_len + ncol - 1
    idx = np.arange(period)
    dist = cache_len - np.where(idx < ncol, idx, idx - period)
    buckets = _t5_buckets(np.clip(dist, 0, MAX_WINDOW))
    tabs = []
    for g, (win, dil) in enumerate(DILATION_GROUPS):
        valid = (dist >= 0) & (dist % dil == 0) & (dist <= win)
        tbl = _bucket_rows(rel_bias[:, g * N_HEADS:(g + 1) * N_HEADS], buckets).T
        tabs.append(jnp.where(valid[None], tbl, NEG_INF))
    rows = _toeplitz(jnp.stack(tabs), seq_len, ncol)
    bias = rows.reshape(N_GROUPS * N_HEADS * seq_len, ncol)
    return bias[:, :cache_len], bias[:, cache_len:]


def _attn_decode_kernel(q_ref, cache_ref, kvn_ref, h_ref, bc_ref, bn_ref, wo_ref, out_ref, *, seq_len):
    kvd = KV_HEADS * HEAD_DIM
    nslot = N_GROUPS * N_HEADS
    rows_g = N_HEADS * seq_len
    nt = (((1,), (1,)), ((), ()))
    cache = cache_ref[...]
    kc = cache[:, :kvd].astype(BF16)
    vc = cache[:, kvd:].astype(BF16)
    kvn = kvn_ref[...]
    pad = jnp.zeros((LANES - seq_len, kvd), F32)
    kn = jnp.concatenate([kvn[:, :kvd], pad], axis=0).astype(BF16)
    vn = jnp.concatenate([kvn[:, kvd:], pad], axis=0).astype(BF16)
    lhs = jnp.concatenate([q_ref[:, s * kvd:(s + 1) * kvd] for s in range(nslot)], axis=0).astype(BF16)
    cache_len = kc.shape[0]
    sn = lax.dot_general(lhs, kn, nt, preferred_element_type=F32) + bn_ref[...]
    scs, m = [], None
    for g, (win, _) in enumerate(DILATION_GROUPS):
        c0 = (cache_len - min(win, cache_len)) // LANES * LANES
        rows = slice(g * rows_g, (g + 1) * rows_g)
        sc = lax.dot_general(lhs[rows], kc[c0:], nt, preferred_element_type=F32) + bc_ref[rows, c0:]
        scs.append((sc, c0, rows))
        m_g = jnp.maximum(jnp.max(sc, axis=-1, keepdims=True), jnp.max(sn[rows], axis=-1, keepdims=True))
        m = m_g if m is None else jnp.maximum(m, m_g)
    l = jnp.zeros((rows_g, 1), F32)
    num = jnp.zeros((rows_g, kvd), F32)
    for sc, c0, rows in scs:
        pc = jnp.exp(sc - m)
        pn = jnp.exp(sn[rows] - m)
        l = l + jnp.sum(pc, axis=-1, keepdims=True) + jnp.sum(pn, axis=-1, keepdims=True)
        num = (num + jnp.dot(pc.astype(BF16), vc[c0:], preferred_element_type=F32)
               + jnp.dot(pn.astype(BF16), vn, preferred_element_type=F32))
    row = lax.broadcasted_iota(jnp.int32, (rows_g, kvd), 0)
    lane = lax.broadcasted_iota(jnp.int32, (rows_g, kvd), 1)
    own = (row // (Q_PER_KV * seq_len)) == (lane // HEAD_DIM)
    att = jnp.where(own, num / l, 0.0)
    out = h_ref[...]
    for r in range(Q_PER_KV):
        a_r = att[r * seq_len:(r + 1) * seq_len]
        for c in range(1, KV_HEADS):
            a_r = a_r + att[(c * Q_PER_KV + r) * seq_len:(c * Q_PER_KV + r + 1) * seq_len]
        out = out + _bdot(a_r, wo_ref[r])
    out_ref[...] = out


def _attn_decode(q, cache, kv_new, h, bias_c, bias_n, wo_r, nb, seq_len):
    cache_len = cache.shape[1]
    qw = q.shape[1]
    kvw = 2 * KV_HEADS * HEAD_DIM
    return pl.pallas_call(
        functools.partial(_attn_decode_kernel, seq_len=seq_len),
        grid=(nb,),
        in_specs=[pl.BlockSpec((seq_len, qw), lambda b: (b, 0)),
                  pl.BlockSpec((None, cache_len, kvw), lambda b: (b, 0, 0)),
                  pl.BlockSpec((seq_len, kvw), lambda b: (b, 0)),
                  pl.BlockSpec((seq_len, D_MODEL), lambda b: (b, 0)),
                  _const_spec(bias_c.shape), _const_spec(bias_n.shape), _const_spec(wo_r.shape)],
        out_specs=pl.BlockSpec((seq_len, D_MODEL), lambda b: (b, 0)),
        out_shape=jax.ShapeDtypeStruct((nb * seq_len, D_MODEL), F32),
        compiler_params=_params("arbitrary", n_in=7),
        name="attn_decode",
    )(q, cache, kv_new, h, bias_c, bias_n, wo_r)


def _prep_weights(norm_w, ffn1_wi, ffn1_wo, ffn2_wi, ffn2_wo, pe_proj, pe_gate,
                  rwkv_mix, rwkv_wrkv, rwkv_wo, rwkv_w0, rwkv_w1, rwkv_w2, rwkv_a0, rwkv_a1, rwkv_a2,
                  rwkv_g1, rwkv_g2, rwkv_kk, rwkv_ka, rwkv_rk, rwkv_lnx_w, rwkv_lnx_b,
                  attn_wq, attn_wo, kv_norm, w_kv, rel_bias, final_norm):
    def row(v):
        return v.reshape(1, -1).astype(F32)

    def pad_cols(w, n):
        return jnp.pad(w, ((0, 0), (0, n - w.shape[1]))).astype(BF16)

    def pad_rows(w, n):
        return jnp.pad(w, ((0, n - w.shape[0]), (0, 0))).astype(BF16)

    head_of_lane = np.arange(D_MODEL) // HEAD_DIM
    e = jnp.asarray(head_of_lane[:, None] == np.arange(LANES)[None, :], BF16)
    et = jnp.asarray(np.arange(LANES)[:, None] == head_of_lane[None, :], BF16)

    depth = norm_w.shape[0]
    layers = []
    for i in range(depth):
        layers.append(dict(
            nw=[row(norm_w[i, j]) for j in range(4)],
            ffn1=(ffn1_wi.astype(F32), ffn1_wo.astype(F32)), ffn2=(ffn2_wi.astype(F32), ffn2_wo.astype(F32)),
            pe_gate=pe_gate[i].astype(BF16), pe_proj=pe_proj[i].astype(BF16)))
    n_a = depth // 2
    rw = []
    for i in range(n_a):
        rw.append(dict(
            nw=row(norm_w[i, 1]), mix=rwkv_mix[i].astype(F32), wrkv=rwkv_wrkv[i].astype(BF16),
            w0=row(rwkv_w0[i]), w1=pad_cols(rwkv_w1[i], LANES), w2=pad_rows(rwkv_w2[i], LANES),
            a0=row(rwkv_a0[i]), a1=pad_cols(rwkv_a1[i], LANES), a2=pad_rows(rwkv_a2[i], LANES),
            g1=pad_cols(rwkv_g1[i], 2 * LANES), g2=pad_rows(rwkv_g2[i], 2 * LANES),
            kk=row(rwkv_kk[i]), ka=row(rwkv_ka[i]), rk=row(rwkv_rk[i]),
            lnw=row(rwkv_lnx_w[i]), lnb=row(rwkv_lnx_b[i]), wo=rwkv_wo[i].astype(BF16), e=e, et=et))
    scale = HEAD_DIM ** -0.5
    at = []
    for j in range(depth - n_a):
        wq = attn_wq[j] * scale
        wo_r = attn_wo[j].reshape(KV_HEADS, Q_PER_KV, HEAD_DIM, D_MODEL).transpose(1, 0, 2, 3)
        wo_r = wo_r.reshape(Q_PER_KV, KV_HEADS * HEAD_DIM, D_MODEL).astype(BF16)
        wq_seg = wq.reshape(D_MODEL, N_GROUPS, 2, 2, Q_PER_KV, HEAD_DIM).transpose(0, 1, 2, 4, 3, 5)
        wq_seg = wq_seg.reshape(D_MODEL, -1)
        wo_seg = attn_wo[j].reshape(2, 2, Q_PER_KV, HEAD_DIM, D_MODEL).transpose(0, 2, 1, 3, 4)
        wo_seg = wo_seg.reshape(D_MODEL, D_MODEL)
        at.append(dict(wq_seg=wq_seg.astype(BF16), wo_seg=wo_seg.astype(BF16), wo_r=wo_r))
    return dict(layers=layers, rwkv=rw, attn=at, kv_norm=row(kv_norm), w_kv=w_kv.astype(BF16),
                final_norm=row(final_norm), rel_bias=rel_bias, et=et)


def _trunks(streams, w):
    depth = len(w["layers"])
    n_a = depth // 2
    kvw = 2 * KV_HEADS * HEAD_DIM
    st = []
    for sd in streams:
        nb, seq_len, _ = sd["x"].shape
        n = nb * seq_len
        st.append(dict(nb=nb, seq_len=seq_len, n=n, h=sd["x"].reshape(n, D_MODEL).astype(F32),
                       p=sd["p"].reshape(depth, n, PLE_DIM).astype(F32), wkv_out=[], shift_out=[],
                       kv_rows=None, kv_rm=None, **{k: sd[k] for k in ("wkv0", "shift0", "cache")}))

    def ffn(nw, weights, **kw):
        for t, h in zip(st, _ffn([t["h"] for t in st], nw, *weights, **kw)):
            t["h"] = h

    for i in range(depth):
        lw = w["layers"][i]
        if i == n_a:
            for t in st:
                if t["cache"] is None:
                    t["kv_rows"], *t["kv_rm"] = _proj_rm(t["h"], w["kv_norm"], w["w_kv"], t["nb"], t["seq_len"], True,
                                                         [(0, kvw)] * N_GROUPS, "kv_proj")
                else:
                    t["kv_rows"] = _norm_mm(t["h"], w["kv_norm"], w["w_kv"], F32, name="kv_proj")
        ffn(lw["nw"][0], lw["ffn1"], layer=i)
        for t in st:
            nb, seq_len, n, h = t["nb"], t["seq_len"], t["n"], t["h"]
            if i < n_a:
                h, sh, state = _rwkv_layer(h, t["shift0"][i].astype(F32), t["wkv0"][i].astype(F32), seq_len,
                                           w["rwkv"][i])
                t["wkv_out"].append(state)
                t["shift_out"].append(sh)
            elif t["cache"] is None:
                al = w["attn"][i - n_a]
                q_rm = _proj_rm(h, lw["nw"][1], al["wq_seg"], nb, seq_len, False,
                                [(gi * D_MODEL, (gi + 1) * D_MODEL) for gi in range(N_GROUPS)], "q_proj")
                outs, lses = [], []
                for gi in range(N_GROUPS):
                    o, lse = _attn_group(q_rm[gi], t["kv_rm"][gi], _band_bias(w["rel_bias"], gi))
                    outs.append(o)
                    lses.append(lse)
                h = _attn_out(outs, lses, h, al["wo_seg"], w["et"], seq_len)
            else:
                al = w["attn"][i - n_a]
                q = _norm_mm(h, lw["nw"][1], al["wq_seg"], F32, name="q_proj_decode")
                q = q.reshape(n, N_GROUPS, 2, Q_PER_KV, 2, HEAD_DIM).transpose(0, 1, 2, 4, 3, 5)
                q = q.reshape(n, N_GROUPS, KV_HEADS, Q_PER_KV, 1, HEAD_DIM)
                q = q * jnp.eye(KV_HEADS, dtype=F32).reshape(1, 1, KV_HEADS, 1, KV_HEADS, 1)
                q = q.reshape(n, N_GROUPS * N_HEADS * KV_HEADS * HEAD_DIM)
                bias_c, bias_n = _decode_bias(w["rel_bias"], seq_len, t["cache"].shape[1])
                h = _attn_decode(q, t["cache"], t["kv_rows"], h, bias_c, bias_n, al["wo_r"], nb, seq_len)
            t["h"] = h
        pe = ([t["p"] for t in st], lw["nw"][3], lw["pe_gate"], lw["pe_proj"], w["final_norm"])
        ffn(lw["nw"][2], lw["ffn2"], pe=pe, layer=i, final=(i == depth - 1))
    results = []
    for t, sd in zip(st, streams):
        dt = sd["x"].dtype
        y = t["h"].reshape(t["nb"], t["seq_len"], D_MODEL).astype(dt)
        kv_rows = t["kv_rows"].reshape(t["nb"], t["seq_len"], kvw).astype(dt)
        results.append((y, jnp.stack(t["wkv_out"]).astype(dt), jnp.stack(t["shift_out"]).astype(dt), kv_rows))
    return results


def _kv_heads(kv):
    return kv.reshape(kv.shape[:2] + (2, KV_HEADS, HEAD_DIM))


def _trunk(x, p, wkv0, shift0, cache, w):
    y, wkv, shift, kv = _trunks([dict(x=x, p=p, wkv0=wkv0, shift0=shift0, cache=cache)], w)[0]
    return y, wkv, shift, _kv_heads(kv)


def kernel(x_prompt, x_sample, state_wkv, state_shift, cache_kv, p_prompt, p_sample, norm_w, ffn1_wi, ffn1_wo, ffn2_wi, ffn2_wo, pe_proj, pe_gate, rwkv_mix, rwkv_wrkv, rwkv_wo, rwkv_w0, rwkv_w1, rwkv_w2, rwkv_a0, rwkv_a1, rwkv_a2, rwkv_g1, rwkv_g2, rwkv_kk, rwkv_ka, rwkv_rk, rwkv_lnx_w, rwkv_lnx_b, attn_wq, attn_wo, kv_norm, w_kv, rel_bias, final_norm):
    w = _prep_weights(norm_w, ffn1_wi, ffn1_wo, ffn2_wi, ffn2_wo, pe_proj, pe_gate,
                      rwkv_mix, rwkv_wrkv, rwkv_wo, rwkv_w0, rwkv_w1, rwkv_w2, rwkv_a0, rwkv_a1, rwkv_a2,
                      rwkv_g1, rwkv_g2, rwkv_kk, rwkv_ka, rwkv_rk, rwkv_lnx_w, rwkv_lnx_b,
                      attn_wq, attn_wo, kv_norm, w_kv, rel_bias, final_norm)
    n_a = norm_w.shape[0] // 2
    nb, seq_len, _ = x_prompt.shape
    wkv0 = jnp.zeros((n_a, nb, N_HEADS, HEAD_DIM, HEAD_DIM), F32)
    shift0 = jnp.zeros((n_a, nb, D_MODEL), x_prompt.dtype)
    cache = cache_kv.reshape(cache_kv.shape[0], cache_kv.shape[1], 2 * KV_HEADS * HEAD_DIM).astype(F32)
    (y_p, wkv_p, shift_p, kv_p), (y_s, wkv_s, shift_s, kv_s) = _trunks(
        [dict(x=x_prompt, p=p_prompt, wkv0=wkv0, shift0=shift0, cache=None),
         dict(x=x_sample, p=p_sample, wkv0=state_wkv, shift0=state_shift, cache=cache)], w)
    kv_prompt = _kv_heads(kv_p[:, seq_len - min(MAX_WINDOW, seq_len):])
    return (y_p, y_s, wkv_p, shift_p, kv_prompt, wkv_s, shift_s, _kv_heads(kv_s))
```
